```python
import math
import jax, jax.numpy as jnp
from jax import lax
import numpy as np

D_MODEL = 2048
BATCH = 8
SEQ = 4096
DEPTH = 1

N_Q_HEADS = 32
N_KV_HEADS = 4
GROUP = N_Q_HEADS // N_KV_HEADS
HEAD_DIM = 64
ATTN_WIDTH = N_Q_HEADS * HEAD_DIM
KV_WIDTH = N_KV_HEADS * HEAD_DIM
WINDOW = 128
BLOCK = 128
NEG_INF = -1e30
N_BUCKETS = 32
MAX_DISTANCE = 128
LRU_WIDTH = D_MODEL
LRU_BLOCKS = 16
LRU_BLOCK_W = LRU_WIDTH // LRU_BLOCKS
CONV_WIDTH = 4
LRU_C = 8.0
D_FF = 4 * D_MODEL
EPS = 1e-6
IN_SPLITS = (LRU_WIDTH, LRU_WIDTH, ATTN_WIDTH, KV_WIDTH, KV_WIDTH, D_MODEL, D_MODEL)
IN_WIDTH = sum(IN_SPLITS)

kernel_name = "hybrid_rglru_swa_sink_sqrelu_adaln"


def _rms_norm(x, g):
    xf = x.astype(jnp.float32)
    y = xf * lax.rsqrt(jnp.mean(xf * xf, axis=-1, keepdims=True) + EPS)
    return (y * g.astype(jnp.float32)).astype(x.dtype)


def _modulate(h, shift, scale):
    return h * (1.0 + scale[:, None, :]) + shift[:, None, :]


def _causal_depthwise_conv(x, w, b):
    s = x.shape[1]
    xp = jnp.pad(x, ((0, 0), (CONV_WIDTH - 1, 0), (0, 0)))
    y = b
    for k in range(CONV_WIDTH):
        y = y + xp[:, k:k + s] * w[k]
    return y


def _rg_lru(x, wa, ba, wx, bx, lam):
    b_, s, w = x.shape
    xb = x.reshape(b_, s, LRU_BLOCKS, LRU_BLOCK_W)
    r = jax.nn.sigmoid(jnp.einsum("bshi,hij->bshj", xb, wa).reshape(b_, s, w) + ba)
    i = jax.nn.sigmoid(jnp.einsum("bshi,hij->bshj", xb, wx).reshape(b_, s, w) + bx)
    log_a = -LRU_C * r.astype(jnp.float32) * jax.nn.softplus(-lam.astype(jnp.float32))
    a = jnp.exp(log_a)
    mult = jnp.sqrt(-jnp.expm1(2.0 * log_a))
    mult = jnp.where(jnp.arange(s)[None, :, None] == 0, 1.0, mult)
    u = mult * (i * x).astype(jnp.float32)

    def step(h, au):
        a_t, u_t = au
        h = a_t * h + u_t
        return h, h

    h0 = jnp.zeros((b_, w), jnp.float32)
    _, hs = lax.scan(step, h0, (jnp.swapaxes(a, 0, 1), jnp.swapaxes(u, 0, 1)))
    return jnp.swapaxes(hs, 0, 1).astype(x.dtype)


def _t5_causal_bucket(rel):
    max_exact = N_BUCKETS // 2
    relf = jnp.maximum(rel, 1).astype(jnp.float32)
    large = max_exact + (jnp.log(relf / max_exact) / math.log(MAX_DISTANCE / max_exact)
                         * (N_BUCKETS - max_exact)).astype(jnp.int32)
    large = jnp.minimum(large, N_BUCKETS - 1)
    return jnp.where(rel < max_exact, rel, large)


def _band_bias_and_mask(rel_bias, n_blocks):
    qi = jnp.arange(BLOCK)[:, None]
    ki = jnp.arange(2 * BLOCK)[None, :]
    rel = qi + BLOCK - ki
    bucket = _t5_causal_bucket(jnp.maximum(rel, 0))
    bias = jnp.transpose(rel_bias[bucket], (2, 0, 1)).astype(jnp.float32)
    bias = bias.reshape(N_KV_HEADS, GROUP, BLOCK, 2 * BLOCK)
    kpos = jnp.arange(n_blocks)[:, None, None] * BLOCK - BLOCK + ki[None]
    valid = (kpos >= 0) & (rel[None] >= 0) & (rel[None] < WINDOW)
    return bias, valid


def _swa_sink_attention(q, k, v, sinks, rel_bias):
    b_, s, _ = q.shape
    n = s // BLOCK
    bias, valid = _band_bias_and_mask(rel_bias, n)
    q = q.reshape(b_, n, BLOCK, N_KV_HEADS, GROUP, HEAD_DIM)

    def band(t):
        t = t.reshape(b_, s, N_KV_HEADS, HEAD_DIM)
        tp = jnp.pad(t, ((0, 0), (BLOCK, 0), (0, 0), (0, 0)))
        tp = tp.reshape(b_, n + 1, BLOCK, N_KV_HEADS, HEAD_DIM)
        return jnp.concatenate([tp[:, :-1], tp[:, 1:]], axis=2)

    kb, vb = band(k), band(v)
    logits = jnp.einsum("bnqkgd,bnskd->bnkgqs", q, kb,
                        preferred_element_type=jnp.float32) * (HEAD_DIM ** -0.5)
    logits = jnp.where(valid[None, :, None, None], logits + bias[None, None], NEG_INF)
    sink = sinks.astype(jnp.float32).reshape(N_KV_HEADS, GROUP)[None, None, :, :, None, None]
    m = jnp.maximum(jnp.max(logits, axis=-1, keepdims=True), sink)
    e = jnp.exp(logits - m)
    p = e / (jnp.sum(e, axis=-1, keepdims=True) + jnp.exp(sink - m))
    o = jnp.einsum("bnkgqs,bnskd->bnqkgd", p.astype(vb.dtype), vb)
    return o.reshape(b_, s, ATTN_WIDTH)


def _fwd_setup_inputs(seed: int = 0) -> dict:
    key = jax.random.key(seed)
    ks = jax.random.split(key, 24)
    f32 = jnp.float32
    L, D = DEPTH, D_MODEL

    def nrm(k, shape, scale):
        return jax.random.normal(k, shape, f32) * scale

    a_c = jax.random.uniform(ks[12], (L, LRU_WIDTH), f32, 0.9, 0.999)
    a0 = a_c ** (1.0 / LRU_C)
    lam = jnp.log(a0) - jnp.log1p(-a0)
    return {
        "x": nrm(ks[0], (BATCH, SEQ, D), 1.0),
        "c": nrm(ks[1], (BATCH, D), 1.0),
        "w_ada": nrm(ks[2], (L, D, 6 * D), 0.5 * D ** -0.5),
        "b_ada": nrm(ks[3], (L, 6 * D), 0.02),
        "norm1_g": 1.0 + nrm(ks[4], (L, D), 0.02),
        "w_in": nrm(ks[5], (L, D, IN_WIDTH), D ** -0.5),
        "conv_w": nrm(ks[6], (L, CONV_WIDTH, LRU_WIDTH), CONV_WIDTH ** -0.5),
        "conv_b": nrm(ks[7], (L, LRU_WIDTH), 0.02),
        "lru_wa": nrm(ks[8], (L, LRU_BLOCKS, LRU_BLOCK_W, LRU_BLOCK_W), LRU_BLOCK_W ** -0.5),
        "lru_ba": nrm(ks[9], (L, LRU_WIDTH), 0.02),
        "lru_wx": nrm(ks[10], (L, LRU_BLOCKS, LRU_BLOCK_W, LRU_BLOCK_W), LRU_BLOCK_W ** -0.5),
        "lru_bx": nrm(ks[11], (L, LRU_WIDTH), 0.02),
        "lru_lambda": lam,
        "w_lru_out": nrm(ks[13], (L, LRU_WIDTH, D), LRU_WIDTH ** -0.5),
        "w_attn_out": nrm(ks[14], (L, ATTN_WIDTH, D), ATTN_WIDTH ** -0.5),
        "attn_sinks": nrm(ks[15], (L, N_Q_HEADS), 1.0),
        "rel_bias": nrm(ks[16], (N_BUCKETS, N_Q_HEADS), 0.5),
        "w_out": nrm(ks[17], (L, D, D), D ** -0.5),
        "norm2_g": 1.0 + nrm(ks[18], (L, D), 0.02),
        "w_ff1": nrm(ks[19], (L, D, D_FF), D ** -0.5),
        "w_ff2": nrm(ks[20], (L, D_FF, D), D_FF ** -0.5),
        "final_g": 1.0 + nrm(ks[21], (D,), 0.02),
    }


def _fwd_reference(x, c, w_ada, b_ada, norm1_g, w_in, conv_w, conv_b, lru_wa, lru_ba, lru_wx,
              lru_bx, lru_lambda, w_lru_out, w_attn_out, attn_sinks, rel_bias, w_out,
              norm2_g, w_ff1, w_ff2, final_g):
    split_idx = list(np.cumsum(IN_SPLITS)[:-1])
    c_act = jax.nn.silu(c)
    for l in range(DEPTH):
        mod = jnp.dot(c_act, w_ada[l]) + b_ada[l]
        shift1, scale1, gate1, shift2, scale2, gate2 = jnp.split(mod, 6, axis=-1)

        h = _modulate(_rms_norm(x, norm1_g[l]), shift1, scale1)
        proj = jnp.einsum("bsd,de->bse", h, w_in[l])
        lru_x, lru_gate, q, k, v, g_a, g_b = jnp.split(proj, split_idx, axis=-1)

        xc = _causal_depthwise_conv(lru_x, conv_w[l], conv_b[l])
        rec = _rg_lru(xc, lru_wa[l], lru_ba[l], lru_wx[l], lru_bx[l], lru_lambda[l])
        y_a = jnp.einsum("bsw,wd->bsd", rec * jax.nn.gelu(lru_gate, approximate=True),
                         w_lru_out[l])

        att = _swa_sink_attention(q, k, v, attn_sinks[l], rel_bias)
        y_b = jnp.einsum("bsw,wd->bsd", att, w_attn_out[l])

        merged = jax.nn.sigmoid(g_a) * y_a + jax.nn.sigmoid(g_b) * y_b
        x = x + gate1[:, None, :] * jnp.einsum("bsd,de->bse", merged, w_out[l])

        h2 = _modulate(_rms_norm(x, norm2_g[l]), shift2, scale2)
        ff = jnp.square(jax.nn.relu(jnp.einsum("bsd,df->bsf", h2, w_ff1[l])))
        x = x + gate2[:, None, :] * jnp.einsum("bsf,fd->bsd", ff, w_ff2[l])

    return _rms_norm(x, final_g)


import jax as _jax
import jax.numpy as _jnp

TWIN_FORMAT = 'train_step'
FWD_PARAMS = ['x', 'c', 'w_ada', 'b_ada', 'norm1_g', 'w_in', 'conv_w', 'conv_b', 'lru_wa', 'lru_ba', 'lru_wx', 'lru_bx', 'lru_lambda', 'w_lru_out', 'w_attn_out', 'attn_sinks', 'rel_bias', 'w_out', 'norm2_g', 'w_ff1', 'w_ff2', 'final_g']
TWIN_WEIGHTS = ['w_ada', 'b_ada', 'norm1_g', 'w_in', 'conv_w', 'conv_b', 'lru_wa', 'lru_ba', 'lru_wx', 'lru_bx', 'lru_lambda', 'w_lru_out', 'w_attn_out', 'attn_sinks', 'rel_bias', 'w_out', 'norm2_g', 'w_ff1', 'w_ff2', 'final_g']
TWIN_DIFF_INPUT = 'x'
TWIN_INPUTS = ['x', 'c', 'w_ada', 'b_ada', 'norm1_g', 'w_in', 'conv_w', 'conv_b', 'lru_wa', 'lru_ba', 'lru_wx', 'lru_bx', 'lru_lambda', 'w_lru_out', 'w_attn_out', 'attn_sinks', 'rel_bias', 'w_out', 'norm2_g', 'w_ff1', 'w_ff2', 'final_g', 'loss_target', 'm_w_ada', 'm_b_ada', 'm_norm1_g', 'm_w_in', 'm_conv_w', 'm_conv_b', 'm_lru_wa', 'm_lru_ba', 'm_lru_wx', 'm_lru_bx', 'm_lru_lambda', 'm_w_lru_out', 'm_w_attn_out', 'm_attn_sinks', 'm_rel_bias', 'm_w_out', 'm_norm2_g', 'm_w_ff1', 'm_w_ff2', 'm_final_g', 'v_w_ada', 'v_b_ada', 'v_norm1_g', 'v_w_in', 'v_conv_w', 'v_conv_b', 'v_lru_wa', 'v_lru_ba', 'v_lru_wx', 'v_lru_bx', 'v_lru_lambda', 'v_w_lru_out', 'v_w_attn_out', 'v_attn_sinks', 'v_rel_bias', 'v_w_out', 'v_norm2_g', 'v_w_ff1', 'v_w_ff2', 'v_final_g']
TWIN_OUTPUTS = ['loss', 'grad_x', 'grad_w_ada', 'grad_b_ada', 'grad_norm1_g', 'grad_w_in', 'grad_conv_w', 'grad_conv_b', 'grad_lru_wa', 'grad_lru_ba', 'grad_lru_wx', 'grad_lru_bx', 'grad_lru_lambda', 'grad_w_lru_out', 'grad_w_attn_out', 'grad_attn_sinks', 'grad_rel_bias', 'grad_w_out', 'grad_norm2_g', 'grad_w_ff1', 'grad_w_ff2', 'grad_final_g', 'delta_w_ada', 'delta_b_ada', 'delta_norm1_g', 'delta_w_in', 'delta_conv_w', 'delta_conv_b', 'delta_lru_wa', 'delta_lru_ba', 'delta_lru_wx', 'delta_lru_bx', 'delta_lru_lambda', 'delta_w_lru_out', 'delta_w_attn_out', 'delta_attn_sinks', 'delta_rel_bias', 'delta_w_out', 'delta_norm2_g', 'delta_w_ff1', 'delta_w_ff2', 'delta_final_g', 'new_m_w_ada', 'new_m_b_ada', 'new_m_norm1_g', 'new_m_w_in', 'new_m_conv_w', 'new_m_conv_b', 'new_m_lru_wa', 'new_m_lru_ba', 'new_m_lru_wx', 'new_m_lru_bx', 'new_m_lru_lambda', 'new_m_w_lru_out', 'new_m_w_attn_out', 'new_m_attn_sinks', 'new_m_rel_bias', 'new_m_w_out', 'new_m_norm2_g', 'new_m_w_ff1', 'new_m_w_ff2', 'new_m_final_g', 'new_v_w_ada', 'new_v_b_ada', 'new_v_norm1_g', 'new_v_w_in', 'new_v_conv_w', 'new_v_conv_b', 'new_v_lru_wa', 'new_v_lru_ba', 'new_v_lru_wx', 'new_v_lru_bx', 'new_v_lru_lambda', 'new_v_w_lru_out', 'new_v_w_attn_out', 'new_v_attn_sinks', 'new_v_rel_bias', 'new_v_w_out', 'new_v_norm2_g', 'new_v_w_ff1', 'new_v_w_ff2', 'new_v_final_g']
TWIN_LEAF_KINDS = {'loss': 'loss', 'grad_x': 'grad_x', 'grad_w_ada': 'grad_w', 'grad_b_ada': 'grad_w', 'grad_norm1_g': 'grad_w', 'grad_w_in': 'grad_w', 'grad_conv_w': 'grad_w', 'grad_conv_b': 'grad_w', 'grad_lru_wa': 'grad_w', 'grad_lru_ba': 'grad_w', 'grad_lru_wx': 'grad_w', 'grad_lru_bx': 'grad_w', 'grad_lru_lambda': 'grad_w', 'grad_w_lru_out': 'grad_w', 'grad_w_attn_out': 'grad_w', 'grad_attn_sinks': 'grad_w', 'grad_rel_bias': 'grad_w', 'grad_w_out': 'grad_w', 'grad_norm2_g': 'grad_w', 'grad_w_ff1': 'grad_w', 'grad_w_ff2': 'grad_w', 'grad_final_g': 'grad_w', 'delta_w_ada': 'delta_w', 'delta_b_ada': 'delta_w', 'delta_norm1_g': 'delta_w', 'delta_w_in': 'delta_w', 'delta_conv_w': 'delta_w', 'delta_conv_b': 'delta_w', 'delta_lru_wa': 'delta_w', 'delta_lru_ba': 'delta_w', 'delta_lru_wx': 'delta_w', 'delta_lru_bx': 'delta_w', 'delta_lru_lambda': 'delta_w', 'delta_w_lru_out': 'delta_w', 'delta_w_attn_out': 'delta_w', 'delta_attn_sinks': 'delta_w', 'delta_rel_bias': 'delta_w', 'delta_w_out': 'delta_w', 'delta_norm2_g': 'delta_w', 'delta_w_ff1': 'delta_w', 'delta_w_ff2': 'delta_w', 'delta_final_g': 'delta_w', 'new_m_w_ada': 'new_m', 'new_m_b_ada': 'new_m', 'new_m_norm1_g': 'new_m', 'new_m_w_in': 'new_m', 'new_m_conv_w': 'new_m', 'new_m_conv_b': 'new_m', 'new_m_lru_wa': 'new_m', 'new_m_lru_ba': 'new_m', 'new_m_lru_wx': 'new_m', 'new_m_lru_bx': 'new_m', 'new_m_lru_lambda': 'new_m', 'new_m_w_lru_out': 'new_m', 'new_m_w_attn_out': 'new_m', 'new_m_attn_sinks': 'new_m', 'new_m_rel_bias': 'new_m', 'new_m_w_out': 'new_m', 'new_m_norm2_g': 'new_m', 'new_m_w_ff1': 'new_m', 'new_m_w_ff2': 'new_m', 'new_m_final_g': 'new_m', 'new_v_w_ada': 'new_v', 'new_v_b_ada': 'new_v', 'new_v_norm1_g': 'new_v', 'new_v_w_in': 'new_v', 'new_v_conv_w': 'new_v', 'new_v_conv_b': 'new_v', 'new_v_lru_wa': 'new_v', 'new_v_lru_ba': 'new_v', 'new_v_lru_wx': 'new_v', 'new_v_lru_bx': 'new_v', 'new_v_lru_lambda': 'new_v', 'new_v_w_lru_out': 'new_v', 'new_v_w_attn_out': 'new_v', 'new_v_attn_sinks': 'new_v', 'new_v_rel_bias': 'new_v', 'new_v_w_out': 'new_v', 'new_v_norm2_g': 'new_v', 'new_v_w_ff1': 'new_v', 'new_v_w_ff2': 'new_v', 'new_v_final_g': 'new_v'}


def _forward(args):
    return _fwd_reference(*[args[k] for k in FWD_PARAMS])


def _output_shape():
    out = _jax.eval_shape(lambda: _forward(_fwd_setup_inputs(0)))
    return out.shape, out.dtype

N_MICROBATCH = 1
ADAM_LR = 0.001
ADAM_B1 = 0.9
ADAM_B2 = 0.999
ADAM_EPS = 1e-08
ADAM_WD = 0.01
ADAM_STEP = 10
PER_EXAMPLE_BATCH_AXIS = {'x': 0, 'c': 0, 'loss_target': 0}
SHARED_INPUTS = []
_WEIGHT_DTYPES = {'w_ada': _jnp.float32, 'b_ada': _jnp.float32, 'norm1_g': _jnp.float32, 'w_in': _jnp.float32, 'conv_w': _jnp.float32, 'conv_b': _jnp.float32, 'lru_wa': _jnp.float32, 'lru_ba': _jnp.float32, 'lru_wx': _jnp.float32, 'lru_bx': _jnp.float32, 'lru_lambda': _jnp.float32, 'w_lru_out': _jnp.float32, 'w_attn_out': _jnp.float32, 'attn_sinks': _jnp.float32, 'rel_bias': _jnp.float32, 'w_out': _jnp.float32, 'norm2_g': _jnp.float32, 'w_ff1': _jnp.float32, 'w_ff2': _jnp.float32, 'final_g': _jnp.float32}
MOMENT_SCALE = {'w_ada': 5.005782e-02, 'b_ada': 8.565353e-02, 'norm1_g': 2.966579e-02, 'w_in': 1.785396e-02, 'conv_w': 2.784917e-02, 'conv_b': 9.040988e-02, 'lru_wa': 3.057778e-03, 'lru_ba': 6.233723e-03, 'lru_wx': 6.095672e-03, 'lru_bx': 1.190346e-02, 'lru_lambda': 1.777784e-02, 'w_lru_out': 2.794451e-02, 'w_attn_out': 5.666419e-03, 'attn_sinks': 2.790739e-03, 'rel_bias': 4.338972e-03, 'w_out': 2.756098e-02, 'norm2_g': 4.036243e-02, 'w_ff1': 2.051199e-02, 'w_ff2': 3.729515e-02, 'final_g': 1.610468e+01}


def _to_microbatches(a, axis):
    t = _jnp.moveaxis(a, axis, 0)
    t = t.reshape((N_MICROBATCH, t.shape[0] // N_MICROBATCH) + t.shape[1:])
    return _jnp.moveaxis(t, 1, axis + 1)


def setup_inputs(seed: int = 0) -> dict:
    inp = _fwd_setup_inputs(seed)
    key = _jax.random.fold_in(_jax.random.key(seed), 7919)
    shape, _ = _output_shape()
    out = dict(inp)
    out["loss_target"] = _jax.random.normal(_jax.random.fold_in(key, 0), shape, _jnp.float32)
    for i, name in enumerate(TWIN_WEIGHTS):
        w = inp[name].astype(_jnp.float32)
        if MOMENT_SCALE is None:
            s = _jnp.sqrt(_jnp.mean(_jnp.square(w)) + 1e-30)
        else:
            s = MOMENT_SCALE[name]
        km, kv = _jax.random.split(_jax.random.fold_in(key, i + 1))
        out[name] = w
        out["m_" + name] = s * _jax.random.normal(km, w.shape, _jnp.float32)
        out["v_" + name] = (s * s) * _jax.random.uniform(kv, w.shape, _jnp.float32, 0.5, 1.5)
    if N_MICROBATCH > 1:
        for name, axis in PER_EXAMPLE_BATCH_AXIS.items():
            out[name] = _to_microbatches(out[name], axis)
    return {'x': out['x'], 'c': out['c'], 'w_ada': out['w_ada'], 'b_ada': out['b_ada'], 'norm1_g': out['norm1_g'], 'w_in': out['w_in'], 'conv_w': out['conv_w'], 'conv_b': out['conv_b'], 'lru_wa': out['lru_wa'], 'lru_ba': out['lru_ba'], 'lru_wx': out['lru_wx'], 'lru_bx': out['lru_bx'], 'lru_lambda': out['lru_lambda'], 'w_lru_out': out['w_lru_out'], 'w_attn_out': out['w_attn_out'], 'attn_sinks': out['attn_sinks'], 'rel_bias': out['rel_bias'], 'w_out': out['w_out'], 'norm2_g': out['norm2_g'], 'w_ff1': out['w_ff1'], 'w_ff2': out['w_ff2'], 'final_g': out['final_g'], 'loss_target': out['loss_target'], 'm_w_ada': out['m_w_ada'], 'm_b_ada': out['m_b_ada'], 'm_norm1_g': out['m_norm1_g'], 'm_w_in': out['m_w_in'], 'm_conv_w': out['m_conv_w'], 'm_conv_b': out['m_conv_b'], 'm_lru_wa': out['m_lru_wa'], 'm_lru_ba': out['m_lru_ba'], 'm_lru_wx': out['m_lru_wx'], 'm_lru_bx': out['m_lru_bx'], 'm_lru_lambda': out['m_lru_lambda'], 'm_w_lru_out': out['m_w_lru_out'], 'm_w_attn_out': out['m_w_attn_out'], 'm_attn_sinks': out['m_attn_sinks'], 'm_rel_bias': out['m_rel_bias'], 'm_w_out': out['m_w_out'], 'm_norm2_g': out['m_norm2_g'], 'm_w_ff1': out['m_w_ff1'], 'm_w_ff2': out['m_w_ff2'], 'm_final_g': out['m_final_g'], 'v_w_ada': out['v_w_ada'], 'v_b_ada': out['v_b_ada'], 'v_norm1_g': out['v_norm1_g'], 'v_w_in': out['v_w_in'], 'v_conv_w': out['v_conv_w'], 'v_conv_b': out['v_conv_b'], 'v_lru_wa': out['v_lru_wa'], 'v_lru_ba': out['v_lru_ba'], 'v_lru_wx': out['v_lru_wx'], 'v_lru_bx': out['v_lru_bx'], 'v_lru_lambda': out['v_lru_lambda'], 'v_w_lru_out': out['v_w_lru_out'], 'v_w_attn_out': out['v_w_attn_out'], 'v_attn_sinks': out['v_attn_sinks'], 'v_rel_bias': out['v_rel_bias'], 'v_w_out': out['v_w_out'], 'v_norm2_g': out['v_norm2_g'], 'v_w_ff1': out['v_w_ff1'], 'v_w_ff2': out['v_w_ff2'], 'v_final_g': out['v_final_g']}


def _loss(weights, diff, rest, loss_target):
    with _jax.named_scope("forward"):
        args = {**rest, TWIN_DIFF_INPUT: diff, **{k: w.astype(_WEIGHT_DTYPES[k]) for k, w in weights.items()}}
        y = _forward(args)
    with _jax.named_scope("loss_head"):
        err = _jnp.square(y.astype(_jnp.float32) - loss_target)
        return 0.5 * _jnp.sum(_jnp.mean(err, axis=-1)) if err.ndim else 0.5 * err


def _adamw(w, g, m, v):
    m = ADAM_B1 * m + (1.0 - ADAM_B1) * g
    v = ADAM_B2 * v + (1.0 - ADAM_B2) * _jnp.square(g)
    m_hat = m / (1.0 - ADAM_B1 ** ADAM_STEP)
    v_hat = v / (1.0 - ADAM_B2 ** ADAM_STEP)
    delta = -ADAM_LR * (m_hat / (_jnp.sqrt(v_hat) + ADAM_EPS) + ADAM_WD * w)
    return delta, m, v


def reference(x, c, w_ada, b_ada, norm1_g, w_in, conv_w, conv_b, lru_wa, lru_ba, lru_wx, lru_bx, lru_lambda, w_lru_out, w_attn_out, attn_sinks, rel_bias, w_out, norm2_g, w_ff1, w_ff2, final_g, loss_target, m_w_ada, m_b_ada, m_norm1_g, m_w_in, m_conv_w, m_conv_b, m_lru_wa, m_lru_ba, m_lru_wx, m_lru_bx, m_lru_lambda, m_w_lru_out, m_w_attn_out, m_attn_sinks, m_rel_bias, m_w_out, m_norm2_g, m_w_ff1, m_w_ff2, m_final_g, v_w_ada, v_b_ada, v_norm1_g, v_w_in, v_conv_w, v_conv_b, v_lru_wa, v_lru_ba, v_lru_wx, v_lru_bx, v_lru_lambda, v_w_lru_out, v_w_attn_out, v_attn_sinks, v_rel_bias, v_w_out, v_norm2_g, v_w_ff1, v_w_ff2, v_final_g):
    given = dict(x=x, c=c, w_ada=w_ada, b_ada=b_ada, norm1_g=norm1_g, w_in=w_in, conv_w=conv_w, conv_b=conv_b, lru_wa=lru_wa, lru_ba=lru_ba, lru_wx=lru_wx, lru_bx=lru_bx, lru_lambda=lru_lambda, w_lru_out=w_lru_out, w_attn_out=w_attn_out, attn_sinks=attn_sinks, rel_bias=rel_bias, w_out=w_out, norm2_g=norm2_g, w_ff1=w_ff1, w_ff2=w_ff2, final_g=final_g, loss_target=loss_target, m_w_ada=m_w_ada, m_b_ada=m_b_ada, m_norm1_g=m_norm1_g, m_w_in=m_w_in, m_conv_w=m_conv_w, m_conv_b=m_conv_b, m_lru_wa=m_lru_wa, m_lru_ba=m_lru_ba, m_lru_wx=m_lru_wx, m_lru_bx=m_lru_bx, m_lru_lambda=m_lru_lambda, m_w_lru_out=m_w_lru_out, m_w_attn_out=m_w_attn_out, m_attn_sinks=m_attn_sinks, m_rel_bias=m_rel_bias, m_w_out=m_w_out, m_norm2_g=m_norm2_g, m_w_ff1=m_w_ff1, m_w_ff2=m_w_ff2, m_final_g=m_final_g, v_w_ada=v_w_ada, v_b_ada=v_b_ada, v_norm1_g=v_norm1_g, v_w_in=v_w_in, v_conv_w=v_conv_w, v_conv_b=v_conv_b, v_lru_wa=v_lru_wa, v_lru_ba=v_lru_ba, v_lru_wx=v_lru_wx, v_lru_bx=v_lru_bx, v_lru_lambda=v_lru_lambda, v_w_lru_out=v_w_lru_out, v_w_attn_out=v_w_attn_out, v_attn_sinks=v_attn_sinks, v_rel_bias=v_rel_bias, v_w_out=v_w_out, v_norm2_g=v_norm2_g, v_w_ff1=v_w_ff1, v_w_ff2=v_w_ff2, v_final_g=v_final_g)
    weights = {n: given[n] for n in TWIN_WEIGHTS}
    shared = {n: given[n] for n in SHARED_INPUTS}
    per_example = {n: given[n] for n in ['x', 'c']}
    grad_fn = _jax.value_and_grad(_loss, argnums=(0, 1))

    def one_microbatch(ex, loss_target):
        ex = dict(ex)
        diff = ex.pop(TWIN_DIFF_INPUT)
        return grad_fn(weights, diff, {**shared, **ex}, loss_target)

    if N_MICROBATCH == 1:
        loss, (grad_w, grad_x) = one_microbatch(per_example, given["loss_target"])
    else:
        def body(carry, xs):
            loss_sum, grad_sum = carry
            l_k, (gw_k, gx_k) = one_microbatch(xs[0], xs[1])
            with _jax.named_scope("update"):
                return (loss_sum + l_k, _jax.tree.map(_jnp.add, grad_sum, gw_k)), gx_k

        init = (_jnp.zeros((), _jnp.float32), _jax.tree.map(_jnp.zeros_like, weights))
        (loss, grad_w), grad_x = _jax.lax.scan(body, init, (per_example, given["loss_target"]))
    with _jax.named_scope("update"):
        delta_w, new_m, new_v = {}, {}, {}
        for n in TWIN_WEIGHTS:
            delta_w[n], new_m[n], new_v[n] = _adamw(weights[n], grad_w[n], given["m_" + n], given["v_" + n])
    return (loss, grad_x, *[grad_w[n] for n in TWIN_WEIGHTS], *[delta_w[n] for n in TWIN_WEIGHTS],
            *[new_m[n] for n in TWIN_WEIGHTS], *[new_v[n] for n in TWIN_WEIGHTS])
```

```python
import math

import numpy as np
import jax
import jax.numpy as jnp
from jax import lax
from jax.experimental import pallas as pl
from jax.experimental.pallas import tpu as pltpu

F32 = jnp.float32
BF16 = jnp.bfloat16
SDS = jax.ShapeDtypeStruct
MESH = pl.DeviceIdType.MESH

D = 2048
D_FF = 4 * D
N_HEADS = 32
HEAD_DIM = 64
BLOCK = 128
N_LRU_BLOCKS = 16
LRU_C = 8.0
EPS = 1e-6
NEG_INF = -1e30
N_BUCKETS = 32
MAX_DISTANCE = 128
IN_W = 10752
IN_SHARD = IN_W // 4
IN_TILE = 896
ADA_SHARD = 6 * D // 4
OFF_LRU, OFF_GATE, OFF_Q, OFF_K, OFF_V, OFF_GA, OFF_GB = 0, 2048, 4096, 6144, 6400, 6656, 8704
SCALE = HEAD_DIM ** -0.5
N_CHIPS = 4
N_DEV = 8

ADAM_LR, ADAM_B1, ADAM_B2, ADAM_EPS, ADAM_WD, ADAM_STEP = 0.001, 0.9, 0.999, 1e-08, 0.01, 10
ADAM_C1 = 1.0 - ADAM_B1 ** ADAM_STEP
ADAM_C2 = 1.0 - ADAM_B2 ** ADAM_STEP

VMEM_LIMIT = 52 * 2 ** 20
SUB = 128
V_G1, V_SCALE1, V_SHIFT1, V_GATE1, V_G2, V_SCALE2, V_SHIFT2, V_GATE2, V_G3 = range(9)
L_CW0, L_CB, L_BA, L_BX, L_LAM = 0, 4, 5, 6, 7
P_VEC, P_CONVW, P_MOD, P_ATT, P_WA, P_WX, P_ROWS = 0, 8, 16, 24, 32, 160, 288


def _cp(n_axes):
    return pltpu.CompilerParams(dimension_semantics=("arbitrary",) * n_axes, vmem_limit_bytes=VMEM_LIMIT)


def _dot(a, b):
    return jnp.dot(a, b, preferred_element_type=F32)


def _dot_nt(a, b):
    return lax.dot_general(a, b, (((1,), (1,)), ((), ())), preferred_element_type=F32)


def _dot_tn(a, b):
    return lax.dot_general(a, b, (((0,), (0,)), ((), ())), preferred_element_type=F32)


_G0 = math.sqrt(2.0 / math.pi)
_G1 = 0.044715


def _gelu(x):
    return 0.5 * x * (1.0 + jnp.tanh(_G0 * (x + _G1 * x * x * x)))


def _gelu_grad(x):
    x2 = x * x
    t = jnp.tanh(_G0 * (x + _G1 * x * x2))
    return 0.5 * (1.0 + t) + 0.5 * x * (1.0 - t * t) * _G0 * (1.0 + 3.0 * _G1 * x2)


def _expm1(x):
    u = jnp.exp(x)
    um1 = u - 1.0
    k = um1 * x / jnp.log(u)
    return jnp.where(um1 == 0.0, x, jnp.where(u < 0.5, um1, k))


def _softplus(z):
    e = jnp.exp(-jnp.abs(z))
    u = 1.0 + e
    l1p = jnp.where(u == 1.0, e, jnp.log(u) * e / (u - 1.0))
    return jnp.maximum(z, 0.0) + l1p


def _adamw_math(w, g, m, v):
    m2 = ADAM_B1 * m + (1.0 - ADAM_B1) * g
    v2 = ADAM_B2 * v + (1.0 - ADAM_B2) * (g * g)
    m_hat = m2 / ADAM_C1
    v_hat = v2 / ADAM_C2
    delta = -ADAM_LR * (m_hat / (jnp.sqrt(v_hat) + ADAM_EPS) + ADAM_WD * w)
    return delta, m2, v2


def _rms_parts(xv):
    r = lax.rsqrt(jnp.mean(xv * xv, axis=-1, keepdims=True) + EPS)
    return r, xv * r


def inproj_fwd(x, vecs, w_in):
    s = x.shape[0]
    tm = min(512, s)
    per = IN_SHARD // IN_TILE

    def body(x_ref, v_ref, w_ref, proj_ref, h_ref):
        @pl.when(pl.program_id(1) == 0)
        def _():
            _, xh = _rms_parts(x_ref[...])
            h = (xh * v_ref[V_G1:V_G1 + 1, :]) * (1.0 + v_ref[V_SCALE1:V_SCALE1 + 1, :]) + v_ref[V_SHIFT1:V_SHIFT1 + 1, :]
            h_ref[...] = h.astype(BF16)
        proj_ref[...] = _dot(h_ref[...], w_ref[...]).astype(BF16)

    return pl.pallas_call(
        body, name="inproj_fwd", grid=(s // tm, IN_W // IN_TILE),
        in_specs=[pl.BlockSpec((tm, D), lambda i, j: (i, 0)),
                  pl.BlockSpec((16, D), lambda i, j: (0, 0)),
                  pl.BlockSpec((None, D, IN_TILE), lambda i, j: (j // per, 0, j % per))],
        out_specs=[pl.BlockSpec((tm, IN_TILE), lambda i, j: (i, j)),
                   pl.BlockSpec((tm, D), lambda i, j: (i, 0))],
        out_shape=[SDS((s, IN_W), BF16), SDS((s, D), BF16)],
        compiler_params=_cp(2))(x, vecs, w_in)


def _lru_block_fwd(xbuf, lv_ref, wa_ref, wx_ref, b, t, first):
    cs = slice(b * 128, (b + 1) * 128)
    x0 = xbuf[pl.ds(8, t), cs]
    x1 = xbuf[pl.ds(7, t), cs]
    x2 = xbuf[pl.ds(6, t), cs]
    x3 = xbuf[pl.ds(5, t), cs]
    xc = (lv_ref[L_CB:L_CB + 1, cs] + lv_ref[3:4, cs] * x0 + lv_ref[2:3, cs] * x1
          + lv_ref[1:2, cs] * x2 + lv_ref[0:1, cs] * x3)
    xcb = xc.astype(BF16)
    r = jax.nn.sigmoid(_dot(xcb, wa_ref[b]) + lv_ref[L_BA:L_BA + 1, cs])
    ig = jax.nn.sigmoid(_dot(xcb, wx_ref[b]) + lv_ref[L_BX:L_BX + 1, cs])
    sp = _softplus(-lv_ref[L_LAM:L_LAM + 1, cs])
    log_a = (-LRU_C) * r * sp
    a = jnp.exp(log_a)
    mult = jnp.where(first, 1.0, jnp.sqrt(-_expm1(2.0 * log_a)))
    return (x0, x1, x2, x3), xc, xcb, r, ig, sp, a, mult


def lru_fwd(proj, lvec, wa, wx):
    s = proj.shape[0]
    t = min(256, s)

    def body(lx_ref, gate_ref, lv_ref, wa_ref, wx_ref, ya_ref, rec_ref, xbuf, a_s, u_s, hc):
        i = pl.program_id(0)

        @pl.when(i == 0)
        def _():
            xbuf[pl.ds(0, 8), :] = jnp.zeros((8, D), F32)
            hc[...] = jnp.zeros((8, D), F32)

        @pl.when(i > 0)
        def _():
            xbuf[pl.ds(0, 8), :] = xbuf[pl.ds(t, 8), :]

        xbuf[pl.ds(8, t), :] = lx_ref[...].astype(F32)
        first = (lax.broadcasted_iota(jnp.int32, (t, 128), 0) + i * t) == 0
        for b in range(N_LRU_BLOCKS):
            cs = slice(b * 128, (b + 1) * 128)
            _, xc, _, _, ig, _, a, mult = _lru_block_fwd(xbuf, lv_ref, wa_ref, wx_ref, b, t, first)
            a_s[:, cs] = a
            u_s[:, cs] = mult * (ig * xc)

        def step(tt, h):
            h = a_s[pl.ds(tt, 1), :] * h + u_s[pl.ds(tt, 1), :]
            rec_ref[pl.ds(tt, 1), :] = h
            return h

        hc[0:1, :] = lax.fori_loop(0, t, step, hc[0:1, :], unroll=8)
        for b in range(N_LRU_BLOCKS):
            cs = slice(b * 128, (b + 1) * 128)
            ya_ref[:, cs] = (rec_ref[:, cs] * _gelu(gate_ref[:, cs].astype(F32))).astype(BF16)

    return pl.pallas_call(
        body, name="lru_fwd", grid=(s // t,),
        in_specs=[pl.BlockSpec((t, D), lambda i: (i, OFF_LRU // D)),
                  pl.BlockSpec((t, D), lambda i: (i, OFF_GATE // D)),
                  pl.BlockSpec((8, D), lambda i: (0, 0)),
                  pl.BlockSpec((N_LRU_BLOCKS, 128, 128), lambda i: (0, 0, 0)),
                  pl.BlockSpec((N_LRU_BLOCKS, 128, 128), lambda i: (0, 0, 0))],
        out_specs=[pl.BlockSpec((t, D), lambda i: (i, 0)), pl.BlockSpec((t, D), lambda i: (i, 0))],
        out_shape=[SDS((s, D), BF16), SDS((s, D), F32)],
        scratch_shapes=[pltpu.VMEM((t + 8, D), F32), pltpu.VMEM((t, D), F32), pltpu.VMEM((t, D), F32),
                        pltpu.VMEM((8, D), F32)],
        compiler_params=_cp(1))(proj, proj, lvec, wa, wx)


def t5_bucket_table():
    qi = np.arange(BLOCK)[:, None]
    ki = np.arange(2 * BLOCK)[None, :]
    rel = qi + BLOCK - ki
    relc = np.maximum(rel, 0)
    max_exact = N_BUCKETS // 2
    relf = np.maximum(relc, 1).astype(np.float32)
    large = max_exact + (np.log(relf / np.float32(max_exact)) / np.float32(math.log(MAX_DISTANCE / max_exact))
                         * np.float32(N_BUCKETS - max_exact)).astype(np.int32)
    large = np.minimum(large, N_BUCKETS - 1)
    bucket = np.where(relc < max_exact, relc, large)
    bucket = np.where((rel >= 0) & (rel < BLOCK), bucket, -1)
    return jnp.asarray(bucket.reshape(1, BLOCK * 2 * BLOCK), jnp.int32)


def bias_band(rel_bias_t, buckets):
    n = BLOCK * 2 * BLOCK
    tn = 4096

    def body(bk_ref, rb_ref, o_ref):
        row = lax.broadcasted_iota(jnp.int32, (N_BUCKETS, tn), 0)
        oh = jnp.where(row == bk_ref[...], 1.0, 0.0).astype(BF16)
        rb = rb_ref[...]
        p0 = rb.astype(BF16)
        r1 = rb - p0.astype(F32)
        p1 = r1.astype(BF16)
        p2 = (r1 - p1.astype(F32)).astype(BF16)
        o_ref[...] = _dot(p0, oh) + _dot(p1, oh) + _dot(p2, oh)

    return pl.pallas_call(
        body, name="bias_band", grid=(n // tn,),
        in_specs=[pl.BlockSpec((1, tn), lambda i: (0, i)), pl.BlockSpec((N_HEADS, N_BUCKETS), lambda i: (0, 0))],
        out_specs=pl.BlockSpec((N_HEADS, tn), lambda i: (0, i)),
        out_shape=SDS((N_HEADS, n), F32), compiler_params=_cp(1))(buckets, rel_bias_t)


def bias_band_bwd(dband, buckets):
    n = BLOCK * 2 * BLOCK
    tn = 4096

    def body(bk_ref, d_ref, o_ref):
        @pl.when(pl.program_id(0) == 0)
        def _():
            o_ref[...] = jnp.zeros_like(o_ref)
        row = lax.broadcasted_iota(jnp.int32, (N_BUCKETS, tn), 0)
        oh = jnp.where(row == bk_ref[...], 1.0, 0.0).astype(BF16)
        dv = d_ref[...]
        p0 = dv.astype(BF16)
        r1 = dv - p0.astype(F32)
        p1 = r1.astype(BF16)
        p2 = (r1 - p1.astype(F32)).astype(BF16)
        o_ref[...] += _dot_nt(oh, p0) + _dot_nt(oh, p1) + _dot_nt(oh, p2)

    return pl.pallas_call(
        body, name="bias_band_bwd", grid=(n // tn,),
        in_specs=[pl.BlockSpec((1, tn), lambda i: (0, i)), pl.BlockSpec((N_HEADS, tn), lambda i: (0, i))],
        out_specs=pl.BlockSpec((N_BUCKETS, N_HEADS), lambda i: (0, 0)),
        out_shape=SDS((N_BUCKETS, N_HEADS), F32), compiler_params=_cp(1))(buckets, dband)


def _dup_half(band, which):
    lane = lax.broadcasted_iota(jnp.int32, band.shape, 1)
    rolled = pltpu.roll(band, 64, 1)
    keep = (lane < 64) if which == 0 else (lane >= 64)
    return jnp.where(keep, band, rolled)


def _attn_probs(qm, kk, bias, sink, valid):
    sc = _dot_nt(qm, kk) * SCALE + bias
    sc = jnp.where(valid, sc, NEG_INF)
    m = jnp.maximum(jnp.max(sc, axis=-1, keepdims=True), sink)
    e = jnp.exp(sc - m)
    es = jnp.exp(sink - m)
    inv = 1.0 / (jnp.sum(e, axis=-1, keepdims=True) + es)
    return e * inv, es * inv


def _band_valid(n):
    qi = lax.broadcasted_iota(jnp.int32, (BLOCK, 2 * BLOCK), 0)
    ki = lax.broadcasted_iota(jnp.int32, (BLOCK, 2 * BLOCK), 1)
    rel = qi + BLOCK - ki
    return (rel >= 0) & (rel < BLOCK) & ((ki >= BLOCK) | (n > 0))


def _kv_bands(prev_ref, cur_ref):
    band = jnp.concatenate([prev_ref[...].astype(F32), cur_ref[...].astype(F32)], axis=0)
    return [_dup_half(band, 0).astype(BF16), _dup_half(band, 1).astype(BF16)]


def attn_fwd(proj, band, sinks):
    s = proj.shape[0]
    nb = s // BLOCK
    qw = 1024

    def body(sk_ref, q_ref, kp_ref, kc_ref, vp_ref, vc_ref, b_ref, o_ref):
        n = pl.program_id(0)
        gp = pl.program_id(1)
        valid = _band_valid(n)
        kks = _kv_bands(kp_ref, kc_ref)
        vvs = _kv_bands(vp_ref, vc_ref)
        lane = lax.broadcasted_iota(jnp.int32, (BLOCK, 128), 1)
        for j in range(8):
            qs = q_ref[:, j * 128:(j + 1) * 128]
            outs = []
            for hh in range(2):
                hl = 2 * j + hh
                qm = jnp.where((lane < 64) if hh == 0 else (lane >= 64), qs, jnp.zeros_like(qs))
                p, _ = _attn_probs(qm, kks[j // 4], b_ref[hl], sk_ref[gp * 16 + hl], valid)
                outs.append(_dot(p.astype(BF16), vvs[j // 4]))
            o_ref[:, j * 128:(j + 1) * 128] = jnp.where(lane < 64, outs[0], outs[1]).astype(BF16)

    kb, vb = OFF_K // 128, OFF_V // 128
    return pl.pallas_call(
        body, name="attn_fwd", grid=(nb, 2),
        in_specs=[pl.BlockSpec(memory_space=pltpu.SMEM),
                  pl.BlockSpec((BLOCK, qw), lambda n, g: (n, OFF_Q // qw + g)),
                  pl.BlockSpec((BLOCK, 128), lambda n, g: (jnp.maximum(n - 1, 0), kb + g)),
                  pl.BlockSpec((BLOCK, 128), lambda n, g: (n, kb + g)),
                  pl.BlockSpec((BLOCK, 128), lambda n, g: (jnp.maximum(n - 1, 0), vb + g)),
                  pl.BlockSpec((BLOCK, 128), lambda n, g: (n, vb + g)),
                  pl.BlockSpec((16, BLOCK, 2 * BLOCK), lambda n, g: (g, 0, 0))],
        out_specs=pl.BlockSpec((BLOCK, qw), lambda n, g: (n, g)),
        out_shape=SDS((s, D), BF16), compiler_params=_cp(2))(sinks, proj, proj, proj, proj, proj, band)


def merge_fwd(ya, att, w_lru_out, w_attn_out, proj):
    s = ya.shape[0]
    tm, tn = min(512, s), 512

    def body(ya_ref, at_ref, wl_ref, wt_ref, ga_ref, gb_ref, yab_ref, mg_ref):
        y_a = _dot(ya_ref[...], wl_ref[...])
        y_b = _dot(at_ref[...], wt_ref[...])
        yab_ref[0] = y_a.astype(BF16)
        yab_ref[1] = y_b.astype(BF16)
        mg_ref[...] = (jax.nn.sigmoid(ga_ref[...].astype(F32)) * y_a
                       + jax.nn.sigmoid(gb_ref[...].astype(F32)) * y_b).astype(BF16)

    return pl.pallas_call(
        body, name="merge_fwd", grid=(s // tm, D // tn),
        in_specs=[pl.BlockSpec((tm, D), lambda i, j: (i, 0)), pl.BlockSpec((tm, D), lambda i, j: (i, 0)),
                  pl.BlockSpec((D, tn), lambda i, j: (0, j)), pl.BlockSpec((D, tn), lambda i, j: (0, j)),
                  pl.BlockSpec((tm, tn), lambda i, j: (i, OFF_GA // tn + j)),
                  pl.BlockSpec((tm, tn), lambda i, j: (i, OFF_GB // tn + j))],
        out_specs=[pl.BlockSpec((2, tm, tn), lambda i, j: (0, i, j)), pl.BlockSpec((tm, tn), lambda i, j: (i, j))],
        out_shape=[SDS((2, s, D), BF16), SDS((s, D), BF16)],
        compiler_params=_cp(2))(ya, att, w_lru_out, w_attn_out, proj, proj)


def outproj_fwd(merged, w_out, x, vecs):
    s = x.shape[0]
    tm, tn = min(512, s), 512

    def body(m_ref, w_ref, x_ref, v_ref, x1_ref, o1_ref):
        o1 = _dot(m_ref[...], w_ref[...])
        o1_ref[...] = o1.astype(BF16)
        x1_ref[...] = x_ref[...] + v_ref[V_GATE1:V_GATE1 + 1, :] * o1

    return pl.pallas_call(
        body, name="outproj_fwd", grid=(s // tm, D // tn),
        in_specs=[pl.BlockSpec((tm, D), lambda i, j: (i, 0)), pl.BlockSpec((D, tn), lambda i, j: (0, j)),
                  pl.BlockSpec((tm, tn), lambda i, j: (i, j)), pl.BlockSpec((16, tn), lambda i, j: (0, j))],
        out_specs=[pl.BlockSpec((tm, tn), lambda i, j: (i, j)), pl.BlockSpec((tm, tn), lambda i, j: (i, j))],
        out_shape=[SDS((s, D), F32), SDS((s, D), BF16)],
        compiler_params=_cp(2))(merged, w_out, x, vecs)


def ff1_fwd(x1, vecs, w_ff1):
    s = x1.shape[0]
    tm, tn = min(512, s), 512
    per = D // tn

    def body(x_ref, v_ref, w_ref, f_ref, h_ref):
        @pl.when(pl.program_id(1) == 0)
        def _():
            _, xh = _rms_parts(x_ref[...])
            h = (xh * v_ref[V_G2:V_G2 + 1, :]) * (1.0 + v_ref[V_SCALE2:V_SCALE2 + 1, :]) + v_ref[V_SHIFT2:V_SHIFT2 + 1, :]
            h_ref[...] = h.astype(BF16)
        f_ref[...] = _dot(h_ref[...], w_ref[...]).astype(BF16)

    return pl.pallas_call(
        body, name="ff1_fwd", grid=(s // tm, D_FF // tn),
        in_specs=[pl.BlockSpec((tm, D), lambda i, j: (i, 0)), pl.BlockSpec((16, D), lambda i, j: (0, 0)),
                  pl.BlockSpec((None, D, tn), lambda i, j: (j // per, 0, j % per))],
        out_specs=[pl.BlockSpec((tm, tn), lambda i, j: (i, j)), pl.BlockSpec((tm, D), lambda i, j: (i, 0))],
        out_shape=[SDS((s, D_FF), BF16), SDS((s, D), BF16)],
        compiler_params=_cp(2))(x1, vecs, w_ff1)


def ff2_loss(f, w_ff2, x1, tgt, vecs):
    s = x1.shape[0]
    tm, tk = min(512, s), 512
    nk = D_FF // tk

    def body(f_ref, w_ref, x1_ref, t_ref, v_ref, dx2_ref, do2_ref, sums_ref, loss_ref, acc):
        i, k = pl.program_id(0), pl.program_id(1)

        @pl.when((i == 0) & (k == 0))
        def _():
            sums_ref[...] = jnp.zeros_like(sums_ref)
            loss_ref[...] = jnp.zeros_like(loss_ref)

        @pl.when(k == 0)
        def _():
            acc[...] = jnp.zeros_like(acc)

        fv = jnp.maximum(f_ref[...].astype(F32), 0.0)
        acc[...] += _dot((fv * fv).astype(BF16), w_ref[...])

        @pl.when(k == nk - 1)
        def _():
            gate2 = v_ref[V_GATE2:V_GATE2 + 1, :]
            g3 = v_ref[V_G3:V_G3 + 1, :]

            def sub(rb, carry):
                rs = pl.ds(pl.multiple_of(rb * SUB, SUB), SUB)
                o2 = acc[rs, :]
                x2 = x1_ref[rs, :] + gate2 * o2
                r3, xh = _rms_parts(x2)
                e = xh * g3 - t_ref[rs, :]
                loss_ref[...] += (0.5 / D) * jnp.sum(e * e)
                dy = e * (1.0 / D)
                sums_ref[0:1, :] += jnp.sum(dy * xh, axis=0, keepdims=True)
                dxh = dy * g3
                dx2 = r3 * (dxh - xh * jnp.mean(dxh * xh, axis=-1, keepdims=True))
                sums_ref[1:2, :] += jnp.sum(dx2 * o2, axis=0, keepdims=True)
                dx2_ref[rs, :] = dx2
                do2_ref[rs, :] = (dx2 * gate2).astype(BF16)
                return carry

            lax.fori_loop(0, tm // SUB, sub, 0)

    return pl.pallas_call(
        body, name="ff2_loss", grid=(s // tm, nk),
        in_specs=[pl.BlockSpec((tm, tk), lambda i, k: (i, k)), pl.BlockSpec((tk, D), lambda i, k: (k, 0)),
                  pl.BlockSpec((tm, D), lambda i, k: (i, 0)), pl.BlockSpec((tm, D), lambda i, k: (i, 0)),
                  pl.BlockSpec((16, D), lambda i, k: (0, 0))],
        out_specs=[pl.BlockSpec((tm, D), lambda i, k: (i, 0)), pl.BlockSpec((tm, D), lambda i, k: (i, 0)),
                   pl.BlockSpec((8, D), lambda i, k: (0, 0)), pl.BlockSpec((8, 128), lambda i, k: (0, 0))],
        out_shape=[SDS((s, D), F32), SDS((s, D), BF16), SDS((8, D), F32), SDS((8, 128), F32)],
        scratch_shapes=[pltpu.VMEM((tm, D), F32)],
        compiler_params=_cp(2))(f, w_ff2, x1, tgt, vecs)


def ff2_bwd(do2, w_ff2, f):
    s = do2.shape[0]
    tm, tn = min(512, s), 512

    def body(d_ref, w_ref, f_ref, o_ref):
        dff = _dot_nt(d_ref[...], w_ref[...])
        o_ref[...] = (dff * (2.0 * jnp.maximum(f_ref[...].astype(F32), 0.0))).astype(BF16)

    return pl.pallas_call(
        body, name="ff2_bwd", grid=(s // tm, D_FF // tn),
        in_specs=[pl.BlockSpec((tm, D), lambda i, j: (i, 0)), pl.BlockSpec((tn, D), lambda i, j: (j, 0)),
                  pl.BlockSpec((tm, tn), lambda i, j: (i, j))],
        out_specs=pl.BlockSpec((tm, tn), lambda i, j: (i, j)),
        out_shape=SDS((s, D_FF), BF16), compiler_params=_cp(2))(do2, w_ff2, f)


def weight_grad(name, a, b, tn, out_shape, out_block, out_map, relu2=False):
    s, m = a.shape
    n = b.shape[1]
    tm, tk = 512, min(512, s)
    nk = s // tk

    def body(a_ref, b_ref, o_ref, acc):
        k = pl.program_id(2)

        @pl.when(k == 0)
        def _():
            acc[...] = jnp.zeros_like(acc)

        av = a_ref[...]
        if relu2:
            fv = jnp.maximum(av.astype(F32), 0.0)
            av = (fv * fv).astype(BF16)
        acc[...] += _dot_tn(av, b_ref[...])

        @pl.when(k == nk - 1)
        def _():
            o_ref[...] = acc[...]

    return pl.pallas_call(
        body, name=name, grid=(m // tm, n // tn, nk),
        in_specs=[pl.BlockSpec((tk, tm), lambda i, j, k: (k, i)), pl.BlockSpec((tk, tn), lambda i, j, k: (k, j))],
        out_specs=pl.BlockSpec(out_block, lambda i, j, k: out_map(i, j)),
        out_shape=SDS(out_shape, F32), scratch_shapes=[pltpu.VMEM((tm, tn), F32)],
        compiler_params=_cp(3))(a, b)


def ff1_bwd(df, w_ff1, x1, dx2, o1, vecs):
    s = df.shape[0]
    tm, tk = min(512, s), 512
    nk = D_FF // tk
    per = D // tk

    def body(d_ref, w_ref, x1_ref, dx2_ref, o1_ref, v_ref, dx1_ref, do1_ref, sums_ref, acc):
        i, k = pl.program_id(0), pl.program_id(1)

        @pl.when((i == 0) & (k == 0))
        def _():
            sums_ref[...] = jnp.zeros_like(sums_ref)

        @pl.when(k == 0)
        def _():
            acc[...] = jnp.zeros_like(acc)

        acc[...] += _dot_nt(d_ref[...], w_ref[...])

        @pl.when(k == nk - 1)
        def _():
            g2 = v_ref[V_G2:V_G2 + 1, :]
            scale2 = v_ref[V_SCALE2:V_SCALE2 + 1, :]
            gate1 = v_ref[V_GATE1:V_GATE1 + 1, :]

            def sub(rb, carry):
                rs = pl.ds(pl.multiple_of(rb * SUB, SUB), SUB)
                dh = acc[rs, :]
                r2, xh = _rms_parts(x1_ref[rs, :])
                sums_ref[0:1, :] += jnp.sum(dh, axis=0, keepdims=True)
                sums_ref[1:2, :] += jnp.sum(dh * (xh * g2), axis=0, keepdims=True)
                dxn = dh * (1.0 + scale2)
                sums_ref[2:3, :] += jnp.sum(dxn * xh, axis=0, keepdims=True)
                dxh = dxn * g2
                dx1 = dx2_ref[rs, :] + r2 * (dxh - xh * jnp.mean(dxh * xh, axis=-1, keepdims=True))
                sums_ref[3:4, :] += jnp.sum(dx1 * o1_ref[rs, :].astype(F32), axis=0, keepdims=True)
                dx1_ref[rs, :] = dx1
                do1_ref[rs, :] = (dx1 * gate1).astype(BF16)
                return carry

            lax.fori_loop(0, tm // SUB, sub, 0)

    return pl.pallas_call(
        body, name="ff1_bwd", grid=(s // tm, nk),
        in_specs=[pl.BlockSpec((tm, tk), lambda i, k: (i, k)),
                  pl.BlockSpec((None, D, tk), lambda i, k: (k // per, 0, k % per)),
                  pl.BlockSpec((tm, D), lambda i, k: (i, 0)), pl.BlockSpec((tm, D), lambda i, k: (i, 0)),
                  pl.BlockSpec((tm, D), lambda i, k: (i, 0)), pl.BlockSpec((16, D), lambda i, k: (0, 0))],
        out_specs=[pl.BlockSpec((tm, D), lambda i, k: (i, 0)), pl.BlockSpec((tm, D), lambda i, k: (i, 0)),
                   pl.BlockSpec((8, D), lambda i, k: (0, 0))],
        out_shape=[SDS((s, D), F32), SDS((s, D), BF16), SDS((8, D), F32)],
        scratch_shapes=[pltpu.VMEM((tm, D), F32)],
        compiler_params=_cp(2))(df, w_ff1, x1, dx2, o1, vecs)


def outproj_bwd(do1, w_out, yab, proj):
    s = do1.shape[0]
    tm, tn = min(512, s), 512
    per = D // tn

    def body(d_ref, w_ref, y_ref, g_ref, dy_ref, dp_ref):
        dm = _dot_nt(d_ref[...], w_ref[...])
        sg = jax.nn.sigmoid(g_ref[...].astype(F32))
        dy_ref[...] = (dm * sg).astype(BF16)
        dp_ref[...] = (dm * y_ref[...].astype(F32) * sg * (1.0 - sg)).astype(BF16)

    return pl.pallas_call(
        body, name="outproj_bwd", grid=(s // tm, 2 * per),
        in_specs=[pl.BlockSpec((tm, D), lambda i, j: (i, 0)), pl.BlockSpec((tn, D), lambda i, j: (j % per, 0)),
                  pl.BlockSpec((None, tm, tn), lambda i, j: (j // per, i, j % per)),
                  pl.BlockSpec((tm, tn), lambda i, j: (i, OFF_GA // tn + j))],
        out_specs=[pl.BlockSpec((None, tm, tn), lambda i, j: (j // per, i, j % per)),
                   pl.BlockSpec((tm, tn), lambda i, j: (i, OFF_GA // tn + j))],
        out_shape=[SDS((2, s, D), BF16), SDS((s, IN_W), BF16)],
        compiler_params=_cp(2))(do1, w_out, yab, proj)


def lruout_bwd(dyab, w_lru_out, rec, proj, dproj):
    s = rec.shape[0]
    tm, tn = min(512, s), 512

    def body(d_ref, w_ref, r_ref, g_ref, dp_in, dr_ref, dp_ref):
        dya = _dot_nt(d_ref[...], w_ref[...])
        gate = g_ref[...].astype(F32)
        dr_ref[...] = dya * _gelu(gate)
        dp_ref[...] = (dya * r_ref[...] * _gelu_grad(gate)).astype(BF16)

    return pl.pallas_call(
        body, name="lruout_bwd", grid=(s // tm, D // tn),
        in_specs=[pl.BlockSpec((None, tm, D), lambda i, j: (0, i, 0)), pl.BlockSpec((tn, D), lambda i, j: (j, 0)),
                  pl.BlockSpec((tm, tn), lambda i, j: (i, j)),
                  pl.BlockSpec((tm, tn), lambda i, j: (i, OFF_GATE // tn + j)),
                  pl.BlockSpec(memory_space=pl.ANY)],
        out_specs=[pl.BlockSpec((tm, tn), lambda i, j: (i, j)),
                   pl.BlockSpec((tm, tn), lambda i, j: (i, OFF_GATE // tn + j))],
        out_shape=[SDS((s, D), F32), SDS((s, IN_W), BF16)],
        input_output_aliases={4: 1},
        compiler_params=_cp(2))(dyab, w_lru_out, rec, proj, dproj)


def attnout_bwd(dyab, w_attn_out):
    s = dyab.shape[1]
    tm, tn = min(512, s), 512

    def body(d_ref, w_ref, o_ref):
        o_ref[...] = _dot_nt(d_ref[...], w_ref[...]).astype(BF16)

    return pl.pallas_call(
        body, name="attnout_bwd", grid=(s // tm, D // tn),
        in_specs=[pl.BlockSpec((None, tm, D), lambda i, j: (1, i, 0)), pl.BlockSpec((tn, D), lambda i, j: (j, 0))],
        out_specs=pl.BlockSpec((tm, tn), lambda i, j: (i, j)),
        out_shape=SDS((s, D), BF16), compiler_params=_cp(2))(dyab, w_attn_out)


def attn_bwd(proj, band, sinks, datt, dproj):
    s = proj.shape[0]
    nb = s // BLOCK
    qw = 1024

    def body(sk_ref, q_ref, kp_ref, kc_ref, vp_ref, vc_ref, b_ref, do_ref, dp_in,
             dq_ref, dkb_ref, dvb_ref, db_ref, ds_ref):
        gp = pl.program_id(0)
        n = pl.program_id(1)

        @pl.when(n == 0)
        def _():
            db_ref[...] = jnp.zeros_like(db_ref)
            ds_ref[...] = jnp.zeros_like(ds_ref)

        valid = _band_valid(n)
        kks = _kv_bands(kp_ref, kc_ref)
        vvs = _kv_bands(vp_ref, vc_ref)
        lane = lax.broadcasted_iota(jnp.int32, (BLOCK, 128), 1)
        lane_b = lax.broadcasted_iota(jnp.int32, (2 * BLOCK, 128), 1)
        dk_acc = [jnp.zeros((2 * BLOCK, 128), F32), jnp.zeros((2 * BLOCK, 128), F32)]
        dv_acc = [jnp.zeros((2 * BLOCK, 128), F32), jnp.zeros((2 * BLOCK, 128), F32)]
        for j in range(8):
            kv = j // 4
            qs = q_ref[:, j * 128:(j + 1) * 128]
            dos = do_ref[:, j * 128:(j + 1) * 128]
            dqs = []
            for hh in range(2):
                hl = 2 * j + hh
                half = (lane < 64) if hh == 0 else (lane >= 64)
                qm = jnp.where(half, qs, jnp.zeros_like(qs))
                dom = jnp.where(half, dos, jnp.zeros_like(dos))
                p, ps = _attn_probs(qm, kks[kv], b_ref[hl], sk_ref[gp * 16 + hl], valid)
                dp = _dot_nt(dom, vvs[kv])
                delta = jnp.sum(p * dp, axis=-1, keepdims=True)
                dsc = p * (dp - delta)
                db_ref[hl] += dsc
                ds_ref[pl.ds(hl, 1), :] += jnp.zeros((1, 128), F32) - jnp.sum(ps * delta)
                dsb = (dsc * SCALE).astype(BF16)
                dqs.append(_dot(dsb, kks[kv]))
                dk_acc[kv] = dk_acc[kv] + _dot_tn(dsb, qm)
                dv_acc[kv] = dv_acc[kv] + _dot_tn(p.astype(BF16), dom)
            dq_ref[:, j * 128:(j + 1) * 128] = jnp.where(lane < 64, dqs[0], dqs[1]).astype(BF16)
        for acc, ref in ((dk_acc, dkb_ref), (dv_acc, dvb_ref)):
            d0 = acc[0] + pltpu.roll(acc[0], 64, 1)
            d1 = acc[1] + pltpu.roll(acc[1], 64, 1)
            ref[...] = jnp.where(lane_b < 64, d0, d1)

    kb, vb = OFF_K // 128, OFF_V // 128
    return pl.pallas_call(
        body, name="attn_bwd", grid=(2, nb),
        in_specs=[pl.BlockSpec(memory_space=pltpu.SMEM),
                  pl.BlockSpec((BLOCK, qw), lambda g, n: (n, OFF_Q // qw + g)),
                  pl.BlockSpec((BLOCK, 128), lambda g, n: (jnp.maximum(n - 1, 0), kb + g)),
                  pl.BlockSpec((BLOCK, 128), lambda g, n: (n, kb + g)),
                  pl.BlockSpec((BLOCK, 128), lambda g, n: (jnp.maximum(n - 1, 0), vb + g)),
                  pl.BlockSpec((BLOCK, 128), lambda g, n: (n, vb + g)),
                  pl.BlockSpec((16, BLOCK, 2 * BLOCK), lambda g, n: (g, 0, 0)),
                  pl.BlockSpec((BLOCK, qw), lambda g, n: (n, g)),
                  pl.BlockSpec(memory_space=pl.ANY)],
        out_specs=[pl.BlockSpec((BLOCK, qw), lambda g, n: (n, OFF_Q // qw + g)),
                   pl.BlockSpec((2 * BLOCK, 128), lambda g, n: (n, g)),
                   pl.BlockSpec((2 * BLOCK, 128), lambda g, n: (n, g)),
                   pl.BlockSpec((16, BLOCK, 2 * BLOCK), lambda g, n: (g, 0, 0)),
                   pl.BlockSpec((16, 128), lambda g, n: (g, 0))],
        out_shape=[SDS((s, IN_W), BF16), SDS((nb * 2 * BLOCK, 256), F32), SDS((nb * 2 * BLOCK, 256), F32),
                   SDS((N_HEADS, BLOCK, 2 * BLOCK), F32), SDS((N_HEADS, 128), F32)],
        input_output_aliases={8: 0},
        compiler_params=_cp(2))(sinks, proj, proj, proj, proj, proj, band, datt, dproj)


def dkv_combine(dkb, dvb, dproj):
    nb = dkb.shape[0] // (2 * BLOCK)
    s = nb * BLOCK
    dkb3 = dkb.reshape(nb, 2 * BLOCK, 256)
    dvb3 = dvb.reshape(nb, 2 * BLOCK, 256)

    def body(k1, k2, v1, v2, dp_in, o_ref):
        nxt = jnp.where(pl.program_id(0) < nb - 1, 1.0, 0.0)
        o_ref[:, 0:256] = (k1[...] + nxt * k2[...]).astype(BF16)
        o_ref[:, 256:512] = (v1[...] + nxt * v2[...]).astype(BF16)

    spec1 = pl.BlockSpec((None, BLOCK, 256), lambda m: (m, 1, 0))
    spec2 = pl.BlockSpec((None, BLOCK, 256), lambda m: (jnp.minimum(m + 1, nb - 1), 0, 0))
    return pl.pallas_call(
        body, name="dkv_combine", grid=(nb,),
        in_specs=[spec1, spec2, spec1, spec2, pl.BlockSpec(memory_space=pl.ANY)],
        out_specs=pl.BlockSpec((BLOCK, 512), lambda m: (m, OFF_K // 512)),
        out_shape=SDS((s, IN_W), BF16), input_output_aliases={4: 0},
        compiler_params=_cp(1))(dkb3, dkb3, dvb3, dvb3, dproj)


def lru_bwd(proj, rec, drec, lvec, wa, wx, dproj):
    s = proj.shape[0]
    t = min(256, s)
    nt = s // t

    def body(lx_ref, lxh_ref, rec_ref, rech_ref, dr_ref, lv_ref, wa_ref, wx_ref, dp_in,
             dlx_ref, sums_ref, dwa_ref, dwx_ref,
             xbuf, hbuf, dxbuf, a_s, dh_s, xc_s, r_s, ig_s, mu_s, gc):
        step_i = pl.program_id(0)
        ti = nt - 1 - step_i

        @pl.when(step_i == 0)
        def _():
            sums_ref[...] = jnp.zeros_like(sums_ref)
            dwa_ref[...] = jnp.zeros_like(dwa_ref)
            dwx_ref[...] = jnp.zeros_like(dwx_ref)
            dxbuf[pl.ds(t, 8), :] = jnp.zeros((8, D), F32)
            gc[...] = jnp.zeros((8, D), F32)

        live = jnp.where(ti > 0, 1.0, 0.0)
        xbuf[pl.ds(0, 8), :] = lxh_ref[...].astype(F32)[8:16] * live
        xbuf[pl.ds(8, t), :] = lx_ref[...].astype(F32)
        hbuf[pl.ds(0, 8), :] = rech_ref[...] * live
        hbuf[pl.ds(8, t), :] = rec_ref[...]
        first = (lax.broadcasted_iota(jnp.int32, (t, 128), 0) + ti * t) == 0
        for b in range(N_LRU_BLOCKS):
            cs = slice(b * 128, (b + 1) * 128)
            _, xc, _, r, ig, _, a, mult = _lru_block_fwd(xbuf, lv_ref, wa_ref, wx_ref, b, t, first)
            a_s[:, cs] = a
            xc_s[:, cs] = xc
            r_s[:, cs] = r
            ig_s[:, cs] = ig
            mu_s[:, cs] = mult

        def step(q, g):
            tt = t - 1 - q
            dh = dr_ref[pl.ds(tt, 1), :] + g
            dh_s[pl.ds(tt, 1), :] = dh
            return a_s[pl.ds(tt, 1), :] * dh

        gc[0:1, :] = lax.fori_loop(0, t, step, gc[0:1, :], unroll=8)
        for b in range(N_LRU_BLOCKS):
            cs = slice(b * 128, (b + 1) * 128)
            dh = dh_s[:, cs]
            a = a_s[:, cs]
            xc = xc_s[:, cs]
            r = r_s[:, cs]
            ig = ig_s[:, cs]
            mult = mu_s[:, cs]
            sp = _softplus(-lv_ref[L_LAM:L_LAM + 1, cs])
            lam = lv_ref[L_LAM:L_LAM + 1, cs]
            da = dh * hbuf[pl.ds(7, t), cs]
            dmult = jnp.where(first, 0.0, dh * ig * xc)
            dig = dh * mult * xc
            dxc = dh * mult * ig
            dlog_a = da * a - dmult * (a * a) / mult
            dr = dlog_a * ((-LRU_C) * sp)
            dsp = jnp.sum(dlog_a * ((-LRU_C) * r), axis=0, keepdims=True)
            dza = dr * r * (1.0 - r)
            dzx = dig * ig * (1.0 - ig)
            dzab = dza.astype(BF16)
            dzxb = dzx.astype(BF16)
            xcb = xc.astype(BF16)
            dwa_ref[b] += _dot_tn(xcb, dzab)
            dwx_ref[b] += _dot_tn(xcb, dzxb)
            dxc = dxc + _dot_nt(dzab, wa_ref[b]) + _dot_nt(dzxb, wx_ref[b])
            sums_ref[L_LAM:L_LAM + 1, cs] += dsp * (-jax.nn.sigmoid(-lam))
            sums_ref[L_BA:L_BA + 1, cs] += jnp.sum(dza, axis=0, keepdims=True)
            sums_ref[L_BX:L_BX + 1, cs] += jnp.sum(dzx, axis=0, keepdims=True)
            sums_ref[L_CB:L_CB + 1, cs] += jnp.sum(dxc, axis=0, keepdims=True)
            for kk in range(4):
                sums_ref[kk:kk + 1, cs] += jnp.sum(dxc * xbuf[pl.ds(5 + kk, t), cs], axis=0, keepdims=True)
            dxbuf[pl.ds(0, t), cs] = dxc
            dlx = (lv_ref[3:4, cs] * dxc + lv_ref[2:3, cs] * dxbuf[pl.ds(1, t), cs]
                   + lv_ref[1:2, cs] * dxbuf[pl.ds(2, t), cs] + lv_ref[0:1, cs] * dxbuf[pl.ds(3, t), cs])
            dlx_ref[:, cs] = dlx.astype(BF16)
        dxbuf[pl.ds(t, 8), :] = dxbuf[pl.ds(0, 8), :]

    rev = lambda i: nt - 1 - i
    return pl.pallas_call(
        body, name="lru_bwd", grid=(nt,),
        in_specs=[pl.BlockSpec((t, D), lambda i: (rev(i), 0)),
                  pl.BlockSpec((16, D), lambda i: (jnp.maximum(rev(i) * (t // 16) - 1, 0), 0)),
                  pl.BlockSpec((t, D), lambda i: (rev(i), 0)),
                  pl.BlockSpec((8, D), lambda i: (jnp.maximum(rev(i) * (t // 8) - 1, 0), 0)),
                  pl.BlockSpec((t, D), lambda i: (rev(i), 0)),
                  pl.BlockSpec((8, D), lambda i: (0, 0)),
                  pl.BlockSpec((N_LRU_BLOCKS, 128, 128), lambda i: (0, 0, 0)),
                  pl.BlockSpec((N_LRU_BLOCKS, 128, 128), lambda i: (0, 0, 0)),
                  pl.BlockSpec(memory_space=pl.ANY)],
        out_specs=[pl.BlockSpec((t, D), lambda i: (rev(i), 0)),
                   pl.BlockSpec((8, D), lambda i: (0, 0)),
                   pl.BlockSpec((N_LRU_BLOCKS, 128, 128), lambda i: (0, 0, 0)),
                   pl.BlockSpec((N_LRU_BLOCKS, 128, 128), lambda i: (0, 0, 0))],
        out_shape=[SDS((s, IN_W), BF16), SDS((8, D), F32), SDS((N_LRU_BLOCKS, 128, 128), F32),
                   SDS((N_LRU_BLOCKS, 128, 128), F32)],
        scratch_shapes=[pltpu.VMEM((t + 8, D), F32), pltpu.VMEM((t + 8, D), F32), pltpu.VMEM((t + 8, D), F32)]
        + [pltpu.VMEM((t, D), F32)] * 6 + [pltpu.VMEM((8, D), F32)],
        input_output_aliases={8: 0},
        compiler_params=_cp(1))(proj, proj, rec, rec, drec, lvec, wa, wx, dproj)


def inproj_bwd(dproj, w_in, x, dx1, vecs):
    s = x.shape[0]
    tm, tk = min(512, s), IN_TILE
    nk = IN_W // tk
    per = IN_SHARD // tk

    def body(d_ref, w_ref, x_ref, dx1_ref, v_ref, gx_ref, sums_ref, acc):
        i, k = pl.program_id(0), pl.program_id(1)

        @pl.when((i == 0) & (k == 0))
        def _():
            sums_ref[...] = jnp.zeros_like(sums_ref)

        @pl.when(k == 0)
        def _():
            acc[...] = jnp.zeros_like(acc)

        acc[...] += _dot_nt(d_ref[...], w_ref[...])

        @pl.when(k == nk - 1)
        def _():
            g1 = v_ref[V_G1:V_G1 + 1, :]
            scale1 = v_ref[V_SCALE1:V_SCALE1 + 1, :]

            def sub(rb, carry):
                rs = pl.ds(pl.multiple_of(rb * SUB, SUB), SUB)
                dh = acc[rs, :]
                r1, xh = _rms_parts(x_ref[rs, :])
                sums_ref[0:1, :] += jnp.sum(dh, axis=0, keepdims=True)
                sums_ref[1:2, :] += jnp.sum(dh * (xh * g1), axis=0, keepdims=True)
                dxn = dh * (1.0 + scale1)
                sums_ref[2:3, :] += jnp.sum(dxn * xh, axis=0, keepdims=True)
                dxh = dxn * g1
                gx_ref[rs, :] = dx1_ref[rs, :] + r1 * (dxh - xh * jnp.mean(dxh * xh, axis=-1, keepdims=True))
                return carry

            lax.fori_loop(0, tm // SUB, sub, 0)

    return pl.pallas_call(
        body, name="inproj_bwd", grid=(s // tm, nk),
        in_specs=[pl.BlockSpec((tm, tk), lambda i, k: (i, k)),
                  pl.BlockSpec((None, D, tk), lambda i, k: (k // per, 0, k % per)),
                  pl.BlockSpec((tm, D), lambda i, k: (i, 0)), pl.BlockSpec((tm, D), lambda i, k: (i, 0)),
                  pl.BlockSpec((16, D), lambda i, k: (0, 0))],
        out_specs=[pl.BlockSpec((tm, D), lambda i, k: (i, 0)), pl.BlockSpec((8, D), lambda i, k: (0, 0))],
        out_shape=[SDS((s, D), F32), SDS((8, D), F32)],
        scratch_shapes=[pltpu.VMEM((tm, D), F32)],
        compiler_params=_cp(2))(dproj, w_in, x, dx1, vecs)


def mod_columns(c16, w_ada, b_cols):
    tn = 512

    def body(c_ref, w_ref, b_ref, o_ref):
        cv = c_ref[...]
        ca = (cv * jax.nn.sigmoid(cv)).astype(BF16)
        o_ref[...] = _dot(ca, w_ref[...].astype(BF16)) + b_ref[...]

    return pl.pallas_call(
        body, name="mod_columns", grid=(ADA_SHARD // tn,),
        in_specs=[pl.BlockSpec((16, D), lambda j: (0, 0)), pl.BlockSpec((D, tn), lambda j: (0, j)),
                  pl.BlockSpec((1, tn), lambda j: (0, j))],
        out_specs=pl.BlockSpec((16, tn), lambda j: (0, j)),
        out_shape=SDS((16, ADA_SHARD), F32), compiler_params=_cp(1))(c16, w_ada, b_cols)


def wada_update(c16, dmod16, w, m, v):
    tm, tn = 512, 512

    def body(c_ref, d_ref, w_ref, m_ref, v_ref, g_out, dl_out, m_out, v_out):
        cv = c_ref[...]
        ca = (cv * jax.nn.sigmoid(cv)).astype(BF16)
        g = _dot_tn(ca, d_ref[...].astype(BF16))
        dl, m2, v2 = _adamw_math(w_ref[...], g, m_ref[...], v_ref[...])
        g_out[...] = g
        dl_out[...] = dl
        m_out[...] = m2
        v_out[...] = v2

    tile = pl.BlockSpec((tm, tn), lambda i, j: (i, j))
    return pl.pallas_call(
        body, name="wada_update", grid=(D // tm, ADA_SHARD // tn),
        in_specs=[pl.BlockSpec((16, tm), lambda i, j: (0, i)), pl.BlockSpec((16, tn), lambda i, j: (0, j)),
                  tile, tile, tile],
        out_specs=[tile] * 4, out_shape=[SDS((D, ADA_SHARD), F32)] * 4,
        compiler_params=_cp(2))(c16, dmod16, w, m, v)


def adamw_big(name, w, g, m, v):
    r, c = w.shape
    tr = 128

    def body(w_ref, g_ref, m_ref, v_ref, dl_out, m_out, v_out):
        dl, m2, v2 = _adamw_math(w_ref[...], g_ref[...], m_ref[...], v_ref[...])
        dl_out[...] = dl
        m_out[...] = m2
        v_out[...] = v2

    tile = pl.BlockSpec((tr, c), lambda i: (i, 0))
    return pl.pallas_call(
        body, name=name, grid=(r // tr,), in_specs=[tile] * 4, out_specs=[tile] * 3,
        out_shape=[SDS((r, c), F32)] * 3, compiler_params=_cp(1))(w, g, m, v)


def adamw_small(ws, gs, ms, vs):
    n = len(ws)

    def body(*refs):
        for i in range(n):
            dl, m2, v2 = _adamw_math(refs[i][...], refs[n + i][...], refs[2 * n + i][...], refs[3 * n + i][...])
            refs[4 * n + i][...] = dl
            refs[5 * n + i][...] = m2
            refs[6 * n + i][...] = v2

    vm = pl.BlockSpec(memory_space=pltpu.VMEM)
    shapes = [SDS(w.shape, F32) for w in ws]
    outs = pl.pallas_call(
        body, name="adamw_small", in_specs=[vm] * (4 * n), out_specs=[vm] * (3 * n), out_shape=shapes * 3,
        compiler_params=pltpu.CompilerParams(vmem_limit_bytes=VMEM_LIMIT))(*ws, *gs, *ms, *vs)
    return outs[:n], outs[n:2 * n], outs[2 * n:]


def sum_devices(gathered):
    rows = gathered.shape[1]
    tr = 96

    def body(x_ref, o_ref):
        acc = x_ref[0]
        for d in range(1, N_DEV):
            acc = acc + x_ref[d]
        o_ref[...] = acc

    return pl.pallas_call(
        body, name="sum_devices", grid=(rows // tr,),
        in_specs=[pl.BlockSpec((N_DEV, tr, D), lambda i: (0, i, 0))],
        out_specs=pl.BlockSpec((tr, D), lambda i: (i, 0)),
        out_shape=SDS((rows, D), F32), compiler_params=_cp(1))(gathered)


def _mesh_pos():
    return lax.axis_index("x"), lax.axis_index("y"), lax.axis_index("c")


def _other_chips(x, y):
    return [(1 - x, y), (x, 1 - y), (1 - x, 1 - y)]


def all_gather_small(name, block):
    m_per, n = block.shape

    def body(x_ref, out_ref, send_sems, recv_sems, local_sem):
        x, y, c = _mesh_pos()
        me, sibling = (x, y, c), (x, y, 1 - c)
        chips = _other_chips(x, y)

        def rows(px, py, pc):
            return out_ref.at[pl.ds((4 * px + 2 * py + pc) * m_per, m_per), :]

        def copy(k, blk, to, src=None):
            return pltpu.make_async_remote_copy(
                src_ref=rows(*blk) if src is None else src, dst_ref=rows(*blk),
                send_sem=send_sems.at[k], recv_sem=recv_sems.at[k], device_id=to, device_id_type=MESH)

        mine = pltpu.make_async_copy(x_ref, rows(*me), local_sem)
        mine.start()
        first = [copy(0, me, sibling, src=x_ref)]
        first += [copy(1 + j, me, (*chip, c), src=x_ref) for j, chip in enumerate(chips)]
        for cp in first:
            cp.start()
        passed = [copy(4 + j, (*chip, c), sibling) for j, chip in enumerate(chips)]
        for j, chip in enumerate(chips):
            copy(1 + j, (*chip, c), me).wait_recv()
            passed[j].start()
        copy(0, sibling, me).wait_recv()
        for j, chip in enumerate(chips):
            copy(4 + j, (*chip, 1 - c), me).wait_recv()
        for cp in first + passed:
            cp.wait_send()
        mine.wait()

    vm = pl.BlockSpec(memory_space=pltpu.VMEM)
    return pl.pallas_call(
        body, name=name, out_shape=SDS((N_DEV * m_per, n), block.dtype), in_specs=[vm], out_specs=vm,
        scratch_shapes=[pltpu.SemaphoreType.DMA((7,)), pltpu.SemaphoreType.DMA((7,)), pltpu.SemaphoreType.DMA],
        compiler_params=pltpu.CompilerParams(vmem_limit_bytes=VMEM_LIMIT))(block)


def all_gather_weights(shards):
    n = len(shards)
    halves = [w.shape[0] // 2 for w in shards]

    def body(*refs):
        ins, outs = refs[:n], refs[n:2 * n]
        send_sems, recv_sems, local_sems = refs[2 * n:]
        x, y, c = _mesh_pos()
        k = 2 * x + y
        me, sibling = (x, y, c), (x, y, 1 - c)
        chips = _other_chips(x, y)

        def region(i, chip_idx, half):
            return outs[i].at[chip_idx, pl.ds(half * halves[i], halves[i]), :]

        def copy(i, j, src, dst, to):
            return pltpu.make_async_remote_copy(src_ref=src, dst_ref=dst, send_sem=send_sems.at[i, j],
                                                recv_sem=recv_sems.at[i, j], device_id=to, device_id_type=MESH)

        local = [pltpu.make_async_copy(ins[i], outs[i].at[k], local_sems.at[i]) for i in range(n)]
        for cp in local:
            cp.start()
        started = []
        for i in range(n):
            for j, chip in enumerate(chips):
                cp = copy(i, j, ins[i].at[pl.ds(c * halves[i], halves[i]), :], region(i, k, c), (*chip, c))
                cp.start()
                started.append(cp)
        for i in range(n):
            for j, chip in enumerate(chips):
                kj = 2 * chip[0] + chip[1]
                copy(i, j, region(i, kj, c), region(i, kj, c), me).wait_recv()
                cp = copy(i, 3 + j, region(i, kj, c), region(i, kj, c), sibling)
                cp.start()
                started.append(cp)
        for i in range(n):
            for j, chip in enumerate(chips):
                kj = 2 * chip[0] + chip[1]
                copy(i, 3 + j, region(i, kj, 1 - c), region(i, kj, 1 - c), me).wait_recv()
        for cp in started:
            cp.wait_send()
        for cp in local:
            cp.wait()

    hbm = pl.BlockSpec(memory_space=pl.ANY)
    return pl.pallas_call(
        body, name="all_gather_weights", in_specs=[hbm] * n, out_specs=[hbm] * n,
        out_shape=[SDS((N_CHIPS,) + w.shape, w.dtype) for w in shards],
        scratch_shapes=[pltpu.SemaphoreType.DMA((n, 6)), pltpu.SemaphoreType.DMA((n, 6)),
                        pltpu.SemaphoreType.DMA((n,))])(*shards)


def sibling_exchange(grads):
    n = len(grads)
    halves = [g.shape[1] // 2 for g in grads]

    def body(*refs):
        ins, outs = refs[:n], refs[n:2 * n]
        send_sems, recv_sems = refs[2 * n:]
        x, y, c = _mesh_pos()
        sibling = (x, y, 1 - c)
        cps = []
        for i in range(n):
            cp = pltpu.make_async_remote_copy(
                src_ref=ins[i].at[:, pl.ds((1 - c) * halves[i], halves[i]), :], dst_ref=outs[i],
                send_sem=send_sems.at[i], recv_sem=recv_sems.at[i], device_id=sibling, device_id_type=MESH)
            cp.start()
            cps.append(cp)
        for cp in cps:
            cp.wait_recv()
        for cp in cps:
            cp.wait_send()

    hbm = pl.BlockSpec(memory_space=pl.ANY)
    return pl.pallas_call(
        body, name="sibling_exchange", in_specs=[hbm] * n, out_specs=[hbm] * n,
        out_shape=[SDS((N_CHIPS, h, g.shape[2]), F32) for g, h in zip(grads, halves)],
        scratch_shapes=[pltpu.SemaphoreType.DMA((n,)), pltpu.SemaphoreType.DMA((n,))])(*grads)


def sibling_sum(name, grad, other, c_idx):
    _, r, cc = grad.shape
    h = r // 2
    tr = min(256, h)
    g4 = grad.reshape(N_CHIPS, 2, h, cc)

    def body(c_ref, a_ref, b_ref, o_ref):
        o_ref[...] = (a_ref[...] + b_ref[...]).astype(BF16)

    gs = pltpu.PrefetchScalarGridSpec(
        num_scalar_prefetch=1, grid=(N_CHIPS, h // tr),
        in_specs=[pl.BlockSpec((None, None, tr, cc), lambda s, i, cr: (s, cr[0], i, 0)),
                  pl.BlockSpec((None, tr, cc), lambda s, i, cr: (s, i, 0))],
        out_specs=pl.BlockSpec((None, tr, cc), lambda s, i, cr: (s, i, 0)))
    return pl.pallas_call(body, name=name, grid_spec=gs, out_shape=SDS((N_CHIPS, h, cc), BF16),
                          compiler_params=_cp(2))(c_idx, g4, other)


def chip_exchange(sums):
    n = len(sums)

    def body(*refs):
        ins, outs = refs[:n], refs[n:2 * n]
        send_sems, recv_sems, local_sems = refs[2 * n:]
        x, y, c = _mesh_pos()
        k = 2 * x + y
        chips = _other_chips(x, y)
        local = [pltpu.make_async_copy(ins[i].at[k], outs[i].at[k], local_sems.at[i]) for i in range(n)]
        for cp in local:
            cp.start()
        cps = []
        for i in range(n):
            for j, chip in enumerate(chips):
                kj = 2 * chip[0] + chip[1]
                cp = pltpu.make_async_remote_copy(
                    src_ref=ins[i].at[kj], dst_ref=outs[i].at[k], send_sem=send_sems.at[i, j],
                    recv_sem=recv_sems.at[i, j], device_id=(*chip, c), device_id_type=MESH)
                cp.start()
                cps.append((cp, i, j, kj))
        for cp, i, j, kj in cps:
            pltpu.make_async_remote_copy(
                src_ref=ins[i].at[kj], dst_ref=outs[i].at[kj], send_sem=send_sems.at[i, j],
                recv_sem=recv_sems.at[i, j], device_id=(x, y, c), device_id_type=MESH).wait_recv()
        for cp, _, _, _ in cps:
            cp.wait_send()
        for cp in local:
            cp.wait()

    hbm = pl.BlockSpec(memory_space=pl.ANY)
    return pl.pallas_call(
        body, name="chip_exchange", in_specs=[hbm] * n, out_specs=[hbm] * n,
        out_shape=[SDS(t.shape, t.dtype) for t in sums],
        scratch_shapes=[pltpu.SemaphoreType.DMA((n, 3)), pltpu.SemaphoreType.DMA((n, 3)),
                        pltpu.SemaphoreType.DMA((n,))])(*sums)


def chip_sum(name, parts):
    _, h, cc = parts.shape
    tr = min(256, h)

    def body(p_ref, o_ref):
        acc = p_ref[0].astype(F32)
        for s in range(1, N_CHIPS):
            acc = acc + p_ref[s].astype(F32)
        o_ref[...] = acc

    return pl.pallas_call(
        body, name=name, grid=(h // tr,),
        in_specs=[pl.BlockSpec((N_CHIPS, tr, cc), lambda i: (0, i, 0))],
        out_specs=pl.BlockSpec((tr, cc), lambda i: (i, 0)),
        out_shape=SDS((h, cc), F32), compiler_params=_cp(1))(parts)


def halves_exchange(halves):
    n = len(halves)

    def body(*refs):
        ins, outs = refs[:n], refs[n:2 * n]
        send_sems, recv_sems, local_sems = refs[2 * n:]
        x, y, c = _mesh_pos()
        sibling = (x, y, 1 - c)
        local = [pltpu.make_async_copy(ins[i], outs[i].at[c], local_sems.at[i]) for i in range(n)]
        for cp in local:
            cp.start()
        cps = []
        for i in range(n):
            cp = pltpu.make_async_remote_copy(
                src_ref=ins[i], dst_ref=outs[i].at[c], send_sem=send_sems.at[i], recv_sem=recv_sems.at[i],
                device_id=sibling, device_id_type=MESH)
            cp.start()
            cps.append(cp)
        for i in range(n):
            pltpu.make_async_remote_copy(
                src_ref=ins[i], dst_ref=outs[i].at[1 - c], send_sem=send_sems.at[i], recv_sem=recv_sems.at[i],
                device_id=(x, y, c), device_id_type=MESH).wait_recv()
        for cp in cps:
            cp.wait_send()
        for cp in local:
            cp.wait()

    hbm = pl.BlockSpec(memory_space=pl.ANY)
    return pl.pallas_call(
        body, name="halves_exchange", in_specs=[hbm] * n, out_specs=[hbm] * n,
        out_shape=[SDS((2,) + t.shape, F32) for t in halves],
        scratch_shapes=[pltpu.SemaphoreType.DMA((n,)), pltpu.SemaphoreType.DMA((n,)),
                        pltpu.SemaphoreType.DMA((n,))])(*halves)


def local_step(x, tgt, vecs, lvec, wa, wx, sinks, rel_bias, weights):
    w_in, w_lru_out, w_attn_out, w_out, w_ff1, w_ff2 = weights
    s = x.shape[0]
    buckets = t5_bucket_table()
    band = bias_band(rel_bias.T, buckets).reshape(N_HEADS, BLOCK, 2 * BLOCK)
    w_lru_out2, w_attn_out2, w_out2 = w_lru_out.reshape(D, D), w_attn_out.reshape(D, D), w_out.reshape(D, D)
    w_ff2_2 = w_ff2.reshape(D_FF, D)

    proj, h = inproj_fwd(x, vecs, w_in)
    ya, rec = lru_fwd(proj, lvec, wa, wx)
    att = attn_fwd(proj, band, sinks)
    yab, merged = merge_fwd(ya, att, w_lru_out2, w_attn_out2, proj)
    x1, o1 = outproj_fwd(merged, w_out2, x, vecs)
    f, h2 = ff1_fwd(x1, vecs, w_ff1)
    dx2, do2, sums_f, loss = ff2_loss(f, w_ff2_2, x1, tgt, vecs)

    df = ff2_bwd(do2, w_ff2_2, f)
    g_ff2 = weight_grad("dw_ff2", f, do2, 512, (D_FF, D), (512, 512), lambda i, j: (i, j), relu2=True)
    dx1, do1, sums_2 = ff1_bwd(df, w_ff1, x1, dx2, o1, vecs)
    g_ff1 = weight_grad("dw_ff1", h2, df, 512, (N_CHIPS, D, D), (None, 512, 512), lambda i, j: (j // 4, i, j % 4))
    dyab, dproj = outproj_bwd(do1, w_out2, yab, proj)
    g_out = weight_grad("dw_out", merged, do1, 512, (D, D), (512, 512), lambda i, j: (i, j))
    drec, dproj = lruout_bwd(dyab, w_lru_out2, rec, proj, dproj)
    g_lru_out = weight_grad("dw_lru_out", ya, dyab[0], 512, (D, D), (512, 512), lambda i, j: (i, j))
    datt = attnout_bwd(dyab, w_attn_out2)
    g_attn_out = weight_grad("dw_attn_out", att, dyab[1], 512, (D, D), (512, 512), lambda i, j: (i, j))
    dproj, dkb, dvb, dband, dsink = attn_bwd(proj, band, sinks, datt, dproj)
    dproj = dkv_combine(dkb, dvb, dproj)
    dproj, sums_l, d_wa, d_wx = lru_bwd(proj, rec, drec, lvec, wa, wx, dproj)
    grad_x, sums_1 = inproj_bwd(dproj, w_in, x, dx1, vecs)
    per = IN_SHARD // IN_TILE
    g_in = weight_grad("dw_in", h, dproj, IN_TILE, (N_CHIPS, D, IN_SHARD), (None, 512, IN_TILE),
                       lambda i, j: (j // per, i, j % per))
    d_rel_bias = bias_band_bwd(dband.reshape(N_HEADS, BLOCK * 2 * BLOCK), buckets)

    big = [g_in, g_lru_out.reshape(N_CHIPS, D // 4, D), g_attn_out.reshape(N_CHIPS, D // 4, D),
           g_out.reshape(N_CHIPS, D // 4, D), g_ff1, g_ff2.reshape(N_CHIPS, D_FF // 4, D)]
    small = dict(sums_f=sums_f, sums_2=sums_2, sums_1=sums_1, sums_l=sums_l, d_wa=d_wa, d_wx=d_wx,
                 d_sinks=dsink[:, 0], d_rel_bias=d_rel_bias)
    return loss, grad_x, big, small


def _pad_rows(a, rows):
    return jnp.concatenate([a, jnp.zeros((rows - a.shape[0], a.shape[1]), a.dtype)], axis=0)


def kernel(x, c, w_ada, b_ada, norm1_g, w_in, conv_w, conv_b, lru_wa, lru_ba, lru_wx, lru_bx, lru_lambda, w_lru_out, w_attn_out, attn_sinks, rel_bias, w_out, norm2_g, w_ff1, w_ff2, final_g, loss_target, m_w_ada, m_b_ada, m_norm1_g, m_w_in, m_conv_w, m_conv_b, m_lru_wa, m_lru_ba, m_lru_wx, m_lru_bx, m_lru_lambda, m_w_lru_out, m_w_attn_out, m_attn_sinks, m_rel_bias, m_w_out, m_norm2_g, m_w_ff1, m_w_ff2, m_final_g, v_w_ada, v_b_ada, v_norm1_g, v_w_in, v_conv_w, v_conv_b, v_lru_wa, v_lru_ba, v_lru_wx, v_lru_bx, v_lru_lambda, v_w_lru_out, v_w_attn_out, v_attn_sinks, v_rel_bias, v_w_out, v_norm2_g, v_w_ff1, v_w_ff2, v_final_g):
    xi, yi, ci = _mesh_pos()
    chip = 2 * xi + yi
    dev = 2 * chip + ci
    z8 = jnp.zeros((8, D), F32)

    conv_rows = jnp.concatenate([conv_w[0], jnp.zeros((4, D - D // 4), F32)], axis=1)
    pack0 = jnp.concatenate([c, conv_rows, jnp.zeros((3, D), F32)], axis=0)
    g0 = all_gather_small("gather_cond", pack0).reshape(N_DEV, 8, D)
    c_all = g0[:, 0, :]
    conv_full = jnp.concatenate([g0[2 * k, 1:5, :D // 4] for k in range(N_CHIPS)], axis=1)
    c16 = jnp.concatenate([c_all, z8], axis=0)
    b_cols = lax.dynamic_slice_in_dim(b_ada, chip * ADA_SHARD, ADA_SHARD, axis=1)
    mod_c = mod_columns(c16, w_ada[0], b_cols)
    g1 = all_gather_small("gather_mod", mod_c).reshape(N_DEV, 16, ADA_SHARD)
    mod = jnp.concatenate([lax.dynamic_index_in_dim(g1[2 * k], dev, axis=0, keepdims=False) for k in range(N_CHIPS)])
    shift1, scale1, gate1, shift2, scale2, gate2 = [mod[i * D:(i + 1) * D] for i in range(6)]
    vecs = jnp.stack([norm1_g[0], scale1, shift1, gate1, norm2_g[0], scale2, shift2, gate2, final_g]
                     + [jnp.zeros((D,), F32)] * 7)
    lvec = jnp.concatenate([conv_full, conv_b, lru_ba, lru_bx, lru_lambda], axis=0)

    shards = [w_in[0], w_lru_out[0], w_attn_out[0], w_out[0], w_ff1[0], w_ff2[0]]
    weights = all_gather_weights([w.astype(BF16) for w in shards])

    loss_t, grad_x, big, small = local_step(
        x[0], loss_target[0], vecs, lvec, lru_wa[0].astype(BF16), lru_wx[0].astype(BF16),
        attn_sinks[0], rel_bias, weights)
    loss = lax.psum(loss_t[0, 0], ("x", "y", "c"))

    sums_f, sums_2, sums_1, sums_l = small["sums_f"], small["sums_2"], small["sums_1"], small["sums_l"]
    vec_rows = jnp.stack([sums_1[2], sums_2[2], sums_f[0], sums_l[L_CB], sums_l[L_BA], sums_l[L_BX],
                          sums_l[L_LAM], jnp.zeros((D,), F32)])
    mod_rows = jnp.stack([sums_1[0], sums_1[1], sums_2[3], sums_2[0], sums_2[1], sums_f[1],
                          jnp.zeros((D,), F32), jnp.zeros((D,), F32)])
    att_rows = jnp.concatenate([
        jnp.concatenate([small["d_sinks"], jnp.zeros((D - N_HEADS,), F32)])[None],
        jnp.concatenate([small["d_rel_bias"].reshape(-1), jnp.zeros((D - N_BUCKETS * N_HEADS,), F32)])[None],
        jnp.zeros((6, D), F32)], axis=0)
    pack = jnp.concatenate([vec_rows, _pad_rows(sums_l[0:4], 8), mod_rows, att_rows,
                            small["d_wa"].reshape(128, D), small["d_wx"].reshape(128, D)], axis=0)
    gathered = all_gather_small("gather_small_grads", pack).reshape(N_DEV, P_ROWS, D)
    total = sum_devices(gathered)
    dmod_all = gathered[:, P_MOD:P_MOD + 6, :].reshape(N_DEV, 6 * D)
    dmod16 = jnp.concatenate([lax.dynamic_slice_in_dim(dmod_all, chip * ADA_SHARD, ADA_SHARD, axis=1),
                              jnp.zeros((8, ADA_SHARD), F32)], axis=0)
    g_w_ada, d_w_ada, nm_w_ada, nv_w_ada = wada_update(c16, dmod16, w_ada[0], m_w_ada[0], v_w_ada[0])

    c_idx = jnp.reshape(ci, (1,)).astype(jnp.int32)
    from_sibling = sibling_exchange(big)
    names = ["w_in", "w_lru_out", "w_attn_out", "w_out", "w_ff1", "w_ff2"]
    chip_sums = [sibling_sum("sibling_sum_" + nm, g, o, c_idx) for nm, g, o in zip(names, big, from_sibling)]
    parts = chip_exchange(chip_sums)
    halves = [chip_sum("chip_sum_" + nm, p) for nm, p in zip(names, parts)]
    full = halves_exchange(halves)
    big_m = [m_w_in, m_w_lru_out, m_w_attn_out, m_w_out, m_w_ff1, m_w_ff2]
    big_v = [v_w_in, v_w_lru_out, v_w_attn_out, v_w_out, v_w_ff1, v_w_ff2]
    g_big, d_big, nm_big, nv_big = {}, {}, {}, {}
    for nm, w, g, m, v in zip(names, shards, full, big_m, big_v):
        g2 = g.reshape(w.shape)
        dl, m2, v2 = adamw_big("adamw_" + nm, w, g2, m[0], v[0])
        g_big[nm], d_big[nm], nm_big[nm], nv_big[nm] = g2[None], dl[None], m2[None], v2[None]

    conv_g = lax.dynamic_slice_in_dim(total[P_CONVW:P_CONVW + 4], chip * (D // 4), D // 4, axis=1)
    sm_names = ["b_ada", "norm1_g", "conv_w", "conv_b", "lru_wa", "lru_ba", "lru_wx", "lru_bx", "lru_lambda",
                "attn_sinks", "rel_bias", "norm2_g", "final_g"]
    sm_w = [b_ada.reshape(6, D), norm1_g, conv_w[0], conv_b, lru_wa.reshape(128, D), lru_ba, lru_wx.reshape(128, D),
            lru_bx, lru_lambda, attn_sinks, rel_bias, norm2_g, final_g[None]]
    sm_m = [m_b_ada.reshape(6, D), m_norm1_g, m_conv_w[0], m_conv_b, m_lru_wa.reshape(128, D), m_lru_ba,
            m_lru_wx.reshape(128, D), m_lru_bx, m_lru_lambda, m_attn_sinks, m_rel_bias, m_norm2_g, m_final_g[None]]
    sm_v = [v_b_ada.reshape(6, D), v_norm1_g, v_conv_w[0], v_conv_b, v_lru_wa.reshape(128, D), v_lru_ba,
            v_lru_wx.reshape(128, D), v_lru_bx, v_lru_lambda, v_attn_sinks, v_rel_bias, v_norm2_g, v_final_g[None]]
    sm_g = [total[P_MOD:P_MOD + 6], total[0:1], conv_g, total[3:4], total[P_WA:P_WA + 128], total[4:5],
            total[P_WX:P_WX + 128], total[5:6], total[6:7], total[P_ATT:P_ATT + 1, :N_HEADS],
            total[P_ATT + 1, :N_BUCKETS * N_HEADS].reshape(N_BUCKETS, N_HEADS), total[1:2], total[2:3]]
    sm_d, sm_nm, sm_nv = adamw_small(sm_w, sm_g, sm_m, sm_v)
    shapes = dict(b_ada=b_ada.shape, norm1_g=norm1_g.shape, conv_w=conv_w.shape, conv_b=conv_b.shape,
                  lru_wa=lru_wa.shape, lru_ba=lru_ba.shape, lru_wx=lru_wx.shape, lru_bx=lru_bx.shape,
                  lru_lambda=lru_lambda.shape, attn_sinks=attn_sinks.shape, rel_bias=rel_bias.shape,
                  norm2_g=norm2_g.shape, final_g=final_g.shape)
    grads = dict(w_ada=g_w_ada[None], **g_big)
    deltas = dict(w_ada=d_w_ada[None], **d_big)
    new_m = dict(w_ada=nm_w_ada[None], **nm_big)
    new_v = dict(w_ada=nv_w_ada[None], **nv_big)
    for i, nm in enumerate(sm_names):
        grads[nm] = sm_g[i].reshape(shapes[nm])
        deltas[nm] = sm_d[i].reshape(shapes[nm])
        new_m[nm] = sm_nm[i].reshape(shapes[nm])
        new_v[nm] = sm_nv[i].reshape(shapes[nm])
    order = ["w_ada", "b_ada", "norm1_g", "w_in", "conv_w", "conv_b", "lru_wa", "lru_ba", "lru_wx", "lru_bx",
             "lru_lambda", "w_lru_out", "w_attn_out", "attn_sinks", "rel_bias", "w_out", "norm2_g", "w_ff1", "w_ff2",
             "final_g"]
    return (loss, grad_x[None], *[grads[n] for n in order], *[deltas[n] for n in order],
            *[new_m[n] for n in order], *[new_v[n] for n in order])
```

```python
import math

import numpy as np
import jax
import jax.numpy as jnp
from jax import lax
from jax.experimental import pallas as pl
from jax.experimental.pallas import tpu as pltpu

F32 = jnp.float32
BF16 = jnp.bfloat16
SDS = jax.ShapeDtypeStruct
MESH = pl.DeviceIdType.MESH

D = 2048
D_FF = 4 * D
N_HEADS = 32
HEAD_DIM = 64
BLOCK = 128
N_LRU_BLOCKS = 16
LRU_C = 8.0
EPS = 1e-6
NEG_INF = -1e30
N_BUCKETS = 32
MAX_DISTANCE = 128
IN_W = 10752
IN_SHARD = IN_W // 4
IN_TILE = 896
ADA_SHARD = 6 * D // 4
OFF_LRU, OFF_GATE, OFF_Q, OFF_K, OFF_V, OFF_GA, OFF_GB = 0, 2048, 4096, 6144, 6400, 6656, 8704
SCALE = HEAD_DIM ** -0.5
N_CHIPS = 4
N_DEV = 8

ADAM_LR, ADAM_B1, ADAM_B2, ADAM_EPS, ADAM_WD, ADAM_STEP = 0.001, 0.9, 0.999, 1e-08, 0.01, 10
ADAM_C1 = 1.0 - ADAM_B1 ** ADAM_STEP
ADAM_C2 = 1.0 - ADAM_B2 ** ADAM_STEP

VMEM_LIMIT = 52 * 2 ** 20
SUB = 128
WG_TM = 1024
V_G1, V_SCALE1, V_SHIFT1, V_GATE1, V_G2, V_SCALE2, V_SHIFT2, V_GATE2, V_G3 = range(9)
L_CW0, L_CB, L_BA, L_BX, L_LAM = 0, 4, 5, 6, 7
P_VEC, P_CONVW, P_MOD, P_ATT, P_WA, P_WX, P_ROWS = 0, 8, 16, 24, 32, 160, 288


def _cp(n_axes):
    return pltpu.CompilerParams(dimension_semantics=("arbitrary",) * n_axes, vmem_limit_bytes=VMEM_LIMIT)


def _dot(a, b):
    return jnp.dot(a, b, preferred_element_type=F32)


def _dot_nt(a, b):
    return lax.dot_general(a, b, (((1,), (1,)), ((), ())), preferred_element_type=F32)


def _dot_tn(a, b):
    return lax.dot_general(a, b, (((0,), (0,)), ((), ())), preferred_element_type=F32)


_G0 = math.sqrt(2.0 / math.pi)
_G1 = 0.044715


def _gelu(x):
    return 0.5 * x * (1.0 + jnp.tanh(_G0 * (x + _G1 * x * x * x)))


def _gelu_grad(x):
    x2 = x * x
    t = jnp.tanh(_G0 * (x + _G1 * x * x2))
    return 0.5 * (1.0 + t) + 0.5 * x * (1.0 - t * t) * _G0 * (1.0 + 3.0 * _G1 * x2)


def _expm1(x):
    u = jnp.exp(x)
    um1 = u - 1.0
    k = um1 * x / jnp.log(u)
    return jnp.where(um1 == 0.0, x, jnp.where(u < 0.5, um1, k))


def _softplus(z):
    e = jnp.exp(-jnp.abs(z))
    u = 1.0 + e
    l1p = jnp.where(u == 1.0, e, jnp.log(u) * e / (u - 1.0))
    return jnp.maximum(z, 0.0) + l1p


def _adamw_math(w, g, m, v):
    m2 = ADAM_B1 * m + (1.0 - ADAM_B1) * g
    v2 = ADAM_B2 * v + (1.0 - ADAM_B2) * (g * g)
    m_hat = m2 / ADAM_C1
    v_hat = v2 / ADAM_C2
    delta = -ADAM_LR * (m_hat / (jnp.sqrt(v_hat) + ADAM_EPS) + ADAM_WD * w)
    return delta, m2, v2


def _rms_parts(xv):
    r = lax.rsqrt(jnp.mean(xv * xv, axis=-1, keepdims=True) + EPS)
    return r, xv * r


def inproj_fwd(x, vecs, w_in):
    s = x.shape[0]
    tm = min(512, s)
    per = IN_SHARD // IN_TILE

    def body(x_ref, v_ref, w_ref, proj_ref, h_ref):
        @pl.when(pl.program_id(1) == 0)
        def _():
            _, xh = _rms_parts(x_ref[...])
            h = (xh * v_ref[V_G1:V_G1 + 1, :]) * (1.0 + v_ref[V_SCALE1:V_SCALE1 + 1, :]) + v_ref[V_SHIFT1:V_SHIFT1 + 1, :]
            h_ref[...] = h.astype(BF16)
        proj_ref[...] = _dot(h_ref[...], w_ref[...]).astype(BF16)

    return pl.pallas_call(
        body, name="inproj_fwd", grid=(s // tm, IN_W // IN_TILE),
        in_specs=[pl.BlockSpec((tm, D), lambda i, j: (i, 0)),
                  pl.BlockSpec((16, D), lambda i, j: (0, 0)),
                  pl.BlockSpec((None, D, IN_TILE), lambda i, j: (j // per, 0, j % per))],
        out_specs=[pl.BlockSpec((tm, IN_TILE), lambda i, j: (i, j)),
                   pl.BlockSpec((tm, D), lambda i, j: (i, 0))],
        out_shape=[SDS((s, IN_W), BF16), SDS((s, D), BF16)],
        compiler_params=_cp(2))(x, vecs, w_in)


def _lru_block_fwd(xbuf, lv_ref, wa_ref, wx_ref, b, t, first):
    cs = slice(b * 128, (b + 1) * 128)
    x0 = xbuf[pl.ds(8, t), cs]
    x1 = xbuf[pl.ds(7, t), cs]
    x2 = xbuf[pl.ds(6, t), cs]
    x3 = xbuf[pl.ds(5, t), cs]
    xc = (lv_ref[L_CB:L_CB + 1, cs] + lv_ref[3:4, cs] * x0 + lv_ref[2:3, cs] * x1
          + lv_ref[1:2, cs] * x2 + lv_ref[0:1, cs] * x3)
    xcb = xc.astype(BF16)
    r = jax.nn.sigmoid(_dot(xcb, wa_ref[b]) + lv_ref[L_BA:L_BA + 1, cs])
    ig = jax.nn.sigmoid(_dot(xcb, wx_ref[b]) + lv_ref[L_BX:L_BX + 1, cs])
    sp = _softplus(-lv_ref[L_LAM:L_LAM + 1, cs])
    log_a = (-LRU_C) * r * sp
    a = jnp.exp(log_a)
    mult = jnp.where(first, 1.0, jnp.sqrt(-_expm1(2.0 * log_a)))
    return (x0, x1, x2, x3), xc, xcb, r, ig, sp, a, mult


def lru_fwd(proj, lvec, wa, wx):
    s = proj.shape[0]
    t = min(256, s)

    def body(lx_ref, gate_ref, lv_ref, wa_ref, wx_ref, ya_ref, rec_ref, xbuf, a_s, u_s, hc):
        i = pl.program_id(0)

        @pl.when(i == 0)
        def _():
            xbuf[pl.ds(0, 8), :] = jnp.zeros((8, D), F32)
            hc[...] = jnp.zeros((8, D), F32)

        @pl.when(i > 0)
        def _():
            xbuf[pl.ds(0, 8), :] = xbuf[pl.ds(t, 8), :]

        xbuf[pl.ds(8, t), :] = lx_ref[...].astype(F32)
        first = (lax.broadcasted_iota(jnp.int32, (t, 128), 0) + i * t) == 0
        for b in range(N_LRU_BLOCKS):
            cs = slice(b * 128, (b + 1) * 128)
            _, xc, _, _, ig, _, a, mult = _lru_block_fwd(xbuf, lv_ref, wa_ref, wx_ref, b, t, first)
            a_s[:, cs] = a
            u_s[:, cs] = mult * (ig * xc)

        def step(tt, h):
            h = a_s[pl.ds(tt, 1), :] * h + u_s[pl.ds(tt, 1), :]
            rec_ref[pl.ds(tt, 1), :] = h
            return h

        hc[0:1, :] = lax.fori_loop(0, t, step, hc[0:1, :], unroll=8)
        for b in range(N_LRU_BLOCKS):
            cs = slice(b * 128, (b + 1) * 128)
            ya_ref[:, cs] = (rec_ref[:, cs] * _gelu(gate_ref[:, cs].astype(F32))).astype(BF16)

    return pl.pallas_call(
        body, name="lru_fwd", grid=(s // t,),
        in_specs=[pl.BlockSpec((t, D), lambda i: (i, OFF_LRU // D)),
                  pl.BlockSpec((t, D), lambda i: (i, OFF_GATE // D)),
                  pl.BlockSpec((8, D), lambda i: (0, 0)),
                  pl.BlockSpec((N_LRU_BLOCKS, 128, 128), lambda i: (0, 0, 0)),
                  pl.BlockSpec((N_LRU_BLOCKS, 128, 128), lambda i: (0, 0, 0))],
        out_specs=[pl.BlockSpec((t, D), lambda i: (i, 0)), pl.BlockSpec((t, D), lambda i: (i, 0))],
        out_shape=[SDS((s, D), BF16), SDS((s, D), F32)],
        scratch_shapes=[pltpu.VMEM((t + 8, D), F32), pltpu.VMEM((t, D), F32), pltpu.VMEM((t, D), F32),
                        pltpu.VMEM((8, D), F32)],
        compiler_params=_cp(1))(proj, proj, lvec, wa, wx)


def t5_bucket_table():
    qi = np.arange(BLOCK)[:, None]
    ki = np.arange(2 * BLOCK)[None, :]
    rel = qi + BLOCK - ki
    relc = np.maximum(rel, 0)
    max_exact = N_BUCKETS // 2
    relf = np.maximum(relc, 1).astype(np.float32)
    large = max_exact + (np.log(relf / np.float32(max_exact)) / np.float32(math.log(MAX_DISTANCE / max_exact))
                         * np.float32(N_BUCKETS - max_exact)).astype(np.int32)
    large = np.minimum(large, N_BUCKETS - 1)
    bucket = np.where(relc < max_exact, relc, large)
    bucket = np.where((rel >= 0) & (rel < BLOCK), bucket, -1)
    return jnp.asarray(bucket.reshape(1, BLOCK * 2 * BLOCK), jnp.int32)


def bias_band(rel_bias_t, buckets):
    n = BLOCK * 2 * BLOCK
    tn = 4096

    def body(bk_ref, rb_ref, o_ref):
        row = lax.broadcasted_iota(jnp.int32, (N_BUCKETS, tn), 0)
        oh = jnp.where(row == bk_ref[...], 1.0, 0.0).astype(BF16)
        rb = rb_ref[...]
        p0 = rb.astype(BF16)
        r1 = rb - p0.astype(F32)
        p1 = r1.astype(BF16)
        p2 = (r1 - p1.astype(F32)).astype(BF16)
        o_ref[...] = _dot(p0, oh) + _dot(p1, oh) + _dot(p2, oh)

    return pl.pallas_call(
        body, name="bias_band", grid=(n // tn,),
        in_specs=[pl.BlockSpec((1, tn), lambda i: (0, i)), pl.BlockSpec((N_HEADS, N_BUCKETS), lambda i: (0, 0))],
        out_specs=pl.BlockSpec((N_HEADS, tn), lambda i: (0, i)),
        out_shape=SDS((N_HEADS, n), F32), compiler_params=_cp(1))(buckets, rel_bias_t)


def bias_band_bwd(dband, buckets):
    n = BLOCK * 2 * BLOCK
    tn = 4096

    def body(bk_ref, d_ref, o_ref):
        @pl.when(pl.program_id(0) == 0)
        def _():
            o_ref[...] = jnp.zeros_like(o_ref)
        row = lax.broadcasted_iota(jnp.int32, (N_BUCKETS, tn), 0)
        oh = jnp.where(row == bk_ref[...], 1.0, 0.0).astype(BF16)
        dv = d_ref[...]
        p0 = dv.astype(BF16)
        r1 = dv - p0.astype(F32)
        p1 = r1.astype(BF16)
        p2 = (r1 - p1.astype(F32)).astype(BF16)
        o_ref[...] += _dot_nt(oh, p0) + _dot_nt(oh, p1) + _dot_nt(oh, p2)

    return pl.pallas_call(
        body, name="bias_band_bwd", grid=(n // tn,),
        in_specs=[pl.BlockSpec((1, tn), lambda i: (0, i)), pl.BlockSpec((N_HEADS, tn), lambda i: (0, i))],
        out_specs=pl.BlockSpec((N_BUCKETS, N_HEADS), lambda i: (0, 0)),
        out_shape=SDS((N_BUCKETS, N_HEADS), F32), compiler_params=_cp(1))(buckets, dband)


def _dup_half(band, which):
    lane = lax.broadcasted_iota(jnp.int32, band.shape, 1)
    rolled = pltpu.roll(band, 64, 1)
    keep = (lane < 64) if which == 0 else (lane >= 64)
    return jnp.where(keep, band, rolled)


def _attn_probs(qm, kk, bias, sink, valid):
    sc = _dot_nt(qm, kk) * SCALE + bias
    sc = jnp.where(valid, sc, NEG_INF)
    m = jnp.maximum(jnp.max(sc, axis=-1, keepdims=True), sink)
    e = jnp.exp(sc - m)
    es = jnp.exp(sink - m)
    inv = 1.0 / (jnp.sum(e, axis=-1, keepdims=True) + es)
    return e * inv, es * inv


def _band_valid(n):
    qi = lax.broadcasted_iota(jnp.int32, (BLOCK, 2 * BLOCK), 0)
    ki = lax.broadcasted_iota(jnp.int32, (BLOCK, 2 * BLOCK), 1)
    rel = qi + BLOCK - ki
    return (rel >= 0) & (rel < BLOCK) & ((ki >= BLOCK) | (n > 0))


def _kv_bands(prev_ref, cur_ref):
    band = jnp.concatenate([prev_ref[...].astype(F32), cur_ref[...].astype(F32)], axis=0)
    return [_dup_half(band, 0).astype(BF16), _dup_half(band, 1).astype(BF16)]


def attn_fwd(proj, band, sinks):
    s = proj.shape[0]
    nb = s // BLOCK
    qw = 1024

    def body(sk_ref, q_ref, kp_ref, kc_ref, vp_ref, vc_ref, b_ref, o_ref):
        n = pl.program_id(0)
        gp = pl.program_id(1)
        valid = _band_valid(n)
        kks = _kv_bands(kp_ref, kc_ref)
        vvs = _kv_bands(vp_ref, vc_ref)
        lane = lax.broadcasted_iota(jnp.int32, (BLOCK, 128), 1)
        for j in range(8):
            qs = q_ref[:, j * 128:(j + 1) * 128]
            outs = []
            for hh in range(2):
                hl = 2 * j + hh
                qm = jnp.where((lane < 64) if hh == 0 else (lane >= 64), qs, jnp.zeros_like(qs))
                p, _ = _attn_probs(qm, kks[j // 4], b_ref[hl], sk_ref[gp * 16 + hl], valid)
                outs.append(_dot(p.astype(BF16), vvs[j // 4]))
            o_ref[:, j * 128:(j + 1) * 128] = jnp.where(lane < 64, outs[0], outs[1]).astype(BF16)

    kb, vb = OFF_K // 128, OFF_V // 128
    return pl.pallas_call(
        body, name="attn_fwd", grid=(nb, 2),
        in_specs=[pl.BlockSpec(memory_space=pltpu.SMEM),
                  pl.BlockSpec((BLOCK, qw), lambda n, g: (n, OFF_Q // qw + g)),
                  pl.BlockSpec((BLOCK, 128), lambda n, g: (jnp.maximum(n - 1, 0), kb + g)),
                  pl.BlockSpec((BLOCK, 128), lambda n, g: (n, kb + g)),
                  pl.BlockSpec((BLOCK, 128), lambda n, g: (jnp.maximum(n - 1, 0), vb + g)),
                  pl.BlockSpec((BLOCK, 128), lambda n, g: (n, vb + g)),
                  pl.BlockSpec((16, BLOCK, 2 * BLOCK), lambda n, g: (g, 0, 0))],
        out_specs=pl.BlockSpec((BLOCK, qw), lambda n, g: (n, g)),
        out_shape=SDS((s, D), BF16), compiler_params=_cp(2))(sinks, proj, proj, proj, proj, proj, band)


def merge_fwd(ya, att, w_lru_out, w_attn_out, proj):
    s = ya.shape[0]
    tm, tn = min(512, s), 512

    def body(ya_ref, at_ref, wl_ref, wt_ref, ga_ref, gb_ref, yab_ref, mg_ref):
        y_a = _dot(ya_ref[...], wl_ref[...])
        y_b = _dot(at_ref[...], wt_ref[...])
        yab_ref[0] = y_a.astype(BF16)
        yab_ref[1] = y_b.astype(BF16)
        mg_ref[...] = (jax.nn.sigmoid(ga_ref[...].astype(F32)) * y_a
                       + jax.nn.sigmoid(gb_ref[...].astype(F32)) * y_b).astype(BF16)

    return pl.pallas_call(
        body, name="merge_fwd", grid=(s // tm, D // tn),
        in_specs=[pl.BlockSpec((tm, D), lambda i, j: (i, 0)), pl.BlockSpec((tm, D), lambda i, j: (i, 0)),
                  pl.BlockSpec((D, tn), lambda i, j: (0, j)), pl.BlockSpec((D, tn), lambda i, j: (0, j)),
                  pl.BlockSpec((tm, tn), lambda i, j: (i, OFF_GA // tn + j)),
                  pl.BlockSpec((tm, tn), lambda i, j: (i, OFF_GB // tn + j))],
        out_specs=[pl.BlockSpec((2, tm, tn), lambda i, j: (0, i, j)), pl.BlockSpec((tm, tn), lambda i, j: (i, j))],
        out_shape=[SDS((2, s, D), BF16), SDS((s, D), BF16)],
        compiler_params=_cp(2))(ya, att, w_lru_out, w_attn_out, proj, proj)


def outproj_fwd(merged, w_out, x, vecs):
    s = x.shape[0]
    tm, tn = min(512, s), 512

    def body(m_ref, w_ref, x_ref, v_ref, x1_ref, o1_ref):
        o1 = _dot(m_ref[...], w_ref[...])
        o1_ref[...] = o1.astype(BF16)
        x1_ref[...] = x_ref[...] + v_ref[V_GATE1:V_GATE1 + 1, :] * o1

    return pl.pallas_call(
        body, name="outproj_fwd", grid=(s // tm, D // tn),
        in_specs=[pl.BlockSpec((tm, D), lambda i, j: (i, 0)), pl.BlockSpec((D, tn), lambda i, j: (0, j)),
                  pl.BlockSpec((tm, tn), lambda i, j: (i, j)), pl.BlockSpec((16, tn), lambda i, j: (0, j))],
        out_specs=[pl.BlockSpec((tm, tn), lambda i, j: (i, j)), pl.BlockSpec((tm, tn), lambda i, j: (i, j))],
        out_shape=[SDS((s, D), F32), SDS((s, D), BF16)],
        compiler_params=_cp(2))(merged, w_out, x, vecs)


def ff1_fwd(x1, vecs, w_ff1):
    s = x1.shape[0]
    tm, tn = min(512, s), 512
    per = D // tn

    def body(x_ref, v_ref, w_ref, f_ref, h_ref):
        @pl.when(pl.program_id(1) == 0)
        def _():
            _, xh = _rms_parts(x_ref[...])
            h = (xh * v_ref[V_G2:V_G2 + 1, :]) * (1.0 + v_ref[V_SCALE2:V_SCALE2 + 1, :]) + v_ref[V_SHIFT2:V_SHIFT2 + 1, :]
            h_ref[...] = h.astype(BF16)
        f_ref[...] = _dot(h_ref[...], w_ref[...]).astype(BF16)

    return pl.pallas_call(
        body, name="ff1_fwd", grid=(s // tm, D_FF // tn),
        in_specs=[pl.BlockSpec((tm, D), lambda i, j: (i, 0)), pl.BlockSpec((16, D), lambda i, j: (0, 0)),
                  pl.BlockSpec((None, D, tn), lambda i, j: (j // per, 0, j % per))],
        out_specs=[pl.BlockSpec((tm, tn), lambda i, j: (i, j)), pl.BlockSpec((tm, D), lambda i, j: (i, 0))],
        out_shape=[SDS((s, D_FF), BF16), SDS((s, D), BF16)],
        compiler_params=_cp(2))(x1, vecs, w_ff1)


def ff2_loss(f, w_ff2, x1, tgt, vecs):
    s = x1.shape[0]
    tm, tk = min(512, s), 512
    nk = D_FF // tk

    def body(f_ref, w_ref, x1_ref, t_ref, v_ref, dx2_ref, do2_ref, sums_ref, loss_ref, acc):
        i, k = pl.program_id(0), pl.program_id(1)

        @pl.when((i == 0) & (k == 0))
        def _():
            sums_ref[...] = jnp.zeros_like(sums_ref)
            loss_ref[...] = jnp.zeros_like(loss_ref)

        @pl.when(k == 0)
        def _():
            acc[...] = jnp.zeros_like(acc)

        fv = jnp.maximum(f_ref[...].astype(F32), 0.0)
        acc[...] += _dot((fv * fv).astype(BF16), w_ref[...])

        @pl.when(k == nk - 1)
        def _():
            gate2 = v_ref[V_GATE2:V_GATE2 + 1, :]
            g3 = v_ref[V_G3:V_G3 + 1, :]

            def sub(rb, carry):
                rs = pl.ds(pl.multiple_of(rb * SUB, SUB), SUB)
                o2 = acc[rs, :]
                x2 = x1_ref[rs, :] + gate2 * o2
                r3, xh = _rms_parts(x2)
                e = xh * g3 - t_ref[rs, :]
                loss_ref[...] += (0.5 / D) * jnp.sum(e * e)
                dy = e * (1.0 / D)
                sums_ref[0:1, :] += jnp.sum(dy * xh, axis=0, keepdims=True)
                dxh = dy * g3
                dx2 = r3 * (dxh - xh * jnp.mean(dxh * xh, axis=-1, keepdims=True))
                sums_ref[1:2, :] += jnp.sum(dx2 * o2, axis=0, keepdims=True)
                dx2_ref[rs, :] = dx2
                do2_ref[rs, :] = (dx2 * gate2).astype(BF16)
                return carry

            lax.fori_loop(0, tm // SUB, sub, 0)

    return pl.pallas_call(
        body, name="ff2_loss", grid=(s // tm, nk),
        in_specs=[pl.BlockSpec((tm, tk), lambda i, k: (i, k)), pl.BlockSpec((tk, D), lambda i, k: (k, 0)),
                  pl.BlockSpec((tm, D), lambda i, k: (i, 0)), pl.BlockSpec((tm, D), lambda i, k: (i, 0)),
                  pl.BlockSpec((16, D), lambda i, k: (0, 0))],
        out_specs=[pl.BlockSpec((tm, D), lambda i, k: (i, 0)), pl.BlockSpec((tm, D), lambda i, k: (i, 0)),
                   pl.BlockSpec((8, D), lambda i, k: (0, 0)), pl.BlockSpec((8, 128), lambda i, k: (0, 0))],
        out_shape=[SDS((s, D), F32), SDS((s, D), BF16), SDS((8, D), F32), SDS((8, 128), F32)],
        scratch_shapes=[pltpu.VMEM((tm, D), F32)],
        compiler_params=_cp(2))(f, w_ff2, x1, tgt, vecs)


def ff2_bwd(do2, w_ff2, f):
    s = do2.shape[0]
    tm, tn = min(512, s), 512

    def body(d_ref, w_ref, f_ref, o_ref):
        dff = _dot_nt(d_ref[...], w_ref[...])
        o_ref[...] = (dff * (2.0 * jnp.maximum(f_ref[...].astype(F32), 0.0))).astype(BF16)

    return pl.pallas_call(
        body, name="ff2_bwd", grid=(s // tm, D_FF // tn),
        in_specs=[pl.BlockSpec((tm, D), lambda i, j: (i, 0)), pl.BlockSpec((tn, D), lambda i, j: (j, 0)),
                  pl.BlockSpec((tm, tn), lambda i, j: (i, j))],
        out_specs=pl.BlockSpec((tm, tn), lambda i, j: (i, j)),
        out_shape=SDS((s, D_FF), BF16), compiler_params=_cp(2))(do2, w_ff2, f)


def weight_grad(name, a, b, tn, out_shape, out_block, out_map, relu2=False):
    s, m = a.shape
    n = b.shape[1]
    tm = WG_TM
    chunk = min(1024, s)
    nch = s // chunk

    def body(a_hbm, b_ref, o_ref, a_buf, at_s, sem):
        i = pl.program_id(0)

        @pl.when(pl.program_id(1) == 0)
        def _():
            def fetch(ch):
                return pltpu.make_async_copy(a_hbm.at[pl.ds(ch * chunk, chunk), pl.ds(i * tm, tm)],
                                             a_buf.at[ch % 2], sem.at[ch % 2])
            fetch(0).start()
            for ch in range(nch):
                if ch + 1 < nch:
                    fetch(ch + 1).start()
                fetch(ch).wait()
                av = a_buf[ch % 2]
                if relu2:
                    fv = jnp.maximum(av.astype(F32), 0.0)
                    av = (fv * fv).astype(BF16)
                at_s[:, ch * chunk:(ch + 1) * chunk] = av.T

        o_ref[...] = _dot(at_s[...], b_ref[...])

    return pl.pallas_call(
        body, name=name, grid=(m // tm, n // tn),
        in_specs=[pl.BlockSpec(memory_space=pl.ANY), pl.BlockSpec((s, tn), lambda i, j: (0, j))],
        out_specs=pl.BlockSpec(out_block, lambda i, j: out_map(i, j)),
        out_shape=SDS(out_shape, F32),
        scratch_shapes=[pltpu.VMEM((2, chunk, tm), BF16), pltpu.VMEM((tm, s), BF16), pltpu.SemaphoreType.DMA((2,))],
        compiler_params=_cp(2))(a, b)


def ff1_bwd(df, w_ff1, x1, dx2, o1, vecs):
    s = df.shape[0]
    tm, tk = min(512, s), 512
    nk = D_FF // tk
    per = D // tk

    def body(d_ref, w_ref, x1_ref, dx2_ref, o1_ref, v_ref, dx1_ref, do1_ref, sums_ref, acc):
        i, k = pl.program_id(0), pl.program_id(1)

        @pl.when((i == 0) & (k == 0))
        def _():
            sums_ref[...] = jnp.zeros_like(sums_ref)

        @pl.when(k == 0)
        def _():
            acc[...] = jnp.zeros_like(acc)

        acc[...] += _dot_nt(d_ref[...], w_ref[...])

        @pl.when(k == nk - 1)
        def _():
            g2 = v_ref[V_G2:V_G2 + 1, :]
            scale2 = v_ref[V_SCALE2:V_SCALE2 + 1, :]
            gate1 = v_ref[V_GATE1:V_GATE1 + 1, :]

            def sub(rb, carry):
                rs = pl.ds(pl.multiple_of(rb * SUB, SUB), SUB)
                dh = acc[rs, :]
                r2, xh = _rms_parts(x1_ref[rs, :])
                sums_ref[0:1, :] += jnp.sum(dh, axis=0, keepdims=True)
                sums_ref[1:2, :] += jnp.sum(dh * (xh * g2), axis=0, keepdims=True)
                dxn = dh * (1.0 + scale2)
                sums_ref[2:3, :] += jnp.sum(dxn * xh, axis=0, keepdims=True)
                dxh = dxn * g2
                dx1 = dx2_ref[rs, :] + r2 * (dxh - xh * jnp.mean(dxh * xh, axis=-1, keepdims=True))
                sums_ref[3:4, :] += jnp.sum(dx1 * o1_ref[rs, :].astype(F32), axis=0, keepdims=True)
                dx1_ref[rs, :] = dx1
                do1_ref[rs, :] = (dx1 * gate1).astype(BF16)
                return carry

            lax.fori_loop(0, tm // SUB, sub, 0)

    return pl.pallas_call(
        body, name="ff1_bwd", grid=(s // tm, nk),
        in_specs=[pl.BlockSpec((tm, tk), lambda i, k: (i, k)),
                  pl.BlockSpec((None, D, tk), lambda i, k: (k // per, 0, k % per)),
                  pl.BlockSpec((tm, D), lambda i, k: (i, 0)), pl.BlockSpec((tm, D), lambda i, k: (i, 0)),
                  pl.BlockSpec((tm, D), lambda i, k: (i, 0)), pl.BlockSpec((16, D), lambda i, k: (0, 0))],
        out_specs=[pl.BlockSpec((tm, D), lambda i, k: (i, 0)), pl.BlockSpec((tm, D), lambda i, k: (i, 0)),
                   pl.BlockSpec((8, D), lambda i, k: (0, 0))],
        out_shape=[SDS((s, D), F32), SDS((s, D), BF16), SDS((8, D), F32)],
        scratch_shapes=[pltpu.VMEM((tm, D), F32)],
        compiler_params=_cp(2))(df, w_ff1, x1, dx2, o1, vecs)


def outproj_bwd(do1, w_out, yab, proj):
    s = do1.shape[0]
    tm, tn = min(512, s), 512
    per = D // tn

    def body(d_ref, w_ref, y_ref, g_ref, dy_ref, dp_ref):
        dm = _dot_nt(d_ref[...], w_ref[...])
        sg = jax.nn.sigmoid(g_ref[...].astype(F32))
        dy_ref[...] = (dm * sg).astype(BF16)
        dp_ref[...] = (dm * y_ref[...].astype(F32) * sg * (1.0 - sg)).astype(BF16)

    return pl.pallas_call(
        body, name="outproj_bwd", grid=(s // tm, 2 * per),
        in_specs=[pl.BlockSpec((tm, D), lambda i, j: (i, 0)), pl.BlockSpec((tn, D), lambda i, j: (j % per, 0)),
                  pl.BlockSpec((None, tm, tn), lambda i, j: (j // per, i, j % per)),
                  pl.BlockSpec((tm, tn), lambda i, j: (i, OFF_GA // tn + j))],
        out_specs=[pl.BlockSpec((None, tm, tn), lambda i, j: (j // per, i, j % per)),
                   pl.BlockSpec((tm, tn), lambda i, j: (i, OFF_GA // tn + j))],
        out_shape=[SDS((2, s, D), BF16), SDS((s, IN_W), BF16)],
        compiler_params=_cp(2))(do1, w_out, yab, proj)


def lruout_bwd(dyab, w_lru_out, rec, proj, dproj):
    s = rec.shape[0]
    tm, tn = min(512, s), 512

    def body(d_ref, w_ref, r_ref, g_ref, dp_in, dr_ref, dp_ref):
        dya = _dot_nt(d_ref[...], w_ref[...])
        gate = g_ref[...].astype(F32)
        dr_ref[...] = dya * _gelu(gate)
        dp_ref[...] = (dya * r_ref[...] * _gelu_grad(gate)).astype(BF16)

    return pl.pallas_call(
        body, name="lruout_bwd", grid=(s // tm, D // tn),
        in_specs=[pl.BlockSpec((None, tm, D), lambda i, j: (0, i, 0)), pl.BlockSpec((tn, D), lambda i, j: (j, 0)),
                  pl.BlockSpec((tm, tn), lambda i, j: (i, j)),
                  pl.BlockSpec((tm, tn), lambda i, j: (i, OFF_GATE // tn + j)),
                  pl.BlockSpec(memory_space=pl.ANY)],
        out_specs=[pl.BlockSpec((tm, tn), lambda i, j: (i, j)),
                   pl.BlockSpec((tm, tn), lambda i, j: (i, OFF_GATE // tn + j))],
        out_shape=[SDS((s, D), F32), SDS((s, IN_W), BF16)],
        input_output_aliases={4: 1},
        compiler_params=_cp(2))(dyab, w_lru_out, rec, proj, dproj)


def attnout_bwd(dyab, w_attn_out):
    s = dyab.shape[1]
    tm, tn = min(512, s), 512

    def body(d_ref, w_ref, o_ref):
        o_ref[...] = _dot_nt(d_ref[...], w_ref[...]).astype(BF16)

    return pl.pallas_call(
        body, name="attnout_bwd", grid=(s // tm, D // tn),
        in_specs=[pl.BlockSpec((None, tm, D), lambda i, j: (1, i, 0)), pl.BlockSpec((tn, D), lambda i, j: (j, 0))],
        out_specs=pl.BlockSpec((tm, tn), lambda i, j: (i, j)),
        out_shape=SDS((s, D), BF16), compiler_params=_cp(2))(dyab, w_attn_out)


def attn_bwd(proj, band, sinks, datt, dproj):
    s = proj.shape[0]
    nb = s // BLOCK
    qw = 1024

    def body(sk_ref, q_ref, kp_ref, kc_ref, vp_ref, vc_ref, b_ref, do_ref, dp_in,
             dq_ref, dkb_ref, dvb_ref, db_ref, ds_ref):
        gp = pl.program_id(0)
        n = pl.program_id(1)

        @pl.when(n == 0)
        def _():
            db_ref[...] = jnp.zeros_like(db_ref)
            ds_ref[...] = jnp.zeros_like(ds_ref)

        valid = _band_valid(n)
        kks = _kv_bands(kp_ref, kc_ref)
        vvs = _kv_bands(vp_ref, vc_ref)
        lane = lax.broadcasted_iota(jnp.int32, (BLOCK, 128), 1)
        lane_b = lax.broadcasted_iota(jnp.int32, (2 * BLOCK, 128), 1)
        dk_acc = [jnp.zeros((2 * BLOCK, 128), F32), jnp.zeros((2 * BLOCK, 128), F32)]
        dv_acc = [jnp.zeros((2 * BLOCK, 128), F32), jnp.zeros((2 * BLOCK, 128), F32)]
        for j in range(8):
            kv = j // 4
            qs = q_ref[:, j * 128:(j + 1) * 128]
            dos = do_ref[:, j * 128:(j + 1) * 128]
            dqs = []
            for hh in range(2):
                hl = 2 * j + hh
                half = (lane < 64) if hh == 0 else (lane >= 64)
                qm = jnp.where(half, qs, jnp.zeros_like(qs))
                dom = jnp.where(half, dos, jnp.zeros_like(dos))
                p, ps = _attn_probs(qm, kks[kv], b_ref[hl], sk_ref[gp * 16 + hl], valid)
                dp = _dot_nt(dom, vvs[kv])
                delta = jnp.sum(p * dp, axis=-1, keepdims=True)
                dsc = p * (dp - delta)
                db_ref[hl] += dsc
                ds_ref[pl.ds(hl, 1), :] += jnp.zeros((1, 128), F32) - jnp.sum(ps * delta)
                dsb = (dsc * SCALE).astype(BF16)
                dqs.append(_dot(dsb, kks[kv]))
                dk_acc[kv] = dk_acc[kv] + _dot_tn(dsb, qm)
                dv_acc[kv] = dv_acc[kv] + _dot_tn(p.astype(BF16), dom)
            dq_ref[:, j * 128:(j + 1) * 128] = jnp.where(lane < 64, dqs[0], dqs[1]).astype(BF16)
        for acc, ref in ((dk_acc, dkb_ref), (dv_acc, dvb_ref)):
            d0 = acc[0] + pltpu.roll(acc[0], 64, 1)
            d1 = acc[1] + pltpu.roll(acc[1], 64, 1)
            ref[...] = jnp.where(lane_b < 64, d0, d1)

    kb, vb = OFF_K // 128, OFF_V // 128
    return pl.pallas_call(
        body, name="attn_bwd", grid=(2, nb),
        in_specs=[pl.BlockSpec(memory_space=pltpu.SMEM),
                  pl.BlockSpec((BLOCK, qw), lambda g, n: (n, OFF_Q // qw + g)),
                  pl.BlockSpec((BLOCK, 128), lambda g, n: (jnp.maximum(n - 1, 0), kb + g)),
                  pl.BlockSpec((BLOCK, 128), lambda g, n: (n, kb + g)),
                  pl.BlockSpec((BLOCK, 128), lambda g, n: (jnp.maximum(n - 1, 0), vb + g)),
                  pl.BlockSpec((BLOCK, 128), lambda g, n: (n, vb + g)),
                  pl.BlockSpec((16, BLOCK, 2 * BLOCK), lambda g, n: (g, 0, 0)),
                  pl.BlockSpec((BLOCK, qw), lambda g, n: (n, g)),
                  pl.BlockSpec(memory_space=pl.ANY)],
        out_specs=[pl.BlockSpec((BLOCK, qw), lambda g, n: (n, OFF_Q // qw + g)),
                   pl.BlockSpec((2 * BLOCK, 128), lambda g, n: (n, g)),
                   pl.BlockSpec((2 * BLOCK, 128), lambda g, n: (n, g)),
                   pl.BlockSpec((16, BLOCK, 2 * BLOCK), lambda g, n: (g, 0, 0)),
                   pl.BlockSpec((16, 128), lambda g, n: (g, 0))],
        out_shape=[SDS((s, IN_W), BF16), SDS((nb * 2 * BLOCK, 256), F32), SDS((nb * 2 * BLOCK, 256), F32),
                   SDS((N_HEADS, BLOCK, 2 * BLOCK), F32), SDS((N_HEADS, 128), F32)],
        input_output_aliases={8: 0},
        compiler_params=_cp(2))(sinks, proj, proj, proj, proj, proj, band, datt, dproj)


def dkv_combine(dkb, dvb, dproj):
    nb = dkb.shape[0] // (2 * BLOCK)
    s = nb * BLOCK
    dkb3 = dkb.reshape(nb, 2 * BLOCK, 256)
    dvb3 = dvb.reshape(nb, 2 * BLOCK, 256)

    def body(k1, k2, v1, v2, dp_in, o_ref):
        nxt = jnp.where(pl.program_id(0) < nb - 1, 1.0, 0.0)
        o_ref[:, 0:256] = (k1[...] + nxt * k2[...]).astype(BF16)
        o_ref[:, 256:512] = (v1[...] + nxt * v2[...]).astype(BF16)

    spec1 = pl.BlockSpec((None, BLOCK, 256), lambda m: (m, 1, 0))
    spec2 = pl.BlockSpec((None, BLOCK, 256), lambda m: (jnp.minimum(m + 1, nb - 1), 0, 0))
    return pl.pallas_call(
        body, name="dkv_combine", grid=(nb,),
        in_specs=[spec1, spec2, spec1, spec2, pl.BlockSpec(memory_space=pl.ANY)],
        out_specs=pl.BlockSpec((BLOCK, 512), lambda m: (m, OFF_K // 512)),
        out_shape=SDS((s, IN_W), BF16), input_output_aliases={4: 0},
        compiler_params=_cp(1))(dkb3, dkb3, dvb3, dvb3, dproj)


def lru_bwd(proj, rec, drec, lvec, wa, wx, dproj):
    s = proj.shape[0]
    t = min(256, s)
    nt = s // t

    def body(lx_ref, lxh_ref, rec_ref, rech_ref, dr_ref, lv_ref, wa_ref, wx_ref, dp_in,
             dlx_ref, sums_ref, dwa_ref, dwx_ref,
             xbuf, hbuf, dxbuf, a_s, dh_s, xc_s, r_s, ig_s, mu_s, gc):
        step_i = pl.program_id(0)
        ti = nt - 1 - step_i

        @pl.when(step_i == 0)
        def _():
            sums_ref[...] = jnp.zeros_like(sums_ref)
            dwa_ref[...] = jnp.zeros_like(dwa_ref)
            dwx_ref[...] = jnp.zeros_like(dwx_ref)
            dxbuf[pl.ds(t, 8), :] = jnp.zeros((8, D), F32)
            gc[...] = jnp.zeros((8, D), F32)

        live = jnp.where(ti > 0, 1.0, 0.0)
        xbuf[pl.ds(0, 8), :] = lxh_ref[...].astype(F32)[8:16] * live
        xbuf[pl.ds(8, t), :] = lx_ref[...].astype(F32)
        hbuf[pl.ds(0, 8), :] = rech_ref[...] * live
        hbuf[pl.ds(8, t), :] = rec_ref[...]
        first = (lax.broadcasted_iota(jnp.int32, (t, 128), 0) + ti * t) == 0
        for b in range(N_LRU_BLOCKS):
            cs = slice(b * 128, (b + 1) * 128)
            _, xc, _, r, ig, _, a, mult = _lru_block_fwd(xbuf, lv_ref, wa_ref, wx_ref, b, t, first)
            a_s[:, cs] = a
            xc_s[:, cs] = xc
            r_s[:, cs] = r
            ig_s[:, cs] = ig
            mu_s[:, cs] = mult

        def step(q, g):
            tt = t - 1 - q
            dh = dr_ref[pl.ds(tt, 1), :] + g
            dh_s[pl.ds(tt, 1), :] = dh
            return a_s[pl.ds(tt, 1), :] * dh

        gc[0:1, :] = lax.fori_loop(0, t, step, gc[0:1, :], unroll=8)
        for b in range(N_LRU_BLOCKS):
            cs = slice(b * 128, (b + 1) * 128)
            dh = dh_s[:, cs]
            a = a_s[:, cs]
            xc = xc_s[:, cs]
            r = r_s[:, cs]
            ig = ig_s[:, cs]
            mult = mu_s[:, cs]
            sp = _softplus(-lv_ref[L_LAM:L_LAM + 1, cs])
            lam = lv_ref[L_LAM:L_LAM + 1, cs]
            da = dh * hbuf[pl.ds(7, t), cs]
            dmult = jnp.where(first, 0.0, dh * ig * xc)
            dig = dh * mult * xc
            dxc = dh * mult * ig
            dlog_a = da * a - dmult * (a * a) / mult
            dr = dlog_a * ((-LRU_C) * sp)
            dsp = jnp.sum(dlog_a * ((-LRU_C) * r), axis=0, keepdims=True)
            dza = dr * r * (1.0 - r)
            dzx = dig * ig * (1.0 - ig)
            dzab = dza.astype(BF16)
            dzxb = dzx.astype(BF16)
            xcb = xc.astype(BF16)
            dwa_ref[b] += _dot_tn(xcb, dzab)
            dwx_ref[b] += _dot_tn(xcb, dzxb)
            dxc = dxc + _dot_nt(dzab, wa_ref[b]) + _dot_nt(dzxb, wx_ref[b])
            sums_ref[L_LAM:L_LAM + 1, cs] += dsp * (-jax.nn.sigmoid(-lam))
            sums_ref[L_BA:L_BA + 1, cs] += jnp.sum(dza, axis=0, keepdims=True)
            sums_ref[L_BX:L_BX + 1, cs] += jnp.sum(dzx, axis=0, keepdims=True)
            sums_ref[L_CB:L_CB + 1, cs] += jnp.sum(dxc, axis=0, keepdims=True)
            for kk in range(4):
                sums_ref[kk:kk + 1, cs] += jnp.sum(dxc * xbuf[pl.ds(5 + kk, t), cs], axis=0, keepdims=True)
            dxbuf[pl.ds(0, t), cs] = dxc
            dlx = (lv_ref[3:4, cs] * dxc + lv_ref[2:3, cs] * dxbuf[pl.ds(1, t), cs]
                   + lv_ref[1:2, cs] * dxbuf[pl.ds(2, t), cs] + lv_ref[0:1, cs] * dxbuf[pl.ds(3, t), cs])
            dlx_ref[:, cs] = dlx.astype(BF16)
        dxbuf[pl.ds(t, 8), :] = dxbuf[pl.ds(0, 8), :]

    rev = lambda i: nt - 1 - i
    return pl.pallas_call(
        body, name="lru_bwd", grid=(nt,),
        in_specs=[pl.BlockSpec((t, D), lambda i: (rev(i), 0)),
                  pl.BlockSpec((16, D), lambda i: (jnp.maximum(rev(i) * (t // 16) - 1, 0), 0)),
                  pl.BlockSpec((t, D), lambda i: (rev(i), 0)),
                  pl.BlockSpec((8, D), lambda i: (jnp.maximum(rev(i) * (t // 8) - 1, 0), 0)),
                  pl.BlockSpec((t, D), lambda i: (rev(i), 0)),
                  pl.BlockSpec((8, D), lambda i: (0, 0)),
                  pl.BlockSpec((N_LRU_BLOCKS, 128, 128), lambda i: (0, 0, 0)),
                  pl.BlockSpec((N_LRU_BLOCKS, 128, 128), lambda i: (0, 0, 0)),
                  pl.BlockSpec(memory_space=pl.ANY)],
        out_specs=[pl.BlockSpec((t, D), lambda i: (rev(i), 0)),
                   pl.BlockSpec((8, D), lambda i: (0, 0)),
                   pl.BlockSpec((N_LRU_BLOCKS, 128, 128), lambda i: (0, 0, 0)),
                   pl.BlockSpec((N_LRU_BLOCKS, 128, 128), lambda i: (0, 0, 0))],
        out_shape=[SDS((s, IN_W), BF16), SDS((8, D), F32), SDS((N_LRU_BLOCKS, 128, 128), F32),
                   SDS((N_LRU_BLOCKS, 128, 128), F32)],
        scratch_shapes=[pltpu.VMEM((t + 8, D), F32), pltpu.VMEM((t + 8, D), F32), pltpu.VMEM((t + 8, D), F32)]
        + [pltpu.VMEM((t, D), F32)] * 6 + [pltpu.VMEM((8, D), F32)],
        input_output_aliases={8: 0},
        compiler_params=_cp(1))(proj, proj, rec, rec, drec, lvec, wa, wx, dproj)


def inproj_bwd(dproj, w_in, x, dx1, vecs):
    s = x.shape[0]
    tm, tk = min(512, s), IN_TILE
    nk = IN_W // tk
    per = IN_SHARD // tk

    def body(d_ref, w_ref, x_ref, dx1_ref, v_ref, gx_ref, sums_ref, acc):
        i, k = pl.program_id(0), pl.program_id(1)

        @pl.when((i == 0) & (k == 0))
        def _():
            sums_ref[...] = jnp.zeros_like(sums_ref)

        @pl.when(k == 0)
        def _():
            acc[...] = jnp.zeros_like(acc)

        acc[...] += _dot_nt(d_ref[...], w_ref[...])

        @pl.when(k == nk - 1)
        def _():
            g1 = v_ref[V_G1:V_G1 + 1, :]
            scale1 = v_ref[V_SCALE1:V_SCALE1 + 1, :]

            def sub(rb, carry):
                rs = pl.ds(pl.multiple_of(rb * SUB, SUB), SUB)
                dh = acc[rs, :]
                r1, xh = _rms_parts(x_ref[rs, :])
                sums_ref[0:1, :] += jnp.sum(dh, axis=0, keepdims=True)
                sums_ref[1:2, :] += jnp.sum(dh * (xh * g1), axis=0, keepdims=True)
                dxn = dh * (1.0 + scale1)
                sums_ref[2:3, :] += jnp.sum(dxn * xh, axis=0, keepdims=True)
                dxh = dxn * g1
                gx_ref[rs, :] = dx1_ref[rs, :] + r1 * (dxh - xh * jnp.mean(dxh * xh, axis=-1, keepdims=True))
                return carry

            lax.fori_loop(0, tm // SUB, sub, 0)

    return pl.pallas_call(
        body, name="inproj_bwd", grid=(s // tm, nk),
        in_specs=[pl.BlockSpec((tm, tk), lambda i, k: (i, k)),
                  pl.BlockSpec((None, D, tk), lambda i, k: (k // per, 0, k % per)),
                  pl.BlockSpec((tm, D), lambda i, k: (i, 0)), pl.BlockSpec((tm, D), lambda i, k: (i, 0)),
                  pl.BlockSpec((16, D), lambda i, k: (0, 0))],
        out_specs=[pl.BlockSpec((tm, D), lambda i, k: (i, 0)), pl.BlockSpec((8, D), lambda i, k: (0, 0))],
        out_shape=[SDS((s, D), F32), SDS((8, D), F32)],
        scratch_shapes=[pltpu.VMEM((tm, D), F32)],
        compiler_params=_cp(2))(dproj, w_in, x, dx1, vecs)


def mod_columns(c16, w_ada, b_cols):
    tn = 512

    def body(c_ref, w_ref, b_ref, o_ref):
        cv = c_ref[...]
        ca = (cv * jax.nn.sigmoid(cv)).astype(BF16)
        o_ref[...] = _dot(ca, w_ref[...].astype(BF16)) + b_ref[...]

    return pl.pallas_call(
        body, name="mod_columns", grid=(ADA_SHARD // tn,),
        in_specs=[pl.BlockSpec((16, D), lambda j: (0, 0)), pl.BlockSpec((D, tn), lambda j: (0, j)),
                  pl.BlockSpec((1, tn), lambda j: (0, j))],
        out_specs=pl.BlockSpec((16, tn), lambda j: (0, j)),
        out_shape=SDS((16, ADA_SHARD), F32), compiler_params=_cp(1))(c16, w_ada, b_cols)


def wada_update(c16, dmod16, w, m, v):
    tm, tn = 512, 512

    def body(c_ref, d_ref, w_ref, m_ref, v_ref, g_out, dl_out, m_out, v_out):
        cv = c_ref[...]
        ca = (cv * jax.nn.sigmoid(cv)).astype(BF16)
        g = _dot_tn(ca, d_ref[...].astype(BF16))
        dl, m2, v2 = _adamw_math(w_ref[...], g, m_ref[...], v_ref[...])
        g_out[...] = g
        dl_out[...] = dl
        m_out[...] = m2
        v_out[...] = v2

    tile = pl.BlockSpec((tm, tn), lambda i, j: (i, j))
    return pl.pallas_call(
        body, name="wada_update", grid=(D // tm, ADA_SHARD // tn),
        in_specs=[pl.BlockSpec((16, tm), lambda i, j: (0, i)), pl.BlockSpec((16, tn), lambda i, j: (0, j)),
                  tile, tile, tile],
        out_specs=[tile] * 4, out_shape=[SDS((D, ADA_SHARD), F32)] * 4,
        compiler_params=_cp(2))(c16, dmod16, w, m, v)


def adamw_big(name, w, mine, theirs, m, v, c_idx):
    r, c = w.shape
    tr = 128
    per = (r // 2) // tr

    def body(c_ref, w_ref, a_ref, b_ref, m_ref, v_ref, g_out, dl_out, m_out, v_out):
        own = (pl.program_id(0) // per) == c_ref[0]
        g = jnp.where(own, a_ref[...], b_ref[...])
        dl, m2, v2 = _adamw_math(w_ref[...], g, m_ref[...], v_ref[...])
        g_out[...] = g
        dl_out[...] = dl
        m_out[...] = m2
        v_out[...] = v2

    tile = pl.BlockSpec((tr, c), lambda i, cr: (i, 0))
    half = pl.BlockSpec((tr, c), lambda i, cr: (i % per, 0))
    gs = pltpu.PrefetchScalarGridSpec(num_scalar_prefetch=1, grid=(r // tr,),
                                      in_specs=[tile, half, half, tile, tile], out_specs=[tile] * 4)
    return pl.pallas_call(body, name=name, grid_spec=gs, out_shape=[SDS((r, c), F32)] * 4,
                          compiler_params=_cp(1))(c_idx, w, mine, theirs, m, v)


def cast_into_slot(name, w, k_idx):
    r, c = w.shape
    tr = 256

    def body(k_ref, w_ref, o_ref):
        o_ref[...] = w_ref[...].astype(BF16)

    gs = pltpu.PrefetchScalarGridSpec(
        num_scalar_prefetch=1, grid=(r // tr,),
        in_specs=[pl.BlockSpec((tr, c), lambda i, kr: (i, 0))],
        out_specs=pl.BlockSpec((None, tr, c), lambda i, kr: (kr[0], i, 0)))
    return pl.pallas_call(body, name=name, grid_spec=gs, out_shape=SDS((N_CHIPS, r, c), BF16),
                          compiler_params=_cp(1))(k_idx, w)


def adamw_small(ws, gs, ms, vs):
    n = len(ws)

    def body(*refs):
        for i in range(n):
            dl, m2, v2 = _adamw_math(refs[i][...], refs[n + i][...], refs[2 * n + i][...], refs[3 * n + i][...])
            refs[4 * n + i][...] = dl
            refs[5 * n + i][...] = m2
            refs[6 * n + i][...] = v2

    vm = pl.BlockSpec(memory_space=pltpu.VMEM)
    shapes = [SDS(w.shape, F32) for w in ws]
    outs = pl.pallas_call(
        body, name="adamw_small", in_specs=[vm] * (4 * n), out_specs=[vm] * (3 * n), out_shape=shapes * 3,
        compiler_params=pltpu.CompilerParams(vmem_limit_bytes=VMEM_LIMIT))(*ws, *gs, *ms, *vs)
    return outs[:n], outs[n:2 * n], outs[2 * n:]


def sum_devices(gathered):
    rows = gathered.shape[1]
    tr = 96

    def body(x_ref, o_ref):
        acc = x_ref[0]
        for d in range(1, N_DEV):
            acc = acc + x_ref[d]
        o_ref[...] = acc

    return pl.pallas_call(
        body, name="sum_devices", grid=(rows // tr,),
        in_specs=[pl.BlockSpec((N_DEV, tr, D), lambda i: (0, i, 0))],
        out_specs=pl.BlockSpec((tr, D), lambda i: (i, 0)),
        out_shape=SDS((rows, D), F32), compiler_params=_cp(1))(gathered)


def _mesh_pos():
    return lax.axis_index("x"), lax.axis_index("y"), lax.axis_index("c")


def _other_chips(x, y):
    return [(1 - x, y), (x, 1 - y), (1 - x, 1 - y)]


def all_gather_small(name, block):
    m_per, n = block.shape

    def body(x_ref, out_ref, send_sems, recv_sems, local_sem):
        x, y, c = _mesh_pos()
        me, sibling = (x, y, c), (x, y, 1 - c)
        chips = _other_chips(x, y)

        def rows(px, py, pc):
            return out_ref.at[pl.ds((4 * px + 2 * py + pc) * m_per, m_per), :]

        def copy(k, blk, to, src=None):
            return pltpu.make_async_remote_copy(
                src_ref=rows(*blk) if src is None else src, dst_ref=rows(*blk),
                send_sem=send_sems.at[k], recv_sem=recv_sems.at[k], device_id=to, device_id_type=MESH)

        mine = pltpu.make_async_copy(x_ref, rows(*me), local_sem)
        mine.start()
        first = [copy(0, me, sibling, src=x_ref)]
        first += [copy(1 + j, me, (*chip, c), src=x_ref) for j, chip in enumerate(chips)]
        for cp in first:
            cp.start()
        passed = [copy(4 + j, (*chip, c), sibling) for j, chip in enumerate(chips)]
        for j, chip in enumerate(chips):
            copy(1 + j, (*chip, c), me).wait_recv()
            passed[j].start()
        copy(0, sibling, me).wait_recv()
        for j, chip in enumerate(chips):
            copy(4 + j, (*chip, 1 - c), me).wait_recv()
        for cp in first + passed:
            cp.wait_send()
        mine.wait()

    vm = pl.BlockSpec(memory_space=pltpu.VMEM)
    return pl.pallas_call(
        body, name=name, out_shape=SDS((N_DEV * m_per, n), block.dtype), in_specs=[vm], out_specs=vm,
        scratch_shapes=[pltpu.SemaphoreType.DMA((7,)), pltpu.SemaphoreType.DMA((7,)), pltpu.SemaphoreType.DMA],
        compiler_params=pltpu.CompilerParams(vmem_limit_bytes=VMEM_LIMIT))(block)


def all_gather_weights(bufs):
    n = len(bufs)
    halves = [w.shape[1] // 2 for w in bufs]

    def body(*refs):
        outs = refs[n:2 * n]
        send_sems, recv_sems = refs[2 * n:]
        x, y, c = _mesh_pos()
        k = 2 * x + y
        me, sibling = (x, y, c), (x, y, 1 - c)
        chips = _other_chips(x, y)

        def region(i, chip_idx, half):
            return outs[i].at[chip_idx, pl.ds(half * halves[i], halves[i]), :]

        def copy(i, j, reg, to):
            return pltpu.make_async_remote_copy(src_ref=reg, dst_ref=reg, send_sem=send_sems.at[i, j],
                                                recv_sem=recv_sems.at[i, j], device_id=to, device_id_type=MESH)

        started = []
        for i in range(n):
            for j, chip in enumerate(chips):
                cp = copy(i, j, region(i, k, c), (*chip, c))
                cp.start()
                started.append(cp)
        for i in range(n):
            for j, chip in enumerate(chips):
                kj = 2 * chip[0] + chip[1]
                copy(i, j, region(i, kj, c), me).wait_recv()
                cp = copy(i, 3 + j, region(i, kj, c), sibling)
                cp.start()
                started.append(cp)
        for i in range(n):
            for j, chip in enumerate(chips):
                kj = 2 * chip[0] + chip[1]
                copy(i, 3 + j, region(i, kj, 1 - c), me).wait_recv()
        for cp in started:
            cp.wait_send()

    hbm = pl.BlockSpec(memory_space=pl.ANY)
    return pl.pallas_call(
        body, name="all_gather_weights", in_specs=[hbm] * n, out_specs=[hbm] * n,
        out_shape=[SDS(w.shape, w.dtype) for w in bufs],
        input_output_aliases={i: i for i in range(n)},
        scratch_shapes=[pltpu.SemaphoreType.DMA((n, 6)), pltpu.SemaphoreType.DMA((n, 6))])(*bufs)


def sibling_exchange(grads):
    n = len(grads)
    halves = [g.shape[1] // 2 for g in grads]

    def body(*refs):
        ins, outs = refs[:n], refs[n:2 * n]
        send_sems, recv_sems = refs[2 * n:]
        x, y, c = _mesh_pos()
        sibling = (x, y, 1 - c)
        cps = []
        for i in range(n):
            cp = pltpu.make_async_remote_copy(
                src_ref=ins[i].at[:, pl.ds((1 - c) * halves[i], halves[i]), :], dst_ref=outs[i],
                send_sem=send_sems.at[i], recv_sem=recv_sems.at[i], device_id=sibling, device_id_type=MESH)
            cp.start()
            cps.append(cp)
        for cp in cps:
            cp.wait_recv()
        for cp in cps:
            cp.wait_send()

    hbm = pl.BlockSpec(memory_space=pl.ANY)
    return pl.pallas_call(
        body, name="sibling_exchange", in_specs=[hbm] * n, out_specs=[hbm] * n,
        out_shape=[SDS((N_CHIPS, h, g.shape[2]), F32) for g, h in zip(grads, halves)],
        scratch_shapes=[pltpu.SemaphoreType.DMA((n,)), pltpu.SemaphoreType.DMA((n,))])(*grads)


def sibling_sum(name, grad, other, c_idx):
    _, r, cc = grad.shape
    h = r // 2
    tr = min(256, h)
    g4 = grad.reshape(N_CHIPS, 2, h, cc)

    def body(c_ref, a_ref, b_ref, o_ref):
        o_ref[...] = (a_ref[...] + b_ref[...]).astype(BF16)

    gs = pltpu.PrefetchScalarGridSpec(
        num_scalar_prefetch=1, grid=(N_CHIPS, h // tr),
        in_specs=[pl.BlockSpec((None, None, tr, cc), lambda s, i, cr: (s, cr[0], i, 0)),
                  pl.BlockSpec((None, tr, cc), lambda s, i, cr: (s, i, 0))],
        out_specs=pl.BlockSpec((None, tr, cc), lambda s, i, cr: (s, i, 0)))
    return pl.pallas_call(body, name=name, grid_spec=gs, out_shape=SDS((N_CHIPS, h, cc), BF16),
                          compiler_params=_cp(2))(c_idx, g4, other)


def chip_exchange(sums):
    n = len(sums)

    def body(*refs):
        ins, outs = refs[:n], refs[n:2 * n]
        send_sems, recv_sems = refs[2 * n:]
        x, y, c = _mesh_pos()
        chips = _other_chips(x, y)
        cps = []
        for i in range(n):
            for j, chip in enumerate(chips):
                kj = 2 * chip[0] + chip[1]
                cp = pltpu.make_async_remote_copy(
                    src_ref=ins[i].at[kj], dst_ref=outs[i].at[j], send_sem=send_sems.at[i, j],
                    recv_sem=recv_sems.at[i, j], device_id=(*chip, c), device_id_type=MESH)
                cp.start()
                cps.append(cp)
        for cp in cps:
            cp.wait_recv()
        for cp in cps:
            cp.wait_send()

    hbm = pl.BlockSpec(memory_space=pl.ANY)
    return pl.pallas_call(
        body, name="chip_exchange", in_specs=[hbm] * n, out_specs=[hbm] * n,
        out_shape=[SDS((3,) + t.shape[1:], t.dtype) for t in sums],
        scratch_shapes=[pltpu.SemaphoreType.DMA((n, 3)), pltpu.SemaphoreType.DMA((n, 3))])(*sums)


def chip_sum(name, sums, parts, k_idx):
    _, h, cc = parts.shape
    tr = min(256, h)

    def body(k_ref, own_ref, p_ref, o_ref):
        acc = own_ref[...].astype(F32)
        for s in range(3):
            acc = acc + p_ref[s].astype(F32)
        o_ref[...] = acc

    gs = pltpu.PrefetchScalarGridSpec(
        num_scalar_prefetch=1, grid=(h // tr,),
        in_specs=[pl.BlockSpec((None, tr, cc), lambda i, kr: (kr[0], i, 0)),
                  pl.BlockSpec((3, tr, cc), lambda i, kr: (0, i, 0))],
        out_specs=pl.BlockSpec((tr, cc), lambda i, kr: (i, 0)))
    return pl.pallas_call(body, name=name, grid_spec=gs, out_shape=SDS((h, cc), F32),
                          compiler_params=_cp(1))(k_idx, sums, parts)


def halves_exchange(halves):
    n = len(halves)

    def body(*refs):
        ins, outs = refs[:n], refs[n:2 * n]
        send_sems, recv_sems = refs[2 * n:]
        x, y, c = _mesh_pos()
        cps = []
        for i in range(n):
            cp = pltpu.make_async_remote_copy(
                src_ref=ins[i], dst_ref=outs[i], send_sem=send_sems.at[i], recv_sem=recv_sems.at[i],
                device_id=(x, y, 1 - c), device_id_type=MESH)
            cp.start()
            cps.append(cp)
        for cp in cps:
            cp.wait_recv()
        for cp in cps:
            cp.wait_send()

    hbm = pl.BlockSpec(memory_space=pl.ANY)
    return pl.pallas_call(
        body, name="halves_exchange", in_specs=[hbm] * n, out_specs=[hbm] * n,
        out_shape=[SDS(t.shape, F32) for t in halves],
        scratch_shapes=[pltpu.SemaphoreType.DMA((n,)), pltpu.SemaphoreType.DMA((n,))])(*halves)


def local_step(x, tgt, vecs, lvec, wa, wx, sinks, rel_bias, weights):
    w_in, w_lru_out, w_attn_out, w_out, w_ff1, w_ff2 = weights
    s = x.shape[0]
    buckets = t5_bucket_table()
    band = bias_band(rel_bias.T, buckets).reshape(N_HEADS, BLOCK, 2 * BLOCK)
    w_lru_out2, w_attn_out2, w_out2 = w_lru_out.reshape(D, D), w_attn_out.reshape(D, D), w_out.reshape(D, D)
    w_ff2_2 = w_ff2.reshape(D_FF, D)

    proj, h = inproj_fwd(x, vecs, w_in)
    ya, rec = lru_fwd(proj, lvec, wa, wx)
    att = attn_fwd(proj, band, sinks)
    yab, merged = merge_fwd(ya, att, w_lru_out2, w_attn_out2, proj)
    x1, o1 = outproj_fwd(merged, w_out2, x, vecs)
    f, h2 = ff1_fwd(x1, vecs, w_ff1)
    dx2, do2, sums_f, loss = ff2_loss(f, w_ff2_2, x1, tgt, vecs)

    df = ff2_bwd(do2, w_ff2_2, f)
    g_ff2 = weight_grad("dw_ff2", f, do2, 512, (D_FF, D), (WG_TM, 512), lambda i, j: (i, j), relu2=True)
    dx1, do1, sums_2 = ff1_bwd(df, w_ff1, x1, dx2, o1, vecs)
    g_ff1 = weight_grad("dw_ff1", h2, df, 512, (N_CHIPS, D, D), (None, WG_TM, 512), lambda i, j: (j // 4, i, j % 4))
    dyab, dproj = outproj_bwd(do1, w_out2, yab, proj)
    g_out = weight_grad("dw_out", merged, do1, 512, (D, D), (WG_TM, 512), lambda i, j: (i, j))
    drec, dproj = lruout_bwd(dyab, w_lru_out2, rec, proj, dproj)
    g_lru_out = weight_grad("dw_lru_out", ya, dyab[0], 512, (D, D), (WG_TM, 512), lambda i, j: (i, j))
    datt = attnout_bwd(dyab, w_attn_out2)
    g_attn_out = weight_grad("dw_attn_out", att, dyab[1], 512, (D, D), (WG_TM, 512), lambda i, j: (i, j))
    dproj, dkb, dvb, dband, dsink = attn_bwd(proj, band, sinks, datt, dproj)
    dproj = dkv_combine(dkb, dvb, dproj)
    dproj, sums_l, d_wa, d_wx = lru_bwd(proj, rec, drec, lvec, wa, wx, dproj)
    grad_x, sums_1 = inproj_bwd(dproj, w_in, x, dx1, vecs)
    per = IN_SHARD // IN_TILE
    g_in = weight_grad("dw_in", h, dproj, IN_TILE, (N_CHIPS, D, IN_SHARD), (None, WG_TM, IN_TILE),
                       lambda i, j: (j // per, i, j % per))
    d_rel_bias = bias_band_bwd(dband.reshape(N_HEADS, BLOCK * 2 * BLOCK), buckets)

    big = [g_in, g_lru_out.reshape(N_CHIPS, D // 4, D), g_attn_out.reshape(N_CHIPS, D // 4, D),
           g_out.reshape(N_CHIPS, D // 4, D), g_ff1, g_ff2.reshape(N_CHIPS, D_FF // 4, D)]
    small = dict(sums_f=sums_f, sums_2=sums_2, sums_1=sums_1, sums_l=sums_l, d_wa=d_wa, d_wx=d_wx,
                 d_sinks=dsink[:, 0], d_rel_bias=d_rel_bias)
    return loss, grad_x, big, small


def _pad_rows(a, rows):
    return jnp.concatenate([a, jnp.zeros((rows - a.shape[0], a.shape[1]), a.dtype)], axis=0)


def kernel(x, c, w_ada, b_ada, norm1_g, w_in, conv_w, conv_b, lru_wa, lru_ba, lru_wx, lru_bx, lru_lambda, w_lru_out, w_attn_out, attn_sinks, rel_bias, w_out, norm2_g, w_ff1, w_ff2, final_g, loss_target, m_w_ada, m_b_ada, m_norm1_g, m_w_in, m_conv_w, m_conv_b, m_lru_wa, m_lru_ba, m_lru_wx, m_lru_bx, m_lru_lambda, m_w_lru_out, m_w_attn_out, m_attn_sinks, m_rel_bias, m_w_out, m_norm2_g, m_w_ff1, m_w_ff2, m_final_g, v_w_ada, v_b_ada, v_norm1_g, v_w_in, v_conv_w, v_conv_b, v_lru_wa, v_lru_ba, v_lru_wx, v_lru_bx, v_lru_lambda, v_w_lru_out, v_w_attn_out, v_attn_sinks, v_rel_bias, v_w_out, v_norm2_g, v_w_ff1, v_w_ff2, v_final_g):
    xi, yi, ci = _mesh_pos()
    chip = 2 * xi + yi
    dev = 2 * chip + ci
    z8 = jnp.zeros((8, D), F32)

    conv_rows = jnp.concatenate([conv_w[0], jnp.zeros((4, D - D // 4), F32)], axis=1)
    pack0 = jnp.concatenate([c, conv_rows, jnp.zeros((3, D), F32)], axis=0)
    g0 = all_gather_small("gather_cond", pack0).reshape(N_DEV, 8, D)
    c_all = g0[:, 0, :]
    conv_full = jnp.concatenate([g0[2 * k, 1:5, :D // 4] for k in range(N_CHIPS)], axis=1)
    c16 = jnp.concatenate([c_all, z8], axis=0)
    b_cols = lax.dynamic_slice_in_dim(b_ada, chip * ADA_SHARD, ADA_SHARD, axis=1)
    mod_c = mod_columns(c16, w_ada[0], b_cols)
    g1 = all_gather_small("gather_mod", mod_c).reshape(N_DEV, 16, ADA_SHARD)
    mod = jnp.concatenate([lax.dynamic_index_in_dim(g1[2 * k], dev, axis=0, keepdims=False) for k in range(N_CHIPS)])
    shift1, scale1, gate1, shift2, scale2, gate2 = [mod[i * D:(i + 1) * D] for i in range(6)]
    vecs = jnp.stack([norm1_g[0], scale1, shift1, gate1, norm2_g[0], scale2, shift2, gate2, final_g]
                     + [jnp.zeros((D,), F32)] * 7)
    lvec = jnp.concatenate([conv_full, conv_b, lru_ba, lru_bx, lru_lambda], axis=0)

    shards = [w_in[0], w_lru_out[0], w_attn_out[0], w_out[0], w_ff1[0], w_ff2[0]]
    names = ["w_in", "w_lru_out", "w_attn_out", "w_out", "w_ff1", "w_ff2"]
    k_idx = jnp.reshape(chip, (1,)).astype(jnp.int32)
    c_idx = jnp.reshape(ci, (1,)).astype(jnp.int32)
    weights = all_gather_weights([cast_into_slot("cast_" + nm, w, k_idx) for nm, w in zip(names, shards)])

    loss_t, grad_x, big, small = local_step(
        x[0], loss_target[0], vecs, lvec, lru_wa[0].astype(BF16), lru_wx[0].astype(BF16),
        attn_sinks[0], rel_bias, weights)
    loss = lax.psum(loss_t[0, 0], ("x", "y", "c"))

    sums_f, sums_2, sums_1, sums_l = small["sums_f"], small["sums_2"], small["sums_1"], small["sums_l"]
    vec_rows = jnp.stack([sums_1[2], sums_2[2], sums_f[0], sums_l[L_CB], sums_l[L_BA], sums_l[L_BX],
                          sums_l[L_LAM], jnp.zeros((D,), F32)])
    mod_rows = jnp.stack([sums_1[0], sums_1[1], sums_2[3], sums_2[0], sums_2[1], sums_f[1],
                          jnp.zeros((D,), F32), jnp.zeros((D,), F32)])
    att_rows = jnp.concatenate([
        jnp.concatenate([small["d_sinks"], jnp.zeros((D - N_HEADS,), F32)])[None],
        jnp.concatenate([small["d_rel_bias"].reshape(-1), jnp.zeros((D - N_BUCKETS * N_HEADS,), F32)])[None],
        jnp.zeros((6, D), F32)], axis=0)
    pack = jnp.concatenate([vec_rows, _pad_rows(sums_l[0:4], 8), mod_rows, att_rows,
                            small["d_wa"].reshape(128, D), small["d_wx"].reshape(128, D)], axis=0)
    gathered = all_gather_small("gather_small_grads", pack).reshape(N_DEV, P_ROWS, D)
    total = sum_devices(gathered)
    dmod_all = gathered[:, P_MOD:P_MOD + 6, :].reshape(N_DEV, 6 * D)
    dmod16 = jnp.concatenate([lax.dynamic_slice_in_dim(dmod_all, chip * ADA_SHARD, ADA_SHARD, axis=1),
                              jnp.zeros((8, ADA_SHARD), F32)], axis=0)
    g_w_ada, d_w_ada, nm_w_ada, nv_w_ada = wada_update(c16, dmod16, w_ada[0], m_w_ada[0], v_w_ada[0])

    from_sibling = sibling_exchange(big)
    chip_sums = [sibling_sum("sibling_sum_" + nm, g, o, c_idx) for nm, g, o in zip(names, big, from_sibling)]
    parts = chip_exchange(chip_sums)
    mine = [chip_sum("chip_sum_" + nm, t, p, k_idx) for nm, t, p in zip(names, chip_sums, parts)]
    theirs = halves_exchange(mine)
    big_m = [m_w_in, m_w_lru_out, m_w_attn_out, m_w_out, m_w_ff1, m_w_ff2]
    big_v = [v_w_in, v_w_lru_out, v_w_attn_out, v_w_out, v_w_ff1, v_w_ff2]
    g_big, d_big, nm_big, nv_big = {}, {}, {}, {}
    for nm, w, a, b, m, v in zip(names, shards, mine, theirs, big_m, big_v):
        g2, dl, m2, v2 = adamw_big("adamw_" + nm, w, a, b, m[0], v[0], c_idx)
        g_big[nm], d_big[nm], nm_big[nm], nv_big[nm] = g2[None], dl[None], m2[None], v2[None]

    conv_g = lax.dynamic_slice_in_dim(total[P_CONVW:P_CONVW + 4], chip * (D // 4), D // 4, axis=1)
    sm_names = ["b_ada", "norm1_g", "conv_w", "conv_b", "lru_wa", "lru_ba", "lru_wx", "lru_bx", "lru_lambda",
                "attn_sinks", "rel_bias", "norm2_g", "final_g"]
    sm_w = [b_ada.reshape(6, D), norm1_g, conv_w[0], conv_b, lru_wa.reshape(128, D), lru_ba, lru_wx.reshape(128, D),
            lru_bx, lru_lambda, attn_sinks, rel_bias, norm2_g, final_g[None]]
    sm_m = [m_b_ada.reshape(6, D), m_norm1_g, m_conv_w[0], m_conv_b, m_lru_wa.reshape(128, D), m_lru_ba,
            m_lru_wx.reshape(128, D), m_lru_bx, m_lru_lambda, m_attn_sinks, m_rel_bias, m_norm2_g, m_final_g[None]]
    sm_v = [v_b_ada.reshape(6, D), v_norm1_g, v_conv_w[0], v_conv_b, v_lru_wa.reshape(128, D), v_lru_ba,
            v_lru_wx.reshape(128, D), v_lru_bx, v_lru_lambda, v_attn_sinks, v_rel_bias, v_norm2_g, v_final_g[None]]
    sm_g = [total[P_MOD:P_MOD + 6], total[0:1], conv_g, total[3:4], total[P_WA:P_WA + 128], total[4:5],
            total[P_WX:P_WX + 128], total[5:6], total[6:7], total[P_ATT:P_ATT + 1, :N_HEADS],
            total[P_ATT + 1, :N_BUCKETS * N_HEADS].reshape(N_BUCKETS, N_HEADS), total[1:2], total[2:3]]
    sm_d, sm_nm, sm_nv = adamw_small(sm_w, sm_g, sm_m, sm_v)
    shapes = dict(b_ada=b_ada.shape, norm1_g=norm1_g.shape, conv_w=conv_w.shape, conv_b=conv_b.shape,
                  lru_wa=lru_wa.shape, lru_ba=lru_ba.shape, lru_wx=lru_wx.shape, lru_bx=lru_bx.shape,
                  lru_lambda=lru_lambda.shape, attn_sinks=attn_sinks.shape, rel_bias=rel_bias.shape,
                  norm2_g=norm2_g.shape, final_g=final_g.shape)
    grads = dict(w_ada=g_w_ada[None], **g_big)
    deltas = dict(w_ada=d_w_ada[None], **d_big)
    new_m = dict(w_ada=nm_w_ada[None], **nm_big)
    new_v = dict(w_ada=nv_w_ada[None], **nv_big)
    for i, nm in enumerate(sm_names):
        grads[nm] = sm_g[i].reshape(shapes[nm])
        deltas[nm] = sm_d[i].reshape(shapes[nm])
        new_m[nm] = sm_nm[i].reshape(shapes[nm])
        new_v[nm] = sm_nv[i].reshape(shapes[nm])
    order = ["w_ada", "b_ada", "norm1_g", "w_in", "conv_w", "conv_b", "lru_wa", "lru_ba", "lru_wx", "lru_bx",
             "lru_lambda", "w_lru_out", "w_attn_out", "attn_sinks", "rel_bias", "w_out", "norm2_g", "w_ff1", "w_ff2",
             "final_g"]
    return (loss, grad_x[None], *[grads[n] for n in order], *[deltas[n] for n in order],
            *[new_m[n] for n in order], *[new_v[n] for n in order])
```

```python
import math

import numpy as np
import jax
import jax.numpy as jnp
from jax import lax
from jax.experimental import pallas as pl
from jax.experimental.pallas import tpu as pltpu

F32 = jnp.float32
BF16 = jnp.bfloat16
SDS = jax.ShapeDtypeStruct
MESH = pl.DeviceIdType.MESH

D = 2048
D_FF = 4 * D
N_HEADS = 32
HEAD_DIM = 64
BLOCK = 128
N_LRU_BLOCKS = 16
LRU_C = 8.0
EPS = 1e-6
NEG_INF = -1e30
N_BUCKETS = 32
MAX_DISTANCE = 128
IN_W = 10752
IN_SHARD = IN_W // 4
IN_TILE = 896
ADA_SHARD = 6 * D // 4
OFF_LRU, OFF_GATE, OFF_Q, OFF_K, OFF_V, OFF_GA, OFF_GB = 0, 2048, 4096, 6144, 6400, 6656, 8704
SCALE = HEAD_DIM ** -0.5
N_CHIPS = 4
N_DEV = 8

ADAM_LR, ADAM_B1, ADAM_B2, ADAM_EPS, ADAM_WD, ADAM_STEP = 0.001, 0.9, 0.999, 1e-08, 0.01, 10
ADAM_C1 = 1.0 - ADAM_B1 ** ADAM_STEP
ADAM_C2 = 1.0 - ADAM_B2 ** ADAM_STEP

VMEM_LIMIT = 52 * 2 ** 20
SUB = 128
WG_TM = 1024
V_G1, V_SCALE1, V_SHIFT1, V_GATE1, V_G2, V_SCALE2, V_SHIFT2, V_GATE2, V_G3 = range(9)
L_CW0, L_CB, L_BA, L_BX, L_LAM = 0, 4, 5, 6, 7
P_VEC, P_CONVW, P_MOD, P_ATT, P_WA, P_WX, P_ROWS = 0, 8, 16, 24, 32, 160, 288


def _cp(n_axes):
    return pltpu.CompilerParams(dimension_semantics=("arbitrary",) * n_axes, vmem_limit_bytes=VMEM_LIMIT)


def _dot(a, b):
    return jnp.dot(a, b, preferred_element_type=F32)


def _dot_nt(a, b):
    return lax.dot_general(a, b, (((1,), (1,)), ((), ())), preferred_element_type=F32)


def _dot_tn(a, b):
    return lax.dot_general(a, b, (((0,), (0,)), ((), ())), preferred_element_type=F32)


_G0 = math.sqrt(2.0 / math.pi)
_G1 = 0.044715


def _gelu(x):
    return 0.5 * x * (1.0 + jnp.tanh(_G0 * (x + _G1 * x * x * x)))


def _gelu_grad(x):
    x2 = x * x
    t = jnp.tanh(_G0 * (x + _G1 * x * x2))
    return 0.5 * (1.0 + t) + 0.5 * x * (1.0 - t * t) * _G0 * (1.0 + 3.0 * _G1 * x2)


def _expm1(x):
    u = jnp.exp(x)
    um1 = u - 1.0
    k = um1 * x / jnp.log(u)
    return jnp.where(um1 == 0.0, x, jnp.where(u < 0.5, um1, k))


def _softplus(z):
    e = jnp.exp(-jnp.abs(z))
    u = 1.0 + e
    l1p = jnp.where(u == 1.0, e, jnp.log(u) * e / (u - 1.0))
    return jnp.maximum(z, 0.0) + l1p


def _adamw_math(w, g, m, v):
    m2 = ADAM_B1 * m + (1.0 - ADAM_B1) * g
    v2 = ADAM_B2 * v + (1.0 - ADAM_B2) * (g * g)
    m_hat = m2 / ADAM_C1
    v_hat = v2 / ADAM_C2
    delta = -ADAM_LR * (m_hat / (jnp.sqrt(v_hat) + ADAM_EPS) + ADAM_WD * w)
    return delta, m2, v2


def _rms_parts(xv):
    r = lax.rsqrt(jnp.mean(xv * xv, axis=-1, keepdims=True) + EPS)
    return r, xv * r


def inproj_fwd(x, vecs, w_in):
    s = x.shape[0]
    tm = min(512, s)
    per = IN_SHARD // IN_TILE

    def body(x_ref, v_ref, w_ref, proj_ref, h_ref):
        @pl.when(pl.program_id(1) == 0)
        def _():
            _, xh = _rms_parts(x_ref[...])
            h = (xh * v_ref[V_G1:V_G1 + 1, :]) * (1.0 + v_ref[V_SCALE1:V_SCALE1 + 1, :]) + v_ref[V_SHIFT1:V_SHIFT1 + 1, :]
            h_ref[...] = h.astype(BF16)
        proj_ref[...] = _dot(h_ref[...], w_ref[...]).astype(BF16)

    return pl.pallas_call(
        body, name="inproj_fwd", grid=(s // tm, IN_W // IN_TILE),
        in_specs=[pl.BlockSpec((tm, D), lambda i, j: (i, 0)),
                  pl.BlockSpec((16, D), lambda i, j: (0, 0)),
                  pl.BlockSpec((None, D, IN_TILE), lambda i, j: (j // per, 0, j % per))],
        out_specs=[pl.BlockSpec((tm, IN_TILE), lambda i, j: (i, j)),
                   pl.BlockSpec((tm, D), lambda i, j: (i, 0))],
        out_shape=[SDS((s, IN_W), BF16), SDS((s, D), BF16)],
        compiler_params=_cp(2))(x, vecs, w_in)


def _lru_block_fwd(xbuf, lv_ref, wa_ref, wx_ref, b, t, first):
    cs = slice(b * 128, (b + 1) * 128)
    x0 = xbuf[pl.ds(8, t), cs]
    x1 = xbuf[pl.ds(7, t), cs]
    x2 = xbuf[pl.ds(6, t), cs]
    x3 = xbuf[pl.ds(5, t), cs]
    xc = (lv_ref[L_CB:L_CB + 1, cs] + lv_ref[3:4, cs] * x0 + lv_ref[2:3, cs] * x1
          + lv_ref[1:2, cs] * x2 + lv_ref[0:1, cs] * x3)
    xcb = xc.astype(BF16)
    r = jax.nn.sigmoid(_dot(xcb, wa_ref[b]) + lv_ref[L_BA:L_BA + 1, cs])
    ig = jax.nn.sigmoid(_dot(xcb, wx_ref[b]) + lv_ref[L_BX:L_BX + 1, cs])
    sp = _softplus(-lv_ref[L_LAM:L_LAM + 1, cs])
    log_a = (-LRU_C) * r * sp
    a = jnp.exp(log_a)
    mult = jnp.where(first, 1.0, jnp.sqrt(-_expm1(2.0 * log_a)))
    return (x0, x1, x2, x3), xc, xcb, r, ig, sp, a, mult


def lru_fwd(proj, lvec, wa, wx):
    s = proj.shape[0]
    t = min(256, s)

    def body(lx_ref, gate_ref, lv_ref, wa_ref, wx_ref, ya_ref, rec_ref, xbuf, a_s, u_s, hc):
        i = pl.program_id(0)

        @pl.when(i == 0)
        def _():
            xbuf[pl.ds(0, 8), :] = jnp.zeros((8, D), F32)
            hc[...] = jnp.zeros((8, D), F32)

        @pl.when(i > 0)
        def _():
            xbuf[pl.ds(0, 8), :] = xbuf[pl.ds(t, 8), :]

        xbuf[pl.ds(8, t), :] = lx_ref[...].astype(F32)
        first = (lax.broadcasted_iota(jnp.int32, (t, 128), 0) + i * t) == 0
        for b in range(N_LRU_BLOCKS):
            cs = slice(b * 128, (b + 1) * 128)
            _, xc, _, _, ig, _, a, mult = _lru_block_fwd(xbuf, lv_ref, wa_ref, wx_ref, b, t, first)
            a_s[:, cs] = a
            u_s[:, cs] = mult * (ig * xc)

        def step(tt, h):
            h = a_s[pl.ds(tt, 1), :] * h + u_s[pl.ds(tt, 1), :]
            rec_ref[pl.ds(tt, 1), :] = h
            return h

        hc[0:1, :] = lax.fori_loop(0, t, step, hc[0:1, :], unroll=8)
        for b in range(N_LRU_BLOCKS):
            cs = slice(b * 128, (b + 1) * 128)
            ya_ref[:, cs] = (rec_ref[:, cs] * _gelu(gate_ref[:, cs].astype(F32))).astype(BF16)

    return pl.pallas_call(
        body, name="lru_fwd", grid=(s // t,),
        in_specs=[pl.BlockSpec((t, D), lambda i: (i, OFF_LRU // D)),
                  pl.BlockSpec((t, D), lambda i: (i, OFF_GATE // D)),
                  pl.BlockSpec((8, D), lambda i: (0, 0)),
                  pl.BlockSpec((N_LRU_BLOCKS, 128, 128), lambda i: (0, 0, 0)),
                  pl.BlockSpec((N_LRU_BLOCKS, 128, 128), lambda i: (0, 0, 0))],
        out_specs=[pl.BlockSpec((t, D), lambda i: (i, 0)), pl.BlockSpec((t, D), lambda i: (i, 0))],
        out_shape=[SDS((s, D), BF16), SDS((s, D), F32)],
        scratch_shapes=[pltpu.VMEM((t + 8, D), F32), pltpu.VMEM((t, D), F32), pltpu.VMEM((t, D), F32),
                        pltpu.VMEM((8, D), F32)],
        compiler_params=_cp(1))(proj, proj, lvec, wa, wx)


def t5_bucket_table():
    qi = np.arange(BLOCK)[:, None]
    ki = np.arange(2 * BLOCK)[None, :]
    rel = qi + BLOCK - ki
    relc = np.maximum(rel, 0)
    max_exact = N_BUCKETS // 2
    relf = np.maximum(relc, 1).astype(np.float32)
    large = max_exact + (np.log(relf / np.float32(max_exact)) / np.float32(math.log(MAX_DISTANCE / max_exact))
                         * np.float32(N_BUCKETS - max_exact)).astype(np.int32)
    large = np.minimum(large, N_BUCKETS - 1)
    bucket = np.where(relc < max_exact, relc, large)
    bucket = np.where((rel >= 0) & (rel < BLOCK), bucket, -1)
    return jnp.asarray(bucket.reshape(1, BLOCK * 2 * BLOCK), jnp.int32)


def bias_band(rel_bias_t, buckets):
    n = BLOCK * 2 * BLOCK
    tn = 4096

    def body(bk_ref, rb_ref, o_ref):
        row = lax.broadcasted_iota(jnp.int32, (N_BUCKETS, tn), 0)
        oh = jnp.where(row == bk_ref[...], 1.0, 0.0).astype(BF16)
        rb = rb_ref[...]
        p0 = rb.astype(BF16)
        r1 = rb - p0.astype(F32)
        p1 = r1.astype(BF16)
        p2 = (r1 - p1.astype(F32)).astype(BF16)
        o_ref[...] = _dot(p0, oh) + _dot(p1, oh) + _dot(p2, oh)

    return pl.pallas_call(
        body, name="bias_band", grid=(n // tn,),
        in_specs=[pl.BlockSpec((1, tn), lambda i: (0, i)), pl.BlockSpec((N_HEADS, N_BUCKETS), lambda i: (0, 0))],
        out_specs=pl.BlockSpec((N_HEADS, tn), lambda i: (0, i)),
        out_shape=SDS((N_HEADS, n), F32), compiler_params=_cp(1))(buckets, rel_bias_t)


def bias_band_bwd(dband, buckets):
    n = BLOCK * 2 * BLOCK
    tn = 4096

    def body(bk_ref, d_ref, o_ref):
        @pl.when(pl.program_id(0) == 0)
        def _():
            o_ref[...] = jnp.zeros_like(o_ref)
        row = lax.broadcasted_iota(jnp.int32, (N_BUCKETS, tn), 0)
        oh = jnp.where(row == bk_ref[...], 1.0, 0.0).astype(BF16)
        dv = d_ref[...]
        p0 = dv.astype(BF16)
        r1 = dv - p0.astype(F32)
        p1 = r1.astype(BF16)
        p2 = (r1 - p1.astype(F32)).astype(BF16)
        o_ref[...] += _dot_nt(oh, p0) + _dot_nt(oh, p1) + _dot_nt(oh, p2)

    return pl.pallas_call(
        body, name="bias_band_bwd", grid=(n // tn,),
        in_specs=[pl.BlockSpec((1, tn), lambda i: (0, i)), pl.BlockSpec((N_HEADS, tn), lambda i: (0, i))],
        out_specs=pl.BlockSpec((N_BUCKETS, N_HEADS), lambda i: (0, 0)),
        out_shape=SDS((N_BUCKETS, N_HEADS), F32), compiler_params=_cp(1))(buckets, dband)


def _dup_half(band, which):
    lane = lax.broadcasted_iota(jnp.int32, band.shape, 1)
    rolled = pltpu.roll(band, 64, 1)
    keep = (lane < 64) if which == 0 else (lane >= 64)
    return jnp.where(keep, band, rolled)


def _attn_probs(qm, kk, bias, sink, valid):
    sc = _dot_nt(qm, kk) * SCALE + bias
    sc = jnp.where(valid, sc, NEG_INF)
    m = jnp.maximum(jnp.max(sc, axis=-1, keepdims=True), sink)
    e = jnp.exp(sc - m)
    es = jnp.exp(sink - m)
    inv = 1.0 / (jnp.sum(e, axis=-1, keepdims=True) + es)
    return e * inv, es * inv


def _band_valid(n):
    qi = lax.broadcasted_iota(jnp.int32, (BLOCK, 2 * BLOCK), 0)
    ki = lax.broadcasted_iota(jnp.int32, (BLOCK, 2 * BLOCK), 1)
    rel = qi + BLOCK - ki
    return (rel >= 0) & (rel < BLOCK) & ((ki >= BLOCK) | (n > 0))


def _kv_bands(prev_ref, cur_ref):
    band = jnp.concatenate([prev_ref[...].astype(F32), cur_ref[...].astype(F32)], axis=0)
    return [_dup_half(band, 0).astype(BF16), _dup_half(band, 1).astype(BF16)]


def attn_fwd(proj, band, sinks):
    s = proj.shape[0]
    nb = s // BLOCK
    qw = 1024

    def body(sk_ref, q_ref, kp_ref, kc_ref, vp_ref, vc_ref, b_ref, o_ref):
        n = pl.program_id(0)
        gp = pl.program_id(1)
        valid = _band_valid(n)
        kks = _kv_bands(kp_ref, kc_ref)
        vvs = _kv_bands(vp_ref, vc_ref)
        lane = lax.broadcasted_iota(jnp.int32, (BLOCK, 128), 1)
        for j in range(8):
            qs = q_ref[:, j * 128:(j + 1) * 128]
            outs = []
            for hh in range(2):
                hl = 2 * j + hh
                qm = jnp.where((lane < 64) if hh == 0 else (lane >= 64), qs, jnp.zeros_like(qs))
                p, _ = _attn_probs(qm, kks[j // 4], b_ref[hl], sk_ref[gp * 16 + hl], valid)
                outs.append(_dot(p.astype(BF16), vvs[j // 4]))
            o_ref[:, j * 128:(j + 1) * 128] = jnp.where(lane < 64, outs[0], outs[1]).astype(BF16)

    kb, vb = OFF_K // 128, OFF_V // 128
    return pl.pallas_call(
        body, name="attn_fwd", grid=(nb, 2),
        in_specs=[pl.BlockSpec(memory_space=pltpu.SMEM),
                  pl.BlockSpec((BLOCK, qw), lambda n, g: (n, OFF_Q // qw + g)),
                  pl.BlockSpec((BLOCK, 128), lambda n, g: (jnp.maximum(n - 1, 0), kb + g)),
                  pl.BlockSpec((BLOCK, 128), lambda n, g: (n, kb + g)),
                  pl.BlockSpec((BLOCK, 128), lambda n, g: (jnp.maximum(n - 1, 0), vb + g)),
                  pl.BlockSpec((BLOCK, 128), lambda n, g: (n, vb + g)),
                  pl.BlockSpec((16, BLOCK, 2 * BLOCK), lambda n, g: (g, 0, 0))],
        out_specs=pl.BlockSpec((BLOCK, qw), lambda n, g: (n, g)),
        out_shape=SDS((s, D), BF16), compiler_params=_cp(2))(sinks, proj, proj, proj, proj, proj, band)


def merge_fwd(ya, att, w_lru_out, w_attn_out, proj):
    s = ya.shape[0]
    tm, tn = min(512, s), 512

    def body(ya_ref, at_ref, wl_ref, wt_ref, ga_ref, gb_ref, yab_ref, mg_ref):
        y_a = _dot(ya_ref[...], wl_ref[...])
        y_b = _dot(at_ref[...], wt_ref[...])
        yab_ref[0] = y_a.astype(BF16)
        yab_ref[1] = y_b.astype(BF16)
        mg_ref[...] = (jax.nn.sigmoid(ga_ref[...].astype(F32)) * y_a
                       + jax.nn.sigmoid(gb_ref[...].astype(F32)) * y_b).astype(BF16)

    return pl.pallas_call(
        body, name="merge_fwd", grid=(s // tm, D // tn),
        in_specs=[pl.BlockSpec((tm, D), lambda i, j: (i, 0)), pl.BlockSpec((tm, D), lambda i, j: (i, 0)),
                  pl.BlockSpec((D, tn), lambda i, j: (0, j)), pl.BlockSpec((D, tn), lambda i, j: (0, j)),
                  pl.BlockSpec((tm, tn), lambda i, j: (i, OFF_GA // tn + j)),
                  pl.BlockSpec((tm, tn), lambda i, j: (i, OFF_GB // tn + j))],
        out_specs=[pl.BlockSpec((2, tm, tn), lambda i, j: (0, i, j)), pl.BlockSpec((tm, tn), lambda i, j: (i, j))],
        out_shape=[SDS((2, s, D), BF16), SDS((s, D), BF16)],
        compiler_params=_cp(2))(ya, att, w_lru_out, w_attn_out, proj, proj)


def outproj_fwd(merged, w_out, x, vecs):
    s = x.shape[0]
    tm, tn = min(512, s), 512

    def body(m_ref, w_ref, x_ref, v_ref, x1_ref, o1_ref):
        o1 = _dot(m_ref[...], w_ref[...])
        o1_ref[...] = o1.astype(BF16)
        x1_ref[...] = x_ref[...] + v_ref[V_GATE1:V_GATE1 + 1, :] * o1

    return pl.pallas_call(
        body, name="outproj_fwd", grid=(s // tm, D // tn),
        in_specs=[pl.BlockSpec((tm, D), lambda i, j: (i, 0)), pl.BlockSpec((D, tn), lambda i, j: (0, j)),
                  pl.BlockSpec((tm, tn), lambda i, j: (i, j)), pl.BlockSpec((16, tn), lambda i, j: (0, j))],
        out_specs=[pl.BlockSpec((tm, tn), lambda i, j: (i, j)), pl.BlockSpec((tm, tn), lambda i, j: (i, j))],
        out_shape=[SDS((s, D), F32), SDS((s, D), BF16)],
        compiler_params=_cp(2))(merged, w_out, x, vecs)


def ff1_fwd(x1, vecs, w_ff1):
    s = x1.shape[0]
    tm, tn = min(512, s), 512
    per = D // tn

    def body(x_ref, v_ref, w_ref, f_ref, h_ref):
        @pl.when(pl.program_id(1) == 0)
        def _():
            _, xh = _rms_parts(x_ref[...])
            h = (xh * v_ref[V_G2:V_G2 + 1, :]) * (1.0 + v_ref[V_SCALE2:V_SCALE2 + 1, :]) + v_ref[V_SHIFT2:V_SHIFT2 + 1, :]
            h_ref[...] = h.astype(BF16)
        f_ref[...] = _dot(h_ref[...], w_ref[...]).astype(BF16)

    return pl.pallas_call(
        body, name="ff1_fwd", grid=(s // tm, D_FF // tn),
        in_specs=[pl.BlockSpec((tm, D), lambda i, j: (i, 0)), pl.BlockSpec((16, D), lambda i, j: (0, 0)),
                  pl.BlockSpec((None, D, tn), lambda i, j: (j // per, 0, j % per))],
        out_specs=[pl.BlockSpec((tm, tn), lambda i, j: (i, j)), pl.BlockSpec((tm, D), lambda i, j: (i, 0))],
        out_shape=[SDS((s, D_FF), BF16), SDS((s, D), BF16)],
        compiler_params=_cp(2))(x1, vecs, w_ff1)


def ff2_loss(f, w_ff2, x1, tgt, vecs):
    s = x1.shape[0]
    tm, tk = min(512, s), 512
    nk = D_FF // tk

    def body(f_ref, w_ref, x1_ref, t_ref, v_ref, dx2_ref, do2_ref, sums_ref, loss_ref, acc):
        i, k = pl.program_id(0), pl.program_id(1)

        @pl.when((i == 0) & (k == 0))
        def _():
            sums_ref[...] = jnp.zeros_like(sums_ref)
            loss_ref[...] = jnp.zeros_like(loss_ref)

        @pl.when(k == 0)
        def _():
            acc[...] = jnp.zeros_like(acc)

        fv = jnp.maximum(f_ref[...].astype(F32), 0.0)
        acc[...] += _dot((fv * fv).astype(BF16), w_ref[...])

        @pl.when(k == nk - 1)
        def _():
            gate2 = v_ref[V_GATE2:V_GATE2 + 1, :]
            g3 = v_ref[V_G3:V_G3 + 1, :]

            def sub(rb, carry):
                rs = pl.ds(pl.multiple_of(rb * SUB, SUB), SUB)
                o2 = acc[rs, :]
                x2 = x1_ref[rs, :] + gate2 * o2
                r3, xh = _rms_parts(x2)
                e = xh * g3 - t_ref[rs, :]
                loss_ref[...] += (0.5 / D) * jnp.sum(e * e)
                dy = e * (1.0 / D)
                sums_ref[0:1, :] += jnp.sum(dy * xh, axis=0, keepdims=True)
                dxh = dy * g3
                dx2 = r3 * (dxh - xh * jnp.mean(dxh * xh, axis=-1, keepdims=True))
                sums_ref[1:2, :] += jnp.sum(dx2 * o2, axis=0, keepdims=True)
                dx2_ref[rs, :] = dx2
                do2_ref[rs, :] = (dx2 * gate2).astype(BF16)
                return carry

            lax.fori_loop(0, tm // SUB, sub, 0)

    return pl.pallas_call(
        body, name="ff2_loss", grid=(s // tm, nk),
        in_specs=[pl.BlockSpec((tm, tk), lambda i, k: (i, k)), pl.BlockSpec((tk, D), lambda i, k: (k, 0)),
                  pl.BlockSpec((tm, D), lambda i, k: (i, 0)), pl.BlockSpec((tm, D), lambda i, k: (i, 0)),
                  pl.BlockSpec((16, D), lambda i, k: (0, 0))],
        out_specs=[pl.BlockSpec((tm, D), lambda i, k: (i, 0)), pl.BlockSpec((tm, D), lambda i, k: (i, 0)),
                   pl.BlockSpec((8, D), lambda i, k: (0, 0)), pl.BlockSpec((8, 128), lambda i, k: (0, 0))],
        out_shape=[SDS((s, D), F32), SDS((s, D), BF16), SDS((8, D), F32), SDS((8, 128), F32)],
        scratch_shapes=[pltpu.VMEM((tm, D), F32)],
        compiler_params=_cp(2))(f, w_ff2, x1, tgt, vecs)


def ff2_bwd(do2, w_ff2, f):
    s = do2.shape[0]
    tm, tn = min(512, s), 512

    def body(d_ref, w_ref, f_ref, o_ref):
        dff = _dot_nt(d_ref[...], w_ref[...])
        o_ref[...] = (dff * (2.0 * jnp.maximum(f_ref[...].astype(F32), 0.0))).astype(BF16)

    return pl.pallas_call(
        body, name="ff2_bwd", grid=(s // tm, D_FF // tn),
        in_specs=[pl.BlockSpec((tm, D), lambda i, j: (i, 0)), pl.BlockSpec((tn, D), lambda i, j: (j, 0)),
                  pl.BlockSpec((tm, tn), lambda i, j: (i, j))],
        out_specs=pl.BlockSpec((tm, tn), lambda i, j: (i, j)),
        out_shape=SDS((s, D_FF), BF16), compiler_params=_cp(2))(do2, w_ff2, f)


def weight_grad(name, a, b, tn, out_shape, out_block, out_map, relu2=False):
    s, m = a.shape
    n = b.shape[1]
    tm = WG_TM
    chunk = min(1024, s)
    nch = s // chunk

    def body(a_hbm, b_ref, o_ref, a_buf, at_s, sem):
        i = pl.program_id(0)

        @pl.when(pl.program_id(1) == 0)
        def _():
            def fetch(ch):
                return pltpu.make_async_copy(a_hbm.at[pl.ds(ch * chunk, chunk), pl.ds(i * tm, tm)],
                                             a_buf.at[ch % 2], sem.at[ch % 2])
            fetch(0).start()
            for ch in range(nch):
                if ch + 1 < nch:
                    fetch(ch + 1).start()
                fetch(ch).wait()
                av = a_buf[ch % 2]
                if relu2:
                    fv = jnp.maximum(av.astype(F32), 0.0)
                    av = (fv * fv).astype(BF16)
                at_s[:, ch * chunk:(ch + 1) * chunk] = av.T

        o_ref[...] = _dot(at_s[...], b_ref[...])

    return pl.pallas_call(
        body, name=name, grid=(m // tm, n // tn),
        in_specs=[pl.BlockSpec(memory_space=pl.ANY), pl.BlockSpec((s, tn), lambda i, j: (0, j))],
        out_specs=pl.BlockSpec(out_block, lambda i, j: out_map(i, j)),
        out_shape=SDS(out_shape, F32),
        scratch_shapes=[pltpu.VMEM((2, chunk, tm), BF16), pltpu.VMEM((tm, s), BF16), pltpu.SemaphoreType.DMA((2,))],
        compiler_params=_cp(2))(a, b)


def ff1_bwd(df, w_ff1, x1, dx2, o1, vecs):
    s = df.shape[0]
    tm, tk = min(512, s), 512
    nk = D_FF // tk
    per = D // tk

    def body(d_ref, w_ref, x1_ref, dx2_ref, o1_ref, v_ref, dx1_ref, do1_ref, sums_ref, acc):
        i, k = pl.program_id(0), pl.program_id(1)

        @pl.when((i == 0) & (k == 0))
        def _():
            sums_ref[...] = jnp.zeros_like(sums_ref)

        @pl.when(k == 0)
        def _():
            acc[...] = jnp.zeros_like(acc)

        acc[...] += _dot_nt(d_ref[...], w_ref[...])

        @pl.when(k == nk - 1)
        def _():
            g2 = v_ref[V_G2:V_G2 + 1, :]
            scale2 = v_ref[V_SCALE2:V_SCALE2 + 1, :]
            gate1 = v_ref[V_GATE1:V_GATE1 + 1, :]

            def sub(rb, carry):
                rs = pl.ds(pl.multiple_of(rb * SUB, SUB), SUB)
                dh = acc[rs, :]
                r2, xh = _rms_parts(x1_ref[rs, :])
                sums_ref[0:1, :] += jnp.sum(dh, axis=0, keepdims=True)
                sums_ref[1:2, :] += jnp.sum(dh * (xh * g2), axis=0, keepdims=True)
                dxn = dh * (1.0 + scale2)
                sums_ref[2:3, :] += jnp.sum(dxn * xh, axis=0, keepdims=True)
                dxh = dxn * g2
                dx1 = dx2_ref[rs, :] + r2 * (dxh - xh * jnp.mean(dxh * xh, axis=-1, keepdims=True))
                sums_ref[3:4, :] += jnp.sum(dx1 * o1_ref[rs, :].astype(F32), axis=0, keepdims=True)
                dx1_ref[rs, :] = dx1
                do1_ref[rs, :] = (dx1 * gate1).astype(BF16)
                return carry

            lax.fori_loop(0, tm // SUB, sub, 0)

    return pl.pallas_call(
        body, name="ff1_bwd", grid=(s // tm, nk),
        in_specs=[pl.BlockSpec((tm, tk), lambda i, k: (i, k)),
                  pl.BlockSpec((None, D, tk), lambda i, k: (k // per, 0, k % per)),
                  pl.BlockSpec((tm, D), lambda i, k: (i, 0)), pl.BlockSpec((tm, D), lambda i, k: (i, 0)),
                  pl.BlockSpec((tm, D), lambda i, k: (i, 0)), pl.BlockSpec((16, D), lambda i, k: (0, 0))],
        out_specs=[pl.BlockSpec((tm, D), lambda i, k: (i, 0)), pl.BlockSpec((tm, D), lambda i, k: (i, 0)),
                   pl.BlockSpec((8, D), lambda i, k: (0, 0))],
        out_shape=[SDS((s, D), F32), SDS((s, D), BF16), SDS((8, D), F32)],
        scratch_shapes=[pltpu.VMEM((tm, D), F32)],
        compiler_params=_cp(2))(df, w_ff1, x1, dx2, o1, vecs)


def outproj_bwd(do1, w_out, yab, proj):
    s = do1.shape[0]
    tm, tn = min(512, s), 512
    per = D // tn

    def body(d_ref, w_ref, y_ref, g_ref, dy_ref, dp_ref):
        dm = _dot_nt(d_ref[...], w_ref[...])
        sg = jax.nn.sigmoid(g_ref[...].astype(F32))
        dy_ref[...] = (dm * sg).astype(BF16)
        dp_ref[...] = (dm * y_ref[...].astype(F32) * sg * (1.0 - sg)).astype(BF16)

    return pl.pallas_call(
        body, name="outproj_bwd", grid=(s // tm, 2 * per),
        in_specs=[pl.BlockSpec((tm, D), lambda i, j: (i, 0)), pl.BlockSpec((tn, D), lambda i, j: (j % per, 0)),
                  pl.BlockSpec((None, tm, tn), lambda i, j: (j // per, i, j % per)),
                  pl.BlockSpec((tm, tn), lambda i, j: (i, OFF_GA // tn + j))],
        out_specs=[pl.BlockSpec((None, tm, tn), lambda i, j: (j // per, i, j % per)),
                   pl.BlockSpec((tm, tn), lambda i, j: (i, OFF_GA // tn + j))],
        out_shape=[SDS((2, s, D), BF16), SDS((s, IN_W), BF16)],
        compiler_params=_cp(2))(do1, w_out, yab, proj)


def lruout_bwd(dyab, w_lru_out, rec, proj, dproj):
    s = rec.shape[0]
    tm, tn = min(512, s), 512

    def body(d_ref, w_ref, r_ref, g_ref, dp_in, dr_ref, dp_ref):
        dya = _dot_nt(d_ref[...], w_ref[...])
        gate = g_ref[...].astype(F32)
        dr_ref[...] = dya * _gelu(gate)
        dp_ref[...] = (dya * r_ref[...] * _gelu_grad(gate)).astype(BF16)

    return pl.pallas_call(
        body, name="lruout_bwd", grid=(s // tm, D // tn),
        in_specs=[pl.BlockSpec((None, tm, D), lambda i, j: (0, i, 0)), pl.BlockSpec((tn, D), lambda i, j: (j, 0)),
                  pl.BlockSpec((tm, tn), lambda i, j: (i, j)),
                  pl.BlockSpec((tm, tn), lambda i, j: (i, OFF_GATE // tn + j)),
                  pl.BlockSpec(memory_space=pl.ANY)],
        out_specs=[pl.BlockSpec((tm, tn), lambda i, j: (i, j)),
                   pl.BlockSpec((tm, tn), lambda i, j: (i, OFF_GATE // tn + j))],
        out_shape=[SDS((s, D), F32), SDS((s, IN_W), BF16)],
        input_output_aliases={4: 1},
        compiler_params=_cp(2))(dyab, w_lru_out, rec, proj, dproj)


def attnout_bwd(dyab, w_attn_out):
    s = dyab.shape[1]
    tm, tn = min(512, s), 512

    def body(d_ref, w_ref, o_ref):
        o_ref[...] = _dot_nt(d_ref[...], w_ref[...]).astype(BF16)

    return pl.pallas_call(
        body, name="attnout_bwd", grid=(s // tm, D // tn),
        in_specs=[pl.BlockSpec((None, tm, D), lambda i, j: (1, i, 0)), pl.BlockSpec((tn, D), lambda i, j: (j, 0))],
        out_specs=pl.BlockSpec((tm, tn), lambda i, j: (i, j)),
        out_shape=SDS((s, D), BF16), compiler_params=_cp(2))(dyab, w_attn_out)


def attn_bwd(proj, band, sinks, datt, dproj):
    s = proj.shape[0]
    nb = s // BLOCK
    qw = 1024

    def body(sk_ref, q_ref, kp_ref, kc_ref, vp_ref, vc_ref, b_ref, do_ref, dp_in,
             dq_ref, dkb_ref, dvb_ref, db_ref, ds_ref):
        gp = pl.program_id(0)
        n = pl.program_id(1)

        @pl.when(n == 0)
        def _():
            db_ref[...] = jnp.zeros_like(db_ref)
            ds_ref[...] = jnp.zeros_like(ds_ref)

        valid = _band_valid(n)
        kks = _kv_bands(kp_ref, kc_ref)
        vvs = _kv_bands(vp_ref, vc_ref)
        lane = lax.broadcasted_iota(jnp.int32, (BLOCK, 128), 1)
        lane_b = lax.broadcasted_iota(jnp.int32, (2 * BLOCK, 128), 1)
        dk_acc = [jnp.zeros((2 * BLOCK, 128), F32), jnp.zeros((2 * BLOCK, 128), F32)]
        dv_acc = [jnp.zeros((2 * BLOCK, 128), F32), jnp.zeros((2 * BLOCK, 128), F32)]
        for j in range(8):
            kv = j // 4
            qs = q_ref[:, j * 128:(j + 1) * 128]
            dos = do_ref[:, j * 128:(j + 1) * 128]
            dqs = []
            for hh in range(2):
                hl = 2 * j + hh
                half = (lane < 64) if hh == 0 else (lane >= 64)
                qm = jnp.where(half, qs, jnp.zeros_like(qs))
                dom = jnp.where(half, dos, jnp.zeros_like(dos))
                p, ps = _attn_probs(qm, kks[kv], b_ref[hl], sk_ref[gp * 16 + hl], valid)
                dp = _dot_nt(dom, vvs[kv])
                delta = jnp.sum(p * dp, axis=-1, keepdims=True)
                dsc = p * (dp - delta)
                db_ref[hl] += dsc
                ds_ref[pl.ds(hl, 1), :] += jnp.zeros((1, 128), F32) - jnp.sum(ps * delta)
                dsb = (dsc * SCALE).astype(BF16)
                dqs.append(_dot(dsb, kks[kv]))
                dk_acc[kv] = dk_acc[kv] + _dot_tn(dsb, qm)
                dv_acc[kv] = dv_acc[kv] + _dot_tn(p.astype(BF16), dom)
            dq_ref[:, j * 128:(j + 1) * 128] = jnp.where(lane < 64, dqs[0], dqs[1]).astype(BF16)
        for acc, ref in ((dk_acc, dkb_ref), (dv_acc, dvb_ref)):
            d0 = acc[0] + pltpu.roll(acc[0], 64, 1)
            d1 = acc[1] + pltpu.roll(acc[1], 64, 1)
            ref[...] = jnp.where(lane_b < 64, d0, d1)

    kb, vb = OFF_K // 128, OFF_V // 128
    return pl.pallas_call(
        body, name="attn_bwd", grid=(2, nb),
        in_specs=[pl.BlockSpec(memory_space=pltpu.SMEM),
                  pl.BlockSpec((BLOCK, qw), lambda g, n: (n, OFF_Q // qw + g)),
                  pl.BlockSpec((BLOCK, 128), lambda g, n: (jnp.maximum(n - 1, 0), kb + g)),
                  pl.BlockSpec((BLOCK, 128), lambda g, n: (n, kb + g)),
                  pl.BlockSpec((BLOCK, 128), lambda g, n: (jnp.maximum(n - 1, 0), vb + g)),
                  pl.BlockSpec((BLOCK, 128), lambda g, n: (n, vb + g)),
                  pl.BlockSpec((16, BLOCK, 2 * BLOCK), lambda g, n: (g, 0, 0)),
                  pl.BlockSpec((BLOCK, qw), lambda g, n: (n, g)),
                  pl.BlockSpec(memory_space=pl.ANY)],
        out_specs=[pl.BlockSpec((BLOCK, qw), lambda g, n: (n, OFF_Q // qw + g)),
                   pl.BlockSpec((2 * BLOCK, 128), lambda g, n: (n, g)),
                   pl.BlockSpec((2 * BLOCK, 128), lambda g, n: (n, g)),
                   pl.BlockSpec((16, BLOCK, 2 * BLOCK), lambda g, n: (g, 0, 0)),
                   pl.BlockSpec((16, 128), lambda g, n: (g, 0))],
        out_shape=[SDS((s, IN_W), BF16), SDS((nb * 2 * BLOCK, 256), F32), SDS((nb * 2 * BLOCK, 256), F32),
                   SDS((N_HEADS, BLOCK, 2 * BLOCK), F32), SDS((N_HEADS, 128), F32)],
        input_output_aliases={8: 0},
        compiler_params=_cp(2))(sinks, proj, proj, proj, proj, proj, band, datt, dproj)


def dkv_combine(dkb, dvb, dproj):
    nb = dkb.shape[0] // (2 * BLOCK)
    s = nb * BLOCK
    dkb3 = dkb.reshape(nb, 2 * BLOCK, 256)
    dvb3 = dvb.reshape(nb, 2 * BLOCK, 256)

    def body(k1, k2, v1, v2, dp_in, o_ref):
        nxt = jnp.where(pl.program_id(0) < nb - 1, 1.0, 0.0)
        o_ref[:, 0:256] = (k1[...] + nxt * k2[...]).astype(BF16)
        o_ref[:, 256:512] = (v1[...] + nxt * v2[...]).astype(BF16)

    spec1 = pl.BlockSpec((None, BLOCK, 256), lambda m: (m, 1, 0))
    spec2 = pl.BlockSpec((None, BLOCK, 256), lambda m: (jnp.minimum(m + 1, nb - 1), 0, 0))
    return pl.pallas_call(
        body, name="dkv_combine", grid=(nb,),
        in_specs=[spec1, spec2, spec1, spec2, pl.BlockSpec(memory_space=pl.ANY)],
        out_specs=pl.BlockSpec((BLOCK, 512), lambda m: (m, OFF_K // 512)),
        out_shape=SDS((s, IN_W), BF16), input_output_aliases={4: 0},
        compiler_params=_cp(1))(dkb3, dkb3, dvb3, dvb3, dproj)


def lru_bwd(proj, rec, drec, lvec, wa, wx, dproj):
    s = proj.shape[0]
    t = min(256, s)
    nt = s // t

    def body(lx_ref, lxh_ref, rec_ref, rech_ref, dr_ref, lv_ref, wa_ref, wx_ref, dp_in,
             dlx_ref, sums_ref, dwa_ref, dwx_ref,
             xbuf, hbuf, dxbuf, a_s, dh_s, xc_s, r_s, ig_s, mu_s, gc):
        step_i = pl.program_id(0)
        ti = nt - 1 - step_i

        @pl.when(step_i == 0)
        def _():
            sums_ref[...] = jnp.zeros_like(sums_ref)
            dwa_ref[...] = jnp.zeros_like(dwa_ref)
            dwx_ref[...] = jnp.zeros_like(dwx_ref)
            dxbuf[pl.ds(t, 8), :] = jnp.zeros((8, D), F32)
            gc[...] = jnp.zeros((8, D), F32)

        live = jnp.where(ti > 0, 1.0, 0.0)
        xbuf[pl.ds(0, 8), :] = lxh_ref[...].astype(F32)[8:16] * live
        xbuf[pl.ds(8, t), :] = lx_ref[...].astype(F32)
        hbuf[pl.ds(0, 8), :] = rech_ref[...] * live
        hbuf[pl.ds(8, t), :] = rec_ref[...]
        first = (lax.broadcasted_iota(jnp.int32, (t, 128), 0) + ti * t) == 0
        for b in range(N_LRU_BLOCKS):
            cs = slice(b * 128, (b + 1) * 128)
            _, xc, _, r, ig, _, a, mult = _lru_block_fwd(xbuf, lv_ref, wa_ref, wx_ref, b, t, first)
            a_s[:, cs] = a
            xc_s[:, cs] = xc
            r_s[:, cs] = r
            ig_s[:, cs] = ig
            mu_s[:, cs] = mult

        def step(q, g):
            tt = t - 1 - q
            dh = dr_ref[pl.ds(tt, 1), :] + g
            dh_s[pl.ds(tt, 1), :] = dh
            return a_s[pl.ds(tt, 1), :] * dh

        gc[0:1, :] = lax.fori_loop(0, t, step, gc[0:1, :], unroll=8)
        for b in range(N_LRU_BLOCKS):
            cs = slice(b * 128, (b + 1) * 128)
            dh = dh_s[:, cs]
            a = a_s[:, cs]
            xc = xc_s[:, cs]
            r = r_s[:, cs]
            ig = ig_s[:, cs]
            mult = mu_s[:, cs]
            sp = _softplus(-lv_ref[L_LAM:L_LAM + 1, cs])
            lam = lv_ref[L_LAM:L_LAM + 1, cs]
            da = dh * hbuf[pl.ds(7, t), cs]
            dmult = jnp.where(first, 0.0, dh * ig * xc)
            dig = dh * mult * xc
            dxc = dh * mult * ig
            dlog_a = da * a - dmult * (a * a) / mult
            dr = dlog_a * ((-LRU_C) * sp)
            dsp = jnp.sum(dlog_a * ((-LRU_C) * r), axis=0, keepdims=True)
            dza = dr * r * (1.0 - r)
            dzx = dig * ig * (1.0 - ig)
            dzab = dza.astype(BF16)
            dzxb = dzx.astype(BF16)
            xcb = xc.astype(BF16)
            dwa_ref[b] += _dot_tn(xcb, dzab)
            dwx_ref[b] += _dot_tn(xcb, dzxb)
            dxc = dxc + _dot_nt(dzab, wa_ref[b]) + _dot_nt(dzxb, wx_ref[b])
            sums_ref[L_LAM:L_LAM + 1, cs] += dsp * (-jax.nn.sigmoid(-lam))
            sums_ref[L_BA:L_BA + 1, cs] += jnp.sum(dza, axis=0, keepdims=True)
            sums_ref[L_BX:L_BX + 1, cs] += jnp.sum(dzx, axis=0, keepdims=True)
            sums_ref[L_CB:L_CB + 1, cs] += jnp.sum(dxc, axis=0, keepdims=True)
            for kk in range(4):
                sums_ref[kk:kk + 1, cs] += jnp.sum(dxc * xbuf[pl.ds(5 + kk, t), cs], axis=0, keepdims=True)
            dxbuf[pl.ds(0, t), cs] = dxc
            dlx = (lv_ref[3:4, cs] * dxc + lv_ref[2:3, cs] * dxbuf[pl.ds(1, t), cs]
                   + lv_ref[1:2, cs] * dxbuf[pl.ds(2, t), cs] + lv_ref[0:1, cs] * dxbuf[pl.ds(3, t), cs])
            dlx_ref[:, cs] = dlx.astype(BF16)
        dxbuf[pl.ds(t, 8), :] = dxbuf[pl.ds(0, 8), :]

    rev = lambda i: nt - 1 - i
    return pl.pallas_call(
        body, name="lru_bwd", grid=(nt,),
        in_specs=[pl.BlockSpec((t, D), lambda i: (rev(i), 0)),
                  pl.BlockSpec((16, D), lambda i: (jnp.maximum(rev(i) * (t // 16) - 1, 0), 0)),
                  pl.BlockSpec((t, D), lambda i: (rev(i), 0)),
                  pl.BlockSpec((8, D), lambda i: (jnp.maximum(rev(i) * (t // 8) - 1, 0), 0)),
                  pl.BlockSpec((t, D), lambda i: (rev(i), 0)),
                  pl.BlockSpec((8, D), lambda i: (0, 0)),
                  pl.BlockSpec((N_LRU_BLOCKS, 128, 128), lambda i: (0, 0, 0)),
                  pl.BlockSpec((N_LRU_BLOCKS, 128, 128), lambda i: (0, 0, 0)),
                  pl.BlockSpec(memory_space=pl.ANY)],
        out_specs=[pl.BlockSpec((t, D), lambda i: (rev(i), 0)),
                   pl.BlockSpec((8, D), lambda i: (0, 0)),
                   pl.BlockSpec((N_LRU_BLOCKS, 128, 128), lambda i: (0, 0, 0)),
                   pl.BlockSpec((N_LRU_BLOCKS, 128, 128), lambda i: (0, 0, 0))],
        out_shape=[SDS((s, IN_W), BF16), SDS((8, D), F32), SDS((N_LRU_BLOCKS, 128, 128), F32),
                   SDS((N_LRU_BLOCKS, 128, 128), F32)],
        scratch_shapes=[pltpu.VMEM((t + 8, D), F32), pltpu.VMEM((t + 8, D), F32), pltpu.VMEM((t + 8, D), F32)]
        + [pltpu.VMEM((t, D), F32)] * 6 + [pltpu.VMEM((8, D), F32)],
        input_output_aliases={8: 0},
        compiler_params=_cp(1))(proj, proj, rec, rec, drec, lvec, wa, wx, dproj)


def inproj_bwd(dproj, w_in, x, dx1, vecs):
    s = x.shape[0]
    tm, tk = min(512, s), IN_TILE
    nk = IN_W // tk
    per = IN_SHARD // tk

    def body(d_ref, w_ref, x_ref, dx1_ref, v_ref, gx_ref, sums_ref, acc):
        i, k = pl.program_id(0), pl.program_id(1)

        @pl.when((i == 0) & (k == 0))
        def _():
            sums_ref[...] = jnp.zeros_like(sums_ref)

        @pl.when(k == 0)
        def _():
            acc[...] = jnp.zeros_like(acc)

        acc[...] += _dot_nt(d_ref[...], w_ref[...])

        @pl.when(k == nk - 1)
        def _():
            g1 = v_ref[V_G1:V_G1 + 1, :]
            scale1 = v_ref[V_SCALE1:V_SCALE1 + 1, :]

            def sub(rb, carry):
                rs = pl.ds(pl.multiple_of(rb * SUB, SUB), SUB)
                dh = acc[rs, :]
                r1, xh = _rms_parts(x_ref[rs, :])
                sums_ref[0:1, :] += jnp.sum(dh, axis=0, keepdims=True)
                sums_ref[1:2, :] += jnp.sum(dh * (xh * g1), axis=0, keepdims=True)
                dxn = dh * (1.0 + scale1)
                sums_ref[2:3, :] += jnp.sum(dxn * xh, axis=0, keepdims=True)
                dxh = dxn * g1
                gx_ref[rs, :] = dx1_ref[rs, :] + r1 * (dxh - xh * jnp.mean(dxh * xh, axis=-1, keepdims=True))
                return carry

            lax.fori_loop(0, tm // SUB, sub, 0)

    return pl.pallas_call(
        body, name="inproj_bwd", grid=(s // tm, nk),
        in_specs=[pl.BlockSpec((tm, tk), lambda i, k: (i, k)),
                  pl.BlockSpec((None, D, tk), lambda i, k: (k // per, 0, k % per)),
                  pl.BlockSpec((tm, D), lambda i, k: (i, 0)), pl.BlockSpec((tm, D), lambda i, k: (i, 0)),
                  pl.BlockSpec((16, D), lambda i, k: (0, 0))],
        out_specs=[pl.BlockSpec((tm, D), lambda i, k: (i, 0)), pl.BlockSpec((8, D), lambda i, k: (0, 0))],
        out_shape=[SDS((s, D), F32), SDS((8, D), F32)],
        scratch_shapes=[pltpu.VMEM((tm, D), F32)],
        compiler_params=_cp(2))(dproj, w_in, x, dx1, vecs)


def mod_columns(c16, w_ada, b_cols):
    tn = 512

    def body(c_ref, w_ref, b_ref, o_ref):
        cv = c_ref[...]
        ca = (cv * jax.nn.sigmoid(cv)).astype(BF16)
        o_ref[...] = _dot(ca, w_ref[...].astype(BF16)) + b_ref[...]

    return pl.pallas_call(
        body, name="mod_columns", grid=(ADA_SHARD // tn,),
        in_specs=[pl.BlockSpec((16, D), lambda j: (0, 0)), pl.BlockSpec((D, tn), lambda j: (0, j)),
                  pl.BlockSpec((1, tn), lambda j: (0, j))],
        out_specs=pl.BlockSpec((16, tn), lambda j: (0, j)),
        out_shape=SDS((16, ADA_SHARD), F32), compiler_params=_cp(1))(c16, w_ada, b_cols)


def wada_update(c16, dmod16, w, m, v):
    tm, tn = 512, 512

    def body(c_ref, d_ref, w_ref, m_ref, v_ref, g_out, dl_out, m_out, v_out):
        cv = c_ref[...]
        ca = (cv * jax.nn.sigmoid(cv)).astype(BF16)
        g = _dot_tn(ca, d_ref[...].astype(BF16))
        dl, m2, v2 = _adamw_math(w_ref[...], g, m_ref[...], v_ref[...])
        g_out[...] = g
        dl_out[...] = dl
        m_out[...] = m2
        v_out[...] = v2

    tile = pl.BlockSpec((tm, tn), lambda i, j: (i, j))
    return pl.pallas_call(
        body, name="wada_update", grid=(D // tm, ADA_SHARD // tn),
        in_specs=[pl.BlockSpec((16, tm), lambda i, j: (0, i)), pl.BlockSpec((16, tn), lambda i, j: (0, j)),
                  tile, tile, tile],
        out_specs=[tile] * 4, out_shape=[SDS((D, ADA_SHARD), F32)] * 4,
        compiler_params=_cp(2))(c16, dmod16, w, m, v)


def adamw_big(name, w, mine, theirs, m, v, c_idx):
    r, c = w.shape
    tr = 128
    per = (r // 2) // tr

    def body(c_ref, w_ref, a_ref, b_ref, m_ref, v_ref, g_out, dl_out, m_out, v_out):
        own = (pl.program_id(0) // per) == c_ref[0]
        g = jnp.where(own, a_ref[...], b_ref[...])
        dl, m2, v2 = _adamw_math(w_ref[...], g, m_ref[...], v_ref[...])
        g_out[...] = g
        dl_out[...] = dl
        m_out[...] = m2
        v_out[...] = v2

    tile = pl.BlockSpec((tr, c), lambda i, cr: (i, 0))
    half = pl.BlockSpec((tr, c), lambda i, cr: (i % per, 0))
    gs = pltpu.PrefetchScalarGridSpec(num_scalar_prefetch=1, grid=(r // tr,),
                                      in_specs=[tile, half, half, tile, tile], out_specs=[tile] * 4)
    return pl.pallas_call(body, name=name, grid_spec=gs, out_shape=[SDS((r, c), F32)] * 4,
                          compiler_params=_cp(1))(c_idx, w, mine, theirs, m, v)


def cast_into_slot(name, w, k_idx):
    r, c = w.shape
    tr = 256

    def body(k_ref, w_ref, o_ref):
        o_ref[...] = w_ref[...].astype(BF16)

    gs = pltpu.PrefetchScalarGridSpec(
        num_scalar_prefetch=1, grid=(r // tr,),
        in_specs=[pl.BlockSpec((tr, c), lambda i, kr: (i, 0))],
        out_specs=pl.BlockSpec((None, tr, c), lambda i, kr: (kr[0], i, 0)))
    return pl.pallas_call(body, name=name, grid_spec=gs, out_shape=SDS((N_CHIPS, r, c), BF16),
                          compiler_params=_cp(1))(k_idx, w)


def adamw_small(ws, gs, ms, vs):
    n = len(ws)

    def body(*refs):
        for i in range(n):
            dl, m2, v2 = _adamw_math(refs[i][...], refs[n + i][...], refs[2 * n + i][...], refs[3 * n + i][...])
            refs[4 * n + i][...] = dl
            refs[5 * n + i][...] = m2
            refs[6 * n + i][...] = v2

    vm = pl.BlockSpec(memory_space=pltpu.VMEM)
    shapes = [SDS(w.shape, F32) for w in ws]
    outs = pl.pallas_call(
        body, name="adamw_small", in_specs=[vm] * (4 * n), out_specs=[vm] * (3 * n), out_shape=shapes * 3,
        compiler_params=pltpu.CompilerParams(vmem_limit_bytes=VMEM_LIMIT))(*ws, *gs, *ms, *vs)
    return outs[:n], outs[n:2 * n], outs[2 * n:]


def sum_devices(gathered):
    rows = gathered.shape[1]
    tr = 96

    def body(x_ref, o_ref):
        acc = x_ref[0]
        for d in range(1, N_DEV):
            acc = acc + x_ref[d]
        o_ref[...] = acc

    return pl.pallas_call(
        body, name="sum_devices", grid=(rows // tr,),
        in_specs=[pl.BlockSpec((N_DEV, tr, D), lambda i: (0, i, 0))],
        out_specs=pl.BlockSpec((tr, D), lambda i: (i, 0)),
        out_shape=SDS((rows, D), F32), compiler_params=_cp(1))(gathered)


def _mesh_pos():
    return lax.axis_index("x"), lax.axis_index("y"), lax.axis_index("c")


def _other_chips(x, y):
    return [(1 - x, y), (x, 1 - y), (1 - x, 1 - y)]


def all_gather_small(name, block):
    m_per, n = block.shape

    def body(x_ref, out_ref, send_sems, recv_sems, local_sem):
        x, y, c = _mesh_pos()
        me, sibling = (x, y, c), (x, y, 1 - c)
        chips = _other_chips(x, y)

        def rows(px, py, pc):
            return out_ref.at[pl.ds((4 * px + 2 * py + pc) * m_per, m_per), :]

        def copy(k, blk, to, src=None):
            return pltpu.make_async_remote_copy(
                src_ref=rows(*blk) if src is None else src, dst_ref=rows(*blk),
                send_sem=send_sems.at[k], recv_sem=recv_sems.at[k], device_id=to, device_id_type=MESH)

        mine = pltpu.make_async_copy(x_ref, rows(*me), local_sem)
        mine.start()
        first = [copy(0, me, sibling, src=x_ref)]
        first += [copy(1 + j, me, (*chip, c), src=x_ref) for j, chip in enumerate(chips)]
        for cp in first:
            cp.start()
        passed = [copy(4 + j, (*chip, c), sibling) for j, chip in enumerate(chips)]
        for j, chip in enumerate(chips):
            copy(1 + j, (*chip, c), me).wait_recv()
            passed[j].start()
        copy(0, sibling, me).wait_recv()
        for j, chip in enumerate(chips):
            copy(4 + j, (*chip, 1 - c), me).wait_recv()
        for cp in first + passed:
            cp.wait_send()
        mine.wait()

    vm = pl.BlockSpec(memory_space=pltpu.VMEM)
    return pl.pallas_call(
        body, name=name, out_shape=SDS((N_DEV * m_per, n), block.dtype), in_specs=[vm], out_specs=vm,
        scratch_shapes=[pltpu.SemaphoreType.DMA((7,)), pltpu.SemaphoreType.DMA((7,)), pltpu.SemaphoreType.DMA],
        compiler_params=pltpu.CompilerParams(vmem_limit_bytes=VMEM_LIMIT))(block)


def all_gather_weights(bufs):
    n = len(bufs)
    halves = [w.shape[1] // 2 for w in bufs]

    def body(*refs):
        outs = refs[n:2 * n]
        send_sems, recv_sems = refs[2 * n:]
        x, y, c = _mesh_pos()
        k = 2 * x + y
        me, sibling = (x, y, c), (x, y, 1 - c)
        chips = _other_chips(x, y)

        def region(i, chip_idx, half):
            return outs[i].at[chip_idx, pl.ds(half * halves[i], halves[i]), :]

        def copy(i, j, reg, to):
            return pltpu.make_async_remote_copy(src_ref=reg, dst_ref=reg, send_sem=send_sems.at[i, j],
                                                recv_sem=recv_sems.at[i, j], device_id=to, device_id_type=MESH)

        started = []
        for i in range(n):
            for j, chip in enumerate(chips):
                cp = copy(i, j, region(i, k, c), (*chip, c))
                cp.start()
                started.append(cp)
        for i in range(n):
            for j, chip in enumerate(chips):
                kj = 2 * chip[0] + chip[1]
                copy(i, j, region(i, kj, c), me).wait_recv()
                cp = copy(i, 3 + j, region(i, kj, c), sibling)
                cp.start()
                started.append(cp)
        for i in range(n):
            for j, chip in enumerate(chips):
                kj = 2 * chip[0] + chip[1]
                copy(i, 3 + j, region(i, kj, 1 - c), me).wait_recv()
        for cp in started:
            cp.wait_send()

    hbm = pl.BlockSpec(memory_space=pl.ANY)
    return pl.pallas_call(
        body, name="all_gather_weights", in_specs=[hbm] * n, out_specs=[hbm] * n,
        out_shape=[SDS(w.shape, w.dtype) for w in bufs],
        input_output_aliases={i: i for i in range(n)},
        scratch_shapes=[pltpu.SemaphoreType.DMA((n, 6)), pltpu.SemaphoreType.DMA((n, 6))])(*bufs)


def sibling_exchange(name, grads):
    n = len(grads)
    halves = [g.shape[1] // 2 for g in grads]

    def body(*refs):
        ins, outs = refs[:n], refs[n:2 * n]
        send_sems, recv_sems = refs[2 * n:]
        x, y, c = _mesh_pos()
        sibling = (x, y, 1 - c)
        cps = []
        for i in range(n):
            cp = pltpu.make_async_remote_copy(
                src_ref=ins[i].at[:, pl.ds((1 - c) * halves[i], halves[i]), :], dst_ref=outs[i],
                send_sem=send_sems.at[i], recv_sem=recv_sems.at[i], device_id=sibling, device_id_type=MESH)
            cp.start()
            cps.append(cp)
        for cp in cps:
            cp.wait_recv()
        for cp in cps:
            cp.wait_send()

    hbm = pl.BlockSpec(memory_space=pl.ANY)
    return pl.pallas_call(
        body, name=name, in_specs=[hbm] * n, out_specs=[hbm] * n,
        out_shape=[SDS((N_CHIPS, h, g.shape[2]), F32) for g, h in zip(grads, halves)],
        scratch_shapes=[pltpu.SemaphoreType.DMA((n,)), pltpu.SemaphoreType.DMA((n,))])(*grads)


def sibling_sum(name, grad, other, c_idx):
    _, r, cc = grad.shape
    h = r // 2
    tr = min(256, h)
    g4 = grad.reshape(N_CHIPS, 2, h, cc)

    def body(c_ref, a_ref, b_ref, o_ref):
        o_ref[...] = (a_ref[...] + b_ref[...]).astype(BF16)

    gs = pltpu.PrefetchScalarGridSpec(
        num_scalar_prefetch=1, grid=(N_CHIPS, h // tr),
        in_specs=[pl.BlockSpec((None, None, tr, cc), lambda s, i, cr: (s, cr[0], i, 0)),
                  pl.BlockSpec((None, tr, cc), lambda s, i, cr: (s, i, 0))],
        out_specs=pl.BlockSpec((None, tr, cc), lambda s, i, cr: (s, i, 0)))
    return pl.pallas_call(body, name=name, grid_spec=gs, out_shape=SDS((N_CHIPS, h, cc), BF16),
                          compiler_params=_cp(2))(c_idx, g4, other)


HBM_SPEC = pl.BlockSpec(memory_space=pltpu.HBM)
SEM_SPEC = pl.BlockSpec(memory_space=pltpu.SEMAPHORE)


def _side_effecting():
    return pltpu.CompilerParams(has_side_effects=pltpu.SideEffectType.DATAFLOW_SIDE_EFFECTING)


def _in_hbm(a):
    return pltpu.with_memory_space_constraint(a, pltpu.HBM)


def gather_start(bufs):
    n = len(bufs)
    halves = [w.shape[1] // 2 for w in bufs]

    def body(*refs):
        ins = refs[:n]
        send_sems, recv_sems, token = refs[n], refs[n + 1], refs[-1]
        x, y, c = _mesh_pos()
        k = 2 * x + y
        for i in range(n):
            reg = ins[i].at[k, pl.ds(c * halves[i], halves[i]), :]
            for j, chip in enumerate(_other_chips(x, y)):
                pltpu.make_async_remote_copy(src_ref=reg, dst_ref=reg, send_sem=send_sems.at[3 * i + j],
                                             recv_sem=recv_sems.at[3 * i + j], device_id=(*chip, c),
                                             device_id_type=MESH).start()
        token[...] = jnp.zeros_like(token)

    outs = pl.pallas_call(
        body, name="gather_start",
        out_shape=(pltpu.SemaphoreType.DMA((3 * n,)), pltpu.SemaphoreType.DMA((3 * n,)),
                   *[pltpu.HBM(w.shape, w.dtype) for w in bufs], SDS((8, 128), F32)),
        in_specs=[HBM_SPEC] * n,
        out_specs=(SEM_SPEC, SEM_SPEC, *[HBM_SPEC] * n, pl.BlockSpec(memory_space=pltpu.VMEM)),
        input_output_aliases={i: 2 + i for i in range(n)},
        compiler_params=_side_effecting())(*[_in_hbm(w) for w in bufs])
    return outs[0], outs[1], list(outs[2:2 + n]), outs[-1]


def gather_wait(send_sems, recv_sems, bufs, after):
    n = len(bufs)
    halves = [w.shape[1] // 2 for w in bufs]

    def body(*refs):
        ins = refs[:n]
        send_sems, recv_sems = refs[n], refs[n + 1]
        x, y, c = _mesh_pos()
        k = 2 * x + y
        for i in range(n):
            for j, chip in enumerate(_other_chips(x, y)):
                kj = 2 * chip[0] + chip[1]
                cp = pltpu.make_async_remote_copy(
                    src_ref=ins[i].at[k, pl.ds(c * halves[i], halves[i]), :],
                    dst_ref=ins[i].at[kj, pl.ds(c * halves[i], halves[i]), :],
                    send_sem=send_sems.at[3 * i + j], recv_sem=recv_sems.at[3 * i + j], device_id=(*chip, c),
                    device_id_type=MESH)
                cp.wait_send()
                cp.wait_recv()

    return pl.pallas_call(
        body, name="gather_wait", out_shape=[pltpu.HBM(w.shape, w.dtype) for w in bufs],
        in_specs=[HBM_SPEC] * n + [SEM_SPEC, SEM_SPEC, pl.BlockSpec(memory_space=pl.ANY)],
        out_specs=[HBM_SPEC] * n, input_output_aliases={i: i for i in range(n)},
        compiler_params=_side_effecting())(*bufs, send_sems, recv_sems, after)


def gather_forward(bufs):
    n = len(bufs)
    halves = [w.shape[1] // 2 for w in bufs]

    def body(*refs):
        outs = refs[n:2 * n]
        send_sems, recv_sems = refs[2 * n:]
        x, y, c = _mesh_pos()
        chips = _other_chips(x, y)

        def copy(i, j, half, to):
            kj = 2 * chips[j][0] + chips[j][1]
            reg = outs[i].at[kj, pl.ds(half * halves[i], halves[i]), :]
            return pltpu.make_async_remote_copy(src_ref=reg, dst_ref=reg, send_sem=send_sems.at[i, j],
                                                recv_sem=recv_sems.at[i, j], device_id=to, device_id_type=MESH)

        cps = [copy(i, j, c, (x, y, 1 - c)) for i in range(n) for j in range(3)]
        for cp in cps:
            cp.start()
        for i in range(n):
            for j in range(3):
                copy(i, j, 1 - c, (x, y, c)).wait_recv()
        for cp in cps:
            cp.wait_send()

    hbm = pl.BlockSpec(memory_space=pl.ANY)
    return pl.pallas_call(
        body, name="gather_forward", in_specs=[hbm] * n, out_specs=[hbm] * n,
        out_shape=[SDS(w.shape, w.dtype) for w in bufs], input_output_aliases={i: i for i in range(n)},
        scratch_shapes=[pltpu.SemaphoreType.DMA((n, 3)), pltpu.SemaphoreType.DMA((n, 3))])(*bufs)


def exchange_start(name, sums):
    n = len(sums)
    lands = [lax.empty((3,) + t.shape[1:], t.dtype) for t in sums]

    def body(*refs):
        ins, zones = refs[:n], refs[n:2 * n]
        send_sems, recv_sems, token = refs[2 * n], refs[2 * n + 1], refs[-1]
        x, y, c = _mesh_pos()
        for i in range(n):
            for j, chip in enumerate(_other_chips(x, y)):
                kj = 2 * chip[0] + chip[1]
                pltpu.make_async_remote_copy(src_ref=ins[i].at[kj], dst_ref=zones[i].at[j],
                                             send_sem=send_sems.at[3 * i + j], recv_sem=recv_sems.at[3 * i + j],
                                             device_id=(*chip, c), device_id_type=MESH).start()
        token[...] = jnp.zeros_like(token)

    outs = pl.pallas_call(
        body, name=name,
        out_shape=(pltpu.SemaphoreType.DMA((3 * n,)), pltpu.SemaphoreType.DMA((3 * n,)),
                   *[pltpu.HBM(t.shape, t.dtype) for t in sums], *[pltpu.HBM(t.shape, t.dtype) for t in lands],
                   SDS((8, 128), F32)),
        in_specs=[HBM_SPEC] * (2 * n),
        out_specs=(SEM_SPEC, SEM_SPEC, *[HBM_SPEC] * (2 * n), pl.BlockSpec(memory_space=pltpu.VMEM)),
        input_output_aliases={i: 2 + i for i in range(2 * n)},
        compiler_params=_side_effecting())(*[_in_hbm(t) for t in sums], *[_in_hbm(t) for t in lands])
    return outs[0], outs[1], list(outs[2:2 + n]), list(outs[2 + n:2 + 2 * n]), outs[-1]


def exchange_wait(name, send_sems, recv_sems, sums, lands, after):
    n = len(sums)

    def body(*refs):
        ins, zones = refs[:n], refs[n:2 * n]
        send_sems, recv_sems = refs[2 * n], refs[2 * n + 1]
        x, y, c = _mesh_pos()
        for i in range(n):
            for j, chip in enumerate(_other_chips(x, y)):
                kj = 2 * chip[0] + chip[1]
                cp = pltpu.make_async_remote_copy(src_ref=ins[i].at[kj], dst_ref=zones[i].at[j],
                                                  send_sem=send_sems.at[3 * i + j], recv_sem=recv_sems.at[3 * i + j],
                                                  device_id=(*chip, c), device_id_type=MESH)
                cp.wait_send()
                cp.wait_recv()

    outs = pl.pallas_call(
        body, name=name, out_shape=[pltpu.HBM(t.shape, t.dtype) for t in sums + lands],
        in_specs=[HBM_SPEC] * (2 * n) + [SEM_SPEC, SEM_SPEC, pl.BlockSpec(memory_space=pl.ANY)],
        out_specs=[HBM_SPEC] * (2 * n), input_output_aliases={i: i for i in range(2 * n)},
        compiler_params=_side_effecting())(*sums, *lands, send_sems, recv_sems, after)
    return list(outs[:n]), list(outs[n:])


def chip_sum(name, sums, parts, k_idx):
    _, h, cc = parts.shape
    tr = min(256, h)

    def body(k_ref, own_ref, p_ref, o_ref):
        acc = own_ref[...].astype(F32)
        for s in range(3):
            acc = acc + p_ref[s].astype(F32)
        o_ref[...] = acc

    gs = pltpu.PrefetchScalarGridSpec(
        num_scalar_prefetch=1, grid=(h // tr,),
        in_specs=[pl.BlockSpec((None, tr, cc), lambda i, kr: (kr[0], i, 0)),
                  pl.BlockSpec((3, tr, cc), lambda i, kr: (0, i, 0))],
        out_specs=pl.BlockSpec((tr, cc), lambda i, kr: (i, 0)))
    return pl.pallas_call(body, name=name, grid_spec=gs, out_shape=SDS((h, cc), F32),
                          compiler_params=_cp(1))(k_idx, sums, parts)


def halves_exchange(name, halves):
    n = len(halves)

    def body(*refs):
        ins, outs = refs[:n], refs[n:2 * n]
        send_sems, recv_sems = refs[2 * n:]
        x, y, c = _mesh_pos()
        cps = []
        for i in range(n):
            cp = pltpu.make_async_remote_copy(
                src_ref=ins[i], dst_ref=outs[i], send_sem=send_sems.at[i], recv_sem=recv_sems.at[i],
                device_id=(x, y, 1 - c), device_id_type=MESH)
            cp.start()
            cps.append(cp)
        for cp in cps:
            cp.wait_recv()
        for cp in cps:
            cp.wait_send()

    hbm = pl.BlockSpec(memory_space=pl.ANY)
    return pl.pallas_call(
        body, name=name, in_specs=[hbm] * n, out_specs=[hbm] * n,
        out_shape=[SDS(t.shape, F32) for t in halves],
        scratch_shapes=[pltpu.SemaphoreType.DMA((n,)), pltpu.SemaphoreType.DMA((n,))])(*halves)


def local_step(x, tgt, vecs, lvec, wa, wx, sinks, rel_bias, w_in, rest_weights, grads_out, grad_in_out):
    buckets = t5_bucket_table()
    band = bias_band(rel_bias.T, buckets).reshape(N_HEADS, BLOCK, 2 * BLOCK)

    proj, h = inproj_fwd(x, vecs, w_in)
    ya, rec = lru_fwd(proj, lvec, wa, wx)
    att = attn_fwd(proj, band, sinks)
    w_lru_out, w_attn_out, w_out, w_ff1, w_ff2 = rest_weights(att)
    w_lru_out2, w_attn_out2, w_out2 = w_lru_out.reshape(D, D), w_attn_out.reshape(D, D), w_out.reshape(D, D)
    w_ff2_2 = w_ff2.reshape(D_FF, D)
    yab, merged = merge_fwd(ya, att, w_lru_out2, w_attn_out2, proj)
    x1, o1 = outproj_fwd(merged, w_out2, x, vecs)
    f, h2 = ff1_fwd(x1, vecs, w_ff1)
    dx2, do2, sums_f, loss = ff2_loss(f, w_ff2_2, x1, tgt, vecs)

    df = ff2_bwd(do2, w_ff2_2, f)
    g_ff2 = weight_grad("dw_ff2", f, do2, 512, (D_FF, D), (WG_TM, 512), lambda i, j: (i, j), relu2=True)
    dx1, do1, sums_2 = ff1_bwd(df, w_ff1, x1, dx2, o1, vecs)
    g_ff1 = weight_grad("dw_ff1", h2, df, 512, (N_CHIPS, D, D), (None, WG_TM, 512), lambda i, j: (j // 4, i, j % 4))
    dyab, dproj = outproj_bwd(do1, w_out2, yab, proj)
    g_out = weight_grad("dw_out", merged, do1, 512, (D, D), (WG_TM, 512), lambda i, j: (i, j))
    drec, dproj = lruout_bwd(dyab, w_lru_out2, rec, proj, dproj)
    g_lru_out = weight_grad("dw_lru_out", ya, dyab[0], 512, (D, D), (WG_TM, 512), lambda i, j: (i, j))
    datt = attnout_bwd(dyab, w_attn_out2)
    g_attn_out = weight_grad("dw_attn_out", att, dyab[1], 512, (D, D), (WG_TM, 512), lambda i, j: (i, j))
    zero = grads_out([g_lru_out.reshape(N_CHIPS, D // 4, D), g_attn_out.reshape(N_CHIPS, D // 4, D),
                      g_out.reshape(N_CHIPS, D // 4, D), g_ff1, g_ff2.reshape(N_CHIPS, D_FF // 4, D)])
    dproj, dkb, dvb, dband, dsink = attn_bwd(proj, band, sinks + zero, datt, dproj)
    dproj = dkv_combine(dkb, dvb, dproj)
    dproj, sums_l, d_wa, d_wx = lru_bwd(proj, rec, drec, lvec, wa, wx, dproj)
    per = IN_SHARD // IN_TILE
    g_in = weight_grad("dw_in", h, dproj, IN_TILE, (N_CHIPS, D, IN_SHARD), (None, WG_TM, IN_TILE),
                       lambda i, j: (j // per, i, j % per))
    zero = grad_in_out(g_in)
    grad_x, sums_1 = inproj_bwd(dproj, w_in, x, dx1, vecs + zero)
    d_rel_bias = bias_band_bwd(dband.reshape(N_HEADS, BLOCK * 2 * BLOCK), buckets)

    small = dict(sums_f=sums_f, sums_2=sums_2, sums_1=sums_1, sums_l=sums_l, d_wa=d_wa, d_wx=d_wx,
                 d_sinks=dsink[:, 0], d_rel_bias=d_rel_bias)
    return loss, grad_x, small


def _pad_rows(a, rows):
    return jnp.concatenate([a, jnp.zeros((rows - a.shape[0], a.shape[1]), a.dtype)], axis=0)


def kernel(x, c, w_ada, b_ada, norm1_g, w_in, conv_w, conv_b, lru_wa, lru_ba, lru_wx, lru_bx, lru_lambda, w_lru_out, w_attn_out, attn_sinks, rel_bias, w_out, norm2_g, w_ff1, w_ff2, final_g, loss_target, m_w_ada, m_b_ada, m_norm1_g, m_w_in, m_conv_w, m_conv_b, m_lru_wa, m_lru_ba, m_lru_wx, m_lru_bx, m_lru_lambda, m_w_lru_out, m_w_attn_out, m_attn_sinks, m_rel_bias, m_w_out, m_norm2_g, m_w_ff1, m_w_ff2, m_final_g, v_w_ada, v_b_ada, v_norm1_g, v_w_in, v_conv_w, v_conv_b, v_lru_wa, v_lru_ba, v_lru_wx, v_lru_bx, v_lru_lambda, v_w_lru_out, v_w_attn_out, v_attn_sinks, v_rel_bias, v_w_out, v_norm2_g, v_w_ff1, v_w_ff2, v_final_g):
    xi, yi, ci = _mesh_pos()
    chip = 2 * xi + yi
    dev = 2 * chip + ci
    z8 = jnp.zeros((8, D), F32)

    conv_rows = jnp.concatenate([conv_w[0], jnp.zeros((4, D - D // 4), F32)], axis=1)
    pack0 = jnp.concatenate([c, conv_rows, jnp.zeros((3, D), F32)], axis=0)
    g0 = all_gather_small("gather_cond", pack0).reshape(N_DEV, 8, D)
    c_all = g0[:, 0, :]
    conv_full = jnp.concatenate([g0[2 * k, 1:5, :D // 4] for k in range(N_CHIPS)], axis=1)
    c16 = jnp.concatenate([c_all, z8], axis=0)
    b_cols = lax.dynamic_slice_in_dim(b_ada, chip * ADA_SHARD, ADA_SHARD, axis=1)
    mod_c = mod_columns(c16, w_ada[0], b_cols)
    g1 = all_gather_small("gather_mod", mod_c).reshape(N_DEV, 16, ADA_SHARD)
    mod = jnp.concatenate([lax.dynamic_index_in_dim(g1[2 * k], dev, axis=0, keepdims=False) for k in range(N_CHIPS)])
    shift1, scale1, gate1, shift2, scale2, gate2 = [mod[i * D:(i + 1) * D] for i in range(6)]
    vecs = jnp.stack([norm1_g[0], scale1, shift1, gate1, norm2_g[0], scale2, shift2, gate2, final_g]
                     + [jnp.zeros((D,), F32)] * 7)
    lvec = jnp.concatenate([conv_full, conv_b, lru_ba, lru_bx, lru_lambda], axis=0)

    shards = [w_in[0], w_lru_out[0], w_attn_out[0], w_out[0], w_ff1[0], w_ff2[0]]
    names = ["w_in", "w_lru_out", "w_attn_out", "w_out", "w_ff1", "w_ff2"]
    k_idx = jnp.reshape(chip, (1,)).astype(jnp.int32)
    c_idx = jnp.reshape(ci, (1,)).astype(jnp.int32)
    slots = [cast_into_slot("cast_" + nm, w, k_idx) for nm, w in zip(names, shards)]
    w_in_full = all_gather_weights(slots[:1])[0]
    g_send, g_recv, in_flight, token = gather_start(slots[1:])
    vecs = vecs + token[0, 0]
    pending = {}

    def rest_weights(after):
        return gather_forward(gather_wait(g_send, g_recv, in_flight, after))

    def start_reduce(tag, nms, grads):
        from_sibling = sibling_exchange("sibling_exchange_" + tag, grads)
        sums = [sibling_sum("sibling_sum_" + nm, g, o, c_idx) for nm, g, o in zip(nms, grads, from_sibling)]
        pending[tag] = exchange_start("exchange_start_" + tag, sums)
        return pending[tag][-1][0, 0]

    loss_t, grad_x, small = local_step(
        x[0], loss_target[0], vecs, lvec, lru_wa[0].astype(BF16), lru_wx[0].astype(BF16),
        attn_sinks[0], rel_bias, w_in_full, rest_weights,
        lambda grads: start_reduce("a", names[1:], grads), lambda g: start_reduce("b", names[:1], [g]))
    loss = lax.psum(loss_t[0, 0], ("x", "y", "c"))

    sums_f, sums_2, sums_1, sums_l = small["sums_f"], small["sums_2"], small["sums_1"], small["sums_l"]
    vec_rows = jnp.stack([sums_1[2], sums_2[2], sums_f[0], sums_l[L_CB], sums_l[L_BA], sums_l[L_BX],
                          sums_l[L_LAM], jnp.zeros((D,), F32)])
    mod_rows = jnp.stack([sums_1[0], sums_1[1], sums_2[3], sums_2[0], sums_2[1], sums_f[1],
                          jnp.zeros((D,), F32), jnp.zeros((D,), F32)])
    att_rows = jnp.concatenate([
        jnp.concatenate([small["d_sinks"], jnp.zeros((D - N_HEADS,), F32)])[None],
        jnp.concatenate([small["d_rel_bias"].reshape(-1), jnp.zeros((D - N_BUCKETS * N_HEADS,), F32)])[None],
        jnp.zeros((6, D), F32)], axis=0)
    pack = jnp.concatenate([vec_rows, _pad_rows(sums_l[0:4], 8), mod_rows, att_rows,
                            small["d_wa"].reshape(128, D), small["d_wx"].reshape(128, D)], axis=0)
    gathered = all_gather_small("gather_small_grads", pack).reshape(N_DEV, P_ROWS, D)
    total = sum_devices(gathered)
    dmod_all = gathered[:, P_MOD:P_MOD + 6, :].reshape(N_DEV, 6 * D)
    dmod16 = jnp.concatenate([lax.dynamic_slice_in_dim(dmod_all, chip * ADA_SHARD, ADA_SHARD, axis=1),
                              jnp.zeros((8, ADA_SHARD), F32)], axis=0)
    g_w_ada, d_w_ada, nm_w_ada, nv_w_ada = wada_update(c16, dmod16, w_ada[0], m_w_ada[0], v_w_ada[0])

    mine, theirs = {}, {}
    for tag, nms in (("a", names[1:]), ("b", names[:1])):
        send_sems, recv_sems, sums, lands, _ = pending[tag]
        sums, lands = exchange_wait("exchange_wait_" + tag, send_sems, recv_sems, sums, lands, grad_x)
        halves = [chip_sum("chip_sum_" + nm, t, p, k_idx) for nm, t, p in zip(nms, sums, lands)]
        for nm, a, b in zip(nms, halves, halves_exchange("halves_exchange_" + tag, halves)):
            mine[nm], theirs[nm] = a, b
    big_m = [m_w_in, m_w_lru_out, m_w_attn_out, m_w_out, m_w_ff1, m_w_ff2]
    big_v = [v_w_in, v_w_lru_out, v_w_attn_out, v_w_out, v_w_ff1, v_w_ff2]
    g_big, d_big, nm_big, nv_big = {}, {}, {}, {}
    for nm, w, m, v in zip(names, shards, big_m, big_v):
        g2, dl, m2, v2 = adamw_big("adamw_" + nm, w, mine[nm], theirs[nm], m[0], v[0], c_idx)
        g_big[nm], d_big[nm], nm_big[nm], nv_big[nm] = g2[None], dl[None], m2[None], v2[None]

    conv_g = lax.dynamic_slice_in_dim(total[P_CONVW:P_CONVW + 4], chip * (D // 4), D // 4, axis=1)
    sm_names = ["b_ada", "norm1_g", "conv_w", "conv_b", "lru_wa", "lru_ba", "lru_wx", "lru_bx", "lru_lambda",
                "attn_sinks", "rel_bias", "norm2_g", "final_g"]
    sm_w = [b_ada.reshape(6, D), norm1_g, conv_w[0], conv_b, lru_wa.reshape(128, D), lru_ba, lru_wx.reshape(128, D),
            lru_bx, lru_lambda, attn_sinks, rel_bias, norm2_g, final_g[None]]
    sm_m = [m_b_ada.reshape(6, D), m_norm1_g, m_conv_w[0], m_conv_b, m_lru_wa.reshape(128, D), m_lru_ba,
            m_lru_wx.reshape(128, D), m_lru_bx, m_lru_lambda, m_attn_sinks, m_rel_bias, m_norm2_g, m_final_g[None]]
    sm_v = [v_b_ada.reshape(6, D), v_norm1_g, v_conv_w[0], v_conv_b, v_lru_wa.reshape(128, D), v_lru_ba,
            v_lru_wx.reshape(128, D), v_lru_bx, v_lru_lambda, v_attn_sinks, v_rel_bias, v_norm2_g, v_final_g[None]]
    sm_g = [total[P_MOD:P_MOD + 6], total[0:1], conv_g, total[3:4], total[P_WA:P_WA + 128], total[4:5],
            total[P_WX:P_WX + 128], total[5:6], total[6:7], total[P_ATT:P_ATT + 1, :N_HEADS],
            total[P_ATT + 1, :N_BUCKETS * N_HEADS].reshape(N_BUCKETS, N_HEADS), total[1:2], total[2:3]]
    sm_d, sm_nm, sm_nv = adamw_small(sm_w, sm_g, sm_m, sm_v)
    shapes = dict(b_ada=b_ada.shape, norm1_g=norm1_g.shape, conv_w=conv_w.shape, conv_b=conv_b.shape,
                  lru_wa=lru_wa.shape, lru_ba=lru_ba.shape, lru_wx=lru_wx.shape, lru_bx=lru_bx.shape,
                  lru_lambda=lru_lambda.shape, attn_sinks=attn_sinks.shape, rel_bias=rel_bias.shape,
                  norm2_g=norm2_g.shape, final_g=final_g.shape)
    grads = dict(w_ada=g_w_ada[None], **g_big)
    deltas = dict(w_ada=d_w_ada[None], **d_big)
    new_m = dict(w_ada=nm_w_ada[None], **nm_big)
    new_v = dict(w_ada=nv_w_ada[None], **nv_big)
    for i, nm in enumerate(sm_names):
        grads[nm] = sm_g[i].reshape(shapes[nm])
        deltas[nm] = sm_d[i].reshape(shapes[nm])
        new_m[nm] = sm_nm[i].reshape(shapes[nm])
        new_v[nm] = sm_nv[i].reshape(shapes[nm])
    order = ["w_ada", "b_ada", "norm1_g", "w_in", "conv_w", "conv_b", "lru_wa", "lru_ba", "lru_wx", "lru_bx",
             "lru_lambda", "w_lru_out", "w_attn_out", "attn_sinks", "rel_bias", "w_out", "norm2_g", "w_ff1", "w_ff2",
             "final_g"]
    return (loss, grad_x[None], *[grads[n] for n in order], *[deltas[n] for n in order],
            *[new_m[n] for n in order], *[new_v[n] for n in order])
```

```python
import math

import numpy as np
import jax
import jax.numpy as jnp
from jax import lax
from jax.experimental import pallas as pl
from jax.experimental.pallas import tpu as pltpu

F32 = jnp.float32
BF16 = jnp.bfloat16
SDS = jax.ShapeDtypeStruct
MESH = pl.DeviceIdType.MESH

D = 2048
D_FF = 4 * D
N_HEADS = 32
HEAD_DIM = 64
BLOCK = 128
N_LRU_BLOCKS = 16
LRU_C = 8.0
EPS = 1e-6
NEG_INF = -1e30
N_BUCKETS = 32
MAX_DISTANCE = 128
IN_W = 10752
IN_SHARD = IN_W // 4
IN_TILE = 896
ADA_SHARD = 6 * D // 4
OFF_LRU, OFF_GATE, OFF_Q, OFF_K, OFF_V, OFF_GA, OFF_GB = 0, 2048, 4096, 6144, 6400, 6656, 8704
SCALE = HEAD_DIM ** -0.5
N_CHIPS = 4
N_DEV = 8

ADAM_LR, ADAM_B1, ADAM_B2, ADAM_EPS, ADAM_WD, ADAM_STEP = 0.001, 0.9, 0.999, 1e-08, 0.01, 10
ADAM_C1 = 1.0 - ADAM_B1 ** ADAM_STEP
ADAM_C2 = 1.0 - ADAM_B2 ** ADAM_STEP

VMEM_LIMIT = 52 * 2 ** 20
SUB = 128
WG_TM = 1024
V_G1, V_SCALE1, V_SHIFT1, V_GATE1, V_G2, V_SCALE2, V_SHIFT2, V_GATE2, V_G3 = range(9)
L_CW0, L_CB, L_BA, L_BX, L_LAM = 0, 4, 5, 6, 7
P_VEC, P_CONVW, P_MOD, P_ATT, P_WA, P_WX, P_ROWS = 0, 8, 16, 24, 32, 160, 288


def _cp(n_axes):
    return pltpu.CompilerParams(dimension_semantics=("arbitrary",) * n_axes, vmem_limit_bytes=VMEM_LIMIT)


def _dot(a, b):
    return jnp.dot(a, b, preferred_element_type=F32)


def _dot_nt(a, b):
    return lax.dot_general(a, b, (((1,), (1,)), ((), ())), preferred_element_type=F32)


def _dot_tn(a, b):
    return lax.dot_general(a, b, (((0,), (0,)), ((), ())), preferred_element_type=F32)


_G0 = math.sqrt(2.0 / math.pi)
_G1 = 0.044715


def _gelu(x):
    return 0.5 * x * (1.0 + jnp.tanh(_G0 * (x + _G1 * x * x * x)))


def _gelu_grad(x):
    x2 = x * x
    t = jnp.tanh(_G0 * (x + _G1 * x * x2))
    return 0.5 * (1.0 + t) + 0.5 * x * (1.0 - t * t) * _G0 * (1.0 + 3.0 * _G1 * x2)


def _expm1(x):
    u = jnp.exp(x)
    um1 = u - 1.0
    k = um1 * x / jnp.log(u)
    return jnp.where(um1 == 0.0, x, jnp.where(u < 0.5, um1, k))


def _softplus(z):
    e = jnp.exp(-jnp.abs(z))
    u = 1.0 + e
    l1p = jnp.where(u == 1.0, e, jnp.log(u) * e / (u - 1.0))
    return jnp.maximum(z, 0.0) + l1p


def _adamw_math(w, g, m, v):
    m2 = ADAM_B1 * m + (1.0 - ADAM_B1) * g
    v2 = ADAM_B2 * v + (1.0 - ADAM_B2) * (g * g)
    m_hat = m2 / ADAM_C1
    v_hat = v2 / ADAM_C2
    delta = -ADAM_LR * (m_hat / (jnp.sqrt(v_hat) + ADAM_EPS) + ADAM_WD * w)
    return delta, m2, v2


def _rms_parts(xv):
    r = lax.rsqrt(jnp.mean(xv * xv, axis=-1, keepdims=True) + EPS)
    return r, xv * r


def inproj_fwd(x, vecs, w_in):
    s = x.shape[0]
    tm = min(512, s)
    per = IN_SHARD // IN_TILE

    def body(x_ref, v_ref, w_ref, proj_ref, h_ref):
        @pl.when(pl.program_id(1) == 0)
        def _():
            _, xh = _rms_parts(x_ref[...])
            h = (xh * v_ref[V_G1:V_G1 + 1, :]) * (1.0 + v_ref[V_SCALE1:V_SCALE1 + 1, :]) + v_ref[V_SHIFT1:V_SHIFT1 + 1, :]
            h_ref[...] = h.astype(BF16)
        proj_ref[...] = _dot(h_ref[...], w_ref[...]).astype(BF16)

    return pl.pallas_call(
        body, name="inproj_fwd", grid=(s // tm, IN_W // IN_TILE),
        in_specs=[pl.BlockSpec((tm, D), lambda i, j: (i, 0)),
                  pl.BlockSpec((16, D), lambda i, j: (0, 0)),
                  pl.BlockSpec((None, D, IN_TILE), lambda i, j: (j // per, 0, j % per))],
        out_specs=[pl.BlockSpec((tm, IN_TILE), lambda i, j: (i, j)),
                   pl.BlockSpec((tm, D), lambda i, j: (i, 0))],
        out_shape=[SDS((s, IN_W), BF16), SDS((s, D), BF16)],
        compiler_params=_cp(2))(x, vecs, w_in)


def _lru_block_fwd(xbuf, lv_ref, wa_ref, wx_ref, b, t, first):
    cs = slice(b * 128, (b + 1) * 128)
    x0 = xbuf[pl.ds(8, t), cs]
    x1 = xbuf[pl.ds(7, t), cs]
    x2 = xbuf[pl.ds(6, t), cs]
    x3 = xbuf[pl.ds(5, t), cs]
    xc = (lv_ref[L_CB:L_CB + 1, cs] + lv_ref[3:4, cs] * x0 + lv_ref[2:3, cs] * x1
          + lv_ref[1:2, cs] * x2 + lv_ref[0:1, cs] * x3)
    xcb = xc.astype(BF16)
    r = jax.nn.sigmoid(_dot(xcb, wa_ref[b]) + lv_ref[L_BA:L_BA + 1, cs])
    ig = jax.nn.sigmoid(_dot(xcb, wx_ref[b]) + lv_ref[L_BX:L_BX + 1, cs])
    sp = _softplus(-lv_ref[L_LAM:L_LAM + 1, cs])
    log_a = (-LRU_C) * r * sp
    a = jnp.exp(log_a)
    mult = jnp.where(first, 1.0, jnp.sqrt(-_expm1(2.0 * log_a)))
    return (x0, x1, x2, x3), xc, xcb, r, ig, sp, a, mult


def lru_fwd(proj, lvec, wa, wx):
    s = proj.shape[0]
    t = min(256, s)

    def body(lx_ref, gate_ref, lv_ref, wa_ref, wx_ref, ya_ref, rec_ref, xbuf, a_s, u_s, hc):
        i = pl.program_id(0)

        @pl.when(i == 0)
        def _():
            xbuf[pl.ds(0, 8), :] = jnp.zeros((8, D), F32)
            hc[...] = jnp.zeros((8, D), F32)

        @pl.when(i > 0)
        def _():
            xbuf[pl.ds(0, 8), :] = xbuf[pl.ds(t, 8), :]

        xbuf[pl.ds(8, t), :] = lx_ref[...].astype(F32)
        first = (lax.broadcasted_iota(jnp.int32, (t, 128), 0) + i * t) == 0
        for b in range(N_LRU_BLOCKS):
            cs = slice(b * 128, (b + 1) * 128)
            _, xc, _, _, ig, _, a, mult = _lru_block_fwd(xbuf, lv_ref, wa_ref, wx_ref, b, t, first)
            a_s[:, cs] = a
            u_s[:, cs] = mult * (ig * xc)

        def step(tt, h):
            h = a_s[pl.ds(tt, 1), :] * h + u_s[pl.ds(tt, 1), :]
            rec_ref[pl.ds(tt, 1), :] = h
            return h

        hc[0:1, :] = lax.fori_loop(0, t, step, hc[0:1, :], unroll=8)
        for b in range(N_LRU_BLOCKS):
            cs = slice(b * 128, (b + 1) * 128)
            ya_ref[:, cs] = (rec_ref[:, cs] * _gelu(gate_ref[:, cs].astype(F32))).astype(BF16)

    return pl.pallas_call(
        body, name="lru_fwd", grid=(s // t,),
        in_specs=[pl.BlockSpec((t, D), lambda i: (i, OFF_LRU // D)),
                  pl.BlockSpec((t, D), lambda i: (i, OFF_GATE // D)),
                  pl.BlockSpec((8, D), lambda i: (0, 0)),
                  pl.BlockSpec((N_LRU_BLOCKS, 128, 128), lambda i: (0, 0, 0)),
                  pl.BlockSpec((N_LRU_BLOCKS, 128, 128), lambda i: (0, 0, 0))],
        out_specs=[pl.BlockSpec((t, D), lambda i: (i, 0)), pl.BlockSpec((t, D), lambda i: (i, 0))],
        out_shape=[SDS((s, D), BF16), SDS((s, D), F32)],
        scratch_shapes=[pltpu.VMEM((t + 8, D), F32), pltpu.VMEM((t, D), F32), pltpu.VMEM((t, D), F32),
                        pltpu.VMEM((8, D), F32)],
        compiler_params=_cp(1))(proj, proj, lvec, wa, wx)


def t5_bucket_table():
    qi = np.arange(BLOCK)[:, None]
    ki = np.arange(2 * BLOCK)[None, :]
    rel = qi + BLOCK - ki
    relc = np.maximum(rel, 0)
    max_exact = N_BUCKETS // 2
    relf = np.maximum(relc, 1).astype(np.float32)
    large = max_exact + (np.log(relf / np.float32(max_exact)) / np.float32(math.log(MAX_DISTANCE / max_exact))
                         * np.float32(N_BUCKETS - max_exact)).astype(np.int32)
    large = np.minimum(large, N_BUCKETS - 1)
    bucket = np.where(relc < max_exact, relc, large)
    bucket = np.where((rel >= 0) & (rel < BLOCK), bucket, -1)
    return jnp.asarray(bucket.reshape(1, BLOCK * 2 * BLOCK), jnp.int32)


def bias_band(rel_bias_t, buckets):
    n = BLOCK * 2 * BLOCK
    tn = 4096

    def body(bk_ref, rb_ref, o_ref):
        row = lax.broadcasted_iota(jnp.int32, (N_BUCKETS, tn), 0)
        oh = jnp.where(row == bk_ref[...], 1.0, 0.0).astype(BF16)
        rb = rb_ref[...]
        p0 = rb.astype(BF16)
        r1 = rb - p0.astype(F32)
        p1 = r1.astype(BF16)
        p2 = (r1 - p1.astype(F32)).astype(BF16)
        o_ref[...] = _dot(p0, oh) + _dot(p1, oh) + _dot(p2, oh)

    return pl.pallas_call(
        body, name="bias_band", grid=(n // tn,),
        in_specs=[pl.BlockSpec((1, tn), lambda i: (0, i)), pl.BlockSpec((N_HEADS, N_BUCKETS), lambda i: (0, 0))],
        out_specs=pl.BlockSpec((N_HEADS, tn), lambda i: (0, i)),
        out_shape=SDS((N_HEADS, n), F32), compiler_params=_cp(1))(buckets, rel_bias_t)


def bias_band_bwd(dband, buckets):
    n = BLOCK * 2 * BLOCK
    tn = 4096

    def body(bk_ref, d_ref, o_ref):
        @pl.when(pl.program_id(0) == 0)
        def _():
            o_ref[...] = jnp.zeros_like(o_ref)
        row = lax.broadcasted_iota(jnp.int32, (N_BUCKETS, tn), 0)
        oh = jnp.where(row == bk_ref[...], 1.0, 0.0).astype(BF16)
        dv = d_ref[...]
        p0 = dv.astype(BF16)
        r1 = dv - p0.astype(F32)
        p1 = r1.astype(BF16)
        p2 = (r1 - p1.astype(F32)).astype(BF16)
        o_ref[...] += _dot_nt(oh, p0) + _dot_nt(oh, p1) + _dot_nt(oh, p2)

    return pl.pallas_call(
        body, name="bias_band_bwd", grid=(n // tn,),
        in_specs=[pl.BlockSpec((1, tn), lambda i: (0, i)), pl.BlockSpec((N_HEADS, tn), lambda i: (0, i))],
        out_specs=pl.BlockSpec((N_BUCKETS, N_HEADS), lambda i: (0, 0)),
        out_shape=SDS((N_BUCKETS, N_HEADS), F32), compiler_params=_cp(1))(buckets, dband)


def _dup_half(band, which):
    lane = lax.broadcasted_iota(jnp.int32, band.shape, 1)
    rolled = pltpu.roll(band, 64, 1)
    keep = (lane < 64) if which == 0 else (lane >= 64)
    return jnp.where(keep, band, rolled)


def _attn_probs(qm, kk, bias, sink, valid):
    sc = _dot_nt(qm, kk) * SCALE + bias
    sc = jnp.where(valid, sc, NEG_INF)
    m = jnp.maximum(jnp.max(sc, axis=-1, keepdims=True), sink)
    e = jnp.exp(sc - m)
    es = jnp.exp(sink - m)
    inv = 1.0 / (jnp.sum(e, axis=-1, keepdims=True) + es)
    return e * inv, es * inv


def _band_valid(n):
    qi = lax.broadcasted_iota(jnp.int32, (BLOCK, 2 * BLOCK), 0)
    ki = lax.broadcasted_iota(jnp.int32, (BLOCK, 2 * BLOCK), 1)
    rel = qi + BLOCK - ki
    return (rel >= 0) & (rel < BLOCK) & ((ki >= BLOCK) | (n > 0))


def _kv_bands(prev_ref, cur_ref):
    band = jnp.concatenate([prev_ref[...].astype(F32), cur_ref[...].astype(F32)], axis=0)
    return [_dup_half(band, 0).astype(BF16), _dup_half(band, 1).astype(BF16)]


def attn_fwd(proj, band, sinks):
    s = proj.shape[0]
    nb = s // BLOCK
    qw = 1024

    def body(sk_ref, q_ref, kp_ref, kc_ref, vp_ref, vc_ref, b_ref, o_ref):
        n = pl.program_id(0)
        gp = pl.program_id(1)
        valid = _band_valid(n)
        kks = _kv_bands(kp_ref, kc_ref)
        vvs = _kv_bands(vp_ref, vc_ref)
        lane = lax.broadcasted_iota(jnp.int32, (BLOCK, 128), 1)
        for j in range(8):
            qs = q_ref[:, j * 128:(j + 1) * 128]
            outs = []
            for hh in range(2):
                hl = 2 * j + hh
                qm = jnp.where((lane < 64) if hh == 0 else (lane >= 64), qs, jnp.zeros_like(qs))
                p, _ = _attn_probs(qm, kks[j // 4], b_ref[hl], sk_ref[gp * 16 + hl], valid)
                outs.append(_dot(p.astype(BF16), vvs[j // 4]))
            o_ref[:, j * 128:(j + 1) * 128] = jnp.where(lane < 64, outs[0], outs[1]).astype(BF16)

    kb, vb = OFF_K // 128, OFF_V // 128
    return pl.pallas_call(
        body, name="attn_fwd", grid=(nb, 2),
        in_specs=[pl.BlockSpec(memory_space=pltpu.SMEM),
                  pl.BlockSpec((BLOCK, qw), lambda n, g: (n, OFF_Q // qw + g)),
                  pl.BlockSpec((BLOCK, 128), lambda n, g: (jnp.maximum(n - 1, 0), kb + g)),
                  pl.BlockSpec((BLOCK, 128), lambda n, g: (n, kb + g)),
                  pl.BlockSpec((BLOCK, 128), lambda n, g: (jnp.maximum(n - 1, 0), vb + g)),
                  pl.BlockSpec((BLOCK, 128), lambda n, g: (n, vb + g)),
                  pl.BlockSpec((16, BLOCK, 2 * BLOCK), lambda n, g: (g, 0, 0))],
        out_specs=pl.BlockSpec((BLOCK, qw), lambda n, g: (n, g)),
        out_shape=SDS((s, D), BF16), compiler_params=_cp(2))(sinks, proj, proj, proj, proj, proj, band)


def merge_fwd(ya, att, w_lru_out, w_attn_out, proj):
    s = ya.shape[0]
    tm, tn = min(512, s), 512

    def body(ya_ref, at_ref, wl_ref, wt_ref, ga_ref, gb_ref, yab_ref, mg_ref):
        y_a = _dot(ya_ref[...], wl_ref[...])
        y_b = _dot(at_ref[...], wt_ref[...])
        yab_ref[0] = y_a.astype(BF16)
        yab_ref[1] = y_b.astype(BF16)
        mg_ref[...] = (jax.nn.sigmoid(ga_ref[...].astype(F32)) * y_a
                       + jax.nn.sigmoid(gb_ref[...].astype(F32)) * y_b).astype(BF16)

    return pl.pallas_call(
        body, name="merge_fwd", grid=(s // tm, D // tn),
        in_specs=[pl.BlockSpec((tm, D), lambda i, j: (i, 0)), pl.BlockSpec((tm, D), lambda i, j: (i, 0)),
                  pl.BlockSpec((D, tn), lambda i, j: (0, j)), pl.BlockSpec((D, tn), lambda i, j: (0, j)),
                  pl.BlockSpec((tm, tn), lambda i, j: (i, OFF_GA // tn + j)),
                  pl.BlockSpec((tm, tn), lambda i, j: (i, OFF_GB // tn + j))],
        out_specs=[pl.BlockSpec((2, tm, tn), lambda i, j: (0, i, j)), pl.BlockSpec((tm, tn), lambda i, j: (i, j))],
        out_shape=[SDS((2, s, D), BF16), SDS((s, D), BF16)],
        compiler_params=_cp(2))(ya, att, w_lru_out, w_attn_out, proj, proj)


def outproj_fwd(merged, w_out, x, vecs):
    s = x.shape[0]
    tm, tn = min(512, s), 512

    def body(m_ref, w_ref, x_ref, v_ref, x1_ref, o1_ref):
        o1 = _dot(m_ref[...], w_ref[...])
        o1_ref[...] = o1.astype(BF16)
        x1_ref[...] = x_ref[...] + v_ref[V_GATE1:V_GATE1 + 1, :] * o1

    return pl.pallas_call(
        body, name="outproj_fwd", grid=(s // tm, D // tn),
        in_specs=[pl.BlockSpec((tm, D), lambda i, j: (i, 0)), pl.BlockSpec((D, tn), lambda i, j: (0, j)),
                  pl.BlockSpec((tm, tn), lambda i, j: (i, j)), pl.BlockSpec((16, tn), lambda i, j: (0, j))],
        out_specs=[pl.BlockSpec((tm, tn), lambda i, j: (i, j)), pl.BlockSpec((tm, tn), lambda i, j: (i, j))],
        out_shape=[SDS((s, D), F32), SDS((s, D), BF16)],
        compiler_params=_cp(2))(merged, w_out, x, vecs)


def ff1_fwd(x1, vecs, w_ff1):
    s = x1.shape[0]
    tm, tn = min(512, s), 512
    per = D // tn

    def body(x_ref, v_ref, w_ref, f_ref, h_ref):
        @pl.when(pl.program_id(1) == 0)
        def _():
            _, xh = _rms_parts(x_ref[...])
            h = (xh * v_ref[V_G2:V_G2 + 1, :]) * (1.0 + v_ref[V_SCALE2:V_SCALE2 + 1, :]) + v_ref[V_SHIFT2:V_SHIFT2 + 1, :]
            h_ref[...] = h.astype(BF16)
        f_ref[...] = _dot(h_ref[...], w_ref[...]).astype(BF16)

    return pl.pallas_call(
        body, name="ff1_fwd", grid=(s // tm, D_FF // tn),
        in_specs=[pl.BlockSpec((tm, D), lambda i, j: (i, 0)), pl.BlockSpec((16, D), lambda i, j: (0, 0)),
                  pl.BlockSpec((None, D, tn), lambda i, j: (j // per, 0, j % per))],
        out_specs=[pl.BlockSpec((tm, tn), lambda i, j: (i, j)), pl.BlockSpec((tm, D), lambda i, j: (i, 0))],
        out_shape=[SDS((s, D_FF), BF16), SDS((s, D), BF16)],
        compiler_params=_cp(2))(x1, vecs, w_ff1)


def ff2_loss(f, w_ff2, x1, tgt, vecs):
    s = x1.shape[0]
    tm, tk = min(512, s), 512
    nk = D_FF // tk

    def body(f_ref, w_ref, x1_ref, t_ref, v_ref, dx2_ref, do2_ref, sums_ref, loss_ref, acc):
        i, k = pl.program_id(0), pl.program_id(1)

        @pl.when((i == 0) & (k == 0))
        def _():
            sums_ref[...] = jnp.zeros_like(sums_ref)
            loss_ref[...] = jnp.zeros_like(loss_ref)

        @pl.when(k == 0)
        def _():
            acc[...] = jnp.zeros_like(acc)

        fv = jnp.maximum(f_ref[...].astype(F32), 0.0)
        acc[...] += _dot((fv * fv).astype(BF16), w_ref[...])

        @pl.when(k == nk - 1)
        def _():
            gate2 = v_ref[V_GATE2:V_GATE2 + 1, :]
            g3 = v_ref[V_G3:V_G3 + 1, :]

            def sub(rb, carry):
                rs = pl.ds(pl.multiple_of(rb * SUB, SUB), SUB)
                o2 = acc[rs, :]
                x2 = x1_ref[rs, :] + gate2 * o2
                r3, xh = _rms_parts(x2)
                e = xh * g3 - t_ref[rs, :]
                loss_ref[...] += (0.5 / D) * jnp.sum(e * e)
                dy = e * (1.0 / D)
                sums_ref[0:1, :] += jnp.sum(dy * xh, axis=0, keepdims=True)
                dxh = dy * g3
                dx2 = r3 * (dxh - xh * jnp.mean(dxh * xh, axis=-1, keepdims=True))
                sums_ref[1:2, :] += jnp.sum(dx2 * o2, axis=0, keepdims=True)
                dx2_ref[rs, :] = dx2
                do2_ref[rs, :] = (dx2 * gate2).astype(BF16)
                return carry

            lax.fori_loop(0, tm // SUB, sub, 0)

    return pl.pallas_call(
        body, name="ff2_loss", grid=(s // tm, nk),
        in_specs=[pl.BlockSpec((tm, tk), lambda i, k: (i, k)), pl.BlockSpec((tk, D), lambda i, k: (k, 0)),
                  pl.BlockSpec((tm, D), lambda i, k: (i, 0)), pl.BlockSpec((tm, D), lambda i, k: (i, 0)),
                  pl.BlockSpec((16, D), lambda i, k: (0, 0))],
        out_specs=[pl.BlockSpec((tm, D), lambda i, k: (i, 0)), pl.BlockSpec((tm, D), lambda i, k: (i, 0)),
                   pl.BlockSpec((8, D), lambda i, k: (0, 0)), pl.BlockSpec((8, 128), lambda i, k: (0, 0))],
        out_shape=[SDS((s, D), F32), SDS((s, D), BF16), SDS((8, D), F32), SDS((8, 128), F32)],
        scratch_shapes=[pltpu.VMEM((tm, D), F32)],
        compiler_params=_cp(2))(f, w_ff2, x1, tgt, vecs)


def ff2_bwd(do2, w_ff2, f):
    s = do2.shape[0]
    tm, tn = min(512, s), 512

    def body(d_ref, w_ref, f_ref, o_ref):
        dff = _dot_nt(d_ref[...], w_ref[...])
        o_ref[...] = (dff * (2.0 * jnp.maximum(f_ref[...].astype(F32), 0.0))).astype(BF16)

    return pl.pallas_call(
        body, name="ff2_bwd", grid=(s // tm, D_FF // tn),
        in_specs=[pl.BlockSpec((tm, D), lambda i, j: (i, 0)), pl.BlockSpec((tn, D), lambda i, j: (j, 0)),
                  pl.BlockSpec((tm, tn), lambda i, j: (i, j))],
        out_specs=pl.BlockSpec((tm, tn), lambda i, j: (i, j)),
        out_shape=SDS((s, D_FF), BF16), compiler_params=_cp(2))(do2, w_ff2, f)


def weight_grad(name, a, b, tn, out_shape, out_block, out_map, relu2=False):
    s, m = a.shape
    n = b.shape[1]
    tm = WG_TM
    chunk = min(1024, s)
    nch = s // chunk

    def body(a_hbm, b_ref, o_ref, a_buf, at_s, sem):
        i = pl.program_id(0)

        @pl.when(pl.program_id(1) == 0)
        def _():
            def fetch(ch):
                return pltpu.make_async_copy(a_hbm.at[pl.ds(ch * chunk, chunk), pl.ds(i * tm, tm)],
                                             a_buf.at[ch % 2], sem.at[ch % 2])
            fetch(0).start()
            for ch in range(nch):
                if ch + 1 < nch:
                    fetch(ch + 1).start()
                fetch(ch).wait()
                av = a_buf[ch % 2]
                if relu2:
                    fv = jnp.maximum(av.astype(F32), 0.0)
                    av = (fv * fv).astype(BF16)
                at_s[:, ch * chunk:(ch + 1) * chunk] = av.T

        o_ref[...] = _dot(at_s[...], b_ref[...]).astype(BF16)

    return pl.pallas_call(
        body, name=name, grid=(m // tm, n // tn),
        in_specs=[pl.BlockSpec(memory_space=pl.ANY), pl.BlockSpec((s, tn), lambda i, j: (0, j))],
        out_specs=pl.BlockSpec(out_block, lambda i, j: out_map(i, j)),
        out_shape=SDS(out_shape, BF16),
        scratch_shapes=[pltpu.VMEM((2, chunk, tm), BF16), pltpu.VMEM((tm, s), BF16), pltpu.SemaphoreType.DMA((2,))],
        compiler_params=_cp(2))(a, b)


def ff1_bwd(df, w_ff1, x1, dx2, o1, vecs):
    s = df.shape[0]
    tm, tk = min(512, s), 512
    nk = D_FF // tk
    per = D // tk

    def body(d_ref, w_ref, x1_ref, dx2_ref, o1_ref, v_ref, dx1_ref, do1_ref, sums_ref, acc):
        i, k = pl.program_id(0), pl.program_id(1)

        @pl.when((i == 0) & (k == 0))
        def _():
            sums_ref[...] = jnp.zeros_like(sums_ref)

        @pl.when(k == 0)
        def _():
            acc[...] = jnp.zeros_like(acc)

        acc[...] += _dot_nt(d_ref[...], w_ref[...])

        @pl.when(k == nk - 1)
        def _():
            g2 = v_ref[V_G2:V_G2 + 1, :]
            scale2 = v_ref[V_SCALE2:V_SCALE2 + 1, :]
            gate1 = v_ref[V_GATE1:V_GATE1 + 1, :]

            def sub(rb, carry):
                rs = pl.ds(pl.multiple_of(rb * SUB, SUB), SUB)
                dh = acc[rs, :]
                r2, xh = _rms_parts(x1_ref[rs, :])
                sums_ref[0:1, :] += jnp.sum(dh, axis=0, keepdims=True)
                sums_ref[1:2, :] += jnp.sum(dh * (xh * g2), axis=0, keepdims=True)
                dxn = dh * (1.0 + scale2)
                sums_ref[2:3, :] += jnp.sum(dxn * xh, axis=0, keepdims=True)
                dxh = dxn * g2
                dx1 = dx2_ref[rs, :] + r2 * (dxh - xh * jnp.mean(dxh * xh, axis=-1, keepdims=True))
                sums_ref[3:4, :] += jnp.sum(dx1 * o1_ref[rs, :].astype(F32), axis=0, keepdims=True)
                dx1_ref[rs, :] = dx1
                do1_ref[rs, :] = (dx1 * gate1).astype(BF16)
                return carry

            lax.fori_loop(0, tm // SUB, sub, 0)

    return pl.pallas_call(
        body, name="ff1_bwd", grid=(s // tm, nk),
        in_specs=[pl.BlockSpec((tm, tk), lambda i, k: (i, k)),
                  pl.BlockSpec((None, D, tk), lambda i, k: (k // per, 0, k % per)),
                  pl.BlockSpec((tm, D), lambda i, k: (i, 0)), pl.BlockSpec((tm, D), lambda i, k: (i, 0)),
                  pl.BlockSpec((tm, D), lambda i, k: (i, 0)), pl.BlockSpec((16, D), lambda i, k: (0, 0))],
        out_specs=[pl.BlockSpec((tm, D), lambda i, k: (i, 0)), pl.BlockSpec((tm, D), lambda i, k: (i, 0)),
                   pl.BlockSpec((8, D), lambda i, k: (0, 0))],
        out_shape=[SDS((s, D), F32), SDS((s, D), BF16), SDS((8, D), F32)],
        scratch_shapes=[pltpu.VMEM((tm, D), F32)],
        compiler_params=_cp(2))(df, w_ff1, x1, dx2, o1, vecs)


def outproj_bwd(do1, w_out, yab, proj):
    s = do1.shape[0]
    tm, tn = min(512, s), 512
    per = D // tn

    def body(d_ref, w_ref, y_ref, g_ref, dy_ref, dp_ref):
        dm = _dot_nt(d_ref[...], w_ref[...])
        sg = jax.nn.sigmoid(g_ref[...].astype(F32))
        dy_ref[...] = (dm * sg).astype(BF16)
        dp_ref[...] = (dm * y_ref[...].astype(F32) * sg * (1.0 - sg)).astype(BF16)

    return pl.pallas_call(
        body, name="outproj_bwd", grid=(s // tm, 2 * per),
        in_specs=[pl.BlockSpec((tm, D), lambda i, j: (i, 0)), pl.BlockSpec((tn, D), lambda i, j: (j % per, 0)),
                  pl.BlockSpec((None, tm, tn), lambda i, j: (j // per, i, j % per)),
                  pl.BlockSpec((tm, tn), lambda i, j: (i, OFF_GA // tn + j))],
        out_specs=[pl.BlockSpec((None, tm, tn), lambda i, j: (j // per, i, j % per)),
                   pl.BlockSpec((tm, tn), lambda i, j: (i, OFF_GA // tn + j))],
        out_shape=[SDS((2, s, D), BF16), SDS((s, IN_W), BF16)],
        compiler_params=_cp(2))(do1, w_out, yab, proj)


def lruout_bwd(dyab, w_lru_out, rec, proj, dproj):
    s = rec.shape[0]
    tm, tn = min(512, s), 512

    def body(d_ref, w_ref, r_ref, g_ref, dp_in, dr_ref, dp_ref):
        dya = _dot_nt(d_ref[...], w_ref[...])
        gate = g_ref[...].astype(F32)
        dr_ref[...] = dya * _gelu(gate)
        dp_ref[...] = (dya * r_ref[...] * _gelu_grad(gate)).astype(BF16)

    return pl.pallas_call(
        body, name="lruout_bwd", grid=(s // tm, D // tn),
        in_specs=[pl.BlockSpec((None, tm, D), lambda i, j: (0, i, 0)), pl.BlockSpec((tn, D), lambda i, j: (j, 0)),
                  pl.BlockSpec((tm, tn), lambda i, j: (i, j)),
                  pl.BlockSpec((tm, tn), lambda i, j: (i, OFF_GATE // tn + j)),
                  pl.BlockSpec(memory_space=pl.ANY)],
        out_specs=[pl.BlockSpec((tm, tn), lambda i, j: (i, j)),
                   pl.BlockSpec((tm, tn), lambda i, j: (i, OFF_GATE // tn + j))],
        out_shape=[SDS((s, D), F32), SDS((s, IN_W), BF16)],
        input_output_aliases={4: 1},
        compiler_params=_cp(2))(dyab, w_lru_out, rec, proj, dproj)


def attnout_bwd(dyab, w_attn_out):
    s = dyab.shape[1]
    tm, tn = min(512, s), 512

    def body(d_ref, w_ref, o_ref):
        o_ref[...] = _dot_nt(d_ref[...], w_ref[...]).astype(BF16)

    return pl.pallas_call(
        body, name="attnout_bwd", grid=(s // tm, D // tn),
        in_specs=[pl.BlockSpec((None, tm, D), lambda i, j: (1, i, 0)), pl.BlockSpec((tn, D), lambda i, j: (j, 0))],
        out_specs=pl.BlockSpec((tm, tn), lambda i, j: (i, j)),
        out_shape=SDS((s, D), BF16), compiler_params=_cp(2))(dyab, w_attn_out)


def attn_bwd(proj, band, sinks, datt, dproj):
    s = proj.shape[0]
    nb = s // BLOCK
    qw = 1024

    def body(sk_ref, q_ref, kp_ref, kc_ref, vp_ref, vc_ref, b_ref, do_ref, dp_in,
             dq_ref, dkb_ref, dvb_ref, db_ref, ds_ref):
        gp = pl.program_id(0)
        n = pl.program_id(1)

        @pl.when(n == 0)
        def _():
            db_ref[...] = jnp.zeros_like(db_ref)
            ds_ref[...] = jnp.zeros_like(ds_ref)

        valid = _band_valid(n)
        kks = _kv_bands(kp_ref, kc_ref)
        vvs = _kv_bands(vp_ref, vc_ref)
        lane = lax.broadcasted_iota(jnp.int32, (BLOCK, 128), 1)
        lane_b = lax.broadcasted_iota(jnp.int32, (2 * BLOCK, 128), 1)
        dk_acc = [jnp.zeros((2 * BLOCK, 128), F32), jnp.zeros((2 * BLOCK, 128), F32)]
        dv_acc = [jnp.zeros((2 * BLOCK, 128), F32), jnp.zeros((2 * BLOCK, 128), F32)]
        for j in range(8):
            kv = j // 4
            qs = q_ref[:, j * 128:(j + 1) * 128]
            dos = do_ref[:, j * 128:(j + 1) * 128]
            dqs = []
            for hh in range(2):
                hl = 2 * j + hh
                half = (lane < 64) if hh == 0 else (lane >= 64)
                qm = jnp.where(half, qs, jnp.zeros_like(qs))
                dom = jnp.where(half, dos, jnp.zeros_like(dos))
                p, ps = _attn_probs(qm, kks[kv], b_ref[hl], sk_ref[gp * 16 + hl], valid)
                dp = _dot_nt(dom, vvs[kv])
                delta = jnp.sum(p * dp, axis=-1, keepdims=True)
                dsc = p * (dp - delta)
                db_ref[hl] += dsc
                ds_ref[pl.ds(hl, 1), :] += jnp.zeros((1, 128), F32) - jnp.sum(ps * delta)
                dsb = (dsc * SCALE).astype(BF16)
                dqs.append(_dot(dsb, kks[kv]))
                dk_acc[kv] = dk_acc[kv] + _dot_tn(dsb, qm)
                dv_acc[kv] = dv_acc[kv] + _dot_tn(p.astype(BF16), dom)
            dq_ref[:, j * 128:(j + 1) * 128] = jnp.where(lane < 64, dqs[0], dqs[1]).astype(BF16)
        for acc, ref in ((dk_acc, dkb_ref), (dv_acc, dvb_ref)):
            d0 = acc[0] + pltpu.roll(acc[0], 64, 1)
            d1 = acc[1] + pltpu.roll(acc[1], 64, 1)
            ref[...] = jnp.where(lane_b < 64, d0, d1)

    kb, vb = OFF_K // 128, OFF_V // 128
    return pl.pallas_call(
        body, name="attn_bwd", grid=(2, nb),
        in_specs=[pl.BlockSpec(memory_space=pltpu.SMEM),
                  pl.BlockSpec((BLOCK, qw), lambda g, n: (n, OFF_Q // qw + g)),
                  pl.BlockSpec((BLOCK, 128), lambda g, n: (jnp.maximum(n - 1, 0), kb + g)),
                  pl.BlockSpec((BLOCK, 128), lambda g, n: (n, kb + g)),
                  pl.BlockSpec((BLOCK, 128), lambda g, n: (jnp.maximum(n - 1, 0), vb + g)),
                  pl.BlockSpec((BLOCK, 128), lambda g, n: (n, vb + g)),
                  pl.BlockSpec((16, BLOCK, 2 * BLOCK), lambda g, n: (g, 0, 0)),
                  pl.BlockSpec((BLOCK, qw), lambda g, n: (n, g)),
                  pl.BlockSpec(memory_space=pl.ANY)],
        out_specs=[pl.BlockSpec((BLOCK, qw), lambda g, n: (n, OFF_Q // qw + g)),
                   pl.BlockSpec((2 * BLOCK, 128), lambda g, n: (n, g)),
                   pl.BlockSpec((2 * BLOCK, 128), lambda g, n: (n, g)),
                   pl.BlockSpec((16, BLOCK, 2 * BLOCK), lambda g, n: (g, 0, 0)),
                   pl.BlockSpec((16, 128), lambda g, n: (g, 0))],
        out_shape=[SDS((s, IN_W), BF16), SDS((nb * 2 * BLOCK, 256), F32), SDS((nb * 2 * BLOCK, 256), F32),
                   SDS((N_HEADS, BLOCK, 2 * BLOCK), F32), SDS((N_HEADS, 128), F32)],
        input_output_aliases={8: 0},
        compiler_params=_cp(2))(sinks, proj, proj, proj, proj, proj, band, datt, dproj)


def dkv_combine(dkb, dvb, dproj):
    nb = dkb.shape[0] // (2 * BLOCK)
    s = nb * BLOCK
    dkb3 = dkb.reshape(nb, 2 * BLOCK, 256)
    dvb3 = dvb.reshape(nb, 2 * BLOCK, 256)

    def body(k1, k2, v1, v2, dp_in, o_ref):
        nxt = jnp.where(pl.program_id(0) < nb - 1, 1.0, 0.0)
        o_ref[:, 0:256] = (k1[...] + nxt * k2[...]).astype(BF16)
        o_ref[:, 256:512] = (v1[...] + nxt * v2[...]).astype(BF16)

    spec1 = pl.BlockSpec((None, BLOCK, 256), lambda m: (m, 1, 0))
    spec2 = pl.BlockSpec((None, BLOCK, 256), lambda m: (jnp.minimum(m + 1, nb - 1), 0, 0))
    return pl.pallas_call(
        body, name="dkv_combine", grid=(nb,),
        in_specs=[spec1, spec2, spec1, spec2, pl.BlockSpec(memory_space=pl.ANY)],
        out_specs=pl.BlockSpec((BLOCK, 512), lambda m: (m, OFF_K // 512)),
        out_shape=SDS((s, IN_W), BF16), input_output_aliases={4: 0},
        compiler_params=_cp(1))(dkb3, dkb3, dvb3, dvb3, dproj)


def lru_bwd(proj, rec, drec, lvec, wa, wx, dproj):
    s = proj.shape[0]
    t = min(256, s)
    nt = s // t

    def body(lx_ref, lxh_ref, rec_ref, rech_ref, dr_ref, lv_ref, wa_ref, wx_ref, dp_in,
             dlx_ref, sums_ref, dwa_ref, dwx_ref,
             xbuf, hbuf, dxbuf, a_s, dh_s, xc_s, r_s, ig_s, mu_s, gc):
        step_i = pl.program_id(0)
        ti = nt - 1 - step_i

        @pl.when(step_i == 0)
        def _():
            sums_ref[...] = jnp.zeros_like(sums_ref)
            dwa_ref[...] = jnp.zeros_like(dwa_ref)
            dwx_ref[...] = jnp.zeros_like(dwx_ref)
            dxbuf[pl.ds(t, 8), :] = jnp.zeros((8, D), F32)
            gc[...] = jnp.zeros((8, D), F32)

        live = jnp.where(ti > 0, 1.0, 0.0)
        xbuf[pl.ds(0, 8), :] = lxh_ref[...].astype(F32)[8:16] * live
        xbuf[pl.ds(8, t), :] = lx_ref[...].astype(F32)
        hbuf[pl.ds(0, 8), :] = rech_ref[...] * live
        hbuf[pl.ds(8, t), :] = rec_ref[...]
        first = (lax.broadcasted_iota(jnp.int32, (t, 128), 0) + ti * t) == 0
        for b in range(N_LRU_BLOCKS):
            cs = slice(b * 128, (b + 1) * 128)
            _, xc, _, r, ig, _, a, mult = _lru_block_fwd(xbuf, lv_ref, wa_ref, wx_ref, b, t, first)
            a_s[:, cs] = a
            xc_s[:, cs] = xc
            r_s[:, cs] = r
            ig_s[:, cs] = ig
            mu_s[:, cs] = mult

        def step(q, g):
            tt = t - 1 - q
            dh = dr_ref[pl.ds(tt, 1), :] + g
            dh_s[pl.ds(tt, 1), :] = dh
            return a_s[pl.ds(tt, 1), :] * dh

        gc[0:1, :] = lax.fori_loop(0, t, step, gc[0:1, :], unroll=8)
        for b in range(N_LRU_BLOCKS):
            cs = slice(b * 128, (b + 1) * 128)
            dh = dh_s[:, cs]
            a = a_s[:, cs]
            xc = xc_s[:, cs]
            r = r_s[:, cs]
            ig = ig_s[:, cs]
            mult = mu_s[:, cs]
            sp = _softplus(-lv_ref[L_LAM:L_LAM + 1, cs])
            lam = lv_ref[L_LAM:L_LAM + 1, cs]
            da = dh * hbuf[pl.ds(7, t), cs]
            dmult = jnp.where(first, 0.0, dh * ig * xc)
            dig = dh * mult * xc
            dxc = dh * mult * ig
            dlog_a = da * a - dmult * (a * a) / mult
            dr = dlog_a * ((-LRU_C) * sp)
            dsp = jnp.sum(dlog_a * ((-LRU_C) * r), axis=0, keepdims=True)
            dza = dr * r * (1.0 - r)
            dzx = dig * ig * (1.0 - ig)
            dzab = dza.astype(BF16)
            dzxb = dzx.astype(BF16)
            xcb = xc.astype(BF16)
            dwa_ref[b] += _dot_tn(xcb, dzab)
            dwx_ref[b] += _dot_tn(xcb, dzxb)
            dxc = dxc + _dot_nt(dzab, wa_ref[b]) + _dot_nt(dzxb, wx_ref[b])
            sums_ref[L_LAM:L_LAM + 1, cs] += dsp * (-jax.nn.sigmoid(-lam))
            sums_ref[L_BA:L_BA + 1, cs] += jnp.sum(dza, axis=0, keepdims=True)
            sums_ref[L_BX:L_BX + 1, cs] += jnp.sum(dzx, axis=0, keepdims=True)
            sums_ref[L_CB:L_CB + 1, cs] += jnp.sum(dxc, axis=0, keepdims=True)
            for kk in range(4):
                sums_ref[kk:kk + 1, cs] += jnp.sum(dxc * xbuf[pl.ds(5 + kk, t), cs], axis=0, keepdims=True)
            dxbuf[pl.ds(0, t), cs] = dxc
            dlx = (lv_ref[3:4, cs] * dxc + lv_ref[2:3, cs] * dxbuf[pl.ds(1, t), cs]
                   + lv_ref[1:2, cs] * dxbuf[pl.ds(2, t), cs] + lv_ref[0:1, cs] * dxbuf[pl.ds(3, t), cs])
            dlx_ref[:, cs] = dlx.astype(BF16)
        dxbuf[pl.ds(t, 8), :] = dxbuf[pl.ds(0, 8), :]

    rev = lambda i: nt - 1 - i
    return pl.pallas_call(
        body, name="lru_bwd", grid=(nt,),
        in_specs=[pl.BlockSpec((t, D), lambda i: (rev(i), 0)),
                  pl.BlockSpec((16, D), lambda i: (jnp.maximum(rev(i) * (t // 16) - 1, 0), 0)),
                  pl.BlockSpec((t, D), lambda i: (rev(i), 0)),
                  pl.BlockSpec((8, D), lambda i: (jnp.maximum(rev(i) * (t // 8) - 1, 0), 0)),
                  pl.BlockSpec((t, D), lambda i: (rev(i), 0)),
                  pl.BlockSpec((8, D), lambda i: (0, 0)),
                  pl.BlockSpec((N_LRU_BLOCKS, 128, 128), lambda i: (0, 0, 0)),
                  pl.BlockSpec((N_LRU_BLOCKS, 128, 128), lambda i: (0, 0, 0)),
                  pl.BlockSpec(memory_space=pl.ANY)],
        out_specs=[pl.BlockSpec((t, D), lambda i: (rev(i), 0)),
                   pl.BlockSpec((8, D), lambda i: (0, 0)),
                   pl.BlockSpec((N_LRU_BLOCKS, 128, 128), lambda i: (0, 0, 0)),
                   pl.BlockSpec((N_LRU_BLOCKS, 128, 128), lambda i: (0, 0, 0))],
        out_shape=[SDS((s, IN_W), BF16), SDS((8, D), F32), SDS((N_LRU_BLOCKS, 128, 128), F32),
                   SDS((N_LRU_BLOCKS, 128, 128), F32)],
        scratch_shapes=[pltpu.VMEM((t + 8, D), F32), pltpu.VMEM((t + 8, D), F32), pltpu.VMEM((t + 8, D), F32)]
        + [pltpu.VMEM((t, D), F32)] * 6 + [pltpu.VMEM((8, D), F32)],
        input_output_aliases={8: 0},
        compiler_params=_cp(1))(proj, proj, rec, rec, drec, lvec, wa, wx, dproj)


def inproj_bwd(dproj, w_in, x, dx1, vecs):
    s = x.shape[0]
    tm, tk = min(512, s), IN_TILE
    nk = IN_W // tk
    per = IN_SHARD // tk

    def body(d_ref, w_ref, x_ref, dx1_ref, v_ref, gx_ref, sums_ref, acc):
        i, k = pl.program_id(0), pl.program_id(1)

        @pl.when((i == 0) & (k == 0))
        def _():
            sums_ref[...] = jnp.zeros_like(sums_ref)

        @pl.when(k == 0)
        def _():
            acc[...] = jnp.zeros_like(acc)

        acc[...] += _dot_nt(d_ref[...], w_ref[...])

        @pl.when(k == nk - 1)
        def _():
            g1 = v_ref[V_G1:V_G1 + 1, :]
            scale1 = v_ref[V_SCALE1:V_SCALE1 + 1, :]

            def sub(rb, carry):
                rs = pl.ds(pl.multiple_of(rb * SUB, SUB), SUB)
                dh = acc[rs, :]
                r1, xh = _rms_parts(x_ref[rs, :])
                sums_ref[0:1, :] += jnp.sum(dh, axis=0, keepdims=True)
                sums_ref[1:2, :] += jnp.sum(dh * (xh * g1), axis=0, keepdims=True)
                dxn = dh * (1.0 + scale1)
                sums_ref[2:3, :] += jnp.sum(dxn * xh, axis=0, keepdims=True)
                dxh = dxn * g1
                gx_ref[rs, :] = dx1_ref[rs, :] + r1 * (dxh - xh * jnp.mean(dxh * xh, axis=-1, keepdims=True))
                return carry

            lax.fori_loop(0, tm // SUB, sub, 0)

    return pl.pallas_call(
        body, name="inproj_bwd", grid=(s // tm, nk),
        in_specs=[pl.BlockSpec((tm, tk), lambda i, k: (i, k)),
                  pl.BlockSpec((None, D, tk), lambda i, k: (k // per, 0, k % per)),
                  pl.BlockSpec((tm, D), lambda i, k: (i, 0)), pl.BlockSpec((tm, D), lambda i, k: (i, 0)),
                  pl.BlockSpec((16, D), lambda i, k: (0, 0))],
        out_specs=[pl.BlockSpec((tm, D), lambda i, k: (i, 0)), pl.BlockSpec((8, D), lambda i, k: (0, 0))],
        out_shape=[SDS((s, D), F32), SDS((8, D), F32)],
        scratch_shapes=[pltpu.VMEM((tm, D), F32)],
        compiler_params=_cp(2))(dproj, w_in, x, dx1, vecs)


def mod_columns(c16, w_ada, b_cols):
    tn = 512

    def body(c_ref, w_ref, b_ref, o_ref):
        cv = c_ref[...]
        ca = (cv * jax.nn.sigmoid(cv)).astype(BF16)
        o_ref[...] = _dot(ca, w_ref[...].astype(BF16)) + b_ref[...]

    return pl.pallas_call(
        body, name="mod_columns", grid=(ADA_SHARD // tn,),
        in_specs=[pl.BlockSpec((16, D), lambda j: (0, 0)), pl.BlockSpec((D, tn), lambda j: (0, j)),
                  pl.BlockSpec((1, tn), lambda j: (0, j))],
        out_specs=pl.BlockSpec((16, tn), lambda j: (0, j)),
        out_shape=SDS((16, ADA_SHARD), F32), compiler_params=_cp(1))(c16, w_ada, b_cols)


def wada_update(c16, dmod16, w, m, v):
    tm, tn = 512, 512

    def body(c_ref, d_ref, w_ref, m_ref, v_ref, g_out, dl_out, m_out, v_out):
        cv = c_ref[...]
        ca = (cv * jax.nn.sigmoid(cv)).astype(BF16)
        g = _dot_tn(ca, d_ref[...].astype(BF16))
        dl, m2, v2 = _adamw_math(w_ref[...], g, m_ref[...], v_ref[...])
        g_out[...] = g
        dl_out[...] = dl
        m_out[...] = m2
        v_out[...] = v2

    tile = pl.BlockSpec((tm, tn), lambda i, j: (i, j))
    return pl.pallas_call(
        body, name="wada_update", grid=(D // tm, ADA_SHARD // tn),
        in_specs=[pl.BlockSpec((16, tm), lambda i, j: (0, i)), pl.BlockSpec((16, tn), lambda i, j: (0, j)),
                  tile, tile, tile],
        out_specs=[tile] * 4, out_shape=[SDS((D, ADA_SHARD), F32)] * 4,
        compiler_params=_cp(2))(c16, dmod16, w, m, v)


def adamw_big(name, w, mine, theirs, m, v, c_idx):
    r, c = w.shape
    tr = 128
    per = (r // 2) // tr

    def body(c_ref, w_ref, a_ref, b_ref, m_ref, v_ref, g_out, dl_out, m_out, v_out):
        own = (pl.program_id(0) // per) == c_ref[0]
        g = jnp.where(own, a_ref[...], b_ref[...])
        dl, m2, v2 = _adamw_math(w_ref[...], g, m_ref[...], v_ref[...])
        g_out[...] = g
        dl_out[...] = dl
        m_out[...] = m2
        v_out[...] = v2

    tile = pl.BlockSpec((tr, c), lambda i, cr: (i, 0))
    half = pl.BlockSpec((tr, c), lambda i, cr: (i % per, 0))
    gs = pltpu.PrefetchScalarGridSpec(num_scalar_prefetch=1, grid=(r // tr,),
                                      in_specs=[tile, half, half, tile, tile], out_specs=[tile] * 4)
    return pl.pallas_call(body, name=name, grid_spec=gs, out_shape=[SDS((r, c), F32)] * 4,
                          compiler_params=_cp(1))(c_idx, w, mine, theirs, m, v)


def cast_into_slot(name, w, k_idx):
    r, c = w.shape
    tr = 256

    def body(k_ref, w_ref, o_ref):
        o_ref[...] = w_ref[...].astype(BF16)

    gs = pltpu.PrefetchScalarGridSpec(
        num_scalar_prefetch=1, grid=(r // tr,),
        in_specs=[pl.BlockSpec((tr, c), lambda i, kr: (i, 0))],
        out_specs=pl.BlockSpec((None, tr, c), lambda i, kr: (kr[0], i, 0)))
    return pl.pallas_call(body, name=name, grid_spec=gs, out_shape=SDS((N_CHIPS, r, c), BF16),
                          compiler_params=_cp(1))(k_idx, w)


def adamw_small(ws, gs, ms, vs):
    n = len(ws)

    def body(*refs):
        for i in range(n):
            dl, m2, v2 = _adamw_math(refs[i][...], refs[n + i][...], refs[2 * n + i][...], refs[3 * n + i][...])
            refs[4 * n + i][...] = dl
            refs[5 * n + i][...] = m2
            refs[6 * n + i][...] = v2

    vm = pl.BlockSpec(memory_space=pltpu.VMEM)
    shapes = [SDS(w.shape, F32) for w in ws]
    outs = pl.pallas_call(
        body, name="adamw_small", in_specs=[vm] * (4 * n), out_specs=[vm] * (3 * n), out_shape=shapes * 3,
        compiler_params=pltpu.CompilerParams(vmem_limit_bytes=VMEM_LIMIT))(*ws, *gs, *ms, *vs)
    return outs[:n], outs[n:2 * n], outs[2 * n:]


def sum_devices(gathered):
    rows = gathered.shape[1]
    tr = 96

    def body(x_ref, o_ref):
        acc = x_ref[0]
        for d in range(1, N_DEV):
            acc = acc + x_ref[d]
        o_ref[...] = acc

    return pl.pallas_call(
        body, name="sum_devices", grid=(rows // tr,),
        in_specs=[pl.BlockSpec((N_DEV, tr, D), lambda i: (0, i, 0))],
        out_specs=pl.BlockSpec((tr, D), lambda i: (i, 0)),
        out_shape=SDS((rows, D), F32), compiler_params=_cp(1))(gathered)


def _mesh_pos():
    return lax.axis_index("x"), lax.axis_index("y"), lax.axis_index("c")


def _other_chips(x, y):
    return [(1 - x, y), (x, 1 - y), (1 - x, 1 - y)]


def all_gather_small(name, block):
    m_per, n = block.shape

    def body(x_ref, out_ref, send_sems, recv_sems, local_sem):
        x, y, c = _mesh_pos()
        me, sibling = (x, y, c), (x, y, 1 - c)
        chips = _other_chips(x, y)

        def rows(px, py, pc):
            return out_ref.at[pl.ds((4 * px + 2 * py + pc) * m_per, m_per), :]

        def copy(k, blk, to, src=None):
            return pltpu.make_async_remote_copy(
                src_ref=rows(*blk) if src is None else src, dst_ref=rows(*blk),
                send_sem=send_sems.at[k], recv_sem=recv_sems.at[k], device_id=to, device_id_type=MESH)

        mine = pltpu.make_async_copy(x_ref, rows(*me), local_sem)
        mine.start()
        first = [copy(0, me, sibling, src=x_ref)]
        first += [copy(1 + j, me, (*chip, c), src=x_ref) for j, chip in enumerate(chips)]
        for cp in first:
            cp.start()
        passed = [copy(4 + j, (*chip, c), sibling) for j, chip in enumerate(chips)]
        for j, chip in enumerate(chips):
            copy(1 + j, (*chip, c), me).wait_recv()
            passed[j].start()
        copy(0, sibling, me).wait_recv()
        for j, chip in enumerate(chips):
            copy(4 + j, (*chip, 1 - c), me).wait_recv()
        for cp in first + passed:
            cp.wait_send()
        mine.wait()

    vm = pl.BlockSpec(memory_space=pltpu.VMEM)
    return pl.pallas_call(
        body, name=name, out_shape=SDS((N_DEV * m_per, n), block.dtype), in_specs=[vm], out_specs=vm,
        scratch_shapes=[pltpu.SemaphoreType.DMA((7,)), pltpu.SemaphoreType.DMA((7,)), pltpu.SemaphoreType.DMA],
        compiler_params=pltpu.CompilerParams(vmem_limit_bytes=VMEM_LIMIT))(block)


def all_gather_weights(bufs):
    n = len(bufs)
    halves = [w.shape[1] // 2 for w in bufs]

    def body(*refs):
        outs = refs[n:2 * n]
        send_sems, recv_sems = refs[2 * n:]
        x, y, c = _mesh_pos()
        k = 2 * x + y
        me, sibling = (x, y, c), (x, y, 1 - c)
        chips = _other_chips(x, y)

        def region(i, chip_idx, half):
            return outs[i].at[chip_idx, pl.ds(half * halves[i], halves[i]), :]

        def copy(i, j, reg, to):
            return pltpu.make_async_remote_copy(src_ref=reg, dst_ref=reg, send_sem=send_sems.at[i, j],
                                                recv_sem=recv_sems.at[i, j], device_id=to, device_id_type=MESH)

        started = []
        for i in range(n):
            for j, chip in enumerate(chips):
                cp = copy(i, j, region(i, k, c), (*chip, c))
                cp.start()
                started.append(cp)
        for i in range(n):
            for j, chip in enumerate(chips):
                kj = 2 * chip[0] + chip[1]
                copy(i, j, region(i, kj, c), me).wait_recv()
                cp = copy(i, 3 + j, region(i, kj, c), sibling)
                cp.start()
                started.append(cp)
        for i in range(n):
            for j, chip in enumerate(chips):
                kj = 2 * chip[0] + chip[1]
                copy(i, 3 + j, region(i, kj, 1 - c), me).wait_recv()
        for cp in started:
            cp.wait_send()

    hbm = pl.BlockSpec(memory_space=pl.ANY)
    return pl.pallas_call(
        body, name="all_gather_weights", in_specs=[hbm] * n, out_specs=[hbm] * n,
        out_shape=[SDS(w.shape, w.dtype) for w in bufs],
        input_output_aliases={i: i for i in range(n)},
        scratch_shapes=[pltpu.SemaphoreType.DMA((n, 6)), pltpu.SemaphoreType.DMA((n, 6))])(*bufs)


def sibling_exchange(name, grads):
    n = len(grads)
    halves = [g.shape[1] // 2 for g in grads]

    def body(*refs):
        ins, outs = refs[:n], refs[n:2 * n]
        send_sems, recv_sems = refs[2 * n:]
        x, y, c = _mesh_pos()
        sibling = (x, y, 1 - c)
        cps = []
        for i in range(n):
            cp = pltpu.make_async_remote_copy(
                src_ref=ins[i].at[:, pl.ds((1 - c) * halves[i], halves[i]), :], dst_ref=outs[i],
                send_sem=send_sems.at[i], recv_sem=recv_sems.at[i], device_id=sibling, device_id_type=MESH)
            cp.start()
            cps.append(cp)
        for cp in cps:
            cp.wait_recv()
        for cp in cps:
            cp.wait_send()

    hbm = pl.BlockSpec(memory_space=pl.ANY)
    return pl.pallas_call(
        body, name=name, in_specs=[hbm] * n, out_specs=[hbm] * n,
        out_shape=[SDS((N_CHIPS, h, g.shape[2]), g.dtype) for g, h in zip(grads, halves)],
        scratch_shapes=[pltpu.SemaphoreType.DMA((n,)), pltpu.SemaphoreType.DMA((n,))])(*grads)


def sibling_sum(name, grad, other, c_idx):
    _, r, cc = grad.shape
    h = r // 2
    tr = min(256, h)
    g4 = grad.reshape(N_CHIPS, 2, h, cc)

    def body(c_ref, a_ref, b_ref, o_ref):
        o_ref[...] = (a_ref[...].astype(F32) + b_ref[...].astype(F32)).astype(BF16)

    gs = pltpu.PrefetchScalarGridSpec(
        num_scalar_prefetch=1, grid=(N_CHIPS, h // tr),
        in_specs=[pl.BlockSpec((None, None, tr, cc), lambda s, i, cr: (s, cr[0], i, 0)),
                  pl.BlockSpec((None, tr, cc), lambda s, i, cr: (s, i, 0))],
        out_specs=pl.BlockSpec((None, tr, cc), lambda s, i, cr: (s, i, 0)))
    return pl.pallas_call(body, name=name, grid_spec=gs, out_shape=SDS((N_CHIPS, h, cc), BF16),
                          compiler_params=_cp(2))(c_idx, g4, other)


HBM_SPEC = pl.BlockSpec(memory_space=pltpu.HBM)
SEM_SPEC = pl.BlockSpec(memory_space=pltpu.SEMAPHORE)


def _side_effecting():
    return pltpu.CompilerParams(has_side_effects=pltpu.SideEffectType.DATAFLOW_SIDE_EFFECTING)


def _in_hbm(a):
    return pltpu.with_memory_space_constraint(a, pltpu.HBM)


def gather_start(bufs, after):
    n = len(bufs)
    halves = [w.shape[1] // 2 for w in bufs]

    def body(*refs):
        ins = refs[:n]
        send_sems, recv_sems, token = refs[n + 1], refs[n + 2], refs[-1]
        x, y, c = _mesh_pos()
        k = 2 * x + y
        for i in range(n):
            reg = ins[i].at[k, pl.ds(c * halves[i], halves[i]), :]
            for j, chip in enumerate(_other_chips(x, y)):
                pltpu.make_async_remote_copy(src_ref=reg, dst_ref=reg, send_sem=send_sems.at[3 * i + j],
                                             recv_sem=recv_sems.at[3 * i + j], device_id=(*chip, c),
                                             device_id_type=MESH).start()
        token[...] = jnp.zeros_like(token)

    outs = pl.pallas_call(
        body, name="gather_start",
        out_shape=(pltpu.SemaphoreType.DMA((3 * n,)), pltpu.SemaphoreType.DMA((3 * n,)),
                   *[pltpu.HBM(w.shape, w.dtype) for w in bufs], SDS((8, 128), F32)),
        in_specs=[HBM_SPEC] * n + [pl.BlockSpec(memory_space=pl.ANY)],
        out_specs=(SEM_SPEC, SEM_SPEC, *[HBM_SPEC] * n, pl.BlockSpec(memory_space=pltpu.VMEM)),
        input_output_aliases={i: 2 + i for i in range(n)},
        compiler_params=_side_effecting())(*[_in_hbm(w) for w in bufs], after)
    return outs[0], outs[1], list(outs[2:2 + n]), outs[-1]


def gather_wait(send_sems, recv_sems, bufs, after):
    n = len(bufs)
    halves = [w.shape[1] // 2 for w in bufs]

    def body(*refs):
        ins = refs[:n]
        send_sems, recv_sems = refs[n], refs[n + 1]
        x, y, c = _mesh_pos()
        k = 2 * x + y
        for i in range(n):
            for j, chip in enumerate(_other_chips(x, y)):
                kj = 2 * chip[0] + chip[1]
                cp = pltpu.make_async_remote_copy(
                    src_ref=ins[i].at[k, pl.ds(c * halves[i], halves[i]), :],
                    dst_ref=ins[i].at[kj, pl.ds(c * halves[i], halves[i]), :],
                    send_sem=send_sems.at[3 * i + j], recv_sem=recv_sems.at[3 * i + j], device_id=(*chip, c),
                    device_id_type=MESH)
                cp.wait_send()
                cp.wait_recv()

    return pl.pallas_call(
        body, name="gather_wait", out_shape=[pltpu.HBM(w.shape, w.dtype) for w in bufs],
        in_specs=[HBM_SPEC] * n + [SEM_SPEC, SEM_SPEC, pl.BlockSpec(memory_space=pl.ANY)],
        out_specs=[HBM_SPEC] * n, input_output_aliases={i: i for i in range(n)},
        compiler_params=_side_effecting())(*bufs, send_sems, recv_sems, after)


def gather_forward(bufs):
    n = len(bufs)
    halves = [w.shape[1] // 2 for w in bufs]

    def body(*refs):
        outs = refs[n:2 * n]
        send_sems, recv_sems = refs[2 * n:]
        x, y, c = _mesh_pos()
        chips = _other_chips(x, y)

        def copy(i, j, half, to):
            kj = 2 * chips[j][0] + chips[j][1]
            reg = outs[i].at[kj, pl.ds(half * halves[i], halves[i]), :]
            return pltpu.make_async_remote_copy(src_ref=reg, dst_ref=reg, send_sem=send_sems.at[i, j],
                                                recv_sem=recv_sems.at[i, j], device_id=to, device_id_type=MESH)

        cps = [copy(i, j, c, (x, y, 1 - c)) for i in range(n) for j in range(3)]
        for cp in cps:
            cp.start()
        for i in range(n):
            for j in range(3):
                copy(i, j, 1 - c, (x, y, c)).wait_recv()
        for cp in cps:
            cp.wait_send()

    hbm = pl.BlockSpec(memory_space=pl.ANY)
    return pl.pallas_call(
        body, name="gather_forward", in_specs=[hbm] * n, out_specs=[hbm] * n,
        out_shape=[SDS(w.shape, w.dtype) for w in bufs], input_output_aliases={i: i for i in range(n)},
        scratch_shapes=[pltpu.SemaphoreType.DMA((n, 3)), pltpu.SemaphoreType.DMA((n, 3))])(*bufs)


def exchange_start(name, sums):
    n = len(sums)
    lands = [lax.empty((3,) + t.shape[1:], t.dtype) for t in sums]

    def body(*refs):
        ins, zones = refs[:n], refs[n:2 * n]
        send_sems, recv_sems, token = refs[2 * n], refs[2 * n + 1], refs[-1]
        x, y, c = _mesh_pos()
        for i in range(n):
            for j, chip in enumerate(_other_chips(x, y)):
                kj = 2 * chip[0] + chip[1]
                pltpu.make_async_remote_copy(src_ref=ins[i].at[kj], dst_ref=zones[i].at[j],
                                             send_sem=send_sems.at[3 * i + j], recv_sem=recv_sems.at[3 * i + j],
                                             device_id=(*chip, c), device_id_type=MESH).start()
        token[...] = jnp.zeros_like(token)

    outs = pl.pallas_call(
        body, name=name,
        out_shape=(pltpu.SemaphoreType.DMA((3 * n,)), pltpu.SemaphoreType.DMA((3 * n,)),
                   *[pltpu.HBM(t.shape, t.dtype) for t in sums], *[pltpu.HBM(t.shape, t.dtype) for t in lands],
                   SDS((8, 128), F32)),
        in_specs=[HBM_SPEC] * (2 * n),
        out_specs=(SEM_SPEC, SEM_SPEC, *[HBM_SPEC] * (2 * n), pl.BlockSpec(memory_space=pltpu.VMEM)),
        input_output_aliases={i: 2 + i for i in range(2 * n)},
        compiler_params=_side_effecting())(*[_in_hbm(t) for t in sums], *[_in_hbm(t) for t in lands])
    return outs[0], outs[1], list(outs[2:2 + n]), list(outs[2 + n:2 + 2 * n]), outs[-1]


def exchange_wait(name, send_sems, recv_sems, sums, lands, after):
    n = len(sums)

    def body(*refs):
        ins, zones = refs[:n], refs[n:2 * n]
        send_sems, recv_sems = refs[2 * n], refs[2 * n + 1]
        x, y, c = _mesh_pos()
        for i in range(n):
            for j, chip in enumerate(_other_chips(x, y)):
                kj = 2 * chip[0] + chip[1]
                cp = pltpu.make_async_remote_copy(src_ref=ins[i].at[kj], dst_ref=zones[i].at[j],
                                                  send_sem=send_sems.at[3 * i + j], recv_sem=recv_sems.at[3 * i + j],
                                                  device_id=(*chip, c), device_id_type=MESH)
                cp.wait_send()
                cp.wait_recv()

    outs = pl.pallas_call(
        body, name=name, out_shape=[pltpu.HBM(t.shape, t.dtype) for t in sums + lands],
        in_specs=[HBM_SPEC] * (2 * n) + [SEM_SPEC, SEM_SPEC, pl.BlockSpec(memory_space=pl.ANY)],
        out_specs=[HBM_SPEC] * (2 * n), input_output_aliases={i: i for i in range(2 * n)},
        compiler_params=_side_effecting())(*sums, *lands, send_sems, recv_sems, after)
    return list(outs[:n]), list(outs[n:])


def chip_sum(name, sums, parts, k_idx):
    _, h, cc = parts.shape
    tr = min(256, h)

    def body(k_ref, own_ref, p_ref, o_ref):
        acc = own_ref[...].astype(F32)
        for s in range(3):
            acc = acc + p_ref[s].astype(F32)
        o_ref[...] = acc

    gs = pltpu.PrefetchScalarGridSpec(
        num_scalar_prefetch=1, grid=(h // tr,),
        in_specs=[pl.BlockSpec((None, tr, cc), lambda i, kr: (kr[0], i, 0)),
                  pl.BlockSpec((3, tr, cc), lambda i, kr: (0, i, 0))],
        out_specs=pl.BlockSpec((tr, cc), lambda i, kr: (i, 0)))
    return pl.pallas_call(body, name=name, grid_spec=gs, out_shape=SDS((h, cc), F32),
                          compiler_params=_cp(1))(k_idx, sums, parts)


def halves_exchange(name, halves):
    n = len(halves)

    def body(*refs):
        ins, outs = refs[:n], refs[n:2 * n]
        send_sems, recv_sems = refs[2 * n:]
        x, y, c = _mesh_pos()
        cps = []
        for i in range(n):
            cp = pltpu.make_async_remote_copy(
                src_ref=ins[i], dst_ref=outs[i], send_sem=send_sems.at[i], recv_sem=recv_sems.at[i],
                device_id=(x, y, 1 - c), device_id_type=MESH)
            cp.start()
            cps.append(cp)
        for cp in cps:
            cp.wait_recv()
        for cp in cps:
            cp.wait_send()

    hbm = pl.BlockSpec(memory_space=pl.ANY)
    return pl.pallas_call(
        body, name=name, in_specs=[hbm] * n, out_specs=[hbm] * n,
        out_shape=[SDS(t.shape, F32) for t in halves],
        scratch_shapes=[pltpu.SemaphoreType.DMA((n,)), pltpu.SemaphoreType.DMA((n,))])(*halves)


def local_step(x, tgt, vecs, lvec, wa, wx, sinks, rel_bias, w_in, rest_weights, grads_out, grad_in_out):
    buckets = t5_bucket_table()
    band = bias_band(rel_bias.T, buckets).reshape(N_HEADS, BLOCK, 2 * BLOCK)

    proj, h = inproj_fwd(x, vecs, w_in)
    ya, rec = lru_fwd(proj, lvec, wa, wx)
    att = attn_fwd(proj, band, sinks)
    w_lru_out, w_attn_out, w_out, w_ff1, w_ff2 = rest_weights(att)
    w_lru_out2, w_attn_out2, w_out2 = w_lru_out.reshape(D, D), w_attn_out.reshape(D, D), w_out.reshape(D, D)
    w_ff2_2 = w_ff2.reshape(D_FF, D)
    yab, merged = merge_fwd(ya, att, w_lru_out2, w_attn_out2, proj)
    x1, o1 = outproj_fwd(merged, w_out2, x, vecs)
    f, h2 = ff1_fwd(x1, vecs, w_ff1)
    dx2, do2, sums_f, loss = ff2_loss(f, w_ff2_2, x1, tgt, vecs)

    df = ff2_bwd(do2, w_ff2_2, f)
    g_ff2 = weight_grad("dw_ff2", f, do2, 512, (D_FF, D), (WG_TM, 512), lambda i, j: (i, j), relu2=True)
    dx1, do1, sums_2 = ff1_bwd(df, w_ff1, x1, dx2, o1, vecs)
    g_ff1 = weight_grad("dw_ff1", h2, df, 512, (N_CHIPS, D, D), (None, WG_TM, 512), lambda i, j: (j // 4, i, j % 4))
    dyab, dproj = outproj_bwd(do1, w_out2, yab, proj)
    g_out = weight_grad("dw_out", merged, do1, 512, (D, D), (WG_TM, 512), lambda i, j: (i, j))
    drec, dproj = lruout_bwd(dyab, w_lru_out2, rec, proj, dproj)
    g_lru_out = weight_grad("dw_lru_out", ya, dyab[0], 512, (D, D), (WG_TM, 512), lambda i, j: (i, j))
    datt = attnout_bwd(dyab, w_attn_out2)
    g_attn_out = weight_grad("dw_attn_out", att, dyab[1], 512, (D, D), (WG_TM, 512), lambda i, j: (i, j))
    zero = grads_out([g_lru_out.reshape(N_CHIPS, D // 4, D), g_attn_out.reshape(N_CHIPS, D // 4, D),
                      g_out.reshape(N_CHIPS, D // 4, D), g_ff1, g_ff2.reshape(N_CHIPS, D_FF // 4, D)])
    dproj, dkb, dvb, dband, dsink = attn_bwd(proj, band, sinks + zero, datt, dproj)
    dproj = dkv_combine(dkb, dvb, dproj)
    dproj, sums_l, d_wa, d_wx = lru_bwd(proj, rec, drec, lvec, wa, wx, dproj)
    per = IN_SHARD // IN_TILE
    g_in = weight_grad("dw_in", h, dproj, IN_TILE, (N_CHIPS, D, IN_SHARD), (None, WG_TM, IN_TILE),
                       lambda i, j: (j // per, i, j % per))
    zero = grad_in_out(g_in)
    grad_x, sums_1 = inproj_bwd(dproj, w_in, x, dx1, vecs + zero)
    d_rel_bias = bias_band_bwd(dband.reshape(N_HEADS, BLOCK * 2 * BLOCK), buckets)

    small = dict(sums_f=sums_f, sums_2=sums_2, sums_1=sums_1, sums_l=sums_l, d_wa=d_wa, d_wx=d_wx,
                 d_sinks=dsink[:, 0], d_rel_bias=d_rel_bias)
    return loss, grad_x, small


def _pad_rows(a, rows):
    return jnp.concatenate([a, jnp.zeros((rows - a.shape[0], a.shape[1]), a.dtype)], axis=0)


def kernel(x, c, w_ada, b_ada, norm1_g, w_in, conv_w, conv_b, lru_wa, lru_ba, lru_wx, lru_bx, lru_lambda, w_lru_out, w_attn_out, attn_sinks, rel_bias, w_out, norm2_g, w_ff1, w_ff2, final_g, loss_target, m_w_ada, m_b_ada, m_norm1_g, m_w_in, m_conv_w, m_conv_b, m_lru_wa, m_lru_ba, m_lru_wx, m_lru_bx, m_lru_lambda, m_w_lru_out, m_w_attn_out, m_attn_sinks, m_rel_bias, m_w_out, m_norm2_g, m_w_ff1, m_w_ff2, m_final_g, v_w_ada, v_b_ada, v_norm1_g, v_w_in, v_conv_w, v_conv_b, v_lru_wa, v_lru_ba, v_lru_wx, v_lru_bx, v_lru_lambda, v_w_lru_out, v_w_attn_out, v_attn_sinks, v_rel_bias, v_w_out, v_norm2_g, v_w_ff1, v_w_ff2, v_final_g):
    xi, yi, ci = _mesh_pos()
    chip = 2 * xi + yi
    dev = 2 * chip + ci
    z8 = jnp.zeros((8, D), F32)

    conv_rows = jnp.concatenate([conv_w[0], jnp.zeros((4, D - D // 4), F32)], axis=1)
    pack0 = jnp.concatenate([c, conv_rows, jnp.zeros((3, D), F32)], axis=0)
    g0 = all_gather_small("gather_cond", pack0).reshape(N_DEV, 8, D)
    c_all = g0[:, 0, :]
    conv_full = jnp.concatenate([g0[2 * k, 1:5, :D // 4] for k in range(N_CHIPS)], axis=1)
    c16 = jnp.concatenate([c_all, z8], axis=0)
    b_cols = lax.dynamic_slice_in_dim(b_ada, chip * ADA_SHARD, ADA_SHARD, axis=1)
    mod_c = mod_columns(c16, w_ada[0], b_cols)
    g1 = all_gather_small("gather_mod", mod_c).reshape(N_DEV, 16, ADA_SHARD)
    mod = jnp.concatenate([lax.dynamic_index_in_dim(g1[2 * k], dev, axis=0, keepdims=False) for k in range(N_CHIPS)])
    shift1, scale1, gate1, shift2, scale2, gate2 = [mod[i * D:(i + 1) * D] for i in range(6)]
    vecs = jnp.stack([norm1_g[0], scale1, shift1, gate1, norm2_g[0], scale2, shift2, gate2, final_g]
                     + [jnp.zeros((D,), F32)] * 7)
    lvec = jnp.concatenate([conv_full, conv_b, lru_ba, lru_bx, lru_lambda], axis=0)

    shards = [w_in[0], w_lru_out[0], w_attn_out[0], w_out[0], w_ff1[0], w_ff2[0]]
    names = ["w_in", "w_lru_out", "w_attn_out", "w_out", "w_ff1", "w_ff2"]
    k_idx = jnp.reshape(chip, (1,)).astype(jnp.int32)
    c_idx = jnp.reshape(ci, (1,)).astype(jnp.int32)
    slots = [cast_into_slot("cast_" + nm, w, k_idx) for nm, w in zip(names, shards)]
    w_in_full = all_gather_weights(slots[:1])[0]
    g_send, g_recv, in_flight, token = gather_start(slots[1:], w_in_full)
    vecs = vecs + token[0, 0]
    pending = {}

    def rest_weights(after):
        return gather_forward(gather_wait(g_send, g_recv, in_flight, after))

    def start_reduce(tag, nms, grads):
        from_sibling = sibling_exchange("sibling_exchange_" + tag, grads)
        sums = [sibling_sum("sibling_sum_" + nm, g, o, c_idx) for nm, g, o in zip(nms, grads, from_sibling)]
        pending[tag] = exchange_start("exchange_start_" + tag, sums)
        return pending[tag][-1][0, 0]

    loss_t, grad_x, small = local_step(
        x[0], loss_target[0], vecs, lvec, lru_wa[0].astype(BF16), lru_wx[0].astype(BF16),
        attn_sinks[0], rel_bias, w_in_full, rest_weights,
        lambda grads: start_reduce("a", names[1:], grads), lambda g: start_reduce("b", names[:1], [g]))
    loss = lax.psum(loss_t[0, 0], ("x", "y", "c"))

    sums_f, sums_2, sums_1, sums_l = small["sums_f"], small["sums_2"], small["sums_1"], small["sums_l"]
    vec_rows = jnp.stack([sums_1[2], sums_2[2], sums_f[0], sums_l[L_CB], sums_l[L_BA], sums_l[L_BX],
                          sums_l[L_LAM], jnp.zeros((D,), F32)])
    mod_rows = jnp.stack([sums_1[0], sums_1[1], sums_2[3], sums_2[0], sums_2[1], sums_f[1],
                          jnp.zeros((D,), F32), jnp.zeros((D,), F32)])
    att_rows = jnp.concatenate([
        jnp.concatenate([small["d_sinks"], jnp.zeros((D - N_HEADS,), F32)])[None],
        jnp.concatenate([small["d_rel_bias"].reshape(-1), jnp.zeros((D - N_BUCKETS * N_HEADS,), F32)])[None],
        jnp.zeros((6, D), F32)], axis=0)
    pack = jnp.concatenate([vec_rows, _pad_rows(sums_l[0:4], 8), mod_rows, att_rows,
                            small["d_wa"].reshape(128, D), small["d_wx"].reshape(128, D)], axis=0)
    gathered = all_gather_small("gather_small_grads", pack).reshape(N_DEV, P_ROWS, D)
    total = sum_devices(gathered)
    dmod_all = gathered[:, P_MOD:P_MOD + 6, :].reshape(N_DEV, 6 * D)
    dmod16 = jnp.concatenate([lax.dynamic_slice_in_dim(dmod_all, chip * ADA_SHARD, ADA_SHARD, axis=1),
                              jnp.zeros((8, ADA_SHARD), F32)], axis=0)
    g_w_ada, d_w_ada, nm_w_ada, nv_w_ada = wada_update(c16, dmod16, w_ada[0], m_w_ada[0], v_w_ada[0])

    mine, theirs = {}, {}
    for tag, nms in (("a", names[1:]), ("b", names[:1])):
        send_sems, recv_sems, sums, lands, _ = pending[tag]
        sums, lands = exchange_wait("exchange_wait_" + tag, send_sems, recv_sems, sums, lands, grad_x)
        halves = [chip_sum("chip_sum_" + nm, t, p, k_idx) for nm, t, p in zip(nms, sums, lands)]
        for nm, a, b in zip(nms, halves, halves_exchange("halves_exchange_" + tag, halves)):
            mine[nm], theirs[nm] = a, b
    big_m = [m_w_in, m_w_lru_out, m_w_attn_out, m_w_out, m_w_ff1, m_w_ff2]
    big_v = [v_w_in, v_w_lru_out, v_w_attn_out, v_w_out, v_w_ff1, v_w_ff2]
    g_big, d_big, nm_big, nv_big = {}, {}, {}, {}
    for nm, w, m, v in zip(names, shards, big_m, big_v):
        g2, dl, m2, v2 = adamw_big("adamw_" + nm, w, mine[nm], theirs[nm], m[0], v[0], c_idx)
        g_big[nm], d_big[nm], nm_big[nm], nv_big[nm] = g2[None], dl[None], m2[None], v2[None]

    conv_g = lax.dynamic_slice_in_dim(total[P_CONVW:P_CONVW + 4], chip * (D // 4), D // 4, axis=1)
    sm_names = ["b_ada", "norm1_g", "conv_w", "conv_b", "lru_wa", "lru_ba", "lru_wx", "lru_bx", "lru_lambda",
                "attn_sinks", "rel_bias", "norm2_g", "final_g"]
    sm_w = [b_ada.reshape(6, D), norm1_g, conv_w[0], conv_b, lru_wa.reshape(128, D), lru_ba, lru_wx.reshape(128, D),
            lru_bx, lru_lambda, attn_sinks, rel_bias, norm2_g, final_g[None]]
    sm_m = [m_b_ada.reshape(6, D), m_norm1_g, m_conv_w[0], m_conv_b, m_lru_wa.reshape(128, D), m_lru_ba,
            m_lru_wx.reshape(128, D), m_lru_bx, m_lru_lambda, m_attn_sinks, m_rel_bias, m_norm2_g, m_final_g[None]]
    sm_v = [v_b_ada.reshape(6, D), v_norm1_g, v_conv_w[0], v_conv_b, v_lru_wa.reshape(128, D), v_lru_ba,
            v_lru_wx.reshape(128, D), v_lru_bx, v_lru_lambda, v_attn_sinks, v_rel_bias, v_norm2_g, v_final_g[None]]
    sm_g = [total[P_MOD:P_MOD + 6], total[0:1], conv_g, total[3:4], total[P_WA:P_WA + 128], total[4:5],
            total[P_WX:P_WX + 128], total[5:6], total[6:7], total[P_ATT:P_ATT + 1, :N_HEADS],
            total[P_ATT + 1, :N_BUCKETS * N_HEADS].reshape(N_BUCKETS, N_HEADS), total[1:2], total[2:3]]
    sm_d, sm_nm, sm_nv = adamw_small(sm_w, sm_g, sm_m, sm_v)
    shapes = dict(b_ada=b_ada.shape, norm1_g=norm1_g.shape, conv_w=conv_w.shape, conv_b=conv_b.shape,
                  lru_wa=lru_wa.shape, lru_ba=lru_ba.shape, lru_wx=lru_wx.shape, lru_bx=lru_bx.shape,
                  lru_lambda=lru_lambda.shape, attn_sinks=attn_sinks.shape, rel_bias=rel_bias.shape,
                  norm2_g=norm2_g.shape, final_g=final_g.shape)
    grads = dict(w_ada=g_w_ada[None], **g_big)
    deltas = dict(w_ada=d_w_ada[None], **d_big)
    new_m = dict(w_ada=nm_w_ada[None], **nm_big)
    new_v = dict(w_ada=nv_w_ada[None], **nv_big)
    for i, nm in enumerate(sm_names):
        grads[nm] = sm_g[i].reshape(shapes[nm])
        deltas[nm] = sm_d[i].reshape(shapes[nm])
        new_m[nm] = sm_nm[i].reshape(shapes[nm])
        new_v[nm] = sm_nv[i].reshape(shapes[nm])
    order = ["w_ada", "b_ada", "norm1_g", "w_in", "conv_w", "conv_b", "lru_wa", "lru_ba", "lru_wx", "lru_bx",
             "lru_lambda", "w_lru_out", "w_attn_out", "attn_sinks", "rel_bias", "w_out", "norm2_g", "w_ff1", "w_ff2",
             "final_g"]
    return (loss, grad_x[None], *[grads[n] for n in order], *[deltas[n] for n in order],
            *[new_m[n] for n in order], *[new_v[n] for n in order])
```

```python
import math

import numpy as np
import jax
import jax.numpy as jnp
from jax import lax
from jax.experimental import pallas as pl
from jax.experimental.pallas import tpu as pltpu

F32 = jnp.float32
BF16 = jnp.bfloat16
SDS = jax.ShapeDtypeStruct
MESH = pl.DeviceIdType.MESH

D = 2048
D_FF = 4 * D
N_HEADS = 32
HEAD_DIM = 64
BLOCK = 128
N_LRU_BLOCKS = 16
LRU_C = 8.0
EPS = 1e-6
NEG_INF = -1e30
N_BUCKETS = 32
MAX_DISTANCE = 128
IN_W = 10752
IN_SHARD = IN_W // 4
IN_TILE = 896
ADA_SHARD = 6 * D // 4
OFF_LRU, OFF_GATE, OFF_Q, OFF_K, OFF_V, OFF_GA, OFF_GB = 0, 2048, 4096, 6144, 6400, 6656, 8704
SCALE = HEAD_DIM ** -0.5
N_CHIPS = 4
N_DEV = 8

ADAM_LR, ADAM_B1, ADAM_B2, ADAM_EPS, ADAM_WD, ADAM_STEP = 0.001, 0.9, 0.999, 1e-08, 0.01, 10
ADAM_C1 = 1.0 - ADAM_B1 ** ADAM_STEP
ADAM_C2 = 1.0 - ADAM_B2 ** ADAM_STEP

VMEM_LIMIT = 52 * 2 ** 20
SUB = 128
WG_TM = 1024
V_G1, V_SCALE1, V_SHIFT1, V_GATE1, V_G2, V_SCALE2, V_SHIFT2, V_GATE2, V_G3 = range(9)
L_CW0, L_CB, L_BA, L_BX, L_LAM = 0, 4, 5, 6, 7
P_VEC, P_CONVW, P_MOD, P_ATT, P_WA, P_WX, P_ROWS = 0, 8, 16, 24, 32, 160, 288


def _cp(n_axes):
    return pltpu.CompilerParams(dimension_semantics=("arbitrary",) * n_axes, vmem_limit_bytes=VMEM_LIMIT)


def _dot(a, b):
    return jnp.dot(a, b, preferred_element_type=F32)


def _dot_nt(a, b):
    return lax.dot_general(a, b, (((1,), (1,)), ((), ())), preferred_element_type=F32)


def _dot_tn(a, b):
    return lax.dot_general(a, b, (((0,), (0,)), ((), ())), preferred_element_type=F32)


_G0 = math.sqrt(2.0 / math.pi)
_G1 = 0.044715


def _gelu(x):
    return 0.5 * x * (1.0 + jnp.tanh(_G0 * (x + _G1 * x * x * x)))


def _gelu_grad(x):
    x2 = x * x
    t = jnp.tanh(_G0 * (x + _G1 * x * x2))
    return 0.5 * (1.0 + t) + 0.5 * x * (1.0 - t * t) * _G0 * (1.0 + 3.0 * _G1 * x2)


def _expm1(x):
    u = jnp.exp(x)
    um1 = u - 1.0
    k = um1 * x / jnp.log(u)
    return jnp.where(um1 == 0.0, x, jnp.where(u < 0.5, um1, k))


def _softplus(z):
    e = jnp.exp(-jnp.abs(z))
    u = 1.0 + e
    l1p = jnp.where(u == 1.0, e, jnp.log(u) * e / (u - 1.0))
    return jnp.maximum(z, 0.0) + l1p


def _adamw_math(w, g, m, v):
    m2 = ADAM_B1 * m + (1.0 - ADAM_B1) * g
    v2 = ADAM_B2 * v + (1.0 - ADAM_B2) * (g * g)
    m_hat = m2 / ADAM_C1
    v_hat = v2 / ADAM_C2
    delta = -ADAM_LR * (m_hat / (jnp.sqrt(v_hat) + ADAM_EPS) + ADAM_WD * w)
    return delta, m2, v2


def _rms_parts(xv):
    r = lax.rsqrt(jnp.mean(xv * xv, axis=-1, keepdims=True) + EPS)
    return r, xv * r


def _row_fetches(hbm_refs, bufs, sems, i, rows):
    return [pltpu.make_async_copy(h.at[pl.ds(i * rows, rows), :], b, sems.at[n])
            for n, (h, b) in enumerate(zip(hbm_refs, bufs))]


def _modulated_norm(x_ref, v_ref, row_g, row_scale, row_shift, h_ref, rows):
    g, scale, shift = v_ref[row_g:row_g + 1, :], v_ref[row_scale:row_scale + 1, :], v_ref[row_shift:row_shift + 1, :]

    def sub(rb, carry):
        rs = pl.ds(pl.multiple_of(rb * SUB, SUB), SUB)
        _, xh = _rms_parts(x_ref[rs, :])
        h_ref[rs, :] = ((xh * g) * (1.0 + scale) + shift).astype(BF16)
        return carry

    lax.fori_loop(0, rows // SUB, sub, 0)


def inproj_fwd(x, vecs, w_in):
    s = x.shape[0]
    tm = min(1024, s)
    per = IN_SHARD // IN_TILE

    def body(x_ref, v_ref, w_ref, proj_ref, h_ref):
        @pl.when(pl.program_id(1) == 0)
        def _():
            _modulated_norm(x_ref, v_ref, V_G1, V_SCALE1, V_SHIFT1, h_ref, tm)
        proj_ref[...] = _dot(h_ref[...], w_ref[...]).astype(BF16)

    return pl.pallas_call(
        body, name="inproj_fwd", grid=(s // tm, IN_W // IN_TILE),
        in_specs=[pl.BlockSpec((tm, D), lambda i, j: (i, 0)),
                  pl.BlockSpec((16, D), lambda i, j: (0, 0)),
                  pl.BlockSpec((None, D, IN_TILE), lambda i, j: (j // per, 0, j % per))],
        out_specs=[pl.BlockSpec((tm, IN_TILE), lambda i, j: (i, j)),
                   pl.BlockSpec((tm, D), lambda i, j: (i, 0))],
        out_shape=[SDS((s, IN_W), BF16), SDS((s, D), BF16)],
        compiler_params=_cp(2))(x, vecs, w_in)


def _lru_block_fwd(xbuf, lv_ref, wa_ref, wx_ref, b, t, first):
    cs = slice(b * 128, (b + 1) * 128)
    x0 = xbuf[pl.ds(8, t), cs]
    x1 = xbuf[pl.ds(7, t), cs]
    x2 = xbuf[pl.ds(6, t), cs]
    x3 = xbuf[pl.ds(5, t), cs]
    xc = (lv_ref[L_CB:L_CB + 1, cs] + lv_ref[3:4, cs] * x0 + lv_ref[2:3, cs] * x1
          + lv_ref[1:2, cs] * x2 + lv_ref[0:1, cs] * x3)
    xcb = xc.astype(BF16)
    r = jax.nn.sigmoid(_dot(xcb, wa_ref[b]) + lv_ref[L_BA:L_BA + 1, cs])
    ig = jax.nn.sigmoid(_dot(xcb, wx_ref[b]) + lv_ref[L_BX:L_BX + 1, cs])
    sp = _softplus(-lv_ref[L_LAM:L_LAM + 1, cs])
    log_a = (-LRU_C) * r * sp
    a = jnp.exp(log_a)
    mult = jnp.where(first, 1.0, jnp.sqrt(-_expm1(2.0 * log_a)))
    return (x0, x1, x2, x3), xc, xcb, r, ig, sp, a, mult


def lru_fwd(proj, lvec, wa, wx):
    s = proj.shape[0]
    t = min(256, s)

    def body(lx_ref, gate_ref, lv_ref, wa_ref, wx_ref, ya_ref, rec_ref, xbuf, a_s, u_s, hc):
        i = pl.program_id(0)

        @pl.when(i == 0)
        def _():
            xbuf[pl.ds(0, 8), :] = jnp.zeros((8, D), F32)
            hc[...] = jnp.zeros((8, D), F32)

        @pl.when(i > 0)
        def _():
            xbuf[pl.ds(0, 8), :] = xbuf[pl.ds(t, 8), :]

        xbuf[pl.ds(8, t), :] = lx_ref[...].astype(F32)
        first = (lax.broadcasted_iota(jnp.int32, (t, 128), 0) + i * t) == 0
        for b in range(N_LRU_BLOCKS):
            cs = slice(b * 128, (b + 1) * 128)
            _, xc, _, _, ig, _, a, mult = _lru_block_fwd(xbuf, lv_ref, wa_ref, wx_ref, b, t, first)
            a_s[:, cs] = a
            u_s[:, cs] = mult * (ig * xc)

        def step(tt, h):
            h = a_s[pl.ds(tt, 1), :] * h + u_s[pl.ds(tt, 1), :]
            rec_ref[pl.ds(tt, 1), :] = h
            return h

        hc[0:1, :] = lax.fori_loop(0, t, step, hc[0:1, :], unroll=8)
        for b in range(N_LRU_BLOCKS):
            cs = slice(b * 128, (b + 1) * 128)
            ya_ref[:, cs] = (rec_ref[:, cs] * _gelu(gate_ref[:, cs].astype(F32))).astype(BF16)

    return pl.pallas_call(
        body, name="lru_fwd", grid=(s // t,),
        in_specs=[pl.BlockSpec((t, D), lambda i: (i, OFF_LRU // D)),
                  pl.BlockSpec((t, D), lambda i: (i, OFF_GATE // D)),
                  pl.BlockSpec((8, D), lambda i: (0, 0)),
                  pl.BlockSpec((N_LRU_BLOCKS, 128, 128), lambda i: (0, 0, 0)),
                  pl.BlockSpec((N_LRU_BLOCKS, 128, 128), lambda i: (0, 0, 0))],
        out_specs=[pl.BlockSpec((t, D), lambda i: (i, 0)), pl.BlockSpec((t, D), lambda i: (i, 0))],
        out_shape=[SDS((s, D), BF16), SDS((s, D), F32)],
        scratch_shapes=[pltpu.VMEM((t + 8, D), F32), pltpu.VMEM((t, D), F32), pltpu.VMEM((t, D), F32),
                        pltpu.VMEM((8, D), F32)],
        compiler_params=_cp(1))(proj, proj, lvec, wa, wx)


def t5_bucket_table():
    qi = np.arange(BLOCK)[:, None]
    ki = np.arange(2 * BLOCK)[None, :]
    rel = qi + BLOCK - ki
    relc = np.maximum(rel, 0)
    max_exact = N_BUCKETS // 2
    relf = np.maximum(relc, 1).astype(np.float32)
    large = max_exact + (np.log(relf / np.float32(max_exact)) / np.float32(math.log(MAX_DISTANCE / max_exact))
                         * np.float32(N_BUCKETS - max_exact)).astype(np.int32)
    large = np.minimum(large, N_BUCKETS - 1)
    bucket = np.where(relc < max_exact, relc, large)
    bucket = np.where((rel >= 0) & (rel < BLOCK), bucket, -1)
    return jnp.asarray(bucket.reshape(1, BLOCK * 2 * BLOCK), jnp.int32)


def bias_band(rel_bias_t, buckets):
    n = BLOCK * 2 * BLOCK
    tn = 4096

    def body(bk_ref, rb_ref, o_ref):
        row = lax.broadcasted_iota(jnp.int32, (N_BUCKETS, tn), 0)
        oh = jnp.where(row == bk_ref[...], 1.0, 0.0).astype(BF16)
        rb = rb_ref[...]
        p0 = rb.astype(BF16)
        r1 = rb - p0.astype(F32)
        p1 = r1.astype(BF16)
        p2 = (r1 - p1.astype(F32)).astype(BF16)
        o_ref[...] = _dot(p0, oh) + _dot(p1, oh) + _dot(p2, oh)

    return pl.pallas_call(
        body, name="bias_band", grid=(n // tn,),
        in_specs=[pl.BlockSpec((1, tn), lambda i: (0, i)), pl.BlockSpec((N_HEADS, N_BUCKETS), lambda i: (0, 0))],
        out_specs=pl.BlockSpec((N_HEADS, tn), lambda i: (0, i)),
        out_shape=SDS((N_HEADS, n), F32), compiler_params=_cp(1))(buckets, rel_bias_t)


def bias_band_bwd(dband, buckets):
    n = BLOCK * 2 * BLOCK
    tn = 4096

    def body(bk_ref, d_ref, o_ref):
        @pl.when(pl.program_id(0) == 0)
        def _():
            o_ref[...] = jnp.zeros_like(o_ref)
        row = lax.broadcasted_iota(jnp.int32, (N_BUCKETS, tn), 0)
        oh = jnp.where(row == bk_ref[...], 1.0, 0.0).astype(BF16)
        dv = d_ref[...]
        p0 = dv.astype(BF16)
        r1 = dv - p0.astype(F32)
        p1 = r1.astype(BF16)
        p2 = (r1 - p1.astype(F32)).astype(BF16)
        o_ref[...] += _dot_nt(oh, p0) + _dot_nt(oh, p1) + _dot_nt(oh, p2)

    return pl.pallas_call(
        body, name="bias_band_bwd", grid=(n // tn,),
        in_specs=[pl.BlockSpec((1, tn), lambda i: (0, i)), pl.BlockSpec((N_HEADS, tn), lambda i: (0, i))],
        out_specs=pl.BlockSpec((N_BUCKETS, N_HEADS), lambda i: (0, 0)),
        out_shape=SDS((N_BUCKETS, N_HEADS), F32), compiler_params=_cp(1))(buckets, dband)


def _dup_half(band, which):
    lane = lax.broadcasted_iota(jnp.int32, band.shape, 1)
    rolled = pltpu.roll(band, 64, 1)
    keep = (lane < 64) if which == 0 else (lane >= 64)
    return jnp.where(keep, band, rolled)


def _attn_probs(qm, kk, bias, sink, valid):
    sc = _dot_nt(qm, kk) * SCALE + bias
    sc = jnp.where(valid, sc, NEG_INF)
    m = jnp.maximum(jnp.max(sc, axis=-1, keepdims=True), sink)
    e = jnp.exp(sc - m)
    es = jnp.exp(sink - m)
    inv = 1.0 / (jnp.sum(e, axis=-1, keepdims=True) + es)
    return e * inv, es * inv


def _band_valid(n):
    qi = lax.broadcasted_iota(jnp.int32, (BLOCK, 2 * BLOCK), 0)
    ki = lax.broadcasted_iota(jnp.int32, (BLOCK, 2 * BLOCK), 1)
    rel = qi + BLOCK - ki
    return (rel >= 0) & (rel < BLOCK) & ((ki >= BLOCK) | (n > 0))


def _kv_bands(prev_ref, cur_ref):
    band = jnp.concatenate([prev_ref[...].astype(F32), cur_ref[...].astype(F32)], axis=0)
    return [_dup_half(band, 0).astype(BF16), _dup_half(band, 1).astype(BF16)]


def attn_fwd(proj, band, sinks):
    s = proj.shape[0]
    nb = s // BLOCK
    qw = 1024

    def body(sk_ref, q_ref, kp_ref, kc_ref, vp_ref, vc_ref, b_ref, o_ref):
        n = pl.program_id(0)
        gp = pl.program_id(1)
        valid = _band_valid(n)
        kks = _kv_bands(kp_ref, kc_ref)
        vvs = _kv_bands(vp_ref, vc_ref)
        lane = lax.broadcasted_iota(jnp.int32, (BLOCK, 128), 1)
        for j in range(8):
            qs = q_ref[:, j * 128:(j + 1) * 128]
            outs = []
            for hh in range(2):
                hl = 2 * j + hh
                qm = jnp.where((lane < 64) if hh == 0 else (lane >= 64), qs, jnp.zeros_like(qs))
                p, _ = _attn_probs(qm, kks[j // 4], b_ref[hl], sk_ref[gp * 16 + hl], valid)
                outs.append(_dot(p.astype(BF16), vvs[j // 4]))
            o_ref[:, j * 128:(j + 1) * 128] = jnp.where(lane < 64, outs[0], outs[1]).astype(BF16)

    kb, vb = OFF_K // 128, OFF_V // 128
    return pl.pallas_call(
        body, name="attn_fwd", grid=(nb, 2),
        in_specs=[pl.BlockSpec(memory_space=pltpu.SMEM),
                  pl.BlockSpec((BLOCK, qw), lambda n, g: (n, OFF_Q // qw + g)),
                  pl.BlockSpec((BLOCK, 128), lambda n, g: (jnp.maximum(n - 1, 0), kb + g)),
                  pl.BlockSpec((BLOCK, 128), lambda n, g: (n, kb + g)),
                  pl.BlockSpec((BLOCK, 128), lambda n, g: (jnp.maximum(n - 1, 0), vb + g)),
                  pl.BlockSpec((BLOCK, 128), lambda n, g: (n, vb + g)),
                  pl.BlockSpec((16, BLOCK, 2 * BLOCK), lambda n, g: (g, 0, 0))],
        out_specs=pl.BlockSpec((BLOCK, qw), lambda n, g: (n, g)),
        out_shape=SDS((s, D), BF16), compiler_params=_cp(2))(sinks, proj, proj, proj, proj, proj, band)


def merge_fwd(ya, att, w_lru_out, w_attn_out, proj):
    s = ya.shape[0]
    tm, tn = min(1024, s), 512

    def body(ya_ref, at_ref, wl_ref, wt_ref, ga_ref, gb_ref, yab_ref, mg_ref):
        y_a = _dot(ya_ref[...], wl_ref[...])
        y_b = _dot(at_ref[...], wt_ref[...])
        yab_ref[0] = y_a.astype(BF16)
        yab_ref[1] = y_b.astype(BF16)
        mg_ref[...] = (jax.nn.sigmoid(ga_ref[...].astype(F32)) * y_a
                       + jax.nn.sigmoid(gb_ref[...].astype(F32)) * y_b).astype(BF16)

    return pl.pallas_call(
        body, name="merge_fwd", grid=(s // tm, D // tn),
        in_specs=[pl.BlockSpec((tm, D), lambda i, j: (i, 0)), pl.BlockSpec((tm, D), lambda i, j: (i, 0)),
                  pl.BlockSpec((D, tn), lambda i, j: (0, j)), pl.BlockSpec((D, tn), lambda i, j: (0, j)),
                  pl.BlockSpec((tm, tn), lambda i, j: (i, OFF_GA // tn + j)),
                  pl.BlockSpec((tm, tn), lambda i, j: (i, OFF_GB // tn + j))],
        out_specs=[pl.BlockSpec((2, tm, tn), lambda i, j: (0, i, j)), pl.BlockSpec((tm, tn), lambda i, j: (i, j))],
        out_shape=[SDS((2, s, D), BF16), SDS((s, D), BF16)],
        compiler_params=_cp(2))(ya, att, w_lru_out, w_attn_out, proj, proj)


def outproj_fwd(merged, w_out, x, vecs):
    s = x.shape[0]
    tm, tn = min(1024, s), 512

    def body(m_ref, w_ref, x_ref, v_ref, x1_ref, o1_ref):
        o1 = _dot(m_ref[...], w_ref[...])
        o1_ref[...] = o1.astype(BF16)
        x1_ref[...] = x_ref[...] + v_ref[V_GATE1:V_GATE1 + 1, :] * o1

    return pl.pallas_call(
        body, name="outproj_fwd", grid=(s // tm, D // tn),
        in_specs=[pl.BlockSpec((tm, D), lambda i, j: (i, 0)), pl.BlockSpec((D, tn), lambda i, j: (0, j)),
                  pl.BlockSpec((tm, tn), lambda i, j: (i, j)), pl.BlockSpec((16, tn), lambda i, j: (0, j))],
        out_specs=[pl.BlockSpec((tm, tn), lambda i, j: (i, j)), pl.BlockSpec((tm, tn), lambda i, j: (i, j))],
        out_shape=[SDS((s, D), F32), SDS((s, D), BF16)],
        compiler_params=_cp(2))(merged, w_out, x, vecs)


def ff1_fwd(x1, vecs, w_ff1):
    s = x1.shape[0]
    tm, tn = min(1024, s), 512
    per = D // tn

    def body(x_ref, v_ref, w_ref, f_ref, h_ref):
        @pl.when(pl.program_id(1) == 0)
        def _():
            _modulated_norm(x_ref, v_ref, V_G2, V_SCALE2, V_SHIFT2, h_ref, tm)
        f_ref[...] = _dot(h_ref[...], w_ref[...]).astype(BF16)

    return pl.pallas_call(
        body, name="ff1_fwd", grid=(s // tm, D_FF // tn),
        in_specs=[pl.BlockSpec((tm, D), lambda i, j: (i, 0)), pl.BlockSpec((16, D), lambda i, j: (0, 0)),
                  pl.BlockSpec((None, D, tn), lambda i, j: (j // per, 0, j % per))],
        out_specs=[pl.BlockSpec((tm, tn), lambda i, j: (i, j)), pl.BlockSpec((tm, D), lambda i, j: (i, 0))],
        out_shape=[SDS((s, D_FF), BF16), SDS((s, D), BF16)],
        compiler_params=_cp(2))(x1, vecs, w_ff1)


def ff2_loss(f, w_ff2, x1, tgt, vecs):
    s = x1.shape[0]
    tm, tk = min(512, s), 1024
    nk = D_FF // tk

    def body(f_ref, w_ref, x1_hbm, t_hbm, v_ref, dx2_ref, do2_ref, sums_ref, loss_ref, acc, x1_ref, t_ref, sems):
        i, k = pl.program_id(0), pl.program_id(1)
        fetches = _row_fetches((x1_hbm, t_hbm), (x1_ref, t_ref), sems, i, tm)

        @pl.when((i == 0) & (k == 0))
        def _():
            sums_ref[...] = jnp.zeros_like(sums_ref)
            loss_ref[...] = jnp.zeros_like(loss_ref)

        @pl.when(k == 0)
        def _():
            acc[...] = jnp.zeros_like(acc)
            for cp in fetches:
                cp.start()

        fv = jnp.maximum(f_ref[...].astype(F32), 0.0)
        acc[...] += _dot((fv * fv).astype(BF16), w_ref[...])

        @pl.when(k == nk - 1)
        def _():
            for cp in fetches:
                cp.wait()
            gate2 = v_ref[V_GATE2:V_GATE2 + 1, :]
            g3 = v_ref[V_G3:V_G3 + 1, :]

            def sub(rb, carry):
                rs = pl.ds(pl.multiple_of(rb * SUB, SUB), SUB)
                o2 = acc[rs, :]
                x2 = x1_ref[rs, :] + gate2 * o2
                r3, xh = _rms_parts(x2)
                e = xh * g3 - t_ref[rs, :]
                loss_ref[...] += (0.5 / D) * jnp.sum(e * e)
                dy = e * (1.0 / D)
                sums_ref[0:1, :] += jnp.sum(dy * xh, axis=0, keepdims=True)
                dxh = dy * g3
                dx2 = r3 * (dxh - xh * jnp.mean(dxh * xh, axis=-1, keepdims=True))
                sums_ref[1:2, :] += jnp.sum(dx2 * o2, axis=0, keepdims=True)
                dx2_ref[rs, :] = dx2
                do2_ref[rs, :] = (dx2 * gate2).astype(BF16)
                return carry

            lax.fori_loop(0, tm // SUB, sub, 0)

    return pl.pallas_call(
        body, name="ff2_loss", grid=(s // tm, nk),
        in_specs=[pl.BlockSpec((tm, tk), lambda i, k: (i, k)), pl.BlockSpec((tk, D), lambda i, k: (k, 0)),
                  pl.BlockSpec(memory_space=pl.ANY), pl.BlockSpec(memory_space=pl.ANY),
                  pl.BlockSpec((16, D), lambda i, k: (0, 0))],
        out_specs=[pl.BlockSpec((tm, D), lambda i, k: (i, 0)), pl.BlockSpec((tm, D), lambda i, k: (i, 0)),
                   pl.BlockSpec((8, D), lambda i, k: (0, 0)), pl.BlockSpec((8, 128), lambda i, k: (0, 0))],
        out_shape=[SDS((s, D), F32), SDS((s, D), BF16), SDS((8, D), F32), SDS((8, 128), F32)],
        scratch_shapes=[pltpu.VMEM((tm, D), F32), pltpu.VMEM((tm, D), F32), pltpu.VMEM((tm, D), F32),
                        pltpu.SemaphoreType.DMA((2,))],
        compiler_params=_cp(2))(f, w_ff2, x1, tgt, vecs)


def ff2_bwd(do2, w_ff2, f):
    s = do2.shape[0]
    tm, tn = min(1024, s), 512

    def body(d_ref, w_ref, f_ref, o_ref):
        dff = _dot_nt(d_ref[...], w_ref[...])
        o_ref[...] = (dff * (2.0 * jnp.maximum(f_ref[...].astype(F32), 0.0))).astype(BF16)

    return pl.pallas_call(
        body, name="ff2_bwd", grid=(s // tm, D_FF // tn),
        in_specs=[pl.BlockSpec((tm, D), lambda i, j: (i, 0)), pl.BlockSpec((tn, D), lambda i, j: (j, 0)),
                  pl.BlockSpec((tm, tn), lambda i, j: (i, j))],
        out_specs=pl.BlockSpec((tm, tn), lambda i, j: (i, j)),
        out_shape=SDS((s, D_FF), BF16), compiler_params=_cp(2))(do2, w_ff2, f)


def weight_grad(name, a, b, tn, out_shape, out_block, out_map, relu2=False):
    s, m = a.shape
    n = b.shape[1]
    tm = WG_TM
    chunk = min(1024, s)
    nch = s // chunk

    def body(a_hbm, b_ref, o_ref, a_buf, at_s, sem):
        i = pl.program_id(0)

        @pl.when(pl.program_id(1) == 0)
        def _():
            def fetch(ch):
                return pltpu.make_async_copy(a_hbm.at[pl.ds(ch * chunk, chunk), pl.ds(i * tm, tm)],
                                             a_buf.at[ch % 2], sem.at[ch % 2])
            fetch(0).start()
            for ch in range(nch):
                if ch + 1 < nch:
                    fetch(ch + 1).start()
                fetch(ch).wait()
                av = a_buf[ch % 2]
                if relu2:
                    fv = jnp.maximum(av.astype(F32), 0.0)
                    av = (fv * fv).astype(BF16)
                at_s[:, ch * chunk:(ch + 1) * chunk] = av.T

        o_ref[...] = _dot(at_s[...], b_ref[...]).astype(BF16)

    return pl.pallas_call(
        body, name=name, grid=(m // tm, n // tn),
        in_specs=[pl.BlockSpec(memory_space=pl.ANY), pl.BlockSpec((s, tn), lambda i, j: (0, j))],
        out_specs=pl.BlockSpec(out_block, lambda i, j: out_map(i, j)),
        out_shape=SDS(out_shape, BF16),
        scratch_shapes=[pltpu.VMEM((2, chunk, tm), BF16), pltpu.VMEM((tm, s), BF16), pltpu.SemaphoreType.DMA((2,))],
        compiler_params=_cp(2))(a, b)


def ff1_bwd(df, w_ff1, x1, dx2, o1, vecs):
    s = df.shape[0]
    tm, tk = min(512, s), 1024
    nk = D_FF // tk
    per = D // tk

    def body(d_ref, w_ref, x1_hbm, dx2_hbm, o1_hbm, v_ref, dx1_ref, do1_ref, sums_ref, acc, x1_ref, dx2_ref, o1_ref, sems):
        i, k = pl.program_id(0), pl.program_id(1)
        fetches = _row_fetches((x1_hbm, dx2_hbm, o1_hbm), (x1_ref, dx2_ref, o1_ref), sems, i, tm)

        @pl.when((i == 0) & (k == 0))
        def _():
            sums_ref[...] = jnp.zeros_like(sums_ref)

        @pl.when(k == 0)
        def _():
            acc[...] = jnp.zeros_like(acc)
            for cp in fetches:
                cp.start()

        acc[...] += _dot_nt(d_ref[...], w_ref[...])

        @pl.when(k == nk - 1)
        def _():
            for cp in fetches:
                cp.wait()
            g2 = v_ref[V_G2:V_G2 + 1, :]
            scale2 = v_ref[V_SCALE2:V_SCALE2 + 1, :]
            gate1 = v_ref[V_GATE1:V_GATE1 + 1, :]

            def sub(rb, carry):
                rs = pl.ds(pl.multiple_of(rb * SUB, SUB), SUB)
                dh = acc[rs, :]
                r2, xh = _rms_parts(x1_ref[rs, :])
                sums_ref[0:1, :] += jnp.sum(dh, axis=0, keepdims=True)
                sums_ref[1:2, :] += jnp.sum(dh * (xh * g2), axis=0, keepdims=True)
                dxn = dh * (1.0 + scale2)
                sums_ref[2:3, :] += jnp.sum(dxn * xh, axis=0, keepdims=True)
                dxh = dxn * g2
                dx1 = dx2_ref[rs, :] + r2 * (dxh - xh * jnp.mean(dxh * xh, axis=-1, keepdims=True))
                sums_ref[3:4, :] += jnp.sum(dx1 * o1_ref[rs, :].astype(F32), axis=0, keepdims=True)
                dx1_ref[rs, :] = dx1
                do1_ref[rs, :] = (dx1 * gate1).astype(BF16)
                return carry

            lax.fori_loop(0, tm // SUB, sub, 0)

    return pl.pallas_call(
        body, name="ff1_bwd", grid=(s // tm, nk),
        in_specs=[pl.BlockSpec((tm, tk), lambda i, k: (i, k)),
                  pl.BlockSpec((None, D, tk), lambda i, k: (k // per, 0, k % per)),
                  pl.BlockSpec(memory_space=pl.ANY), pl.BlockSpec(memory_space=pl.ANY),
                  pl.BlockSpec(memory_space=pl.ANY), pl.BlockSpec((16, D), lambda i, k: (0, 0))],
        out_specs=[pl.BlockSpec((tm, D), lambda i, k: (i, 0)), pl.BlockSpec((tm, D), lambda i, k: (i, 0)),
                   pl.BlockSpec((8, D), lambda i, k: (0, 0))],
        out_shape=[SDS((s, D), F32), SDS((s, D), BF16), SDS((8, D), F32)],
        scratch_shapes=[pltpu.VMEM((tm, D), F32), pltpu.VMEM((tm, D), F32), pltpu.VMEM((tm, D), F32),
                        pltpu.VMEM((tm, D), BF16), pltpu.SemaphoreType.DMA((3,))],
        compiler_params=_cp(2))(df, w_ff1, x1, dx2, o1, vecs)


def outproj_bwd(do1, w_out, yab, proj):
    s = do1.shape[0]
    tm, tn = min(1024, s), 512
    per = D // tn

    def body(d_ref, w_ref, y_ref, g_ref, dy_ref, dp_ref):
        dm = _dot_nt(d_ref[...], w_ref[...])
        sg = jax.nn.sigmoid(g_ref[...].astype(F32))
        dy_ref[...] = (dm * sg).astype(BF16)
        dp_ref[...] = (dm * y_ref[...].astype(F32) * sg * (1.0 - sg)).astype(BF16)

    return pl.pallas_call(
        body, name="outproj_bwd", grid=(s // tm, 2 * per),
        in_specs=[pl.BlockSpec((tm, D), lambda i, j: (i, 0)), pl.BlockSpec((tn, D), lambda i, j: (j % per, 0)),
                  pl.BlockSpec((None, tm, tn), lambda i, j: (j // per, i, j % per)),
                  pl.BlockSpec((tm, tn), lambda i, j: (i, OFF_GA // tn + j))],
        out_specs=[pl.BlockSpec((None, tm, tn), lambda i, j: (j // per, i, j % per)),
                   pl.BlockSpec((tm, tn), lambda i, j: (i, OFF_GA // tn + j))],
        out_shape=[SDS((2, s, D), BF16), SDS((s, IN_W), BF16)],
        compiler_params=_cp(2))(do1, w_out, yab, proj)


def lruout_bwd(dyab, w_lru_out, rec, proj, dproj):
    s = rec.shape[0]
    tm, tn = min(1024, s), 512

    def body(d_ref, w_ref, r_ref, g_ref, dp_in, dr_ref, dp_ref):
        dya = _dot_nt(d_ref[...], w_ref[...])
        gate = g_ref[...].astype(F32)
        dr_ref[...] = dya * _gelu(gate)
        dp_ref[...] = (dya * r_ref[...] * _gelu_grad(gate)).astype(BF16)

    return pl.pallas_call(
        body, name="lruout_bwd", grid=(s // tm, D // tn),
        in_specs=[pl.BlockSpec((None, tm, D), lambda i, j: (0, i, 0)), pl.BlockSpec((tn, D), lambda i, j: (j, 0)),
                  pl.BlockSpec((tm, tn), lambda i, j: (i, j)),
                  pl.BlockSpec((tm, tn), lambda i, j: (i, OFF_GATE // tn + j)),
                  pl.BlockSpec(memory_space=pl.ANY)],
        out_specs=[pl.BlockSpec((tm, tn), lambda i, j: (i, j)),
                   pl.BlockSpec((tm, tn), lambda i, j: (i, OFF_GATE // tn + j))],
        out_shape=[SDS((s, D), F32), SDS((s, IN_W), BF16)],
        input_output_aliases={4: 1},
        compiler_params=_cp(2))(dyab, w_lru_out, rec, proj, dproj)


def attnout_bwd(dyab, w_attn_out):
    s = dyab.shape[1]
    tm, tn = min(1024, s), 512

    def body(d_ref, w_ref, o_ref):
        o_ref[...] = _dot_nt(d_ref[...], w_ref[...]).astype(BF16)

    return pl.pallas_call(
        body, name="attnout_bwd", grid=(s // tm, D // tn),
        in_specs=[pl.BlockSpec((None, tm, D), lambda i, j: (1, i, 0)), pl.BlockSpec((tn, D), lambda i, j: (j, 0))],
        out_specs=pl.BlockSpec((tm, tn), lambda i, j: (i, j)),
        out_shape=SDS((s, D), BF16), compiler_params=_cp(2))(dyab, w_attn_out)


def attn_bwd(proj, band, sinks, datt, dproj):
    s = proj.shape[0]
    nb = s // BLOCK
    qw = 1024

    def body(sk_ref, q_ref, kp_ref, kc_ref, vp_ref, vc_ref, b_ref, do_ref, dp_in,
             dq_ref, dkb_ref, dvb_ref, db_ref, ds_ref):
        gp = pl.program_id(0)
        n = pl.program_id(1)

        @pl.when(n == 0)
        def _():
            db_ref[...] = jnp.zeros_like(db_ref)
            ds_ref[...] = jnp.zeros_like(ds_ref)

        valid = _band_valid(n)
        kks = _kv_bands(kp_ref, kc_ref)
        vvs = _kv_bands(vp_ref, vc_ref)
        lane = lax.broadcasted_iota(jnp.int32, (BLOCK, 128), 1)
        lane_b = lax.broadcasted_iota(jnp.int32, (2 * BLOCK, 128), 1)
        dk_acc = [jnp.zeros((2 * BLOCK, 128), F32), jnp.zeros((2 * BLOCK, 128), F32)]
        dv_acc = [jnp.zeros((2 * BLOCK, 128), F32), jnp.zeros((2 * BLOCK, 128), F32)]
        for j in range(8):
            kv = j // 4
            qs = q_ref[:, j * 128:(j + 1) * 128]
            dos = do_ref[:, j * 128:(j + 1) * 128]
            dqs = []
            for hh in range(2):
                hl = 2 * j + hh
                half = (lane < 64) if hh == 0 else (lane >= 64)
                qm = jnp.where(half, qs, jnp.zeros_like(qs))
                dom = jnp.where(half, dos, jnp.zeros_like(dos))
                p, ps = _attn_probs(qm, kks[kv], b_ref[hl], sk_ref[gp * 16 + hl], valid)
                dp = _dot_nt(dom, vvs[kv])
                delta = jnp.sum(p * dp, axis=-1, keepdims=True)
                dsc = p * (dp - delta)
                db_ref[hl] += dsc
                ds_ref[pl.ds(hl, 1), :] += jnp.zeros((1, 128), F32) - jnp.sum(ps * delta)
                dsb = (dsc * SCALE).astype(BF16)
                dqs.append(_dot(dsb, kks[kv]))
                dk_acc[kv] = dk_acc[kv] + _dot_tn(dsb, qm)
                dv_acc[kv] = dv_acc[kv] + _dot_tn(p.astype(BF16), dom)
            dq_ref[:, j * 128:(j + 1) * 128] = jnp.where(lane < 64, dqs[0], dqs[1]).astype(BF16)
        for acc, ref in ((dk_acc, dkb_ref), (dv_acc, dvb_ref)):
            d0 = acc[0] + pltpu.roll(acc[0], 64, 1)
            d1 = acc[1] + pltpu.roll(acc[1], 64, 1)
            ref[...] = jnp.where(lane_b < 64, d0, d1)

    kb, vb = OFF_K // 128, OFF_V // 128
    return pl.pallas_call(
        body, name="attn_bwd", grid=(2, nb),
        in_specs=[pl.BlockSpec(memory_space=pltpu.SMEM),
                  pl.BlockSpec((BLOCK, qw), lambda g, n: (n, OFF_Q // qw + g)),
                  pl.BlockSpec((BLOCK, 128), lambda g, n: (jnp.maximum(n - 1, 0), kb + g)),
                  pl.BlockSpec((BLOCK, 128), lambda g, n: (n, kb + g)),
                  pl.BlockSpec((BLOCK, 128), lambda g, n: (jnp.maximum(n - 1, 0), vb + g)),
                  pl.BlockSpec((BLOCK, 128), lambda g, n: (n, vb + g)),
                  pl.BlockSpec((16, BLOCK, 2 * BLOCK), lambda g, n: (g, 0, 0)),
                  pl.BlockSpec((BLOCK, qw), lambda g, n: (n, g)),
                  pl.BlockSpec(memory_space=pl.ANY)],
        out_specs=[pl.BlockSpec((BLOCK, qw), lambda g, n: (n, OFF_Q // qw + g)),
                   pl.BlockSpec((2 * BLOCK, 128), lambda g, n: (n, g)),
                   pl.BlockSpec((2 * BLOCK, 128), lambda g, n: (n, g)),
                   pl.BlockSpec((16, BLOCK, 2 * BLOCK), lambda g, n: (g, 0, 0)),
                   pl.BlockSpec((16, 128), lambda g, n: (g, 0))],
        out_shape=[SDS((s, IN_W), BF16), SDS((nb * 2 * BLOCK, 256), F32), SDS((nb * 2 * BLOCK, 256), F32),
                   SDS((N_HEADS, BLOCK, 2 * BLOCK), F32), SDS((N_HEADS, 128), F32)],
        input_output_aliases={8: 0},
        compiler_params=_cp(2))(sinks, proj, proj, proj, proj, proj, band, datt, dproj)


def dkv_combine(dkb, dvb, dproj):
    nb = dkb.shape[0] // (2 * BLOCK)
    s = nb * BLOCK
    dkb3 = dkb.reshape(nb, 2 * BLOCK, 256)
    dvb3 = dvb.reshape(nb, 2 * BLOCK, 256)

    def body(k1, k2, v1, v2, dp_in, o_ref):
        nxt = jnp.where(pl.program_id(0) < nb - 1, 1.0, 0.0)
        o_ref[:, 0:256] = (k1[...] + nxt * k2[...]).astype(BF16)
        o_ref[:, 256:512] = (v1[...] + nxt * v2[...]).astype(BF16)

    spec1 = pl.BlockSpec((None, BLOCK, 256), lambda m: (m, 1, 0))
    spec2 = pl.BlockSpec((None, BLOCK, 256), lambda m: (jnp.minimum(m + 1, nb - 1), 0, 0))
    return pl.pallas_call(
        body, name="dkv_combine", grid=(nb,),
        in_specs=[spec1, spec2, spec1, spec2, pl.BlockSpec(memory_space=pl.ANY)],
        out_specs=pl.BlockSpec((BLOCK, 512), lambda m: (m, OFF_K // 512)),
        out_shape=SDS((s, IN_W), BF16), input_output_aliases={4: 0},
        compiler_params=_cp(1))(dkb3, dkb3, dvb3, dvb3, dproj)


def lru_bwd(proj, rec, drec, lvec, wa, wx, dproj):
    s = proj.shape[0]
    t = min(256, s)
    nt = s // t

    def body(lx_ref, lxh_ref, rec_ref, rech_ref, dr_ref, lv_ref, wa_ref, wx_ref, dp_in,
             dlx_ref, sums_ref, dwa_ref, dwx_ref,
             xbuf, hbuf, dxbuf, a_s, dh_s, xc_s, r_s, ig_s, mu_s, gc):
        step_i = pl.program_id(0)
        ti = nt - 1 - step_i

        @pl.when(step_i == 0)
        def _():
            sums_ref[...] = jnp.zeros_like(sums_ref)
            dwa_ref[...] = jnp.zeros_like(dwa_ref)
            dwx_ref[...] = jnp.zeros_like(dwx_ref)
            dxbuf[pl.ds(t, 8), :] = jnp.zeros((8, D), F32)
            gc[...] = jnp.zeros((8, D), F32)

        live = jnp.where(ti > 0, 1.0, 0.0)
        xbuf[pl.ds(0, 8), :] = lxh_ref[...].astype(F32)[8:16] * live
        xbuf[pl.ds(8, t), :] = lx_ref[...].astype(F32)
        hbuf[pl.ds(0, 8), :] = rech_ref[...] * live
        hbuf[pl.ds(8, t), :] = rec_ref[...]
        first = (lax.broadcasted_iota(jnp.int32, (t, 128), 0) + ti * t) == 0
        for b in range(N_LRU_BLOCKS):
            cs = slice(b * 128, (b + 1) * 128)
            _, xc, _, r, ig, _, a, mult = _lru_block_fwd(xbuf, lv_ref, wa_ref, wx_ref, b, t, first)
            a_s[:, cs] = a
            xc_s[:, cs] = xc
            r_s[:, cs] = r
            ig_s[:, cs] = ig
            mu_s[:, cs] = mult

        def step(q, g):
            tt = t - 1 - q
            dh = dr_ref[pl.ds(tt, 1), :] + g
            dh_s[pl.ds(tt, 1), :] = dh
            return a_s[pl.ds(tt, 1), :] * dh

        gc[0:1, :] = lax.fori_loop(0, t, step, gc[0:1, :], unroll=8)
        for b in range(N_LRU_BLOCKS):
            cs = slice(b * 128, (b + 1) * 128)
            dh = dh_s[:, cs]
            a = a_s[:, cs]
            xc = xc_s[:, cs]
            r = r_s[:, cs]
            ig = ig_s[:, cs]
            mult = mu_s[:, cs]
            sp = _softplus(-lv_ref[L_LAM:L_LAM + 1, cs])
            lam = lv_ref[L_LAM:L_LAM + 1, cs]
            da = dh * hbuf[pl.ds(7, t), cs]
            dmult = jnp.where(first, 0.0, dh * ig * xc)
            dig = dh * mult * xc
            dxc = dh * mult * ig
            dlog_a = da * a - dmult * (a * a) / mult
            dr = dlog_a * ((-LRU_C) * sp)
            dsp = jnp.sum(dlog_a * ((-LRU_C) * r), axis=0, keepdims=True)
            dza = dr * r * (1.0 - r)
            dzx = dig * ig * (1.0 - ig)
            dzab = dza.astype(BF16)
            dzxb = dzx.astype(BF16)
            xcb = xc.astype(BF16)
            dwa_ref[b] += _dot_tn(xcb, dzab)
            dwx_ref[b] += _dot_tn(xcb, dzxb)
            dxc = dxc + _dot_nt(dzab, wa_ref[b]) + _dot_nt(dzxb, wx_ref[b])
            sums_ref[L_LAM:L_LAM + 1, cs] += dsp * (-jax.nn.sigmoid(-lam))
            sums_ref[L_BA:L_BA + 1, cs] += jnp.sum(dza, axis=0, keepdims=True)
            sums_ref[L_BX:L_BX + 1, cs] += jnp.sum(dzx, axis=0, keepdims=True)
            sums_ref[L_CB:L_CB + 1, cs] += jnp.sum(dxc, axis=0, keepdims=True)
            for kk in range(4):
                sums_ref[kk:kk + 1, cs] += jnp.sum(dxc * xbuf[pl.ds(5 + kk, t), cs], axis=0, keepdims=True)
            dxbuf[pl.ds(0, t), cs] = dxc
            dlx = (lv_ref[3:4, cs] * dxc + lv_ref[2:3, cs] * dxbuf[pl.ds(1, t), cs]
                   + lv_ref[1:2, cs] * dxbuf[pl.ds(2, t), cs] + lv_ref[0:1, cs] * dxbuf[pl.ds(3, t), cs])
            dlx_ref[:, cs] = dlx.astype(BF16)
        dxbuf[pl.ds(t, 8), :] = dxbuf[pl.ds(0, 8), :]

    rev = lambda i: nt - 1 - i
    return pl.pallas_call(
        body, name="lru_bwd", grid=(nt,),
        in_specs=[pl.BlockSpec((t, D), lambda i: (rev(i), 0)),
                  pl.BlockSpec((16, D), lambda i: (jnp.maximum(rev(i) * (t // 16) - 1, 0), 0)),
                  pl.BlockSpec((t, D), lambda i: (rev(i), 0)),
                  pl.BlockSpec((8, D), lambda i: (jnp.maximum(rev(i) * (t // 8) - 1, 0), 0)),
                  pl.BlockSpec((t, D), lambda i: (rev(i), 0)),
                  pl.BlockSpec((8, D), lambda i: (0, 0)),
                  pl.BlockSpec((N_LRU_BLOCKS, 128, 128), lambda i: (0, 0, 0)),
                  pl.BlockSpec((N_LRU_BLOCKS, 128, 128), lambda i: (0, 0, 0)),
                  pl.BlockSpec(memory_space=pl.ANY)],
        out_specs=[pl.BlockSpec((t, D), lambda i: (rev(i), 0)),
                   pl.BlockSpec((8, D), lambda i: (0, 0)),
                   pl.BlockSpec((N_LRU_BLOCKS, 128, 128), lambda i: (0, 0, 0)),
                   pl.BlockSpec((N_LRU_BLOCKS, 128, 128), lambda i: (0, 0, 0))],
        out_shape=[SDS((s, IN_W), BF16), SDS((8, D), F32), SDS((N_LRU_BLOCKS, 128, 128), F32),
                   SDS((N_LRU_BLOCKS, 128, 128), F32)],
        scratch_shapes=[pltpu.VMEM((t + 8, D), F32), pltpu.VMEM((t + 8, D), F32), pltpu.VMEM((t + 8, D), F32)]
        + [pltpu.VMEM((t, D), F32)] * 6 + [pltpu.VMEM((8, D), F32)],
        input_output_aliases={8: 0},
        compiler_params=_cp(1))(proj, proj, rec, rec, drec, lvec, wa, wx, dproj)


def inproj_bwd(dproj, w_in, x, dx1, vecs):
    s = x.shape[0]
    tm, tk = min(512, s), IN_TILE
    nk = IN_W // tk
    per = IN_SHARD // tk

    def body(d_ref, w_ref, x_hbm, dx1_hbm, v_ref, gx_ref, sums_ref, acc, x_ref, dx1_ref, sems):
        i, k = pl.program_id(0), pl.program_id(1)
        fetches = _row_fetches((x_hbm, dx1_hbm), (x_ref, dx1_ref), sems, i, tm)

        @pl.when((i == 0) & (k == 0))
        def _():
            sums_ref[...] = jnp.zeros_like(sums_ref)

        @pl.when(k == 0)
        def _():
            acc[...] = jnp.zeros_like(acc)
            for cp in fetches:
                cp.start()

        acc[...] += _dot_nt(d_ref[...], w_ref[...])

        @pl.when(k == nk - 1)
        def _():
            for cp in fetches:
                cp.wait()
            g1 = v_ref[V_G1:V_G1 + 1, :]
            scale1 = v_ref[V_SCALE1:V_SCALE1 + 1, :]

            def sub(rb, carry):
                rs = pl.ds(pl.multiple_of(rb * SUB, SUB), SUB)
                dh = acc[rs, :]
                r1, xh = _rms_parts(x_ref[rs, :])
                sums_ref[0:1, :] += jnp.sum(dh, axis=0, keepdims=True)
                sums_ref[1:2, :] += jnp.sum(dh * (xh * g1), axis=0, keepdims=True)
                dxn = dh * (1.0 + scale1)
                sums_ref[2:3, :] += jnp.sum(dxn * xh, axis=0, keepdims=True)
                dxh = dxn * g1
                gx_ref[rs, :] = dx1_ref[rs, :] + r1 * (dxh - xh * jnp.mean(dxh * xh, axis=-1, keepdims=True))
                return carry

            lax.fori_loop(0, tm // SUB, sub, 0)

    return pl.pallas_call(
        body, name="inproj_bwd", grid=(s // tm, nk),
        in_specs=[pl.BlockSpec((tm, tk), lambda i, k: (i, k)),
                  pl.BlockSpec((None, D, tk), lambda i, k: (k // per, 0, k % per)),
                  pl.BlockSpec(memory_space=pl.ANY), pl.BlockSpec(memory_space=pl.ANY),
                  pl.BlockSpec((16, D), lambda i, k: (0, 0))],
        out_specs=[pl.BlockSpec((tm, D), lambda i, k: (i, 0)), pl.BlockSpec((8, D), lambda i, k: (0, 0))],
        out_shape=[SDS((s, D), F32), SDS((8, D), F32)],
        scratch_shapes=[pltpu.VMEM((tm, D), F32), pltpu.VMEM((tm, D), F32), pltpu.VMEM((tm, D), F32),
                        pltpu.SemaphoreType.DMA((2,))],
        compiler_params=_cp(2))(dproj, w_in, x, dx1, vecs)


def mod_columns(c16, w_ada, b_cols):
    tn = 512

    def body(c_ref, w_ref, b_ref, o_ref):
        cv = c_ref[...]
        ca = (cv * jax.nn.sigmoid(cv)).astype(BF16)
        o_ref[...] = _dot(ca, w_ref[...].astype(BF16)) + b_ref[...]

    return pl.pallas_call(
        body, name="mod_columns", grid=(ADA_SHARD // tn,),
        in_specs=[pl.BlockSpec((16, D), lambda j: (0, 0)), pl.BlockSpec((D, tn), lambda j: (0, j)),
                  pl.BlockSpec((1, tn), lambda j: (0, j))],
        out_specs=pl.BlockSpec((16, tn), lambda j: (0, j)),
        out_shape=SDS((16, ADA_SHARD), F32), compiler_params=_cp(1))(c16, w_ada, b_cols)


def wada_update(c16, dmod16, w, m, v):
    tm, tn = 512, 512

    def body(c_ref, d_ref, w_ref, m_ref, v_ref, g_out, dl_out, m_out, v_out):
        cv = c_ref[...]
        ca = (cv * jax.nn.sigmoid(cv)).astype(BF16)
        g = _dot_tn(ca, d_ref[...].astype(BF16))
        dl, m2, v2 = _adamw_math(w_ref[...], g, m_ref[...], v_ref[...])
        g_out[...] = g
        dl_out[...] = dl
        m_out[...] = m2
        v_out[...] = v2

    tile = pl.BlockSpec((tm, tn), lambda i, j: (i, j))
    return pl.pallas_call(
        body, name="wada_update", grid=(D // tm, ADA_SHARD // tn),
        in_specs=[pl.BlockSpec((16, tm), lambda i, j: (0, i)), pl.BlockSpec((16, tn), lambda i, j: (0, j)),
                  tile, tile, tile],
        out_specs=[tile] * 4, out_shape=[SDS((D, ADA_SHARD), F32)] * 4,
        compiler_params=_cp(2))(c16, dmod16, w, m, v)


def adamw_big(name, w, mine, theirs, m, v, c_idx):
    r, c = w.shape
    tr = 128
    per = (r // 2) // tr

    def body(c_ref, w_ref, a_ref, b_ref, m_ref, v_ref, g_out, dl_out, m_out, v_out):
        own = (pl.program_id(0) // per) == c_ref[0]
        g = jnp.where(own, a_ref[...], b_ref[...])
        dl, m2, v2 = _adamw_math(w_ref[...], g, m_ref[...], v_ref[...])
        g_out[...] = g
        dl_out[...] = dl
        m_out[...] = m2
        v_out[...] = v2

    tile = pl.BlockSpec((tr, c), lambda i, cr: (i, 0))
    half = pl.BlockSpec((tr, c), lambda i, cr: (i % per, 0))
    gs = pltpu.PrefetchScalarGridSpec(num_scalar_prefetch=1, grid=(r // tr,),
                                      in_specs=[tile, half, half, tile, tile], out_specs=[tile] * 4)
    return pl.pallas_call(body, name=name, grid_spec=gs, out_shape=[SDS((r, c), F32)] * 4,
                          compiler_params=_cp(1))(c_idx, w, mine, theirs, m, v)


def cast_into_slot(name, w, k_idx):
    r, c = w.shape
    tr = 256

    def body(k_ref, w_ref, o_ref):
        o_ref[...] = w_ref[...].astype(BF16)

    gs = pltpu.PrefetchScalarGridSpec(
        num_scalar_prefetch=1, grid=(r // tr,),
        in_specs=[pl.BlockSpec((tr, c), lambda i, kr: (i, 0))],
        out_specs=pl.BlockSpec((None, tr, c), lambda i, kr: (kr[0], i, 0)))
    return pl.pallas_call(body, name=name, grid_spec=gs, out_shape=SDS((N_CHIPS, r, c), BF16),
                          compiler_params=_cp(1))(k_idx, w)


def adamw_small(ws, gs, ms, vs):
    n = len(ws)

    def body(*refs):
        for i in range(n):
            dl, m2, v2 = _adamw_math(refs[i][...], refs[n + i][...], refs[2 * n + i][...], refs[3 * n + i][...])
            refs[4 * n + i][...] = dl
            refs[5 * n + i][...] = m2
            refs[6 * n + i][...] = v2

    vm = pl.BlockSpec(memory_space=pltpu.VMEM)
    shapes = [SDS(w.shape, F32) for w in ws]
    outs = pl.pallas_call(
        body, name="adamw_small", in_specs=[vm] * (4 * n), out_specs=[vm] * (3 * n), out_shape=shapes * 3,
        compiler_params=pltpu.CompilerParams(vmem_limit_bytes=VMEM_LIMIT))(*ws, *gs, *ms, *vs)
    return outs[:n], outs[n:2 * n], outs[2 * n:]


def sum_devices(gathered):
    rows = gathered.shape[1]
    tr = 96

    def body(x_ref, o_ref):
        acc = x_ref[0]
        for d in range(1, N_DEV):
            acc = acc + x_ref[d]
        o_ref[...] = acc

    return pl.pallas_call(
        body, name="sum_devices", grid=(rows // tr,),
        in_specs=[pl.BlockSpec((N_DEV, tr, D), lambda i: (0, i, 0))],
        out_specs=pl.BlockSpec((tr, D), lambda i: (i, 0)),
        out_shape=SDS((rows, D), F32), compiler_params=_cp(1))(gathered)


def _mesh_pos():
    return lax.axis_index("x"), lax.axis_index("y"), lax.axis_index("c")


def _other_chips(x, y):
    return [(1 - x, y), (x, 1 - y), (1 - x, 1 - y)]


def all_gather_small(name, block):
    m_per, n = block.shape

    def body(x_ref, out_ref, send_sems, recv_sems, local_sem):
        x, y, c = _mesh_pos()
        me, sibling = (x, y, c), (x, y, 1 - c)
        chips = _other_chips(x, y)

        def rows(px, py, pc):
            return out_ref.at[pl.ds((4 * px + 2 * py + pc) * m_per, m_per), :]

        def copy(k, blk, to, src=None):
            return pltpu.make_async_remote_copy(
                src_ref=rows(*blk) if src is None else src, dst_ref=rows(*blk),
                send_sem=send_sems.at[k], recv_sem=recv_sems.at[k], device_id=to, device_id_type=MESH)

        mine = pltpu.make_async_copy(x_ref, rows(*me), local_sem)
        mine.start()
        first = [copy(0, me, sibling, src=x_ref)]
        first += [copy(1 + j, me, (*chip, c), src=x_ref) for j, chip in enumerate(chips)]
        for cp in first:
            cp.start()
        passed = [copy(4 + j, (*chip, c), sibling) for j, chip in enumerate(chips)]
        for j, chip in enumerate(chips):
            copy(1 + j, (*chip, c), me).wait_recv()
            passed[j].start()
        copy(0, sibling, me).wait_recv()
        for j, chip in enumerate(chips):
            copy(4 + j, (*chip, 1 - c), me).wait_recv()
        for cp in first + passed:
            cp.wait_send()
        mine.wait()

    vm = pl.BlockSpec(memory_space=pltpu.VMEM)
    return pl.pallas_call(
        body, name=name, out_shape=SDS((N_DEV * m_per, n), block.dtype), in_specs=[vm], out_specs=vm,
        scratch_shapes=[pltpu.SemaphoreType.DMA((7,)), pltpu.SemaphoreType.DMA((7,)), pltpu.SemaphoreType.DMA],
        compiler_params=pltpu.CompilerParams(vmem_limit_bytes=VMEM_LIMIT))(block)


def all_gather_weights(bufs):
    n = len(bufs)
    halves = [w.shape[1] // 2 for w in bufs]

    def body(*refs):
        outs = refs[n:2 * n]
        send_sems, recv_sems = refs[2 * n:]
        x, y, c = _mesh_pos()
        k = 2 * x + y
        me, sibling = (x, y, c), (x, y, 1 - c)
        chips = _other_chips(x, y)

        def region(i, chip_idx, half):
            return outs[i].at[chip_idx, pl.ds(half * halves[i], halves[i]), :]

        def copy(i, j, reg, to):
            return pltpu.make_async_remote_copy(src_ref=reg, dst_ref=reg, send_sem=send_sems.at[i, j],
                                                recv_sem=recv_sems.at[i, j], device_id=to, device_id_type=MESH)

        started = []
        for i in range(n):
            for j, chip in enumerate(chips):
                cp = copy(i, j, region(i, k, c), (*chip, c))
                cp.start()
                started.append(cp)
        for i in range(n):
            for j, chip in enumerate(chips):
                kj = 2 * chip[0] + chip[1]
                copy(i, j, region(i, kj, c), me).wait_recv()
                cp = copy(i, 3 + j, region(i, kj, c), sibling)
                cp.start()
                started.append(cp)
        for i in range(n):
            for j, chip in enumerate(chips):
                kj = 2 * chip[0] + chip[1]
                copy(i, 3 + j, region(i, kj, 1 - c), me).wait_recv()
        for cp in started:
            cp.wait_send()

    hbm = pl.BlockSpec(memory_space=pl.ANY)
    return pl.pallas_call(
        body, name="all_gather_weights", in_specs=[hbm] * n, out_specs=[hbm] * n,
        out_shape=[SDS(w.shape, w.dtype) for w in bufs],
        input_output_aliases={i: i for i in range(n)},
        scratch_shapes=[pltpu.SemaphoreType.DMA((n, 6)), pltpu.SemaphoreType.DMA((n, 6))])(*bufs)


def sibling_exchange(name, grads):
    n = len(grads)
    halves = [g.shape[1] // 2 for g in grads]

    def body(*refs):
        ins, outs = refs[:n], refs[n:2 * n]
        send_sems, recv_sems = refs[2 * n:]
        x, y, c = _mesh_pos()
        sibling = (x, y, 1 - c)
        cps = []
        for i in range(n):
            cp = pltpu.make_async_remote_copy(
                src_ref=ins[i].at[:, pl.ds((1 - c) * halves[i], halves[i]), :], dst_ref=outs[i],
                send_sem=send_sems.at[i], recv_sem=recv_sems.at[i], device_id=sibling, device_id_type=MESH)
            cp.start()
            cps.append(cp)
        for cp in cps:
            cp.wait_recv()
        for cp in cps:
            cp.wait_send()

    hbm = pl.BlockSpec(memory_space=pl.ANY)
    return pl.pallas_call(
        body, name=name, in_specs=[hbm] * n, out_specs=[hbm] * n,
        out_shape=[SDS((N_CHIPS, h, g.shape[2]), g.dtype) for g, h in zip(grads, halves)],
        scratch_shapes=[pltpu.SemaphoreType.DMA((n,)), pltpu.SemaphoreType.DMA((n,))])(*grads)


def sibling_sum(name, grad, other, c_idx):
    _, r, cc = grad.shape
    h = r // 2
    tr = min(256, h)
    g4 = grad.reshape(N_CHIPS, 2, h, cc)

    def body(c_ref, a_ref, b_ref, o_ref):
        o_ref[...] = (a_ref[...].astype(F32) + b_ref[...].astype(F32)).astype(BF16)

    gs = pltpu.PrefetchScalarGridSpec(
        num_scalar_prefetch=1, grid=(N_CHIPS, h // tr),
        in_specs=[pl.BlockSpec((None, None, tr, cc), lambda s, i, cr: (s, cr[0], i, 0)),
                  pl.BlockSpec((None, tr, cc), lambda s, i, cr: (s, i, 0))],
        out_specs=pl.BlockSpec((None, tr, cc), lambda s, i, cr: (s, i, 0)))
    return pl.pallas_call(body, name=name, grid_spec=gs, out_shape=SDS((N_CHIPS, h, cc), BF16),
                          compiler_params=_cp(2))(c_idx, g4, other)


HBM_SPEC = pl.BlockSpec(memory_space=pltpu.HBM)
SEM_SPEC = pl.BlockSpec(memory_space=pltpu.SEMAPHORE)


def _side_effecting():
    return pltpu.CompilerParams(has_side_effects=pltpu.SideEffectType.DATAFLOW_SIDE_EFFECTING)


def _in_hbm(a):
    return pltpu.with_memory_space_constraint(a, pltpu.HBM)


def gather_start(bufs, after):
    n = len(bufs)
    halves = [w.shape[1] // 2 for w in bufs]

    def body(*refs):
        ins = refs[:n]
        send_sems, recv_sems, token = refs[n + 1], refs[n + 2], refs[-1]
        x, y, c = _mesh_pos()
        k = 2 * x + y
        for i in range(n):
            reg = ins[i].at[k, pl.ds(c * halves[i], halves[i]), :]
            for j, chip in enumerate(_other_chips(x, y)):
                pltpu.make_async_remote_copy(src_ref=reg, dst_ref=reg, send_sem=send_sems.at[3 * i + j],
                                             recv_sem=recv_sems.at[3 * i + j], device_id=(*chip, c),
                                             device_id_type=MESH).start()
        token[...] = jnp.zeros_like(token)

    outs = pl.pallas_call(
        body, name="gather_start",
        out_shape=(pltpu.SemaphoreType.DMA((3 * n,)), pltpu.SemaphoreType.DMA((3 * n,)),
                   *[pltpu.HBM(w.shape, w.dtype) for w in bufs], SDS((8, 128), F32)),
        in_specs=[HBM_SPEC] * n + [pl.BlockSpec(memory_space=pl.ANY)],
        out_specs=(SEM_SPEC, SEM_SPEC, *[HBM_SPEC] * n, pl.BlockSpec(memory_space=pltpu.VMEM)),
        input_output_aliases={i: 2 + i for i in range(n)},
        compiler_params=_side_effecting())(*[_in_hbm(w) for w in bufs], after)
    return outs[0], outs[1], list(outs[2:2 + n]), outs[-1]


def gather_wait(send_sems, recv_sems, bufs, after):
    n = len(bufs)
    halves = [w.shape[1] // 2 for w in bufs]

    def body(*refs):
        ins = refs[:n]
        send_sems, recv_sems = refs[n], refs[n + 1]
        x, y, c = _mesh_pos()
        k = 2 * x + y
        for i in range(n):
            for j, chip in enumerate(_other_chips(x, y)):
                kj = 2 * chip[0] + chip[1]
                cp = pltpu.make_async_remote_copy(
                    src_ref=ins[i].at[k, pl.ds(c * halves[i], halves[i]), :],
                    dst_ref=ins[i].at[kj, pl.ds(c * halves[i], halves[i]), :],
                    send_sem=send_sems.at[3 * i + j], recv_sem=recv_sems.at[3 * i + j], device_id=(*chip, c),
                    device_id_type=MESH)
                cp.wait_send()
                cp.wait_recv()

    return pl.pallas_call(
        body, name="gather_wait", out_shape=[pltpu.HBM(w.shape, w.dtype) for w in bufs],
        in_specs=[HBM_SPEC] * n + [SEM_SPEC, SEM_SPEC, pl.BlockSpec(memory_space=pl.ANY)],
        out_specs=[HBM_SPEC] * n, input_output_aliases={i: i for i in range(n)},
        compiler_params=_side_effecting())(*bufs, send_sems, recv_sems, after)


def gather_forward(bufs):
    n = len(bufs)
    halves = [w.shape[1] // 2 for w in bufs]

    def body(*refs):
        outs = refs[n:2 * n]
        send_sems, recv_sems = refs[2 * n:]
        x, y, c = _mesh_pos()
        chips = _other_chips(x, y)

        def copy(i, j, half, to):
            kj = 2 * chips[j][0] + chips[j][1]
            reg = outs[i].at[kj, pl.ds(half * halves[i], halves[i]), :]
            return pltpu.make_async_remote_copy(src_ref=reg, dst_ref=reg, send_sem=send_sems.at[i, j],
                                                recv_sem=recv_sems.at[i, j], device_id=to, device_id_type=MESH)

        cps = [copy(i, j, c, (x, y, 1 - c)) for i in range(n) for j in range(3)]
        for cp in cps:
            cp.start()
        for i in range(n):
            for j in range(3):
                copy(i, j, 1 - c, (x, y, c)).wait_recv()
        for cp in cps:
            cp.wait_send()

    hbm = pl.BlockSpec(memory_space=pl.ANY)
    return pl.pallas_call(
        body, name="gather_forward", in_specs=[hbm] * n, out_specs=[hbm] * n,
        out_shape=[SDS(w.shape, w.dtype) for w in bufs], input_output_aliases={i: i for i in range(n)},
        scratch_shapes=[pltpu.SemaphoreType.DMA((n, 3)), pltpu.SemaphoreType.DMA((n, 3))])(*bufs)


def exchange_start(name, sums):
    n = len(sums)
    lands = [lax.empty((3,) + t.shape[1:], t.dtype) for t in sums]

    def body(*refs):
        ins, zones = refs[:n], refs[n:2 * n]
        send_sems, recv_sems, token = refs[2 * n], refs[2 * n + 1], refs[-1]
        x, y, c = _mesh_pos()
        for i in range(n):
            for j, chip in enumerate(_other_chips(x, y)):
                kj = 2 * chip[0] + chip[1]
                pltpu.make_async_remote_copy(src_ref=ins[i].at[kj], dst_ref=zones[i].at[j],
                                             send_sem=send_sems.at[3 * i + j], recv_sem=recv_sems.at[3 * i + j],
                                             device_id=(*chip, c), device_id_type=MESH).start()
        token[...] = jnp.zeros_like(token)

    outs = pl.pallas_call(
        body, name=name,
        out_shape=(pltpu.SemaphoreType.DMA((3 * n,)), pltpu.SemaphoreType.DMA((3 * n,)),
                   *[pltpu.HBM(t.shape, t.dtype) for t in sums], *[pltpu.HBM(t.shape, t.dtype) for t in lands],
                   SDS((8, 128), F32)),
        in_specs=[HBM_SPEC] * (2 * n),
        out_specs=(SEM_SPEC, SEM_SPEC, *[HBM_SPEC] * (2 * n), pl.BlockSpec(memory_space=pltpu.VMEM)),
        input_output_aliases={i: 2 + i for i in range(2 * n)},
        compiler_params=_side_effecting())(*[_in_hbm(t) for t in sums], *[_in_hbm(t) for t in lands])
    return outs[0], outs[1], list(outs[2:2 + n]), list(outs[2 + n:2 + 2 * n]), outs[-1]


def exchange_wait(name, send_sems, recv_sems, sums, lands, after):
    n = len(sums)

    def body(*refs):
        ins, zones = refs[:n], refs[n:2 * n]
        send_sems, recv_sems = refs[2 * n], refs[2 * n + 1]
        x, y, c = _mesh_pos()
        for i in range(n):
            for j, chip in enumerate(_other_chips(x, y)):
                kj = 2 * chip[0] + chip[1]
                cp = pltpu.make_async_remote_copy(src_ref=ins[i].at[kj], dst_ref=zones[i].at[j],
                                                  send_sem=send_sems.at[3 * i + j], recv_sem=recv_sems.at[3 * i + j],
                                                  device_id=(*chip, c), device_id_type=MESH)
                cp.wait_send()
                cp.wait_recv()

    outs = pl.pallas_call(
        body, name=name, out_shape=[pltpu.HBM(t.shape, t.dtype) for t in sums + lands],
        in_specs=[HBM_SPEC] * (2 * n) + [SEM_SPEC, SEM_SPEC, pl.BlockSpec(memory_space=pl.ANY)],
        out_specs=[HBM_SPEC] * (2 * n), input_output_aliases={i: i for i in range(2 * n)},
        compiler_params=_side_effecting())(*sums, *lands, send_sems, recv_sems, after)
    return list(outs[:n]), list(outs[n:])


def chip_sum(name, sums, parts, k_idx):
    _, h, cc = parts.shape
    tr = min(256, h)

    def body(k_ref, own_ref, p_ref, o_ref):
        acc = own_ref[...].astype(F32)
        for s in range(3):
            acc = acc + p_ref[s].astype(F32)
        o_ref[...] = acc

    gs = pltpu.PrefetchScalarGridSpec(
        num_scalar_prefetch=1, grid=(h // tr,),
        in_specs=[pl.BlockSpec((None, tr, cc), lambda i, kr: (kr[0], i, 0)),
                  pl.BlockSpec((3, tr, cc), lambda i, kr: (0, i, 0))],
        out_specs=pl.BlockSpec((tr, cc), lambda i, kr: (i, 0)))
    return pl.pallas_call(body, name=name, grid_spec=gs, out_shape=SDS((h, cc), F32),
                          compiler_params=_cp(1))(k_idx, sums, parts)


def halves_exchange(name, halves):
    n = len(halves)

    def body(*refs):
        ins, outs = refs[:n], refs[n:2 * n]
        send_sems, recv_sems = refs[2 * n:]
        x, y, c = _mesh_pos()
        cps = []
        for i in range(n):
            cp = pltpu.make_async_remote_copy(
                src_ref=ins[i], dst_ref=outs[i], send_sem=send_sems.at[i], recv_sem=recv_sems.at[i],
                device_id=(x, y, 1 - c), device_id_type=MESH)
            cp.start()
            cps.append(cp)
        for cp in cps:
            cp.wait_recv()
        for cp in cps:
            cp.wait_send()

    hbm = pl.BlockSpec(memory_space=pl.ANY)
    return pl.pallas_call(
        body, name=name, in_specs=[hbm] * n, out_specs=[hbm] * n,
        out_shape=[SDS(t.shape, F32) for t in halves],
        scratch_shapes=[pltpu.SemaphoreType.DMA((n,)), pltpu.SemaphoreType.DMA((n,))])(*halves)


def local_step(x, tgt, vecs, lvec, wa, wx, sinks, rel_bias, w_in, rest_weights, grads_out, grad_in_out):
    buckets = t5_bucket_table()
    band = bias_band(rel_bias.T, buckets).reshape(N_HEADS, BLOCK, 2 * BLOCK)

    proj, h = inproj_fwd(x, vecs, w_in)
    ya, rec = lru_fwd(proj, lvec, wa, wx)
    att = attn_fwd(proj, band, sinks)
    w_lru_out, w_attn_out, w_out, w_ff1, w_ff2 = rest_weights(att[:8, :128] + ya[:8, :128])
    w_lru_out2, w_attn_out2, w_out2 = w_lru_out.reshape(D, D), w_attn_out.reshape(D, D), w_out.reshape(D, D)
    w_ff2_2 = w_ff2.reshape(D_FF, D)
    yab, merged = merge_fwd(ya, att, w_lru_out2, w_attn_out2, proj)
    x1, o1 = outproj_fwd(merged, w_out2, x, vecs)
    f, h2 = ff1_fwd(x1, vecs, w_ff1)
    dx2, do2, sums_f, loss = ff2_loss(f, w_ff2_2, x1, tgt, vecs)

    df = ff2_bwd(do2, w_ff2_2, f)
    g_ff2 = weight_grad("dw_ff2", f, do2, 512, (D_FF, D), (WG_TM, 512), lambda i, j: (i, j), relu2=True)
    dx1, do1, sums_2 = ff1_bwd(df, w_ff1, x1, dx2, o1, vecs)
    g_ff1 = weight_grad("dw_ff1", h2, df, 512, (N_CHIPS, D, D), (None, WG_TM, 512), lambda i, j: (j // 4, i, j % 4))
    dyab, dproj = outproj_bwd(do1, w_out2, yab, proj)
    g_out = weight_grad("dw_out", merged, do1, 512, (D, D), (WG_TM, 512), lambda i, j: (i, j))
    drec, dproj = lruout_bwd(dyab, w_lru_out2, rec, proj, dproj)
    g_lru_out = weight_grad("dw_lru_out", ya, dyab[0], 512, (D, D), (WG_TM, 512), lambda i, j: (i, j))
    datt = attnout_bwd(dyab, w_attn_out2)
    g_attn_out = weight_grad("dw_attn_out", att, dyab[1], 512, (D, D), (WG_TM, 512), lambda i, j: (i, j))
    zero = grads_out([g_lru_out.reshape(N_CHIPS, D // 4, D), g_attn_out.reshape(N_CHIPS, D // 4, D),
                      g_out.reshape(N_CHIPS, D // 4, D), g_ff1, g_ff2.reshape(N_CHIPS, D_FF // 4, D)])
    dproj, dkb, dvb, dband, dsink = attn_bwd(proj, band, sinks + zero, datt, dproj)
    dproj = dkv_combine(dkb, dvb, dproj)
    dproj, sums_l, d_wa, d_wx = lru_bwd(proj, rec, drec, lvec, wa, wx, dproj)
    per = IN_SHARD // IN_TILE
    g_in = weight_grad("dw_in", h, dproj, IN_TILE, (N_CHIPS, D, IN_SHARD), (None, WG_TM, IN_TILE),
                       lambda i, j: (j // per, i, j % per))
    zero = grad_in_out(g_in)
    grad_x, sums_1 = inproj_bwd(dproj, w_in, x, dx1, vecs + zero)
    d_rel_bias = bias_band_bwd(dband.reshape(N_HEADS, BLOCK * 2 * BLOCK), buckets)

    small = dict(sums_f=sums_f, sums_2=sums_2, sums_1=sums_1, sums_l=sums_l, d_wa=d_wa, d_wx=d_wx,
                 d_sinks=dsink[:, 0], d_rel_bias=d_rel_bias)
    return loss, grad_x, small


def _pad_rows(a, rows):
    return jnp.concatenate([a, jnp.zeros((rows - a.shape[0], a.shape[1]), a.dtype)], axis=0)


def kernel(x, c, w_ada, b_ada, norm1_g, w_in, conv_w, conv_b, lru_wa, lru_ba, lru_wx, lru_bx, lru_lambda, w_lru_out, w_attn_out, attn_sinks, rel_bias, w_out, norm2_g, w_ff1, w_ff2, final_g, loss_target, m_w_ada, m_b_ada, m_norm1_g, m_w_in, m_conv_w, m_conv_b, m_lru_wa, m_lru_ba, m_lru_wx, m_lru_bx, m_lru_lambda, m_w_lru_out, m_w_attn_out, m_attn_sinks, m_rel_bias, m_w_out, m_norm2_g, m_w_ff1, m_w_ff2, m_final_g, v_w_ada, v_b_ada, v_norm1_g, v_w_in, v_conv_w, v_conv_b, v_lru_wa, v_lru_ba, v_lru_wx, v_lru_bx, v_lru_lambda, v_w_lru_out, v_w_attn_out, v_attn_sinks, v_rel_bias, v_w_out, v_norm2_g, v_w_ff1, v_w_ff2, v_final_g):
    xi, yi, ci = _mesh_pos()
    chip = 2 * xi + yi
    dev = 2 * chip + ci
    z8 = jnp.zeros((8, D), F32)

    conv_rows = jnp.concatenate([conv_w[0], jnp.zeros((4, D - D // 4), F32)], axis=1)
    pack0 = jnp.concatenate([c, conv_rows, jnp.zeros((3, D), F32)], axis=0)
    g0 = all_gather_small("gather_cond", pack0).reshape(N_DEV, 8, D)
    c_all = g0[:, 0, :]
    conv_full = jnp.concatenate([g0[2 * k, 1:5, :D // 4] for k in range(N_CHIPS)], axis=1)
    c16 = jnp.concatenate([c_all, z8], axis=0)
    b_cols = lax.dynamic_slice_in_dim(b_ada, chip * ADA_SHARD, ADA_SHARD, axis=1)
    mod_c = mod_columns(c16, w_ada[0], b_cols)
    g1 = all_gather_small("gather_mod", mod_c).reshape(N_DEV, 16, ADA_SHARD)
    mod = jnp.concatenate([lax.dynamic_index_in_dim(g1[2 * k], dev, axis=0, keepdims=False) for k in range(N_CHIPS)])
    shift1, scale1, gate1, shift2, scale2, gate2 = [mod[i * D:(i + 1) * D] for i in range(6)]
    vecs = jnp.stack([norm1_g[0], scale1, shift1, gate1, norm2_g[0], scale2, shift2, gate2, final_g]
                     + [jnp.zeros((D,), F32)] * 7)
    lvec = jnp.concatenate([conv_full, conv_b, lru_ba, lru_bx, lru_lambda], axis=0)

    shards = [w_in[0], w_lru_out[0], w_attn_out[0], w_out[0], w_ff1[0], w_ff2[0]]
    names = ["w_in", "w_lru_out", "w_attn_out", "w_out", "w_ff1", "w_ff2"]
    k_idx = jnp.reshape(chip, (1,)).astype(jnp.int32)
    c_idx = jnp.reshape(ci, (1,)).astype(jnp.int32)
    slots = [cast_into_slot("cast_" + nm, w, k_idx) for nm, w in zip(names, shards)]
    w_in_full = all_gather_weights(slots[:1])[0]
    g_send, g_recv, in_flight, token = gather_start(slots[1:], w_in_full)
    vecs = vecs + token[0, 0]
    pending = {}

    def rest_weights(after):
        return gather_forward(gather_wait(g_send, g_recv, in_flight, after))

    def start_reduce(tag, nms, grads):
        from_sibling = sibling_exchange("sibling_exchange_" + tag, grads)
        sums = [sibling_sum("sibling_sum_" + nm, g, o, c_idx) for nm, g, o in zip(nms, grads, from_sibling)]
        pending[tag] = exchange_start("exchange_start_" + tag, sums)
        return pending[tag][-1][0, 0]

    loss_t, grad_x, small = local_step(
        x[0], loss_target[0], vecs, lvec, lru_wa[0].astype(BF16), lru_wx[0].astype(BF16),
        attn_sinks[0], rel_bias, w_in_full, rest_weights,
        lambda grads: start_reduce("a", names[1:], grads), lambda g: start_reduce("b", names[:1], [g]))
    loss = lax.psum(loss_t[0, 0], ("x", "y", "c"))

    sums_f, sums_2, sums_1, sums_l = small["sums_f"], small["sums_2"], small["sums_1"], small["sums_l"]
    vec_rows = jnp.stack([sums_1[2], sums_2[2], sums_f[0], sums_l[L_CB], sums_l[L_BA], sums_l[L_BX],
                          sums_l[L_LAM], jnp.zeros((D,), F32)])
    mod_rows = jnp.stack([sums_1[0], sums_1[1], sums_2[3], sums_2[0], sums_2[1], sums_f[1],
                          jnp.zeros((D,), F32), jnp.zeros((D,), F32)])
    att_rows = jnp.concatenate([
        jnp.concatenate([small["d_sinks"], jnp.zeros((D - N_HEADS,), F32)])[None],
        jnp.concatenate([small["d_rel_bias"].reshape(-1), jnp.zeros((D - N_BUCKETS * N_HEADS,), F32)])[None],
        jnp.zeros((6, D), F32)], axis=0)
    pack = jnp.concatenate([vec_rows, _pad_rows(sums_l[0:4], 8), mod_rows, att_rows,
                            small["d_wa"].reshape(128, D), small["d_wx"].reshape(128, D)], axis=0)
    gathered = all_gather_small("gather_small_grads", pack).reshape(N_DEV, P_ROWS, D)
    total = sum_devices(gathered)
    dmod_all = gathered[:, P_MOD:P_MOD + 6, :].reshape(N_DEV, 6 * D)
    dmod16 = jnp.concatenate([lax.dynamic_slice_in_dim(dmod_all, chip * ADA_SHARD, ADA_SHARD, axis=1),
                              jnp.zeros((8, ADA_SHARD), F32)], axis=0)
    g_w_ada, d_w_ada, nm_w_ada, nv_w_ada = wada_update(c16, dmod16, w_ada[0], m_w_ada[0], v_w_ada[0])

    mine, theirs = {}, {}
    for tag, nms in (("a", names[1:]), ("b", names[:1])):
        send_sems, recv_sems, sums, lands, _ = pending[tag]
        sums, lands = exchange_wait("exchange_wait_" + tag, send_sems, recv_sems, sums, lands, grad_x)
        halves = [chip_sum("chip_sum_" + nm, t, p, k_idx) for nm, t, p in zip(nms, sums, lands)]
        for nm, a, b in zip(nms, halves, halves_exchange("halves_exchange_" + tag, halves)):
            mine[nm], theirs[nm] = a, b
    big_m = [m_w_in, m_w_lru_out, m_w_attn_out, m_w_out, m_w_ff1, m_w_ff2]
    big_v = [v_w_in, v_w_lru_out, v_w_attn_out, v_w_out, v_w_ff1, v_w_ff2]
    g_big, d_big, nm_big, nv_big = {}, {}, {}, {}
    for nm, w, m, v in zip(names, shards, big_m, big_v):
        g2, dl, m2, v2 = adamw_big("adamw_" + nm, w, mine[nm], theirs[nm], m[0], v[0], c_idx)
        g_big[nm], d_big[nm], nm_big[nm], nv_big[nm] = g2[None], dl[None], m2[None], v2[None]

    conv_g = lax.dynamic_slice_in_dim(total[P_CONVW:P_CONVW + 4], chip * (D // 4), D // 4, axis=1)
    sm_names = ["b_ada", "norm1_g", "conv_w", "conv_b", "lru_wa", "lru_ba", "lru_wx", "lru_bx", "lru_lambda",
                "attn_sinks", "rel_bias", "norm2_g", "final_g"]
    sm_w = [b_ada.reshape(6, D), norm1_g, conv_w[0], conv_b, lru_wa.reshape(128, D), lru_ba, lru_wx.reshape(128, D),
            lru_bx, lru_lambda, attn_sinks, rel_bias, norm2_g, final_g[None]]
    sm_m = [m_b_ada.reshape(6, D), m_norm1_g, m_conv_w[0], m_conv_b, m_lru_wa.reshape(128, D), m_lru_ba,
            m_lru_wx.reshape(128, D), m_lru_bx, m_lru_lambda, m_attn_sinks, m_rel_bias, m_norm2_g, m_final_g[None]]
    sm_v = [v_b_ada.reshape(6, D), v_norm1_g, v_conv_w[0], v_conv_b, v_lru_wa.reshape(128, D), v_lru_ba,
            v_lru_wx.reshape(128, D), v_lru_bx, v_lru_lambda, v_attn_sinks, v_rel_bias, v_norm2_g, v_final_g[None]]
    sm_g = [total[P_MOD:P_MOD + 6], total[0:1], conv_g, total[3:4], total[P_WA:P_WA + 128], total[4:5],
            total[P_WX:P_WX + 128], total[5:6], total[6:7], total[P_ATT:P_ATT + 1, :N_HEADS],
            total[P_ATT + 1, :N_BUCKETS * N_HEADS].reshape(N_BUCKETS, N_HEADS), total[1:2], total[2:3]]
    sm_d, sm_nm, sm_nv = adamw_small(sm_w, sm_g, sm_m, sm_v)
    shapes = dict(b_ada=b_ada.shape, norm1_g=norm1_g.shape, conv_w=conv_w.shape, conv_b=conv_b.shape,
                  lru_wa=lru_wa.shape, lru_ba=lru_ba.shape, lru_wx=lru_wx.shape, lru_bx=lru_bx.shape,
                  lru_lambda=lru_lambda.shape, attn_sinks=attn_sinks.shape, rel_bias=rel_bias.shape,
                  norm2_g=norm2_g.shape, final_g=final_g.shape)
    grads = dict(w_ada=g_w_ada[None], **g_big)
    deltas = dict(w_ada=d_w_ada[None], **d_big)
    new_m = dict(w_ada=nm_w_ada[None], **nm_big)
    new_v = dict(w_ada=nv_w_ada[None], **nv_big)
    for i, nm in enumerate(sm_names):
        grads[nm] = sm_g[i].reshape(shapes[nm])
        deltas[nm] = sm_d[i].reshape(shapes[nm])
        new_m[nm] = sm_nm[i].reshape(shapes[nm])
        new_v[nm] = sm_nv[i].reshape(shapes[nm])
    order = ["w_ada", "b_ada", "norm1_g", "w_in", "conv_w", "conv_b", "lru_wa", "lru_ba", "lru_wx", "lru_bx",
             "lru_lambda", "w_lru_out", "w_attn_out", "attn_sinks", "rel_bias", "w_out", "norm2_g", "w_ff1", "w_ff2",
             "final_g"]
    return (loss, grad_x[None], *[grads[n] for n in order], *[deltas[n] for n in order],
            *[new_m[n] for n in order], *[new_v[n] for n in order])
```

```python
import math

import numpy as np
import jax
import jax.numpy as jnp
from jax import lax
from jax.experimental import pallas as pl
from jax.experimental.pallas import tpu as pltpu

F32 = jnp.float32
BF16 = jnp.bfloat16
SDS = jax.ShapeDtypeStruct
MESH = pl.DeviceIdType.MESH

D = 2048
D_FF = 4 * D
N_HEADS = 32
HEAD_DIM = 64
BLOCK = 128
N_LRU_BLOCKS = 16
LRU_C = 8.0
EPS = 1e-6
NEG_INF = -1e30
N_BUCKETS = 32
MAX_DISTANCE = 128
IN_W = 10752
IN_SHARD = IN_W // 4
IN_TILE = 896
ADA_SHARD = 6 * D // 4
OFF_LRU, OFF_GATE, OFF_Q, OFF_K, OFF_V, OFF_GA, OFF_GB = 0, 2048, 4096, 6144, 6400, 6656, 8704
SCALE = HEAD_DIM ** -0.5
N_CHIPS = 4
N_DEV = 8

ADAM_LR, ADAM_B1, ADAM_B2, ADAM_EPS, ADAM_WD, ADAM_STEP = 0.001, 0.9, 0.999, 1e-08, 0.01, 10
ADAM_C1 = 1.0 - ADAM_B1 ** ADAM_STEP
ADAM_C2 = 1.0 - ADAM_B2 ** ADAM_STEP

VMEM_LIMIT = 52 * 2 ** 20
SUB = 128
WG_TM = 1024
V_G1, V_SCALE1, V_SHIFT1, V_GATE1, V_G2, V_SCALE2, V_SHIFT2, V_GATE2, V_G3 = range(9)
L_CW0, L_CB, L_BA, L_BX, L_LAM = 0, 4, 5, 6, 7
P_VEC, P_CONVW, P_MOD, P_ATT, P_WA, P_WX, P_ROWS = 0, 8, 16, 24, 32, 160, 288


def _cp(n_axes):
    return pltpu.CompilerParams(dimension_semantics=("arbitrary",) * n_axes, vmem_limit_bytes=VMEM_LIMIT)


def _dot(a, b):
    return jnp.dot(a, b, preferred_element_type=F32)


def _dot_nt(a, b):
    return lax.dot_general(a, b, (((1,), (1,)), ((), ())), preferred_element_type=F32)


def _dot_tn(a, b):
    return lax.dot_general(a, b, (((0,), (0,)), ((), ())), preferred_element_type=F32)


_G0 = math.sqrt(2.0 / math.pi)
_G1 = 0.044715


def _gelu(x):
    return 0.5 * x * (1.0 + jnp.tanh(_G0 * (x + _G1 * x * x * x)))


def _gelu_grad(x):
    x2 = x * x
    t = jnp.tanh(_G0 * (x + _G1 * x * x2))
    return 0.5 * (1.0 + t) + 0.5 * x * (1.0 - t * t) * _G0 * (1.0 + 3.0 * _G1 * x2)


def _sigmoid(x):
    return 0.5 * jnp.tanh(0.5 * x) + 0.5


def _one_minus_exp2(x):
    t = jnp.tanh(x)
    return (-2.0 * t) / (1.0 - t)


def _softplus(z):
    e = jnp.exp(-jnp.abs(z))
    u = 1.0 + e
    l1p = jnp.where(u == 1.0, e, jnp.log(u) * e / (u - 1.0))
    return jnp.maximum(z, 0.0) + l1p


def _adamw_math(w, g, m, v):
    m2 = ADAM_B1 * m + (1.0 - ADAM_B1) * g
    v2 = ADAM_B2 * v + (1.0 - ADAM_B2) * (g * g)
    m_hat = m2 / ADAM_C1
    v_hat = v2 / ADAM_C2
    delta = -ADAM_LR * (m_hat / (jnp.sqrt(v_hat) + ADAM_EPS) + ADAM_WD * w)
    return delta, m2, v2


def _rms_parts(xv):
    r = lax.rsqrt(jnp.mean(xv * xv, axis=-1, keepdims=True) + EPS)
    return r, xv * r


def _row_fetches(hbm_refs, bufs, sems, i, rows):
    return [pltpu.make_async_copy(h.at[pl.ds(i * rows, rows), :], b, sems.at[n])
            for n, (h, b) in enumerate(zip(hbm_refs, bufs))]


def _modulated_norm(x_ref, v_ref, row_g, row_scale, row_shift, h_ref, rows):
    g, scale, shift = v_ref[row_g:row_g + 1, :], v_ref[row_scale:row_scale + 1, :], v_ref[row_shift:row_shift + 1, :]

    def sub(rb, carry):
        rs = pl.ds(pl.multiple_of(rb * SUB, SUB), SUB)
        _, xh = _rms_parts(x_ref[rs, :])
        h_ref[rs, :] = ((xh * g) * (1.0 + scale) + shift).astype(BF16)
        return carry

    lax.fori_loop(0, rows // SUB, sub, 0)


def inproj_fwd(x, vecs, w_in):
    s = x.shape[0]
    tm = min(1024, s)
    per = IN_SHARD // IN_TILE

    def body(x_ref, v_ref, w_ref, proj_ref, h_ref):
        @pl.when(pl.program_id(1) == 0)
        def _():
            _modulated_norm(x_ref, v_ref, V_G1, V_SCALE1, V_SHIFT1, h_ref, tm)
        proj_ref[...] = _dot(h_ref[...], w_ref[...]).astype(BF16)

    return pl.pallas_call(
        body, name="inproj_fwd", grid=(s // tm, IN_W // IN_TILE),
        in_specs=[pl.BlockSpec((tm, D), lambda i, j: (i, 0)),
                  pl.BlockSpec((16, D), lambda i, j: (0, 0)),
                  pl.BlockSpec((None, D, IN_TILE), lambda i, j: (j // per, 0, j % per))],
        out_specs=[pl.BlockSpec((tm, IN_TILE), lambda i, j: (i, j)),
                   pl.BlockSpec((tm, D), lambda i, j: (i, 0))],
        out_shape=[SDS((s, IN_W), BF16), SDS((s, D), BF16)],
        compiler_params=_cp(2))(x, vecs, w_in)


def _lru_block_fwd(xbuf, lv_ref, wa_ref, wx_ref, b, t, first):
    cs = slice(b * 128, (b + 1) * 128)
    x0 = xbuf[pl.ds(8, t), cs]
    x1 = xbuf[pl.ds(7, t), cs]
    x2 = xbuf[pl.ds(6, t), cs]
    x3 = xbuf[pl.ds(5, t), cs]
    xc = (lv_ref[L_CB:L_CB + 1, cs] + lv_ref[3:4, cs] * x0 + lv_ref[2:3, cs] * x1
          + lv_ref[1:2, cs] * x2 + lv_ref[0:1, cs] * x3)
    xcb = xc.astype(BF16)
    r = _sigmoid(_dot(xcb, wa_ref[b]) + lv_ref[L_BA:L_BA + 1, cs])
    ig = _sigmoid(_dot(xcb, wx_ref[b]) + lv_ref[L_BX:L_BX + 1, cs])
    sp = _softplus(-lv_ref[L_LAM:L_LAM + 1, cs])
    log_a = (-LRU_C) * r * sp
    a = jnp.exp(log_a)
    mult = jnp.where(first, 1.0, jnp.sqrt(_one_minus_exp2(log_a)))
    return (x0, x1, x2, x3), xc, xcb, r, ig, sp, a, mult


def lru_fwd(proj, lvec, wa, wx):
    s = proj.shape[0]
    t = min(256, s)

    def body(lx_ref, gate_ref, lv_ref, wa_ref, wx_ref, ya_ref, rec_ref, xbuf, a_s, u_s, hc):
        i = pl.program_id(0)

        @pl.when(i == 0)
        def _():
            xbuf[pl.ds(0, 8), :] = jnp.zeros((8, D), F32)
            hc[...] = jnp.zeros((8, D), F32)

        @pl.when(i > 0)
        def _():
            xbuf[pl.ds(0, 8), :] = xbuf[pl.ds(t, 8), :]

        xbuf[pl.ds(8, t), :] = lx_ref[...].astype(F32)
        first = (lax.broadcasted_iota(jnp.int32, (t, 128), 0) + i * t) == 0
        for b in range(N_LRU_BLOCKS):
            cs = slice(b * 128, (b + 1) * 128)
            _, xc, _, _, ig, _, a, mult = _lru_block_fwd(xbuf, lv_ref, wa_ref, wx_ref, b, t, first)
            a_s[:, cs] = a
            u_s[:, cs] = mult * (ig * xc)

        def step(tt, h):
            h = a_s[pl.ds(tt, 1), :] * h + u_s[pl.ds(tt, 1), :]
            rec_ref[pl.ds(tt, 1), :] = h
            return h

        hc[0:1, :] = lax.fori_loop(0, t, step, hc[0:1, :], unroll=8)
        for b in range(N_LRU_BLOCKS):
            cs = slice(b * 128, (b + 1) * 128)
            ya_ref[:, cs] = (rec_ref[:, cs] * _gelu(gate_ref[:, cs].astype(F32))).astype(BF16)

    return pl.pallas_call(
        body, name="lru_fwd", grid=(s // t,),
        in_specs=[pl.BlockSpec((t, D), lambda i: (i, OFF_LRU // D)),
                  pl.BlockSpec((t, D), lambda i: (i, OFF_GATE // D)),
                  pl.BlockSpec((8, D), lambda i: (0, 0)),
                  pl.BlockSpec((N_LRU_BLOCKS, 128, 128), lambda i: (0, 0, 0)),
                  pl.BlockSpec((N_LRU_BLOCKS, 128, 128), lambda i: (0, 0, 0))],
        out_specs=[pl.BlockSpec((t, D), lambda i: (i, 0)), pl.BlockSpec((t, D), lambda i: (i, 0))],
        out_shape=[SDS((s, D), BF16), SDS((s, D), F32)],
        scratch_shapes=[pltpu.VMEM((t + 8, D), F32), pltpu.VMEM((t, D), F32), pltpu.VMEM((t, D), F32),
                        pltpu.VMEM((8, D), F32)],
        compiler_params=_cp(1))(proj, proj, lvec, wa, wx)


def t5_bucket_table():
    qi = np.arange(BLOCK)[:, None]
    ki = np.arange(2 * BLOCK)[None, :]
    rel = qi + BLOCK - ki
    relc = np.maximum(rel, 0)
    max_exact = N_BUCKETS // 2
    relf = np.maximum(relc, 1).astype(np.float32)
    large = max_exact + (np.log(relf / np.float32(max_exact)) / np.float32(math.log(MAX_DISTANCE / max_exact))
                         * np.float32(N_BUCKETS - max_exact)).astype(np.int32)
    large = np.minimum(large, N_BUCKETS - 1)
    bucket = np.where(relc < max_exact, relc, large)
    bucket = np.where((rel >= 0) & (rel < BLOCK), bucket, -1)
    return jnp.asarray(bucket.reshape(1, BLOCK * 2 * BLOCK), jnp.int32)


def bias_band(rel_bias_t, buckets):
    n = BLOCK * 2 * BLOCK
    tn = 4096

    def body(bk_ref, rb_ref, o_ref):
        row = lax.broadcasted_iota(jnp.int32, (N_BUCKETS, tn), 0)
        oh = jnp.where(row == bk_ref[...], 1.0, 0.0).astype(BF16)
        rb = rb_ref[...]
        p0 = rb.astype(BF16)
        r1 = rb - p0.astype(F32)
        p1 = r1.astype(BF16)
        p2 = (r1 - p1.astype(F32)).astype(BF16)
        o_ref[...] = _dot(p0, oh) + _dot(p1, oh) + _dot(p2, oh)

    return pl.pallas_call(
        body, name="bias_band", grid=(n // tn,),
        in_specs=[pl.BlockSpec((1, tn), lambda i: (0, i)), pl.BlockSpec((N_HEADS, N_BUCKETS), lambda i: (0, 0))],
        out_specs=pl.BlockSpec((N_HEADS, tn), lambda i: (0, i)),
        out_shape=SDS((N_HEADS, n), F32), compiler_params=_cp(1))(buckets, rel_bias_t)


def bias_band_bwd(dband, buckets):
    n = BLOCK * 2 * BLOCK
    tn = 4096

    def body(bk_ref, d_ref, o_ref):
        @pl.when(pl.program_id(0) == 0)
        def _():
            o_ref[...] = jnp.zeros_like(o_ref)
        row = lax.broadcasted_iota(jnp.int32, (N_BUCKETS, tn), 0)
        oh = jnp.where(row == bk_ref[...], 1.0, 0.0).astype(BF16)
        dv = d_ref[...]
        p0 = dv.astype(BF16)
        r1 = dv - p0.astype(F32)
        p1 = r1.astype(BF16)
        p2 = (r1 - p1.astype(F32)).astype(BF16)
        o_ref[...] += _dot_nt(oh, p0) + _dot_nt(oh, p1) + _dot_nt(oh, p2)

    return pl.pallas_call(
        body, name="bias_band_bwd", grid=(n // tn,),
        in_specs=[pl.BlockSpec((1, tn), lambda i: (0, i)), pl.BlockSpec((N_HEADS, tn), lambda i: (0, i))],
        out_specs=pl.BlockSpec((N_BUCKETS, N_HEADS), lambda i: (0, 0)),
        out_shape=SDS((N_BUCKETS, N_HEADS), F32), compiler_params=_cp(1))(buckets, dband)


def _dup_half(band, which):
    lane = lax.broadcasted_iota(jnp.int32, band.shape, 1)
    rolled = pltpu.roll(band, 64, 1)
    keep = (lane < 64) if which == 0 else (lane >= 64)
    return jnp.where(keep, band, rolled)


def _attn_probs(scores, bias, sink, valid):
    sc = jnp.where(valid, scores * SCALE + bias, NEG_INF)
    m = jnp.maximum(jnp.max(sc, axis=-1, keepdims=True), sink)
    e = jnp.exp(sc - m)
    es = jnp.exp(sink - m)
    inv = 1.0 / (jnp.sum(e, axis=-1, keepdims=True) + es)
    return e * inv, es * inv


def _stack_heads(src_ref, kv, dst):
    lane = lax.broadcasted_iota(jnp.int32, (BLOCK, 128), 1)
    for jj in range(4):
        slab = src_ref[:, (4 * kv + jj) * 128:(4 * kv + jj + 1) * 128]
        for hh in range(2):
            keep = (lane < 64) if hh == 0 else (lane >= 64)
            dst[pl.ds((2 * jj + hh) * BLOCK, BLOCK), :] = jnp.where(keep, slab, jnp.zeros_like(slab))


def _unstack_heads(stacked, dst_ref, kv):
    lane = lax.broadcasted_iota(jnp.int32, (BLOCK, 128), 1)
    for jj in range(4):
        lo = stacked[(2 * jj) * BLOCK:(2 * jj + 1) * BLOCK]
        hi = stacked[(2 * jj + 1) * BLOCK:(2 * jj + 2) * BLOCK]
        dst_ref[:, (4 * kv + jj) * 128:(4 * kv + jj + 1) * 128] = jnp.where(lane < 64, lo, hi).astype(dst_ref.dtype)


def _band_valid(n):
    qi = lax.broadcasted_iota(jnp.int32, (BLOCK, 2 * BLOCK), 0)
    ki = lax.broadcasted_iota(jnp.int32, (BLOCK, 2 * BLOCK), 1)
    rel = qi + BLOCK - ki
    return (rel >= 0) & (rel < BLOCK) & ((ki >= BLOCK) | (n > 0))


def _kv_bands(prev_ref, cur_ref):
    band = jnp.concatenate([prev_ref[...].astype(F32), cur_ref[...].astype(F32)], axis=0)
    return [_dup_half(band, 0).astype(BF16), _dup_half(band, 1).astype(BF16)]


def attn_fwd(proj, band, sinks):
    s = proj.shape[0]
    nb = s // BLOCK
    qw = 1024

    def body(sk_ref, q_ref, kp_ref, kc_ref, vp_ref, vc_ref, b_ref, o_ref, qs_buf, s_buf, p_buf):
        n = pl.program_id(0)
        gp = pl.program_id(1)
        valid = _band_valid(n)
        kks = _kv_bands(kp_ref, kc_ref)
        vvs = _kv_bands(vp_ref, vc_ref)
        for kv in range(2):
            _stack_heads(q_ref, kv, qs_buf)
            s_buf[...] = _dot_nt(qs_buf[...], kks[kv])
            for hq in range(8):
                hl = 8 * kv + hq
                rows = pl.ds(hq * BLOCK, BLOCK)
                p, _ = _attn_probs(s_buf[rows, :], b_ref[hl], sk_ref[gp * 16 + hl], valid)
                p_buf[rows, :] = p.astype(BF16)
            _unstack_heads(_dot(p_buf[...], vvs[kv]), o_ref, kv)

    kb, vb = OFF_K // 128, OFF_V // 128
    return pl.pallas_call(
        body, name="attn_fwd", grid=(nb, 2),
        in_specs=[pl.BlockSpec(memory_space=pltpu.SMEM),
                  pl.BlockSpec((BLOCK, qw), lambda n, g: (n, OFF_Q // qw + g)),
                  pl.BlockSpec((BLOCK, 128), lambda n, g: (jnp.maximum(n - 1, 0), kb + g)),
                  pl.BlockSpec((BLOCK, 128), lambda n, g: (n, kb + g)),
                  pl.BlockSpec((BLOCK, 128), lambda n, g: (jnp.maximum(n - 1, 0), vb + g)),
                  pl.BlockSpec((BLOCK, 128), lambda n, g: (n, vb + g)),
                  pl.BlockSpec((16, BLOCK, 2 * BLOCK), lambda n, g: (g, 0, 0))],
        out_specs=pl.BlockSpec((BLOCK, qw), lambda n, g: (n, g)),
        out_shape=SDS((s, D), BF16),
        scratch_shapes=[pltpu.VMEM((8 * BLOCK, 128), BF16), pltpu.VMEM((8 * BLOCK, 2 * BLOCK), F32),
                        pltpu.VMEM((8 * BLOCK, 2 * BLOCK), BF16)],
        compiler_params=_cp(2))(sinks, proj, proj, proj, proj, proj, band)


def merge_fwd(ya, att, w_lru_out, w_attn_out, proj):
    s = ya.shape[0]
    tm, tn = min(1024, s), 512

    def body(ya_ref, at_ref, wl_ref, wt_ref, ga_ref, gb_ref, yab_ref, mg_ref):
        y_a = _dot(ya_ref[...], wl_ref[...])
        y_b = _dot(at_ref[...], wt_ref[...])
        yab_ref[0] = y_a.astype(BF16)
        yab_ref[1] = y_b.astype(BF16)
        mg_ref[...] = (_sigmoid(ga_ref[...].astype(F32)) * y_a + _sigmoid(gb_ref[...].astype(F32)) * y_b).astype(BF16)

    return pl.pallas_call(
        body, name="merge_fwd", grid=(s // tm, D // tn),
        in_specs=[pl.BlockSpec((tm, D), lambda i, j: (i, 0)), pl.BlockSpec((tm, D), lambda i, j: (i, 0)),
                  pl.BlockSpec((D, tn), lambda i, j: (0, j)), pl.BlockSpec((D, tn), lambda i, j: (0, j)),
                  pl.BlockSpec((tm, tn), lambda i, j: (i, OFF_GA // tn + j)),
                  pl.BlockSpec((tm, tn), lambda i, j: (i, OFF_GB // tn + j))],
        out_specs=[pl.BlockSpec((2, tm, tn), lambda i, j: (0, i, j)), pl.BlockSpec((tm, tn), lambda i, j: (i, j))],
        out_shape=[SDS((2, s, D), BF16), SDS((s, D), BF16)],
        compiler_params=_cp(2))(ya, att, w_lru_out, w_attn_out, proj, proj)


def outproj_fwd(merged, w_out, x, vecs):
    s = x.shape[0]
    tm, tn = min(1024, s), 512

    def body(m_ref, w_ref, x_ref, v_ref, x1_ref, o1_ref):
        o1 = _dot(m_ref[...], w_ref[...])
        o1_ref[...] = o1.astype(BF16)
        x1_ref[...] = x_ref[...] + v_ref[V_GATE1:V_GATE1 + 1, :] * o1

    return pl.pallas_call(
        body, name="outproj_fwd", grid=(s // tm, D // tn),
        in_specs=[pl.BlockSpec((tm, D), lambda i, j: (i, 0)), pl.BlockSpec((D, tn), lambda i, j: (0, j)),
                  pl.BlockSpec((tm, tn), lambda i, j: (i, j)), pl.BlockSpec((16, tn), lambda i, j: (0, j))],
        out_specs=[pl.BlockSpec((tm, tn), lambda i, j: (i, j)), pl.BlockSpec((tm, tn), lambda i, j: (i, j))],
        out_shape=[SDS((s, D), F32), SDS((s, D), BF16)],
        compiler_params=_cp(2))(merged, w_out, x, vecs)


def ff1_fwd(x1, vecs, w_ff1):
    s = x1.shape[0]
    tm, tn = min(1024, s), 512
    per = D // tn

    def body(x_ref, v_ref, w_ref, f_ref, h_ref):
        @pl.when(pl.program_id(1) == 0)
        def _():
            _modulated_norm(x_ref, v_ref, V_G2, V_SCALE2, V_SHIFT2, h_ref, tm)
        f_ref[...] = _dot(h_ref[...], w_ref[...]).astype(BF16)

    return pl.pallas_call(
        body, name="ff1_fwd", grid=(s // tm, D_FF // tn),
        in_specs=[pl.BlockSpec((tm, D), lambda i, j: (i, 0)), pl.BlockSpec((16, D), lambda i, j: (0, 0)),
                  pl.BlockSpec((None, D, tn), lambda i, j: (j // per, 0, j % per))],
        out_specs=[pl.BlockSpec((tm, tn), lambda i, j: (i, j)), pl.BlockSpec((tm, D), lambda i, j: (i, 0))],
        out_shape=[SDS((s, D_FF), BF16), SDS((s, D), BF16)],
        compiler_params=_cp(2))(x1, vecs, w_ff1)


def ff2_loss(f, w_ff2, x1, tgt, vecs):
    s = x1.shape[0]
    tm, tk = min(512, s), 1024
    nk = D_FF // tk

    def body(f_ref, w_ref, x1_hbm, t_hbm, v_ref, dx2_ref, do2_ref, sums_ref, loss_ref, acc, x1_ref, t_ref, sems):
        i, k = pl.program_id(0), pl.program_id(1)
        fetches = _row_fetches((x1_hbm, t_hbm), (x1_ref, t_ref), sems, i, tm)

        @pl.when((i == 0) & (k == 0))
        def _():
            sums_ref[...] = jnp.zeros_like(sums_ref)
            loss_ref[...] = jnp.zeros_like(loss_ref)

        @pl.when(k == 0)
        def _():
            acc[...] = jnp.zeros_like(acc)
            for cp in fetches:
                cp.start()

        fv = jnp.maximum(f_ref[...].astype(F32), 0.0)
        acc[...] += _dot((fv * fv).astype(BF16), w_ref[...])

        @pl.when(k == nk - 1)
        def _():
            for cp in fetches:
                cp.wait()
            gate2 = v_ref[V_GATE2:V_GATE2 + 1, :]
            g3 = v_ref[V_G3:V_G3 + 1, :]

            def sub(rb, carry):
                rs = pl.ds(pl.multiple_of(rb * SUB, SUB), SUB)
                o2 = acc[rs, :]
                x2 = x1_ref[rs, :] + gate2 * o2
                r3, xh = _rms_parts(x2)
                e = xh * g3 - t_ref[rs, :]
                loss_ref[...] += (0.5 / D) * jnp.sum(e * e)
                dy = e * (1.0 / D)
                sums_ref[0:1, :] += jnp.sum(dy * xh, axis=0, keepdims=True)
                dxh = dy * g3
                dx2 = r3 * (dxh - xh * jnp.mean(dxh * xh, axis=-1, keepdims=True))
                sums_ref[1:2, :] += jnp.sum(dx2 * o2, axis=0, keepdims=True)
                dx2_ref[rs, :] = dx2
                do2_ref[rs, :] = (dx2 * gate2).astype(BF16)
                return carry

            lax.fori_loop(0, tm // SUB, sub, 0)

    return pl.pallas_call(
        body, name="ff2_loss", grid=(s // tm, nk),
        in_specs=[pl.BlockSpec((tm, tk), lambda i, k: (i, k)), pl.BlockSpec((tk, D), lambda i, k: (k, 0)),
                  pl.BlockSpec(memory_space=pl.ANY), pl.BlockSpec(memory_space=pl.ANY),
                  pl.BlockSpec((16, D), lambda i, k: (0, 0))],
        out_specs=[pl.BlockSpec((tm, D), lambda i, k: (i, 0)), pl.BlockSpec((tm, D), lambda i, k: (i, 0)),
                   pl.BlockSpec((8, D), lambda i, k: (0, 0)), pl.BlockSpec((8, 128), lambda i, k: (0, 0))],
        out_shape=[SDS((s, D), F32), SDS((s, D), BF16), SDS((8, D), F32), SDS((8, 128), F32)],
        scratch_shapes=[pltpu.VMEM((tm, D), F32), pltpu.VMEM((tm, D), F32), pltpu.VMEM((tm, D), F32),
                        pltpu.SemaphoreType.DMA((2,))],
        compiler_params=_cp(2))(f, w_ff2, x1, tgt, vecs)


def ff2_bwd(do2, w_ff2, f):
    s = do2.shape[0]
    tm, tn = min(1024, s), 512

    def body(d_ref, w_ref, f_ref, o_ref):
        dff = _dot_nt(d_ref[...], w_ref[...])
        o_ref[...] = (dff * (2.0 * jnp.maximum(f_ref[...].astype(F32), 0.0))).astype(BF16)

    return pl.pallas_call(
        body, name="ff2_bwd", grid=(s // tm, D_FF // tn),
        in_specs=[pl.BlockSpec((tm, D), lambda i, j: (i, 0)), pl.BlockSpec((tn, D), lambda i, j: (j, 0)),
                  pl.BlockSpec((tm, tn), lambda i, j: (i, j))],
        out_specs=pl.BlockSpec((tm, tn), lambda i, j: (i, j)),
        out_shape=SDS((s, D_FF), BF16), compiler_params=_cp(2))(do2, w_ff2, f)


def weight_grad(name, a, b, tn, out_shape, out_block, out_map, relu2=False):
    s, m = a.shape
    n = b.shape[1]
    tm = WG_TM
    chunk = min(1024, s)
    nch = s // chunk

    def body(a_hbm, b_ref, o_ref, a_buf, at_s, sem):
        i = pl.program_id(0)

        @pl.when(pl.program_id(1) == 0)
        def _():
            def fetch(ch):
                return pltpu.make_async_copy(a_hbm.at[pl.ds(ch * chunk, chunk), pl.ds(i * tm, tm)],
                                             a_buf.at[ch % 2], sem.at[ch % 2])
            fetch(0).start()
            for ch in range(nch):
                if ch + 1 < nch:
                    fetch(ch + 1).start()
                fetch(ch).wait()
                av = a_buf[ch % 2]
                if relu2:
                    fv = jnp.maximum(av.astype(F32), 0.0)
                    av = (fv * fv).astype(BF16)
                at_s[:, ch * chunk:(ch + 1) * chunk] = av.T

        o_ref[...] = _dot(at_s[...], b_ref[...]).astype(BF16)

    return pl.pallas_call(
        body, name=name, grid=(m // tm, n // tn),
        in_specs=[pl.BlockSpec(memory_space=pl.ANY), pl.BlockSpec((s, tn), lambda i, j: (0, j))],
        out_specs=pl.BlockSpec(out_block, lambda i, j: out_map(i, j)),
        out_shape=SDS(out_shape, BF16),
        scratch_shapes=[pltpu.VMEM((2, chunk, tm), BF16), pltpu.VMEM((tm, s), BF16), pltpu.SemaphoreType.DMA((2,))],
        compiler_params=_cp(2))(a, b)


def ff1_bwd(df, w_ff1, x1, dx2, o1, vecs):
    s = df.shape[0]
    tm, tk = min(512, s), 1024
    nk = D_FF // tk
    per = D // tk

    def body(d_ref, w_ref, x1_hbm, dx2_hbm, o1_hbm, v_ref, dx1_ref, do1_ref, sums_ref, acc, x1_ref, dx2_ref, o1_ref, sems):
        i, k = pl.program_id(0), pl.program_id(1)
        fetches = _row_fetches((x1_hbm, dx2_hbm, o1_hbm), (x1_ref, dx2_ref, o1_ref), sems, i, tm)

        @pl.when((i == 0) & (k == 0))
        def _():
            sums_ref[...] = jnp.zeros_like(sums_ref)

        @pl.when(k == 0)
        def _():
            acc[...] = jnp.zeros_like(acc)
            for cp in fetches:
                cp.start()

        acc[...] += _dot_nt(d_ref[...], w_ref[...])

        @pl.when(k == nk - 1)
        def _():
            for cp in fetches:
                cp.wait()
            g2 = v_ref[V_G2:V_G2 + 1, :]
            scale2 = v_ref[V_SCALE2:V_SCALE2 + 1, :]
            gate1 = v_ref[V_GATE1:V_GATE1 + 1, :]

            def sub(rb, carry):
                rs = pl.ds(pl.multiple_of(rb * SUB, SUB), SUB)
                dh = acc[rs, :]
                r2, xh = _rms_parts(x1_ref[rs, :])
                sums_ref[0:1, :] += jnp.sum(dh, axis=0, keepdims=True)
                sums_ref[1:2, :] += jnp.sum(dh * (xh * g2), axis=0, keepdims=True)
                dxn = dh * (1.0 + scale2)
                sums_ref[2:3, :] += jnp.sum(dxn * xh, axis=0, keepdims=True)
                dxh = dxn * g2
                dx1 = dx2_ref[rs, :] + r2 * (dxh - xh * jnp.mean(dxh * xh, axis=-1, keepdims=True))
                sums_ref[3:4, :] += jnp.sum(dx1 * o1_ref[rs, :].astype(F32), axis=0, keepdims=True)
                dx1_ref[rs, :] = dx1
                do1_ref[rs, :] = (dx1 * gate1).astype(BF16)
                return carry

            lax.fori_loop(0, tm // SUB, sub, 0)

    return pl.pallas_call(
        body, name="ff1_bwd", grid=(s // tm, nk),
        in_specs=[pl.BlockSpec((tm, tk), lambda i, k: (i, k)),
                  pl.BlockSpec((None, D, tk), lambda i, k: (k // per, 0, k % per)),
                  pl.BlockSpec(memory_space=pl.ANY), pl.BlockSpec(memory_space=pl.ANY),
                  pl.BlockSpec(memory_space=pl.ANY), pl.BlockSpec((16, D), lambda i, k: (0, 0))],
        out_specs=[pl.BlockSpec((tm, D), lambda i, k: (i, 0)), pl.BlockSpec((tm, D), lambda i, k: (i, 0)),
                   pl.BlockSpec((8, D), lambda i, k: (0, 0))],
        out_shape=[SDS((s, D), F32), SDS((s, D), BF16), SDS((8, D), F32)],
        scratch_shapes=[pltpu.VMEM((tm, D), F32), pltpu.VMEM((tm, D), F32), pltpu.VMEM((tm, D), F32),
                        pltpu.VMEM((tm, D), BF16), pltpu.SemaphoreType.DMA((3,))],
        compiler_params=_cp(2))(df, w_ff1, x1, dx2, o1, vecs)


def outproj_bwd(do1, w_out, yab, proj):
    s = do1.shape[0]
    tm, tn = min(1024, s), 512
    per = D // tn

    def body(d_ref, w_ref, y_ref, g_ref, dy_ref, dp_ref):
        dm = _dot_nt(d_ref[...], w_ref[...])
        sg = _sigmoid(g_ref[...].astype(F32))
        dy_ref[...] = (dm * sg).astype(BF16)
        dp_ref[...] = (dm * y_ref[...].astype(F32) * sg * (1.0 - sg)).astype(BF16)

    return pl.pallas_call(
        body, name="outproj_bwd", grid=(s // tm, 2 * per),
        in_specs=[pl.BlockSpec((tm, D), lambda i, j: (i, 0)), pl.BlockSpec((tn, D), lambda i, j: (j % per, 0)),
                  pl.BlockSpec((None, tm, tn), lambda i, j: (j // per, i, j % per)),
                  pl.BlockSpec((tm, tn), lambda i, j: (i, OFF_GA // tn + j))],
        out_specs=[pl.BlockSpec((None, tm, tn), lambda i, j: (j // per, i, j % per)),
                   pl.BlockSpec((tm, tn), lambda i, j: (i, OFF_GA // tn + j))],
        out_shape=[SDS((2, s, D), BF16), SDS((s, IN_W), BF16)],
        compiler_params=_cp(2))(do1, w_out, yab, proj)


def lruout_bwd(dyab, w_lru_out, rec, proj, dproj):
    s = rec.shape[0]
    tm, tn = min(1024, s), 512

    def body(d_ref, w_ref, r_ref, g_ref, dp_in, dr_ref, dp_ref):
        dya = _dot_nt(d_ref[...], w_ref[...])
        gate = g_ref[...].astype(F32)
        dr_ref[...] = dya * _gelu(gate)
        dp_ref[...] = (dya * r_ref[...] * _gelu_grad(gate)).astype(BF16)

    return pl.pallas_call(
        body, name="lruout_bwd", grid=(s // tm, D // tn),
        in_specs=[pl.BlockSpec((None, tm, D), lambda i, j: (0, i, 0)), pl.BlockSpec((tn, D), lambda i, j: (j, 0)),
                  pl.BlockSpec((tm, tn), lambda i, j: (i, j)),
                  pl.BlockSpec((tm, tn), lambda i, j: (i, OFF_GATE // tn + j)),
                  pl.BlockSpec(memory_space=pl.ANY)],
        out_specs=[pl.BlockSpec((tm, tn), lambda i, j: (i, j)),
                   pl.BlockSpec((tm, tn), lambda i, j: (i, OFF_GATE // tn + j))],
        out_shape=[SDS((s, D), F32), SDS((s, IN_W), BF16)],
        input_output_aliases={4: 1},
        compiler_params=_cp(2))(dyab, w_lru_out, rec, proj, dproj)


def attnout_bwd(dyab, w_attn_out):
    s = dyab.shape[1]
    tm, tn = min(1024, s), 512

    def body(d_ref, w_ref, o_ref):
        o_ref[...] = _dot_nt(d_ref[...], w_ref[...]).astype(BF16)

    return pl.pallas_call(
        body, name="attnout_bwd", grid=(s // tm, D // tn),
        in_specs=[pl.BlockSpec((None, tm, D), lambda i, j: (1, i, 0)), pl.BlockSpec((tn, D), lambda i, j: (j, 0))],
        out_specs=pl.BlockSpec((tm, tn), lambda i, j: (i, j)),
        out_shape=SDS((s, D), BF16), compiler_params=_cp(2))(dyab, w_attn_out)


def attn_bwd(proj, band, sinks, datt, dproj):
    s = proj.shape[0]
    nb = s // BLOCK
    qw = 1024

    def body(sk_ref, q_ref, kp_ref, kc_ref, vp_ref, vc_ref, b_ref, do_ref, dp_in,
             dq_ref, dkb_ref, dvb_ref, db_ref, ds_ref, qs_buf, dos_buf, s_buf, dp_buf, p_buf, dsc_buf):
        gp = pl.program_id(0)
        n = pl.program_id(1)

        @pl.when(n == 0)
        def _():
            db_ref[...] = jnp.zeros_like(db_ref)
            ds_ref[...] = jnp.zeros_like(ds_ref)

        valid = _band_valid(n)
        kks = _kv_bands(kp_ref, kc_ref)
        vvs = _kv_bands(vp_ref, vc_ref)
        lane_b = lax.broadcasted_iota(jnp.int32, (2 * BLOCK, 128), 1)
        dks, dvs = [], []
        for kv in range(2):
            _stack_heads(q_ref, kv, qs_buf)
            _stack_heads(do_ref, kv, dos_buf)
            s_buf[...] = _dot_nt(qs_buf[...], kks[kv])
            dp_buf[...] = _dot_nt(dos_buf[...], vvs[kv])
            for hq in range(8):
                hl = 8 * kv + hq
                rows = pl.ds(hq * BLOCK, BLOCK)
                p, ps = _attn_probs(s_buf[rows, :], b_ref[hl], sk_ref[gp * 16 + hl], valid)
                dp = dp_buf[rows, :]
                delta = jnp.sum(p * dp, axis=-1, keepdims=True)
                dsc = p * (dp - delta)
                db_ref[hl] += dsc
                ds_ref[hl:hl + 1, :] += jnp.zeros((1, 128), F32) - jnp.sum(ps * delta)
                p_buf[rows, :] = p.astype(BF16)
                dsc_buf[rows, :] = (dsc * SCALE).astype(BF16)
            _unstack_heads(_dot(dsc_buf[...], kks[kv]), dq_ref, kv)
            dk = _dot_tn(dsc_buf[...], qs_buf[...])
            dv = _dot_tn(p_buf[...], dos_buf[...])
            dks.append(dk + pltpu.roll(dk, 64, 1))
            dvs.append(dv + pltpu.roll(dv, 64, 1))
        dkb_ref[...] = jnp.where(lane_b < 64, dks[0], dks[1])
        dvb_ref[...] = jnp.where(lane_b < 64, dvs[0], dvs[1])

    kb, vb = OFF_K // 128, OFF_V // 128
    return pl.pallas_call(
        body, name="attn_bwd", grid=(2, nb),
        in_specs=[pl.BlockSpec(memory_space=pltpu.SMEM),
                  pl.BlockSpec((BLOCK, qw), lambda g, n: (n, OFF_Q // qw + g)),
                  pl.BlockSpec((BLOCK, 128), lambda g, n: (jnp.maximum(n - 1, 0), kb + g)),
                  pl.BlockSpec((BLOCK, 128), lambda g, n: (n, kb + g)),
                  pl.BlockSpec((BLOCK, 128), lambda g, n: (jnp.maximum(n - 1, 0), vb + g)),
                  pl.BlockSpec((BLOCK, 128), lambda g, n: (n, vb + g)),
                  pl.BlockSpec((16, BLOCK, 2 * BLOCK), lambda g, n: (g, 0, 0)),
                  pl.BlockSpec((BLOCK, qw), lambda g, n: (n, g)),
                  pl.BlockSpec(memory_space=pl.ANY)],
        out_specs=[pl.BlockSpec((BLOCK, qw), lambda g, n: (n, OFF_Q // qw + g)),
                   pl.BlockSpec((2 * BLOCK, 128), lambda g, n: (n, g)),
                   pl.BlockSpec((2 * BLOCK, 128), lambda g, n: (n, g)),
                   pl.BlockSpec((16, BLOCK, 2 * BLOCK), lambda g, n: (g, 0, 0)),
                   pl.BlockSpec((16, 128), lambda g, n: (g, 0))],
        out_shape=[SDS((s, IN_W), BF16), SDS((nb * 2 * BLOCK, 256), F32), SDS((nb * 2 * BLOCK, 256), F32),
                   SDS((N_HEADS, BLOCK, 2 * BLOCK), F32), SDS((N_HEADS, 128), F32)],
        input_output_aliases={8: 0},
        scratch_shapes=[pltpu.VMEM((8 * BLOCK, 128), BF16), pltpu.VMEM((8 * BLOCK, 128), BF16),
                        pltpu.VMEM((8 * BLOCK, 2 * BLOCK), F32), pltpu.VMEM((8 * BLOCK, 2 * BLOCK), F32),
                        pltpu.VMEM((8 * BLOCK, 2 * BLOCK), BF16), pltpu.VMEM((8 * BLOCK, 2 * BLOCK), BF16)],
        compiler_params=_cp(2))(sinks, proj, proj, proj, proj, proj, band, datt, dproj)


def dkv_combine(dkb, dvb, dproj):
    nb = dkb.shape[0] // (2 * BLOCK)
    s = nb * BLOCK
    dkb3 = dkb.reshape(nb, 2 * BLOCK, 256)
    dvb3 = dvb.reshape(nb, 2 * BLOCK, 256)

    def body(k1, k2, v1, v2, dp_in, o_ref):
        nxt = jnp.where(pl.program_id(0) < nb - 1, 1.0, 0.0)
        o_ref[:, 0:256] = (k1[...] + nxt * k2[...]).astype(BF16)
        o_ref[:, 256:512] = (v1[...] + nxt * v2[...]).astype(BF16)

    spec1 = pl.BlockSpec((None, BLOCK, 256), lambda m: (m, 1, 0))
    spec2 = pl.BlockSpec((None, BLOCK, 256), lambda m: (jnp.minimum(m + 1, nb - 1), 0, 0))
    return pl.pallas_call(
        body, name="dkv_combine", grid=(nb,),
        in_specs=[spec1, spec2, spec1, spec2, pl.BlockSpec(memory_space=pl.ANY)],
        out_specs=pl.BlockSpec((BLOCK, 512), lambda m: (m, OFF_K // 512)),
        out_shape=SDS((s, IN_W), BF16), input_output_aliases={4: 0},
        compiler_params=_cp(1))(dkb3, dkb3, dvb3, dvb3, dproj)


def lru_bwd(proj, rec, drec, lvec, wa, wx, dproj):
    s = proj.shape[0]
    t = min(256, s)
    nt = s // t

    def body(lx_ref, lxh_ref, rec_ref, rech_ref, dr_ref, lv_ref, wa_ref, wx_ref, dp_in,
             dlx_ref, sums_ref, dwa_ref, dwx_ref,
             xbuf, hbuf, dxbuf, a_s, dh_s, xc_s, r_s, ig_s, mu_s, gc):
        step_i = pl.program_id(0)
        ti = nt - 1 - step_i

        @pl.when(step_i == 0)
        def _():
            sums_ref[...] = jnp.zeros_like(sums_ref)
            dwa_ref[...] = jnp.zeros_like(dwa_ref)
            dwx_ref[...] = jnp.zeros_like(dwx_ref)
            dxbuf[pl.ds(t, 8), :] = jnp.zeros((8, D), F32)
            gc[...] = jnp.zeros((8, D), F32)

        live = jnp.where(ti > 0, 1.0, 0.0)
        xbuf[pl.ds(0, 8), :] = lxh_ref[...].astype(F32)[8:16] * live
        xbuf[pl.ds(8, t), :] = lx_ref[...].astype(F32)
        hbuf[pl.ds(0, 8), :] = rech_ref[...] * live
        hbuf[pl.ds(8, t), :] = rec_ref[...]
        first = (lax.broadcasted_iota(jnp.int32, (t, 128), 0) + ti * t) == 0
        for b in range(N_LRU_BLOCKS):
            cs = slice(b * 128, (b + 1) * 128)
            _, xc, _, r, ig, _, a, mult = _lru_block_fwd(xbuf, lv_ref, wa_ref, wx_ref, b, t, first)
            a_s[:, cs] = a
            xc_s[:, cs] = xc
            r_s[:, cs] = r
            ig_s[:, cs] = ig
            mu_s[:, cs] = mult

        def step(q, g):
            tt = t - 1 - q
            dh = dr_ref[pl.ds(tt, 1), :] + g
            dh_s[pl.ds(tt, 1), :] = dh
            return a_s[pl.ds(tt, 1), :] * dh

        gc[0:1, :] = lax.fori_loop(0, t, step, gc[0:1, :], unroll=8)
        for b in range(N_LRU_BLOCKS):
            cs = slice(b * 128, (b + 1) * 128)
            dh = dh_s[:, cs]
            a = a_s[:, cs]
            xc = xc_s[:, cs]
            r = r_s[:, cs]
            ig = ig_s[:, cs]
            mult = mu_s[:, cs]
            sp = _softplus(-lv_ref[L_LAM:L_LAM + 1, cs])
            lam = lv_ref[L_LAM:L_LAM + 1, cs]
            da = dh * hbuf[pl.ds(7, t), cs]
            dmult = jnp.where(first, 0.0, dh * ig * xc)
            dig = dh * mult * xc
            dxc = dh * mult * ig
            dlog_a = da * a - dmult * (a * a) / mult
            dr = dlog_a * ((-LRU_C) * sp)
            dsp = jnp.sum(dlog_a * ((-LRU_C) * r), axis=0, keepdims=True)
            dza = dr * r * (1.0 - r)
            dzx = dig * ig * (1.0 - ig)
            dzab = dza.astype(BF16)
            dzxb = dzx.astype(BF16)
            xcb = xc.astype(BF16)
            dwa_ref[b] += _dot_tn(xcb, dzab)
            dwx_ref[b] += _dot_tn(xcb, dzxb)
            dxc = dxc + _dot_nt(dzab, wa_ref[b]) + _dot_nt(dzxb, wx_ref[b])
            sums_ref[L_LAM:L_LAM + 1, cs] += dsp * (-jax.nn.sigmoid(-lam))
            sums_ref[L_BA:L_BA + 1, cs] += jnp.sum(dza, axis=0, keepdims=True)
            sums_ref[L_BX:L_BX + 1, cs] += jnp.sum(dzx, axis=0, keepdims=True)
            sums_ref[L_CB:L_CB + 1, cs] += jnp.sum(dxc, axis=0, keepdims=True)
            for kk in range(4):
                sums_ref[kk:kk + 1, cs] += jnp.sum(dxc * xbuf[pl.ds(5 + kk, t), cs], axis=0, keepdims=True)
            dxbuf[pl.ds(0, t), cs] = dxc
            dlx = (lv_ref[3:4, cs] * dxc + lv_ref[2:3, cs] * dxbuf[pl.ds(1, t), cs]
                   + lv_ref[1:2, cs] * dxbuf[pl.ds(2, t), cs] + lv_ref[0:1, cs] * dxbuf[pl.ds(3, t), cs])
            dlx_ref[:, cs] = dlx.astype(BF16)
        dxbuf[pl.ds(t, 8), :] = dxbuf[pl.ds(0, 8), :]

    rev = lambda i: nt - 1 - i
    return pl.pallas_call(
        body, name="lru_bwd", grid=(nt,),
        in_specs=[pl.BlockSpec((t, D), lambda i: (rev(i), 0)),
                  pl.BlockSpec((16, D), lambda i: (jnp.maximum(rev(i) * (t // 16) - 1, 0), 0)),
                  pl.BlockSpec((t, D), lambda i: (rev(i), 0)),
                  pl.BlockSpec((8, D), lambda i: (jnp.maximum(rev(i) * (t // 8) - 1, 0), 0)),
                  pl.BlockSpec((t, D), lambda i: (rev(i), 0)),
                  pl.BlockSpec((8, D), lambda i: (0, 0)),
                  pl.BlockSpec((N_LRU_BLOCKS, 128, 128), lambda i: (0, 0, 0)),
                  pl.BlockSpec((N_LRU_BLOCKS, 128, 128), lambda i: (0, 0, 0)),
                  pl.BlockSpec(memory_space=pl.ANY)],
        out_specs=[pl.BlockSpec((t, D), lambda i: (rev(i), 0)),
                   pl.BlockSpec((8, D), lambda i: (0, 0)),
                   pl.BlockSpec((N_LRU_BLOCKS, 128, 128), lambda i: (0, 0, 0)),
                   pl.BlockSpec((N_LRU_BLOCKS, 128, 128), lambda i: (0, 0, 0))],
        out_shape=[SDS((s, IN_W), BF16), SDS((8, D), F32), SDS((N_LRU_BLOCKS, 128, 128), F32),
                   SDS((N_LRU_BLOCKS, 128, 128), F32)],
        scratch_shapes=[pltpu.VMEM((t + 8, D), F32), pltpu.VMEM((t + 8, D), F32), pltpu.VMEM((t + 8, D), F32)]
        + [pltpu.VMEM((t, D), F32)] * 6 + [pltpu.VMEM((8, D), F32)],
        input_output_aliases={8: 0},
        compiler_params=_cp(1))(proj, proj, rec, rec, drec, lvec, wa, wx, dproj)


def inproj_bwd(dproj, w_in, x, dx1, vecs):
    s = x.shape[0]
    tm, tk = min(512, s), IN_TILE
    nk = IN_W // tk
    per = IN_SHARD // tk

    def body(d_ref, w_ref, x_hbm, dx1_hbm, v_ref, gx_ref, sums_ref, acc, x_ref, dx1_ref, sems):
        i, k = pl.program_id(0), pl.program_id(1)
        fetches = _row_fetches((x_hbm, dx1_hbm), (x_ref, dx1_ref), sems, i, tm)

        @pl.when((i == 0) & (k == 0))
        def _():
            sums_ref[...] = jnp.zeros_like(sums_ref)

        @pl.when(k == 0)
        def _():
            acc[...] = jnp.zeros_like(acc)
            for cp in fetches:
                cp.start()

        acc[...] += _dot_nt(d_ref[...], w_ref[...])

        @pl.when(k == nk - 1)
        def _():
            for cp in fetches:
                cp.wait()
            g1 = v_ref[V_G1:V_G1 + 1, :]
            scale1 = v_ref[V_SCALE1:V_SCALE1 + 1, :]

            def sub(rb, carry):
                rs = pl.ds(pl.multiple_of(rb * SUB, SUB), SUB)
                dh = acc[rs, :]
                r1, xh = _rms_parts(x_ref[rs, :])
                sums_ref[0:1, :] += jnp.sum(dh, axis=0, keepdims=True)
                sums_ref[1:2, :] += jnp.sum(dh * (xh * g1), axis=0, keepdims=True)
                dxn = dh * (1.0 + scale1)
                sums_ref[2:3, :] += jnp.sum(dxn * xh, axis=0, keepdims=True)
                dxh = dxn * g1
                gx_ref[rs, :] = dx1_ref[rs, :] + r1 * (dxh - xh * jnp.mean(dxh * xh, axis=-1, keepdims=True))
                return carry

            lax.fori_loop(0, tm // SUB, sub, 0)

    return pl.pallas_call(
        body, name="inproj_bwd", grid=(s // tm, nk),
        in_specs=[pl.BlockSpec((tm, tk), lambda i, k: (i, k)),
                  pl.BlockSpec((None, D, tk), lambda i, k: (k // per, 0, k % per)),
                  pl.BlockSpec(memory_space=pl.ANY), pl.BlockSpec(memory_space=pl.ANY),
                  pl.BlockSpec((16, D), lambda i, k: (0, 0))],
        out_specs=[pl.BlockSpec((tm, D), lambda i, k: (i, 0)), pl.BlockSpec((8, D), lambda i, k: (0, 0))],
        out_shape=[SDS((s, D), F32), SDS((8, D), F32)],
        scratch_shapes=[pltpu.VMEM((tm, D), F32), pltpu.VMEM((tm, D), F32), pltpu.VMEM((tm, D), F32),
                        pltpu.SemaphoreType.DMA((2,))],
        compiler_params=_cp(2))(dproj, w_in, x, dx1, vecs)


def mod_columns(c16, w_ada, b_cols):
    tn = 512

    def body(c_ref, w_ref, b_ref, o_ref):
        cv = c_ref[...]
        ca = (cv * jax.nn.sigmoid(cv)).astype(BF16)
        o_ref[...] = _dot(ca, w_ref[...].astype(BF16)) + b_ref[...]

    return pl.pallas_call(
        body, name="mod_columns", grid=(ADA_SHARD // tn,),
        in_specs=[pl.BlockSpec((16, D), lambda j: (0, 0)), pl.BlockSpec((D, tn), lambda j: (0, j)),
                  pl.BlockSpec((1, tn), lambda j: (0, j))],
        out_specs=pl.BlockSpec((16, tn), lambda j: (0, j)),
        out_shape=SDS((16, ADA_SHARD), F32), compiler_params=_cp(1))(c16, w_ada, b_cols)


def wada_update(c16, dmod16, w, m, v):
    tm, tn = 512, 512

    def body(c_ref, d_ref, w_ref, m_ref, v_ref, g_out, dl_out, m_out, v_out):
        cv = c_ref[...]
        ca = (cv * jax.nn.sigmoid(cv)).astype(BF16)
        g = _dot_tn(ca, d_ref[...].astype(BF16))
        dl, m2, v2 = _adamw_math(w_ref[...], g, m_ref[...], v_ref[...])
        g_out[...] = g
        dl_out[...] = dl
        m_out[...] = m2
        v_out[...] = v2

    tile = pl.BlockSpec((tm, tn), lambda i, j: (i, j))
    return pl.pallas_call(
        body, name="wada_update", grid=(D // tm, ADA_SHARD // tn),
        in_specs=[pl.BlockSpec((16, tm), lambda i, j: (0, i)), pl.BlockSpec((16, tn), lambda i, j: (0, j)),
                  tile, tile, tile],
        out_specs=[tile] * 4, out_shape=[SDS((D, ADA_SHARD), F32)] * 4,
        compiler_params=_cp(2))(c16, dmod16, w, m, v)


def adamw_big(name, w, mine, theirs, m, v, c_idx):
    r, c = w.shape
    tr = 128
    per = (r // 2) // tr

    def body(c_ref, w_ref, a_ref, b_ref, m_ref, v_ref, g_out, dl_out, m_out, v_out):
        own = (pl.program_id(0) // per) == c_ref[0]
        g = jnp.where(own, a_ref[...], b_ref[...])
        dl, m2, v2 = _adamw_math(w_ref[...], g, m_ref[...], v_ref[...])
        g_out[...] = g
        dl_out[...] = dl
        m_out[...] = m2
        v_out[...] = v2

    tile = pl.BlockSpec((tr, c), lambda i, cr: (i, 0))
    half = pl.BlockSpec((tr, c), lambda i, cr: (i % per, 0))
    gs = pltpu.PrefetchScalarGridSpec(num_scalar_prefetch=1, grid=(r // tr,),
                                      in_specs=[tile, half, half, tile, tile], out_specs=[tile] * 4)
    return pl.pallas_call(body, name=name, grid_spec=gs, out_shape=[SDS((r, c), F32)] * 4,
                          compiler_params=_cp(1))(c_idx, w, mine, theirs, m, v)


def cast_into_slot(name, w, k_idx):
    r, c = w.shape
    tr = 256

    def body(k_ref, w_ref, o_ref):
        o_ref[...] = w_ref[...].astype(BF16)

    gs = pltpu.PrefetchScalarGridSpec(
        num_scalar_prefetch=1, grid=(r // tr,),
        in_specs=[pl.BlockSpec((tr, c), lambda i, kr: (i, 0))],
        out_specs=pl.BlockSpec((None, tr, c), lambda i, kr: (kr[0], i, 0)))
    return pl.pallas_call(body, name=name, grid_spec=gs, out_shape=SDS((N_CHIPS, r, c), BF16),
                          compiler_params=_cp(1))(k_idx, w)


def adamw_small(ws, gs, ms, vs):
    n = len(ws)

    def body(*refs):
        for i in range(n):
            dl, m2, v2 = _adamw_math(refs[i][...], refs[n + i][...], refs[2 * n + i][...], refs[3 * n + i][...])
            refs[4 * n + i][...] = dl
            refs[5 * n + i][...] = m2
            refs[6 * n + i][...] = v2

    vm = pl.BlockSpec(memory_space=pltpu.VMEM)
    shapes = [SDS(w.shape, F32) for w in ws]
    outs = pl.pallas_call(
        body, name="adamw_small", in_specs=[vm] * (4 * n), out_specs=[vm] * (3 * n), out_shape=shapes * 3,
        compiler_params=pltpu.CompilerParams(vmem_limit_bytes=VMEM_LIMIT))(*ws, *gs, *ms, *vs)
    return outs[:n], outs[n:2 * n], outs[2 * n:]


def sum_devices(gathered):
    rows = gathered.shape[1]
    tr = 96

    def body(x_ref, o_ref):
        acc = x_ref[0]
        for d in range(1, N_DEV):
            acc = acc + x_ref[d]
        o_ref[...] = acc

    return pl.pallas_call(
        body, name="sum_devices", grid=(rows // tr,),
        in_specs=[pl.BlockSpec((N_DEV, tr, D), lambda i: (0, i, 0))],
        out_specs=pl.BlockSpec((tr, D), lambda i: (i, 0)),
        out_shape=SDS((rows, D), F32), compiler_params=_cp(1))(gathered)


def _mesh_pos():
    return lax.axis_index("x"), lax.axis_index("y"), lax.axis_index("c")


def _other_chips(x, y):
    return [(1 - x, y), (x, 1 - y), (1 - x, 1 - y)]


def all_gather_small(name, block):
    m_per, n = block.shape

    def body(x_ref, out_ref, send_sems, recv_sems, local_sem):
        x, y, c = _mesh_pos()
        me, sibling = (x, y, c), (x, y, 1 - c)
        chips = _other_chips(x, y)

        def rows(px, py, pc):
            return out_ref.at[pl.ds((4 * px + 2 * py + pc) * m_per, m_per), :]

        def copy(k, blk, to, src=None):
            return pltpu.make_async_remote_copy(
                src_ref=rows(*blk) if src is None else src, dst_ref=rows(*blk),
                send_sem=send_sems.at[k], recv_sem=recv_sems.at[k], device_id=to, device_id_type=MESH)

        mine = pltpu.make_async_copy(x_ref, rows(*me), local_sem)
        mine.start()
        first = [copy(0, me, sibling, src=x_ref)]
        first += [copy(1 + j, me, (*chip, c), src=x_ref) for j, chip in enumerate(chips)]
        for cp in first:
            cp.start()
        passed = [copy(4 + j, (*chip, c), sibling) for j, chip in enumerate(chips)]
        for j, chip in enumerate(chips):
            copy(1 + j, (*chip, c), me).wait_recv()
            passed[j].start()
        copy(0, sibling, me).wait_recv()
        for j, chip in enumerate(chips):
            copy(4 + j, (*chip, 1 - c), me).wait_recv()
        for cp in first + passed:
            cp.wait_send()
        mine.wait()

    vm = pl.BlockSpec(memory_space=pltpu.VMEM)
    return pl.pallas_call(
        body, name=name, out_shape=SDS((N_DEV * m_per, n), block.dtype), in_specs=[vm], out_specs=vm,
        scratch_shapes=[pltpu.SemaphoreType.DMA((7,)), pltpu.SemaphoreType.DMA((7,)), pltpu.SemaphoreType.DMA],
        compiler_params=pltpu.CompilerParams(vmem_limit_bytes=VMEM_LIMIT))(block)


def all_gather_weights(bufs):
    n = len(bufs)
    halves = [w.shape[1] // 2 for w in bufs]

    def body(*refs):
        outs = refs[n:2 * n]
        send_sems, recv_sems = refs[2 * n:]
        x, y, c = _mesh_pos()
        k = 2 * x + y
        me, sibling = (x, y, c), (x, y, 1 - c)
        chips = _other_chips(x, y)

        def region(i, chip_idx, half):
            return outs[i].at[chip_idx, pl.ds(half * halves[i], halves[i]), :]

        def copy(i, j, reg, to):
            return pltpu.make_async_remote_copy(src_ref=reg, dst_ref=reg, send_sem=send_sems.at[i, j],
                                                recv_sem=recv_sems.at[i, j], device_id=to, device_id_type=MESH)

        started = []
        for i in range(n):
            for j, chip in enumerate(chips):
                cp = copy(i, j, region(i, k, c), (*chip, c))
                cp.start()
                started.append(cp)
        for i in range(n):
            for j, chip in enumerate(chips):
                kj = 2 * chip[0] + chip[1]
                copy(i, j, region(i, kj, c), me).wait_recv()
                cp = copy(i, 3 + j, region(i, kj, c), sibling)
                cp.start()
                started.append(cp)
        for i in range(n):
            for j, chip in enumerate(chips):
                kj = 2 * chip[0] + chip[1]
                copy(i, 3 + j, region(i, kj, 1 - c), me).wait_recv()
        for cp in started:
            cp.wait_send()

    hbm = pl.BlockSpec(memory_space=pl.ANY)
    return pl.pallas_call(
        body, name="all_gather_weights", in_specs=[hbm] * n, out_specs=[hbm] * n,
        out_shape=[SDS(w.shape, w.dtype) for w in bufs],
        input_output_aliases={i: i for i in range(n)},
        scratch_shapes=[pltpu.SemaphoreType.DMA((n, 6)), pltpu.SemaphoreType.DMA((n, 6))])(*bufs)


def sibling_exchange(name, grads):
    n = len(grads)
    halves = [g.shape[1] // 2 for g in grads]

    def body(*refs):
        ins, outs = refs[:n], refs[n:2 * n]
        send_sems, recv_sems = refs[2 * n:]
        x, y, c = _mesh_pos()
        sibling = (x, y, 1 - c)
        cps = []
        for i in range(n):
            cp = pltpu.make_async_remote_copy(
                src_ref=ins[i].at[:, pl.ds((1 - c) * halves[i], halves[i]), :], dst_ref=outs[i],
                send_sem=send_sems.at[i], recv_sem=recv_sems.at[i], device_id=sibling, device_id_type=MESH)
            cp.start()
            cps.append(cp)
        for cp in cps:
            cp.wait_recv()
        for cp in cps:
            cp.wait_send()

    hbm = pl.BlockSpec(memory_space=pl.ANY)
    return pl.pallas_call(
        body, name=name, in_specs=[hbm] * n, out_specs=[hbm] * n,
        out_shape=[SDS((N_CHIPS, h, g.shape[2]), g.dtype) for g, h in zip(grads, halves)],
        scratch_shapes=[pltpu.SemaphoreType.DMA((n,)), pltpu.SemaphoreType.DMA((n,))])(*grads)


def sibling_sum(name, grad, other, c_idx):
    _, r, cc = grad.shape
    h = r // 2
    tr = min(256, h)
    g4 = grad.reshape(N_CHIPS, 2, h, cc)

    def body(c_ref, a_ref, b_ref, o_ref):
        o_ref[...] = (a_ref[...].astype(F32) + b_ref[...].astype(F32)).astype(BF16)

    gs = pltpu.PrefetchScalarGridSpec(
        num_scalar_prefetch=1, grid=(N_CHIPS, h // tr),
        in_specs=[pl.BlockSpec((None, None, tr, cc), lambda s, i, cr: (s, cr[0], i, 0)),
                  pl.BlockSpec((None, tr, cc), lambda s, i, cr: (s, i, 0))],
        out_specs=pl.BlockSpec((None, tr, cc), lambda s, i, cr: (s, i, 0)))
    return pl.pallas_call(body, name=name, grid_spec=gs, out_shape=SDS((N_CHIPS, h, cc), BF16),
                          compiler_params=_cp(2))(c_idx, g4, other)


HBM_SPEC = pl.BlockSpec(memory_space=pltpu.HBM)
SEM_SPEC = pl.BlockSpec(memory_space=pltpu.SEMAPHORE)


def _side_effecting():
    return pltpu.CompilerParams(has_side_effects=pltpu.SideEffectType.DATAFLOW_SIDE_EFFECTING)


def _in_hbm(a):
    return pltpu.with_memory_space_constraint(a, pltpu.HBM)


def gather_start(bufs, after):
    n = len(bufs)
    halves = [w.shape[1] // 2 for w in bufs]

    def body(*refs):
        ins = refs[:n]
        send_sems, recv_sems, token = refs[n + 1], refs[n + 2], refs[-1]
        x, y, c = _mesh_pos()
        k = 2 * x + y
        for i in range(n):
            reg = ins[i].at[k, pl.ds(c * halves[i], halves[i]), :]
            for j, chip in enumerate(_other_chips(x, y)):
                pltpu.make_async_remote_copy(src_ref=reg, dst_ref=reg, send_sem=send_sems.at[3 * i + j],
                                             recv_sem=recv_sems.at[3 * i + j], device_id=(*chip, c),
                                             device_id_type=MESH).start()
        token[...] = jnp.zeros_like(token)

    outs = pl.pallas_call(
        body, name="gather_start",
        out_shape=(pltpu.SemaphoreType.DMA((3 * n,)), pltpu.SemaphoreType.DMA((3 * n,)),
                   *[pltpu.HBM(w.shape, w.dtype) for w in bufs], SDS((8, 128), F32)),
        in_specs=[HBM_SPEC] * n + [pl.BlockSpec(memory_space=pl.ANY)],
        out_specs=(SEM_SPEC, SEM_SPEC, *[HBM_SPEC] * n, pl.BlockSpec(memory_space=pltpu.VMEM)),
        input_output_aliases={i: 2 + i for i in range(n)},
        compiler_params=_side_effecting())(*[_in_hbm(w) for w in bufs], after)
    return outs[0], outs[1], list(outs[2:2 + n]), outs[-1]


def gather_wait(send_sems, recv_sems, bufs, after):
    n = len(bufs)
    halves = [w.shape[1] // 2 for w in bufs]

    def body(*refs):
        ins = refs[:n]
        send_sems, recv_sems = refs[n], refs[n + 1]
        x, y, c = _mesh_pos()
        k = 2 * x + y
        for i in range(n):
            for j, chip in enumerate(_other_chips(x, y)):
                kj = 2 * chip[0] + chip[1]
                cp = pltpu.make_async_remote_copy(
                    src_ref=ins[i].at[k, pl.ds(c * halves[i], halves[i]), :],
                    dst_ref=ins[i].at[kj, pl.ds(c * halves[i], halves[i]), :],
                    send_sem=send_sems.at[3 * i + j], recv_sem=recv_sems.at[3 * i + j], device_id=(*chip, c),
                    device_id_type=MESH)
                cp.wait_send()
                cp.wait_recv()

    return pl.pallas_call(
        body, name="gather_wait", out_shape=[pltpu.HBM(w.shape, w.dtype) for w in bufs],
        in_specs=[HBM_SPEC] * n + [SEM_SPEC, SEM_SPEC, pl.BlockSpec(memory_space=pl.ANY)],
        out_specs=[HBM_SPEC] * n, input_output_aliases={i: i for i in range(n)},
        compiler_params=_side_effecting())(*bufs, send_sems, recv_sems, after)


def gather_forward(bufs):
    n = len(bufs)
    halves = [w.shape[1] // 2 for w in bufs]

    def body(*refs):
        outs = refs[n:2 * n]
        send_sems, recv_sems = refs[2 * n:]
        x, y, c = _mesh_pos()
        chips = _other_chips(x, y)

        def copy(i, j, half, to):
            kj = 2 * chips[j][0] + chips[j][1]
            reg = outs[i].at[kj, pl.ds(half * halves[i], halves[i]), :]
            return pltpu.make_async_remote_copy(src_ref=reg, dst_ref=reg, send_sem=send_sems.at[i, j],
                                                recv_sem=recv_sems.at[i, j], device_id=to, device_id_type=MESH)

        cps = [copy(i, j, c, (x, y, 1 - c)) for i in range(n) for j in range(3)]
        for cp in cps:
            cp.start()
        for i in range(n):
            for j in range(3):
                copy(i, j, 1 - c, (x, y, c)).wait_recv()
        for cp in cps:
            cp.wait_send()

    hbm = pl.BlockSpec(memory_space=pl.ANY)
    return pl.pallas_call(
        body, name="gather_forward", in_specs=[hbm] * n, out_specs=[hbm] * n,
        out_shape=[SDS(w.shape, w.dtype) for w in bufs], input_output_aliases={i: i for i in range(n)},
        scratch_shapes=[pltpu.SemaphoreType.DMA((n, 3)), pltpu.SemaphoreType.DMA((n, 3))])(*bufs)


def exchange_start(name, sums):
    n = len(sums)
    lands = [lax.empty((3,) + t.shape[1:], t.dtype) for t in sums]

    def body(*refs):
        ins, zones = refs[:n], refs[n:2 * n]
        send_sems, recv_sems, token = refs[2 * n], refs[2 * n + 1], refs[-1]
        x, y, c = _mesh_pos()
        for i in range(n):
            for j, chip in enumerate(_other_chips(x, y)):
                kj = 2 * chip[0] + chip[1]
                pltpu.make_async_remote_copy(src_ref=ins[i].at[kj], dst_ref=zones[i].at[j],
                                             send_sem=send_sems.at[3 * i + j], recv_sem=recv_sems.at[3 * i + j],
                                             device_id=(*chip, c), device_id_type=MESH).start()
        token[...] = jnp.zeros_like(token)

    outs = pl.pallas_call(
        body, name=name,
        out_shape=(pltpu.SemaphoreType.DMA((3 * n,)), pltpu.SemaphoreType.DMA((3 * n,)),
                   *[pltpu.HBM(t.shape, t.dtype) for t in sums], *[pltpu.HBM(t.shape, t.dtype) for t in lands],
                   SDS((8, 128), F32)),
        in_specs=[HBM_SPEC] * (2 * n),
        out_specs=(SEM_SPEC, SEM_SPEC, *[HBM_SPEC] * (2 * n), pl.BlockSpec(memory_space=pltpu.VMEM)),
        input_output_aliases={i: 2 + i for i in range(2 * n)},
        compiler_params=_side_effecting())(*[_in_hbm(t) for t in sums], *[_in_hbm(t) for t in lands])
    return outs[0], outs[1], list(outs[2:2 + n]), list(outs[2 + n:2 + 2 * n]), outs[-1]


def exchange_wait(name, send_sems, recv_sems, sums, lands, after):
    n = len(sums)

    def body(*refs):
        ins, zones = refs[:n], refs[n:2 * n]
        send_sems, recv_sems = refs[2 * n], refs[2 * n + 1]
        x, y, c = _mesh_pos()
        for i in range(n):
            for j, chip in enumerate(_other_chips(x, y)):
                kj = 2 * chip[0] + chip[1]
                cp = pltpu.make_async_remote_copy(src_ref=ins[i].at[kj], dst_ref=zones[i].at[j],
                                                  send_sem=send_sems.at[3 * i + j], recv_sem=recv_sems.at[3 * i + j],
                                                  device_id=(*chip, c), device_id_type=MESH)
                cp.wait_send()
                cp.wait_recv()

    outs = pl.pallas_call(
        body, name=name, out_shape=[pltpu.HBM(t.shape, t.dtype) for t in sums + lands],
        in_specs=[HBM_SPEC] * (2 * n) + [SEM_SPEC, SEM_SPEC, pl.BlockSpec(memory_space=pl.ANY)],
        out_specs=[HBM_SPEC] * (2 * n), input_output_aliases={i: i for i in range(2 * n)},
        compiler_params=_side_effecting())(*sums, *lands, send_sems, recv_sems, after)
    return list(outs[:n]), list(outs[n:])


def chip_sum(name, sums, parts, k_idx):
    _, h, cc = parts.shape
    tr = min(256, h)

    def body(k_ref, own_ref, p_ref, o_ref):
        acc = own_ref[...].astype(F32)
        for s in range(3):
            acc = acc + p_ref[s].astype(F32)
        o_ref[...] = acc

    gs = pltpu.PrefetchScalarGridSpec(
        num_scalar_prefetch=1, grid=(h // tr,),
        in_specs=[pl.BlockSpec((None, tr, cc), lambda i, kr: (kr[0], i, 0)),
                  pl.BlockSpec((3, tr, cc), lambda i, kr: (0, i, 0))],
        out_specs=pl.BlockSpec((tr, cc), lambda i, kr: (i, 0)))
    return pl.pallas_call(body, name=name, grid_spec=gs, out_shape=SDS((h, cc), F32),
                          compiler_params=_cp(1))(k_idx, sums, parts)


def halves_exchange(name, halves):
    n = len(halves)

    def body(*refs):
        ins, outs = refs[:n], refs[n:2 * n]
        send_sems, recv_sems = refs[2 * n:]
        x, y, c = _mesh_pos()
        cps = []
        for i in range(n):
            cp = pltpu.make_async_remote_copy(
                src_ref=ins[i], dst_ref=outs[i], send_sem=send_sems.at[i], recv_sem=recv_sems.at[i],
                device_id=(x, y, 1 - c), device_id_type=MESH)
            cp.start()
            cps.append(cp)
        for cp in cps:
            cp.wait_recv()
        for cp in cps:
            cp.wait_send()

    hbm = pl.BlockSpec(memory_space=pl.ANY)
    return pl.pallas_call(
        body, name=name, in_specs=[hbm] * n, out_specs=[hbm] * n,
        out_shape=[SDS(t.shape, F32) for t in halves],
        scratch_shapes=[pltpu.SemaphoreType.DMA((n,)), pltpu.SemaphoreType.DMA((n,))])(*halves)


def local_step(x, tgt, vecs, lvec, wa, wx, sinks, rel_bias, w_in, rest_weights, grads_out, grad_in_out):
    buckets = t5_bucket_table()
    band = bias_band(rel_bias.T, buckets).reshape(N_HEADS, BLOCK, 2 * BLOCK)

    proj, h = inproj_fwd(x, vecs, w_in)
    ya, rec = lru_fwd(proj, lvec, wa, wx)
    att = attn_fwd(proj, band, sinks)
    w_lru_out, w_attn_out, w_out, w_ff1, w_ff2 = rest_weights(att[:8, :128] + ya[:8, :128])
    w_lru_out2, w_attn_out2, w_out2 = w_lru_out.reshape(D, D), w_attn_out.reshape(D, D), w_out.reshape(D, D)
    w_ff2_2 = w_ff2.reshape(D_FF, D)
    yab, merged = merge_fwd(ya, att, w_lru_out2, w_attn_out2, proj)
    x1, o1 = outproj_fwd(merged, w_out2, x, vecs)
    f, h2 = ff1_fwd(x1, vecs, w_ff1)
    dx2, do2, sums_f, loss = ff2_loss(f, w_ff2_2, x1, tgt, vecs)

    df = ff2_bwd(do2, w_ff2_2, f)
    g_ff2 = weight_grad("dw_ff2", f, do2, 512, (D_FF, D), (WG_TM, 512), lambda i, j: (i, j), relu2=True)
    dx1, do1, sums_2 = ff1_bwd(df, w_ff1, x1, dx2, o1, vecs)
    g_ff1 = weight_grad("dw_ff1", h2, df, 512, (N_CHIPS, D, D), (None, WG_TM, 512), lambda i, j: (j // 4, i, j % 4))
    dyab, dproj = outproj_bwd(do1, w_out2, yab, proj)
    g_out = weight_grad("dw_out", merged, do1, 512, (D, D), (WG_TM, 512), lambda i, j: (i, j))
    drec, dproj = lruout_bwd(dyab, w_lru_out2, rec, proj, dproj)
    g_lru_out = weight_grad("dw_lru_out", ya, dyab[0], 512, (D, D), (WG_TM, 512), lambda i, j: (i, j))
    datt = attnout_bwd(dyab, w_attn_out2)
    g_attn_out = weight_grad("dw_attn_out", att, dyab[1], 512, (D, D), (WG_TM, 512), lambda i, j: (i, j))
    zero = grads_out([g_lru_out.reshape(N_CHIPS, D // 4, D), g_attn_out.reshape(N_CHIPS, D // 4, D),
                      g_out.reshape(N_CHIPS, D // 4, D), g_ff1, g_ff2.reshape(N_CHIPS, D_FF // 4, D)])
    dproj, dkb, dvb, dband, dsink = attn_bwd(proj, band, sinks + zero, datt, dproj)
    dproj = dkv_combine(dkb, dvb, dproj)
    dproj, sums_l, d_wa, d_wx = lru_bwd(proj, rec, drec, lvec, wa, wx, dproj)
    per = IN_SHARD // IN_TILE
    g_in = weight_grad("dw_in", h, dproj, IN_TILE, (N_CHIPS, D, IN_SHARD), (None, WG_TM, IN_TILE),
                       lambda i, j: (j // per, i, j % per))
    zero = grad_in_out(g_in)
    grad_x, sums_1 = inproj_bwd(dproj, w_in, x, dx1, vecs + zero)
    d_rel_bias = bias_band_bwd(dband.reshape(N_HEADS, BLOCK * 2 * BLOCK), buckets)

    small = dict(sums_f=sums_f, sums_2=sums_2, sums_1=sums_1, sums_l=sums_l, d_wa=d_wa, d_wx=d_wx,
                 d_sinks=dsink[:, 0], d_rel_bias=d_rel_bias)
    return loss, grad_x, small


def _pad_rows(a, rows):
    return jnp.concatenate([a, jnp.zeros((rows - a.shape[0], a.shape[1]), a.dtype)], axis=0)


def kernel(x, c, w_ada, b_ada, norm1_g, w_in, conv_w, conv_b, lru_wa, lru_ba, lru_wx, lru_bx, lru_lambda, w_lru_out, w_attn_out, attn_sinks, rel_bias, w_out, norm2_g, w_ff1, w_ff2, final_g, loss_target, m_w_ada, m_b_ada, m_norm1_g, m_w_in, m_conv_w, m_conv_b, m_lru_wa, m_lru_ba, m_lru_wx, m_lru_bx, m_lru_lambda, m_w_lru_out, m_w_attn_out, m_attn_sinks, m_rel_bias, m_w_out, m_norm2_g, m_w_ff1, m_w_ff2, m_final_g, v_w_ada, v_b_ada, v_norm1_g, v_w_in, v_conv_w, v_conv_b, v_lru_wa, v_lru_ba, v_lru_wx, v_lru_bx, v_lru_lambda, v_w_lru_out, v_w_attn_out, v_attn_sinks, v_rel_bias, v_w_out, v_norm2_g, v_w_ff1, v_w_ff2, v_final_g):
    xi, yi, ci = _mesh_pos()
    chip = 2 * xi + yi
    dev = 2 * chip + ci
    z8 = jnp.zeros((8, D), F32)

    conv_rows = jnp.concatenate([conv_w[0], jnp.zeros((4, D - D // 4), F32)], axis=1)
    pack0 = jnp.concatenate([c, conv_rows, jnp.zeros((3, D), F32)], axis=0)
    g0 = all_gather_small("gather_cond", pack0).reshape(N_DEV, 8, D)
    c_all = g0[:, 0, :]
    conv_full = jnp.concatenate([g0[2 * k, 1:5, :D // 4] for k in range(N_CHIPS)], axis=1)
    c16 = jnp.concatenate([c_all, z8], axis=0)
    b_cols = lax.dynamic_slice_in_dim(b_ada, chip * ADA_SHARD, ADA_SHARD, axis=1)
    mod_c = mod_columns(c16, w_ada[0], b_cols)
    g1 = all_gather_small("gather_mod", mod_c).reshape(N_DEV, 16, ADA_SHARD)
    mod = jnp.concatenate([lax.dynamic_index_in_dim(g1[2 * k], dev, axis=0, keepdims=False) for k in range(N_CHIPS)])
    shift1, scale1, gate1, shift2, scale2, gate2 = [mod[i * D:(i + 1) * D] for i in range(6)]
    vecs = jnp.stack([norm1_g[0], scale1, shift1, gate1, norm2_g[0], scale2, shift2, gate2, final_g]
                     + [jnp.zeros((D,), F32)] * 7)
    lvec = jnp.concatenate([conv_full, conv_b, lru_ba, lru_bx, lru_lambda], axis=0)

    shards = [w_in[0], w_lru_out[0], w_attn_out[0], w_out[0], w_ff1[0], w_ff2[0]]
    names = ["w_in", "w_lru_out", "w_attn_out", "w_out", "w_ff1", "w_ff2"]
    k_idx = jnp.reshape(chip, (1,)).astype(jnp.int32)
    c_idx = jnp.reshape(ci, (1,)).astype(jnp.int32)
    slots = [cast_into_slot("cast_" + nm, w, k_idx) for nm, w in zip(names, shards)]
    w_in_full = all_gather_weights(slots[:1])[0]
    g_send, g_recv, in_flight, token = gather_start(slots[1:], w_in_full)
    vecs = vecs + token[0, 0]
    pending = {}

    def rest_weights(after):
        return gather_forward(gather_wait(g_send, g_recv, in_flight, after))

    def start_reduce(tag, nms, grads):
        from_sibling = sibling_exchange("sibling_exchange_" + tag, grads)
        sums = [sibling_sum("sibling_sum_" + nm, g, o, c_idx) for nm, g, o in zip(nms, grads, from_sibling)]
        pending[tag] = exchange_start("exchange_start_" + tag, sums)
        return pending[tag][-1][0, 0]

    loss_t, grad_x, small = local_step(
        x[0], loss_target[0], vecs, lvec, lru_wa[0].astype(BF16), lru_wx[0].astype(BF16),
        attn_sinks[0], rel_bias, w_in_full, rest_weights,
        lambda grads: start_reduce("a", names[1:], grads), lambda g: start_reduce("b", names[:1], [g]))
    loss = lax.psum(loss_t[0, 0], ("x", "y", "c"))

    sums_f, sums_2, sums_1, sums_l = small["sums_f"], small["sums_2"], small["sums_1"], small["sums_l"]
    vec_rows = jnp.stack([sums_1[2], sums_2[2], sums_f[0], sums_l[L_CB], sums_l[L_BA], sums_l[L_BX],
                          sums_l[L_LAM], jnp.zeros((D,), F32)])
    mod_rows = jnp.stack([sums_1[0], sums_1[1], sums_2[3], sums_2[0], sums_2[1], sums_f[1],
                          jnp.zeros((D,), F32), jnp.zeros((D,), F32)])
    att_rows = jnp.concatenate([
        jnp.concatenate([small["d_sinks"], jnp.zeros((D - N_HEADS,), F32)])[None],
        jnp.concatenate([small["d_rel_bias"].reshape(-1), jnp.zeros((D - N_BUCKETS * N_HEADS,), F32)])[None],
        jnp.zeros((6, D), F32)], axis=0)
    pack = jnp.concatenate([vec_rows, _pad_rows(sums_l[0:4], 8), mod_rows, att_rows,
                            small["d_wa"].reshape(128, D), small["d_wx"].reshape(128, D)], axis=0)
    gathered = all_gather_small("gather_small_grads", pack).reshape(N_DEV, P_ROWS, D)
    total = sum_devices(gathered)
    dmod_all = gathered[:, P_MOD:P_MOD + 6, :].reshape(N_DEV, 6 * D)
    dmod16 = jnp.concatenate([lax.dynamic_slice_in_dim(dmod_all, chip * ADA_SHARD, ADA_SHARD, axis=1),
                              jnp.zeros((8, ADA_SHARD), F32)], axis=0)
    g_w_ada, d_w_ada, nm_w_ada, nv_w_ada = wada_update(c16, dmod16, w_ada[0], m_w_ada[0], v_w_ada[0])

    mine, theirs = {}, {}
    for tag, nms in (("a", names[1:]), ("b", names[:1])):
        send_sems, recv_sems, sums, lands, _ = pending[tag]
        sums, lands = exchange_wait("exchange_wait_" + tag, send_sems, recv_sems, sums, lands, grad_x)
        halves = [chip_sum("chip_sum_" + nm, t, p, k_idx) for nm, t, p in zip(nms, sums, lands)]
        for nm, a, b in zip(nms, halves, halves_exchange("halves_exchange_" + tag, halves)):
            mine[nm], theirs[nm] = a, b
    big_m = [m_w_in, m_w_lru_out, m_w_attn_out, m_w_out, m_w_ff1, m_w_ff2]
    big_v = [v_w_in, v_w_lru_out, v_w_attn_out, v_w_out, v_w_ff1, v_w_ff2]
    g_big, d_big, nm_big, nv_big = {}, {}, {}, {}
    for nm, w, m, v in zip(names, shards, big_m, big_v):
        g2, dl, m2, v2 = adamw_big("adamw_" + nm, w, mine[nm], theirs[nm], m[0], v[0], c_idx)
        g_big[nm], d_big[nm], nm_big[nm], nv_big[nm] = g2[None], dl[None], m2[None], v2[None]

    conv_g = lax.dynamic_slice_in_dim(total[P_CONVW:P_CONVW + 4], chip * (D // 4), D // 4, axis=1)
    sm_names = ["b_ada", "norm1_g", "conv_w", "conv_b", "lru_wa", "lru_ba", "lru_wx", "lru_bx", "lru_lambda",
                "attn_sinks", "rel_bias", "norm2_g", "final_g"]
    sm_w = [b_ada.reshape(6, D), norm1_g, conv_w[0], conv_b, lru_wa.reshape(128, D), lru_ba, lru_wx.reshape(128, D),
            lru_bx, lru_lambda, attn_sinks, rel_bias, norm2_g, final_g[None]]
    sm_m = [m_b_ada.reshape(6, D), m_norm1_g, m_conv_w[0], m_conv_b, m_lru_wa.reshape(128, D), m_lru_ba,
            m_lru_wx.reshape(128, D), m_lru_bx, m_lru_lambda, m_attn_sinks, m_rel_bias, m_norm2_g, m_final_g[None]]
    sm_v = [v_b_ada.reshape(6, D), v_norm1_g, v_conv_w[0], v_conv_b, v_lru_wa.reshape(128, D), v_lru_ba,
            v_lru_wx.reshape(128, D), v_lru_bx, v_lru_lambda, v_attn_sinks, v_rel_bias, v_norm2_g, v_final_g[None]]
    sm_g = [total[P_MOD:P_MOD + 6], total[0:1], conv_g, total[3:4], total[P_WA:P_WA + 128], total[4:5],
            total[P_WX:P_WX + 128], total[5:6], total[6:7], total[P_ATT:P_ATT + 1, :N_HEADS],
            total[P_ATT + 1, :N_BUCKETS * N_HEADS].reshape(N_BUCKETS, N_HEADS), total[1:2], total[2:3]]
    sm_d, sm_nm, sm_nv = adamw_small(sm_w, sm_g, sm_m, sm_v)
    shapes = dict(b_ada=b_ada.shape, norm1_g=norm1_g.shape, conv_w=conv_w.shape, conv_b=conv_b.shape,
                  lru_wa=lru_wa.shape, lru_ba=lru_ba.shape, lru_wx=lru_wx.shape, lru_bx=lru_bx.shape,
                  lru_lambda=lru_lambda.shape, attn_sinks=attn_sinks.shape, rel_bias=rel_bias.shape,
                  norm2_g=norm2_g.shape, final_g=final_g.shape)
    grads = dict(w_ada=g_w_ada[None], **g_big)
    deltas = dict(w_ada=d_w_ada[None], **d_big)
    new_m = dict(w_ada=nm_w_ada[None], **nm_big)
    new_v = dict(w_ada=nv_w_ada[None], **nv_big)
    for i, nm in enumerate(sm_names):
        grads[nm] = sm_g[i].reshape(shapes[nm])
        deltas[nm] = sm_d[i].reshape(shapes[nm])
        new_m[nm] = sm_nm[i].reshape(shapes[nm])
        new_v[nm] = sm_nv[i].reshape(shapes[nm])
    order = ["w_ada", "b_ada", "norm1_g", "w_in", "conv_w", "conv_b", "lru_wa", "lru_ba", "lru_wx", "lru_bx",
             "lru_lambda", "w_lru_out", "w_attn_out", "attn_sinks", "rel_bias", "w_out", "norm2_g", "w_ff1", "w_ff2",
             "final_g"]
    return (loss, grad_x[None], *[grads[n] for n in order], *[deltas[n] for n in order],
            *[new_m[n] for n in order], *[new_v[n] for n in order])
```

```python
import math

import numpy as np
import jax
import jax.numpy as jnp
from jax import lax
from jax.experimental import pallas as pl
from jax.experimental.pallas import tpu as pltpu

F32 = jnp.float32
BF16 = jnp.bfloat16
SDS = jax.ShapeDtypeStruct
MESH = pl.DeviceIdType.MESH

D = 2048
D_FF = 4 * D
N_HEADS = 32
HEAD_DIM = 64
BLOCK = 128
N_LRU_BLOCKS = 16
LRU_C = 8.0
EPS = 1e-6
NEG_INF = -1e30
N_BUCKETS = 32
MAX_DISTANCE = 128
IN_W = 10752
IN_SHARD = IN_W // 4
IN_TILE = 896
ADA_SHARD = 6 * D // 4
OFF_LRU, OFF_GATE, OFF_Q, OFF_K, OFF_V, OFF_GA, OFF_GB = 0, 2048, 4096, 6144, 6400, 6656, 8704
SCALE = HEAD_DIM ** -0.5
N_CHIPS = 4
N_DEV = 8

ADAM_LR, ADAM_B1, ADAM_B2, ADAM_EPS, ADAM_WD, ADAM_STEP = 0.001, 0.9, 0.999, 1e-08, 0.01, 10
ADAM_C1 = 1.0 - ADAM_B1 ** ADAM_STEP
ADAM_C2 = 1.0 - ADAM_B2 ** ADAM_STEP

VMEM_LIMIT = 52 * 2 ** 20
SUB = 128
WG_TM = 1024
V_G1, V_SCALE1, V_SHIFT1, V_GATE1, V_G2, V_SCALE2, V_SHIFT2, V_GATE2, V_G3 = range(9)
L_CW0, L_CB, L_BA, L_BX, L_LAM = 0, 4, 5, 6, 7
P_VEC, P_CONVW, P_MOD, P_ATT, P_WA = 0, 8, 16, 24, 32


def _cp(n_axes):
    return pltpu.CompilerParams(dimension_semantics=("arbitrary",) * n_axes, vmem_limit_bytes=VMEM_LIMIT)


def _dot(a, b):
    return jnp.dot(a, b, preferred_element_type=F32)


def _dot_nt(a, b):
    return lax.dot_general(a, b, (((1,), (1,)), ((), ())), preferred_element_type=F32)


def _dot_tn(a, b):
    return lax.dot_general(a, b, (((0,), (0,)), ((), ())), preferred_element_type=F32)


_G0 = math.sqrt(2.0 / math.pi)
_G1 = 0.044715


def _gelu(x):
    return 0.5 * x * (1.0 + jnp.tanh(_G0 * (x + _G1 * x * x * x)))


def _gelu_grad(x):
    x2 = x * x
    t = jnp.tanh(_G0 * (x + _G1 * x * x2))
    return 0.5 * (1.0 + t) + 0.5 * x * (1.0 - t * t) * _G0 * (1.0 + 3.0 * _G1 * x2)


def _sigmoid(x):
    return 0.5 * jnp.tanh(0.5 * x) + 0.5


def _one_minus_exp2(x):
    t = jnp.tanh(x)
    return (-2.0 * t) / (1.0 - t)


def _softplus(z):
    e = jnp.exp(-jnp.abs(z))
    u = 1.0 + e
    l1p = jnp.where(u == 1.0, e, jnp.log(u) * e / (u - 1.0))
    return jnp.maximum(z, 0.0) + l1p


def _adamw_math(w, g, m, v):
    m2 = ADAM_B1 * m + (1.0 - ADAM_B1) * g
    v2 = ADAM_B2 * v + (1.0 - ADAM_B2) * (g * g)
    m_hat = m2 / ADAM_C1
    v_hat = v2 / ADAM_C2
    delta = -ADAM_LR * (m_hat / (jnp.sqrt(v_hat) + ADAM_EPS) + ADAM_WD * w)
    return delta, m2, v2


def _rms_parts(xv):
    r = lax.rsqrt(jnp.mean(xv * xv, axis=-1, keepdims=True) + EPS)
    return r, xv * r


def _row_fetches(hbm_refs, bufs, sems, i, rows):
    return [pltpu.make_async_copy(h.at[pl.ds(i * rows, rows), :], b, sems.at[n])
            for n, (h, b) in enumerate(zip(hbm_refs, bufs))]


def _modulated_norm(x_ref, v_ref, row_g, row_scale, row_shift, h_ref, rows):
    g, scale, shift = v_ref[row_g:row_g + 1, :], v_ref[row_scale:row_scale + 1, :], v_ref[row_shift:row_shift + 1, :]

    def sub(rb, carry):
        rs = pl.ds(pl.multiple_of(rb * SUB, SUB), SUB)
        _, xh = _rms_parts(x_ref[rs, :])
        h_ref[rs, :] = ((xh * g) * (1.0 + scale) + shift).astype(BF16)
        return carry

    lax.fori_loop(0, rows // SUB, sub, 0)


def inproj_fwd(x, vecs, w_in):
    s = x.shape[0]
    tm = min(1024, s)
    per = IN_SHARD // IN_TILE

    def body(x_ref, v_ref, w_ref, proj_ref, h_ref):
        @pl.when(pl.program_id(1) == 0)
        def _():
            _modulated_norm(x_ref, v_ref, V_G1, V_SCALE1, V_SHIFT1, h_ref, tm)
        proj_ref[...] = _dot(h_ref[...], w_ref[...]).astype(BF16)

    return pl.pallas_call(
        body, name="inproj_fwd", grid=(s // tm, IN_W // IN_TILE),
        in_specs=[pl.BlockSpec((tm, D), lambda i, j: (i, 0)),
                  pl.BlockSpec((16, D), lambda i, j: (0, 0)),
                  pl.BlockSpec((None, D, IN_TILE), lambda i, j: (j // per, 0, j % per))],
        out_specs=[pl.BlockSpec((tm, IN_TILE), lambda i, j: (i, j)),
                   pl.BlockSpec((tm, D), lambda i, j: (i, 0))],
        out_shape=[SDS((s, IN_W), BF16), SDS((s, D), BF16)],
        compiler_params=_cp(2))(x, vecs, w_in)


def _lru_block_fwd(xbuf, lv_ref, wa_ref, wx_ref, b, t, first):
    cs = slice(b * 128, (b + 1) * 128)
    x0 = xbuf[pl.ds(8, t), cs]
    x1 = xbuf[pl.ds(7, t), cs]
    x2 = xbuf[pl.ds(6, t), cs]
    x3 = xbuf[pl.ds(5, t), cs]
    xc = (lv_ref[L_CB:L_CB + 1, cs] + lv_ref[3:4, cs] * x0 + lv_ref[2:3, cs] * x1
          + lv_ref[1:2, cs] * x2 + lv_ref[0:1, cs] * x3)
    xcb = xc.astype(BF16)
    r = _sigmoid(_dot(xcb, wa_ref[b]) + lv_ref[L_BA:L_BA + 1, cs])
    ig = _sigmoid(_dot(xcb, wx_ref[b]) + lv_ref[L_BX:L_BX + 1, cs])
    sp = _softplus(-lv_ref[L_LAM:L_LAM + 1, cs])
    log_a = (-LRU_C) * r * sp
    a = jnp.exp(log_a)
    mult = jnp.where(first, 1.0, jnp.sqrt(_one_minus_exp2(log_a)))
    return (x0, x1, x2, x3), xc, xcb, r, ig, sp, a, mult


def lru_fwd(proj, lvec, wa, wx):
    s = proj.shape[0]
    t = min(256, s)

    def body(lx_ref, gate_ref, lv_ref, wa_ref, wx_ref, ya_ref, rec_ref, xbuf, a_s, u_s, hc):
        i = pl.program_id(0)

        @pl.when(i == 0)
        def _():
            xbuf[pl.ds(0, 8), :] = jnp.zeros((8, D), F32)
            hc[...] = jnp.zeros((8, D), F32)

        @pl.when(i > 0)
        def _():
            xbuf[pl.ds(0, 8), :] = xbuf[pl.ds(t, 8), :]

        xbuf[pl.ds(8, t), :] = lx_ref[...].astype(F32)
        first = (lax.broadcasted_iota(jnp.int32, (t, 128), 0) + i * t) == 0
        for b in range(N_LRU_BLOCKS):
            cs = slice(b * 128, (b + 1) * 128)
            _, xc, _, _, ig, _, a, mult = _lru_block_fwd(xbuf, lv_ref, wa_ref, wx_ref, b, t, first)
            a_s[:, cs] = a
            u_s[:, cs] = mult * (ig * xc)

        def step(tt, h):
            h = a_s[pl.ds(tt, 1), :] * h + u_s[pl.ds(tt, 1), :]
            rec_ref[pl.ds(tt, 1), :] = h
            return h

        hc[0:1, :] = lax.fori_loop(0, t, step, hc[0:1, :], unroll=8)
        for b in range(N_LRU_BLOCKS):
            cs = slice(b * 128, (b + 1) * 128)
            ya_ref[:, cs] = (rec_ref[:, cs] * _gelu(gate_ref[:, cs].astype(F32))).astype(BF16)

    return pl.pallas_call(
        body, name="lru_fwd", grid=(s // t,),
        in_specs=[pl.BlockSpec((t, D), lambda i: (i, OFF_LRU // D)),
                  pl.BlockSpec((t, D), lambda i: (i, OFF_GATE // D)),
                  pl.BlockSpec((8, D), lambda i: (0, 0)),
                  pl.BlockSpec((N_LRU_BLOCKS, 128, 128), lambda i: (0, 0, 0)),
                  pl.BlockSpec((N_LRU_BLOCKS, 128, 128), lambda i: (0, 0, 0))],
        out_specs=[pl.BlockSpec((t, D), lambda i: (i, 0)), pl.BlockSpec((t, D), lambda i: (i, 0))],
        out_shape=[SDS((s, D), BF16), SDS((s, D), F32)],
        scratch_shapes=[pltpu.VMEM((t + 8, D), F32), pltpu.VMEM((t, D), F32), pltpu.VMEM((t, D), F32),
                        pltpu.VMEM((8, D), F32)],
        compiler_params=_cp(1))(proj, proj, lvec, wa, wx)


def t5_bucket_table():
    qi = np.arange(BLOCK)[:, None]
    ki = np.arange(2 * BLOCK)[None, :]
    rel = qi + BLOCK - ki
    relc = np.maximum(rel, 0)
    max_exact = N_BUCKETS // 2
    relf = np.maximum(relc, 1).astype(np.float32)
    large = max_exact + (np.log(relf / np.float32(max_exact)) / np.float32(math.log(MAX_DISTANCE / max_exact))
                         * np.float32(N_BUCKETS - max_exact)).astype(np.int32)
    large = np.minimum(large, N_BUCKETS - 1)
    bucket = np.where(relc < max_exact, relc, large)
    bucket = np.where((rel >= 0) & (rel < BLOCK), bucket, -1)
    return jnp.asarray(bucket.reshape(1, BLOCK * 2 * BLOCK), jnp.int32)


def bias_band(rel_bias_t, buckets):
    n = BLOCK * 2 * BLOCK
    tn = 4096

    def body(bk_ref, rb_ref, o_ref):
        row = lax.broadcasted_iota(jnp.int32, (N_BUCKETS, tn), 0)
        oh = jnp.where(row == bk_ref[...], 1.0, 0.0).astype(BF16)
        rb = rb_ref[...]
        p0 = rb.astype(BF16)
        r1 = rb - p0.astype(F32)
        p1 = r1.astype(BF16)
        p2 = (r1 - p1.astype(F32)).astype(BF16)
        o_ref[...] = _dot(p0, oh) + _dot(p1, oh) + _dot(p2, oh)

    return pl.pallas_call(
        body, name="bias_band", grid=(n // tn,),
        in_specs=[pl.BlockSpec((1, tn), lambda i: (0, i)), pl.BlockSpec((N_HEADS, N_BUCKETS), lambda i: (0, 0))],
        out_specs=pl.BlockSpec((N_HEADS, tn), lambda i: (0, i)),
        out_shape=SDS((N_HEADS, n), F32), compiler_params=_cp(1))(buckets, rel_bias_t)


def bias_band_bwd(dband, buckets):
    n = BLOCK * 2 * BLOCK
    tn = 4096

    def body(bk_ref, d_ref, o_ref):
        @pl.when(pl.program_id(0) == 0)
        def _():
            o_ref[...] = jnp.zeros_like(o_ref)
        row = lax.broadcasted_iota(jnp.int32, (N_BUCKETS, tn), 0)
        oh = jnp.where(row == bk_ref[...], 1.0, 0.0).astype(BF16)
        dv = d_ref[...]
        p0 = dv.astype(BF16)
        r1 = dv - p0.astype(F32)
        p1 = r1.astype(BF16)
        p2 = (r1 - p1.astype(F32)).astype(BF16)
        o_ref[...] += _dot_nt(oh, p0) + _dot_nt(oh, p1) + _dot_nt(oh, p2)

    return pl.pallas_call(
        body, name="bias_band_bwd", grid=(n // tn,),
        in_specs=[pl.BlockSpec((1, tn), lambda i: (0, i)), pl.BlockSpec((N_HEADS, tn), lambda i: (0, i))],
        out_specs=pl.BlockSpec((N_BUCKETS, N_HEADS), lambda i: (0, 0)),
        out_shape=SDS((N_BUCKETS, N_HEADS), F32), compiler_params=_cp(1))(buckets, dband)


def _dup_half(band, which):
    lane = lax.broadcasted_iota(jnp.int32, band.shape, 1)
    rolled = pltpu.roll(band, 64, 1)
    keep = (lane < 64) if which == 0 else (lane >= 64)
    return jnp.where(keep, band, rolled)


def _attn_probs(scores, bias, sink, valid):
    sc = jnp.where(valid, scores * SCALE + bias, NEG_INF)
    m = jnp.maximum(jnp.max(sc, axis=-1, keepdims=True), sink)
    e = jnp.exp(sc - m)
    es = jnp.exp(sink - m)
    inv = 1.0 / (jnp.sum(e, axis=-1, keepdims=True) + es)
    return e * inv, es * inv


def _stack_heads(src_ref, kv, dst):
    lane = lax.broadcasted_iota(jnp.int32, (BLOCK, 128), 1)
    for jj in range(4):
        slab = src_ref[:, (4 * kv + jj) * 128:(4 * kv + jj + 1) * 128]
        for hh in range(2):
            keep = (lane < 64) if hh == 0 else (lane >= 64)
            dst[pl.ds((2 * jj + hh) * BLOCK, BLOCK), :] = jnp.where(keep, slab, jnp.zeros_like(slab))


def _unstack_heads(stacked, dst_ref, kv):
    lane = lax.broadcasted_iota(jnp.int32, (BLOCK, 128), 1)
    for jj in range(4):
        lo = stacked[(2 * jj) * BLOCK:(2 * jj + 1) * BLOCK]
        hi = stacked[(2 * jj + 1) * BLOCK:(2 * jj + 2) * BLOCK]
        dst_ref[:, (4 * kv + jj) * 128:(4 * kv + jj + 1) * 128] = jnp.where(lane < 64, lo, hi).astype(dst_ref.dtype)


def _band_valid(n):
    qi = lax.broadcasted_iota(jnp.int32, (BLOCK, 2 * BLOCK), 0)
    ki = lax.broadcasted_iota(jnp.int32, (BLOCK, 2 * BLOCK), 1)
    rel = qi + BLOCK - ki
    return (rel >= 0) & (rel < BLOCK) & ((ki >= BLOCK) | (n > 0))


def _kv_bands(prev_ref, cur_ref):
    band = jnp.concatenate([prev_ref[...].astype(F32), cur_ref[...].astype(F32)], axis=0)
    return [_dup_half(band, 0).astype(BF16), _dup_half(band, 1).astype(BF16)]


def attn_fwd(proj, band, sinks):
    s = proj.shape[0]
    nb = s // BLOCK
    qw = 1024

    def body(sk_ref, q_ref, kp_ref, kc_ref, vp_ref, vc_ref, b_ref, o_ref, qs_buf, s_buf, p_buf):
        n = pl.program_id(0)
        gp = pl.program_id(1)
        valid = _band_valid(n)
        kks = _kv_bands(kp_ref, kc_ref)
        vvs = _kv_bands(vp_ref, vc_ref)
        for kv in range(2):
            _stack_heads(q_ref, kv, qs_buf)
            s_buf[...] = _dot_nt(qs_buf[...], kks[kv])
            for hq in range(8):
                hl = 8 * kv + hq
                rows = pl.ds(hq * BLOCK, BLOCK)
                p, _ = _attn_probs(s_buf[rows, :], b_ref[hl], sk_ref[gp * 16 + hl], valid)
                p_buf[rows, :] = p.astype(BF16)
            _unstack_heads(_dot(p_buf[...], vvs[kv]), o_ref, kv)

    kb, vb = OFF_K // 128, OFF_V // 128
    return pl.pallas_call(
        body, name="attn_fwd", grid=(nb, 2),
        in_specs=[pl.BlockSpec(memory_space=pltpu.SMEM),
                  pl.BlockSpec((BLOCK, qw), lambda n, g: (n, OFF_Q // qw + g)),
                  pl.BlockSpec((BLOCK, 128), lambda n, g: (jnp.maximum(n - 1, 0), kb + g)),
                  pl.BlockSpec((BLOCK, 128), lambda n, g: (n, kb + g)),
                  pl.BlockSpec((BLOCK, 128), lambda n, g: (jnp.maximum(n - 1, 0), vb + g)),
                  pl.BlockSpec((BLOCK, 128), lambda n, g: (n, vb + g)),
                  pl.BlockSpec((16, BLOCK, 2 * BLOCK), lambda n, g: (g, 0, 0))],
        out_specs=pl.BlockSpec((BLOCK, qw), lambda n, g: (n, g)),
        out_shape=SDS((s, D), BF16),
        scratch_shapes=[pltpu.VMEM((8 * BLOCK, 128), BF16), pltpu.VMEM((8 * BLOCK, 2 * BLOCK), F32),
                        pltpu.VMEM((8 * BLOCK, 2 * BLOCK), BF16)],
        compiler_params=_cp(2))(sinks, proj, proj, proj, proj, proj, band)


def merge_fwd(ya, att, w_lru_out, w_attn_out, proj):
    s = ya.shape[0]
    tm, tn = min(1024, s), 512

    def body(ya_ref, at_ref, wl_ref, wt_ref, ga_ref, gb_ref, yab_ref, mg_ref):
        y_a = _dot(ya_ref[...], wl_ref[...])
        y_b = _dot(at_ref[...], wt_ref[...])
        yab_ref[0] = y_a.astype(BF16)
        yab_ref[1] = y_b.astype(BF16)
        mg_ref[...] = (_sigmoid(ga_ref[...].astype(F32)) * y_a + _sigmoid(gb_ref[...].astype(F32)) * y_b).astype(BF16)

    return pl.pallas_call(
        body, name="merge_fwd", grid=(s // tm, D // tn),
        in_specs=[pl.BlockSpec((tm, D), lambda i, j: (i, 0)), pl.BlockSpec((tm, D), lambda i, j: (i, 0)),
                  pl.BlockSpec((D, tn), lambda i, j: (0, j)), pl.BlockSpec((D, tn), lambda i, j: (0, j)),
                  pl.BlockSpec((tm, tn), lambda i, j: (i, OFF_GA // tn + j)),
                  pl.BlockSpec((tm, tn), lambda i, j: (i, OFF_GB // tn + j))],
        out_specs=[pl.BlockSpec((2, tm, tn), lambda i, j: (0, i, j)), pl.BlockSpec((tm, tn), lambda i, j: (i, j))],
        out_shape=[SDS((2, s, D), BF16), SDS((s, D), BF16)],
        compiler_params=_cp(2))(ya, att, w_lru_out, w_attn_out, proj, proj)


def outproj_fwd(merged, w_out, x, vecs):
    s = x.shape[0]
    tm, tn = min(1024, s), 512

    def body(m_ref, w_ref, x_ref, v_ref, x1_ref, o1_ref):
        o1 = _dot(m_ref[...], w_ref[...])
        o1_ref[...] = o1.astype(BF16)
        x1_ref[...] = x_ref[...] + v_ref[V_GATE1:V_GATE1 + 1, :] * o1

    return pl.pallas_call(
        body, name="outproj_fwd", grid=(s // tm, D // tn),
        in_specs=[pl.BlockSpec((tm, D), lambda i, j: (i, 0)), pl.BlockSpec((D, tn), lambda i, j: (0, j)),
                  pl.BlockSpec((tm, tn), lambda i, j: (i, j)), pl.BlockSpec((16, tn), lambda i, j: (0, j))],
        out_specs=[pl.BlockSpec((tm, tn), lambda i, j: (i, j)), pl.BlockSpec((tm, tn), lambda i, j: (i, j))],
        out_shape=[SDS((s, D), F32), SDS((s, D), BF16)],
        compiler_params=_cp(2))(merged, w_out, x, vecs)


def ff1_fwd(x1, vecs, w_ff1):
    s = x1.shape[0]
    tm, tn = min(1024, s), 512
    per = D // tn

    def body(x_ref, v_ref, w_ref, f_ref, h_ref):
        @pl.when(pl.program_id(1) == 0)
        def _():
            _modulated_norm(x_ref, v_ref, V_G2, V_SCALE2, V_SHIFT2, h_ref, tm)
        f_ref[...] = _dot(h_ref[...], w_ref[...]).astype(BF16)

    return pl.pallas_call(
        body, name="ff1_fwd", grid=(s // tm, D_FF // tn),
        in_specs=[pl.BlockSpec((tm, D), lambda i, j: (i, 0)), pl.BlockSpec((16, D), lambda i, j: (0, 0)),
                  pl.BlockSpec((None, D, tn), lambda i, j: (j // per, 0, j % per))],
        out_specs=[pl.BlockSpec((tm, tn), lambda i, j: (i, j)), pl.BlockSpec((tm, D), lambda i, j: (i, 0))],
        out_shape=[SDS((s, D_FF), BF16), SDS((s, D), BF16)],
        compiler_params=_cp(2))(x1, vecs, w_ff1)


def ff2_loss(f, w_ff2, x1, tgt, vecs):
    s = x1.shape[0]
    tm, tk = min(512, s), 1024
    nk = D_FF // tk

    def body(f_ref, w_ref, x1_hbm, t_hbm, v_ref, dx2_ref, do2_ref, sums_ref, loss_ref, acc, x1_ref, t_ref, sems):
        i, k = pl.program_id(0), pl.program_id(1)
        fetches = _row_fetches((x1_hbm, t_hbm), (x1_ref, t_ref), sems, i, tm)

        @pl.when((i == 0) & (k == 0))
        def _():
            sums_ref[...] = jnp.zeros_like(sums_ref)
            loss_ref[...] = jnp.zeros_like(loss_ref)

        @pl.when(k == 0)
        def _():
            acc[...] = jnp.zeros_like(acc)
            for cp in fetches:
                cp.start()

        fv = jnp.maximum(f_ref[...].astype(F32), 0.0)
        acc[...] += _dot((fv * fv).astype(BF16), w_ref[...])

        @pl.when(k == nk - 1)
        def _():
            for cp in fetches:
                cp.wait()
            gate2 = v_ref[V_GATE2:V_GATE2 + 1, :]
            g3 = v_ref[V_G3:V_G3 + 1, :]

            def sub(rb, carry):
                rs = pl.ds(pl.multiple_of(rb * SUB, SUB), SUB)
                o2 = acc[rs, :]
                x2 = x1_ref[rs, :] + gate2 * o2
                r3, xh = _rms_parts(x2)
                e = xh * g3 - t_ref[rs, :]
                loss_ref[...] += (0.5 / D) * jnp.sum(e * e)
                dy = e * (1.0 / D)
                sums_ref[0:1, :] += jnp.sum(dy * xh, axis=0, keepdims=True)
                dxh = dy * g3
                dx2 = r3 * (dxh - xh * jnp.mean(dxh * xh, axis=-1, keepdims=True))
                sums_ref[1:2, :] += jnp.sum(dx2 * o2, axis=0, keepdims=True)
                dx2_ref[rs, :] = dx2
                do2_ref[rs, :] = (dx2 * gate2).astype(BF16)
                return carry

            lax.fori_loop(0, tm // SUB, sub, 0)

    return pl.pallas_call(
        body, name="ff2_loss", grid=(s // tm, nk),
        in_specs=[pl.BlockSpec((tm, tk), lambda i, k: (i, k)), pl.BlockSpec((tk, D), lambda i, k: (k, 0)),
                  pl.BlockSpec(memory_space=pl.ANY), pl.BlockSpec(memory_space=pl.ANY),
                  pl.BlockSpec((16, D), lambda i, k: (0, 0))],
        out_specs=[pl.BlockSpec((tm, D), lambda i, k: (i, 0)), pl.BlockSpec((tm, D), lambda i, k: (i, 0)),
                   pl.BlockSpec((8, D), lambda i, k: (0, 0)), pl.BlockSpec((8, 128), lambda i, k: (0, 0))],
        out_shape=[SDS((s, D), F32), SDS((s, D), BF16), SDS((8, D), F32), SDS((8, 128), F32)],
        scratch_shapes=[pltpu.VMEM((tm, D), F32), pltpu.VMEM((tm, D), F32), pltpu.VMEM((tm, D), F32),
                        pltpu.SemaphoreType.DMA((2,))],
        compiler_params=_cp(2))(f, w_ff2, x1, tgt, vecs)


def ff2_bwd(do2, w_ff2, f):
    s = do2.shape[0]
    tm, tn = min(1024, s), 512

    def body(d_ref, w_ref, f_ref, o_ref):
        dff = _dot_nt(d_ref[...], w_ref[...])
        o_ref[...] = (dff * (2.0 * jnp.maximum(f_ref[...].astype(F32), 0.0))).astype(BF16)

    return pl.pallas_call(
        body, name="ff2_bwd", grid=(s // tm, D_FF // tn),
        in_specs=[pl.BlockSpec((tm, D), lambda i, j: (i, 0)), pl.BlockSpec((tn, D), lambda i, j: (j, 0)),
                  pl.BlockSpec((tm, tn), lambda i, j: (i, j))],
        out_specs=pl.BlockSpec((tm, tn), lambda i, j: (i, j)),
        out_shape=SDS((s, D_FF), BF16), compiler_params=_cp(2))(do2, w_ff2, f)


def weight_grad(name, a, b, tn, out_shape, out_block, out_map, relu2=False):
    s, m = a.shape
    n = b.shape[1]
    tm = WG_TM
    chunk = min(1024, s)
    nch = s // chunk

    def body(a_hbm, b_ref, o_ref, a_buf, at_s, sem):
        i = pl.program_id(0)

        @pl.when(pl.program_id(1) == 0)
        def _():
            def fetch(ch):
                return pltpu.make_async_copy(a_hbm.at[pl.ds(ch * chunk, chunk), pl.ds(i * tm, tm)],
                                             a_buf.at[ch % 2], sem.at[ch % 2])
            fetch(0).start()
            for ch in range(nch):
                if ch + 1 < nch:
                    fetch(ch + 1).start()
                fetch(ch).wait()
                av = a_buf[ch % 2]
                if relu2:
                    fv = jnp.maximum(av.astype(F32), 0.0)
                    av = (fv * fv).astype(BF16)
                at_s[:, ch * chunk:(ch + 1) * chunk] = av.T

        o_ref[...] = _dot(at_s[...], b_ref[...]).astype(BF16)

    return pl.pallas_call(
        body, name=name, grid=(m // tm, n // tn),
        in_specs=[pl.BlockSpec(memory_space=pl.ANY), pl.BlockSpec((s, tn), lambda i, j: (0, j))],
        out_specs=pl.BlockSpec(out_block, lambda i, j: out_map(i, j)),
        out_shape=SDS(out_shape, BF16),
        scratch_shapes=[pltpu.VMEM((2, chunk, tm), BF16), pltpu.VMEM((tm, s), BF16), pltpu.SemaphoreType.DMA((2,))],
        compiler_params=_cp(2))(a, b)


def ff1_bwd(df, w_ff1, x1, dx2, o1, vecs):
    s = df.shape[0]
    tm, tk = min(512, s), 1024
    nk = D_FF // tk
    per = D // tk

    def body(d_ref, w_ref, x1_hbm, dx2_hbm, o1_hbm, v_ref, dx1_ref, do1_ref, sums_ref, acc, x1_ref, dx2_ref, o1_ref, sems):
        i, k = pl.program_id(0), pl.program_id(1)
        fetches = _row_fetches((x1_hbm, dx2_hbm, o1_hbm), (x1_ref, dx2_ref, o1_ref), sems, i, tm)

        @pl.when((i == 0) & (k == 0))
        def _():
            sums_ref[...] = jnp.zeros_like(sums_ref)

        @pl.when(k == 0)
        def _():
            acc[...] = jnp.zeros_like(acc)
            for cp in fetches:
                cp.start()

        acc[...] += _dot_nt(d_ref[...], w_ref[...])

        @pl.when(k == nk - 1)
        def _():
            for cp in fetches:
                cp.wait()
            g2 = v_ref[V_G2:V_G2 + 1, :]
            scale2 = v_ref[V_SCALE2:V_SCALE2 + 1, :]
            gate1 = v_ref[V_GATE1:V_GATE1 + 1, :]

            def sub(rb, carry):
                rs = pl.ds(pl.multiple_of(rb * SUB, SUB), SUB)
                dh = acc[rs, :]
                r2, xh = _rms_parts(x1_ref[rs, :])
                sums_ref[0:1, :] += jnp.sum(dh, axis=0, keepdims=True)
                sums_ref[1:2, :] += jnp.sum(dh * (xh * g2), axis=0, keepdims=True)
                dxn = dh * (1.0 + scale2)
                sums_ref[2:3, :] += jnp.sum(dxn * xh, axis=0, keepdims=True)
                dxh = dxn * g2
                dx1 = dx2_ref[rs, :] + r2 * (dxh - xh * jnp.mean(dxh * xh, axis=-1, keepdims=True))
                sums_ref[3:4, :] += jnp.sum(dx1 * o1_ref[rs, :].astype(F32), axis=0, keepdims=True)
                dx1_ref[rs, :] = dx1
                do1_ref[rs, :] = (dx1 * gate1).astype(BF16)
                return carry

            lax.fori_loop(0, tm // SUB, sub, 0)

    return pl.pallas_call(
        body, name="ff1_bwd", grid=(s // tm, nk),
        in_specs=[pl.BlockSpec((tm, tk), lambda i, k: (i, k)),
                  pl.BlockSpec((None, D, tk), lambda i, k: (k // per, 0, k % per)),
                  pl.BlockSpec(memory_space=pl.ANY), pl.BlockSpec(memory_space=pl.ANY),
                  pl.BlockSpec(memory_space=pl.ANY), pl.BlockSpec((16, D), lambda i, k: (0, 0))],
        out_specs=[pl.BlockSpec((tm, D), lambda i, k: (i, 0)), pl.BlockSpec((tm, D), lambda i, k: (i, 0)),
                   pl.BlockSpec((8, D), lambda i, k: (0, 0))],
        out_shape=[SDS((s, D), F32), SDS((s, D), BF16), SDS((8, D), F32)],
        scratch_shapes=[pltpu.VMEM((tm, D), F32), pltpu.VMEM((tm, D), F32), pltpu.VMEM((tm, D), F32),
                        pltpu.VMEM((tm, D), BF16), pltpu.SemaphoreType.DMA((3,))],
        compiler_params=_cp(2))(df, w_ff1, x1, dx2, o1, vecs)


def outproj_bwd(do1, w_out, yab, proj):
    s = do1.shape[0]
    tm, tn = min(1024, s), 512
    per = D // tn

    def body(d_ref, w_ref, y_ref, g_ref, dy_ref, dp_ref):
        dm = _dot_nt(d_ref[...], w_ref[...])
        sg = _sigmoid(g_ref[...].astype(F32))
        dy_ref[...] = (dm * sg).astype(BF16)
        dp_ref[...] = (dm * y_ref[...].astype(F32) * sg * (1.0 - sg)).astype(BF16)

    return pl.pallas_call(
        body, name="outproj_bwd", grid=(s // tm, 2 * per),
        in_specs=[pl.BlockSpec((tm, D), lambda i, j: (i, 0)), pl.BlockSpec((tn, D), lambda i, j: (j % per, 0)),
                  pl.BlockSpec((None, tm, tn), lambda i, j: (j // per, i, j % per)),
                  pl.BlockSpec((tm, tn), lambda i, j: (i, OFF_GA // tn + j))],
        out_specs=[pl.BlockSpec((None, tm, tn), lambda i, j: (j // per, i, j % per)),
                   pl.BlockSpec((tm, tn), lambda i, j: (i, OFF_GA // tn + j))],
        out_shape=[SDS((2, s, D), BF16), SDS((s, IN_W), BF16)],
        compiler_params=_cp(2))(do1, w_out, yab, proj)


def lruout_bwd(dyab, w_lru_out, rec, proj, dproj):
    s = rec.shape[0]
    tm, tn = min(1024, s), 512

    def body(d_ref, w_ref, r_ref, g_ref, dp_in, dr_ref, dp_ref):
        dya = _dot_nt(d_ref[...], w_ref[...])
        gate = g_ref[...].astype(F32)
        dr_ref[...] = dya * _gelu(gate)
        dp_ref[...] = (dya * r_ref[...] * _gelu_grad(gate)).astype(BF16)

    return pl.pallas_call(
        body, name="lruout_bwd", grid=(s // tm, D // tn),
        in_specs=[pl.BlockSpec((None, tm, D), lambda i, j: (0, i, 0)), pl.BlockSpec((tn, D), lambda i, j: (j, 0)),
                  pl.BlockSpec((tm, tn), lambda i, j: (i, j)),
                  pl.BlockSpec((tm, tn), lambda i, j: (i, OFF_GATE // tn + j)),
                  pl.BlockSpec(memory_space=pl.ANY)],
        out_specs=[pl.BlockSpec((tm, tn), lambda i, j: (i, j)),
                   pl.BlockSpec((tm, tn), lambda i, j: (i, OFF_GATE // tn + j))],
        out_shape=[SDS((s, D), F32), SDS((s, IN_W), BF16)],
        input_output_aliases={4: 1},
        compiler_params=_cp(2))(dyab, w_lru_out, rec, proj, dproj)


def attnout_bwd(dyab, w_attn_out):
    s = dyab.shape[1]
    tm, tn = min(1024, s), 512

    def body(d_ref, w_ref, o_ref):
        o_ref[...] = _dot_nt(d_ref[...], w_ref[...]).astype(BF16)

    return pl.pallas_call(
        body, name="attnout_bwd", grid=(s // tm, D // tn),
        in_specs=[pl.BlockSpec((None, tm, D), lambda i, j: (1, i, 0)), pl.BlockSpec((tn, D), lambda i, j: (j, 0))],
        out_specs=pl.BlockSpec((tm, tn), lambda i, j: (i, j)),
        out_shape=SDS((s, D), BF16), compiler_params=_cp(2))(dyab, w_attn_out)


def attn_bwd(proj, band, sinks, datt, dproj):
    s = proj.shape[0]
    nb = s // BLOCK
    qw = 1024

    def body(sk_ref, q_ref, kp_ref, kc_ref, vp_ref, vc_ref, b_ref, do_ref, dp_in,
             dq_ref, dkb_ref, dvb_ref, db_ref, ds_ref, qs_buf, dos_buf, s_buf, dp_buf, p_buf, dsc_buf):
        gp = pl.program_id(0)
        n = pl.program_id(1)

        @pl.when(n == 0)
        def _():
            db_ref[...] = jnp.zeros_like(db_ref)
            ds_ref[...] = jnp.zeros_like(ds_ref)

        valid = _band_valid(n)
        kks = _kv_bands(kp_ref, kc_ref)
        vvs = _kv_bands(vp_ref, vc_ref)
        lane_b = lax.broadcasted_iota(jnp.int32, (2 * BLOCK, 128), 1)
        dks, dvs = [], []
        for kv in range(2):
            _stack_heads(q_ref, kv, qs_buf)
            _stack_heads(do_ref, kv, dos_buf)
            s_buf[...] = _dot_nt(qs_buf[...], kks[kv])
            dp_buf[...] = _dot_nt(dos_buf[...], vvs[kv])
            for hq in range(8):
                hl = 8 * kv + hq
                rows = pl.ds(hq * BLOCK, BLOCK)
                p, ps = _attn_probs(s_buf[rows, :], b_ref[hl], sk_ref[gp * 16 + hl], valid)
                dp = dp_buf[rows, :]
                delta = jnp.sum(p * dp, axis=-1, keepdims=True)
                dsc = p * (dp - delta)
                db_ref[hl] += dsc
                ds_ref[hl:hl + 1, :] += jnp.zeros((1, 128), F32) - jnp.sum(ps * delta)
                p_buf[rows, :] = p.astype(BF16)
                dsc_buf[rows, :] = (dsc * SCALE).astype(BF16)
            _unstack_heads(_dot(dsc_buf[...], kks[kv]), dq_ref, kv)
            dk = _dot_tn(dsc_buf[...], qs_buf[...])
            dv = _dot_tn(p_buf[...], dos_buf[...])
            dks.append(dk + pltpu.roll(dk, 64, 1))
            dvs.append(dv + pltpu.roll(dv, 64, 1))
        dkb_ref[...] = jnp.where(lane_b < 64, dks[0], dks[1])
        dvb_ref[...] = jnp.where(lane_b < 64, dvs[0], dvs[1])

    kb, vb = OFF_K // 128, OFF_V // 128
    return pl.pallas_call(
        body, name="attn_bwd", grid=(2, nb),
        in_specs=[pl.BlockSpec(memory_space=pltpu.SMEM),
                  pl.BlockSpec((BLOCK, qw), lambda g, n: (n, OFF_Q // qw + g)),
                  pl.BlockSpec((BLOCK, 128), lambda g, n: (jnp.maximum(n - 1, 0), kb + g)),
                  pl.BlockSpec((BLOCK, 128), lambda g, n: (n, kb + g)),
                  pl.BlockSpec((BLOCK, 128), lambda g, n: (jnp.maximum(n - 1, 0), vb + g)),
                  pl.BlockSpec((BLOCK, 128), lambda g, n: (n, vb + g)),
                  pl.BlockSpec((16, BLOCK, 2 * BLOCK), lambda g, n: (g, 0, 0)),
                  pl.BlockSpec((BLOCK, qw), lambda g, n: (n, g)),
                  pl.BlockSpec(memory_space=pl.ANY)],
        out_specs=[pl.BlockSpec((BLOCK, qw), lambda g, n: (n, OFF_Q // qw + g)),
                   pl.BlockSpec((2 * BLOCK, 128), lambda g, n: (n, g)),
                   pl.BlockSpec((2 * BLOCK, 128), lambda g, n: (n, g)),
                   pl.BlockSpec((16, BLOCK, 2 * BLOCK), lambda g, n: (g, 0, 0)),
                   pl.BlockSpec((16, 128), lambda g, n: (g, 0))],
        out_shape=[SDS((s, IN_W), BF16), SDS((nb * 2 * BLOCK, 256), F32), SDS((nb * 2 * BLOCK, 256), F32),
                   SDS((N_HEADS, BLOCK, 2 * BLOCK), F32), SDS((N_HEADS, 128), F32)],
        input_output_aliases={8: 0},
        scratch_shapes=[pltpu.VMEM((8 * BLOCK, 128), BF16), pltpu.VMEM((8 * BLOCK, 128), BF16),
                        pltpu.VMEM((8 * BLOCK, 2 * BLOCK), F32), pltpu.VMEM((8 * BLOCK, 2 * BLOCK), F32),
                        pltpu.VMEM((8 * BLOCK, 2 * BLOCK), BF16), pltpu.VMEM((8 * BLOCK, 2 * BLOCK), BF16)],
        compiler_params=_cp(2))(sinks, proj, proj, proj, proj, proj, band, datt, dproj)


def dkv_combine(dkb, dvb, dproj):
    nb = dkb.shape[0] // (2 * BLOCK)
    s = nb * BLOCK
    dkb3 = dkb.reshape(nb, 2 * BLOCK, 256)
    dvb3 = dvb.reshape(nb, 2 * BLOCK, 256)

    def body(k1, k2, v1, v2, dp_in, o_ref):
        nxt = jnp.where(pl.program_id(0) < nb - 1, 1.0, 0.0)
        o_ref[:, 0:256] = (k1[...] + nxt * k2[...]).astype(BF16)
        o_ref[:, 256:512] = (v1[...] + nxt * v2[...]).astype(BF16)

    spec1 = pl.BlockSpec((None, BLOCK, 256), lambda m: (m, 1, 0))
    spec2 = pl.BlockSpec((None, BLOCK, 256), lambda m: (jnp.minimum(m + 1, nb - 1), 0, 0))
    return pl.pallas_call(
        body, name="dkv_combine", grid=(nb,),
        in_specs=[spec1, spec2, spec1, spec2, pl.BlockSpec(memory_space=pl.ANY)],
        out_specs=pl.BlockSpec((BLOCK, 512), lambda m: (m, OFF_K // 512)),
        out_shape=SDS((s, IN_W), BF16), input_output_aliases={4: 0},
        compiler_params=_cp(1))(dkb3, dkb3, dvb3, dvb3, dproj)


def lru_bwd(proj, rec, drec, lvec, wa, wx, dproj):
    s = proj.shape[0]
    t = min(256, s)
    nt = s // t

    def body(lx_ref, lxh_ref, rec_ref, rech_ref, dr_ref, lv_ref, wa_ref, wx_ref, dp_in,
             dlx_ref, sums_ref, dwa_ref, dwx_ref,
             xbuf, hbuf, dxbuf, a_s, dh_s, xc_s, r_s, ig_s, mu_s, gc):
        step_i = pl.program_id(0)
        ti = nt - 1 - step_i

        @pl.when(step_i == 0)
        def _():
            sums_ref[...] = jnp.zeros_like(sums_ref)
            dwa_ref[...] = jnp.zeros_like(dwa_ref)
            dwx_ref[...] = jnp.zeros_like(dwx_ref)
            dxbuf[pl.ds(t, 8), :] = jnp.zeros((8, D), F32)
            gc[...] = jnp.zeros((8, D), F32)

        live = jnp.where(ti > 0, 1.0, 0.0)
        xbuf[pl.ds(0, 8), :] = lxh_ref[...].astype(F32)[8:16] * live
        xbuf[pl.ds(8, t), :] = lx_ref[...].astype(F32)
        hbuf[pl.ds(0, 8), :] = rech_ref[...] * live
        hbuf[pl.ds(8, t), :] = rec_ref[...]
        first = (lax.broadcasted_iota(jnp.int32, (t, 128), 0) + ti * t) == 0
        for b in range(N_LRU_BLOCKS):
            cs = slice(b * 128, (b + 1) * 128)
            _, xc, _, r, ig, _, a, mult = _lru_block_fwd(xbuf, lv_ref, wa_ref, wx_ref, b, t, first)
            a_s[:, cs] = a
            xc_s[:, cs] = xc
            r_s[:, cs] = r
            ig_s[:, cs] = ig
            mu_s[:, cs] = mult

        def step(q, g):
            tt = t - 1 - q
            dh = dr_ref[pl.ds(tt, 1), :] + g
            dh_s[pl.ds(tt, 1), :] = dh
            return a_s[pl.ds(tt, 1), :] * dh

        gc[0:1, :] = lax.fori_loop(0, t, step, gc[0:1, :], unroll=8)
        for b in range(N_LRU_BLOCKS):
            cs = slice(b * 128, (b + 1) * 128)
            dh = dh_s[:, cs]
            a = a_s[:, cs]
            xc = xc_s[:, cs]
            r = r_s[:, cs]
            ig = ig_s[:, cs]
            mult = mu_s[:, cs]
            sp = _softplus(-lv_ref[L_LAM:L_LAM + 1, cs])
            lam = lv_ref[L_LAM:L_LAM + 1, cs]
            da = dh * hbuf[pl.ds(7, t), cs]
            dmult = jnp.where(first, 0.0, dh * ig * xc)
            dig = dh * mult * xc
            dxc = dh * mult * ig
            dlog_a = da * a - dmult * (a * a) / mult
            dr = dlog_a * ((-LRU_C) * sp)
            dsp = jnp.sum(dlog_a * ((-LRU_C) * r), axis=0, keepdims=True)
            dza = dr * r * (1.0 - r)
            dzx = dig * ig * (1.0 - ig)
            dzab = dza.astype(BF16)
            dzxb = dzx.astype(BF16)
            xcb = xc.astype(BF16)
            dwa_ref[b] += _dot_tn(xcb, dzab)
            dwx_ref[b] += _dot_tn(xcb, dzxb)
            dxc = dxc + _dot_nt(dzab, wa_ref[b]) + _dot_nt(dzxb, wx_ref[b])
            sums_ref[L_LAM:L_LAM + 1, cs] += dsp * (-jax.nn.sigmoid(-lam))
            sums_ref[L_BA:L_BA + 1, cs] += jnp.sum(dza, axis=0, keepdims=True)
            sums_ref[L_BX:L_BX + 1, cs] += jnp.sum(dzx, axis=0, keepdims=True)
            sums_ref[L_CB:L_CB + 1, cs] += jnp.sum(dxc, axis=0, keepdims=True)
            for kk in range(4):
                sums_ref[kk:kk + 1, cs] += jnp.sum(dxc * xbuf[pl.ds(5 + kk, t), cs], axis=0, keepdims=True)
            dxbuf[pl.ds(0, t), cs] = dxc
            dlx = (lv_ref[3:4, cs] * dxc + lv_ref[2:3, cs] * dxbuf[pl.ds(1, t), cs]
                   + lv_ref[1:2, cs] * dxbuf[pl.ds(2, t), cs] + lv_ref[0:1, cs] * dxbuf[pl.ds(3, t), cs])
            dlx_ref[:, cs] = dlx.astype(BF16)
        dxbuf[pl.ds(t, 8), :] = dxbuf[pl.ds(0, 8), :]

    rev = lambda i: nt - 1 - i
    return pl.pallas_call(
        body, name="lru_bwd", grid=(nt,),
        in_specs=[pl.BlockSpec((t, D), lambda i: (rev(i), 0)),
                  pl.BlockSpec((16, D), lambda i: (jnp.maximum(rev(i) * (t // 16) - 1, 0), 0)),
                  pl.BlockSpec((t, D), lambda i: (rev(i), 0)),
                  pl.BlockSpec((8, D), lambda i: (jnp.maximum(rev(i) * (t // 8) - 1, 0), 0)),
                  pl.BlockSpec((t, D), lambda i: (rev(i), 0)),
                  pl.BlockSpec((8, D), lambda i: (0, 0)),
                  pl.BlockSpec((N_LRU_BLOCKS, 128, 128), lambda i: (0, 0, 0)),
                  pl.BlockSpec((N_LRU_BLOCKS, 128, 128), lambda i: (0, 0, 0)),
                  pl.BlockSpec(memory_space=pl.ANY)],
        out_specs=[pl.BlockSpec((t, D), lambda i: (rev(i), 0)),
                   pl.BlockSpec((8, D), lambda i: (0, 0)),
                   pl.BlockSpec((N_LRU_BLOCKS, 128, 128), lambda i: (0, 0, 0)),
                   pl.BlockSpec((N_LRU_BLOCKS, 128, 128), lambda i: (0, 0, 0))],
        out_shape=[SDS((s, IN_W), BF16), SDS((8, D), F32), SDS((N_LRU_BLOCKS, 128, 128), F32),
                   SDS((N_LRU_BLOCKS, 128, 128), F32)],
        scratch_shapes=[pltpu.VMEM((t + 8, D), F32), pltpu.VMEM((t + 8, D), F32), pltpu.VMEM((t + 8, D), F32)]
        + [pltpu.VMEM((t, D), F32)] * 6 + [pltpu.VMEM((8, D), F32)],
        input_output_aliases={8: 0},
        compiler_params=_cp(1))(proj, proj, rec, rec, drec, lvec, wa, wx, dproj)


def inproj_bwd(dproj, w_in, x, dx1, vecs):
    s = x.shape[0]
    tm, tk = min(512, s), IN_TILE
    nk = IN_W // tk
    per = IN_SHARD // tk

    def body(d_ref, w_ref, x_hbm, dx1_hbm, v_ref, gx_ref, sums_ref, acc, x_ref, dx1_ref, sems):
        i, k = pl.program_id(0), pl.program_id(1)
        fetches = _row_fetches((x_hbm, dx1_hbm), (x_ref, dx1_ref), sems, i, tm)

        @pl.when((i == 0) & (k == 0))
        def _():
            sums_ref[...] = jnp.zeros_like(sums_ref)

        @pl.when(k == 0)
        def _():
            acc[...] = jnp.zeros_like(acc)
            for cp in fetches:
                cp.start()

        acc[...] += _dot_nt(d_ref[...], w_ref[...])

        @pl.when(k == nk - 1)
        def _():
            for cp in fetches:
                cp.wait()
            g1 = v_ref[V_G1:V_G1 + 1, :]
            scale1 = v_ref[V_SCALE1:V_SCALE1 + 1, :]

            def sub(rb, carry):
                rs = pl.ds(pl.multiple_of(rb * SUB, SUB), SUB)
                dh = acc[rs, :]
                r1, xh = _rms_parts(x_ref[rs, :])
                sums_ref[0:1, :] += jnp.sum(dh, axis=0, keepdims=True)
                sums_ref[1:2, :] += jnp.sum(dh * (xh * g1), axis=0, keepdims=True)
                dxn = dh * (1.0 + scale1)
                sums_ref[2:3, :] += jnp.sum(dxn * xh, axis=0, keepdims=True)
                dxh = dxn * g1
                gx_ref[rs, :] = dx1_ref[rs, :] + r1 * (dxh - xh * jnp.mean(dxh * xh, axis=-1, keepdims=True))
                return carry

            lax.fori_loop(0, tm // SUB, sub, 0)

    return pl.pallas_call(
        body, name="inproj_bwd", grid=(s // tm, nk),
        in_specs=[pl.BlockSpec((tm, tk), lambda i, k: (i, k)),
                  pl.BlockSpec((None, D, tk), lambda i, k: (k // per, 0, k % per)),
                  pl.BlockSpec(memory_space=pl.ANY), pl.BlockSpec(memory_space=pl.ANY),
                  pl.BlockSpec((16, D), lambda i, k: (0, 0))],
        out_specs=[pl.BlockSpec((tm, D), lambda i, k: (i, 0)), pl.BlockSpec((8, D), lambda i, k: (0, 0))],
        out_shape=[SDS((s, D), F32), SDS((8, D), F32)],
        scratch_shapes=[pltpu.VMEM((tm, D), F32), pltpu.VMEM((tm, D), F32), pltpu.VMEM((tm, D), F32),
                        pltpu.SemaphoreType.DMA((2,))],
        compiler_params=_cp(2))(dproj, w_in, x, dx1, vecs)


def mod_columns(c16, w_ada, b_cols):
    tn = 512

    def body(c_ref, w_ref, b_ref, o_ref):
        cv = c_ref[...]
        ca = (cv * jax.nn.sigmoid(cv)).astype(BF16)
        o_ref[...] = _dot(ca, w_ref[...].astype(BF16)) + b_ref[...]

    return pl.pallas_call(
        body, name="mod_columns", grid=(ADA_SHARD // tn,),
        in_specs=[pl.BlockSpec((16, D), lambda j: (0, 0)), pl.BlockSpec((D, tn), lambda j: (0, j)),
                  pl.BlockSpec((1, tn), lambda j: (0, j))],
        out_specs=pl.BlockSpec((16, tn), lambda j: (0, j)),
        out_shape=SDS((16, ADA_SHARD), F32), compiler_params=_cp(1))(c16, w_ada, b_cols)


def wada_update(c16, dmod16, w, m, v):
    tm, tn = 512, 512

    def body(c_ref, d_ref, w_ref, m_ref, v_ref, g_out, dl_out, m_out, v_out):
        cv = c_ref[...]
        ca = (cv * jax.nn.sigmoid(cv)).astype(BF16)
        g = _dot_tn(ca, d_ref[...].astype(BF16))
        dl, m2, v2 = _adamw_math(w_ref[...], g, m_ref[...], v_ref[...])
        g_out[...] = g
        dl_out[...] = dl
        m_out[...] = m2
        v_out[...] = v2

    tile = pl.BlockSpec((tm, tn), lambda i, j: (i, j))
    return pl.pallas_call(
        body, name="wada_update", grid=(D // tm, ADA_SHARD // tn),
        in_specs=[pl.BlockSpec((16, tm), lambda i, j: (0, i)), pl.BlockSpec((16, tn), lambda i, j: (0, j)),
                  tile, tile, tile],
        out_specs=[tile] * 4, out_shape=[SDS((D, ADA_SHARD), F32)] * 4,
        compiler_params=_cp(2))(c16, dmod16, w, m, v)


def adamw_big(name, w, mine, theirs, m, v, c_idx):
    r, c = w.shape
    tr = 128
    per = (r // 2) // tr

    def body(c_ref, w_ref, a_ref, b_ref, m_ref, v_ref, g_out, dl_out, m_out, v_out):
        own = (pl.program_id(0) // per) == c_ref[0]
        g = jnp.where(own, a_ref[...], b_ref[...])
        dl, m2, v2 = _adamw_math(w_ref[...], g, m_ref[...], v_ref[...])
        g_out[...] = g
        dl_out[...] = dl
        m_out[...] = m2
        v_out[...] = v2

    tile = pl.BlockSpec((tr, c), lambda i, cr: (i, 0))
    half = pl.BlockSpec((tr, c), lambda i, cr: (i % per, 0))
    gs = pltpu.PrefetchScalarGridSpec(num_scalar_prefetch=1, grid=(r // tr,),
                                      in_specs=[tile, half, half, tile, tile], out_specs=[tile] * 4)
    return pl.pallas_call(body, name=name, grid_spec=gs, out_shape=[SDS((r, c), F32)] * 4,
                          compiler_params=_cp(1))(c_idx, w, mine, theirs, m, v)


def cast_into_slot(name, w, k_idx):
    r, c = w.shape
    tr = 256

    def body(k_ref, w_ref, o_ref):
        o_ref[...] = w_ref[...].astype(BF16)

    gs = pltpu.PrefetchScalarGridSpec(
        num_scalar_prefetch=1, grid=(r // tr,),
        in_specs=[pl.BlockSpec((tr, c), lambda i, kr: (i, 0))],
        out_specs=pl.BlockSpec((None, tr, c), lambda i, kr: (kr[0], i, 0)))
    return pl.pallas_call(body, name=name, grid_spec=gs, out_shape=SDS((N_CHIPS, r, c), BF16),
                          compiler_params=_cp(1))(k_idx, w)


def adamw_small(ws, gs, ms, vs):
    n = len(ws)

    def body(*refs):
        for i in range(n):
            dl, m2, v2 = _adamw_math(refs[i][...], refs[n + i][...], refs[2 * n + i][...], refs[3 * n + i][...])
            refs[4 * n + i][...] = dl
            refs[5 * n + i][...] = m2
            refs[6 * n + i][...] = v2

    vm = pl.BlockSpec(memory_space=pltpu.VMEM)
    shapes = [SDS(w.shape, F32) for w in ws]
    outs = pl.pallas_call(
        body, name="adamw_small", in_specs=[vm] * (4 * n), out_specs=[vm] * (3 * n), out_shape=shapes * 3,
        compiler_params=pltpu.CompilerParams(vmem_limit_bytes=VMEM_LIMIT))(*ws, *gs, *ms, *vs)
    return outs[:n], outs[n:2 * n], outs[2 * n:]


def sum_devices(name, gathered):
    rows = gathered.shape[1]
    tr = min(rows, 128)

    def body(x_ref, o_ref):
        acc = x_ref[0].astype(F32)
        for d in range(1, N_DEV):
            acc = acc + x_ref[d].astype(F32)
        o_ref[...] = acc

    return pl.pallas_call(
        body, name=name, grid=(rows // tr,),
        in_specs=[pl.BlockSpec((N_DEV, tr, D), lambda i: (0, i, 0))],
        out_specs=pl.BlockSpec((tr, D), lambda i: (i, 0)),
        out_shape=SDS((rows, D), F32), compiler_params=_cp(1))(gathered)


def _mesh_pos():
    return lax.axis_index("x"), lax.axis_index("y"), lax.axis_index("c")


def _other_chips(x, y):
    return [(1 - x, y), (x, 1 - y), (1 - x, 1 - y)]


def all_gather_small(name, block):
    m_per, n = block.shape

    def body(x_ref, out_ref, send_sems, recv_sems, local_sem):
        x, y, c = _mesh_pos()
        me, sibling = (x, y, c), (x, y, 1 - c)
        chips = _other_chips(x, y)

        def rows(px, py, pc):
            return out_ref.at[pl.ds((4 * px + 2 * py + pc) * m_per, m_per), :]

        def copy(k, blk, to, src=None):
            return pltpu.make_async_remote_copy(
                src_ref=rows(*blk) if src is None else src, dst_ref=rows(*blk),
                send_sem=send_sems.at[k], recv_sem=recv_sems.at[k], device_id=to, device_id_type=MESH)

        mine = pltpu.make_async_copy(x_ref, rows(*me), local_sem)
        mine.start()
        first = [copy(0, me, sibling, src=x_ref)]
        first += [copy(1 + j, me, (*chip, c), src=x_ref) for j, chip in enumerate(chips)]
        for cp in first:
            cp.start()
        passed = [copy(4 + j, (*chip, c), sibling) for j, chip in enumerate(chips)]
        for j, chip in enumerate(chips):
            copy(1 + j, (*chip, c), me).wait_recv()
            passed[j].start()
        copy(0, sibling, me).wait_recv()
        for j, chip in enumerate(chips):
            copy(4 + j, (*chip, 1 - c), me).wait_recv()
        for cp in first + passed:
            cp.wait_send()
        mine.wait()

    vm = pl.BlockSpec(memory_space=pltpu.VMEM)
    return pl.pallas_call(
        body, name=name, out_shape=SDS((N_DEV * m_per, n), block.dtype), in_specs=[vm], out_specs=vm,
        scratch_shapes=[pltpu.SemaphoreType.DMA((7,)), pltpu.SemaphoreType.DMA((7,)), pltpu.SemaphoreType.DMA],
        compiler_params=pltpu.CompilerParams(vmem_limit_bytes=VMEM_LIMIT))(block)


def sibling_sum(name, grad, other, c_idx):
    _, r, cc = grad.shape
    h = r // 2
    tr = min(256, h)
    g4 = grad.reshape(N_CHIPS, 2, h, cc)

    def body(c_ref, a_ref, b_ref, o_ref):
        o_ref[...] = (a_ref[...].astype(F32) + b_ref[...].astype(F32)).astype(BF16)

    gs = pltpu.PrefetchScalarGridSpec(
        num_scalar_prefetch=1, grid=(N_CHIPS, h // tr),
        in_specs=[pl.BlockSpec((None, None, tr, cc), lambda s, i, cr: (s, cr[0], i, 0)),
                  pl.BlockSpec((None, tr, cc), lambda s, i, cr: (s, i, 0))],
        out_specs=pl.BlockSpec((None, tr, cc), lambda s, i, cr: (s, i, 0)))
    return pl.pallas_call(body, name=name, grid_spec=gs, out_shape=SDS((N_CHIPS, h, cc), BF16),
                          compiler_params=_cp(2))(c_idx, g4, other)


HBM_SPEC = pl.BlockSpec(memory_space=pltpu.HBM)
SEM_SPEC = pl.BlockSpec(memory_space=pltpu.SEMAPHORE)


def _side_effecting():
    return pltpu.CompilerParams(has_side_effects=pltpu.SideEffectType.DATAFLOW_SIDE_EFFECTING)


def _in_hbm(a):
    return pltpu.with_memory_space_constraint(a, pltpu.HBM)


def gather_start(name, bufs, after):
    n = len(bufs)
    halves = [w.shape[1] // 2 for w in bufs]

    def body(*refs):
        ins = refs[:n]
        send_sems, recv_sems, token = refs[n + 1], refs[n + 2], refs[-1]
        x, y, c = _mesh_pos()
        k = 2 * x + y
        for i in range(n):
            reg = ins[i].at[k, pl.ds(c * halves[i], halves[i]), :]
            for j, chip in enumerate(_other_chips(x, y)):
                pltpu.make_async_remote_copy(src_ref=reg, dst_ref=reg, send_sem=send_sems.at[3 * i + j],
                                             recv_sem=recv_sems.at[3 * i + j], device_id=(*chip, c),
                                             device_id_type=MESH).start()
        token[...] = jnp.zeros_like(token)

    outs = pl.pallas_call(
        body, name=name,
        out_shape=(pltpu.SemaphoreType.DMA((3 * n,)), pltpu.SemaphoreType.DMA((3 * n,)),
                   *[pltpu.HBM(w.shape, w.dtype) for w in bufs], SDS((8, 128), F32)),
        in_specs=[HBM_SPEC] * n + [pl.BlockSpec(memory_space=pl.ANY)],
        out_specs=(SEM_SPEC, SEM_SPEC, *[HBM_SPEC] * n, pl.BlockSpec(memory_space=pltpu.VMEM)),
        input_output_aliases={i: 2 + i for i in range(n)},
        compiler_params=_side_effecting())(*[_in_hbm(w) for w in bufs], after)
    return outs[0], outs[1], list(outs[2:2 + n]), outs[-1]


def gather_wait(name, send_sems, recv_sems, bufs, after):
    n = len(bufs)
    halves = [w.shape[1] // 2 for w in bufs]

    def body(*refs):
        ins = refs[:n]
        send_sems, recv_sems = refs[n], refs[n + 1]
        x, y, c = _mesh_pos()
        k = 2 * x + y
        for i in range(n):
            for j, chip in enumerate(_other_chips(x, y)):
                kj = 2 * chip[0] + chip[1]
                cp = pltpu.make_async_remote_copy(
                    src_ref=ins[i].at[k, pl.ds(c * halves[i], halves[i]), :],
                    dst_ref=ins[i].at[kj, pl.ds(c * halves[i], halves[i]), :],
                    send_sem=send_sems.at[3 * i + j], recv_sem=recv_sems.at[3 * i + j], device_id=(*chip, c),
                    device_id_type=MESH)
                cp.wait_send()
                cp.wait_recv()

    return pl.pallas_call(
        body, name=name, out_shape=[pltpu.HBM(w.shape, w.dtype) for w in bufs],
        in_specs=[HBM_SPEC] * n + [SEM_SPEC, SEM_SPEC, pl.BlockSpec(memory_space=pl.ANY)],
        out_specs=[HBM_SPEC] * n, input_output_aliases={i: i for i in range(n)},
        compiler_params=_side_effecting())(*bufs, send_sems, recv_sems, after)


def gather_forward(name, bufs):
    n = len(bufs)
    halves = [w.shape[1] // 2 for w in bufs]

    def body(*refs):
        outs = refs[n:2 * n]
        send_sems, recv_sems = refs[2 * n:]
        x, y, c = _mesh_pos()
        chips = _other_chips(x, y)

        def copy(i, j, half, to):
            kj = 2 * chips[j][0] + chips[j][1]
            reg = outs[i].at[kj, pl.ds(half * halves[i], halves[i]), :]
            return pltpu.make_async_remote_copy(src_ref=reg, dst_ref=reg, send_sem=send_sems.at[i, j],
                                                recv_sem=recv_sems.at[i, j], device_id=to, device_id_type=MESH)

        cps = [copy(i, j, c, (x, y, 1 - c)) for i in range(n) for j in range(3)]
        for cp in cps:
            cp.start()
        for i in range(n):
            for j in range(3):
                copy(i, j, 1 - c, (x, y, c)).wait_recv()
        for cp in cps:
            cp.wait_send()

    hbm = pl.BlockSpec(memory_space=pl.ANY)
    return pl.pallas_call(
        body, name=name, in_specs=[hbm] * n, out_specs=[hbm] * n,
        out_shape=[SDS(w.shape, w.dtype) for w in bufs], input_output_aliases={i: i for i in range(n)},
        scratch_shapes=[pltpu.SemaphoreType.DMA((n, 3)), pltpu.SemaphoreType.DMA((n, 3))])(*bufs)


def _exchange_plan(kind, srcs, zones):
    x, y, c = _mesh_pos()
    plan = []
    for src, zone in zip(srcs, zones):
        if kind == "chips":
            for j, chip in enumerate(_other_chips(x, y)):
                plan.append((src.at[2 * chip[0] + chip[1]], zone.at[j], (*chip, c)))
        else:
            h = zone.shape[1]
            plan.append((src.at[:, pl.ds((1 - c) * h, h), :], zone, (x, y, 1 - c)))
    return plan


def _landing_zones(kind, srcs):
    if kind == "chips":
        return [lax.empty((3,) + t.shape[1:], t.dtype) for t in srcs]
    return [lax.empty((t.shape[0], t.shape[1] // 2, t.shape[2]), t.dtype) for t in srcs]


def exchange_start(name, kind, srcs):
    n = len(srcs)
    lands = _landing_zones(kind, srcs)
    n_copies = n * (3 if kind == "chips" else 1)

    def body(*refs):
        send_sems, recv_sems, token = refs[2 * n], refs[2 * n + 1], refs[-1]
        for q, (src, dst, dev) in enumerate(_exchange_plan(kind, refs[:n], refs[n:2 * n])):
            pltpu.make_async_remote_copy(src_ref=src, dst_ref=dst, send_sem=send_sems.at[q], recv_sem=recv_sems.at[q],
                                         device_id=dev, device_id_type=MESH).start()
        token[...] = jnp.zeros_like(token)

    outs = pl.pallas_call(
        body, name=name,
        out_shape=(pltpu.SemaphoreType.DMA((n_copies,)), pltpu.SemaphoreType.DMA((n_copies,)),
                   *[pltpu.HBM(t.shape, t.dtype) for t in srcs], *[pltpu.HBM(t.shape, t.dtype) for t in lands],
                   SDS((8, 128), F32)),
        in_specs=[HBM_SPEC] * (2 * n),
        out_specs=(SEM_SPEC, SEM_SPEC, *[HBM_SPEC] * (2 * n), pl.BlockSpec(memory_space=pltpu.VMEM)),
        input_output_aliases={i: 2 + i for i in range(2 * n)},
        compiler_params=_side_effecting())(*[_in_hbm(t) for t in srcs], *[_in_hbm(t) for t in lands])
    return outs[0], outs[1], list(outs[2:2 + n]), list(outs[2 + n:2 + 2 * n]), outs[-1]


def exchange_wait(name, kind, send_sems, recv_sems, srcs, lands, after):
    n = len(srcs)

    def body(*refs):
        send_sems, recv_sems = refs[2 * n], refs[2 * n + 1]
        for q, (src, dst, dev) in enumerate(_exchange_plan(kind, refs[:n], refs[n:2 * n])):
            cp = pltpu.make_async_remote_copy(src_ref=src, dst_ref=dst, send_sem=send_sems.at[q],
                                              recv_sem=recv_sems.at[q], device_id=dev, device_id_type=MESH)
            cp.wait_send()
            cp.wait_recv()

    outs = pl.pallas_call(
        body, name=name, out_shape=[pltpu.HBM(t.shape, t.dtype) for t in srcs + lands],
        in_specs=[HBM_SPEC] * (2 * n) + [SEM_SPEC, SEM_SPEC, pl.BlockSpec(memory_space=pl.ANY)],
        out_specs=[HBM_SPEC] * (2 * n), input_output_aliases={i: i for i in range(2 * n)},
        compiler_params=_side_effecting())(*srcs, *lands, send_sems, recv_sems, after)
    return list(outs[:n]), list(outs[n:])


def chip_sum(name, sums, parts, k_idx):
    _, h, cc = parts.shape
    tr = min(256, h)

    def body(k_ref, own_ref, p_ref, o_ref):
        acc = own_ref[...].astype(F32)
        for s in range(3):
            acc = acc + p_ref[s].astype(F32)
        o_ref[...] = acc

    gs = pltpu.PrefetchScalarGridSpec(
        num_scalar_prefetch=1, grid=(h // tr,),
        in_specs=[pl.BlockSpec((None, tr, cc), lambda i, kr: (kr[0], i, 0)),
                  pl.BlockSpec((3, tr, cc), lambda i, kr: (0, i, 0))],
        out_specs=pl.BlockSpec((tr, cc), lambda i, kr: (i, 0)))
    return pl.pallas_call(body, name=name, grid_spec=gs, out_shape=SDS((h, cc), F32),
                          compiler_params=_cp(1))(k_idx, sums, parts)


def halves_exchange(name, halves):
    n = len(halves)

    def body(*refs):
        ins, outs = refs[:n], refs[n:2 * n]
        send_sems, recv_sems = refs[2 * n:]
        x, y, c = _mesh_pos()
        cps = []
        for i in range(n):
            cp = pltpu.make_async_remote_copy(
                src_ref=ins[i], dst_ref=outs[i], send_sem=send_sems.at[i], recv_sem=recv_sems.at[i],
                device_id=(x, y, 1 - c), device_id_type=MESH)
            cp.start()
            cps.append(cp)
        for cp in cps:
            cp.wait_recv()
        for cp in cps:
            cp.wait_send()

    hbm = pl.BlockSpec(memory_space=pl.ANY)
    return pl.pallas_call(
        body, name=name, in_specs=[hbm] * n, out_specs=[hbm] * n,
        out_shape=[SDS(t.shape, F32) for t in halves],
        scratch_shapes=[pltpu.SemaphoreType.DMA((n,)), pltpu.SemaphoreType.DMA((n,))])(*halves)


def local_step(x, tgt, vecs, lvec, wa, wx, sinks, rel_bias, w_in, rest_weights, hook):
    buckets = t5_bucket_table()
    band = bias_band(rel_bias.T, buckets).reshape(N_HEADS, BLOCK, 2 * BLOCK)

    proj, h = inproj_fwd(x, vecs, w_in)
    ya, rec = lru_fwd(proj, lvec, wa, wx)
    att = attn_fwd(proj, band, sinks)
    w_lru_out, w_attn_out, w_out, w_ff1, w_ff2 = rest_weights(att[:8, :128] + ya[:8, :128])
    w_lru_out2, w_attn_out2, w_out2 = w_lru_out.reshape(D, D), w_attn_out.reshape(D, D), w_out.reshape(D, D)
    w_ff2_2 = w_ff2.reshape(D_FF, D)
    yab, merged = merge_fwd(ya, att, w_lru_out2, w_attn_out2, proj)
    x1, o1 = outproj_fwd(merged, w_out2, x, vecs)
    f, h2 = ff1_fwd(x1, vecs, w_ff1)
    dx2, do2, sums_f, loss = ff2_loss(f, w_ff2_2, x1, tgt, vecs)

    df = ff2_bwd(do2, w_ff2_2, f)
    g_ff2 = weight_grad("dw_ff2", f, do2, 512, (D_FF, D), (WG_TM, 512), lambda i, j: (i, j), relu2=True)
    dx1, do1, sums_2 = ff1_bwd(df, w_ff1, x1, dx2, o1, vecs)
    g_ff1 = weight_grad("dw_ff1", h2, df, 512, (N_CHIPS, D, D), (None, WG_TM, 512), lambda i, j: (j // 4, i, j % 4))
    dyab, dproj = outproj_bwd(do1, w_out2, yab, proj)
    g_out = weight_grad("dw_out", merged, do1, 512, (D, D), (WG_TM, 512), lambda i, j: (i, j))
    drec, dproj = lruout_bwd(dyab, w_lru_out2, rec, proj, dproj)
    g_lru_out = weight_grad("dw_lru_out", ya, dyab[0], 512, (D, D), (WG_TM, 512), lambda i, j: (i, j))
    datt = attnout_bwd(dyab, w_attn_out2)
    g_attn_out = weight_grad("dw_attn_out", att, dyab[1], 512, (D, D), (WG_TM, 512), lambda i, j: (i, j))
    zero = hook("grads_a", [g_lru_out.reshape(N_CHIPS, D // 4, D), g_attn_out.reshape(N_CHIPS, D // 4, D),
                            g_out.reshape(N_CHIPS, D // 4, D), g_ff1, g_ff2.reshape(N_CHIPS, D_FF // 4, D)])
    dproj, dkb, dvb, dband, dsink = attn_bwd(proj, band, sinks + zero, datt, dproj)
    zero = hook("after_attn_bwd", dkb)
    dproj = dkv_combine(dkb, dvb, dproj)
    dproj, sums_l, d_wa, d_wx = lru_bwd(proj, rec, drec, lvec + zero, wa, wx, dproj)
    per = IN_SHARD // IN_TILE
    g_in = weight_grad("dw_in", h, dproj, IN_TILE, (N_CHIPS, D, IN_SHARD), (None, WG_TM, IN_TILE),
                       lambda i, j: (j // per, i, j % per))
    zero = hook("grads_b", [g_in])
    grad_x, sums_1 = inproj_bwd(dproj, w_in, x, dx1, vecs + zero)
    d_rel_bias = bias_band_bwd(dband.reshape(N_HEADS, BLOCK * 2 * BLOCK), buckets)

    small = dict(sums_f=sums_f, sums_2=sums_2, sums_1=sums_1, sums_l=sums_l, d_wa=d_wa, d_wx=d_wx,
                 d_sinks=dsink[:, 0], d_rel_bias=d_rel_bias)
    return loss, grad_x, small


def _pad_rows(a, rows):
    return jnp.concatenate([a, jnp.zeros((rows - a.shape[0], a.shape[1]), a.dtype)], axis=0)


def kernel(x, c, w_ada, b_ada, norm1_g, w_in, conv_w, conv_b, lru_wa, lru_ba, lru_wx, lru_bx, lru_lambda, w_lru_out, w_attn_out, attn_sinks, rel_bias, w_out, norm2_g, w_ff1, w_ff2, final_g, loss_target, m_w_ada, m_b_ada, m_norm1_g, m_w_in, m_conv_w, m_conv_b, m_lru_wa, m_lru_ba, m_lru_wx, m_lru_bx, m_lru_lambda, m_w_lru_out, m_w_attn_out, m_attn_sinks, m_rel_bias, m_w_out, m_norm2_g, m_w_ff1, m_w_ff2, m_final_g, v_w_ada, v_b_ada, v_norm1_g, v_w_in, v_conv_w, v_conv_b, v_lru_wa, v_lru_ba, v_lru_wx, v_lru_bx, v_lru_lambda, v_w_lru_out, v_w_attn_out, v_attn_sinks, v_rel_bias, v_w_out, v_norm2_g, v_w_ff1, v_w_ff2, v_final_g):
    xi, yi, ci = _mesh_pos()
    chip = 2 * xi + yi
    dev = 2 * chip + ci
    z8 = jnp.zeros((8, D), F32)

    shards = [w_in[0], w_lru_out[0], w_attn_out[0], w_out[0], w_ff1[0], w_ff2[0]]
    names = ["w_in", "w_lru_out", "w_attn_out", "w_out", "w_ff1", "w_ff2"]
    k_idx = jnp.reshape(chip, (1,)).astype(jnp.int32)
    c_idx = jnp.reshape(ci, (1,)).astype(jnp.int32)
    in_send, in_recv, in_flight_in, token = gather_start("gather_start_in", [cast_into_slot("cast_w_in", shards[0], k_idx)], c)

    conv_rows = jnp.concatenate([conv_w[0], jnp.zeros((4, D - D // 4), F32)], axis=1)
    pack0 = jnp.concatenate([c + token[0, 0], conv_rows, jnp.zeros((3, D), F32)], axis=0)
    g0 = all_gather_small("gather_cond", pack0).reshape(N_DEV, 8, D)
    c_all = g0[:, 0, :]
    conv_full = jnp.concatenate([g0[2 * k, 1:5, :D // 4] for k in range(N_CHIPS)], axis=1)
    c16 = jnp.concatenate([c_all, z8], axis=0)
    b_cols = lax.dynamic_slice_in_dim(b_ada, chip * ADA_SHARD, ADA_SHARD, axis=1)
    mod_c = mod_columns(c16, w_ada[0], b_cols)
    g1 = all_gather_small("gather_mod", mod_c).reshape(N_DEV, 16, ADA_SHARD)
    mod = jnp.concatenate([lax.dynamic_index_in_dim(g1[2 * k], dev, axis=0, keepdims=False) for k in range(N_CHIPS)])
    shift1, scale1, gate1, shift2, scale2, gate2 = [mod[i * D:(i + 1) * D] for i in range(6)]
    vecs = jnp.stack([norm1_g[0], scale1, shift1, gate1, norm2_g[0], scale2, shift2, gate2, final_g]
                     + [jnp.zeros((D,), F32)] * 7)
    lvec = jnp.concatenate([conv_full, conv_b, lru_ba, lru_bx, lru_lambda], axis=0)

    slots = [cast_into_slot("cast_" + nm, w, k_idx) for nm, w in zip(names[1:], shards[1:])]
    w_in_full = gather_forward("gather_forward_in", gather_wait(
        "gather_wait_in", in_send, in_recv, in_flight_in, vecs[:8, :128] + slots[-1][0, :8, :128].astype(F32)))[0]
    g_send, g_recv, in_flight, token = gather_start("gather_start_rest", slots, w_in_full)
    vecs = vecs + token[0, 0]
    pending = {}

    def rest_weights(after):
        return gather_forward("gather_forward_rest", gather_wait("gather_wait_rest", g_send, g_recv, in_flight, after))

    def reduce_hook(event, payload):
        if event == "grads_a":
            pending["sib_a"] = exchange_start("sibling_start_a", "sibling", payload)
            return pending["sib_a"][-1][0, 0]
        if event == "grads_b":
            pending["sib_b"] = exchange_start("sibling_start_b", "sibling", payload)
            return pending["sib_b"][-1][0, 0]
        return chips_start("a", names[1:], payload)

    def chips_start(tag, nms, after):
        send_sems, recv_sems, grads, lands, _ = pending["sib_" + tag]
        grads, lands = exchange_wait("sibling_wait_" + tag, "sibling", send_sems, recv_sems, grads, lands, after)
        sums = [sibling_sum("sibling_sum_" + nm, g, o, c_idx) for nm, g, o in zip(nms, grads, lands)]
        pending[tag] = exchange_start("exchange_start_" + tag, "chips", sums)
        return pending[tag][-1][0, 0]

    loss_t, grad_x, small = local_step(
        x[0], loss_target[0], vecs, lvec, lru_wa[0].astype(BF16), lru_wx[0].astype(BF16),
        attn_sinks[0], rel_bias, w_in_full, rest_weights, reduce_hook)
    zero = chips_start("b", names[:1], grad_x)
    loss = lax.psum(loss_t[0, 0], ("x", "y", "c"))

    sums_f, sums_2, sums_1, sums_l = small["sums_f"], small["sums_2"], small["sums_1"], small["sums_l"]
    vec_rows = jnp.stack([sums_1[2], sums_2[2], sums_f[0], sums_l[L_CB], sums_l[L_BA], sums_l[L_BX],
                          sums_l[L_LAM], jnp.zeros((D,), F32)])
    mod_rows = jnp.stack([sums_1[0], sums_1[1], sums_2[3], sums_2[0], sums_2[1], sums_f[1],
                          jnp.zeros((D,), F32), jnp.zeros((D,), F32)])
    att_rows = jnp.concatenate([
        jnp.concatenate([small["d_sinks"], jnp.zeros((D - N_HEADS,), F32)])[None],
        jnp.concatenate([small["d_rel_bias"].reshape(-1), jnp.zeros((D - N_BUCKETS * N_HEADS,), F32)])[None],
        jnp.zeros((6, D), F32)], axis=0)
    pack = jnp.concatenate([vec_rows + zero, _pad_rows(sums_l[0:4], 8), mod_rows, att_rows], axis=0)
    pack_w = jnp.concatenate([small["d_wa"].reshape(128, D), small["d_wx"].reshape(128, D)], axis=0).astype(BF16)
    gathered = all_gather_small("gather_small_grads", pack).reshape(N_DEV, P_WA, D)
    total = sum_devices("sum_small_grads", gathered)
    total_w = sum_devices("sum_lru_w_grads", all_gather_small("gather_lru_w_grads", pack_w).reshape(N_DEV, 256, D))
    dmod_all = gathered[:, P_MOD:P_MOD + 6, :].reshape(N_DEV, 6 * D)
    dmod16 = jnp.concatenate([lax.dynamic_slice_in_dim(dmod_all, chip * ADA_SHARD, ADA_SHARD, axis=1),
                              jnp.zeros((8, ADA_SHARD), F32)], axis=0)
    g_w_ada, d_w_ada, nm_w_ada, nv_w_ada = wada_update(c16, dmod16, w_ada[0], m_w_ada[0], v_w_ada[0])

    big_m = dict(zip(names, [m_w_in, m_w_lru_out, m_w_attn_out, m_w_out, m_w_ff1, m_w_ff2]))
    big_v = dict(zip(names, [v_w_in, v_w_lru_out, v_w_attn_out, v_w_out, v_w_ff1, v_w_ff2]))
    local_w = dict(zip(names, shards))
    g_big, d_big, nm_big, nv_big = {}, {}, {}, {}
    after = total
    for tag, nms in (("a", names[1:]), ("b", names[:1])):
        send_sems, recv_sems, sums, lands, _ = pending[tag]
        sums, lands = exchange_wait("exchange_wait_" + tag, "chips", send_sems, recv_sems, sums, lands, after)
        mine = [chip_sum("chip_sum_" + nm, t, p, k_idx) for nm, t, p in zip(nms, sums, lands)]
        theirs = halves_exchange("halves_exchange_" + tag, mine)
        for nm, a, b in zip(nms, mine, theirs):
            g2, dl, m2, v2 = adamw_big("adamw_" + nm, local_w[nm], a, b, big_m[nm][0], big_v[nm][0], c_idx)
            g_big[nm], d_big[nm], nm_big[nm], nv_big[nm] = g2[None], dl[None], m2[None], v2[None]
            after = v2

    conv_g = lax.dynamic_slice_in_dim(total[P_CONVW:P_CONVW + 4], chip * (D // 4), D // 4, axis=1)
    sm_names = ["b_ada", "norm1_g", "conv_w", "conv_b", "lru_wa", "lru_ba", "lru_wx", "lru_bx", "lru_lambda",
                "attn_sinks", "rel_bias", "norm2_g", "final_g"]
    sm_w = [b_ada.reshape(6, D), norm1_g, conv_w[0], conv_b, lru_wa.reshape(128, D), lru_ba, lru_wx.reshape(128, D),
            lru_bx, lru_lambda, attn_sinks, rel_bias, norm2_g, final_g[None]]
    sm_m = [m_b_ada.reshape(6, D), m_norm1_g, m_conv_w[0], m_conv_b, m_lru_wa.reshape(128, D), m_lru_ba,
            m_lru_wx.reshape(128, D), m_lru_bx, m_lru_lambda, m_attn_sinks, m_rel_bias, m_norm2_g, m_final_g[None]]
    sm_v = [v_b_ada.reshape(6, D), v_norm1_g, v_conv_w[0], v_conv_b, v_lru_wa.reshape(128, D), v_lru_ba,
            v_lru_wx.reshape(128, D), v_lru_bx, v_lru_lambda, v_attn_sinks, v_rel_bias, v_norm2_g, v_final_g[None]]
    sm_g = [total[P_MOD:P_MOD + 6], total[0:1], conv_g, total[3:4], total_w[0:128], total[4:5],
            total_w[128:256], total[5:6], total[6:7], total[P_ATT:P_ATT + 1, :N_HEADS],
            total[P_ATT + 1, :N_BUCKETS * N_HEADS].reshape(N_BUCKETS, N_HEADS), total[1:2], total[2:3]]
    sm_d, sm_nm, sm_nv = adamw_small(sm_w, sm_g, sm_m, sm_v)
    shapes = dict(b_ada=b_ada.shape, norm1_g=norm1_g.shape, conv_w=conv_w.shape, conv_b=conv_b.shape,
                  lru_wa=lru_wa.shape, lru_ba=lru_ba.shape, lru_wx=lru_wx.shape, lru_bx=lru_bx.shape,
                  lru_lambda=lru_lambda.shape, attn_sinks=attn_sinks.shape, rel_bias=rel_bias.shape,
                  norm2_g=norm2_g.shape, final_g=final_g.shape)
    grads = dict(w_ada=g_w_ada[None], **g_big)
    deltas = dict(w_ada=d_w_ada[None], **d_big)
    new_m = dict(w_ada=nm_w_ada[None], **nm_big)
    new_v = dict(w_ada=nv_w_ada[None], **nv_big)
    for i, nm in enumerate(sm_names):
        grads[nm] = sm_g[i].reshape(shapes[nm])
        deltas[nm] = sm_d[i].reshape(shapes[nm])
        new_m[nm] = sm_nm[i].reshape(shapes[nm])
        new_v[nm] = sm_nv[i].reshape(shapes[nm])
    order = ["w_ada", "b_ada", "norm1_g", "w_in", "conv_w", "conv_b", "lru_wa", "lru_ba", "lru_wx", "lru_bx",
             "lru_lambda", "w_lru_out", "w_attn_out", "attn_sinks", "rel_bias", "w_out", "norm2_g", "w_ff1", "w_ff2",
             "final_g"]
    return (loss, grad_x[None], *[grads[n] for n in order], *[deltas[n] for n in order],
            *[new_m[n] for n in order], *[new_v[n] for n in order])
```

```python
import math

import numpy as np
import jax
import jax.numpy as jnp
from jax import lax
from jax.experimental import pallas as pl
from jax.experimental.pallas import tpu as pltpu

F32 = jnp.float32
BF16 = jnp.bfloat16
SDS = jax.ShapeDtypeStruct
MESH = pl.DeviceIdType.MESH

D = 2048
D_FF = 4 * D
N_HEADS = 32
HEAD_DIM = 64
BLOCK = 128
N_LRU_BLOCKS = 16
LRU_C = 8.0
EPS = 1e-6
NEG_INF = -1e30
N_BUCKETS = 32
MAX_DISTANCE = 128
IN_W = 10752
IN_SHARD = IN_W // 4
IN_TILE = 896
ADA_SHARD = 6 * D // 4
OFF_LRU, OFF_GATE, OFF_Q, OFF_K, OFF_V, OFF_GA, OFF_GB = 0, 2048, 4096, 6144, 6400, 6656, 8704
SCALE = HEAD_DIM ** -0.5
N_CHIPS = 4
N_DEV = 8

ADAM_LR, ADAM_B1, ADAM_B2, ADAM_EPS, ADAM_WD, ADAM_STEP = 0.001, 0.9, 0.999, 1e-08, 0.01, 10
ADAM_C1 = 1.0 - ADAM_B1 ** ADAM_STEP
ADAM_C2 = 1.0 - ADAM_B2 ** ADAM_STEP

VMEM_LIMIT = 52 * 2 ** 20
SUB = 128
WG_TM = 1024
V_G1, V_SCALE1, V_SHIFT1, V_GATE1, V_G2, V_SCALE2, V_SHIFT2, V_GATE2, V_G3 = range(9)
L_CW0, L_CB, L_BA, L_BX, L_LAM = 0, 4, 5, 6, 7
P_VEC, P_CONVW, P_MOD, P_ATT, P_WA = 0, 8, 16, 24, 32


def _cp(n_axes):
    return pltpu.CompilerParams(dimension_semantics=("arbitrary",) * n_axes, vmem_limit_bytes=VMEM_LIMIT)


def _dot(a, b):
    return jnp.dot(a, b, preferred_element_type=F32)


def _dot_nt(a, b):
    return lax.dot_general(a, b, (((1,), (1,)), ((), ())), preferred_element_type=F32)


def _dot_tn(a, b):
    return lax.dot_general(a, b, (((0,), (0,)), ((), ())), preferred_element_type=F32)


_G0 = math.sqrt(2.0 / math.pi)
_G1 = 0.044715


def _gelu(x):
    return 0.5 * x * (1.0 + jnp.tanh(_G0 * (x + _G1 * x * x * x)))


def _gelu_grad(x):
    x2 = x * x
    t = jnp.tanh(_G0 * (x + _G1 * x * x2))
    return 0.5 * (1.0 + t) + 0.5 * x * (1.0 - t * t) * _G0 * (1.0 + 3.0 * _G1 * x2)


def _sigmoid(x):
    return 0.5 * jnp.tanh(0.5 * x) + 0.5


def _one_minus_exp2(x):
    t = jnp.tanh(x)
    return (-2.0 * t) / (1.0 - t)


def _softplus(z):
    e = jnp.exp(-jnp.abs(z))
    u = 1.0 + e
    l1p = jnp.where(u == 1.0, e, jnp.log(u) * e / (u - 1.0))
    return jnp.maximum(z, 0.0) + l1p


def _adamw_math(w, g, m, v):
    m2 = ADAM_B1 * m + (1.0 - ADAM_B1) * g
    v2 = ADAM_B2 * v + (1.0 - ADAM_B2) * (g * g)
    m_hat = m2 / ADAM_C1
    v_hat = v2 / ADAM_C2
    delta = -ADAM_LR * (m_hat / (jnp.sqrt(v_hat) + ADAM_EPS) + ADAM_WD * w)
    return delta, m2, v2


def _rms_parts(xv):
    r = lax.rsqrt(jnp.mean(xv * xv, axis=-1, keepdims=True) + EPS)
    return r, xv * r


def _row_fetches(hbm_refs, bufs, sems, i, rows):
    return [pltpu.make_async_copy(h.at[pl.ds(i * rows, rows), :], b, sems.at[n])
            for n, (h, b) in enumerate(zip(hbm_refs, bufs))]


def _modulated_norm(x_ref, v_ref, row_g, row_scale, row_shift, h_ref, rows):
    g, scale, shift = v_ref[row_g:row_g + 1, :], v_ref[row_scale:row_scale + 1, :], v_ref[row_shift:row_shift + 1, :]

    def sub(rb, carry):
        rs = pl.ds(pl.multiple_of(rb * SUB, SUB), SUB)
        _, xh = _rms_parts(x_ref[rs, :])
        h_ref[rs, :] = ((xh * g) * (1.0 + scale) + shift).astype(BF16)
        return carry

    lax.fori_loop(0, rows // SUB, sub, 0)


def inproj_fwd(x, vecs, w_in):
    s = x.shape[0]
    tm = min(1024, s)
    per = IN_SHARD // IN_TILE

    def body(x_ref, v_ref, w_ref, proj_ref, h_ref):
        @pl.when(pl.program_id(1) == 0)
        def _():
            _modulated_norm(x_ref, v_ref, V_G1, V_SCALE1, V_SHIFT1, h_ref, tm)
        proj_ref[...] = _dot(h_ref[...], w_ref[...]).astype(BF16)

    return pl.pallas_call(
        body, name="inproj_fwd", grid=(s // tm, IN_W // IN_TILE),
        in_specs=[pl.BlockSpec((tm, D), lambda i, j: (i, 0)),
                  pl.BlockSpec((16, D), lambda i, j: (0, 0)),
                  pl.BlockSpec((None, D, IN_TILE), lambda i, j: (j // per, 0, j % per))],
        out_specs=[pl.BlockSpec((tm, IN_TILE), lambda i, j: (i, j)),
                   pl.BlockSpec((tm, D), lambda i, j: (i, 0))],
        out_shape=[SDS((s, IN_W), BF16), SDS((s, D), BF16)],
        compiler_params=_cp(2))(x, vecs, w_in)


def _lru_block_fwd(xbuf, lv_ref, wa_ref, wx_ref, b, t, first):
    cs = slice(b * 128, (b + 1) * 128)
    x0 = xbuf[pl.ds(8, t), cs]
    x1 = xbuf[pl.ds(7, t), cs]
    x2 = xbuf[pl.ds(6, t), cs]
    x3 = xbuf[pl.ds(5, t), cs]
    xc = (lv_ref[L_CB:L_CB + 1, cs] + lv_ref[3:4, cs] * x0 + lv_ref[2:3, cs] * x1
          + lv_ref[1:2, cs] * x2 + lv_ref[0:1, cs] * x3)
    xcb = xc.astype(BF16)
    r = _sigmoid(_dot(xcb, wa_ref[b]) + lv_ref[L_BA:L_BA + 1, cs])
    ig = _sigmoid(_dot(xcb, wx_ref[b]) + lv_ref[L_BX:L_BX + 1, cs])
    sp = _softplus(-lv_ref[L_LAM:L_LAM + 1, cs])
    log_a = (-LRU_C) * r * sp
    a = jnp.exp(log_a)
    mult = jnp.where(first, 1.0, jnp.sqrt(_one_minus_exp2(log_a)))
    return (x0, x1, x2, x3), xc, xcb, r, ig, sp, a, mult


def lru_fwd(proj, lvec, wa, wx):
    s = proj.shape[0]
    t = min(256, s)

    def body(lx_ref, gate_ref, lv_ref, wa_ref, wx_ref, ya_ref, rec_ref, xbuf, a_s, u_s, hc):
        i = pl.program_id(0)

        @pl.when(i == 0)
        def _():
            xbuf[pl.ds(0, 8), :] = jnp.zeros((8, D), F32)
            hc[...] = jnp.zeros((8, D), F32)

        @pl.when(i > 0)
        def _():
            xbuf[pl.ds(0, 8), :] = xbuf[pl.ds(t, 8), :]

        xbuf[pl.ds(8, t), :] = lx_ref[...].astype(F32)
        first = (lax.broadcasted_iota(jnp.int32, (t, 128), 0) + i * t) == 0
        for b in range(N_LRU_BLOCKS):
            cs = slice(b * 128, (b + 1) * 128)
            _, xc, _, _, ig, _, a, mult = _lru_block_fwd(xbuf, lv_ref, wa_ref, wx_ref, b, t, first)
            a_s[:, cs] = a
            u_s[:, cs] = mult * (ig * xc)

        def step(tt, h):
            h = a_s[pl.ds(tt, 1), :] * h + u_s[pl.ds(tt, 1), :]
            rec_ref[pl.ds(tt, 1), :] = h
            return h

        hc[0:1, :] = lax.fori_loop(0, t, step, hc[0:1, :], unroll=8)
        for b in range(N_LRU_BLOCKS):
            cs = slice(b * 128, (b + 1) * 128)
            ya_ref[:, cs] = (rec_ref[:, cs] * _gelu(gate_ref[:, cs].astype(F32))).astype(BF16)

    return pl.pallas_call(
        body, name="lru_fwd", grid=(s // t,),
        in_specs=[pl.BlockSpec((t, D), lambda i: (i, OFF_LRU // D)),
                  pl.BlockSpec((t, D), lambda i: (i, OFF_GATE // D)),
                  pl.BlockSpec((8, D), lambda i: (0, 0)),
                  pl.BlockSpec((N_LRU_BLOCKS, 128, 128), lambda i: (0, 0, 0)),
                  pl.BlockSpec((N_LRU_BLOCKS, 128, 128), lambda i: (0, 0, 0))],
        out_specs=[pl.BlockSpec((t, D), lambda i: (i, 0)), pl.BlockSpec((t, D), lambda i: (i, 0))],
        out_shape=[SDS((s, D), BF16), SDS((s, D), F32)],
        scratch_shapes=[pltpu.VMEM((t + 8, D), F32), pltpu.VMEM((t, D), F32), pltpu.VMEM((t, D), F32),
                        pltpu.VMEM((8, D), F32)],
        compiler_params=_cp(1))(proj, proj, lvec, wa, wx)


def t5_bucket_table():
    qi = np.arange(BLOCK)[:, None]
    ki = np.arange(2 * BLOCK)[None, :]
    rel = qi + BLOCK - ki
    relc = np.maximum(rel, 0)
    max_exact = N_BUCKETS // 2
    relf = np.maximum(relc, 1).astype(np.float32)
    large = max_exact + (np.log(relf / np.float32(max_exact)) / np.float32(math.log(MAX_DISTANCE / max_exact))
                         * np.float32(N_BUCKETS - max_exact)).astype(np.int32)
    large = np.minimum(large, N_BUCKETS - 1)
    bucket = np.where(relc < max_exact, relc, large)
    bucket = np.where((rel >= 0) & (rel < BLOCK), bucket, -1)
    return jnp.asarray(bucket.reshape(1, BLOCK * 2 * BLOCK), jnp.int32)


def bias_band(rel_bias_t, buckets):
    n = BLOCK * 2 * BLOCK
    tn = 4096

    def body(bk_ref, rb_ref, o_ref):
        row = lax.broadcasted_iota(jnp.int32, (N_BUCKETS, tn), 0)
        oh = jnp.where(row == bk_ref[...], 1.0, 0.0).astype(BF16)
        rb = rb_ref[...]
        p0 = rb.astype(BF16)
        r1 = rb - p0.astype(F32)
        p1 = r1.astype(BF16)
        p2 = (r1 - p1.astype(F32)).astype(BF16)
        o_ref[...] = _dot(p0, oh) + _dot(p1, oh) + _dot(p2, oh)

    return pl.pallas_call(
        body, name="bias_band", grid=(n // tn,),
        in_specs=[pl.BlockSpec((1, tn), lambda i: (0, i)), pl.BlockSpec((N_HEADS, N_BUCKETS), lambda i: (0, 0))],
        out_specs=pl.BlockSpec((N_HEADS, tn), lambda i: (0, i)),
        out_shape=SDS((N_HEADS, n), F32), compiler_params=_cp(1))(buckets, rel_bias_t)


def bias_band_bwd(dband, buckets):
    n = BLOCK * 2 * BLOCK
    tn = 4096

    def body(bk_ref, d_ref, o_ref):
        @pl.when(pl.program_id(0) == 0)
        def _():
            o_ref[...] = jnp.zeros_like(o_ref)
        row = lax.broadcasted_iota(jnp.int32, (N_BUCKETS, tn), 0)
        oh = jnp.where(row == bk_ref[...], 1.0, 0.0).astype(BF16)
        dv = d_ref[...]
        p0 = dv.astype(BF16)
        r1 = dv - p0.astype(F32)
        p1 = r1.astype(BF16)
        p2 = (r1 - p1.astype(F32)).astype(BF16)
        o_ref[...] += _dot_nt(oh, p0) + _dot_nt(oh, p1) + _dot_nt(oh, p2)

    return pl.pallas_call(
        body, name="bias_band_bwd", grid=(n // tn,),
        in_specs=[pl.BlockSpec((1, tn), lambda i: (0, i)), pl.BlockSpec((N_HEADS, tn), lambda i: (0, i))],
        out_specs=pl.BlockSpec((N_BUCKETS, N_HEADS), lambda i: (0, 0)),
        out_shape=SDS((N_BUCKETS, N_HEADS), F32), compiler_params=_cp(1))(buckets, dband)


def _dup_half(band, which):
    lane = lax.broadcasted_iota(jnp.int32, band.shape, 1)
    rolled = pltpu.roll(band, 64, 1)
    keep = (lane < 64) if which == 0 else (lane >= 64)
    return jnp.where(keep, band, rolled)


def _attn_probs(scores, bias, sink, valid):
    sc = jnp.where(valid, scores * SCALE + bias, NEG_INF)
    m = jnp.maximum(jnp.max(sc, axis=-1, keepdims=True), sink)
    e = jnp.exp(sc - m)
    es = jnp.exp(sink - m)
    inv = 1.0 / (jnp.sum(e, axis=-1, keepdims=True) + es)
    return e * inv, es * inv


def _stack_heads(src_ref, kv, dst):
    lane = lax.broadcasted_iota(jnp.int32, (BLOCK, 128), 1)
    for jj in range(4):
        slab = src_ref[:, (4 * kv + jj) * 128:(4 * kv + jj + 1) * 128]
        for hh in range(2):
            keep = (lane < 64) if hh == 0 else (lane >= 64)
            dst[pl.ds((2 * jj + hh) * BLOCK, BLOCK), :] = jnp.where(keep, slab, jnp.zeros_like(slab))


def _unstack_heads(stacked, dst_ref, kv):
    lane = lax.broadcasted_iota(jnp.int32, (BLOCK, 128), 1)
    for jj in range(4):
        lo = stacked[(2 * jj) * BLOCK:(2 * jj + 1) * BLOCK]
        hi = stacked[(2 * jj + 1) * BLOCK:(2 * jj + 2) * BLOCK]
        dst_ref[:, (4 * kv + jj) * 128:(4 * kv + jj + 1) * 128] = jnp.where(lane < 64, lo, hi).astype(dst_ref.dtype)


def _band_valid(n):
    qi = lax.broadcasted_iota(jnp.int32, (BLOCK, 2 * BLOCK), 0)
    ki = lax.broadcasted_iota(jnp.int32, (BLOCK, 2 * BLOCK), 1)
    rel = qi + BLOCK - ki
    return (rel >= 0) & (rel < BLOCK) & ((ki >= BLOCK) | (n > 0))


def _kv_bands(prev_ref, cur_ref):
    band = jnp.concatenate([prev_ref[...].astype(F32), cur_ref[...].astype(F32)], axis=0)
    return [_dup_half(band, 0).astype(BF16), _dup_half(band, 1).astype(BF16)]


def attn_fwd(proj, band, sinks):
    s = proj.shape[0]
    nb = s // BLOCK
    qw = 1024

    def body(sk_ref, q_ref, kp_ref, kc_ref, vp_ref, vc_ref, b_ref, o_ref, qs_buf, s_buf, p_buf):
        n = pl.program_id(0)
        gp = pl.program_id(1)
        valid = _band_valid(n)
        kks = _kv_bands(kp_ref, kc_ref)
        vvs = _kv_bands(vp_ref, vc_ref)
        for kv in range(2):
            _stack_heads(q_ref, kv, qs_buf)
            s_buf[...] = _dot_nt(qs_buf[...], kks[kv])
            for hq in range(8):
                hl = 8 * kv + hq
                rows = pl.ds(hq * BLOCK, BLOCK)
                p, _ = _attn_probs(s_buf[rows, :], b_ref[hl], sk_ref[gp * 16 + hl], valid)
                p_buf[rows, :] = p.astype(BF16)
            _unstack_heads(_dot(p_buf[...], vvs[kv]), o_ref, kv)

    kb, vb = OFF_K // 128, OFF_V // 128
    return pl.pallas_call(
        body, name="attn_fwd", grid=(nb, 2),
        in_specs=[pl.BlockSpec(memory_space=pltpu.SMEM),
                  pl.BlockSpec((BLOCK, qw), lambda n, g: (n, OFF_Q // qw + g)),
                  pl.BlockSpec((BLOCK, 128), lambda n, g: (jnp.maximum(n - 1, 0), kb + g)),
                  pl.BlockSpec((BLOCK, 128), lambda n, g: (n, kb + g)),
                  pl.BlockSpec((BLOCK, 128), lambda n, g: (jnp.maximum(n - 1, 0), vb + g)),
                  pl.BlockSpec((BLOCK, 128), lambda n, g: (n, vb + g)),
                  pl.BlockSpec((16, BLOCK, 2 * BLOCK), lambda n, g: (g, 0, 0))],
        out_specs=pl.BlockSpec((BLOCK, qw), lambda n, g: (n, g)),
        out_shape=SDS((s, D), BF16),
        scratch_shapes=[pltpu.VMEM((8 * BLOCK, 128), BF16), pltpu.VMEM((8 * BLOCK, 2 * BLOCK), F32),
                        pltpu.VMEM((8 * BLOCK, 2 * BLOCK), BF16)],
        compiler_params=_cp(2))(sinks, proj, proj, proj, proj, proj, band)


def merge_fwd(ya, att, w_lru_out, w_attn_out, proj):
    s = ya.shape[0]
    tm, tn = min(1024, s), 512

    def body(ya_ref, at_ref, wl_ref, wt_ref, ga_ref, gb_ref, yab_ref, mg_ref):
        y_a = _dot(ya_ref[...], wl_ref[...])
        y_b = _dot(at_ref[...], wt_ref[...])
        yab_ref[0] = y_a.astype(BF16)
        yab_ref[1] = y_b.astype(BF16)
        mg_ref[...] = (_sigmoid(ga_ref[...].astype(F32)) * y_a + _sigmoid(gb_ref[...].astype(F32)) * y_b).astype(BF16)

    return pl.pallas_call(
        body, name="merge_fwd", grid=(s // tm, D // tn),
        in_specs=[pl.BlockSpec((tm, D), lambda i, j: (i, 0)), pl.BlockSpec((tm, D), lambda i, j: (i, 0)),
                  pl.BlockSpec((D, tn), lambda i, j: (0, j)), pl.BlockSpec((D, tn), lambda i, j: (0, j)),
                  pl.BlockSpec((tm, tn), lambda i, j: (i, OFF_GA // tn + j)),
                  pl.BlockSpec((tm, tn), lambda i, j: (i, OFF_GB // tn + j))],
        out_specs=[pl.BlockSpec((2, tm, tn), lambda i, j: (0, i, j)), pl.BlockSpec((tm, tn), lambda i, j: (i, j))],
        out_shape=[SDS((2, s, D), BF16), SDS((s, D), BF16)],
        compiler_params=_cp(2))(ya, att, w_lru_out, w_attn_out, proj, proj)


def outproj_fwd(merged, w_out, x, vecs):
    s = x.shape[0]
    tm, tn = min(1024, s), 512

    def body(m_ref, w_ref, x_ref, v_ref, x1_ref, o1_ref):
        o1 = _dot(m_ref[...], w_ref[...])
        o1_ref[...] = o1.astype(BF16)
        x1_ref[...] = x_ref[...] + v_ref[V_GATE1:V_GATE1 + 1, :] * o1

    return pl.pallas_call(
        body, name="outproj_fwd", grid=(s // tm, D // tn),
        in_specs=[pl.BlockSpec((tm, D), lambda i, j: (i, 0)), pl.BlockSpec((D, tn), lambda i, j: (0, j)),
                  pl.BlockSpec((tm, tn), lambda i, j: (i, j)), pl.BlockSpec((16, tn), lambda i, j: (0, j))],
        out_specs=[pl.BlockSpec((tm, tn), lambda i, j: (i, j)), pl.BlockSpec((tm, tn), lambda i, j: (i, j))],
        out_shape=[SDS((s, D), F32), SDS((s, D), BF16)],
        compiler_params=_cp(2))(merged, w_out, x, vecs)


def ff1_fwd(x1, vecs, w_ff1):
    s = x1.shape[0]
    tm, tn = min(1024, s), 512
    per = D // tn

    def body(x_ref, v_ref, w_ref, f_ref, h_ref):
        @pl.when(pl.program_id(1) == 0)
        def _():
            _modulated_norm(x_ref, v_ref, V_G2, V_SCALE2, V_SHIFT2, h_ref, tm)
        f_ref[...] = _dot(h_ref[...], w_ref[...]).astype(BF16)

    return pl.pallas_call(
        body, name="ff1_fwd", grid=(s // tm, D_FF // tn),
        in_specs=[pl.BlockSpec((tm, D), lambda i, j: (i, 0)), pl.BlockSpec((16, D), lambda i, j: (0, 0)),
                  pl.BlockSpec((None, D, tn), lambda i, j: (j // per, 0, j % per))],
        out_specs=[pl.BlockSpec((tm, tn), lambda i, j: (i, j)), pl.BlockSpec((tm, D), lambda i, j: (i, 0))],
        out_shape=[SDS((s, D_FF), BF16), SDS((s, D), BF16)],
        compiler_params=_cp(2))(x1, vecs, w_ff1)


def ff2_loss(f, w_ff2, x1, tgt, vecs):
    s = x1.shape[0]
    tm, tk = min(512, s), 1024
    nk = D_FF // tk

    def body(f_ref, w_ref, x1_hbm, t_hbm, v_ref, dx2_ref, do2_ref, sums_ref, loss_ref, acc, x1_ref, t_ref, sems):
        i, k = pl.program_id(0), pl.program_id(1)
        fetches = _row_fetches((x1_hbm, t_hbm), (x1_ref, t_ref), sems, i, tm)

        @pl.when((i == 0) & (k == 0))
        def _():
            sums_ref[...] = jnp.zeros_like(sums_ref)
            loss_ref[...] = jnp.zeros_like(loss_ref)

        @pl.when(k == 0)
        def _():
            acc[...] = jnp.zeros_like(acc)
            for cp in fetches:
                cp.start()

        fv = jnp.maximum(f_ref[...].astype(F32), 0.0)
        acc[...] += _dot((fv * fv).astype(BF16), w_ref[...])

        @pl.when(k == nk - 1)
        def _():
            for cp in fetches:
                cp.wait()
            gate2 = v_ref[V_GATE2:V_GATE2 + 1, :]
            g3 = v_ref[V_G3:V_G3 + 1, :]

            def sub(rb, carry):
                rs = pl.ds(pl.multiple_of(rb * SUB, SUB), SUB)
                o2 = acc[rs, :]
                x2 = x1_ref[rs, :] + gate2 * o2
                r3, xh = _rms_parts(x2)
                e = xh * g3 - t_ref[rs, :]
                loss_ref[...] += (0.5 / D) * jnp.sum(e * e)
                dy = e * (1.0 / D)
                sums_ref[0:1, :] += jnp.sum(dy * xh, axis=0, keepdims=True)
                dxh = dy * g3
                dx2 = r3 * (dxh - xh * jnp.mean(dxh * xh, axis=-1, keepdims=True))
                sums_ref[1:2, :] += jnp.sum(dx2 * o2, axis=0, keepdims=True)
                dx2_ref[rs, :] = dx2
                do2_ref[rs, :] = (dx2 * gate2).astype(BF16)
                return carry

            lax.fori_loop(0, tm // SUB, sub, 0)

    return pl.pallas_call(
        body, name="ff2_loss", grid=(s // tm, nk),
        in_specs=[pl.BlockSpec((tm, tk), lambda i, k: (i, k)), pl.BlockSpec((tk, D), lambda i, k: (k, 0)),
                  pl.BlockSpec(memory_space=pl.ANY), pl.BlockSpec(memory_space=pl.ANY),
                  pl.BlockSpec((16, D), lambda i, k: (0, 0))],
        out_specs=[pl.BlockSpec((tm, D), lambda i, k: (i, 0)), pl.BlockSpec((tm, D), lambda i, k: (i, 0)),
                   pl.BlockSpec((8, D), lambda i, k: (0, 0)), pl.BlockSpec((8, 128), lambda i, k: (0, 0))],
        out_shape=[SDS((s, D), F32), SDS((s, D), BF16), SDS((8, D), F32), SDS((8, 128), F32)],
        scratch_shapes=[pltpu.VMEM((tm, D), F32), pltpu.VMEM((tm, D), F32), pltpu.VMEM((tm, D), F32),
                        pltpu.SemaphoreType.DMA((2,))],
        compiler_params=_cp(2))(f, w_ff2, x1, tgt, vecs)


def ff2_bwd(do2, w_ff2, f):
    s = do2.shape[0]
    tm, tn = min(1024, s), 512

    def body(d_ref, w_ref, f_ref, o_ref):
        dff = _dot_nt(d_ref[...], w_ref[...])
        o_ref[...] = (dff * (2.0 * jnp.maximum(f_ref[...].astype(F32), 0.0))).astype(BF16)

    return pl.pallas_call(
        body, name="ff2_bwd", grid=(s // tm, D_FF // tn),
        in_specs=[pl.BlockSpec((tm, D), lambda i, j: (i, 0)), pl.BlockSpec((tn, D), lambda i, j: (j, 0)),
                  pl.BlockSpec((tm, tn), lambda i, j: (i, j))],
        out_specs=pl.BlockSpec((tm, tn), lambda i, j: (i, j)),
        out_shape=SDS((s, D_FF), BF16), compiler_params=_cp(2))(do2, w_ff2, f)


def weight_grad(name, a, b, tn, out_shape, out_block, out_map, relu2=False):
    s, m = a.shape
    n = b.shape[1]
    tm = WG_TM
    chunk = min(1024, s)
    nch = s // chunk

    def body(a_hbm, b_ref, o_ref, a_buf, at_s, sem):
        i = pl.program_id(0)

        @pl.when(pl.program_id(1) == 0)
        def _():
            def fetch(ch):
                return pltpu.make_async_copy(a_hbm.at[pl.ds(ch * chunk, chunk), pl.ds(i * tm, tm)],
                                             a_buf.at[ch % 2], sem.at[ch % 2])
            fetch(0).start()
            for ch in range(nch):
                if ch + 1 < nch:
                    fetch(ch + 1).start()
                fetch(ch).wait()
                av = a_buf[ch % 2]
                if relu2:
                    fv = jnp.maximum(av.astype(F32), 0.0)
                    av = (fv * fv).astype(BF16)
                at_s[:, ch * chunk:(ch + 1) * chunk] = av.T

        o_ref[...] = _dot(at_s[...], b_ref[...]).astype(BF16)

    return pl.pallas_call(
        body, name=name, grid=(m // tm, n // tn),
        in_specs=[pl.BlockSpec(memory_space=pl.ANY), pl.BlockSpec((s, tn), lambda i, j: (0, j))],
        out_specs=pl.BlockSpec(out_block, lambda i, j: out_map(i, j)),
        out_shape=SDS(out_shape, BF16),
        scratch_shapes=[pltpu.VMEM((2, chunk, tm), BF16), pltpu.VMEM((tm, s), BF16), pltpu.SemaphoreType.DMA((2,))],
        compiler_params=_cp(2))(a, b)


def ff1_bwd(df, w_ff1, x1, dx2, o1, vecs):
    s = df.shape[0]
    tm, tk = min(512, s), 1024
    nk = D_FF // tk
    per = D // tk

    def body(d_ref, w_ref, x1_hbm, dx2_hbm, o1_hbm, v_ref, dx1_ref, do1_ref, sums_ref, acc, x1_ref, dx2_ref, o1_ref, sems):
        i, k = pl.program_id(0), pl.program_id(1)
        fetches = _row_fetches((x1_hbm, dx2_hbm, o1_hbm), (x1_ref, dx2_ref, o1_ref), sems, i, tm)

        @pl.when((i == 0) & (k == 0))
        def _():
            sums_ref[...] = jnp.zeros_like(sums_ref)

        @pl.when(k == 0)
        def _():
            acc[...] = jnp.zeros_like(acc)
            for cp in fetches:
                cp.start()

        acc[...] += _dot_nt(d_ref[...], w_ref[...])

        @pl.when(k == nk - 1)
        def _():
            for cp in fetches:
                cp.wait()
            g2 = v_ref[V_G2:V_G2 + 1, :]
            scale2 = v_ref[V_SCALE2:V_SCALE2 + 1, :]
            gate1 = v_ref[V_GATE1:V_GATE1 + 1, :]

            def sub(rb, carry):
                rs = pl.ds(pl.multiple_of(rb * SUB, SUB), SUB)
                dh = acc[rs, :]
                r2, xh = _rms_parts(x1_ref[rs, :])
                sums_ref[0:1, :] += jnp.sum(dh, axis=0, keepdims=True)
                sums_ref[1:2, :] += jnp.sum(dh * (xh * g2), axis=0, keepdims=True)
                dxn = dh * (1.0 + scale2)
                sums_ref[2:3, :] += jnp.sum(dxn * xh, axis=0, keepdims=True)
                dxh = dxn * g2
                dx1 = dx2_ref[rs, :] + r2 * (dxh - xh * jnp.mean(dxh * xh, axis=-1, keepdims=True))
                sums_ref[3:4, :] += jnp.sum(dx1 * o1_ref[rs, :].astype(F32), axis=0, keepdims=True)
                dx1_ref[rs, :] = dx1
                do1_ref[rs, :] = (dx1 * gate1).astype(BF16)
                return carry

            lax.fori_loop(0, tm // SUB, sub, 0)

    return pl.pallas_call(
        body, name="ff1_bwd", grid=(s // tm, nk),
        in_specs=[pl.BlockSpec((tm, tk), lambda i, k: (i, k)),
                  pl.BlockSpec((None, D, tk), lambda i, k: (k // per, 0, k % per)),
                  pl.BlockSpec(memory_space=pl.ANY), pl.BlockSpec(memory_space=pl.ANY),
                  pl.BlockSpec(memory_space=pl.ANY), pl.BlockSpec((16, D), lambda i, k: (0, 0))],
        out_specs=[pl.BlockSpec((tm, D), lambda i, k: (i, 0)), pl.BlockSpec((tm, D), lambda i, k: (i, 0)),
                   pl.BlockSpec((8, D), lambda i, k: (0, 0))],
        out_shape=[SDS((s, D), F32), SDS((s, D), BF16), SDS((8, D), F32)],
        scratch_shapes=[pltpu.VMEM((tm, D), F32), pltpu.VMEM((tm, D), F32), pltpu.VMEM((tm, D), F32),
                        pltpu.VMEM((tm, D), BF16), pltpu.SemaphoreType.DMA((3,))],
        compiler_params=_cp(2))(df, w_ff1, x1, dx2, o1, vecs)


def outproj_bwd(do1, w_out, yab, proj):
    s = do1.shape[0]
    tm, tn = min(1024, s), 512
    per = D // tn

    def body(d_ref, w_ref, y_ref, g_ref, dy_ref, dp_ref):
        dm = _dot_nt(d_ref[...], w_ref[...])
        sg = _sigmoid(g_ref[...].astype(F32))
        dy_ref[...] = (dm * sg).astype(BF16)
        dp_ref[...] = (dm * y_ref[...].astype(F32) * sg * (1.0 - sg)).astype(BF16)

    return pl.pallas_call(
        body, name="outproj_bwd", grid=(s // tm, 2 * per),
        in_specs=[pl.BlockSpec((tm, D), lambda i, j: (i, 0)), pl.BlockSpec((tn, D), lambda i, j: (j % per, 0)),
                  pl.BlockSpec((None, tm, tn), lambda i, j: (j // per, i, j % per)),
                  pl.BlockSpec((tm, tn), lambda i, j: (i, OFF_GA // tn + j))],
        out_specs=[pl.BlockSpec((None, tm, tn), lambda i, j: (j // per, i, j % per)),
                   pl.BlockSpec((tm, tn), lambda i, j: (i, OFF_GA // tn + j))],
        out_shape=[SDS((2, s, D), BF16), SDS((s, IN_W), BF16)],
        compiler_params=_cp(2))(do1, w_out, yab, proj)


def lruout_bwd(dyab, w_lru_out, rec, proj, dproj):
    s = rec.shape[0]
    tm, tn = min(1024, s), 512

    def body(d_ref, w_ref, r_ref, g_ref, dp_in, dr_ref, dp_ref):
        dya = _dot_nt(d_ref[...], w_ref[...])
        gate = g_ref[...].astype(F32)
        dr_ref[...] = dya * _gelu(gate)
        dp_ref[...] = (dya * r_ref[...] * _gelu_grad(gate)).astype(BF16)

    return pl.pallas_call(
        body, name="lruout_bwd", grid=(s // tm, D // tn),
        in_specs=[pl.BlockSpec((None, tm, D), lambda i, j: (0, i, 0)), pl.BlockSpec((tn, D), lambda i, j: (j, 0)),
                  pl.BlockSpec((tm, tn), lambda i, j: (i, j)),
                  pl.BlockSpec((tm, tn), lambda i, j: (i, OFF_GATE // tn + j)),
                  pl.BlockSpec(memory_space=pl.ANY)],
        out_specs=[pl.BlockSpec((tm, tn), lambda i, j: (i, j)),
                   pl.BlockSpec((tm, tn), lambda i, j: (i, OFF_GATE // tn + j))],
        out_shape=[SDS((s, D), F32), SDS((s, IN_W), BF16)],
        input_output_aliases={4: 1},
        compiler_params=_cp(2))(dyab, w_lru_out, rec, proj, dproj)


def attnout_bwd(dyab, w_attn_out):
    s = dyab.shape[1]
    tm, tn = min(1024, s), 512

    def body(d_ref, w_ref, o_ref):
        o_ref[...] = _dot_nt(d_ref[...], w_ref[...]).astype(BF16)

    return pl.pallas_call(
        body, name="attnout_bwd", grid=(s // tm, D // tn),
        in_specs=[pl.BlockSpec((None, tm, D), lambda i, j: (1, i, 0)), pl.BlockSpec((tn, D), lambda i, j: (j, 0))],
        out_specs=pl.BlockSpec((tm, tn), lambda i, j: (i, j)),
        out_shape=SDS((s, D), BF16), compiler_params=_cp(2))(dyab, w_attn_out)


def attn_bwd(proj, band, sinks, datt, dproj):
    s = proj.shape[0]
    nb = s // BLOCK
    qw = 1024

    def body(sk_ref, q_ref, kp_ref, kc_ref, vp_ref, vc_ref, b_ref, do_ref, dp_in,
             dq_ref, dkb_ref, dvb_ref, db_ref, ds_ref, qs_buf, dos_buf, s_buf, dp_buf, p_buf, dsc_buf):
        gp = pl.program_id(0)
        n = pl.program_id(1)

        @pl.when(n == 0)
        def _():
            db_ref[...] = jnp.zeros_like(db_ref)
            ds_ref[...] = jnp.zeros_like(ds_ref)

        valid = _band_valid(n)
        kks = _kv_bands(kp_ref, kc_ref)
        vvs = _kv_bands(vp_ref, vc_ref)
        lane_b = lax.broadcasted_iota(jnp.int32, (2 * BLOCK, 128), 1)
        dks, dvs = [], []
        for kv in range(2):
            _stack_heads(q_ref, kv, qs_buf)
            _stack_heads(do_ref, kv, dos_buf)
            s_buf[...] = _dot_nt(qs_buf[...], kks[kv])
            dp_buf[...] = _dot_nt(dos_buf[...], vvs[kv])
            for hq in range(8):
                hl = 8 * kv + hq
                rows = pl.ds(hq * BLOCK, BLOCK)
                p, ps = _attn_probs(s_buf[rows, :], b_ref[hl], sk_ref[gp * 16 + hl], valid)
                dp = dp_buf[rows, :]
                delta = jnp.sum(p * dp, axis=-1, keepdims=True)
                dsc = p * (dp - delta)
                db_ref[hl] += dsc
                ds_ref[hl:hl + 1, :] += jnp.zeros((1, 128), F32) - jnp.sum(ps * delta)
                p_buf[rows, :] = p.astype(BF16)
                dsc_buf[rows, :] = (dsc * SCALE).astype(BF16)
            _unstack_heads(_dot(dsc_buf[...], kks[kv]), dq_ref, kv)
            dk = _dot_tn(dsc_buf[...], qs_buf[...])
            dv = _dot_tn(p_buf[...], dos_buf[...])
            dks.append(dk + pltpu.roll(dk, 64, 1))
            dvs.append(dv + pltpu.roll(dv, 64, 1))
        dkb_ref[...] = jnp.where(lane_b < 64, dks[0], dks[1])
        dvb_ref[...] = jnp.where(lane_b < 64, dvs[0], dvs[1])

    kb, vb = OFF_K // 128, OFF_V // 128
    return pl.pallas_call(
        body, name="attn_bwd", grid=(2, nb),
        in_specs=[pl.BlockSpec(memory_space=pltpu.SMEM),
                  pl.BlockSpec((BLOCK, qw), lambda g, n: (n, OFF_Q // qw + g)),
                  pl.BlockSpec((BLOCK, 128), lambda g, n: (jnp.maximum(n - 1, 0), kb + g)),
                  pl.BlockSpec((BLOCK, 128), lambda g, n: (n, kb + g)),
                  pl.BlockSpec((BLOCK, 128), lambda g, n: (jnp.maximum(n - 1, 0), vb + g)),
                  pl.BlockSpec((BLOCK, 128), lambda g, n: (n, vb + g)),
                  pl.BlockSpec((16, BLOCK, 2 * BLOCK), lambda g, n: (g, 0, 0)),
                  pl.BlockSpec((BLOCK, qw), lambda g, n: (n, g)),
                  pl.BlockSpec(memory_space=pl.ANY)],
        out_specs=[pl.BlockSpec((BLOCK, qw), lambda g, n: (n, OFF_Q // qw + g)),
                   pl.BlockSpec((2 * BLOCK, 128), lambda g, n: (n, g)),
                   pl.BlockSpec((2 * BLOCK, 128), lambda g, n: (n, g)),
                   pl.BlockSpec((16, BLOCK, 2 * BLOCK), lambda g, n: (g, 0, 0)),
                   pl.BlockSpec((16, 128), lambda g, n: (g, 0))],
        out_shape=[SDS((s, IN_W), BF16), SDS((nb * 2 * BLOCK, 256), F32), SDS((nb * 2 * BLOCK, 256), F32),
                   SDS((N_HEADS, BLOCK, 2 * BLOCK), F32), SDS((N_HEADS, 128), F32)],
        input_output_aliases={8: 0},
        scratch_shapes=[pltpu.VMEM((8 * BLOCK, 128), BF16), pltpu.VMEM((8 * BLOCK, 128), BF16),
                        pltpu.VMEM((8 * BLOCK, 2 * BLOCK), F32), pltpu.VMEM((8 * BLOCK, 2 * BLOCK), F32),
                        pltpu.VMEM((8 * BLOCK, 2 * BLOCK), BF16), pltpu.VMEM((8 * BLOCK, 2 * BLOCK), BF16)],
        compiler_params=_cp(2))(sinks, proj, proj, proj, proj, proj, band, datt, dproj)


def dkv_combine(dkb, dvb, dproj):
    nb = dkb.shape[0] // (2 * BLOCK)
    s = nb * BLOCK
    dkb3 = dkb.reshape(nb, 2 * BLOCK, 256)
    dvb3 = dvb.reshape(nb, 2 * BLOCK, 256)

    def body(k1, k2, v1, v2, dp_in, o_ref):
        nxt = jnp.where(pl.program_id(0) < nb - 1, 1.0, 0.0)
        o_ref[:, 0:256] = (k1[...] + nxt * k2[...]).astype(BF16)
        o_ref[:, 256:512] = (v1[...] + nxt * v2[...]).astype(BF16)

    spec1 = pl.BlockSpec((None, BLOCK, 256), lambda m: (m, 1, 0))
    spec2 = pl.BlockSpec((None, BLOCK, 256), lambda m: (jnp.minimum(m + 1, nb - 1), 0, 0))
    return pl.pallas_call(
        body, name="dkv_combine", grid=(nb,),
        in_specs=[spec1, spec2, spec1, spec2, pl.BlockSpec(memory_space=pl.ANY)],
        out_specs=pl.BlockSpec((BLOCK, 512), lambda m: (m, OFF_K // 512)),
        out_shape=SDS((s, IN_W), BF16), input_output_aliases={4: 0},
        compiler_params=_cp(1))(dkb3, dkb3, dvb3, dvb3, dproj)


def lru_bwd(proj, rec, drec, lvec, wa, wx, dproj):
    s = proj.shape[0]
    t = min(256, s)
    nt = s // t

    def body(lx_ref, lxh_ref, rec_ref, rech_ref, dr_ref, lv_ref, wa_ref, wx_ref, dp_in,
             dlx_ref, sums_ref, dwa_ref, dwx_ref,
             xbuf, hbuf, dxbuf, a_s, dh_s, xc_s, r_s, ig_s, mu_s, gc):
        step_i = pl.program_id(0)
        ti = nt - 1 - step_i

        @pl.when(step_i == 0)
        def _():
            sums_ref[...] = jnp.zeros_like(sums_ref)
            dwa_ref[...] = jnp.zeros_like(dwa_ref)
            dwx_ref[...] = jnp.zeros_like(dwx_ref)
            dxbuf[pl.ds(t, 8), :] = jnp.zeros((8, D), F32)
            gc[...] = jnp.zeros((8, D), F32)

        live = jnp.where(ti > 0, 1.0, 0.0)
        xbuf[pl.ds(0, 8), :] = lxh_ref[...].astype(F32)[8:16] * live
        xbuf[pl.ds(8, t), :] = lx_ref[...].astype(F32)
        hbuf[pl.ds(0, 8), :] = rech_ref[...] * live
        hbuf[pl.ds(8, t), :] = rec_ref[...]
        first = (lax.broadcasted_iota(jnp.int32, (t, 128), 0) + ti * t) == 0
        for b in range(N_LRU_BLOCKS):
            cs = slice(b * 128, (b + 1) * 128)
            _, xc, _, r, ig, _, a, mult = _lru_block_fwd(xbuf, lv_ref, wa_ref, wx_ref, b, t, first)
            a_s[:, cs] = a
            xc_s[:, cs] = xc
            r_s[:, cs] = r
            ig_s[:, cs] = ig
            mu_s[:, cs] = mult

        def step(q, g):
            tt = t - 1 - q
            dh = dr_ref[pl.ds(tt, 1), :] + g
            dh_s[pl.ds(tt, 1), :] = dh
            return a_s[pl.ds(tt, 1), :] * dh

        gc[0:1, :] = lax.fori_loop(0, t, step, gc[0:1, :], unroll=8)
        for b in range(N_LRU_BLOCKS):
            cs = slice(b * 128, (b + 1) * 128)
            dh = dh_s[:, cs]
            a = a_s[:, cs]
            xc = xc_s[:, cs]
            r = r_s[:, cs]
            ig = ig_s[:, cs]
            mult = mu_s[:, cs]
            sp = _softplus(-lv_ref[L_LAM:L_LAM + 1, cs])
            lam = lv_ref[L_LAM:L_LAM + 1, cs]
            da = dh * hbuf[pl.ds(7, t), cs]
            dmult = jnp.where(first, 0.0, dh * ig * xc)
            dig = dh * mult * xc
            dxc = dh * mult * ig
            dlog_a = da * a - dmult * (a * a) / mult
            dr = dlog_a * ((-LRU_C) * sp)
            dsp = jnp.sum(dlog_a * ((-LRU_C) * r), axis=0, keepdims=True)
            dza = dr * r * (1.0 - r)
            dzx = dig * ig * (1.0 - ig)
            dzab = dza.astype(BF16)
            dzxb = dzx.astype(BF16)
            xcb = xc.astype(BF16)
            dwa_ref[b] += _dot_tn(xcb, dzab)
            dwx_ref[b] += _dot_tn(xcb, dzxb)
            dxc = dxc + _dot_nt(dzab, wa_ref[b]) + _dot_nt(dzxb, wx_ref[b])
            sums_ref[L_LAM:L_LAM + 1, cs] += dsp * (-jax.nn.sigmoid(-lam))
            sums_ref[L_BA:L_BA + 1, cs] += jnp.sum(dza, axis=0, keepdims=True)
            sums_ref[L_BX:L_BX + 1, cs] += jnp.sum(dzx, axis=0, keepdims=True)
            sums_ref[L_CB:L_CB + 1, cs] += jnp.sum(dxc, axis=0, keepdims=True)
            for kk in range(4):
                sums_ref[kk:kk + 1, cs] += jnp.sum(dxc * xbuf[pl.ds(5 + kk, t), cs], axis=0, keepdims=True)
            dxbuf[pl.ds(0, t), cs] = dxc
            dlx = (lv_ref[3:4, cs] * dxc + lv_ref[2:3, cs] * dxbuf[pl.ds(1, t), cs]
                   + lv_ref[1:2, cs] * dxbuf[pl.ds(2, t), cs] + lv_ref[0:1, cs] * dxbuf[pl.ds(3, t), cs])
            dlx_ref[:, cs] = dlx.astype(BF16)
        dxbuf[pl.ds(t, 8), :] = dxbuf[pl.ds(0, 8), :]

    rev = lambda i: nt - 1 - i
    return pl.pallas_call(
        body, name="lru_bwd", grid=(nt,),
        in_specs=[pl.BlockSpec((t, D), lambda i: (rev(i), 0)),
                  pl.BlockSpec((16, D), lambda i: (jnp.maximum(rev(i) * (t // 16) - 1, 0), 0)),
                  pl.BlockSpec((t, D), lambda i: (rev(i), 0)),
                  pl.BlockSpec((8, D), lambda i: (jnp.maximum(rev(i) * (t // 8) - 1, 0), 0)),
                  pl.BlockSpec((t, D), lambda i: (rev(i), 0)),
                  pl.BlockSpec((8, D), lambda i: (0, 0)),
                  pl.BlockSpec((N_LRU_BLOCKS, 128, 128), lambda i: (0, 0, 0)),
                  pl.BlockSpec((N_LRU_BLOCKS, 128, 128), lambda i: (0, 0, 0)),
                  pl.BlockSpec(memory_space=pl.ANY)],
        out_specs=[pl.BlockSpec((t, D), lambda i: (rev(i), 0)),
                   pl.BlockSpec((8, D), lambda i: (0, 0)),
                   pl.BlockSpec((N_LRU_BLOCKS, 128, 128), lambda i: (0, 0, 0)),
                   pl.BlockSpec((N_LRU_BLOCKS, 128, 128), lambda i: (0, 0, 0))],
        out_shape=[SDS((s, IN_W), BF16), SDS((8, D), F32), SDS((N_LRU_BLOCKS, 128, 128), F32),
                   SDS((N_LRU_BLOCKS, 128, 128), F32)],
        scratch_shapes=[pltpu.VMEM((t + 8, D), F32), pltpu.VMEM((t + 8, D), F32), pltpu.VMEM((t + 8, D), F32)]
        + [pltpu.VMEM((t, D), F32)] * 6 + [pltpu.VMEM((8, D), F32)],
        input_output_aliases={8: 0},
        compiler_params=_cp(1))(proj, proj, rec, rec, drec, lvec, wa, wx, dproj)


def inproj_bwd(dproj, w_in, x, dx1, vecs):
    s = x.shape[0]
    tm, tk = min(512, s), IN_TILE
    nk = IN_W // tk
    per = IN_SHARD // tk

    def body(d_ref, w_ref, x_hbm, dx1_hbm, v_ref, gx_ref, sums_ref, acc, x_ref, dx1_ref, sems):
        i, k = pl.program_id(0), pl.program_id(1)
        fetches = _row_fetches((x_hbm, dx1_hbm), (x_ref, dx1_ref), sems, i, tm)

        @pl.when((i == 0) & (k == 0))
        def _():
            sums_ref[...] = jnp.zeros_like(sums_ref)

        @pl.when(k == 0)
        def _():
            acc[...] = jnp.zeros_like(acc)
            for cp in fetches:
                cp.start()

        acc[...] += _dot_nt(d_ref[...], w_ref[...])

        @pl.when(k == nk - 1)
        def _():
            for cp in fetches:
                cp.wait()
            g1 = v_ref[V_G1:V_G1 + 1, :]
            scale1 = v_ref[V_SCALE1:V_SCALE1 + 1, :]

            def sub(rb, carry):
                rs = pl.ds(pl.multiple_of(rb * SUB, SUB), SUB)
                dh = acc[rs, :]
                r1, xh = _rms_parts(x_ref[rs, :])
                sums_ref[0:1, :] += jnp.sum(dh, axis=0, keepdims=True)
                sums_ref[1:2, :] += jnp.sum(dh * (xh * g1), axis=0, keepdims=True)
                dxn = dh * (1.0 + scale1)
                sums_ref[2:3, :] += jnp.sum(dxn * xh, axis=0, keepdims=True)
                dxh = dxn * g1
                gx_ref[rs, :] = dx1_ref[rs, :] + r1 * (dxh - xh * jnp.mean(dxh * xh, axis=-1, keepdims=True))
                return carry

            lax.fori_loop(0, tm // SUB, sub, 0)

    return pl.pallas_call(
        body, name="inproj_bwd", grid=(s // tm, nk),
        in_specs=[pl.BlockSpec((tm, tk), lambda i, k: (i, k)),
                  pl.BlockSpec((None, D, tk), lambda i, k: (k // per, 0, k % per)),
                  pl.BlockSpec(memory_space=pl.ANY), pl.BlockSpec(memory_space=pl.ANY),
                  pl.BlockSpec((16, D), lambda i, k: (0, 0))],
        out_specs=[pl.BlockSpec((tm, D), lambda i, k: (i, 0)), pl.BlockSpec((8, D), lambda i, k: (0, 0))],
        out_shape=[SDS((s, D), F32), SDS((8, D), F32)],
        scratch_shapes=[pltpu.VMEM((tm, D), F32), pltpu.VMEM((tm, D), F32), pltpu.VMEM((tm, D), F32),
                        pltpu.SemaphoreType.DMA((2,))],
        compiler_params=_cp(2))(dproj, w_in, x, dx1, vecs)


def mod_columns(c16, w_ada, b_cols):
    tn = 512

    def body(c_ref, w_ref, b_ref, o_ref):
        cv = c_ref[...]
        ca = (cv * jax.nn.sigmoid(cv)).astype(BF16)
        o_ref[...] = _dot(ca, w_ref[...].astype(BF16)) + b_ref[...]

    return pl.pallas_call(
        body, name="mod_columns", grid=(ADA_SHARD // tn,),
        in_specs=[pl.BlockSpec((16, D), lambda j: (0, 0)), pl.BlockSpec((D, tn), lambda j: (0, j)),
                  pl.BlockSpec((1, tn), lambda j: (0, j))],
        out_specs=pl.BlockSpec((16, tn), lambda j: (0, j)),
        out_shape=SDS((16, ADA_SHARD), F32), compiler_params=_cp(1))(c16, w_ada, b_cols)


def wada_update(c16, dmod16, w, m, v):
    tm, tn = 512, 512

    def body(c_ref, d_ref, w_ref, m_ref, v_ref, g_out, dl_out, m_out, v_out):
        cv = c_ref[...]
        ca = (cv * jax.nn.sigmoid(cv)).astype(BF16)
        g = _dot_tn(ca, d_ref[...].astype(BF16))
        dl, m2, v2 = _adamw_math(w_ref[...], g, m_ref[...], v_ref[...])
        g_out[...] = g
        dl_out[...] = dl
        m_out[...] = m2
        v_out[...] = v2

    tile = pl.BlockSpec((tm, tn), lambda i, j: (i, j))
    return pl.pallas_call(
        body, name="wada_update", grid=(D // tm, ADA_SHARD // tn),
        in_specs=[pl.BlockSpec((16, tm), lambda i, j: (0, i)), pl.BlockSpec((16, tn), lambda i, j: (0, j)),
                  tile, tile, tile],
        out_specs=[tile] * 4, out_shape=[SDS((D, ADA_SHARD), F32)] * 4,
        compiler_params=_cp(2))(c16, dmod16, w, m, v)


def adamw_big(name, w, mine, theirs, m, v, c_idx):
    r, c = w.shape
    tr = 128
    per = (r // 2) // tr

    def body(c_ref, w_ref, a_ref, b_ref, m_ref, v_ref, g_out, dl_out, m_out, v_out):
        own = (pl.program_id(0) // per) == c_ref[0]
        g = jnp.where(own, a_ref[...], b_ref[...])
        dl, m2, v2 = _adamw_math(w_ref[...], g, m_ref[...], v_ref[...])
        g_out[...] = g
        dl_out[...] = dl
        m_out[...] = m2
        v_out[...] = v2

    tile = pl.BlockSpec((tr, c), lambda i, cr: (i, 0))
    half = pl.BlockSpec((tr, c), lambda i, cr: (i % per, 0))
    gs = pltpu.PrefetchScalarGridSpec(num_scalar_prefetch=1, grid=(r // tr,),
                                      in_specs=[tile, half, half, tile, tile], out_specs=[tile] * 4)
    return pl.pallas_call(body, name=name, grid_spec=gs, out_shape=[SDS((r, c), F32)] * 4,
                          compiler_params=_cp(1))(c_idx, w, mine, theirs, m, v)


def cast_into_slot(name, w, k_idx):
    r, c = w.shape
    tr = 256

    def body(k_ref, w_ref, o_ref):
        o_ref[...] = w_ref[...].astype(BF16)

    gs = pltpu.PrefetchScalarGridSpec(
        num_scalar_prefetch=1, grid=(r // tr,),
        in_specs=[pl.BlockSpec((tr, c), lambda i, kr: (i, 0))],
        out_specs=pl.BlockSpec((None, tr, c), lambda i, kr: (kr[0], i, 0)))
    return pl.pallas_call(body, name=name, grid_spec=gs, out_shape=SDS((N_CHIPS, r, c), BF16),
                          compiler_params=_cp(1))(k_idx, w)


def adamw_small(ws, gs, ms, vs):
    n = len(ws)

    def body(*refs):
        for i in range(n):
            dl, m2, v2 = _adamw_math(refs[i][...], refs[n + i][...], refs[2 * n + i][...], refs[3 * n + i][...])
            refs[4 * n + i][...] = dl
            refs[5 * n + i][...] = m2
            refs[6 * n + i][...] = v2

    vm = pl.BlockSpec(memory_space=pltpu.VMEM)
    shapes = [SDS(w.shape, F32) for w in ws]
    outs = pl.pallas_call(
        body, name="adamw_small", in_specs=[vm] * (4 * n), out_specs=[vm] * (3 * n), out_shape=shapes * 3,
        compiler_params=pltpu.CompilerParams(vmem_limit_bytes=VMEM_LIMIT))(*ws, *gs, *ms, *vs)
    return outs[:n], outs[n:2 * n], outs[2 * n:]


def sum_devices(name, gathered):
    rows = gathered.shape[1]
    tr = min(rows, 128)

    def body(x_ref, o_ref):
        acc = x_ref[0].astype(F32)
        for d in range(1, N_DEV):
            acc = acc + x_ref[d].astype(F32)
        o_ref[...] = acc

    return pl.pallas_call(
        body, name=name, grid=(rows // tr,),
        in_specs=[pl.BlockSpec((N_DEV, tr, D), lambda i: (0, i, 0))],
        out_specs=pl.BlockSpec((tr, D), lambda i: (i, 0)),
        out_shape=SDS((rows, D), F32), compiler_params=_cp(1))(gathered)


def _mesh_pos():
    return lax.axis_index("x"), lax.axis_index("y"), lax.axis_index("c")


def _other_chips(x, y):
    return [(1 - x, y), (x, 1 - y), (1 - x, 1 - y)]


def all_gather_small(name, block):
    m_per, n = block.shape

    def body(x_ref, out_ref, send_sems, recv_sems, local_sem):
        x, y, c = _mesh_pos()
        me, sibling = (x, y, c), (x, y, 1 - c)
        chips = _other_chips(x, y)

        def rows(px, py, pc):
            return out_ref.at[pl.ds((4 * px + 2 * py + pc) * m_per, m_per), :]

        def copy(k, blk, to, src=None):
            return pltpu.make_async_remote_copy(
                src_ref=rows(*blk) if src is None else src, dst_ref=rows(*blk),
                send_sem=send_sems.at[k], recv_sem=recv_sems.at[k], device_id=to, device_id_type=MESH)

        mine = pltpu.make_async_copy(x_ref, rows(*me), local_sem)
        mine.start()
        first = [copy(0, me, sibling, src=x_ref)]
        first += [copy(1 + j, me, (*chip, c), src=x_ref) for j, chip in enumerate(chips)]
        for cp in first:
            cp.start()
        passed = [copy(4 + j, (*chip, c), sibling) for j, chip in enumerate(chips)]
        for j, chip in enumerate(chips):
            copy(1 + j, (*chip, c), me).wait_recv()
            passed[j].start()
        copy(0, sibling, me).wait_recv()
        for j, chip in enumerate(chips):
            copy(4 + j, (*chip, 1 - c), me).wait_recv()
        for cp in first + passed:
            cp.wait_send()
        mine.wait()

    vm = pl.BlockSpec(memory_space=pltpu.VMEM)
    return pl.pallas_call(
        body, name=name, out_shape=SDS((N_DEV * m_per, n), block.dtype), in_specs=[vm], out_specs=vm,
        scratch_shapes=[pltpu.SemaphoreType.DMA((7,)), pltpu.SemaphoreType.DMA((7,)), pltpu.SemaphoreType.DMA],
        compiler_params=pltpu.CompilerParams(vmem_limit_bytes=VMEM_LIMIT))(block)


def sibling_sum(name, grad, other, c_idx):
    _, r, cc = grad.shape
    h = r // 2
    tr = min(256, h)
    g4 = grad.reshape(N_CHIPS, 2, h, cc)

    def body(c_ref, a_ref, b_ref, o_ref):
        o_ref[...] = (a_ref[...].astype(F32) + b_ref[...].astype(F32)).astype(BF16)

    gs = pltpu.PrefetchScalarGridSpec(
        num_scalar_prefetch=1, grid=(N_CHIPS, h // tr),
        in_specs=[pl.BlockSpec((None, None, tr, cc), lambda s, i, cr: (s, cr[0], i, 0)),
                  pl.BlockSpec((None, tr, cc), lambda s, i, cr: (s, i, 0))],
        out_specs=pl.BlockSpec((None, tr, cc), lambda s, i, cr: (s, i, 0)))
    return pl.pallas_call(body, name=name, grid_spec=gs, out_shape=SDS((N_CHIPS, h, cc), BF16),
                          compiler_params=_cp(2))(c_idx, g4, other)


HBM_SPEC = pl.BlockSpec(memory_space=pltpu.HBM)
SEM_SPEC = pl.BlockSpec(memory_space=pltpu.SEMAPHORE)


def _side_effecting():
    return pltpu.CompilerParams(has_side_effects=pltpu.SideEffectType.DATAFLOW_SIDE_EFFECTING)


def _in_hbm(a):
    return pltpu.with_memory_space_constraint(a, pltpu.HBM)


def gather_start(name, bufs, after):
    n = len(bufs)
    halves = [w.shape[1] // 2 for w in bufs]

    def body(*refs):
        ins = refs[:n]
        send_sems, recv_sems, token = refs[n + 1], refs[n + 2], refs[-1]
        x, y, c = _mesh_pos()
        k = 2 * x + y
        for i in range(n):
            reg = ins[i].at[k, pl.ds(c * halves[i], halves[i]), :]
            for j, chip in enumerate(_other_chips(x, y)):
                pltpu.make_async_remote_copy(src_ref=reg, dst_ref=reg, send_sem=send_sems.at[3 * i + j],
                                             recv_sem=recv_sems.at[3 * i + j], device_id=(*chip, c),
                                             device_id_type=MESH).start()
        token[...] = jnp.zeros_like(token)

    outs = pl.pallas_call(
        body, name=name,
        out_shape=(pltpu.SemaphoreType.DMA((3 * n,)), pltpu.SemaphoreType.DMA((3 * n,)),
                   *[pltpu.HBM(w.shape, w.dtype) for w in bufs], SDS((8, 128), F32)),
        in_specs=[HBM_SPEC] * n + [pl.BlockSpec(memory_space=pl.ANY)],
        out_specs=(SEM_SPEC, SEM_SPEC, *[HBM_SPEC] * n, pl.BlockSpec(memory_space=pltpu.VMEM)),
        input_output_aliases={i: 2 + i for i in range(n)},
        compiler_params=_side_effecting())(*[_in_hbm(w) for w in bufs], after)
    return outs[0], outs[1], list(outs[2:2 + n]), outs[-1]


def gather_wait(name, send_sems, recv_sems, bufs, after):
    n = len(bufs)
    halves = [w.shape[1] // 2 for w in bufs]

    def body(*refs):
        ins = refs[:n]
        send_sems, recv_sems = refs[n], refs[n + 1]
        x, y, c = _mesh_pos()
        k = 2 * x + y
        for i in range(n):
            for j, chip in enumerate(_other_chips(x, y)):
                kj = 2 * chip[0] + chip[1]
                cp = pltpu.make_async_remote_copy(
                    src_ref=ins[i].at[k, pl.ds(c * halves[i], halves[i]), :],
                    dst_ref=ins[i].at[kj, pl.ds(c * halves[i], halves[i]), :],
                    send_sem=send_sems.at[3 * i + j], recv_sem=recv_sems.at[3 * i + j], device_id=(*chip, c),
                    device_id_type=MESH)
                cp.wait_send()
                cp.wait_recv()

    return pl.pallas_call(
        body, name=name, out_shape=[pltpu.HBM(w.shape, w.dtype) for w in bufs],
        in_specs=[HBM_SPEC] * n + [SEM_SPEC, SEM_SPEC, pl.BlockSpec(memory_space=pl.ANY)],
        out_specs=[HBM_SPEC] * n, input_output_aliases={i: i for i in range(n)},
        compiler_params=_side_effecting())(*bufs, send_sems, recv_sems, after)


def gather_forward(name, bufs):
    n = len(bufs)
    halves = [w.shape[1] // 2 for w in bufs]

    def body(*refs):
        outs = refs[n:2 * n]
        send_sems, recv_sems = refs[2 * n:]
        x, y, c = _mesh_pos()
        chips = _other_chips(x, y)

        def copy(i, j, half, to):
            kj = 2 * chips[j][0] + chips[j][1]
            reg = outs[i].at[kj, pl.ds(half * halves[i], halves[i]), :]
            return pltpu.make_async_remote_copy(src_ref=reg, dst_ref=reg, send_sem=send_sems.at[i, j],
                                                recv_sem=recv_sems.at[i, j], device_id=to, device_id_type=MESH)

        cps = [copy(i, j, c, (x, y, 1 - c)) for i in range(n) for j in range(3)]
        for cp in cps:
            cp.start()
        for i in range(n):
            for j in range(3):
                copy(i, j, 1 - c, (x, y, c)).wait_recv()
        for cp in cps:
            cp.wait_send()

    hbm = pl.BlockSpec(memory_space=pl.ANY)
    return pl.pallas_call(
        body, name=name, in_specs=[hbm] * n, out_specs=[hbm] * n,
        out_shape=[SDS(w.shape, w.dtype) for w in bufs], input_output_aliases={i: i for i in range(n)},
        scratch_shapes=[pltpu.SemaphoreType.DMA((n, 3)), pltpu.SemaphoreType.DMA((n, 3))])(*bufs)


def _exchange_plan(kind, srcs, zones):
    x, y, c = _mesh_pos()
    plan = []
    for src, zone in zip(srcs, zones):
        if kind == "chips":
            for j, chip in enumerate(_other_chips(x, y)):
                plan.append((src.at[2 * chip[0] + chip[1]], zone.at[j], (*chip, c)))
        elif kind == "sibling":
            h = zone.shape[1]
            plan.append((src.at[:, pl.ds((1 - c) * h, h), :], zone, (x, y, 1 - c)))
        else:
            peers = [(x, y, 1 - c)] + [(*chip, cc) for chip in _other_chips(x, y) for cc in (c, 1 - c)]
            plan += [(src, zone.at[4 * x + 2 * y + c], peer) for peer in peers]
    return plan


_COPIES_PER_ARRAY = {"chips": 3, "sibling": 1, "all": N_DEV - 1}


def _landing_zones(kind, srcs):
    if kind == "chips":
        return [lax.empty((3,) + t.shape[1:], t.dtype) for t in srcs]
    if kind == "sibling":
        return [lax.empty((t.shape[0], t.shape[1] // 2, t.shape[2]), t.dtype) for t in srcs]
    return [lax.empty((N_DEV,) + t.shape, t.dtype) for t in srcs]


def exchange_start(name, kind, srcs, after):
    n = len(srcs)
    lands = _landing_zones(kind, srcs)
    n_copies = n * _COPIES_PER_ARRAY[kind]

    def body(*refs):
        send_sems, recv_sems, token = refs[2 * n + 1], refs[2 * n + 2], refs[-1]
        for q, (src, dst, dev) in enumerate(_exchange_plan(kind, refs[:n], refs[n:2 * n])):
            pltpu.make_async_remote_copy(src_ref=src, dst_ref=dst, send_sem=send_sems.at[q], recv_sem=recv_sems.at[q],
                                         device_id=dev, device_id_type=MESH).start()
        token[...] = jnp.zeros_like(token)

    outs = pl.pallas_call(
        body, name=name,
        out_shape=(pltpu.SemaphoreType.DMA((n_copies,)), pltpu.SemaphoreType.DMA((n_copies,)),
                   *[pltpu.HBM(t.shape, t.dtype) for t in srcs], *[pltpu.HBM(t.shape, t.dtype) for t in lands],
                   SDS((8, 128), F32)),
        in_specs=[HBM_SPEC] * (2 * n) + [pl.BlockSpec(memory_space=pl.ANY)],
        out_specs=(SEM_SPEC, SEM_SPEC, *[HBM_SPEC] * (2 * n), pl.BlockSpec(memory_space=pltpu.VMEM)),
        input_output_aliases={i: 2 + i for i in range(2 * n)},
        compiler_params=_side_effecting())(*[_in_hbm(t) for t in srcs], *[_in_hbm(t) for t in lands], after)
    return outs[0], outs[1], list(outs[2:2 + n]), list(outs[2 + n:2 + 2 * n]), outs[-1]


def exchange_wait(name, kind, send_sems, recv_sems, srcs, lands, after):
    n = len(srcs)

    def body(*refs):
        send_sems, recv_sems = refs[2 * n], refs[2 * n + 1]
        for q, (src, dst, dev) in enumerate(_exchange_plan(kind, refs[:n], refs[n:2 * n])):
            cp = pltpu.make_async_remote_copy(src_ref=src, dst_ref=dst, send_sem=send_sems.at[q],
                                              recv_sem=recv_sems.at[q], device_id=dev, device_id_type=MESH)
            cp.wait_send()
            cp.wait_recv()

    outs = pl.pallas_call(
        body, name=name, out_shape=[pltpu.HBM(t.shape, t.dtype) for t in srcs + lands],
        in_specs=[HBM_SPEC] * (2 * n) + [SEM_SPEC, SEM_SPEC, pl.BlockSpec(memory_space=pl.ANY)],
        out_specs=[HBM_SPEC] * (2 * n), input_output_aliases={i: i for i in range(2 * n)},
        compiler_params=_side_effecting())(*srcs, *lands, send_sems, recv_sems, after)
    return list(outs[:n]), list(outs[n:])


def chip_sum(name, sums, parts, k_idx):
    _, h, cc = parts.shape
    tr = min(256, h)

    def body(k_ref, own_ref, p_ref, o_ref):
        acc = own_ref[...].astype(F32)
        for s in range(3):
            acc = acc + p_ref[s].astype(F32)
        o_ref[...] = acc

    gs = pltpu.PrefetchScalarGridSpec(
        num_scalar_prefetch=1, grid=(h // tr,),
        in_specs=[pl.BlockSpec((None, tr, cc), lambda i, kr: (kr[0], i, 0)),
                  pl.BlockSpec((3, tr, cc), lambda i, kr: (0, i, 0))],
        out_specs=pl.BlockSpec((tr, cc), lambda i, kr: (i, 0)))
    return pl.pallas_call(body, name=name, grid_spec=gs, out_shape=SDS((h, cc), F32),
                          compiler_params=_cp(1))(k_idx, sums, parts)


def halves_exchange(name, halves):
    n = len(halves)

    def body(*refs):
        ins, outs = refs[:n], refs[n:2 * n]
        send_sems, recv_sems = refs[2 * n:]
        x, y, c = _mesh_pos()
        cps = []
        for i in range(n):
            cp = pltpu.make_async_remote_copy(
                src_ref=ins[i], dst_ref=outs[i], send_sem=send_sems.at[i], recv_sem=recv_sems.at[i],
                device_id=(x, y, 1 - c), device_id_type=MESH)
            cp.start()
            cps.append(cp)
        for cp in cps:
            cp.wait_recv()
        for cp in cps:
            cp.wait_send()

    hbm = pl.BlockSpec(memory_space=pl.ANY)
    return pl.pallas_call(
        body, name=name, in_specs=[hbm] * n, out_specs=[hbm] * n,
        out_shape=[SDS(t.shape, F32) for t in halves],
        scratch_shapes=[pltpu.SemaphoreType.DMA((n,)), pltpu.SemaphoreType.DMA((n,))])(*halves)


def local_step(x, tgt, vecs, lvec, wa, wx, sinks, rel_bias, w_in, rest_weights, hook):
    buckets = t5_bucket_table()
    band = bias_band(rel_bias.T, buckets).reshape(N_HEADS, BLOCK, 2 * BLOCK)

    proj, h = inproj_fwd(x, vecs, w_in)
    ya, rec = lru_fwd(proj, lvec, wa, wx)
    att = attn_fwd(proj, band, sinks)
    w_lru_out, w_attn_out, w_out, w_ff1, w_ff2 = rest_weights(att[:8, :128] + ya[:8, :128])
    w_lru_out2, w_attn_out2, w_out2 = w_lru_out.reshape(D, D), w_attn_out.reshape(D, D), w_out.reshape(D, D)
    w_ff2_2 = w_ff2.reshape(D_FF, D)
    yab, merged = merge_fwd(ya, att, w_lru_out2, w_attn_out2, proj)
    x1, o1 = outproj_fwd(merged, w_out2, x, vecs)
    f, h2 = ff1_fwd(x1, vecs, w_ff1)
    dx2, do2, sums_f, loss = ff2_loss(f, w_ff2_2, x1, tgt, vecs)

    df = ff2_bwd(do2, w_ff2_2, f)
    g_ff2 = weight_grad("dw_ff2", f, do2, 512, (D_FF, D), (WG_TM, 512), lambda i, j: (i, j), relu2=True)
    dx1, do1, sums_2 = ff1_bwd(df, w_ff1, x1, dx2, o1, vecs)
    g_ff1 = weight_grad("dw_ff1", h2, df, 512, (N_CHIPS, D, D), (None, WG_TM, 512), lambda i, j: (j // 4, i, j % 4))
    dyab, dproj = outproj_bwd(do1, w_out2, yab, proj)
    g_out = weight_grad("dw_out", merged, do1, 512, (D, D), (WG_TM, 512), lambda i, j: (i, j))
    drec, dproj = lruout_bwd(dyab, w_lru_out2, rec, proj, dproj)
    g_lru_out = weight_grad("dw_lru_out", ya, dyab[0], 512, (D, D), (WG_TM, 512), lambda i, j: (i, j))
    datt = attnout_bwd(dyab, w_attn_out2)
    g_attn_out = weight_grad("dw_attn_out", att, dyab[1], 512, (D, D), (WG_TM, 512), lambda i, j: (i, j))
    zero = hook("grads_a", [g_lru_out.reshape(N_CHIPS, D // 4, D), g_attn_out.reshape(N_CHIPS, D // 4, D),
                            g_out.reshape(N_CHIPS, D // 4, D), g_ff1, g_ff2.reshape(N_CHIPS, D_FF // 4, D)])
    dproj, dkb, dvb, dband, dsink = attn_bwd(proj, band, sinks + zero, datt, dproj)
    zero = hook("after_attn_bwd", dkb)
    dproj = dkv_combine(dkb, dvb, dproj)
    dproj, sums_l, d_wa, d_wx = lru_bwd(proj, rec, drec, lvec + zero, wa, wx, dproj)
    hook("lru_grads", (d_wa, d_wx))
    per = IN_SHARD // IN_TILE
    g_in = weight_grad("dw_in", h, dproj, IN_TILE, (N_CHIPS, D, IN_SHARD), (None, WG_TM, IN_TILE),
                       lambda i, j: (j // per, i, j % per))
    zero = hook("grads_b", [g_in])
    grad_x, sums_1 = inproj_bwd(dproj, w_in, x, dx1, vecs + zero)
    d_rel_bias = bias_band_bwd(dband.reshape(N_HEADS, BLOCK * 2 * BLOCK), buckets)

    small = dict(sums_f=sums_f, sums_2=sums_2, sums_1=sums_1, sums_l=sums_l, d_wa=d_wa, d_wx=d_wx,
                 d_sinks=dsink[:, 0], d_rel_bias=d_rel_bias)
    return loss, grad_x, small


def _pad_rows(a, rows):
    return jnp.concatenate([a, jnp.zeros((rows - a.shape[0], a.shape[1]), a.dtype)], axis=0)


def kernel(x, c, w_ada, b_ada, norm1_g, w_in, conv_w, conv_b, lru_wa, lru_ba, lru_wx, lru_bx, lru_lambda, w_lru_out, w_attn_out, attn_sinks, rel_bias, w_out, norm2_g, w_ff1, w_ff2, final_g, loss_target, m_w_ada, m_b_ada, m_norm1_g, m_w_in, m_conv_w, m_conv_b, m_lru_wa, m_lru_ba, m_lru_wx, m_lru_bx, m_lru_lambda, m_w_lru_out, m_w_attn_out, m_attn_sinks, m_rel_bias, m_w_out, m_norm2_g, m_w_ff1, m_w_ff2, m_final_g, v_w_ada, v_b_ada, v_norm1_g, v_w_in, v_conv_w, v_conv_b, v_lru_wa, v_lru_ba, v_lru_wx, v_lru_bx, v_lru_lambda, v_w_lru_out, v_w_attn_out, v_attn_sinks, v_rel_bias, v_w_out, v_norm2_g, v_w_ff1, v_w_ff2, v_final_g):
    xi, yi, ci = _mesh_pos()
    chip = 2 * xi + yi
    dev = 2 * chip + ci
    z8 = jnp.zeros((8, D), F32)

    conv_rows = jnp.concatenate([conv_w[0], jnp.zeros((4, D - D // 4), F32)], axis=1)
    pack0 = jnp.concatenate([c, conv_rows, jnp.zeros((3, D), F32)], axis=0)
    g0 = all_gather_small("gather_cond", pack0).reshape(N_DEV, 8, D)
    c_all = g0[:, 0, :]
    conv_full = jnp.concatenate([g0[2 * k, 1:5, :D // 4] for k in range(N_CHIPS)], axis=1)
    c16 = jnp.concatenate([c_all, z8], axis=0)
    b_cols = lax.dynamic_slice_in_dim(b_ada, chip * ADA_SHARD, ADA_SHARD, axis=1)
    mod_c = mod_columns(c16, w_ada[0], b_cols)
    g1 = all_gather_small("gather_mod", mod_c).reshape(N_DEV, 16, ADA_SHARD)
    mod = jnp.concatenate([lax.dynamic_index_in_dim(g1[2 * k], dev, axis=0, keepdims=False) for k in range(N_CHIPS)])
    shift1, scale1, gate1, shift2, scale2, gate2 = [mod[i * D:(i + 1) * D] for i in range(6)]
    vecs = jnp.stack([norm1_g[0], scale1, shift1, gate1, norm2_g[0], scale2, shift2, gate2, final_g]
                     + [jnp.zeros((D,), F32)] * 7)
    lvec = jnp.concatenate([conv_full, conv_b, lru_ba, lru_bx, lru_lambda], axis=0)

    shards = [w_in[0], w_lru_out[0], w_attn_out[0], w_out[0], w_ff1[0], w_ff2[0]]
    names = ["w_in", "w_lru_out", "w_attn_out", "w_out", "w_ff1", "w_ff2"]
    k_idx = jnp.reshape(chip, (1,)).astype(jnp.int32)
    c_idx = jnp.reshape(ci, (1,)).astype(jnp.int32)
    in_send, in_recv, in_flight_in, _ = gather_start(
        "gather_start_in", [cast_into_slot("cast_w_in", shards[0], k_idx)], vecs)
    slots = [cast_into_slot("cast_" + nm, w, k_idx) for nm, w in zip(names[1:], shards[1:])]
    w_in_full = gather_forward("gather_forward_in", gather_wait(
        "gather_wait_in", in_send, in_recv, in_flight_in, vecs[:8, :128] + slots[-1][0, :8, :128].astype(F32)))[0]
    g_send, g_recv, in_flight, token = gather_start("gather_start_rest", slots, w_in_full)
    vecs = vecs + token[0, 0]
    pending = {}

    def rest_weights(after):
        return gather_forward("gather_forward_rest", gather_wait("gather_wait_rest", g_send, g_recv, in_flight, after))

    def reduce_hook(event, payload):
        if event == "grads_a":
            pending["sib_a"] = exchange_start("sibling_start_a", "sibling", payload, payload[0])
            return pending["sib_a"][-1][0, 0]
        if event == "lru_grads":
            pack_w = jnp.concatenate([payload[0].reshape(128, D), payload[1].reshape(128, D)], axis=0).astype(BF16)
            pending["lru_w"] = exchange_start("lru_w_grads_start", "all", [pack_w], pack_w)
            return pending["lru_w"][-1][0, 0]
        if event == "grads_b":
            pending["sib_b"] = exchange_start("sibling_start_b", "sibling", payload, pending["lru_w"][-1])
            return pending["sib_b"][-1][0, 0]
        return chips_start("a", names[1:], payload)

    def chips_start(tag, nms, after):
        send_sems, recv_sems, grads, lands, _ = pending["sib_" + tag]
        grads, lands = exchange_wait("sibling_wait_" + tag, "sibling", send_sems, recv_sems, grads, lands, after)
        sums = [sibling_sum("sibling_sum_" + nm, g, o, c_idx) for nm, g, o in zip(nms, grads, lands)]
        pending[tag] = exchange_start("exchange_start_" + tag, "chips", sums, sums[0])
        return pending[tag][-1][0, 0]

    loss_t, grad_x, small = local_step(
        x[0], loss_target[0], vecs, lvec, lru_wa[0].astype(BF16), lru_wx[0].astype(BF16),
        attn_sinks[0], rel_bias, w_in_full, rest_weights, reduce_hook)
    chips_start("b", names[:1], grad_x)

    big_m = dict(zip(names, [m_w_in, m_w_lru_out, m_w_attn_out, m_w_out, m_w_ff1, m_w_ff2]))
    big_v = dict(zip(names, [v_w_in, v_w_lru_out, v_w_attn_out, v_w_out, v_w_ff1, v_w_ff2]))
    local_w = dict(zip(names, shards))
    g_big, d_big, nm_big, nv_big = {}, {}, {}, {}

    def links_done(tag, after):
        send_sems, recv_sems, sums, lands, _ = pending[tag]
        return exchange_wait("exchange_wait_" + tag, "chips", send_sems, recv_sems, sums, lands, after)

    def finish_reduce(nms, sums, lands):
        mine = [chip_sum("chip_sum_" + nm, t, p, k_idx) for nm, t, p in zip(nms, sums, lands)]
        theirs = halves_exchange("halves_exchange_" + nms[0], mine)
        for nm, a, b in zip(nms, mine, theirs):
            g2, dl, m2, v2 = adamw_big("adamw_" + nm, local_w[nm], a, b, big_m[nm][0], big_v[nm][0], c_idx)
            g_big[nm], d_big[nm], nm_big[nm], nv_big[nm] = g2[None], dl[None], m2[None], v2[None]
        return v2

    done_a = finish_reduce(names[1:], *links_done("a", grad_x))
    sums_b, lands_b = links_done("b", done_a)
    w_send, w_recv, w_src, w_lands, _ = pending["lru_w"]
    w_src, w_lands = exchange_wait("lru_w_grads_wait", "all", w_send, w_recv, w_src, w_lands, lands_b[0])

    sums_f, sums_2, sums_1, sums_l = small["sums_f"], small["sums_2"], small["sums_1"], small["sums_l"]
    vec_rows = jnp.stack([sums_1[2], sums_2[2], sums_f[0], sums_l[L_CB], sums_l[L_BA], sums_l[L_BX],
                          sums_l[L_LAM], jnp.zeros((D,), F32)])
    mod_rows = jnp.stack([sums_1[0], sums_1[1], sums_2[3], sums_2[0], sums_2[1], sums_f[1],
                          jnp.zeros((D,), F32), jnp.zeros((D,), F32)])
    att_rows = jnp.concatenate([
        jnp.concatenate([small["d_sinks"], jnp.zeros((D - N_HEADS,), F32)])[None],
        jnp.concatenate([small["d_rel_bias"].reshape(-1), jnp.zeros((D - N_BUCKETS * N_HEADS,), F32)])[None],
        jnp.zeros((6, D), F32)], axis=0)
    pack = jnp.concatenate([vec_rows, _pad_rows(sums_l[0:4], 8), mod_rows, att_rows], axis=0)
    pack, lru_w_all = lax.optimization_barrier((pack, w_lands[0]))
    gathered = all_gather_small("gather_small_grads", pack).reshape(N_DEV, P_WA, D)
    total = sum_devices("sum_small_grads", gathered)
    total_w = sum_devices("sum_lru_w_grads", lax.dynamic_update_slice(lru_w_all, w_src[0][None], (dev, 0, 0)))
    dmod_all = gathered[:, P_MOD:P_MOD + 6, :].reshape(N_DEV, 6 * D)
    dmod16 = jnp.concatenate([lax.dynamic_slice_in_dim(dmod_all, chip * ADA_SHARD, ADA_SHARD, axis=1),
                              jnp.zeros((8, ADA_SHARD), F32)], axis=0)
    g_w_ada, d_w_ada, nm_w_ada, nv_w_ada = wada_update(c16, dmod16, w_ada[0], m_w_ada[0], v_w_ada[0])
    finish_reduce(names[:1], sums_b, lands_b)
    loss = lax.psum(lax.optimization_barrier((loss_t, total))[0][0, 0], ("x", "y", "c"))

    conv_g = lax.dynamic_slice_in_dim(total[P_CONVW:P_CONVW + 4], chip * (D // 4), D // 4, axis=1)
    sm_names = ["b_ada", "norm1_g", "conv_w", "conv_b", "lru_wa", "lru_ba", "lru_wx", "lru_bx", "lru_lambda",
                "attn_sinks", "rel_bias", "norm2_g", "final_g"]
    sm_w = [b_ada.reshape(6, D), norm1_g, conv_w[0], conv_b, lru_wa.reshape(128, D), lru_ba, lru_wx.reshape(128, D),
            lru_bx, lru_lambda, attn_sinks, rel_bias, norm2_g, final_g[None]]
    sm_m = [m_b_ada.reshape(6, D), m_norm1_g, m_conv_w[0], m_conv_b, m_lru_wa.reshape(128, D), m_lru_ba,
            m_lru_wx.reshape(128, D), m_lru_bx, m_lru_lambda, m_attn_sinks, m_rel_bias, m_norm2_g, m_final_g[None]]
    sm_v = [v_b_ada.reshape(6, D), v_norm1_g, v_conv_w[0], v_conv_b, v_lru_wa.reshape(128, D), v_lru_ba,
            v_lru_wx.reshape(128, D), v_lru_bx, v_lru_lambda, v_attn_sinks, v_rel_bias, v_norm2_g, v_final_g[None]]
    sm_g = [total[P_MOD:P_MOD + 6], total[0:1], conv_g, total[3:4], total_w[0:128], total[4:5],
            total_w[128:256], total[5:6], total[6:7], total[P_ATT:P_ATT + 1, :N_HEADS],
            total[P_ATT + 1, :N_BUCKETS * N_HEADS].reshape(N_BUCKETS, N_HEADS), total[1:2], total[2:3]]
    sm_d, sm_nm, sm_nv = adamw_small(sm_w, sm_g, sm_m, sm_v)
    shapes = dict(b_ada=b_ada.shape, norm1_g=norm1_g.shape, conv_w=conv_w.shape, conv_b=conv_b.shape,
                  lru_wa=lru_wa.shape, lru_ba=lru_ba.shape, lru_wx=lru_wx.shape, lru_bx=lru_bx.shape,
                  lru_lambda=lru_lambda.shape, attn_sinks=attn_sinks.shape, rel_bias=rel_bias.shape,
                  norm2_g=norm2_g.shape, final_g=final_g.shape)
    grads = dict(w_ada=g_w_ada[None], **g_big)
    deltas = dict(w_ada=d_w_ada[None], **d_big)
    new_m = dict(w_ada=nm_w_ada[None], **nm_big)
    new_v = dict(w_ada=nv_w_ada[None], **nv_big)
    for i, nm in enumerate(sm_names):
        grads[nm] = sm_g[i].reshape(shapes[nm])
        deltas[nm] = sm_d[i].reshape(shapes[nm])
        new_m[nm] = sm_nm[i].reshape(shapes[nm])
        new_v[nm] = sm_nv[i].reshape(shapes[nm])
    order = ["w_ada", "b_ada", "norm1_g", "w_in", "conv_w", "conv_b", "lru_wa", "lru_ba", "lru_wx", "lru_bx",
             "lru_lambda", "w_lru_out", "w_attn_out", "attn_sinks", "rel_bias", "w_out", "norm2_g", "w_ff1", "w_ff2",
             "final_g"]
    return (loss, grad_x[None], *[grads[n] for n in order], *[deltas[n] for n in order],
            *[new_m[n] for n in order], *[new_v[n] for n in order])
```

```python
import math

import numpy as np
import jax
import jax.numpy as jnp
from jax import lax
from jax.experimental import pallas as pl
from jax.experimental.pallas import tpu as pltpu

F32 = jnp.float32
BF16 = jnp.bfloat16
SDS = jax.ShapeDtypeStruct
MESH = pl.DeviceIdType.MESH

D = 2048
D_FF = 4 * D
N_HEADS = 32
HEAD_DIM = 64
BLOCK = 128
N_LRU_BLOCKS = 16
LRU_C = 8.0
EPS = 1e-6
NEG_INF = -1e30
N_BUCKETS = 32
MAX_DISTANCE = 128
IN_W = 10752
IN_SHARD = IN_W // 4
IN_TILE = 896
ADA_SHARD = 6 * D // 4
OFF_LRU, OFF_GATE, OFF_Q, OFF_K, OFF_V, OFF_GA, OFF_GB = 0, 2048, 4096, 6144, 6400, 6656, 8704
SCALE = HEAD_DIM ** -0.5
N_CHIPS = 4
N_DEV = 8

ADAM_LR, ADAM_B1, ADAM_B2, ADAM_EPS, ADAM_WD, ADAM_STEP = 0.001, 0.9, 0.999, 1e-08, 0.01, 10
ADAM_C1 = 1.0 - ADAM_B1 ** ADAM_STEP
ADAM_C2 = 1.0 - ADAM_B2 ** ADAM_STEP

VMEM_LIMIT = 52 * 2 ** 20
SUB = 128
WG_TM = 1024
V_G1, V_SCALE1, V_SHIFT1, V_GATE1, V_G2, V_SCALE2, V_SHIFT2, V_GATE2, V_G3 = range(9)
L_CW0, L_CB, L_BA, L_BX, L_LAM = 0, 4, 5, 6, 7
P_VEC, P_CONVW, P_MOD, P_ATT, P_WA = 0, 8, 16, 24, 32


def _cp(n_axes):
    return pltpu.CompilerParams(dimension_semantics=("arbitrary",) * n_axes, vmem_limit_bytes=VMEM_LIMIT)


def _dot(a, b):
    return jnp.dot(a, b, preferred_element_type=F32)


def _dot_nt(a, b):
    return lax.dot_general(a, b, (((1,), (1,)), ((), ())), preferred_element_type=F32)


def _dot_tn(a, b):
    return lax.dot_general(a, b, (((0,), (0,)), ((), ())), preferred_element_type=F32)


_G0 = math.sqrt(2.0 / math.pi)
_G1 = 0.044715


def _gelu(x):
    return 0.5 * x * (1.0 + jnp.tanh(_G0 * (x + _G1 * x * x * x)))


def _gelu_grad(x):
    x2 = x * x
    t = jnp.tanh(_G0 * (x + _G1 * x * x2))
    return 0.5 * (1.0 + t) + 0.5 * x * (1.0 - t * t) * _G0 * (1.0 + 3.0 * _G1 * x2)


def _sigmoid(x):
    return 0.5 * jnp.tanh(0.5 * x) + 0.5


def _one_minus_exp2(x):
    t = jnp.tanh(x)
    return (-2.0 * t) / (1.0 - t)


def _softplus(z):
    e = jnp.exp(-jnp.abs(z))
    u = 1.0 + e
    l1p = jnp.where(u == 1.0, e, jnp.log(u) * e / (u - 1.0))
    return jnp.maximum(z, 0.0) + l1p


def _adamw_math(w, g, m, v):
    m2 = ADAM_B1 * m + (1.0 - ADAM_B1) * g
    v2 = ADAM_B2 * v + (1.0 - ADAM_B2) * (g * g)
    m_hat = m2 / ADAM_C1
    v_hat = v2 / ADAM_C2
    delta = -ADAM_LR * (m_hat / (jnp.sqrt(v_hat) + ADAM_EPS) + ADAM_WD * w)
    return delta, m2, v2


def _rms_parts(xv):
    r = lax.rsqrt(jnp.mean(xv * xv, axis=-1, keepdims=True) + EPS)
    return r, xv * r


def _row_fetches(hbm_refs, bufs, sems, i, rows):
    return [pltpu.make_async_copy(h.at[pl.ds(i * rows, rows), :], b, sems.at[n])
            for n, (h, b) in enumerate(zip(hbm_refs, bufs))]


def _modulated_norm(x_ref, v_ref, row_g, row_scale, row_shift, h_ref, rows):
    g, scale, shift = v_ref[row_g:row_g + 1, :], v_ref[row_scale:row_scale + 1, :], v_ref[row_shift:row_shift + 1, :]

    def sub(rb, carry):
        rs = pl.ds(pl.multiple_of(rb * SUB, SUB), SUB)
        _, xh = _rms_parts(x_ref[rs, :])
        h_ref[rs, :] = ((xh * g) * (1.0 + scale) + shift).astype(BF16)
        return carry

    lax.fori_loop(0, rows // SUB, sub, 0)


def inproj_fwd(x, vecs, w_in):
    s = x.shape[0]
    tm = min(1024, s)
    per = IN_SHARD // IN_TILE

    def body(x_ref, v_ref, w_ref, proj_ref, h_ref):
        @pl.when(pl.program_id(1) == 0)
        def _():
            _modulated_norm(x_ref, v_ref, V_G1, V_SCALE1, V_SHIFT1, h_ref, tm)
        proj_ref[...] = _dot(h_ref[...], w_ref[...]).astype(BF16)

    return pl.pallas_call(
        body, name="inproj_fwd", grid=(s // tm, IN_W // IN_TILE),
        in_specs=[pl.BlockSpec((tm, D), lambda i, j: (i, 0)),
                  pl.BlockSpec((16, D), lambda i, j: (0, 0)),
                  pl.BlockSpec((None, D, IN_TILE), lambda i, j: (j // per, 0, j % per))],
        out_specs=[pl.BlockSpec((tm, IN_TILE), lambda i, j: (i, j)),
                   pl.BlockSpec((tm, D), lambda i, j: (i, 0))],
        out_shape=[SDS((s, IN_W), BF16), SDS((s, D), BF16)],
        compiler_params=_cp(2))(x, vecs, w_in)


def _lru_block_fwd(xbuf, lv_ref, wa_ref, wx_ref, b, t, first):
    cs = slice(b * 128, (b + 1) * 128)
    x0 = xbuf[pl.ds(8, t), cs]
    x1 = xbuf[pl.ds(7, t), cs]
    x2 = xbuf[pl.ds(6, t), cs]
    x3 = xbuf[pl.ds(5, t), cs]
    xc = (lv_ref[L_CB:L_CB + 1, cs] + lv_ref[3:4, cs] * x0 + lv_ref[2:3, cs] * x1
          + lv_ref[1:2, cs] * x2 + lv_ref[0:1, cs] * x3)
    xcb = xc.astype(BF16)
    r = _sigmoid(_dot(xcb, wa_ref[b]) + lv_ref[L_BA:L_BA + 1, cs])
    ig = _sigmoid(_dot(xcb, wx_ref[b]) + lv_ref[L_BX:L_BX + 1, cs])
    sp = _softplus(-lv_ref[L_LAM:L_LAM + 1, cs])
    log_a = (-LRU_C) * r * sp
    a = jnp.exp(log_a)
    mult = jnp.where(first, 1.0, jnp.sqrt(_one_minus_exp2(log_a)))
    return (x0, x1, x2, x3), xc, xcb, r, ig, sp, a, mult


def lru_fwd(proj, lvec, wa, wx):
    s = proj.shape[0]
    t = min(256, s)

    def body(lx_ref, gate_ref, lv_ref, wa_ref, wx_ref, ya_ref, rec_ref, xbuf, a_s, u_s, hc):
        i = pl.program_id(0)

        @pl.when(i == 0)
        def _():
            xbuf[pl.ds(0, 8), :] = jnp.zeros((8, D), F32)
            hc[...] = jnp.zeros((8, D), F32)

        @pl.when(i > 0)
        def _():
            xbuf[pl.ds(0, 8), :] = xbuf[pl.ds(t, 8), :]

        xbuf[pl.ds(8, t), :] = lx_ref[...].astype(F32)
        first = (lax.broadcasted_iota(jnp.int32, (t, 128), 0) + i * t) == 0
        for b in range(N_LRU_BLOCKS):
            cs = slice(b * 128, (b + 1) * 128)
            _, xc, _, _, ig, _, a, mult = _lru_block_fwd(xbuf, lv_ref, wa_ref, wx_ref, b, t, first)
            a_s[:, cs] = a
            u_s[:, cs] = mult * (ig * xc)

        def step(tt, h):
            h = a_s[pl.ds(tt, 1), :] * h + u_s[pl.ds(tt, 1), :]
            rec_ref[pl.ds(tt, 1), :] = h
            return h

        hc[0:1, :] = lax.fori_loop(0, t, step, hc[0:1, :], unroll=8)
        for b in range(N_LRU_BLOCKS):
            cs = slice(b * 128, (b + 1) * 128)
            ya_ref[:, cs] = (rec_ref[:, cs] * _gelu(gate_ref[:, cs].astype(F32))).astype(BF16)

    return pl.pallas_call(
        body, name="lru_fwd", grid=(s // t,),
        in_specs=[pl.BlockSpec((t, D), lambda i: (i, OFF_LRU // D)),
                  pl.BlockSpec((t, D), lambda i: (i, OFF_GATE // D)),
                  pl.BlockSpec((8, D), lambda i: (0, 0)),
                  pl.BlockSpec((N_LRU_BLOCKS, 128, 128), lambda i: (0, 0, 0)),
                  pl.BlockSpec((N_LRU_BLOCKS, 128, 128), lambda i: (0, 0, 0))],
        out_specs=[pl.BlockSpec((t, D), lambda i: (i, 0)), pl.BlockSpec((t, D), lambda i: (i, 0))],
        out_shape=[SDS((s, D), BF16), SDS((s, D), F32)],
        scratch_shapes=[pltpu.VMEM((t + 8, D), F32), pltpu.VMEM((t, D), F32), pltpu.VMEM((t, D), F32),
                        pltpu.VMEM((8, D), F32)],
        compiler_params=_cp(1))(proj, proj, lvec, wa, wx)


def t5_bucket_table():
    qi = np.arange(BLOCK)[:, None]
    ki = np.arange(2 * BLOCK)[None, :]
    rel = qi + BLOCK - ki
    relc = np.maximum(rel, 0)
    max_exact = N_BUCKETS // 2
    relf = np.maximum(relc, 1).astype(np.float32)
    large = max_exact + (np.log(relf / np.float32(max_exact)) / np.float32(math.log(MAX_DISTANCE / max_exact))
                         * np.float32(N_BUCKETS - max_exact)).astype(np.int32)
    large = np.minimum(large, N_BUCKETS - 1)
    bucket = np.where(relc < max_exact, relc, large)
    bucket = np.where((rel >= 0) & (rel < BLOCK), bucket, -1)
    return jnp.asarray(bucket.reshape(1, BLOCK * 2 * BLOCK), jnp.int32)


def bias_band(rel_bias_t, buckets):
    n = BLOCK * 2 * BLOCK
    tn = 4096

    def body(bk_ref, rb_ref, o_ref):
        row = lax.broadcasted_iota(jnp.int32, (N_BUCKETS, tn), 0)
        oh = jnp.where(row == bk_ref[...], 1.0, 0.0).astype(BF16)
        rb = rb_ref[...]
        p0 = rb.astype(BF16)
        r1 = rb - p0.astype(F32)
        p1 = r1.astype(BF16)
        p2 = (r1 - p1.astype(F32)).astype(BF16)
        o_ref[...] = _dot(p0, oh) + _dot(p1, oh) + _dot(p2, oh)

    return pl.pallas_call(
        body, name="bias_band", grid=(n // tn,),
        in_specs=[pl.BlockSpec((1, tn), lambda i: (0, i)), pl.BlockSpec((N_HEADS, N_BUCKETS), lambda i: (0, 0))],
        out_specs=pl.BlockSpec((N_HEADS, tn), lambda i: (0, i)),
        out_shape=SDS((N_HEADS, n), F32), compiler_params=_cp(1))(buckets, rel_bias_t)


def bias_band_bwd(dband, buckets):
    n = BLOCK * 2 * BLOCK
    tn = 4096

    def body(bk_ref, d_ref, o_ref):
        @pl.when(pl.program_id(0) == 0)
        def _():
            o_ref[...] = jnp.zeros_like(o_ref)
        row = lax.broadcasted_iota(jnp.int32, (N_BUCKETS, tn), 0)
        oh = jnp.where(row == bk_ref[...], 1.0, 0.0).astype(BF16)
        dv = d_ref[...]
        p0 = dv.astype(BF16)
        r1 = dv - p0.astype(F32)
        p1 = r1.astype(BF16)
        p2 = (r1 - p1.astype(F32)).astype(BF16)
        o_ref[...] += _dot_nt(oh, p0) + _dot_nt(oh, p1) + _dot_nt(oh, p2)

    return pl.pallas_call(
        body, name="bias_band_bwd", grid=(n // tn,),
        in_specs=[pl.BlockSpec((1, tn), lambda i: (0, i)), pl.BlockSpec((N_HEADS, tn), lambda i: (0, i))],
        out_specs=pl.BlockSpec((N_BUCKETS, N_HEADS), lambda i: (0, 0)),
        out_shape=SDS((N_BUCKETS, N_HEADS), F32), compiler_params=_cp(1))(buckets, dband)


def _dup_half(band, which):
    lane = lax.broadcasted_iota(jnp.int32, band.shape, 1)
    rolled = pltpu.roll(band, 64, 1)
    keep = (lane < 64) if which == 0 else (lane >= 64)
    return jnp.where(keep, band, rolled)


def _attn_probs(scores, bias, sink, valid):
    sc = jnp.where(valid, scores * SCALE + bias, NEG_INF)
    m = jnp.maximum(jnp.max(sc, axis=-1, keepdims=True), sink)
    e = jnp.exp(sc - m)
    es = jnp.exp(sink - m)
    inv = 1.0 / (jnp.sum(e, axis=-1, keepdims=True) + es)
    return e * inv, es * inv


def _stack_heads(src_ref, kv, dst):
    lane = lax.broadcasted_iota(jnp.int32, (BLOCK, 128), 1)
    for jj in range(4):
        slab = src_ref[:, (4 * kv + jj) * 128:(4 * kv + jj + 1) * 128]
        for hh in range(2):
            keep = (lane < 64) if hh == 0 else (lane >= 64)
            dst[pl.ds((2 * jj + hh) * BLOCK, BLOCK), :] = jnp.where(keep, slab, jnp.zeros_like(slab))


def _unstack_heads(stacked, dst_ref, kv):
    lane = lax.broadcasted_iota(jnp.int32, (BLOCK, 128), 1)
    for jj in range(4):
        lo = stacked[(2 * jj) * BLOCK:(2 * jj + 1) * BLOCK]
        hi = stacked[(2 * jj + 1) * BLOCK:(2 * jj + 2) * BLOCK]
        dst_ref[:, (4 * kv + jj) * 128:(4 * kv + jj + 1) * 128] = jnp.where(lane < 64, lo, hi).astype(dst_ref.dtype)


def _band_valid(n):
    qi = lax.broadcasted_iota(jnp.int32, (BLOCK, 2 * BLOCK), 0)
    ki = lax.broadcasted_iota(jnp.int32, (BLOCK, 2 * BLOCK), 1)
    rel = qi + BLOCK - ki
    return (rel >= 0) & (rel < BLOCK) & ((ki >= BLOCK) | (n > 0))


def _kv_bands(prev_ref, cur_ref):
    band = jnp.concatenate([prev_ref[...].astype(F32), cur_ref[...].astype(F32)], axis=0)
    return [_dup_half(band, 0).astype(BF16), _dup_half(band, 1).astype(BF16)]


def attn_fwd(proj, band, sinks):
    s = proj.shape[0]
    nb = s // BLOCK
    qw = 1024

    def body(sk_ref, q_ref, kp_ref, kc_ref, vp_ref, vc_ref, b_ref, o_ref, qs_buf, s_buf, p_buf):
        n = pl.program_id(0)
        gp = pl.program_id(1)
        valid = _band_valid(n)
        kks = _kv_bands(kp_ref, kc_ref)
        vvs = _kv_bands(vp_ref, vc_ref)
        for kv in range(2):
            _stack_heads(q_ref, kv, qs_buf)
            s_buf[...] = _dot_nt(qs_buf[...], kks[kv])
            for hq in range(8):
                hl = 8 * kv + hq
                rows = pl.ds(hq * BLOCK, BLOCK)
                p, _ = _attn_probs(s_buf[rows, :], b_ref[hl], sk_ref[gp * 16 + hl], valid)
                p_buf[rows, :] = p.astype(BF16)
            _unstack_heads(_dot(p_buf[...], vvs[kv]), o_ref, kv)

    kb, vb = OFF_K // 128, OFF_V // 128
    return pl.pallas_call(
        body, name="attn_fwd", grid=(nb, 2),
        in_specs=[pl.BlockSpec(memory_space=pltpu.SMEM),
                  pl.BlockSpec((BLOCK, qw), lambda n, g: (n, OFF_Q // qw + g)),
                  pl.BlockSpec((BLOCK, 128), lambda n, g: (jnp.maximum(n - 1, 0), kb + g)),
                  pl.BlockSpec((BLOCK, 128), lambda n, g: (n, kb + g)),
                  pl.BlockSpec((BLOCK, 128), lambda n, g: (jnp.maximum(n - 1, 0), vb + g)),
                  pl.BlockSpec((BLOCK, 128), lambda n, g: (n, vb + g)),
                  pl.BlockSpec((16, BLOCK, 2 * BLOCK), lambda n, g: (g, 0, 0))],
        out_specs=pl.BlockSpec((BLOCK, qw), lambda n, g: (n, g)),
        out_shape=SDS((s, D), BF16),
        scratch_shapes=[pltpu.VMEM((8 * BLOCK, 128), BF16), pltpu.VMEM((8 * BLOCK, 2 * BLOCK), F32),
                        pltpu.VMEM((8 * BLOCK, 2 * BLOCK), BF16)],
        compiler_params=_cp(2))(sinks, proj, proj, proj, proj, proj, band)


def merge_fwd(ya, att, w_lru_out, w_attn_out, proj):
    s = ya.shape[0]
    tm, tn = min(1024, s), 512

    def body(ya_ref, at_ref, wl_ref, wt_ref, ga_ref, gb_ref, yab_ref, mg_ref):
        y_a = _dot(ya_ref[...], wl_ref[...])
        y_b = _dot(at_ref[...], wt_ref[...])
        yab_ref[0] = y_a.astype(BF16)
        yab_ref[1] = y_b.astype(BF16)
        mg_ref[...] = (_sigmoid(ga_ref[...].astype(F32)) * y_a + _sigmoid(gb_ref[...].astype(F32)) * y_b).astype(BF16)

    return pl.pallas_call(
        body, name="merge_fwd", grid=(s // tm, D // tn),
        in_specs=[pl.BlockSpec((tm, D), lambda i, j: (i, 0)), pl.BlockSpec((tm, D), lambda i, j: (i, 0)),
                  pl.BlockSpec((D, tn), lambda i, j: (0, j)), pl.BlockSpec((D, tn), lambda i, j: (0, j)),
                  pl.BlockSpec((tm, tn), lambda i, j: (i, OFF_GA // tn + j)),
                  pl.BlockSpec((tm, tn), lambda i, j: (i, OFF_GB // tn + j))],
        out_specs=[pl.BlockSpec((2, tm, tn), lambda i, j: (0, i, j)), pl.BlockSpec((tm, tn), lambda i, j: (i, j))],
        out_shape=[SDS((2, s, D), BF16), SDS((s, D), BF16)],
        compiler_params=_cp(2))(ya, att, w_lru_out, w_attn_out, proj, proj)


def outproj_fwd(merged, w_out, x, vecs):
    s = x.shape[0]
    tm, tn = min(1024, s), 512

    def body(m_ref, w_ref, x_ref, v_ref, x1_ref, o1_ref):
        o1 = _dot(m_ref[...], w_ref[...])
        o1_ref[...] = o1.astype(BF16)
        x1_ref[...] = x_ref[...] + v_ref[V_GATE1:V_GATE1 + 1, :] * o1

    return pl.pallas_call(
        body, name="outproj_fwd", grid=(s // tm, D // tn),
        in_specs=[pl.BlockSpec((tm, D), lambda i, j: (i, 0)), pl.BlockSpec((D, tn), lambda i, j: (0, j)),
                  pl.BlockSpec((tm, tn), lambda i, j: (i, j)), pl.BlockSpec((16, tn), lambda i, j: (0, j))],
        out_specs=[pl.BlockSpec((tm, tn), lambda i, j: (i, j)), pl.BlockSpec((tm, tn), lambda i, j: (i, j))],
        out_shape=[SDS((s, D), F32), SDS((s, D), BF16)],
        compiler_params=_cp(2))(merged, w_out, x, vecs)


def ff1_fwd(x1, vecs, w_ff1):
    s = x1.shape[0]
    tm, tn = min(1024, s), 512
    per = D // tn

    def body(x_ref, v_ref, w_ref, f_ref, h_ref):
        @pl.when(pl.program_id(1) == 0)
        def _():
            _modulated_norm(x_ref, v_ref, V_G2, V_SCALE2, V_SHIFT2, h_ref, tm)
        f_ref[...] = _dot(h_ref[...], w_ref[...]).astype(BF16)

    return pl.pallas_call(
        body, name="ff1_fwd", grid=(s // tm, D_FF // tn),
        in_specs=[pl.BlockSpec((tm, D), lambda i, j: (i, 0)), pl.BlockSpec((16, D), lambda i, j: (0, 0)),
                  pl.BlockSpec((None, D, tn), lambda i, j: (j // per, 0, j % per))],
        out_specs=[pl.BlockSpec((tm, tn), lambda i, j: (i, j)), pl.BlockSpec((tm, D), lambda i, j: (i, 0))],
        out_shape=[SDS((s, D_FF), BF16), SDS((s, D), BF16)],
        compiler_params=_cp(2))(x1, vecs, w_ff1)


def ff2_loss(f, w_ff2, x1, tgt, vecs):
    s = x1.shape[0]
    tm, tk = min(512, s), 1024
    nk = D_FF // tk

    def body(f_ref, w_ref, x1_hbm, t_hbm, v_ref, dx2_ref, do2_ref, sums_ref, loss_ref, acc, x1_ref, t_ref, sems):
        i, k = pl.program_id(0), pl.program_id(1)
        fetches = _row_fetches((x1_hbm, t_hbm), (x1_ref, t_ref), sems, i, tm)

        @pl.when((i == 0) & (k == 0))
        def _():
            sums_ref[...] = jnp.zeros_like(sums_ref)
            loss_ref[...] = jnp.zeros_like(loss_ref)

        @pl.when(k == 0)
        def _():
            acc[...] = jnp.zeros_like(acc)
            for cp in fetches:
                cp.start()

        fv = jnp.maximum(f_ref[...].astype(F32), 0.0)
        acc[...] += _dot((fv * fv).astype(BF16), w_ref[...])

        @pl.when(k == nk - 1)
        def _():
            for cp in fetches:
                cp.wait()
            gate2 = v_ref[V_GATE2:V_GATE2 + 1, :]
            g3 = v_ref[V_G3:V_G3 + 1, :]

            def sub(rb, carry):
                rs = pl.ds(pl.multiple_of(rb * SUB, SUB), SUB)
                o2 = acc[rs, :]
                x2 = x1_ref[rs, :] + gate2 * o2
                r3, xh = _rms_parts(x2)
                e = xh * g3 - t_ref[rs, :]
                loss_ref[...] += (0.5 / D) * jnp.sum(e * e)
                dy = e * (1.0 / D)
                sums_ref[0:1, :] += jnp.sum(dy * xh, axis=0, keepdims=True)
                dxh = dy * g3
                dx2 = r3 * (dxh - xh * jnp.mean(dxh * xh, axis=-1, keepdims=True))
                sums_ref[1:2, :] += jnp.sum(dx2 * o2, axis=0, keepdims=True)
                dx2_ref[rs, :] = dx2
                do2_ref[rs, :] = (dx2 * gate2).astype(BF16)
                return carry

            lax.fori_loop(0, tm // SUB, sub, 0)

    return pl.pallas_call(
        body, name="ff2_loss", grid=(s // tm, nk),
        in_specs=[pl.BlockSpec((tm, tk), lambda i, k: (i, k)), pl.BlockSpec((tk, D), lambda i, k: (k, 0)),
                  pl.BlockSpec(memory_space=pl.ANY), pl.BlockSpec(memory_space=pl.ANY),
                  pl.BlockSpec((16, D), lambda i, k: (0, 0))],
        out_specs=[pl.BlockSpec((tm, D), lambda i, k: (i, 0)), pl.BlockSpec((tm, D), lambda i, k: (i, 0)),
                   pl.BlockSpec((8, D), lambda i, k: (0, 0)), pl.BlockSpec((8, 128), lambda i, k: (0, 0))],
        out_shape=[SDS((s, D), F32), SDS((s, D), BF16), SDS((8, D), F32), SDS((8, 128), F32)],
        scratch_shapes=[pltpu.VMEM((tm, D), F32), pltpu.VMEM((tm, D), F32), pltpu.VMEM((tm, D), F32),
                        pltpu.SemaphoreType.DMA((2,))],
        compiler_params=_cp(2))(f, w_ff2, x1, tgt, vecs)


def ff2_bwd(do2, w_ff2, f):
    s = do2.shape[0]
    tm, tn = min(1024, s), 512

    def body(d_ref, w_ref, f_ref, o_ref):
        dff = _dot_nt(d_ref[...], w_ref[...])
        o_ref[...] = (dff * (2.0 * jnp.maximum(f_ref[...].astype(F32), 0.0))).astype(BF16)

    return pl.pallas_call(
        body, name="ff2_bwd", grid=(s // tm, D_FF // tn),
        in_specs=[pl.BlockSpec((tm, D), lambda i, j: (i, 0)), pl.BlockSpec((tn, D), lambda i, j: (j, 0)),
                  pl.BlockSpec((tm, tn), lambda i, j: (i, j))],
        out_specs=pl.BlockSpec((tm, tn), lambda i, j: (i, j)),
        out_shape=SDS((s, D_FF), BF16), compiler_params=_cp(2))(do2, w_ff2, f)


def weight_grad(name, a, b, tn, out_shape, out_block, out_map, relu2=False, b_part=0):
    s, m = a.shape
    n = b.shape[1]
    tm = WG_TM
    chunk = min(1024, s)
    nch = s // chunk

    def body(a_hbm, b_ref, o_ref, a_buf, at_s, sem):
        i = pl.program_id(0)

        @pl.when(pl.program_id(1) == 0)
        def _():
            def fetch(ch):
                return pltpu.make_async_copy(a_hbm.at[pl.ds(ch * chunk, chunk), pl.ds(i * tm, tm)],
                                             a_buf.at[ch % 2], sem.at[ch % 2])
            fetch(0).start()
            for ch in range(nch):
                if ch + 1 < nch:
                    fetch(ch + 1).start()
                fetch(ch).wait()
                av = a_buf[ch % 2]
                if relu2:
                    fv = jnp.maximum(av.astype(F32), 0.0)
                    av = (fv * fv).astype(BF16)
                at_s[:, ch * chunk:(ch + 1) * chunk] = av.T

        o_ref[...] = _dot(at_s[...], b_ref[...]).astype(BF16)

    return pl.pallas_call(
        body, name=name, grid=(m // tm, n // tn),
        in_specs=[pl.BlockSpec(memory_space=pl.ANY), pl.BlockSpec((s, tn), lambda i, j: (b_part, j))],
        out_specs=pl.BlockSpec(out_block, lambda i, j: out_map(i, j)),
        out_shape=SDS(out_shape, BF16),
        scratch_shapes=[pltpu.VMEM((2, chunk, tm), BF16), pltpu.VMEM((tm, s), BF16), pltpu.SemaphoreType.DMA((2,))],
        compiler_params=_cp(2))(a, b)


def ff1_bwd(df, w_ff1, x1, dx2, o1, vecs):
    s = df.shape[0]
    tm, tk = min(512, s), 1024
    nk = D_FF // tk
    per = D // tk

    def body(d_ref, w_ref, x1_hbm, dx2_hbm, o1_hbm, v_ref, dx1_ref, do1_ref, sums_ref, acc, x1_ref, dx2_ref, o1_ref, sems):
        i, k = pl.program_id(0), pl.program_id(1)
        fetches = _row_fetches((x1_hbm, dx2_hbm, o1_hbm), (x1_ref, dx2_ref, o1_ref), sems, i, tm)

        @pl.when((i == 0) & (k == 0))
        def _():
            sums_ref[...] = jnp.zeros_like(sums_ref)

        @pl.when(k == 0)
        def _():
            acc[...] = jnp.zeros_like(acc)
            for cp in fetches:
                cp.start()

        acc[...] += _dot_nt(d_ref[...], w_ref[...])

        @pl.when(k == nk - 1)
        def _():
            for cp in fetches:
                cp.wait()
            g2 = v_ref[V_G2:V_G2 + 1, :]
            scale2 = v_ref[V_SCALE2:V_SCALE2 + 1, :]
            gate1 = v_ref[V_GATE1:V_GATE1 + 1, :]

            def sub(rb, carry):
                rs = pl.ds(pl.multiple_of(rb * SUB, SUB), SUB)
                dh = acc[rs, :]
                r2, xh = _rms_parts(x1_ref[rs, :])
                sums_ref[0:1, :] += jnp.sum(dh, axis=0, keepdims=True)
                sums_ref[1:2, :] += jnp.sum(dh * (xh * g2), axis=0, keepdims=True)
                dxn = dh * (1.0 + scale2)
                sums_ref[2:3, :] += jnp.sum(dxn * xh, axis=0, keepdims=True)
                dxh = dxn * g2
                dx1 = dx2_ref[rs, :] + r2 * (dxh - xh * jnp.mean(dxh * xh, axis=-1, keepdims=True))
                sums_ref[3:4, :] += jnp.sum(dx1 * o1_ref[rs, :].astype(F32), axis=0, keepdims=True)
                dx1_ref[rs, :] = dx1
                do1_ref[rs, :] = (dx1 * gate1).astype(BF16)
                return carry

            lax.fori_loop(0, tm // SUB, sub, 0)

    return pl.pallas_call(
        body, name="ff1_bwd", grid=(s // tm, nk),
        in_specs=[pl.BlockSpec((tm, tk), lambda i, k: (i, k)),
                  pl.BlockSpec((None, D, tk), lambda i, k: (k // per, 0, k % per)),
                  pl.BlockSpec(memory_space=pl.ANY), pl.BlockSpec(memory_space=pl.ANY),
                  pl.BlockSpec(memory_space=pl.ANY), pl.BlockSpec((16, D), lambda i, k: (0, 0))],
        out_specs=[pl.BlockSpec((tm, D), lambda i, k: (i, 0)), pl.BlockSpec((tm, D), lambda i, k: (i, 0)),
                   pl.BlockSpec((8, D), lambda i, k: (0, 0))],
        out_shape=[SDS((s, D), F32), SDS((s, D), BF16), SDS((8, D), F32)],
        scratch_shapes=[pltpu.VMEM((tm, D), F32), pltpu.VMEM((tm, D), F32), pltpu.VMEM((tm, D), F32),
                        pltpu.VMEM((tm, D), BF16), pltpu.SemaphoreType.DMA((3,))],
        compiler_params=_cp(2))(df, w_ff1, x1, dx2, o1, vecs)


def outproj_bwd(do1, w_out, yab, proj):
    s = do1.shape[0]
    tm, tn = min(1024, s), 512
    per = D // tn

    def body(d_ref, w_ref, y_ref, g_ref, dy_ref, dp_ref):
        dm = _dot_nt(d_ref[...], w_ref[...])
        sg = _sigmoid(g_ref[...].astype(F32))
        dy_ref[...] = (dm * sg).astype(BF16)
        dp_ref[...] = (dm * y_ref[...].astype(F32) * sg * (1.0 - sg)).astype(BF16)

    return pl.pallas_call(
        body, name="outproj_bwd", grid=(s // tm, 2 * per),
        in_specs=[pl.BlockSpec((tm, D), lambda i, j: (i, 0)), pl.BlockSpec((tn, D), lambda i, j: (j % per, 0)),
                  pl.BlockSpec((None, tm, tn), lambda i, j: (j // per, i, j % per)),
                  pl.BlockSpec((tm, tn), lambda i, j: (i, OFF_GA // tn + j))],
        out_specs=[pl.BlockSpec((None, tm, tn), lambda i, j: (j // per, i, j % per)),
                   pl.BlockSpec((tm, tn), lambda i, j: (i, OFF_GA // tn + j))],
        out_shape=[SDS((2, s, D), BF16), SDS((s, IN_W), BF16)],
        compiler_params=_cp(2))(do1, w_out, yab, proj)


def lruout_bwd(dyab, w_lru_out, rec, proj, dproj):
    s = rec.shape[0]
    tm, tn = min(1024, s), 512

    def body(d_ref, w_ref, r_ref, g_ref, dp_in, dr_ref, dp_ref):
        dya = _dot_nt(d_ref[...], w_ref[...])
        gate = g_ref[...].astype(F32)
        dr_ref[...] = dya * _gelu(gate)
        dp_ref[...] = (dya * r_ref[...] * _gelu_grad(gate)).astype(BF16)

    return pl.pallas_call(
        body, name="lruout_bwd", grid=(s // tm, D // tn),
        in_specs=[pl.BlockSpec((None, tm, D), lambda i, j: (0, i, 0)), pl.BlockSpec((tn, D), lambda i, j: (j, 0)),
                  pl.BlockSpec((tm, tn), lambda i, j: (i, j)),
                  pl.BlockSpec((tm, tn), lambda i, j: (i, OFF_GATE // tn + j)),
                  pl.BlockSpec(memory_space=pl.ANY)],
        out_specs=[pl.BlockSpec((tm, tn), lambda i, j: (i, j)),
                   pl.BlockSpec((tm, tn), lambda i, j: (i, OFF_GATE // tn + j))],
        out_shape=[SDS((s, D), F32), SDS((s, IN_W), BF16)],
        input_output_aliases={4: 1},
        compiler_params=_cp(2))(dyab, w_lru_out, rec, proj, dproj)


def attnout_bwd(dyab, w_attn_out):
    s = dyab.shape[1]
    tm, tn = min(1024, s), 512

    def body(d_ref, w_ref, o_ref):
        o_ref[...] = _dot_nt(d_ref[...], w_ref[...]).astype(BF16)

    return pl.pallas_call(
        body, name="attnout_bwd", grid=(s // tm, D // tn),
        in_specs=[pl.BlockSpec((None, tm, D), lambda i, j: (1, i, 0)), pl.BlockSpec((tn, D), lambda i, j: (j, 0))],
        out_specs=pl.BlockSpec((tm, tn), lambda i, j: (i, j)),
        out_shape=SDS((s, D), BF16), compiler_params=_cp(2))(dyab, w_attn_out)


def attn_bwd(proj, band, sinks, datt, dproj):
    s = proj.shape[0]
    nb = s // BLOCK
    qw = 1024

    def body(sk_ref, q_ref, kp_ref, kc_ref, vp_ref, vc_ref, b_ref, do_ref, dp_in,
             dq_ref, dkb_ref, dvb_ref, db_ref, ds_ref, qs_buf, dos_buf, s_buf, dp_buf, p_buf, dsc_buf):
        gp = pl.program_id(0)
        n = pl.program_id(1)

        @pl.when(n == 0)
        def _():
            db_ref[...] = jnp.zeros_like(db_ref)
            ds_ref[...] = jnp.zeros_like(ds_ref)

        valid = _band_valid(n)
        kks = _kv_bands(kp_ref, kc_ref)
        vvs = _kv_bands(vp_ref, vc_ref)
        lane_b = lax.broadcasted_iota(jnp.int32, (2 * BLOCK, 128), 1)
        dks, dvs = [], []
        for kv in range(2):
            _stack_heads(q_ref, kv, qs_buf)
            _stack_heads(do_ref, kv, dos_buf)
            s_buf[...] = _dot_nt(qs_buf[...], kks[kv])
            dp_buf[...] = _dot_nt(dos_buf[...], vvs[kv])
            for hq in range(8):
                hl = 8 * kv + hq
                rows = pl.ds(hq * BLOCK, BLOCK)
                p, ps = _attn_probs(s_buf[rows, :], b_ref[hl], sk_ref[gp * 16 + hl], valid)
                dp = dp_buf[rows, :]
                delta = jnp.sum(p * dp, axis=-1, keepdims=True)
                dsc = p * (dp - delta)
                db_ref[hl] += dsc
                ds_ref[hl:hl + 1, :] += jnp.zeros((1, 128), F32) - jnp.sum(ps * delta)
                p_buf[rows, :] = p.astype(BF16)
                dsc_buf[rows, :] = (dsc * SCALE).astype(BF16)
            _unstack_heads(_dot(dsc_buf[...], kks[kv]), dq_ref, kv)
            dk = _dot_tn(dsc_buf[...], qs_buf[...])
            dv = _dot_tn(p_buf[...], dos_buf[...])
            dks.append(dk + pltpu.roll(dk, 64, 1))
            dvs.append(dv + pltpu.roll(dv, 64, 1))
        dkb_ref[...] = jnp.where(lane_b < 64, dks[0], dks[1])
        dvb_ref[...] = jnp.where(lane_b < 64, dvs[0], dvs[1])

    kb, vb = OFF_K // 128, OFF_V // 128
    return pl.pallas_call(
        body, name="attn_bwd", grid=(2, nb),
        in_specs=[pl.BlockSpec(memory_space=pltpu.SMEM),
                  pl.BlockSpec((BLOCK, qw), lambda g, n: (n, OFF_Q // qw + g)),
                  pl.BlockSpec((BLOCK, 128), lambda g, n: (jnp.maximum(n - 1, 0), kb + g)),
                  pl.BlockSpec((BLOCK, 128), lambda g, n: (n, kb + g)),
                  pl.BlockSpec((BLOCK, 128), lambda g, n: (jnp.maximum(n - 1, 0), vb + g)),
                  pl.BlockSpec((BLOCK, 128), lambda g, n: (n, vb + g)),
                  pl.BlockSpec((16, BLOCK, 2 * BLOCK), lambda g, n: (g, 0, 0)),
                  pl.BlockSpec((BLOCK, qw), lambda g, n: (n, g)),
                  pl.BlockSpec(memory_space=pl.ANY)],
        out_specs=[pl.BlockSpec((BLOCK, qw), lambda g, n: (n, OFF_Q // qw + g)),
                   pl.BlockSpec((2 * BLOCK, 128), lambda g, n: (n, g)),
                   pl.BlockSpec((2 * BLOCK, 128), lambda g, n: (n, g)),
                   pl.BlockSpec((16, BLOCK, 2 * BLOCK), lambda g, n: (g, 0, 0)),
                   pl.BlockSpec((16, 128), lambda g, n: (g, 0))],
        out_shape=[SDS((s, IN_W), BF16), SDS((nb * 2 * BLOCK, 256), F32), SDS((nb * 2 * BLOCK, 256), F32),
                   SDS((N_HEADS, BLOCK, 2 * BLOCK), F32), SDS((N_HEADS, 128), F32)],
        input_output_aliases={8: 0},
        scratch_shapes=[pltpu.VMEM((8 * BLOCK, 128), BF16), pltpu.VMEM((8 * BLOCK, 128), BF16),
                        pltpu.VMEM((8 * BLOCK, 2 * BLOCK), F32), pltpu.VMEM((8 * BLOCK, 2 * BLOCK), F32),
                        pltpu.VMEM((8 * BLOCK, 2 * BLOCK), BF16), pltpu.VMEM((8 * BLOCK, 2 * BLOCK), BF16)],
        compiler_params=_cp(2))(sinks, proj, proj, proj, proj, proj, band, datt, dproj)


def dkv_combine(dkb, dvb, dproj):
    nb = dkb.shape[0] // (2 * BLOCK)
    s = nb * BLOCK
    dkb3 = dkb.reshape(nb, 2 * BLOCK, 256)
    dvb3 = dvb.reshape(nb, 2 * BLOCK, 256)

    def body(k1, k2, v1, v2, dp_in, o_ref):
        nxt = jnp.where(pl.program_id(0) < nb - 1, 1.0, 0.0)
        o_ref[:, 0:256] = (k1[...] + nxt * k2[...]).astype(BF16)
        o_ref[:, 256:512] = (v1[...] + nxt * v2[...]).astype(BF16)

    spec1 = pl.BlockSpec((None, BLOCK, 256), lambda m: (m, 1, 0))
    spec2 = pl.BlockSpec((None, BLOCK, 256), lambda m: (jnp.minimum(m + 1, nb - 1), 0, 0))
    return pl.pallas_call(
        body, name="dkv_combine", grid=(nb,),
        in_specs=[spec1, spec2, spec1, spec2, pl.BlockSpec(memory_space=pl.ANY)],
        out_specs=pl.BlockSpec((BLOCK, 512), lambda m: (m, OFF_K // 512)),
        out_shape=SDS((s, IN_W), BF16), input_output_aliases={4: 0},
        compiler_params=_cp(1))(dkb3, dkb3, dvb3, dvb3, dproj)


def lru_bwd(proj, rec, drec, lvec, wa, wx, dproj):
    s = proj.shape[0]
    t = min(256, s)
    nt = s // t

    def body(lx_ref, lxh_ref, rec_ref, rech_ref, dr_ref, lv_ref, wa_ref, wx_ref, dp_in,
             dlx_ref, sums_ref, dwa_ref, dwx_ref,
             xbuf, hbuf, dxbuf, a_s, dh_s, xc_s, r_s, ig_s, mu_s, gc):
        step_i = pl.program_id(0)
        ti = nt - 1 - step_i

        @pl.when(step_i == 0)
        def _():
            sums_ref[...] = jnp.zeros_like(sums_ref)
            dwa_ref[...] = jnp.zeros_like(dwa_ref)
            dwx_ref[...] = jnp.zeros_like(dwx_ref)
            dxbuf[pl.ds(t, 8), :] = jnp.zeros((8, D), F32)
            gc[...] = jnp.zeros((8, D), F32)

        live = jnp.where(ti > 0, 1.0, 0.0)
        xbuf[pl.ds(0, 8), :] = lxh_ref[...].astype(F32)[8:16] * live
        xbuf[pl.ds(8, t), :] = lx_ref[...].astype(F32)
        hbuf[pl.ds(0, 8), :] = rech_ref[...] * live
        hbuf[pl.ds(8, t), :] = rec_ref[...]
        first = (lax.broadcasted_iota(jnp.int32, (t, 128), 0) + ti * t) == 0
        for b in range(N_LRU_BLOCKS):
            cs = slice(b * 128, (b + 1) * 128)
            _, xc, _, r, ig, _, a, mult = _lru_block_fwd(xbuf, lv_ref, wa_ref, wx_ref, b, t, first)
            a_s[:, cs] = a
            xc_s[:, cs] = xc
            r_s[:, cs] = r
            ig_s[:, cs] = ig
            mu_s[:, cs] = mult

        def step(q, g):
            tt = t - 1 - q
            dh = dr_ref[pl.ds(tt, 1), :] + g
            dh_s[pl.ds(tt, 1), :] = dh
            return a_s[pl.ds(tt, 1), :] * dh

        gc[0:1, :] = lax.fori_loop(0, t, step, gc[0:1, :], unroll=8)
        for b in range(N_LRU_BLOCKS):
            cs = slice(b * 128, (b + 1) * 128)
            dh = dh_s[:, cs]
            a = a_s[:, cs]
            xc = xc_s[:, cs]
            r = r_s[:, cs]
            ig = ig_s[:, cs]
            mult = mu_s[:, cs]
            sp = _softplus(-lv_ref[L_LAM:L_LAM + 1, cs])
            lam = lv_ref[L_LAM:L_LAM + 1, cs]
            da = dh * hbuf[pl.ds(7, t), cs]
            dmult = jnp.where(first, 0.0, dh * ig * xc)
            dig = dh * mult * xc
            dxc = dh * mult * ig
            dlog_a = da * a - dmult * (a * a) / mult
            dr = dlog_a * ((-LRU_C) * sp)
            dsp = jnp.sum(dlog_a * ((-LRU_C) * r), axis=0, keepdims=True)
            dza = dr * r * (1.0 - r)
            dzx = dig * ig * (1.0 - ig)
            dzab = dza.astype(BF16)
            dzxb = dzx.astype(BF16)
            xcb = xc.astype(BF16)
            dwa_ref[b] += _dot_tn(xcb, dzab)
            dwx_ref[b] += _dot_tn(xcb, dzxb)
            dxc = dxc + _dot_nt(dzab, wa_ref[b]) + _dot_nt(dzxb, wx_ref[b])
            sums_ref[L_LAM:L_LAM + 1, cs] += dsp * (-jax.nn.sigmoid(-lam))
            sums_ref[L_BA:L_BA + 1, cs] += jnp.sum(dza, axis=0, keepdims=True)
            sums_ref[L_BX:L_BX + 1, cs] += jnp.sum(dzx, axis=0, keepdims=True)
            sums_ref[L_CB:L_CB + 1, cs] += jnp.sum(dxc, axis=0, keepdims=True)
            for kk in range(4):
                sums_ref[kk:kk + 1, cs] += jnp.sum(dxc * xbuf[pl.ds(5 + kk, t), cs], axis=0, keepdims=True)
            dxbuf[pl.ds(0, t), cs] = dxc
            dlx = (lv_ref[3:4, cs] * dxc + lv_ref[2:3, cs] * dxbuf[pl.ds(1, t), cs]
                   + lv_ref[1:2, cs] * dxbuf[pl.ds(2, t), cs] + lv_ref[0:1, cs] * dxbuf[pl.ds(3, t), cs])
            dlx_ref[:, cs] = dlx.astype(BF16)
        dxbuf[pl.ds(t, 8), :] = dxbuf[pl.ds(0, 8), :]

    rev = lambda i: nt - 1 - i
    return pl.pallas_call(
        body, name="lru_bwd", grid=(nt,),
        in_specs=[pl.BlockSpec((t, D), lambda i: (rev(i), 0)),
                  pl.BlockSpec((16, D), lambda i: (jnp.maximum(rev(i) * (t // 16) - 1, 0), 0)),
                  pl.BlockSpec((t, D), lambda i: (rev(i), 0)),
                  pl.BlockSpec((8, D), lambda i: (jnp.maximum(rev(i) * (t // 8) - 1, 0), 0)),
                  pl.BlockSpec((t, D), lambda i: (rev(i), 0)),
                  pl.BlockSpec((8, D), lambda i: (0, 0)),
                  pl.BlockSpec((N_LRU_BLOCKS, 128, 128), lambda i: (0, 0, 0)),
                  pl.BlockSpec((N_LRU_BLOCKS, 128, 128), lambda i: (0, 0, 0)),
                  pl.BlockSpec(memory_space=pl.ANY)],
        out_specs=[pl.BlockSpec((t, D), lambda i: (rev(i), 0)),
                   pl.BlockSpec((8, D), lambda i: (0, 0)),
                   pl.BlockSpec((N_LRU_BLOCKS, 128, 128), lambda i: (0, 0, 0)),
                   pl.BlockSpec((N_LRU_BLOCKS, 128, 128), lambda i: (0, 0, 0))],
        out_shape=[SDS((s, IN_W), BF16), SDS((8, D), F32), SDS((N_LRU_BLOCKS, 128, 128), F32),
                   SDS((N_LRU_BLOCKS, 128, 128), F32)],
        scratch_shapes=[pltpu.VMEM((t + 8, D), F32), pltpu.VMEM((t + 8, D), F32), pltpu.VMEM((t + 8, D), F32)]
        + [pltpu.VMEM((t, D), F32)] * 6 + [pltpu.VMEM((8, D), F32)],
        input_output_aliases={8: 0},
        compiler_params=_cp(1))(proj, proj, rec, rec, drec, lvec, wa, wx, dproj)


def inproj_bwd(dproj, w_in, x, dx1, vecs):
    s = x.shape[0]
    tm, tk = min(512, s), IN_TILE
    nk = IN_W // tk
    per = IN_SHARD // tk

    def body(d_ref, w_ref, x_hbm, dx1_hbm, v_ref, gx_ref, sums_ref, acc, x_ref, dx1_ref, sems):
        i, k = pl.program_id(0), pl.program_id(1)
        fetches = _row_fetches((x_hbm, dx1_hbm), (x_ref, dx1_ref), sems, i, tm)

        @pl.when((i == 0) & (k == 0))
        def _():
            sums_ref[...] = jnp.zeros_like(sums_ref)

        @pl.when(k == 0)
        def _():
            acc[...] = jnp.zeros_like(acc)
            for cp in fetches:
                cp.start()

        acc[...] += _dot_nt(d_ref[...], w_ref[...])

        @pl.when(k == nk - 1)
        def _():
            for cp in fetches:
                cp.wait()
            g1 = v_ref[V_G1:V_G1 + 1, :]
            scale1 = v_ref[V_SCALE1:V_SCALE1 + 1, :]

            def sub(rb, carry):
                rs = pl.ds(pl.multiple_of(rb * SUB, SUB), SUB)
                dh = acc[rs, :]
                r1, xh = _rms_parts(x_ref[rs, :])
                sums_ref[0:1, :] += jnp.sum(dh, axis=0, keepdims=True)
                sums_ref[1:2, :] += jnp.sum(dh * (xh * g1), axis=0, keepdims=True)
                dxn = dh * (1.0 + scale1)
                sums_ref[2:3, :] += jnp.sum(dxn * xh, axis=0, keepdims=True)
                dxh = dxn * g1
                gx_ref[rs, :] = dx1_ref[rs, :] + r1 * (dxh - xh * jnp.mean(dxh * xh, axis=-1, keepdims=True))
                return carry

            lax.fori_loop(0, tm // SUB, sub, 0)

    return pl.pallas_call(
        body, name="inproj_bwd", grid=(s // tm, nk),
        in_specs=[pl.BlockSpec((tm, tk), lambda i, k: (i, k)),
                  pl.BlockSpec((None, D, tk), lambda i, k: (k // per, 0, k % per)),
                  pl.BlockSpec(memory_space=pl.ANY), pl.BlockSpec(memory_space=pl.ANY),
                  pl.BlockSpec((16, D), lambda i, k: (0, 0))],
        out_specs=[pl.BlockSpec((tm, D), lambda i, k: (i, 0)), pl.BlockSpec((8, D), lambda i, k: (0, 0))],
        out_shape=[SDS((s, D), F32), SDS((8, D), F32)],
        scratch_shapes=[pltpu.VMEM((tm, D), F32), pltpu.VMEM((tm, D), F32), pltpu.VMEM((tm, D), F32),
                        pltpu.SemaphoreType.DMA((2,))],
        compiler_params=_cp(2))(dproj, w_in, x, dx1, vecs)


def mod_columns(c16, w_ada, b_cols):
    tn = 512

    def body(c_ref, w_ref, b_ref, o_ref):
        cv = c_ref[...]
        ca = (cv * jax.nn.sigmoid(cv)).astype(BF16)
        o_ref[...] = _dot(ca, w_ref[...].astype(BF16)) + b_ref[...]

    return pl.pallas_call(
        body, name="mod_columns", grid=(ADA_SHARD // tn,),
        in_specs=[pl.BlockSpec((16, D), lambda j: (0, 0)), pl.BlockSpec((D, tn), lambda j: (0, j)),
                  pl.BlockSpec((1, tn), lambda j: (0, j))],
        out_specs=pl.BlockSpec((16, tn), lambda j: (0, j)),
        out_shape=SDS((16, ADA_SHARD), F32), compiler_params=_cp(1))(c16, w_ada, b_cols)


def wada_update(c16, dmod16, w, m, v):
    tm, tn = 512, 512

    def body(c_ref, d_ref, w_ref, m_ref, v_ref, g_out, dl_out, m_out, v_out):
        cv = c_ref[...]
        ca = (cv * jax.nn.sigmoid(cv)).astype(BF16)
        g = _dot_tn(ca, d_ref[...].astype(BF16))
        dl, m2, v2 = _adamw_math(w_ref[...], g, m_ref[...], v_ref[...])
        g_out[...] = g
        dl_out[...] = dl
        m_out[...] = m2
        v_out[...] = v2

    tile = pl.BlockSpec((tm, tn), lambda i, j: (i, j))
    return pl.pallas_call(
        body, name="wada_update", grid=(D // tm, ADA_SHARD // tn),
        in_specs=[pl.BlockSpec((16, tm), lambda i, j: (0, i)), pl.BlockSpec((16, tn), lambda i, j: (0, j)),
                  tile, tile, tile],
        out_specs=[tile] * 4, out_shape=[SDS((D, ADA_SHARD), F32)] * 4,
        compiler_params=_cp(2))(c16, dmod16, w, m, v)


def adamw_big(name, w, mine, theirs, m, v, c_idx):
    r, c = w.shape
    tr = 128
    per = (r // 2) // tr

    def body(c_ref, w_ref, a_ref, b_ref, m_ref, v_ref, g_out, dl_out, m_out, v_out):
        own = (pl.program_id(0) // per) == c_ref[0]
        g = jnp.where(own, a_ref[...], b_ref[...])
        dl, m2, v2 = _adamw_math(w_ref[...], g, m_ref[...], v_ref[...])
        g_out[...] = g
        dl_out[...] = dl
        m_out[...] = m2
        v_out[...] = v2

    tile = pl.BlockSpec((tr, c), lambda i, cr: (i, 0))
    half = pl.BlockSpec((tr, c), lambda i, cr: (i % per, 0))
    gs = pltpu.PrefetchScalarGridSpec(num_scalar_prefetch=1, grid=(r // tr,),
                                      in_specs=[tile, half, half, tile, tile], out_specs=[tile] * 4)
    return pl.pallas_call(body, name=name, grid_spec=gs, out_shape=[SDS((r, c), F32)] * 4,
                          compiler_params=_cp(1))(c_idx, w, mine, theirs, m, v)


def cast_into_slot(name, w, k_idx):
    r, c = w.shape
    tr = 256

    def body(k_ref, w_ref, o_ref):
        o_ref[...] = w_ref[...].astype(BF16)

    gs = pltpu.PrefetchScalarGridSpec(
        num_scalar_prefetch=1, grid=(r // tr,),
        in_specs=[pl.BlockSpec((tr, c), lambda i, kr: (i, 0))],
        out_specs=pl.BlockSpec((None, tr, c), lambda i, kr: (kr[0], i, 0)))
    return pl.pallas_call(body, name=name, grid_spec=gs, out_shape=SDS((N_CHIPS, r, c), BF16),
                          compiler_params=_cp(1))(k_idx, w)


def adamw_small(ws, gs, ms, vs):
    n = len(ws)

    def body(*refs):
        for i in range(n):
            dl, m2, v2 = _adamw_math(refs[i][...], refs[n + i][...], refs[2 * n + i][...], refs[3 * n + i][...])
            refs[4 * n + i][...] = dl
            refs[5 * n + i][...] = m2
            refs[6 * n + i][...] = v2

    vm = pl.BlockSpec(memory_space=pltpu.VMEM)
    shapes = [SDS(w.shape, F32) for w in ws]
    outs = pl.pallas_call(
        body, name="adamw_small", in_specs=[vm] * (4 * n), out_specs=[vm] * (3 * n), out_shape=shapes * 3,
        compiler_params=pltpu.CompilerParams(vmem_limit_bytes=VMEM_LIMIT))(*ws, *gs, *ms, *vs)
    return outs[:n], outs[n:2 * n], outs[2 * n:]


def sum_devices(name, gathered):
    rows = gathered.shape[1]
    tr = min(rows, 128)

    def body(x_ref, o_ref):
        acc = x_ref[0].astype(F32)
        for d in range(1, N_DEV):
            acc = acc + x_ref[d].astype(F32)
        o_ref[...] = acc

    return pl.pallas_call(
        body, name=name, grid=(rows // tr,),
        in_specs=[pl.BlockSpec((N_DEV, tr, D), lambda i: (0, i, 0))],
        out_specs=pl.BlockSpec((tr, D), lambda i: (i, 0)),
        out_shape=SDS((rows, D), F32), compiler_params=_cp(1))(gathered)


def _mesh_pos():
    return lax.axis_index("x"), lax.axis_index("y"), lax.axis_index("c")


def _other_chips(x, y):
    return [(1 - x, y), (x, 1 - y), (1 - x, 1 - y)]


def all_gather_small(name, block):
    m_per, n = block.shape

    def body(x_ref, out_ref, send_sems, recv_sems, local_sem):
        x, y, c = _mesh_pos()
        me, sibling = (x, y, c), (x, y, 1 - c)
        chips = _other_chips(x, y)

        def rows(px, py, pc):
            return out_ref.at[pl.ds((4 * px + 2 * py + pc) * m_per, m_per), :]

        def copy(k, blk, to, src=None):
            return pltpu.make_async_remote_copy(
                src_ref=rows(*blk) if src is None else src, dst_ref=rows(*blk),
                send_sem=send_sems.at[k], recv_sem=recv_sems.at[k], device_id=to, device_id_type=MESH)

        mine = pltpu.make_async_copy(x_ref, rows(*me), local_sem)
        mine.start()
        first = [copy(0, me, sibling, src=x_ref)]
        first += [copy(1 + j, me, (*chip, c), src=x_ref) for j, chip in enumerate(chips)]
        for cp in first:
            cp.start()
        passed = [copy(4 + j, (*chip, c), sibling) for j, chip in enumerate(chips)]
        for j, chip in enumerate(chips):
            copy(1 + j, (*chip, c), me).wait_recv()
            passed[j].start()
        copy(0, sibling, me).wait_recv()
        for j, chip in enumerate(chips):
            copy(4 + j, (*chip, 1 - c), me).wait_recv()
        for cp in first + passed:
            cp.wait_send()
        mine.wait()

    vm = pl.BlockSpec(memory_space=pltpu.VMEM)
    return pl.pallas_call(
        body, name=name, out_shape=SDS((N_DEV * m_per, n), block.dtype), in_specs=[vm], out_specs=vm,
        scratch_shapes=[pltpu.SemaphoreType.DMA((7,)), pltpu.SemaphoreType.DMA((7,)), pltpu.SemaphoreType.DMA],
        compiler_params=pltpu.CompilerParams(vmem_limit_bytes=VMEM_LIMIT))(block)


def sibling_sum(name, grad, other, c_idx):
    _, r, cc = grad.shape
    h = r // 2
    tr = min(256, h)
    g4 = grad.reshape(N_CHIPS, 2, h, cc)

    def body(c_ref, a_ref, b_ref, o_ref):
        o_ref[...] = (a_ref[...].astype(F32) + b_ref[...].astype(F32)).astype(BF16)

    gs = pltpu.PrefetchScalarGridSpec(
        num_scalar_prefetch=1, grid=(N_CHIPS, h // tr),
        in_specs=[pl.BlockSpec((None, None, tr, cc), lambda s, i, cr: (s, cr[0], i, 0)),
                  pl.BlockSpec((None, tr, cc), lambda s, i, cr: (s, i, 0))],
        out_specs=pl.BlockSpec((None, tr, cc), lambda s, i, cr: (s, i, 0)))
    return pl.pallas_call(body, name=name, grid_spec=gs, out_shape=SDS((N_CHIPS, h, cc), BF16),
                          compiler_params=_cp(2))(c_idx, g4, other)


HBM_SPEC = pl.BlockSpec(memory_space=pltpu.HBM)
SEM_SPEC = pl.BlockSpec(memory_space=pltpu.SEMAPHORE)


def _side_effecting():
    return pltpu.CompilerParams(has_side_effects=pltpu.SideEffectType.DATAFLOW_SIDE_EFFECTING)


def _in_hbm(a):
    return pltpu.with_memory_space_constraint(a, pltpu.HBM)


def gather_start(name, bufs, after):
    n = len(bufs)
    halves = [w.shape[1] // 2 for w in bufs]

    def body(*refs):
        ins = refs[:n]
        send_sems, recv_sems, token = refs[n + 1], refs[n + 2], refs[-1]
        x, y, c = _mesh_pos()
        k = 2 * x + y
        for i in range(n):
            reg = ins[i].at[k, pl.ds(c * halves[i], halves[i]), :]
            for j, chip in enumerate(_other_chips(x, y)):
                pltpu.make_async_remote_copy(src_ref=reg, dst_ref=reg, send_sem=send_sems.at[3 * i + j],
                                             recv_sem=recv_sems.at[3 * i + j], device_id=(*chip, c),
                                             device_id_type=MESH).start()
        token[...] = jnp.zeros_like(token)

    outs = pl.pallas_call(
        body, name=name,
        out_shape=(pltpu.SemaphoreType.DMA((3 * n,)), pltpu.SemaphoreType.DMA((3 * n,)),
                   *[pltpu.HBM(w.shape, w.dtype) for w in bufs], SDS((8, 128), F32)),
        in_specs=[HBM_SPEC] * n + [pl.BlockSpec(memory_space=pl.ANY)],
        out_specs=(SEM_SPEC, SEM_SPEC, *[HBM_SPEC] * n, pl.BlockSpec(memory_space=pltpu.VMEM)),
        input_output_aliases={i: 2 + i for i in range(n)},
        compiler_params=_side_effecting())(*[_in_hbm(w) for w in bufs], after)
    return outs[0], outs[1], list(outs[2:2 + n]), outs[-1]


def gather_wait(name, send_sems, recv_sems, bufs, after):
    n = len(bufs)
    halves = [w.shape[1] // 2 for w in bufs]

    def body(*refs):
        ins = refs[:n]
        send_sems, recv_sems = refs[n], refs[n + 1]
        x, y, c = _mesh_pos()
        k = 2 * x + y
        for i in range(n):
            for j, chip in enumerate(_other_chips(x, y)):
                kj = 2 * chip[0] + chip[1]
                cp = pltpu.make_async_remote_copy(
                    src_ref=ins[i].at[k, pl.ds(c * halves[i], halves[i]), :],
                    dst_ref=ins[i].at[kj, pl.ds(c * halves[i], halves[i]), :],
                    send_sem=send_sems.at[3 * i + j], recv_sem=recv_sems.at[3 * i + j], device_id=(*chip, c),
                    device_id_type=MESH)
                cp.wait_send()
                cp.wait_recv()

    return pl.pallas_call(
        body, name=name, out_shape=[pltpu.HBM(w.shape, w.dtype) for w in bufs],
        in_specs=[HBM_SPEC] * n + [SEM_SPEC, SEM_SPEC, pl.BlockSpec(memory_space=pl.ANY)],
        out_specs=[HBM_SPEC] * n, input_output_aliases={i: i for i in range(n)},
        compiler_params=_side_effecting())(*bufs, send_sems, recv_sems, after)


def gather_forward(name, bufs):
    n = len(bufs)
    halves = [w.shape[1] // 2 for w in bufs]

    def body(*refs):
        outs = refs[n:2 * n]
        send_sems, recv_sems = refs[2 * n:]
        x, y, c = _mesh_pos()
        chips = _other_chips(x, y)

        def copy(i, j, half, to):
            kj = 2 * chips[j][0] + chips[j][1]
            reg = outs[i].at[kj, pl.ds(half * halves[i], halves[i]), :]
            return pltpu.make_async_remote_copy(src_ref=reg, dst_ref=reg, send_sem=send_sems.at[i, j],
                                                recv_sem=recv_sems.at[i, j], device_id=to, device_id_type=MESH)

        cps = [copy(i, j, c, (x, y, 1 - c)) for i in range(n) for j in range(3)]
        for cp in cps:
            cp.start()
        for i in range(n):
            for j in range(3):
                copy(i, j, 1 - c, (x, y, c)).wait_recv()
        for cp in cps:
            cp.wait_send()

    hbm = pl.BlockSpec(memory_space=pl.ANY)
    return pl.pallas_call(
        body, name=name, in_specs=[hbm] * n, out_specs=[hbm] * n,
        out_shape=[SDS(w.shape, w.dtype) for w in bufs], input_output_aliases={i: i for i in range(n)},
        scratch_shapes=[pltpu.SemaphoreType.DMA((n, 3)), pltpu.SemaphoreType.DMA((n, 3))])(*bufs)


def _exchange_plan(kind, srcs, zones):
    x, y, c = _mesh_pos()
    plan = []
    for src, zone in zip(srcs, zones):
        if kind == "chips":
            for j, chip in enumerate(_other_chips(x, y)):
                plan.append((src.at[2 * chip[0] + chip[1]], zone.at[j], (*chip, c)))
        elif kind == "sibling":
            h = zone.shape[1]
            plan.append((src.at[:, pl.ds((1 - c) * h, h), :], zone, (x, y, 1 - c)))
        else:
            peers = [(x, y, 1 - c)] + [(*chip, cc) for chip in _other_chips(x, y) for cc in (c, 1 - c)]
            plan += [(src, zone.at[4 * x + 2 * y + c], peer) for peer in peers]
    return plan


_COPIES_PER_ARRAY = {"chips": 3, "sibling": 1, "all": N_DEV - 1}


def _landing_zones(kind, srcs):
    if kind == "chips":
        return [lax.empty((3,) + t.shape[1:], t.dtype) for t in srcs]
    if kind == "sibling":
        return [lax.empty((t.shape[0], t.shape[1] // 2, t.shape[2]), t.dtype) for t in srcs]
    return [lax.empty((N_DEV,) + t.shape, t.dtype) for t in srcs]


def exchange_start(name, kind, srcs, after):
    n = len(srcs)
    lands = _landing_zones(kind, srcs)
    n_copies = n * _COPIES_PER_ARRAY[kind]

    def body(*refs):
        send_sems, recv_sems, token = refs[2 * n + 1], refs[2 * n + 2], refs[-1]
        for q, (src, dst, dev) in enumerate(_exchange_plan(kind, refs[:n], refs[n:2 * n])):
            pltpu.make_async_remote_copy(src_ref=src, dst_ref=dst, send_sem=send_sems.at[q], recv_sem=recv_sems.at[q],
                                         device_id=dev, device_id_type=MESH).start()
        token[...] = jnp.zeros_like(token)

    outs = pl.pallas_call(
        body, name=name,
        out_shape=(pltpu.SemaphoreType.DMA((n_copies,)), pltpu.SemaphoreType.DMA((n_copies,)),
                   *[pltpu.HBM(t.shape, t.dtype) for t in srcs], *[pltpu.HBM(t.shape, t.dtype) for t in lands],
                   SDS((8, 128), F32)),
        in_specs=[HBM_SPEC] * (2 * n) + [pl.BlockSpec(memory_space=pl.ANY)],
        out_specs=(SEM_SPEC, SEM_SPEC, *[HBM_SPEC] * (2 * n), pl.BlockSpec(memory_space=pltpu.VMEM)),
        input_output_aliases={i: 2 + i for i in range(2 * n)},
        compiler_params=_side_effecting())(*[_in_hbm(t) for t in srcs], *[_in_hbm(t) for t in lands], after)
    return outs[0], outs[1], list(outs[2:2 + n]), list(outs[2 + n:2 + 2 * n]), outs[-1]


def exchange_wait(name, kind, send_sems, recv_sems, srcs, lands, after):
    n = len(srcs)

    def body(*refs):
        send_sems, recv_sems = refs[2 * n], refs[2 * n + 1]
        for q, (src, dst, dev) in enumerate(_exchange_plan(kind, refs[:n], refs[n:2 * n])):
            cp = pltpu.make_async_remote_copy(src_ref=src, dst_ref=dst, send_sem=send_sems.at[q],
                                              recv_sem=recv_sems.at[q], device_id=dev, device_id_type=MESH)
            cp.wait_send()
            cp.wait_recv()

    outs = pl.pallas_call(
        body, name=name, out_shape=[pltpu.HBM(t.shape, t.dtype) for t in srcs + lands],
        in_specs=[HBM_SPEC] * (2 * n) + [SEM_SPEC, SEM_SPEC, pl.BlockSpec(memory_space=pl.ANY)],
        out_specs=[HBM_SPEC] * (2 * n), input_output_aliases={i: i for i in range(2 * n)},
        compiler_params=_side_effecting())(*srcs, *lands, send_sems, recv_sems, after)
    return list(outs[:n]), list(outs[n:])


def chip_sum(name, sums, parts, k_idx):
    _, h, cc = parts.shape
    tr = min(256, h)

    def body(k_ref, own_ref, p_ref, o_ref):
        acc = own_ref[...].astype(F32)
        for s in range(3):
            acc = acc + p_ref[s].astype(F32)
        o_ref[...] = acc

    gs = pltpu.PrefetchScalarGridSpec(
        num_scalar_prefetch=1, grid=(h // tr,),
        in_specs=[pl.BlockSpec((None, tr, cc), lambda i, kr: (kr[0], i, 0)),
                  pl.BlockSpec((3, tr, cc), lambda i, kr: (0, i, 0))],
        out_specs=pl.BlockSpec((tr, cc), lambda i, kr: (i, 0)))
    return pl.pallas_call(body, name=name, grid_spec=gs, out_shape=SDS((h, cc), F32),
                          compiler_params=_cp(1))(k_idx, sums, parts)


def halves_exchange(name, halves):
    n = len(halves)

    def body(*refs):
        ins, outs = refs[:n], refs[n:2 * n]
        send_sems, recv_sems = refs[2 * n:]
        x, y, c = _mesh_pos()
        cps = []
        for i in range(n):
            cp = pltpu.make_async_remote_copy(
                src_ref=ins[i], dst_ref=outs[i], send_sem=send_sems.at[i], recv_sem=recv_sems.at[i],
                device_id=(x, y, 1 - c), device_id_type=MESH)
            cp.start()
            cps.append(cp)
        for cp in cps:
            cp.wait_recv()
        for cp in cps:
            cp.wait_send()

    hbm = pl.BlockSpec(memory_space=pl.ANY)
    return pl.pallas_call(
        body, name=name, in_specs=[hbm] * n, out_specs=[hbm] * n,
        out_shape=[SDS(t.shape, F32) for t in halves],
        scratch_shapes=[pltpu.SemaphoreType.DMA((n,)), pltpu.SemaphoreType.DMA((n,))])(*halves)


def local_step(x, tgt, vecs, lvec, wa, wx, sinks, rel_bias, w_in, rest_weights, hook):
    buckets = t5_bucket_table()
    band = bias_band(rel_bias.T, buckets).reshape(N_HEADS, BLOCK, 2 * BLOCK)

    proj, h = inproj_fwd(x, vecs, w_in)
    ya, rec = lru_fwd(proj, lvec, wa, wx)
    att = attn_fwd(proj, band, sinks)
    w_lru_out, w_attn_out, w_out, w_ff1, w_ff2 = rest_weights(att[:8, :128] + ya[:8, :128])
    w_lru_out2, w_attn_out2, w_out2 = w_lru_out.reshape(D, D), w_attn_out.reshape(D, D), w_out.reshape(D, D)
    w_ff2_2 = w_ff2.reshape(D_FF, D)
    yab, merged = merge_fwd(ya, att, w_lru_out2, w_attn_out2, proj)
    x1, o1 = outproj_fwd(merged, w_out2, x, vecs)
    f, h2 = ff1_fwd(x1, vecs, w_ff1)
    dx2, do2, sums_f, loss = ff2_loss(f, w_ff2_2, x1, tgt, vecs)

    df = ff2_bwd(do2, w_ff2_2, f)
    g_ff2 = weight_grad("dw_ff2", f, do2, 512, (D_FF, D), (WG_TM, 512), lambda i, j: (i, j), relu2=True)
    dx1, do1, sums_2 = ff1_bwd(df, w_ff1, x1, dx2, o1, vecs)
    g_ff1 = weight_grad("dw_ff1", h2, df, 512, (N_CHIPS, D, D), (None, WG_TM, 512), lambda i, j: (j // 4, i, j % 4))
    dyab, dproj = outproj_bwd(do1, w_out2, yab, proj)
    g_out = weight_grad("dw_out", merged, do1, 512, (D, D), (WG_TM, 512), lambda i, j: (i, j))
    drec, dproj = lruout_bwd(dyab, w_lru_out2, rec, proj, dproj)
    dyab2 = dyab.reshape(2 * x.shape[0], D)
    g_lru_out = weight_grad("dw_lru_out", ya, dyab2, 512, (D, D), (WG_TM, 512), lambda i, j: (i, j))
    datt = attnout_bwd(dyab, w_attn_out2)
    g_attn_out = weight_grad("dw_attn_out", att, dyab2, 512, (D, D), (WG_TM, 512), lambda i, j: (i, j), b_part=1)
    zero = hook("grads_a", [g_lru_out.reshape(N_CHIPS, D // 4, D), g_attn_out.reshape(N_CHIPS, D // 4, D),
                            g_out.reshape(N_CHIPS, D // 4, D), g_ff1, g_ff2.reshape(N_CHIPS, D_FF // 4, D)])
    dproj, dkb, dvb, dband, dsink = attn_bwd(proj, band, sinks + zero, datt, dproj)
    zero = hook("after_attn_bwd", dkb)
    dproj = dkv_combine(dkb, dvb, dproj)
    dproj, sums_l, d_wa, d_wx = lru_bwd(proj, rec, drec, lvec + zero, wa, wx, dproj)
    hook("lru_grads", (d_wa, d_wx))
    per = IN_SHARD // IN_TILE
    g_in = weight_grad("dw_in", h, dproj, IN_TILE, (N_CHIPS, D, IN_SHARD), (None, WG_TM, IN_TILE),
                       lambda i, j: (j // per, i, j % per))
    zero = hook("grads_b", [g_in])
    grad_x, sums_1 = inproj_bwd(dproj, w_in, x, dx1, vecs + zero)
    d_rel_bias = bias_band_bwd(dband.reshape(N_HEADS, BLOCK * 2 * BLOCK), buckets)

    small = dict(sums_f=sums_f, sums_2=sums_2, sums_1=sums_1, sums_l=sums_l, d_wa=d_wa, d_wx=d_wx,
                 d_sinks=dsink[:, 0], d_rel_bias=d_rel_bias)
    return loss, grad_x, small


def _pad_rows(a, rows):
    return jnp.concatenate([a, jnp.zeros((rows - a.shape[0], a.shape[1]), a.dtype)], axis=0)


def kernel(x, c, w_ada, b_ada, norm1_g, w_in, conv_w, conv_b, lru_wa, lru_ba, lru_wx, lru_bx, lru_lambda, w_lru_out, w_attn_out, attn_sinks, rel_bias, w_out, norm2_g, w_ff1, w_ff2, final_g, loss_target, m_w_ada, m_b_ada, m_norm1_g, m_w_in, m_conv_w, m_conv_b, m_lru_wa, m_lru_ba, m_lru_wx, m_lru_bx, m_lru_lambda, m_w_lru_out, m_w_attn_out, m_attn_sinks, m_rel_bias, m_w_out, m_norm2_g, m_w_ff1, m_w_ff2, m_final_g, v_w_ada, v_b_ada, v_norm1_g, v_w_in, v_conv_w, v_conv_b, v_lru_wa, v_lru_ba, v_lru_wx, v_lru_bx, v_lru_lambda, v_w_lru_out, v_w_attn_out, v_attn_sinks, v_rel_bias, v_w_out, v_norm2_g, v_w_ff1, v_w_ff2, v_final_g):
    xi, yi, ci = _mesh_pos()
    chip = 2 * xi + yi
    dev = 2 * chip + ci
    z8 = jnp.zeros((8, D), F32)

    conv_rows = jnp.concatenate([conv_w[0], jnp.zeros((4, D - D // 4), F32)], axis=1)
    pack0 = jnp.concatenate([c, conv_rows, jnp.zeros((3, D), F32)], axis=0)
    g0 = all_gather_small("gather_cond", pack0).reshape(N_DEV, 8, D)
    c_all = g0[:, 0, :]
    conv_full = jnp.concatenate([g0[2 * k, 1:5, :D // 4] for k in range(N_CHIPS)], axis=1)
    c16 = jnp.concatenate([c_all, z8], axis=0)
    b_cols = lax.dynamic_slice_in_dim(b_ada, chip * ADA_SHARD, ADA_SHARD, axis=1)
    mod_c = mod_columns(c16, w_ada[0], b_cols)
    g1 = all_gather_small("gather_mod", mod_c).reshape(N_DEV, 16, ADA_SHARD)
    mod = jnp.concatenate([lax.dynamic_index_in_dim(g1[2 * k], dev, axis=0, keepdims=False) for k in range(N_CHIPS)])
    shift1, scale1, gate1, shift2, scale2, gate2 = [mod[i * D:(i + 1) * D] for i in range(6)]
    vecs = jnp.stack([norm1_g[0], scale1, shift1, gate1, norm2_g[0], scale2, shift2, gate2, final_g]
                     + [jnp.zeros((D,), F32)] * 7)
    lvec = jnp.concatenate([conv_full, conv_b, lru_ba, lru_bx, lru_lambda], axis=0)

    shards = [w_in[0], w_lru_out[0], w_attn_out[0], w_out[0], w_ff1[0], w_ff2[0]]
    names = ["w_in", "w_lru_out", "w_attn_out", "w_out", "w_ff1", "w_ff2"]
    k_idx = jnp.reshape(chip, (1,)).astype(jnp.int32)
    c_idx = jnp.reshape(ci, (1,)).astype(jnp.int32)
    in_send, in_recv, in_flight_in, _ = gather_start(
        "gather_start_in", [cast_into_slot("cast_w_in", shards[0], k_idx)], vecs)
    slots = [cast_into_slot("cast_" + nm, w, k_idx) for nm, w in zip(names[1:], shards[1:])]
    w_in_full = gather_forward("gather_forward_in", gather_wait(
        "gather_wait_in", in_send, in_recv, in_flight_in, vecs[:8, :128] + slots[-1][0, :8, :128].astype(F32)))[0]
    g_send, g_recv, in_flight, token = gather_start("gather_start_rest", slots, w_in_full)
    vecs = vecs + token[0, 0]
    pending = {}

    def rest_weights(after):
        return gather_forward("gather_forward_rest", gather_wait("gather_wait_rest", g_send, g_recv, in_flight, after))

    def reduce_hook(event, payload):
        if event == "grads_a":
            pending["sib_a"] = exchange_start("sibling_start_a", "sibling", payload, payload[0])
            return pending["sib_a"][-1][0, 0]
        if event == "lru_grads":
            pack_w = jnp.concatenate([payload[0].reshape(128, D), payload[1].reshape(128, D)], axis=0).astype(BF16)
            pending["lru_w"] = exchange_start("lru_w_grads_start", "all", [pack_w], pack_w)
            return pending["lru_w"][-1][0, 0]
        if event == "grads_b":
            pending["sib_b"] = exchange_start("sibling_start_b", "sibling", payload, pending["lru_w"][-1])
            return pending["sib_b"][-1][0, 0]
        return chips_start("a", names[1:], payload)

    def chips_start(tag, nms, after):
        send_sems, recv_sems, grads, lands, _ = pending["sib_" + tag]
        grads, lands = exchange_wait("sibling_wait_" + tag, "sibling", send_sems, recv_sems, grads, lands, after)
        sums = [sibling_sum("sibling_sum_" + nm, g, o, c_idx) for nm, g, o in zip(nms, grads, lands)]
        pending[tag] = exchange_start("exchange_start_" + tag, "chips", sums, sums[0])
        return pending[tag][-1][0, 0]

    loss_t, grad_x, small = local_step(
        x[0], loss_target[0], vecs, lvec, lru_wa[0].astype(BF16), lru_wx[0].astype(BF16),
        attn_sinks[0], rel_bias, w_in_full, rest_weights, reduce_hook)
    chips_start("b", names[:1], grad_x)

    big_m = dict(zip(names, [m_w_in, m_w_lru_out, m_w_attn_out, m_w_out, m_w_ff1, m_w_ff2]))
    big_v = dict(zip(names, [v_w_in, v_w_lru_out, v_w_attn_out, v_w_out, v_w_ff1, v_w_ff2]))
    local_w = dict(zip(names, shards))
    g_big, d_big, nm_big, nv_big = {}, {}, {}, {}

    def links_done(tag, after):
        send_sems, recv_sems, sums, lands, _ = pending[tag]
        return exchange_wait("exchange_wait_" + tag, "chips", send_sems, recv_sems, sums, lands, after)

    def finish_reduce(nms, sums, lands):
        mine = [chip_sum("chip_sum_" + nm, t, p, k_idx) for nm, t, p in zip(nms, sums, lands)]
        theirs = halves_exchange("halves_exchange_" + nms[0], mine)
        for nm, a, b in zip(nms, mine, theirs):
            g2, dl, m2, v2 = adamw_big("adamw_" + nm, local_w[nm], a, b, big_m[nm][0], big_v[nm][0], c_idx)
            g_big[nm], d_big[nm], nm_big[nm], nv_big[nm] = g2[None], dl[None], m2[None], v2[None]
        return lax.optimization_barrier(tuple(nv_big[nm] for nm in nms))[0]

    done_a = finish_reduce(names[1:], *links_done("a", pending["b"][-1]))
    sums_b, lands_b = links_done("b", done_a)
    w_send, w_recv, w_src, w_lands, _ = pending["lru_w"]
    w_src, w_lands = exchange_wait("lru_w_grads_wait", "all", w_send, w_recv, w_src, w_lands, lands_b[0])

    sums_f, sums_2, sums_1, sums_l = small["sums_f"], small["sums_2"], small["sums_1"], small["sums_l"]
    vec_rows = jnp.stack([sums_1[2], sums_2[2], sums_f[0], sums_l[L_CB], sums_l[L_BA], sums_l[L_BX],
                          sums_l[L_LAM], jnp.zeros((D,), F32)])
    mod_rows = jnp.stack([sums_1[0], sums_1[1], sums_2[3], sums_2[0], sums_2[1], sums_f[1],
                          jnp.zeros((D,), F32), jnp.zeros((D,), F32)])
    att_rows = jnp.concatenate([
        jnp.concatenate([small["d_sinks"], jnp.zeros((D - N_HEADS,), F32)])[None],
        jnp.concatenate([small["d_rel_bias"].reshape(-1), jnp.zeros((D - N_BUCKETS * N_HEADS,), F32)])[None],
        jnp.zeros((6, D), F32)], axis=0)
    pack = jnp.concatenate([vec_rows, _pad_rows(sums_l[0:4], 8), mod_rows, att_rows], axis=0)
    pack, lru_w_all = lax.optimization_barrier((pack, w_lands[0]))
    gathered = all_gather_small("gather_small_grads", pack).reshape(N_DEV, P_WA, D)
    total = sum_devices("sum_small_grads", gathered)
    total_w = sum_devices("sum_lru_w_grads", lax.dynamic_update_slice(lru_w_all, w_src[0][None], (dev, 0, 0)))
    dmod_all = gathered[:, P_MOD:P_MOD + 6, :].reshape(N_DEV, 6 * D)
    dmod16 = jnp.concatenate([lax.dynamic_slice_in_dim(dmod_all, chip * ADA_SHARD, ADA_SHARD, axis=1),
                              jnp.zeros((8, ADA_SHARD), F32)], axis=0)
    g_w_ada, d_w_ada, nm_w_ada, nv_w_ada = wada_update(c16, dmod16, w_ada[0], m_w_ada[0], v_w_ada[0])
    finish_reduce(names[:1], sums_b, lands_b)
    loss = lax.psum(lax.optimization_barrier((loss_t, total))[0][0, 0], ("x", "y", "c"))

    conv_g = lax.dynamic_slice_in_dim(total[P_CONVW:P_CONVW + 4], chip * (D // 4), D // 4, axis=1)
    sm_names = ["b_ada", "norm1_g", "conv_w", "conv_b", "lru_wa", "lru_ba", "lru_wx", "lru_bx", "lru_lambda",
                "attn_sinks", "rel_bias", "norm2_g", "final_g"]
    sm_w = [b_ada.reshape(6, D), norm1_g, conv_w[0], conv_b, lru_wa.reshape(128, D), lru_ba, lru_wx.reshape(128, D),
            lru_bx, lru_lambda, attn_sinks, rel_bias, norm2_g, final_g[None]]
    sm_m = [m_b_ada.reshape(6, D), m_norm1_g, m_conv_w[0], m_conv_b, m_lru_wa.reshape(128, D), m_lru_ba,
            m_lru_wx.reshape(128, D), m_lru_bx, m_lru_lambda, m_attn_sinks, m_rel_bias, m_norm2_g, m_final_g[None]]
    sm_v = [v_b_ada.reshape(6, D), v_norm1_g, v_conv_w[0], v_conv_b, v_lru_wa.reshape(128, D), v_lru_ba,
            v_lru_wx.reshape(128, D), v_lru_bx, v_lru_lambda, v_attn_sinks, v_rel_bias, v_norm2_g, v_final_g[None]]
    sm_g = [total[P_MOD:P_MOD + 6], total[0:1], conv_g, total[3:4], total_w[0:128], total[4:5],
            total_w[128:256], total[5:6], total[6:7], total[P_ATT:P_ATT + 1, :N_HEADS],
            total[P_ATT + 1, :N_BUCKETS * N_HEADS].reshape(N_BUCKETS, N_HEADS), total[1:2], total[2:3]]
    sm_d, sm_nm, sm_nv = adamw_small(sm_w, sm_g, sm_m, sm_v)
    shapes = dict(b_ada=b_ada.shape, norm1_g=norm1_g.shape, conv_w=conv_w.shape, conv_b=conv_b.shape,
                  lru_wa=lru_wa.shape, lru_ba=lru_ba.shape, lru_wx=lru_wx.shape, lru_bx=lru_bx.shape,
                  lru_lambda=lru_lambda.shape, attn_sinks=attn_sinks.shape, rel_bias=rel_bias.shape,
                  norm2_g=norm2_g.shape, final_g=final_g.shape)
    grads = dict(w_ada=g_w_ada[None], **g_big)
    deltas = dict(w_ada=d_w_ada[None], **d_big)
    new_m = dict(w_ada=nm_w_ada[None], **nm_big)
    new_v = dict(w_ada=nv_w_ada[None], **nv_big)
    for i, nm in enumerate(sm_names):
        grads[nm] = sm_g[i].reshape(shapes[nm])
        deltas[nm] = sm_d[i].reshape(shapes[nm])
        new_m[nm] = sm_nm[i].reshape(shapes[nm])
        new_v[nm] = sm_nv[i].reshape(shapes[nm])
    order = ["w_ada", "b_ada", "norm1_g", "w_in", "conv_w", "conv_b", "lru_wa", "lru_ba", "lru_wx", "lru_bx",
             "lru_lambda", "w_lru_out", "w_attn_out", "attn_sinks", "rel_bias", "w_out", "norm2_g", "w_ff1", "w_ff2",
             "final_g"]
    return (loss, grad_x[None], *[grads[n] for n in order], *[deltas[n] for n in order],
            *[new_m[n] for n in order], *[new_v[n] for n in order])
```

```python
import math

import numpy as np
import jax
import jax.numpy as jnp
from jax import lax
from jax.experimental import pallas as pl
from jax.experimental.pallas import tpu as pltpu

F32 = jnp.float32
BF16 = jnp.bfloat16
SDS = jax.ShapeDtypeStruct
MESH = pl.DeviceIdType.MESH

D = 2048
D_FF = 4 * D
N_HEADS = 32
HEAD_DIM = 64
BLOCK = 128
N_LRU_BLOCKS = 16
LRU_C = 8.0
EPS = 1e-6
NEG_INF = -1e30
N_BUCKETS = 32
MAX_DISTANCE = 128
IN_W = 10752
IN_SHARD = IN_W // 4
IN_TILE = 896
ADA_SHARD = 6 * D // 4
OFF_LRU, OFF_GATE, OFF_Q, OFF_K, OFF_V, OFF_GA, OFF_GB = 0, 2048, 4096, 6144, 6400, 6656, 8704
SCALE = HEAD_DIM ** -0.5
N_CHIPS = 4
N_DEV = 8

ADAM_LR, ADAM_B1, ADAM_B2, ADAM_EPS, ADAM_WD, ADAM_STEP = 0.001, 0.9, 0.999, 1e-08, 0.01, 10
ADAM_C1 = 1.0 - ADAM_B1 ** ADAM_STEP
ADAM_C2 = 1.0 - ADAM_B2 ** ADAM_STEP

VMEM_LIMIT = 52 * 2 ** 20
SUB = 128
WG_TM = 1024
V_G1, V_SCALE1, V_SHIFT1, V_GATE1, V_G2, V_SCALE2, V_SHIFT2, V_GATE2, V_G3 = range(9)
L_CW0, L_CB, L_BA, L_BX, L_LAM = 0, 4, 5, 6, 7
P_VEC, P_CONVW, P_MOD, P_ATT, P_WA = 0, 8, 16, 24, 32


def _cp(n_axes):
    return pltpu.CompilerParams(dimension_semantics=("arbitrary",) * n_axes, vmem_limit_bytes=VMEM_LIMIT)


def _dot(a, b):
    return jnp.dot(a, b, preferred_element_type=F32)


def _dot_nt(a, b):
    return lax.dot_general(a, b, (((1,), (1,)), ((), ())), preferred_element_type=F32)


def _dot_tn(a, b):
    return lax.dot_general(a, b, (((0,), (0,)), ((), ())), preferred_element_type=F32)


_G0 = math.sqrt(2.0 / math.pi)
_G1 = 0.044715


def _gelu(x):
    return 0.5 * x * (1.0 + jnp.tanh(_G0 * (x + _G1 * x * x * x)))


def _gelu_grad(x):
    x2 = x * x
    t = jnp.tanh(_G0 * (x + _G1 * x * x2))
    return 0.5 * (1.0 + t) + 0.5 * x * (1.0 - t * t) * _G0 * (1.0 + 3.0 * _G1 * x2)


def _sigmoid(x):
    return 0.5 * jnp.tanh(0.5 * x) + 0.5


def _one_minus_exp2(x):
    t = jnp.tanh(x)
    return (-2.0 * t) / (1.0 - t)


def _softplus(z):
    e = jnp.exp(-jnp.abs(z))
    u = 1.0 + e
    l1p = jnp.where(u == 1.0, e, jnp.log(u) * e / (u - 1.0))
    return jnp.maximum(z, 0.0) + l1p


def _adamw_math(w, g, m, v):
    m2 = ADAM_B1 * m + (1.0 - ADAM_B1) * g
    v2 = ADAM_B2 * v + (1.0 - ADAM_B2) * (g * g)
    m_hat = m2 / ADAM_C1
    v_hat = v2 / ADAM_C2
    delta = -ADAM_LR * (m_hat / (jnp.sqrt(v_hat) + ADAM_EPS) + ADAM_WD * w)
    return delta, m2, v2


def _rms_parts(xv):
    r = lax.rsqrt(jnp.mean(xv * xv, axis=-1, keepdims=True) + EPS)
    return r, xv * r


def _row_fetches(hbm_refs, bufs, sems, i, rows):
    return [pltpu.make_async_copy(h.at[pl.ds(i * rows, rows), :], b, sems.at[n])
            for n, (h, b) in enumerate(zip(hbm_refs, bufs))]


def _modulated_norm(x_ref, v_ref, row_g, row_scale, row_shift, h_ref, rows):
    g, scale, shift = v_ref[row_g:row_g + 1, :], v_ref[row_scale:row_scale + 1, :], v_ref[row_shift:row_shift + 1, :]

    def sub(rb, carry):
        rs = pl.ds(pl.multiple_of(rb * SUB, SUB), SUB)
        _, xh = _rms_parts(x_ref[rs, :])
        h_ref[rs, :] = ((xh * g) * (1.0 + scale) + shift).astype(BF16)
        return carry

    lax.fori_loop(0, rows // SUB, sub, 0)


def inproj_fwd(name, x, h, vecs, w_in, proj, shard):
    s = x.shape[0]
    tm = min(1024, s)
    per = IN_SHARD // IN_TILE
    first = h is None

    def body(*refs):
        if first:
            _, x_ref, v_ref, w_ref, proj_ref, h_ref = refs

            @pl.when(pl.program_id(1) == 0)
            def _():
                _modulated_norm(x_ref, v_ref, V_G1, V_SCALE1, V_SHIFT1, h_ref, tm)
        else:
            _, h_ref, w_ref, _, proj_ref = refs
        proj_ref[...] = _dot(h_ref[...], w_ref[...]).astype(BF16)

    rows = pl.BlockSpec((tm, D), lambda i, j, sr: (i, 0))
    w_spec = pl.BlockSpec((None, D, IN_TILE), lambda i, j, sr: (sr[0], 0, j))
    proj_spec = pl.BlockSpec((tm, IN_TILE), lambda i, j, sr: (i, sr[0] * per + j))
    if first:
        gs = pltpu.PrefetchScalarGridSpec(
            num_scalar_prefetch=1, grid=(s // tm, per),
            in_specs=[rows, pl.BlockSpec((16, D), lambda i, j, sr: (0, 0)), w_spec], out_specs=[proj_spec, rows])
        return pl.pallas_call(body, name=name, grid_spec=gs, out_shape=[SDS((s, IN_W), BF16), SDS((s, D), BF16)],
                              compiler_params=_cp(2))(shard, x, vecs, w_in)
    gs = pltpu.PrefetchScalarGridSpec(
        num_scalar_prefetch=1, grid=(s // tm, per),
        in_specs=[rows, w_spec, pl.BlockSpec(memory_space=pl.ANY)], out_specs=proj_spec)
    return pl.pallas_call(body, name=name, grid_spec=gs, out_shape=SDS((s, IN_W), BF16),
                          input_output_aliases={3: 0}, compiler_params=_cp(2))(shard, h, w_in, proj)


def _lru_block_fwd(xbuf, lv_ref, wa_ref, wx_ref, b, t, first):
    cs = slice(b * 128, (b + 1) * 128)
    x0 = xbuf[pl.ds(8, t), cs]
    x1 = xbuf[pl.ds(7, t), cs]
    x2 = xbuf[pl.ds(6, t), cs]
    x3 = xbuf[pl.ds(5, t), cs]
    xc = (lv_ref[L_CB:L_CB + 1, cs] + lv_ref[3:4, cs] * x0 + lv_ref[2:3, cs] * x1
          + lv_ref[1:2, cs] * x2 + lv_ref[0:1, cs] * x3)
    xcb = xc.astype(BF16)
    r = _sigmoid(_dot(xcb, wa_ref[b]) + lv_ref[L_BA:L_BA + 1, cs])
    ig = _sigmoid(_dot(xcb, wx_ref[b]) + lv_ref[L_BX:L_BX + 1, cs])
    sp = _softplus(-lv_ref[L_LAM:L_LAM + 1, cs])
    log_a = (-LRU_C) * r * sp
    a = jnp.exp(log_a)
    mult = jnp.where(first, 1.0, jnp.sqrt(_one_minus_exp2(log_a)))
    return (x0, x1, x2, x3), xc, xcb, r, ig, sp, a, mult


def lru_fwd(proj, lvec, wa, wx):
    s = proj.shape[0]
    t = min(256, s)

    def body(lx_ref, gate_ref, lv_ref, wa_ref, wx_ref, ya_ref, rec_ref, xbuf, a_s, u_s, hc):
        i = pl.program_id(0)

        @pl.when(i == 0)
        def _():
            xbuf[pl.ds(0, 8), :] = jnp.zeros((8, D), F32)
            hc[...] = jnp.zeros((8, D), F32)

        @pl.when(i > 0)
        def _():
            xbuf[pl.ds(0, 8), :] = xbuf[pl.ds(t, 8), :]

        xbuf[pl.ds(8, t), :] = lx_ref[...].astype(F32)
        first = (lax.broadcasted_iota(jnp.int32, (t, 128), 0) + i * t) == 0
        for b in range(N_LRU_BLOCKS):
            cs = slice(b * 128, (b + 1) * 128)
            _, xc, _, _, ig, _, a, mult = _lru_block_fwd(xbuf, lv_ref, wa_ref, wx_ref, b, t, first)
            a_s[:, cs] = a
            u_s[:, cs] = mult * (ig * xc)

        def step(tt, h):
            h = a_s[pl.ds(tt, 1), :] * h + u_s[pl.ds(tt, 1), :]
            rec_ref[pl.ds(tt, 1), :] = h
            return h

        hc[0:1, :] = lax.fori_loop(0, t, step, hc[0:1, :], unroll=8)
        for b in range(N_LRU_BLOCKS):
            cs = slice(b * 128, (b + 1) * 128)
            ya_ref[:, cs] = (rec_ref[:, cs] * _gelu(gate_ref[:, cs].astype(F32))).astype(BF16)

    return pl.pallas_call(
        body, name="lru_fwd", grid=(s // t,),
        in_specs=[pl.BlockSpec((t, D), lambda i: (i, OFF_LRU // D)),
                  pl.BlockSpec((t, D), lambda i: (i, OFF_GATE // D)),
                  pl.BlockSpec((8, D), lambda i: (0, 0)),
                  pl.BlockSpec((N_LRU_BLOCKS, 128, 128), lambda i: (0, 0, 0)),
                  pl.BlockSpec((N_LRU_BLOCKS, 128, 128), lambda i: (0, 0, 0))],
        out_specs=[pl.BlockSpec((t, D), lambda i: (i, 0)), pl.BlockSpec((t, D), lambda i: (i, 0))],
        out_shape=[SDS((s, D), BF16), SDS((s, D), F32)],
        scratch_shapes=[pltpu.VMEM((t + 8, D), F32), pltpu.VMEM((t, D), F32), pltpu.VMEM((t, D), F32),
                        pltpu.VMEM((8, D), F32)],
        compiler_params=_cp(1))(proj, proj, lvec, wa, wx)


def t5_bucket_table():
    qi = np.arange(BLOCK)[:, None]
    ki = np.arange(2 * BLOCK)[None, :]
    rel = qi + BLOCK - ki
    relc = np.maximum(rel, 0)
    max_exact = N_BUCKETS // 2
    relf = np.maximum(relc, 1).astype(np.float32)
    large = max_exact + (np.log(relf / np.float32(max_exact)) / np.float32(math.log(MAX_DISTANCE / max_exact))
                         * np.float32(N_BUCKETS - max_exact)).astype(np.int32)
    large = np.minimum(large, N_BUCKETS - 1)
    bucket = np.where(relc < max_exact, relc, large)
    bucket = np.where((rel >= 0) & (rel < BLOCK), bucket, -1)
    return jnp.asarray(bucket.reshape(1, BLOCK * 2 * BLOCK), jnp.int32)


def bias_band(rel_bias_t, buckets):
    n = BLOCK * 2 * BLOCK
    tn = 4096

    def body(bk_ref, rb_ref, o_ref):
        row = lax.broadcasted_iota(jnp.int32, (N_BUCKETS, tn), 0)
        oh = jnp.where(row == bk_ref[...], 1.0, 0.0).astype(BF16)
        rb = rb_ref[...]
        p0 = rb.astype(BF16)
        r1 = rb - p0.astype(F32)
        p1 = r1.astype(BF16)
        p2 = (r1 - p1.astype(F32)).astype(BF16)
        o_ref[...] = _dot(p0, oh) + _dot(p1, oh) + _dot(p2, oh)

    return pl.pallas_call(
        body, name="bias_band", grid=(n // tn,),
        in_specs=[pl.BlockSpec((1, tn), lambda i: (0, i)), pl.BlockSpec((N_HEADS, N_BUCKETS), lambda i: (0, 0))],
        out_specs=pl.BlockSpec((N_HEADS, tn), lambda i: (0, i)),
        out_shape=SDS((N_HEADS, n), F32), compiler_params=_cp(1))(buckets, rel_bias_t)


def bias_band_bwd(dband, buckets):
    n = BLOCK * 2 * BLOCK
    tn = 4096

    def body(bk_ref, d_ref, o_ref):
        @pl.when(pl.program_id(0) == 0)
        def _():
            o_ref[...] = jnp.zeros_like(o_ref)
        row = lax.broadcasted_iota(jnp.int32, (N_BUCKETS, tn), 0)
        oh = jnp.where(row == bk_ref[...], 1.0, 0.0).astype(BF16)
        dv = d_ref[...]
        p0 = dv.astype(BF16)
        r1 = dv - p0.astype(F32)
        p1 = r1.astype(BF16)
        p2 = (r1 - p1.astype(F32)).astype(BF16)
        o_ref[...] += _dot_nt(oh, p0) + _dot_nt(oh, p1) + _dot_nt(oh, p2)

    return pl.pallas_call(
        body, name="bias_band_bwd", grid=(n // tn,),
        in_specs=[pl.BlockSpec((1, tn), lambda i: (0, i)), pl.BlockSpec((N_HEADS, tn), lambda i: (0, i))],
        out_specs=pl.BlockSpec((N_BUCKETS, N_HEADS), lambda i: (0, 0)),
        out_shape=SDS((N_BUCKETS, N_HEADS), F32), compiler_params=_cp(1))(buckets, dband)


def _dup_half(band, which):
    lane = lax.broadcasted_iota(jnp.int32, band.shape, 1)
    rolled = pltpu.roll(band, 64, 1)
    keep = (lane < 64) if which == 0 else (lane >= 64)
    return jnp.where(keep, band, rolled)


def _attn_probs(scores, bias, sink, valid):
    sc = jnp.where(valid, scores * SCALE + bias, NEG_INF)
    m = jnp.maximum(jnp.max(sc, axis=-1, keepdims=True), sink)
    e = jnp.exp(sc - m)
    es = jnp.exp(sink - m)
    inv = 1.0 / (jnp.sum(e, axis=-1, keepdims=True) + es)
    return e * inv, es * inv


def _stack_heads(src_ref, kv, dst):
    lane = lax.broadcasted_iota(jnp.int32, (BLOCK, 128), 1)
    for jj in range(4):
        slab = src_ref[:, (4 * kv + jj) * 128:(4 * kv + jj + 1) * 128]
        for hh in range(2):
            keep = (lane < 64) if hh == 0 else (lane >= 64)
            dst[pl.ds((2 * jj + hh) * BLOCK, BLOCK), :] = jnp.where(keep, slab, jnp.zeros_like(slab))


def _unstack_heads(stacked, dst_ref, kv):
    lane = lax.broadcasted_iota(jnp.int32, (BLOCK, 128), 1)
    for jj in range(4):
        lo = stacked[(2 * jj) * BLOCK:(2 * jj + 1) * BLOCK]
        hi = stacked[(2 * jj + 1) * BLOCK:(2 * jj + 2) * BLOCK]
        dst_ref[:, (4 * kv + jj) * 128:(4 * kv + jj + 1) * 128] = jnp.where(lane < 64, lo, hi).astype(dst_ref.dtype)


def _band_valid(n):
    qi = lax.broadcasted_iota(jnp.int32, (BLOCK, 2 * BLOCK), 0)
    ki = lax.broadcasted_iota(jnp.int32, (BLOCK, 2 * BLOCK), 1)
    rel = qi + BLOCK - ki
    return (rel >= 0) & (rel < BLOCK) & ((ki >= BLOCK) | (n > 0))


def _kv_bands(prev_ref, cur_ref):
    band = jnp.concatenate([prev_ref[...].astype(F32), cur_ref[...].astype(F32)], axis=0)
    return [_dup_half(band, 0).astype(BF16), _dup_half(band, 1).astype(BF16)]


def attn_fwd(proj, band, sinks):
    s = proj.shape[0]
    nb = s // BLOCK
    qw = 1024

    def body(sk_ref, q_ref, kp_ref, kc_ref, vp_ref, vc_ref, b_ref, o_ref, qs_buf, s_buf, p_buf):
        n = pl.program_id(0)
        gp = pl.program_id(1)
        valid = _band_valid(n)
        kks = _kv_bands(kp_ref, kc_ref)
        vvs = _kv_bands(vp_ref, vc_ref)
        for kv in range(2):
            _stack_heads(q_ref, kv, qs_buf)
            s_buf[...] = _dot_nt(qs_buf[...], kks[kv])
            for hq in range(8):
                hl = 8 * kv + hq
                rows = pl.ds(hq * BLOCK, BLOCK)
                p, _ = _attn_probs(s_buf[rows, :], b_ref[hl], sk_ref[gp * 16 + hl], valid)
                p_buf[rows, :] = p.astype(BF16)
            _unstack_heads(_dot(p_buf[...], vvs[kv]), o_ref, kv)

    kb, vb = OFF_K // 128, OFF_V // 128
    return pl.pallas_call(
        body, name="attn_fwd", grid=(nb, 2),
        in_specs=[pl.BlockSpec(memory_space=pltpu.SMEM),
                  pl.BlockSpec((BLOCK, qw), lambda n, g: (n, OFF_Q // qw + g)),
                  pl.BlockSpec((BLOCK, 128), lambda n, g: (jnp.maximum(n - 1, 0), kb + g)),
                  pl.BlockSpec((BLOCK, 128), lambda n, g: (n, kb + g)),
                  pl.BlockSpec((BLOCK, 128), lambda n, g: (jnp.maximum(n - 1, 0), vb + g)),
                  pl.BlockSpec((BLOCK, 128), lambda n, g: (n, vb + g)),
                  pl.BlockSpec((16, BLOCK, 2 * BLOCK), lambda n, g: (g, 0, 0))],
        out_specs=pl.BlockSpec((BLOCK, qw), lambda n, g: (n, g)),
        out_shape=SDS((s, D), BF16),
        scratch_shapes=[pltpu.VMEM((8 * BLOCK, 128), BF16), pltpu.VMEM((8 * BLOCK, 2 * BLOCK), F32),
                        pltpu.VMEM((8 * BLOCK, 2 * BLOCK), BF16)],
        compiler_params=_cp(2))(sinks, proj, proj, proj, proj, proj, band)


def merge_fwd(ya, att, w_lru_out, w_attn_out, proj):
    s = ya.shape[0]
    tm, tn = min(1024, s), 512

    def body(ya_ref, at_ref, wl_ref, wt_ref, ga_ref, gb_ref, yab_ref, mg_ref):
        y_a = _dot(ya_ref[...], wl_ref[...])
        y_b = _dot(at_ref[...], wt_ref[...])
        yab_ref[0] = y_a.astype(BF16)
        yab_ref[1] = y_b.astype(BF16)
        mg_ref[...] = (_sigmoid(ga_ref[...].astype(F32)) * y_a + _sigmoid(gb_ref[...].astype(F32)) * y_b).astype(BF16)

    return pl.pallas_call(
        body, name="merge_fwd", grid=(s // tm, D // tn),
        in_specs=[pl.BlockSpec((tm, D), lambda i, j: (i, 0)), pl.BlockSpec((tm, D), lambda i, j: (i, 0)),
                  pl.BlockSpec((D, tn), lambda i, j: (0, j)), pl.BlockSpec((D, tn), lambda i, j: (0, j)),
                  pl.BlockSpec((tm, tn), lambda i, j: (i, OFF_GA // tn + j)),
                  pl.BlockSpec((tm, tn), lambda i, j: (i, OFF_GB // tn + j))],
        out_specs=[pl.BlockSpec((2, tm, tn), lambda i, j: (0, i, j)), pl.BlockSpec((tm, tn), lambda i, j: (i, j))],
        out_shape=[SDS((2, s, D), BF16), SDS((s, D), BF16)],
        compiler_params=_cp(2))(ya, att, w_lru_out, w_attn_out, proj, proj)


def outproj_fwd(merged, w_out, x, vecs):
    s = x.shape[0]
    tm, tn = min(1024, s), 512

    def body(m_ref, w_ref, x_ref, v_ref, x1_ref, o1_ref):
        o1 = _dot(m_ref[...], w_ref[...])
        o1_ref[...] = o1.astype(BF16)
        x1_ref[...] = x_ref[...] + v_ref[V_GATE1:V_GATE1 + 1, :] * o1

    return pl.pallas_call(
        body, name="outproj_fwd", grid=(s // tm, D // tn),
        in_specs=[pl.BlockSpec((tm, D), lambda i, j: (i, 0)), pl.BlockSpec((D, tn), lambda i, j: (0, j)),
                  pl.BlockSpec((tm, tn), lambda i, j: (i, j)), pl.BlockSpec((16, tn), lambda i, j: (0, j))],
        out_specs=[pl.BlockSpec((tm, tn), lambda i, j: (i, j)), pl.BlockSpec((tm, tn), lambda i, j: (i, j))],
        out_shape=[SDS((s, D), F32), SDS((s, D), BF16)],
        compiler_params=_cp(2))(merged, w_out, x, vecs)


def ff1_fwd(x1, vecs, w_ff1):
    s = x1.shape[0]
    tm, tn = min(1024, s), 512
    per = D // tn

    def body(x_ref, v_ref, w_ref, f_ref, h_ref):
        @pl.when(pl.program_id(1) == 0)
        def _():
            _modulated_norm(x_ref, v_ref, V_G2, V_SCALE2, V_SHIFT2, h_ref, tm)
        f_ref[...] = _dot(h_ref[...], w_ref[...]).astype(BF16)

    return pl.pallas_call(
        body, name="ff1_fwd", grid=(s // tm, D_FF // tn),
        in_specs=[pl.BlockSpec((tm, D), lambda i, j: (i, 0)), pl.BlockSpec((16, D), lambda i, j: (0, 0)),
                  pl.BlockSpec((None, D, tn), lambda i, j: (j // per, 0, j % per))],
        out_specs=[pl.BlockSpec((tm, tn), lambda i, j: (i, j)), pl.BlockSpec((tm, D), lambda i, j: (i, 0))],
        out_shape=[SDS((s, D_FF), BF16), SDS((s, D), BF16)],
        compiler_params=_cp(2))(x1, vecs, w_ff1)


def ff2_loss(f, w_ff2, x1, tgt, vecs):
    s = x1.shape[0]
    tm, tk = min(512, s), 1024
    nk = D_FF // tk

    def body(f_ref, w_ref, x1_hbm, t_hbm, v_ref, dx2_ref, do2_ref, sums_ref, loss_ref, acc, x1_ref, t_ref, sems):
        i, k = pl.program_id(0), pl.program_id(1)
        fetches = _row_fetches((x1_hbm, t_hbm), (x1_ref, t_ref), sems, i, tm)

        @pl.when((i == 0) & (k == 0))
        def _():
            sums_ref[...] = jnp.zeros_like(sums_ref)
            loss_ref[...] = jnp.zeros_like(loss_ref)

        @pl.when(k == 0)
        def _():
            acc[...] = jnp.zeros_like(acc)
            for cp in fetches:
                cp.start()

        fv = jnp.maximum(f_ref[...].astype(F32), 0.0)
        acc[...] += _dot((fv * fv).astype(BF16), w_ref[...])

        @pl.when(k == nk - 1)
        def _():
            for cp in fetches:
                cp.wait()
            gate2 = v_ref[V_GATE2:V_GATE2 + 1, :]
            g3 = v_ref[V_G3:V_G3 + 1, :]

            def sub(rb, carry):
                rs = pl.ds(pl.multiple_of(rb * SUB, SUB), SUB)
                o2 = acc[rs, :]
                x2 = x1_ref[rs, :] + gate2 * o2
                r3, xh = _rms_parts(x2)
                e = xh * g3 - t_ref[rs, :]
                loss_ref[...] += (0.5 / D) * jnp.sum(e * e)
                dy = e * (1.0 / D)
                sums_ref[0:1, :] += jnp.sum(dy * xh, axis=0, keepdims=True)
                dxh = dy * g3
                dx2 = r3 * (dxh - xh * jnp.mean(dxh * xh, axis=-1, keepdims=True))
                sums_ref[1:2, :] += jnp.sum(dx2 * o2, axis=0, keepdims=True)
                dx2_ref[rs, :] = dx2
                do2_ref[rs, :] = (dx2 * gate2).astype(BF16)
                return carry

            lax.fori_loop(0, tm // SUB, sub, 0)

    return pl.pallas_call(
        body, name="ff2_loss", grid=(s // tm, nk),
        in_specs=[pl.BlockSpec((tm, tk), lambda i, k: (i, k)), pl.BlockSpec((tk, D), lambda i, k: (k, 0)),
                  pl.BlockSpec(memory_space=pl.ANY), pl.BlockSpec(memory_space=pl.ANY),
                  pl.BlockSpec((16, D), lambda i, k: (0, 0))],
        out_specs=[pl.BlockSpec((tm, D), lambda i, k: (i, 0)), pl.BlockSpec((tm, D), lambda i, k: (i, 0)),
                   pl.BlockSpec((8, D), lambda i, k: (0, 0)), pl.BlockSpec((8, 128), lambda i, k: (0, 0))],
        out_shape=[SDS((s, D), F32), SDS((s, D), BF16), SDS((8, D), F32), SDS((8, 128), F32)],
        scratch_shapes=[pltpu.VMEM((tm, D), F32), pltpu.VMEM((tm, D), F32), pltpu.VMEM((tm, D), F32),
                        pltpu.SemaphoreType.DMA((2,))],
        compiler_params=_cp(2))(f, w_ff2, x1, tgt, vecs)


def ff2_bwd(do2, w_ff2, f):
    s = do2.shape[0]
    tm, tn = min(1024, s), 512

    def body(d_ref, w_ref, f_ref, o_ref):
        dff = _dot_nt(d_ref[...], w_ref[...])
        o_ref[...] = (dff * (2.0 * jnp.maximum(f_ref[...].astype(F32), 0.0))).astype(BF16)

    return pl.pallas_call(
        body, name="ff2_bwd", grid=(s // tm, D_FF // tn),
        in_specs=[pl.BlockSpec((tm, D), lambda i, j: (i, 0)), pl.BlockSpec((tn, D), lambda i, j: (j, 0)),
                  pl.BlockSpec((tm, tn), lambda i, j: (i, j))],
        out_specs=pl.BlockSpec((tm, tn), lambda i, j: (i, j)),
        out_shape=SDS((s, D_FF), BF16), compiler_params=_cp(2))(do2, w_ff2, f)


def weight_grad(name, a, b, tn, out_shape, out_block, out_map, relu2=False, b_part=0):
    s, m = a.shape
    n = b.shape[1]
    tm = WG_TM
    chunk = min(1024, s)
    nch = s // chunk

    def body(a_hbm, b_ref, o_ref, a_buf, at_s, sem):
        i = pl.program_id(0)

        @pl.when(pl.program_id(1) == 0)
        def _():
            def fetch(ch):
                return pltpu.make_async_copy(a_hbm.at[pl.ds(ch * chunk, chunk), pl.ds(i * tm, tm)],
                                             a_buf.at[ch % 2], sem.at[ch % 2])
            fetch(0).start()
            for ch in range(nch):
                if ch + 1 < nch:
                    fetch(ch + 1).start()
                fetch(ch).wait()
                av = a_buf[ch % 2]
                if relu2:
                    fv = jnp.maximum(av.astype(F32), 0.0)
                    av = (fv * fv).astype(BF16)
                at_s[:, ch * chunk:(ch + 1) * chunk] = av.T

        o_ref[...] = _dot(at_s[...], b_ref[...]).astype(BF16)

    return pl.pallas_call(
        body, name=name, grid=(m // tm, n // tn),
        in_specs=[pl.BlockSpec(memory_space=pl.ANY), pl.BlockSpec((s, tn), lambda i, j: (b_part, j))],
        out_specs=pl.BlockSpec(out_block, lambda i, j: out_map(i, j)),
        out_shape=SDS(out_shape, BF16),
        scratch_shapes=[pltpu.VMEM((2, chunk, tm), BF16), pltpu.VMEM((tm, s), BF16), pltpu.SemaphoreType.DMA((2,))],
        compiler_params=_cp(2))(a, b)


def ff1_bwd(df, w_ff1, x1, dx2, o1, vecs):
    s = df.shape[0]
    tm, tk = min(512, s), 1024
    nk = D_FF // tk
    per = D // tk

    def body(d_ref, w_ref, x1_hbm, dx2_hbm, o1_hbm, v_ref, dx1_ref, do1_ref, sums_ref, acc, x1_ref, dx2_ref, o1_ref, sems):
        i, k = pl.program_id(0), pl.program_id(1)
        fetches = _row_fetches((x1_hbm, dx2_hbm, o1_hbm), (x1_ref, dx2_ref, o1_ref), sems, i, tm)

        @pl.when((i == 0) & (k == 0))
        def _():
            sums_ref[...] = jnp.zeros_like(sums_ref)

        @pl.when(k == 0)
        def _():
            acc[...] = jnp.zeros_like(acc)
            for cp in fetches:
                cp.start()

        acc[...] += _dot_nt(d_ref[...], w_ref[...])

        @pl.when(k == nk - 1)
        def _():
            for cp in fetches:
                cp.wait()
            g2 = v_ref[V_G2:V_G2 + 1, :]
            scale2 = v_ref[V_SCALE2:V_SCALE2 + 1, :]
            gate1 = v_ref[V_GATE1:V_GATE1 + 1, :]

            def sub(rb, carry):
                rs = pl.ds(pl.multiple_of(rb * SUB, SUB), SUB)
                dh = acc[rs, :]
                r2, xh = _rms_parts(x1_ref[rs, :])
                sums_ref[0:1, :] += jnp.sum(dh, axis=0, keepdims=True)
                sums_ref[1:2, :] += jnp.sum(dh * (xh * g2), axis=0, keepdims=True)
                dxn = dh * (1.0 + scale2)
                sums_ref[2:3, :] += jnp.sum(dxn * xh, axis=0, keepdims=True)
                dxh = dxn * g2
                dx1 = dx2_ref[rs, :] + r2 * (dxh - xh * jnp.mean(dxh * xh, axis=-1, keepdims=True))
                sums_ref[3:4, :] += jnp.sum(dx1 * o1_ref[rs, :].astype(F32), axis=0, keepdims=True)
                dx1_ref[rs, :] = dx1
                do1_ref[rs, :] = (dx1 * gate1).astype(BF16)
                return carry

            lax.fori_loop(0, tm // SUB, sub, 0)

    return pl.pallas_call(
        body, name="ff1_bwd", grid=(s // tm, nk),
        in_specs=[pl.BlockSpec((tm, tk), lambda i, k: (i, k)),
                  pl.BlockSpec((None, D, tk), lambda i, k: (k // per, 0, k % per)),
                  pl.BlockSpec(memory_space=pl.ANY), pl.BlockSpec(memory_space=pl.ANY),
                  pl.BlockSpec(memory_space=pl.ANY), pl.BlockSpec((16, D), lambda i, k: (0, 0))],
        out_specs=[pl.BlockSpec((tm, D), lambda i, k: (i, 0)), pl.BlockSpec((tm, D), lambda i, k: (i, 0)),
                   pl.BlockSpec((8, D), lambda i, k: (0, 0))],
        out_shape=[SDS((s, D), F32), SDS((s, D), BF16), SDS((8, D), F32)],
        scratch_shapes=[pltpu.VMEM((tm, D), F32), pltpu.VMEM((tm, D), F32), pltpu.VMEM((tm, D), F32),
                        pltpu.VMEM((tm, D), BF16), pltpu.SemaphoreType.DMA((3,))],
        compiler_params=_cp(2))(df, w_ff1, x1, dx2, o1, vecs)


def outproj_bwd(do1, w_out, yab, proj):
    s = do1.shape[0]
    tm, tn = min(1024, s), 512
    per = D // tn

    def body(d_ref, w_ref, y_ref, g_ref, dy_ref, dp_ref):
        dm = _dot_nt(d_ref[...], w_ref[...])
        sg = _sigmoid(g_ref[...].astype(F32))
        dy_ref[...] = (dm * sg).astype(BF16)
        dp_ref[...] = (dm * y_ref[...].astype(F32) * sg * (1.0 - sg)).astype(BF16)

    return pl.pallas_call(
        body, name="outproj_bwd", grid=(s // tm, 2 * per),
        in_specs=[pl.BlockSpec((tm, D), lambda i, j: (i, 0)), pl.BlockSpec((tn, D), lambda i, j: (j % per, 0)),
                  pl.BlockSpec((None, tm, tn), lambda i, j: (j // per, i, j % per)),
                  pl.BlockSpec((tm, tn), lambda i, j: (i, OFF_GA // tn + j))],
        out_specs=[pl.BlockSpec((None, tm, tn), lambda i, j: (j // per, i, j % per)),
                   pl.BlockSpec((tm, tn), lambda i, j: (i, OFF_GA // tn + j))],
        out_shape=[SDS((2, s, D), BF16), SDS((s, IN_W), BF16)],
        compiler_params=_cp(2))(do1, w_out, yab, proj)


def lruout_bwd(dyab, w_lru_out, rec, proj, dproj):
    s = rec.shape[0]
    tm, tn = min(1024, s), 512

    def body(d_ref, w_ref, r_ref, g_ref, dp_in, dr_ref, dp_ref):
        dya = _dot_nt(d_ref[...], w_ref[...])
        gate = g_ref[...].astype(F32)
        dr_ref[...] = dya * _gelu(gate)
        dp_ref[...] = (dya * r_ref[...] * _gelu_grad(gate)).astype(BF16)

    return pl.pallas_call(
        body, name="lruout_bwd", grid=(s // tm, D // tn),
        in_specs=[pl.BlockSpec((None, tm, D), lambda i, j: (0, i, 0)), pl.BlockSpec((tn, D), lambda i, j: (j, 0)),
                  pl.BlockSpec((tm, tn), lambda i, j: (i, j)),
                  pl.BlockSpec((tm, tn), lambda i, j: (i, OFF_GATE // tn + j)),
                  pl.BlockSpec(memory_space=pl.ANY)],
        out_specs=[pl.BlockSpec((tm, tn), lambda i, j: (i, j)),
                   pl.BlockSpec((tm, tn), lambda i, j: (i, OFF_GATE // tn + j))],
        out_shape=[SDS((s, D), F32), SDS((s, IN_W), BF16)],
        input_output_aliases={4: 1},
        compiler_params=_cp(2))(dyab, w_lru_out, rec, proj, dproj)


def attnout_bwd(dyab, w_attn_out):
    s = dyab.shape[1]
    tm, tn = min(1024, s), 512

    def body(d_ref, w_ref, o_ref):
        o_ref[...] = _dot_nt(d_ref[...], w_ref[...]).astype(BF16)

    return pl.pallas_call(
        body, name="attnout_bwd", grid=(s // tm, D // tn),
        in_specs=[pl.BlockSpec((None, tm, D), lambda i, j: (1, i, 0)), pl.BlockSpec((tn, D), lambda i, j: (j, 0))],
        out_specs=pl.BlockSpec((tm, tn), lambda i, j: (i, j)),
        out_shape=SDS((s, D), BF16), compiler_params=_cp(2))(dyab, w_attn_out)


def attn_bwd(proj, band, sinks, datt, dproj):
    s = proj.shape[0]
    nb = s // BLOCK
    qw = 1024

    def body(sk_ref, q_ref, kp_ref, kc_ref, vp_ref, vc_ref, b_ref, do_ref, dp_in,
             dq_ref, dkb_ref, dvb_ref, db_ref, ds_ref, qs_buf, dos_buf, s_buf, dp_buf, p_buf, dsc_buf):
        gp = pl.program_id(0)
        n = pl.program_id(1)

        @pl.when(n == 0)
        def _():
            db_ref[...] = jnp.zeros_like(db_ref)
            ds_ref[...] = jnp.zeros_like(ds_ref)

        valid = _band_valid(n)
        kks = _kv_bands(kp_ref, kc_ref)
        vvs = _kv_bands(vp_ref, vc_ref)
        lane_b = lax.broadcasted_iota(jnp.int32, (2 * BLOCK, 128), 1)
        dks, dvs = [], []
        for kv in range(2):
            _stack_heads(q_ref, kv, qs_buf)
            _stack_heads(do_ref, kv, dos_buf)
            s_buf[...] = _dot_nt(qs_buf[...], kks[kv])
            dp_buf[...] = _dot_nt(dos_buf[...], vvs[kv])
            for hq in range(8):
                hl = 8 * kv + hq
                rows = pl.ds(hq * BLOCK, BLOCK)
                p, ps = _attn_probs(s_buf[rows, :], b_ref[hl], sk_ref[gp * 16 + hl], valid)
                dp = dp_buf[rows, :]
                delta = jnp.sum(p * dp, axis=-1, keepdims=True)
                dsc = p * (dp - delta)
                db_ref[hl] += dsc
                ds_ref[hl:hl + 1, :] += jnp.zeros((1, 128), F32) - jnp.sum(ps * delta)
                p_buf[rows, :] = p.astype(BF16)
                dsc_buf[rows, :] = (dsc * SCALE).astype(BF16)
            _unstack_heads(_dot(dsc_buf[...], kks[kv]), dq_ref, kv)
            dk = _dot_tn(dsc_buf[...], qs_buf[...])
            dv = _dot_tn(p_buf[...], dos_buf[...])
            dks.append(dk + pltpu.roll(dk, 64, 1))
            dvs.append(dv + pltpu.roll(dv, 64, 1))
        dkb_ref[...] = jnp.where(lane_b < 64, dks[0], dks[1])
        dvb_ref[...] = jnp.where(lane_b < 64, dvs[0], dvs[1])

    kb, vb = OFF_K // 128, OFF_V // 128
    return pl.pallas_call(
        body, name="attn_bwd", grid=(2, nb),
        in_specs=[pl.BlockSpec(memory_space=pltpu.SMEM),
                  pl.BlockSpec((BLOCK, qw), lambda g, n: (n, OFF_Q // qw + g)),
                  pl.BlockSpec((BLOCK, 128), lambda g, n: (jnp.maximum(n - 1, 0), kb + g)),
                  pl.BlockSpec((BLOCK, 128), lambda g, n: (n, kb + g)),
                  pl.BlockSpec((BLOCK, 128), lambda g, n: (jnp.maximum(n - 1, 0), vb + g)),
                  pl.BlockSpec((BLOCK, 128), lambda g, n: (n, vb + g)),
                  pl.BlockSpec((16, BLOCK, 2 * BLOCK), lambda g, n: (g, 0, 0)),
                  pl.BlockSpec((BLOCK, qw), lambda g, n: (n, g)),
                  pl.BlockSpec(memory_space=pl.ANY)],
        out_specs=[pl.BlockSpec((BLOCK, qw), lambda g, n: (n, OFF_Q // qw + g)),
                   pl.BlockSpec((2 * BLOCK, 128), lambda g, n: (n, g)),
                   pl.BlockSpec((2 * BLOCK, 128), lambda g, n: (n, g)),
                   pl.BlockSpec((16, BLOCK, 2 * BLOCK), lambda g, n: (g, 0, 0)),
                   pl.BlockSpec((16, 128), lambda g, n: (g, 0))],
        out_shape=[SDS((s, IN_W), BF16), SDS((nb * 2 * BLOCK, 256), F32), SDS((nb * 2 * BLOCK, 256), F32),
                   SDS((N_HEADS, BLOCK, 2 * BLOCK), F32), SDS((N_HEADS, 128), F32)],
        input_output_aliases={8: 0},
        scratch_shapes=[pltpu.VMEM((8 * BLOCK, 128), BF16), pltpu.VMEM((8 * BLOCK, 128), BF16),
                        pltpu.VMEM((8 * BLOCK, 2 * BLOCK), F32), pltpu.VMEM((8 * BLOCK, 2 * BLOCK), F32),
                        pltpu.VMEM((8 * BLOCK, 2 * BLOCK), BF16), pltpu.VMEM((8 * BLOCK, 2 * BLOCK), BF16)],
        compiler_params=_cp(2))(sinks, proj, proj, proj, proj, proj, band, datt, dproj)


def dkv_combine(dkb, dvb, dproj):
    nb = dkb.shape[0] // (2 * BLOCK)
    s = nb * BLOCK
    dkb3 = dkb.reshape(nb, 2 * BLOCK, 256)
    dvb3 = dvb.reshape(nb, 2 * BLOCK, 256)

    def body(k1, k2, v1, v2, dp_in, o_ref):
        nxt = jnp.where(pl.program_id(0) < nb - 1, 1.0, 0.0)
        o_ref[:, 0:256] = (k1[...] + nxt * k2[...]).astype(BF16)
        o_ref[:, 256:512] = (v1[...] + nxt * v2[...]).astype(BF16)

    spec1 = pl.BlockSpec((None, BLOCK, 256), lambda m: (m, 1, 0))
    spec2 = pl.BlockSpec((None, BLOCK, 256), lambda m: (jnp.minimum(m + 1, nb - 1), 0, 0))
    return pl.pallas_call(
        body, name="dkv_combine", grid=(nb,),
        in_specs=[spec1, spec2, spec1, spec2, pl.BlockSpec(memory_space=pl.ANY)],
        out_specs=pl.BlockSpec((BLOCK, 512), lambda m: (m, OFF_K // 512)),
        out_shape=SDS((s, IN_W), BF16), input_output_aliases={4: 0},
        compiler_params=_cp(1))(dkb3, dkb3, dvb3, dvb3, dproj)


def lru_bwd(proj, rec, drec, lvec, wa, wx, dproj):
    s = proj.shape[0]
    t = min(256, s)
    nt = s // t

    def body(lx_ref, lxh_ref, rec_ref, rech_ref, dr_ref, lv_ref, wa_ref, wx_ref, dp_in,
             dlx_ref, sums_ref, dwa_ref, dwx_ref,
             xbuf, hbuf, dxbuf, a_s, dh_s, xc_s, r_s, ig_s, mu_s, gc):
        step_i = pl.program_id(0)
        ti = nt - 1 - step_i

        @pl.when(step_i == 0)
        def _():
            sums_ref[...] = jnp.zeros_like(sums_ref)
            dwa_ref[...] = jnp.zeros_like(dwa_ref)
            dwx_ref[...] = jnp.zeros_like(dwx_ref)
            dxbuf[pl.ds(t, 8), :] = jnp.zeros((8, D), F32)
            gc[...] = jnp.zeros((8, D), F32)

        live = jnp.where(ti > 0, 1.0, 0.0)
        xbuf[pl.ds(0, 8), :] = lxh_ref[...].astype(F32)[8:16] * live
        xbuf[pl.ds(8, t), :] = lx_ref[...].astype(F32)
        hbuf[pl.ds(0, 8), :] = rech_ref[...] * live
        hbuf[pl.ds(8, t), :] = rec_ref[...]
        first = (lax.broadcasted_iota(jnp.int32, (t, 128), 0) + ti * t) == 0
        for b in range(N_LRU_BLOCKS):
            cs = slice(b * 128, (b + 1) * 128)
            _, xc, _, r, ig, _, a, mult = _lru_block_fwd(xbuf, lv_ref, wa_ref, wx_ref, b, t, first)
            a_s[:, cs] = a
            xc_s[:, cs] = xc
            r_s[:, cs] = r
            ig_s[:, cs] = ig
            mu_s[:, cs] = mult

        def step(q, g):
            tt = t - 1 - q
            dh = dr_ref[pl.ds(tt, 1), :] + g
            dh_s[pl.ds(tt, 1), :] = dh
            return a_s[pl.ds(tt, 1), :] * dh

        gc[0:1, :] = lax.fori_loop(0, t, step, gc[0:1, :], unroll=8)
        for b in range(N_LRU_BLOCKS):
            cs = slice(b * 128, (b + 1) * 128)
            dh = dh_s[:, cs]
            a = a_s[:, cs]
            xc = xc_s[:, cs]
            r = r_s[:, cs]
            ig = ig_s[:, cs]
            mult = mu_s[:, cs]
            sp = _softplus(-lv_ref[L_LAM:L_LAM + 1, cs])
            lam = lv_ref[L_LAM:L_LAM + 1, cs]
            da = dh * hbuf[pl.ds(7, t), cs]
            dmult = jnp.where(first, 0.0, dh * ig * xc)
            dig = dh * mult * xc
            dxc = dh * mult * ig
            dlog_a = da * a - dmult * (a * a) / mult
            dr = dlog_a * ((-LRU_C) * sp)
            dsp = jnp.sum(dlog_a * ((-LRU_C) * r), axis=0, keepdims=True)
            dza = dr * r * (1.0 - r)
            dzx = dig * ig * (1.0 - ig)
            dzab = dza.astype(BF16)
            dzxb = dzx.astype(BF16)
            xcb = xc.astype(BF16)
            dwa_ref[b] += _dot_tn(xcb, dzab)
            dwx_ref[b] += _dot_tn(xcb, dzxb)
            dxc = dxc + _dot_nt(dzab, wa_ref[b]) + _dot_nt(dzxb, wx_ref[b])
            sums_ref[L_LAM:L_LAM + 1, cs] += dsp * (-jax.nn.sigmoid(-lam))
            sums_ref[L_BA:L_BA + 1, cs] += jnp.sum(dza, axis=0, keepdims=True)
            sums_ref[L_BX:L_BX + 1, cs] += jnp.sum(dzx, axis=0, keepdims=True)
            sums_ref[L_CB:L_CB + 1, cs] += jnp.sum(dxc, axis=0, keepdims=True)
            for kk in range(4):
                sums_ref[kk:kk + 1, cs] += jnp.sum(dxc * xbuf[pl.ds(5 + kk, t), cs], axis=0, keepdims=True)
            dxbuf[pl.ds(0, t), cs] = dxc
            dlx = (lv_ref[3:4, cs] * dxc + lv_ref[2:3, cs] * dxbuf[pl.ds(1, t), cs]
                   + lv_ref[1:2, cs] * dxbuf[pl.ds(2, t), cs] + lv_ref[0:1, cs] * dxbuf[pl.ds(3, t), cs])
            dlx_ref[:, cs] = dlx.astype(BF16)
        dxbuf[pl.ds(t, 8), :] = dxbuf[pl.ds(0, 8), :]

    rev = lambda i: nt - 1 - i
    return pl.pallas_call(
        body, name="lru_bwd", grid=(nt,),
        in_specs=[pl.BlockSpec((t, D), lambda i: (rev(i), 0)),
                  pl.BlockSpec((16, D), lambda i: (jnp.maximum(rev(i) * (t // 16) - 1, 0), 0)),
                  pl.BlockSpec((t, D), lambda i: (rev(i), 0)),
                  pl.BlockSpec((8, D), lambda i: (jnp.maximum(rev(i) * (t // 8) - 1, 0), 0)),
                  pl.BlockSpec((t, D), lambda i: (rev(i), 0)),
                  pl.BlockSpec((8, D), lambda i: (0, 0)),
                  pl.BlockSpec((N_LRU_BLOCKS, 128, 128), lambda i: (0, 0, 0)),
                  pl.BlockSpec((N_LRU_BLOCKS, 128, 128), lambda i: (0, 0, 0)),
                  pl.BlockSpec(memory_space=pl.ANY)],
        out_specs=[pl.BlockSpec((t, D), lambda i: (rev(i), 0)),
                   pl.BlockSpec((8, D), lambda i: (0, 0)),
                   pl.BlockSpec((N_LRU_BLOCKS, 128, 128), lambda i: (0, 0, 0)),
                   pl.BlockSpec((N_LRU_BLOCKS, 128, 128), lambda i: (0, 0, 0))],
        out_shape=[SDS((s, IN_W), BF16), SDS((8, D), F32), SDS((N_LRU_BLOCKS, 128, 128), F32),
                   SDS((N_LRU_BLOCKS, 128, 128), F32)],
        scratch_shapes=[pltpu.VMEM((t + 8, D), F32), pltpu.VMEM((t + 8, D), F32), pltpu.VMEM((t + 8, D), F32)]
        + [pltpu.VMEM((t, D), F32)] * 6 + [pltpu.VMEM((8, D), F32)],
        input_output_aliases={8: 0},
        compiler_params=_cp(1))(proj, proj, rec, rec, drec, lvec, wa, wx, dproj)


def inproj_bwd(dproj, w_in, x, dx1, vecs):
    s = x.shape[0]
    tm, tk = min(512, s), IN_TILE
    nk = IN_W // tk
    per = IN_SHARD // tk

    def body(d_ref, w_ref, x_hbm, dx1_hbm, v_ref, gx_ref, sums_ref, acc, x_ref, dx1_ref, sems):
        i, k = pl.program_id(0), pl.program_id(1)
        fetches = _row_fetches((x_hbm, dx1_hbm), (x_ref, dx1_ref), sems, i, tm)

        @pl.when((i == 0) & (k == 0))
        def _():
            sums_ref[...] = jnp.zeros_like(sums_ref)

        @pl.when(k == 0)
        def _():
            acc[...] = jnp.zeros_like(acc)
            for cp in fetches:
                cp.start()

        acc[...] += _dot_nt(d_ref[...], w_ref[...])

        @pl.when(k == nk - 1)
        def _():
            for cp in fetches:
                cp.wait()
            g1 = v_ref[V_G1:V_G1 + 1, :]
            scale1 = v_ref[V_SCALE1:V_SCALE1 + 1, :]

            def sub(rb, carry):
                rs = pl.ds(pl.multiple_of(rb * SUB, SUB), SUB)
                dh = acc[rs, :]
                r1, xh = _rms_parts(x_ref[rs, :])
                sums_ref[0:1, :] += jnp.sum(dh, axis=0, keepdims=True)
                sums_ref[1:2, :] += jnp.sum(dh * (xh * g1), axis=0, keepdims=True)
                dxn = dh * (1.0 + scale1)
                sums_ref[2:3, :] += jnp.sum(dxn * xh, axis=0, keepdims=True)
                dxh = dxn * g1
                gx_ref[rs, :] = dx1_ref[rs, :] + r1 * (dxh - xh * jnp.mean(dxh * xh, axis=-1, keepdims=True))
                return carry

            lax.fori_loop(0, tm // SUB, sub, 0)

    return pl.pallas_call(
        body, name="inproj_bwd", grid=(s // tm, nk),
        in_specs=[pl.BlockSpec((tm, tk), lambda i, k: (i, k)),
                  pl.BlockSpec((None, D, tk), lambda i, k: (k // per, 0, k % per)),
                  pl.BlockSpec(memory_space=pl.ANY), pl.BlockSpec(memory_space=pl.ANY),
                  pl.BlockSpec((16, D), lambda i, k: (0, 0))],
        out_specs=[pl.BlockSpec((tm, D), lambda i, k: (i, 0)), pl.BlockSpec((8, D), lambda i, k: (0, 0))],
        out_shape=[SDS((s, D), F32), SDS((8, D), F32)],
        scratch_shapes=[pltpu.VMEM((tm, D), F32), pltpu.VMEM((tm, D), F32), pltpu.VMEM((tm, D), F32),
                        pltpu.SemaphoreType.DMA((2,))],
        compiler_params=_cp(2))(dproj, w_in, x, dx1, vecs)


def mod_columns(c16, w_ada, b_cols):
    tn = 512

    def body(c_ref, w_ref, b_ref, o_ref):
        cv = c_ref[...]
        ca = (cv * jax.nn.sigmoid(cv)).astype(BF16)
        o_ref[...] = _dot(ca, w_ref[...].astype(BF16)) + b_ref[...]

    return pl.pallas_call(
        body, name="mod_columns", grid=(ADA_SHARD // tn,),
        in_specs=[pl.BlockSpec((16, D), lambda j: (0, 0)), pl.BlockSpec((D, tn), lambda j: (0, j)),
                  pl.BlockSpec((1, tn), lambda j: (0, j))],
        out_specs=pl.BlockSpec((16, tn), lambda j: (0, j)),
        out_shape=SDS((16, ADA_SHARD), F32), compiler_params=_cp(1))(c16, w_ada, b_cols)


def wada_update(c16, dmod16, w, m, v):
    tm, tn = 512, 512

    def body(c_ref, d_ref, w_ref, m_ref, v_ref, g_out, dl_out, m_out, v_out):
        cv = c_ref[...]
        ca = (cv * jax.nn.sigmoid(cv)).astype(BF16)
        g = _dot_tn(ca, d_ref[...].astype(BF16))
        dl, m2, v2 = _adamw_math(w_ref[...], g, m_ref[...], v_ref[...])
        g_out[...] = g
        dl_out[...] = dl
        m_out[...] = m2
        v_out[...] = v2

    tile = pl.BlockSpec((tm, tn), lambda i, j: (i, j))
    return pl.pallas_call(
        body, name="wada_update", grid=(D // tm, ADA_SHARD // tn),
        in_specs=[pl.BlockSpec((16, tm), lambda i, j: (0, i)), pl.BlockSpec((16, tn), lambda i, j: (0, j)),
                  tile, tile, tile],
        out_specs=[tile] * 4, out_shape=[SDS((D, ADA_SHARD), F32)] * 4,
        compiler_params=_cp(2))(c16, dmod16, w, m, v)


def adamw_big(name, w, mine, theirs, m, v, c_idx):
    r, c = w.shape
    tr = 128
    per = (r // 2) // tr

    def body(c_ref, w_ref, a_ref, b_ref, m_ref, v_ref, g_out, dl_out, m_out, v_out):
        own = (pl.program_id(0) // per) == c_ref[0]
        g = jnp.where(own, a_ref[...], b_ref[...])
        dl, m2, v2 = _adamw_math(w_ref[...], g, m_ref[...], v_ref[...])
        g_out[...] = g
        dl_out[...] = dl
        m_out[...] = m2
        v_out[...] = v2

    tile = pl.BlockSpec((tr, c), lambda i, cr: (i, 0))
    half = pl.BlockSpec((tr, c), lambda i, cr: (i % per, 0))
    gs = pltpu.PrefetchScalarGridSpec(num_scalar_prefetch=1, grid=(r // tr,),
                                      in_specs=[tile, half, half, tile, tile], out_specs=[tile] * 4)
    return pl.pallas_call(body, name=name, grid_spec=gs, out_shape=[SDS((r, c), F32)] * 4,
                          compiler_params=_cp(1))(c_idx, w, mine, theirs, m, v)


def cast_into_slot(name, w, k_idx):
    r, c = w.shape
    tr = 256

    def body(k_ref, w_ref, o_ref):
        o_ref[...] = w_ref[...].astype(BF16)

    gs = pltpu.PrefetchScalarGridSpec(
        num_scalar_prefetch=1, grid=(r // tr,),
        in_specs=[pl.BlockSpec((tr, c), lambda i, kr: (i, 0))],
        out_specs=pl.BlockSpec((None, tr, c), lambda i, kr: (kr[0], i, 0)))
    return pl.pallas_call(body, name=name, grid_spec=gs, out_shape=SDS((N_CHIPS, r, c), BF16),
                          compiler_params=_cp(1))(k_idx, w)


def adamw_small(ws, gs, ms, vs):
    n = len(ws)

    def body(*refs):
        for i in range(n):
            dl, m2, v2 = _adamw_math(refs[i][...], refs[n + i][...], refs[2 * n + i][...], refs[3 * n + i][...])
            refs[4 * n + i][...] = dl
            refs[5 * n + i][...] = m2
            refs[6 * n + i][...] = v2

    vm = pl.BlockSpec(memory_space=pltpu.VMEM)
    shapes = [SDS(w.shape, F32) for w in ws]
    outs = pl.pallas_call(
        body, name="adamw_small", in_specs=[vm] * (4 * n), out_specs=[vm] * (3 * n), out_shape=shapes * 3,
        compiler_params=pltpu.CompilerParams(vmem_limit_bytes=VMEM_LIMIT))(*ws, *gs, *ms, *vs)
    return outs[:n], outs[n:2 * n], outs[2 * n:]


def sum_devices(name, gathered):
    rows = gathered.shape[1]
    tr = min(rows, 128)

    def body(x_ref, o_ref):
        acc = x_ref[0].astype(F32)
        for d in range(1, N_DEV):
            acc = acc + x_ref[d].astype(F32)
        o_ref[...] = acc

    return pl.pallas_call(
        body, name=name, grid=(rows // tr,),
        in_specs=[pl.BlockSpec((N_DEV, tr, D), lambda i: (0, i, 0))],
        out_specs=pl.BlockSpec((tr, D), lambda i: (i, 0)),
        out_shape=SDS((rows, D), F32), compiler_params=_cp(1))(gathered)


def _mesh_pos():
    return lax.axis_index("x"), lax.axis_index("y"), lax.axis_index("c")


def _other_chips(x, y):
    return [(1 - x, y), (x, 1 - y), (1 - x, 1 - y)]


def all_gather_small(name, block):
    m_per, n = block.shape

    def body(x_ref, out_ref, send_sems, recv_sems, local_sem):
        x, y, c = _mesh_pos()
        me, sibling = (x, y, c), (x, y, 1 - c)
        chips = _other_chips(x, y)

        def rows(px, py, pc):
            return out_ref.at[pl.ds((4 * px + 2 * py + pc) * m_per, m_per), :]

        def copy(k, blk, to, src=None):
            return pltpu.make_async_remote_copy(
                src_ref=rows(*blk) if src is None else src, dst_ref=rows(*blk),
                send_sem=send_sems.at[k], recv_sem=recv_sems.at[k], device_id=to, device_id_type=MESH)

        mine = pltpu.make_async_copy(x_ref, rows(*me), local_sem)
        mine.start()
        first = [copy(0, me, sibling, src=x_ref)]
        first += [copy(1 + j, me, (*chip, c), src=x_ref) for j, chip in enumerate(chips)]
        for cp in first:
            cp.start()
        passed = [copy(4 + j, (*chip, c), sibling) for j, chip in enumerate(chips)]
        for j, chip in enumerate(chips):
            copy(1 + j, (*chip, c), me).wait_recv()
            passed[j].start()
        copy(0, sibling, me).wait_recv()
        for j, chip in enumerate(chips):
            copy(4 + j, (*chip, 1 - c), me).wait_recv()
        for cp in first + passed:
            cp.wait_send()
        mine.wait()

    vm = pl.BlockSpec(memory_space=pltpu.VMEM)
    return pl.pallas_call(
        body, name=name, out_shape=SDS((N_DEV * m_per, n), block.dtype), in_specs=[vm], out_specs=vm,
        scratch_shapes=[pltpu.SemaphoreType.DMA((7,)), pltpu.SemaphoreType.DMA((7,)), pltpu.SemaphoreType.DMA],
        compiler_params=pltpu.CompilerParams(vmem_limit_bytes=VMEM_LIMIT))(block)


def sibling_sum(name, grad, other, c_idx):
    _, r, cc = grad.shape
    h = r // 2
    tr = min(256, h)
    g4 = grad.reshape(N_CHIPS, 2, h, cc)

    def body(c_ref, a_ref, b_ref, o_ref):
        o_ref[...] = (a_ref[...].astype(F32) + b_ref[...].astype(F32)).astype(BF16)

    gs = pltpu.PrefetchScalarGridSpec(
        num_scalar_prefetch=1, grid=(N_CHIPS, h // tr),
        in_specs=[pl.BlockSpec((None, None, tr, cc), lambda s, i, cr: (s, cr[0], i, 0)),
                  pl.BlockSpec((None, tr, cc), lambda s, i, cr: (s, i, 0))],
        out_specs=pl.BlockSpec((None, tr, cc), lambda s, i, cr: (s, i, 0)))
    return pl.pallas_call(body, name=name, grid_spec=gs, out_shape=SDS((N_CHIPS, h, cc), BF16),
                          compiler_params=_cp(2))(c_idx, g4, other)


HBM_SPEC = pl.BlockSpec(memory_space=pltpu.HBM)
SEM_SPEC = pl.BlockSpec(memory_space=pltpu.SEMAPHORE)


def _side_effecting():
    return pltpu.CompilerParams(has_side_effects=pltpu.SideEffectType.DATAFLOW_SIDE_EFFECTING)


def _in_hbm(a):
    return pltpu.with_memory_space_constraint(a, pltpu.HBM)


ALL_CHIPS = (0, 1, 2)


def gather_start(name, bufs, after, rel=ALL_CHIPS):
    n = len(bufs)
    nr = len(rel)
    halves = [w.shape[1] // 2 for w in bufs]

    def body(*refs):
        ins = refs[:n]
        send_sems, recv_sems, token = refs[n + 1], refs[n + 2], refs[-1]
        x, y, c = _mesh_pos()
        k = 2 * x + y
        for i in range(n):
            reg = ins[i].at[k, pl.ds(c * halves[i], halves[i]), :]
            for q, j in enumerate(rel):
                chip = _other_chips(x, y)[j]
                pltpu.make_async_remote_copy(src_ref=reg, dst_ref=reg, send_sem=send_sems.at[nr * i + q],
                                             recv_sem=recv_sems.at[nr * i + q], device_id=(*chip, c),
                                             device_id_type=MESH).start()
        token[...] = jnp.zeros_like(token)

    outs = pl.pallas_call(
        body, name=name,
        out_shape=(pltpu.SemaphoreType.DMA((nr * n,)), pltpu.SemaphoreType.DMA((nr * n,)),
                   *[pltpu.HBM(w.shape, w.dtype) for w in bufs], SDS((8, 128), F32)),
        in_specs=[HBM_SPEC] * n + [pl.BlockSpec(memory_space=pl.ANY)],
        out_specs=(SEM_SPEC, SEM_SPEC, *[HBM_SPEC] * n, pl.BlockSpec(memory_space=pltpu.VMEM)),
        input_output_aliases={i: 2 + i for i in range(n)},
        compiler_params=_side_effecting())(*[_in_hbm(w) for w in bufs], after)
    return outs[0], outs[1], list(outs[2:2 + n]), outs[-1]


def gather_wait(name, send_sems, recv_sems, bufs, after, rel=ALL_CHIPS):
    n = len(bufs)
    nr = len(rel)
    halves = [w.shape[1] // 2 for w in bufs]

    def body(*refs):
        ins = refs[:n]
        send_sems, recv_sems = refs[n], refs[n + 1]
        x, y, c = _mesh_pos()
        k = 2 * x + y
        for i in range(n):
            for q, j in enumerate(rel):
                chip = _other_chips(x, y)[j]
                kj = 2 * chip[0] + chip[1]
                cp = pltpu.make_async_remote_copy(
                    src_ref=ins[i].at[k, pl.ds(c * halves[i], halves[i]), :],
                    dst_ref=ins[i].at[kj, pl.ds(c * halves[i], halves[i]), :],
                    send_sem=send_sems.at[nr * i + q], recv_sem=recv_sems.at[nr * i + q], device_id=(*chip, c),
                    device_id_type=MESH)
                cp.wait_send()
                cp.wait_recv()

    return pl.pallas_call(
        body, name=name, out_shape=[pltpu.HBM(w.shape, w.dtype) for w in bufs],
        in_specs=[HBM_SPEC] * n + [SEM_SPEC, SEM_SPEC, pl.BlockSpec(memory_space=pl.ANY)],
        out_specs=[HBM_SPEC] * n, input_output_aliases={i: i for i in range(n)},
        compiler_params=_side_effecting())(*bufs, send_sems, recv_sems, after)


def gather_forward(name, bufs, rel=ALL_CHIPS):
    n = len(bufs)
    halves = [w.shape[1] // 2 for w in bufs]

    def body(*refs):
        outs = refs[n:2 * n]
        send_sems, recv_sems = refs[2 * n:]
        x, y, c = _mesh_pos()
        chips = _other_chips(x, y)

        def copy(i, j, half, to):
            kj = 2 * chips[j][0] + chips[j][1]
            reg = outs[i].at[kj, pl.ds(half * halves[i], halves[i]), :]
            return pltpu.make_async_remote_copy(src_ref=reg, dst_ref=reg, send_sem=send_sems.at[i, j],
                                                recv_sem=recv_sems.at[i, j], device_id=to, device_id_type=MESH)

        cps = [copy(i, j, c, (x, y, 1 - c)) for i in range(n) for j in rel]
        for cp in cps:
            cp.start()
        for i in range(n):
            for j in rel:
                copy(i, j, 1 - c, (x, y, c)).wait_recv()
        for cp in cps:
            cp.wait_send()

    hbm = pl.BlockSpec(memory_space=pl.ANY)
    return pl.pallas_call(
        body, name=name, in_specs=[hbm] * n, out_specs=[hbm] * n,
        out_shape=[SDS(w.shape, w.dtype) for w in bufs], input_output_aliases={i: i for i in range(n)},
        scratch_shapes=[pltpu.SemaphoreType.DMA((n, 3)), pltpu.SemaphoreType.DMA((n, 3))])(*bufs)


def _exchange_plan(kind, srcs, zones):
    x, y, c = _mesh_pos()
    plan = []
    for src, zone in zip(srcs, zones):
        if kind == "chips":
            for j, chip in enumerate(_other_chips(x, y)):
                plan.append((src.at[2 * chip[0] + chip[1]], zone.at[j], (*chip, c)))
        elif kind == "sibling":
            h = zone.shape[1]
            plan.append((src.at[:, pl.ds((1 - c) * h, h), :], zone, (x, y, 1 - c)))
        else:
            peers = [(x, y, 1 - c)] + [(*chip, cc) for chip in _other_chips(x, y) for cc in (c, 1 - c)]
            plan += [(src, zone.at[4 * x + 2 * y + c], peer) for peer in peers]
    return plan


_COPIES_PER_ARRAY = {"chips": 3, "sibling": 1, "all": N_DEV - 1}


def _landing_zones(kind, srcs):
    if kind == "chips":
        return [lax.empty((3,) + t.shape[1:], t.dtype) for t in srcs]
    if kind == "sibling":
        return [lax.empty((t.shape[0], t.shape[1] // 2, t.shape[2]), t.dtype) for t in srcs]
    return [lax.empty((N_DEV,) + t.shape, t.dtype) for t in srcs]


def exchange_start(name, kind, srcs, after):
    n = len(srcs)
    lands = _landing_zones(kind, srcs)
    n_copies = n * _COPIES_PER_ARRAY[kind]

    def body(*refs):
        send_sems, recv_sems, token = refs[2 * n + 1], refs[2 * n + 2], refs[-1]
        for q, (src, dst, dev) in enumerate(_exchange_plan(kind, refs[:n], refs[n:2 * n])):
            pltpu.make_async_remote_copy(src_ref=src, dst_ref=dst, send_sem=send_sems.at[q], recv_sem=recv_sems.at[q],
                                         device_id=dev, device_id_type=MESH).start()
        token[...] = jnp.zeros_like(token)

    outs = pl.pallas_call(
        body, name=name,
        out_shape=(pltpu.SemaphoreType.DMA((n_copies,)), pltpu.SemaphoreType.DMA((n_copies,)),
                   *[pltpu.HBM(t.shape, t.dtype) for t in srcs], *[pltpu.HBM(t.shape, t.dtype) for t in lands],
                   SDS((8, 128), F32)),
        in_specs=[HBM_SPEC] * (2 * n) + [pl.BlockSpec(memory_space=pl.ANY)],
        out_specs=(SEM_SPEC, SEM_SPEC, *[HBM_SPEC] * (2 * n), pl.BlockSpec(memory_space=pltpu.VMEM)),
        input_output_aliases={i: 2 + i for i in range(2 * n)},
        compiler_params=_side_effecting())(*[_in_hbm(t) for t in srcs], *[_in_hbm(t) for t in lands], after)
    return outs[0], outs[1], list(outs[2:2 + n]), list(outs[2 + n:2 + 2 * n]), outs[-1]


def exchange_wait(name, kind, send_sems, recv_sems, srcs, lands, after):
    n = len(srcs)

    def body(*refs):
        send_sems, recv_sems = refs[2 * n], refs[2 * n + 1]
        for q, (src, dst, dev) in enumerate(_exchange_plan(kind, refs[:n], refs[n:2 * n])):
            cp = pltpu.make_async_remote_copy(src_ref=src, dst_ref=dst, send_sem=send_sems.at[q],
                                              recv_sem=recv_sems.at[q], device_id=dev, device_id_type=MESH)
            cp.wait_send()
            cp.wait_recv()

    outs = pl.pallas_call(
        body, name=name, out_shape=[pltpu.HBM(t.shape, t.dtype) for t in srcs + lands],
        in_specs=[HBM_SPEC] * (2 * n) + [SEM_SPEC, SEM_SPEC, pl.BlockSpec(memory_space=pl.ANY)],
        out_specs=[HBM_SPEC] * (2 * n), input_output_aliases={i: i for i in range(2 * n)},
        compiler_params=_side_effecting())(*srcs, *lands, send_sems, recv_sems, after)
    return list(outs[:n]), list(outs[n:])


def chip_sum(name, sums, parts, k_idx):
    _, h, cc = parts.shape
    tr = min(256, h)

    def body(k_ref, own_ref, p_ref, o_ref):
        acc = own_ref[...].astype(F32)
        for s in range(3):
            acc = acc + p_ref[s].astype(F32)
        o_ref[...] = acc

    gs = pltpu.PrefetchScalarGridSpec(
        num_scalar_prefetch=1, grid=(h // tr,),
        in_specs=[pl.BlockSpec((None, tr, cc), lambda i, kr: (kr[0], i, 0)),
                  pl.BlockSpec((3, tr, cc), lambda i, kr: (0, i, 0))],
        out_specs=pl.BlockSpec((tr, cc), lambda i, kr: (i, 0)))
    return pl.pallas_call(body, name=name, grid_spec=gs, out_shape=SDS((h, cc), F32),
                          compiler_params=_cp(1))(k_idx, sums, parts)


def halves_exchange(name, halves):
    n = len(halves)

    def body(*refs):
        ins, outs = refs[:n], refs[n:2 * n]
        send_sems, recv_sems = refs[2 * n:]
        x, y, c = _mesh_pos()
        cps = []
        for i in range(n):
            cp = pltpu.make_async_remote_copy(
                src_ref=ins[i], dst_ref=outs[i], send_sem=send_sems.at[i], recv_sem=recv_sems.at[i],
                device_id=(x, y, 1 - c), device_id_type=MESH)
            cp.start()
            cps.append(cp)
        for cp in cps:
            cp.wait_recv()
        for cp in cps:
            cp.wait_send()

    hbm = pl.BlockSpec(memory_space=pl.ANY)
    return pl.pallas_call(
        body, name=name, in_specs=[hbm] * n, out_specs=[hbm] * n,
        out_shape=[SDS(t.shape, F32) for t in halves],
        scratch_shapes=[pltpu.SemaphoreType.DMA((n,)), pltpu.SemaphoreType.DMA((n,))])(*halves)


def local_step(x, tgt, vecs, lvec, wa, wx, sinks, rel_bias, proj, h, w_in, rest_weights, hook):
    buckets = t5_bucket_table()
    band = bias_band(rel_bias.T, buckets).reshape(N_HEADS, BLOCK, 2 * BLOCK)

    ya, rec = lru_fwd(proj, lvec, wa, wx)
    att = attn_fwd(proj, band, sinks)
    w_lru_out, w_attn_out, w_out, w_ff1, w_ff2 = rest_weights(att[:8, :128] + ya[:8, :128])
    w_lru_out2, w_attn_out2, w_out2 = w_lru_out.reshape(D, D), w_attn_out.reshape(D, D), w_out.reshape(D, D)
    w_ff2_2 = w_ff2.reshape(D_FF, D)
    yab, merged = merge_fwd(ya, att, w_lru_out2, w_attn_out2, proj)
    x1, o1 = outproj_fwd(merged, w_out2, x, vecs)
    f, h2 = ff1_fwd(x1, vecs, w_ff1)
    dx2, do2, sums_f, loss = ff2_loss(f, w_ff2_2, x1, tgt, vecs)

    df = ff2_bwd(do2, w_ff2_2, f)
    g_ff2 = weight_grad("dw_ff2", f, do2, 512, (D_FF, D), (WG_TM, 512), lambda i, j: (i, j), relu2=True)
    dx1, do1, sums_2 = ff1_bwd(df, w_ff1, x1, dx2, o1, vecs)
    g_ff1 = weight_grad("dw_ff1", h2, df, 512, (N_CHIPS, D, D), (None, WG_TM, 512), lambda i, j: (j // 4, i, j % 4))
    dyab, dproj = outproj_bwd(do1, w_out2, yab, proj)
    g_out = weight_grad("dw_out", merged, do1, 512, (D, D), (WG_TM, 512), lambda i, j: (i, j))
    drec, dproj = lruout_bwd(dyab, w_lru_out2, rec, proj, dproj)
    dyab2 = dyab.reshape(2 * x.shape[0], D)
    g_lru_out = weight_grad("dw_lru_out", ya, dyab2, 512, (D, D), (WG_TM, 512), lambda i, j: (i, j))
    datt = attnout_bwd(dyab, w_attn_out2)
    g_attn_out = weight_grad("dw_attn_out", att, dyab2, 512, (D, D), (WG_TM, 512), lambda i, j: (i, j), b_part=1)
    zero = hook("grads_a", [g_lru_out.reshape(N_CHIPS, D // 4, D), g_attn_out.reshape(N_CHIPS, D // 4, D),
                            g_out.reshape(N_CHIPS, D // 4, D), g_ff1, g_ff2.reshape(N_CHIPS, D_FF // 4, D)])
    dproj, dkb, dvb, dband, dsink = attn_bwd(proj, band, sinks + zero, datt, dproj)
    zero = hook("after_attn_bwd", dkb)
    dproj = dkv_combine(dkb, dvb, dproj)
    dproj, sums_l, d_wa, d_wx = lru_bwd(proj, rec, drec, lvec + zero, wa, wx, dproj)
    hook("lru_grads", (d_wa, d_wx))
    per = IN_SHARD // IN_TILE
    g_in = weight_grad("dw_in", h, dproj, IN_TILE, (N_CHIPS, D, IN_SHARD), (None, WG_TM, IN_TILE),
                       lambda i, j: (j // per, i, j % per))
    zero = hook("grads_b", [g_in])
    grad_x, sums_1 = inproj_bwd(dproj, w_in, x, dx1, vecs + zero)
    d_rel_bias = bias_band_bwd(dband.reshape(N_HEADS, BLOCK * 2 * BLOCK), buckets)

    small = dict(sums_f=sums_f, sums_2=sums_2, sums_1=sums_1, sums_l=sums_l, d_wa=d_wa, d_wx=d_wx,
                 d_sinks=dsink[:, 0], d_rel_bias=d_rel_bias)
    return loss, grad_x, small


def _pad_rows(a, rows):
    return jnp.concatenate([a, jnp.zeros((rows - a.shape[0], a.shape[1]), a.dtype)], axis=0)


def kernel(x, c, w_ada, b_ada, norm1_g, w_in, conv_w, conv_b, lru_wa, lru_ba, lru_wx, lru_bx, lru_lambda, w_lru_out, w_attn_out, attn_sinks, rel_bias, w_out, norm2_g, w_ff1, w_ff2, final_g, loss_target, m_w_ada, m_b_ada, m_norm1_g, m_w_in, m_conv_w, m_conv_b, m_lru_wa, m_lru_ba, m_lru_wx, m_lru_bx, m_lru_lambda, m_w_lru_out, m_w_attn_out, m_attn_sinks, m_rel_bias, m_w_out, m_norm2_g, m_w_ff1, m_w_ff2, m_final_g, v_w_ada, v_b_ada, v_norm1_g, v_w_in, v_conv_w, v_conv_b, v_lru_wa, v_lru_ba, v_lru_wx, v_lru_bx, v_lru_lambda, v_w_lru_out, v_w_attn_out, v_attn_sinks, v_rel_bias, v_w_out, v_norm2_g, v_w_ff1, v_w_ff2, v_final_g):
    xi, yi, ci = _mesh_pos()
    chip = 2 * xi + yi
    dev = 2 * chip + ci
    z8 = jnp.zeros((8, D), F32)

    conv_rows = jnp.concatenate([conv_w[0], jnp.zeros((4, D - D // 4), F32)], axis=1)
    pack0 = jnp.concatenate([c, conv_rows, jnp.zeros((3, D), F32)], axis=0)
    g0 = all_gather_small("gather_cond", pack0).reshape(N_DEV, 8, D)
    c_all = g0[:, 0, :]
    conv_full = jnp.concatenate([g0[2 * k, 1:5, :D // 4] for k in range(N_CHIPS)], axis=1)
    c16 = jnp.concatenate([c_all, z8], axis=0)
    b_cols = lax.dynamic_slice_in_dim(b_ada, chip * ADA_SHARD, ADA_SHARD, axis=1)
    mod_c = mod_columns(c16, w_ada[0], b_cols)
    g1 = all_gather_small("gather_mod", mod_c).reshape(N_DEV, 16, ADA_SHARD)
    mod = jnp.concatenate([lax.dynamic_index_in_dim(g1[2 * k], dev, axis=0, keepdims=False) for k in range(N_CHIPS)])
    shift1, scale1, gate1, shift2, scale2, gate2 = [mod[i * D:(i + 1) * D] for i in range(6)]
    vecs = jnp.stack([norm1_g[0], scale1, shift1, gate1, norm2_g[0], scale2, shift2, gate2, final_g]
                     + [jnp.zeros((D,), F32)] * 7)
    lvec = jnp.concatenate([conv_full, conv_b, lru_ba, lru_bx, lru_lambda], axis=0)

    shards = [w_in[0], w_lru_out[0], w_attn_out[0], w_out[0], w_ff1[0], w_ff2[0]]
    names = ["w_in", "w_lru_out", "w_attn_out", "w_out", "w_ff1", "w_ff2"]
    k_idx = jnp.reshape(chip, (1,)).astype(jnp.int32)
    c_idx = jnp.reshape(ci, (1,)).astype(jnp.int32)
    near, far = (0, 1), (2,)
    shard_of = lambda flip: jnp.reshape(chip ^ flip, (1,)).astype(jnp.int32)
    x2d = x[0]
    n_send, n_recv, w_in_buf, _ = gather_start(
        "gather_start_in_near", [cast_into_slot("cast_w_in", shards[0], k_idx)], vecs, near)
    proj, h = inproj_fwd("inproj_fwd_own", x2d, None, vecs, w_in_buf[0], None, k_idx)
    slots = [cast_into_slot("cast_" + nm, w, k_idx) for nm, w in zip(names[1:], shards[1:])]
    w_in_buf = gather_forward("gather_forward_in_near", gather_wait(
        "gather_wait_in_near", n_send, n_recv, w_in_buf, proj[:8, :128] + slots[-1][0, :8, :128], near), near)
    f_send, f_recv, w_in_buf, _ = gather_start("gather_start_in_far", w_in_buf, w_in_buf[0], far)
    proj = inproj_fwd("inproj_fwd_x", x2d, h, vecs, w_in_buf[0], proj, shard_of(2))
    proj = inproj_fwd("inproj_fwd_y", x2d, h, vecs, w_in_buf[0], proj, shard_of(1))
    w_in_buf = gather_forward("gather_forward_in_far", gather_wait(
        "gather_wait_in_far", f_send, f_recv, w_in_buf, proj[:8, :128], far), far)
    proj = inproj_fwd("inproj_fwd_d", x2d, h, vecs, w_in_buf[0], proj, shard_of(3))
    w_in_full = w_in_buf[0]
    g_send, g_recv, in_flight, token = gather_start("gather_start_rest", slots, w_in_full)
    vecs = vecs + token[0, 0]
    pending = {}

    def rest_weights(after):
        return gather_forward("gather_forward_rest", gather_wait("gather_wait_rest", g_send, g_recv, in_flight, after))

    def reduce_hook(event, payload):
        if event == "grads_a":
            pending["sib_a"] = exchange_start("sibling_start_a", "sibling", payload, payload[0])
            return pending["sib_a"][-1][0, 0]
        if event == "lru_grads":
            pack_w = jnp.concatenate([payload[0].reshape(128, D), payload[1].reshape(128, D)], axis=0).astype(BF16)
            pending["lru_w"] = exchange_start("lru_w_grads_start", "all", [pack_w], pack_w)
            return pending["lru_w"][-1][0, 0]
        if event == "grads_b":
            pending["sib_b"] = exchange_start("sibling_start_b", "sibling", payload, pending["lru_w"][-1])
            return pending["sib_b"][-1][0, 0]
        return chips_start("a", names[1:], payload)

    def chips_start(tag, nms, after):
        send_sems, recv_sems, grads, lands, _ = pending["sib_" + tag]
        grads, lands = exchange_wait("sibling_wait_" + tag, "sibling", send_sems, recv_sems, grads, lands, after)
        sums = [sibling_sum("sibling_sum_" + nm, g, o, c_idx) for nm, g, o in zip(nms, grads, lands)]
        pending[tag] = exchange_start("exchange_start_" + tag, "chips", sums, sums[0])
        return pending[tag][-1][0, 0]

    loss_t, grad_x, small = local_step(
        x2d, loss_target[0], vecs, lvec, lru_wa[0].astype(BF16), lru_wx[0].astype(BF16),
        attn_sinks[0], rel_bias, proj, h, w_in_full, rest_weights, reduce_hook)
    chips_start("b", names[:1], grad_x)

    big_m = dict(zip(names, [m_w_in, m_w_lru_out, m_w_attn_out, m_w_out, m_w_ff1, m_w_ff2]))
    big_v = dict(zip(names, [v_w_in, v_w_lru_out, v_w_attn_out, v_w_out, v_w_ff1, v_w_ff2]))
    local_w = dict(zip(names, shards))
    g_big, d_big, nm_big, nv_big = {}, {}, {}, {}

    def links_done(tag, after):
        send_sems, recv_sems, sums, lands, _ = pending[tag]
        return exchange_wait("exchange_wait_" + tag, "chips", send_sems, recv_sems, sums, lands, after)

    def finish_reduce(nms, sums, lands):
        mine = [chip_sum("chip_sum_" + nm, t, p, k_idx) for nm, t, p in zip(nms, sums, lands)]
        theirs = halves_exchange("halves_exchange_" + nms[0], mine)
        for nm, a, b in zip(nms, mine, theirs):
            g2, dl, m2, v2 = adamw_big("adamw_" + nm, local_w[nm], a, b, big_m[nm][0], big_v[nm][0], c_idx)
            g_big[nm], d_big[nm], nm_big[nm], nv_big[nm] = g2[None], dl[None], m2[None], v2[None]
        return lax.optimization_barrier(tuple(nv_big[nm] for nm in nms))[0]

    done_a = finish_reduce(names[1:], *links_done("a", pending["b"][-1]))
    sums_b, lands_b = links_done("b", done_a)
    w_send, w_recv, w_src, w_lands, _ = pending["lru_w"]
    w_src, w_lands = exchange_wait("lru_w_grads_wait", "all", w_send, w_recv, w_src, w_lands, lands_b[0])

    sums_f, sums_2, sums_1, sums_l = small["sums_f"], small["sums_2"], small["sums_1"], small["sums_l"]
    vec_rows = jnp.stack([sums_1[2], sums_2[2], sums_f[0], sums_l[L_CB], sums_l[L_BA], sums_l[L_BX],
                          sums_l[L_LAM], jnp.zeros((D,), F32)])
    mod_rows = jnp.stack([sums_1[0], sums_1[1], sums_2[3], sums_2[0], sums_2[1], sums_f[1],
                          jnp.zeros((D,), F32), jnp.zeros((D,), F32)])
    att_rows = jnp.concatenate([
        jnp.concatenate([small["d_sinks"], jnp.zeros((D - N_HEADS,), F32)])[None],
        jnp.concatenate([small["d_rel_bias"].reshape(-1), jnp.zeros((D - N_BUCKETS * N_HEADS,), F32)])[None],
        jnp.zeros((6, D), F32)], axis=0)
    pack = jnp.concatenate([vec_rows, _pad_rows(sums_l[0:4], 8), mod_rows, att_rows], axis=0)
    pack, lru_w_all = lax.optimization_barrier((pack, w_lands[0]))
    gathered = all_gather_small("gather_small_grads", pack).reshape(N_DEV, P_WA, D)
    total = sum_devices("sum_small_grads", gathered)
    total_w = sum_devices("sum_lru_w_grads", lax.dynamic_update_slice(lru_w_all, w_src[0][None], (dev, 0, 0)))
    dmod_all = gathered[:, P_MOD:P_MOD + 6, :].reshape(N_DEV, 6 * D)
    dmod16 = jnp.concatenate([lax.dynamic_slice_in_dim(dmod_all, chip * ADA_SHARD, ADA_SHARD, axis=1),
                              jnp.zeros((8, ADA_SHARD), F32)], axis=0)
    g_w_ada, d_w_ada, nm_w_ada, nv_w_ada = wada_update(c16, dmod16, w_ada[0], m_w_ada[0], v_w_ada[0])
    finish_reduce(names[:1], sums_b, lands_b)
    loss = lax.psum(lax.optimization_barrier((loss_t, total))[0][0, 0], ("x", "y", "c"))

    conv_g = lax.dynamic_slice_in_dim(total[P_CONVW:P_CONVW + 4], chip * (D // 4), D // 4, axis=1)
    sm_names = ["b_ada", "norm1_g", "conv_w", "conv_b", "lru_wa", "lru_ba", "lru_wx", "lru_bx", "lru_lambda",
                "attn_sinks", "rel_bias", "norm2_g", "final_g"]
    sm_w = [b_ada.reshape(6, D), norm1_g, conv_w[0], conv_b, lru_wa.reshape(128, D), lru_ba, lru_wx.reshape(128, D),
            lru_bx, lru_lambda, attn_sinks, rel_bias, norm2_g, final_g[None]]
    sm_m = [m_b_ada.reshape(6, D), m_norm1_g, m_conv_w[0], m_conv_b, m_lru_wa.reshape(128, D), m_lru_ba,
            m_lru_wx.reshape(128, D), m_lru_bx, m_lru_lambda, m_attn_sinks, m_rel_bias, m_norm2_g, m_final_g[None]]
    sm_v = [v_b_ada.reshape(6, D), v_norm1_g, v_conv_w[0], v_conv_b, v_lru_wa.reshape(128, D), v_lru_ba,
            v_lru_wx.reshape(128, D), v_lru_bx, v_lru_lambda, v_attn_sinks, v_rel_bias, v_norm2_g, v_final_g[None]]
    sm_g = [total[P_MOD:P_MOD + 6], total[0:1], conv_g, total[3:4], total_w[0:128], total[4:5],
            total_w[128:256], total[5:6], total[6:7], total[P_ATT:P_ATT + 1, :N_HEADS],
            total[P_ATT + 1, :N_BUCKETS * N_HEADS].reshape(N_BUCKETS, N_HEADS), total[1:2], total[2:3]]
    sm_d, sm_nm, sm_nv = adamw_small(sm_w, sm_g, sm_m, sm_v)
    shapes = dict(b_ada=b_ada.shape, norm1_g=norm1_g.shape, conv_w=conv_w.shape, conv_b=conv_b.shape,
                  lru_wa=lru_wa.shape, lru_ba=lru_ba.shape, lru_wx=lru_wx.shape, lru_bx=lru_bx.shape,
                  lru_lambda=lru_lambda.shape, attn_sinks=attn_sinks.shape, rel_bias=rel_bias.shape,
                  norm2_g=norm2_g.shape, final_g=final_g.shape)
    grads = dict(w_ada=g_w_ada[None], **g_big)
    deltas = dict(w_ada=d_w_ada[None], **d_big)
    new_m = dict(w_ada=nm_w_ada[None], **nm_big)
    new_v = dict(w_ada=nv_w_ada[None], **nv_big)
    for i, nm in enumerate(sm_names):
        grads[nm] = sm_g[i].reshape(shapes[nm])
        deltas[nm] = sm_d[i].reshape(shapes[nm])
        new_m[nm] = sm_nm[i].reshape(shapes[nm])
        new_v[nm] = sm_nv[i].reshape(shapes[nm])
    order = ["w_ada", "b_ada", "norm1_g", "w_in", "conv_w", "conv_b", "lru_wa", "lru_ba", "lru_wx", "lru_bx",
             "lru_lambda", "w_lru_out", "w_attn_out", "attn_sinks", "rel_bias", "w_out", "norm2_g", "w_ff1", "w_ff2",
             "final_g"]
    return (loss, grad_x[None], *[grads[n] for n in order], *[deltas[n] for n in order],
            *[new_m[n] for n in order], *[new_v[n] for n in order])
```

```python
import math

import numpy as np
import jax
import jax.numpy as jnp
from jax import lax
from jax.experimental import pallas as pl
from jax.experimental.pallas import tpu as pltpu

F32 = jnp.float32
BF16 = jnp.bfloat16
SDS = jax.ShapeDtypeStruct
MESH = pl.DeviceIdType.MESH

D = 2048
D_FF = 4 * D
N_HEADS = 32
HEAD_DIM = 64
BLOCK = 128
N_LRU_BLOCKS = 16
LRU_C = 8.0
EPS = 1e-6
NEG_INF = -1e30
N_BUCKETS = 32
MAX_DISTANCE = 128
IN_W = 10752
IN_SHARD = IN_W // 4
IN_TILE = 896
ADA_SHARD = 6 * D // 4
OFF_LRU, OFF_GATE, OFF_Q, OFF_K, OFF_V, OFF_GA, OFF_GB = 0, 2048, 4096, 6144, 6400, 6656, 8704
SCALE = HEAD_DIM ** -0.5
N_CHIPS = 4
N_DEV = 8

ADAM_LR, ADAM_B1, ADAM_B2, ADAM_EPS, ADAM_WD, ADAM_STEP = 0.001, 0.9, 0.999, 1e-08, 0.01, 10
ADAM_C1 = 1.0 - ADAM_B1 ** ADAM_STEP
ADAM_C2 = 1.0 - ADAM_B2 ** ADAM_STEP

VMEM_LIMIT = 52 * 2 ** 20
SUB = 128
WG_TM = 1024
V_G1, V_SCALE1, V_SHIFT1, V_GATE1, V_G2, V_SCALE2, V_SHIFT2, V_GATE2, V_G3 = range(9)
L_CW0, L_CB, L_BA, L_BX, L_LAM = 0, 4, 5, 6, 7
P_VEC, P_CONVW, P_MOD, P_ATT, P_WA = 0, 8, 16, 24, 32


def _cp(n_axes):
    return pltpu.CompilerParams(dimension_semantics=("arbitrary",) * n_axes, vmem_limit_bytes=VMEM_LIMIT)


def _dot(a, b):
    return jnp.dot(a, b, preferred_element_type=F32)


def _dot_nt(a, b):
    return lax.dot_general(a, b, (((1,), (1,)), ((), ())), preferred_element_type=F32)


def _dot_tn(a, b):
    return lax.dot_general(a, b, (((0,), (0,)), ((), ())), preferred_element_type=F32)


_G0 = math.sqrt(2.0 / math.pi)
_G1 = 0.044715


def _gelu(x):
    return 0.5 * x * (1.0 + jnp.tanh(_G0 * (x + _G1 * x * x * x)))


def _gelu_grad(x):
    x2 = x * x
    t = jnp.tanh(_G0 * (x + _G1 * x * x2))
    return 0.5 * (1.0 + t) + 0.5 * x * (1.0 - t * t) * _G0 * (1.0 + 3.0 * _G1 * x2)


def _sigmoid(x):
    return 0.5 * jnp.tanh(0.5 * x) + 0.5


def _one_minus_exp2(x):
    t = jnp.tanh(x)
    return (-2.0 * t) / (1.0 - t)


def _softplus(z):
    e = jnp.exp(-jnp.abs(z))
    u = 1.0 + e
    l1p = jnp.where(u == 1.0, e, jnp.log(u) * e / (u - 1.0))
    return jnp.maximum(z, 0.0) + l1p


def _adamw_math(w, g, m, v):
    m2 = ADAM_B1 * m + (1.0 - ADAM_B1) * g
    v2 = ADAM_B2 * v + (1.0 - ADAM_B2) * (g * g)
    m_hat = m2 / ADAM_C1
    v_hat = v2 / ADAM_C2
    delta = -ADAM_LR * (m_hat / (jnp.sqrt(v_hat) + ADAM_EPS) + ADAM_WD * w)
    return delta, m2, v2


def _rms_parts(xv):
    r = lax.rsqrt(jnp.mean(xv * xv, axis=-1, keepdims=True) + EPS)
    return r, xv * r


def _row_fetches(hbm_refs, bufs, sems, i, rows):
    return [pltpu.make_async_copy(h.at[pl.ds(i * rows, rows), :], b, sems.at[n])
            for n, (h, b) in enumerate(zip(hbm_refs, bufs))]


def _modulated_norm(x_ref, v_ref, row_g, row_scale, row_shift, h_ref, rows):
    g, scale, shift = v_ref[row_g:row_g + 1, :], v_ref[row_scale:row_scale + 1, :], v_ref[row_shift:row_shift + 1, :]

    def sub(rb, carry):
        rs = pl.ds(pl.multiple_of(rb * SUB, SUB), SUB)
        _, xh = _rms_parts(x_ref[rs, :])
        h_ref[rs, :] = ((xh * g) * (1.0 + scale) + shift).astype(BF16)
        return carry

    lax.fori_loop(0, rows // SUB, sub, 0)


def inproj_fwd(name, x, h, vecs, w_in, proj, shard):
    s = x.shape[0]
    tm = min(1024, s)
    per = IN_SHARD // IN_TILE
    first = h is None

    def body(*refs):
        if first:
            _, x_ref, v_ref, w_ref, proj_ref, h_ref = refs

            @pl.when(pl.program_id(1) == 0)
            def _():
                _modulated_norm(x_ref, v_ref, V_G1, V_SCALE1, V_SHIFT1, h_ref, tm)
        else:
            _, h_ref, w_ref, _, proj_ref = refs
        proj_ref[...] = _dot(h_ref[...], w_ref[...]).astype(BF16)

    rows = pl.BlockSpec((tm, D), lambda i, j, sr: (i, 0))
    w_spec = pl.BlockSpec((None, D, IN_TILE), lambda i, j, sr: (sr[0], 0, j))
    proj_spec = pl.BlockSpec((tm, IN_TILE), lambda i, j, sr: (i, sr[0] * per + j))
    if first:
        gs = pltpu.PrefetchScalarGridSpec(
            num_scalar_prefetch=1, grid=(s // tm, per),
            in_specs=[rows, pl.BlockSpec((16, D), lambda i, j, sr: (0, 0)), w_spec], out_specs=[proj_spec, rows])
        return pl.pallas_call(body, name=name, grid_spec=gs, out_shape=[SDS((s, IN_W), BF16), SDS((s, D), BF16)],
                              compiler_params=_cp(2))(shard, x, vecs, w_in)
    gs = pltpu.PrefetchScalarGridSpec(
        num_scalar_prefetch=1, grid=(s // tm, per),
        in_specs=[rows, w_spec, pl.BlockSpec(memory_space=pl.ANY)], out_specs=proj_spec)
    return pl.pallas_call(body, name=name, grid_spec=gs, out_shape=SDS((s, IN_W), BF16),
                          input_output_aliases={3: 0}, compiler_params=_cp(2))(shard, h, w_in, proj)


def _lru_block_fwd(xbuf, lv_ref, wa_ref, wx_ref, b, t, first):
    cs = slice(b * 128, (b + 1) * 128)
    x0 = xbuf[pl.ds(8, t), cs]
    x1 = xbuf[pl.ds(7, t), cs]
    x2 = xbuf[pl.ds(6, t), cs]
    x3 = xbuf[pl.ds(5, t), cs]
    xc = (lv_ref[L_CB:L_CB + 1, cs] + lv_ref[3:4, cs] * x0 + lv_ref[2:3, cs] * x1
          + lv_ref[1:2, cs] * x2 + lv_ref[0:1, cs] * x3)
    xcb = xc.astype(BF16)
    r = _sigmoid(_dot(xcb, wa_ref[b]) + lv_ref[L_BA:L_BA + 1, cs])
    ig = _sigmoid(_dot(xcb, wx_ref[b]) + lv_ref[L_BX:L_BX + 1, cs])
    sp = _softplus(-lv_ref[L_LAM:L_LAM + 1, cs])
    log_a = (-LRU_C) * r * sp
    a = jnp.exp(log_a)
    mult = jnp.where(first, 1.0, jnp.sqrt(_one_minus_exp2(log_a)))
    return (x0, x1, x2, x3), xc, xcb, r, ig, sp, a, mult


def lru_fwd(proj, lvec, wa, wx):
    s = proj.shape[0]
    t = min(256, s)

    def body(lx_ref, gate_ref, lv_ref, wa_ref, wx_ref, ya_ref, rec_ref, xbuf, a_s, u_s, hc):
        i = pl.program_id(0)

        @pl.when(i == 0)
        def _():
            xbuf[pl.ds(0, 8), :] = jnp.zeros((8, D), F32)
            hc[...] = jnp.zeros((8, D), F32)

        @pl.when(i > 0)
        def _():
            xbuf[pl.ds(0, 8), :] = xbuf[pl.ds(t, 8), :]

        xbuf[pl.ds(8, t), :] = lx_ref[...].astype(F32)
        first = (lax.broadcasted_iota(jnp.int32, (t, 128), 0) + i * t) == 0
        for b in range(N_LRU_BLOCKS):
            cs = slice(b * 128, (b + 1) * 128)
            _, xc, _, _, ig, _, a, mult = _lru_block_fwd(xbuf, lv_ref, wa_ref, wx_ref, b, t, first)
            a_s[:, cs] = a
            u_s[:, cs] = mult * (ig * xc)

        def step(tt, h):
            h = a_s[pl.ds(tt, 1), :] * h + u_s[pl.ds(tt, 1), :]
            rec_ref[pl.ds(tt, 1), :] = h
            return h

        hc[0:1, :] = lax.fori_loop(0, t, step, hc[0:1, :], unroll=8)
        for b in range(N_LRU_BLOCKS):
            cs = slice(b * 128, (b + 1) * 128)
            ya_ref[:, cs] = (rec_ref[:, cs] * _gelu(gate_ref[:, cs].astype(F32))).astype(BF16)

    return pl.pallas_call(
        body, name="lru_fwd", grid=(s // t,),
        in_specs=[pl.BlockSpec((t, D), lambda i: (i, OFF_LRU // D)),
                  pl.BlockSpec((t, D), lambda i: (i, OFF_GATE // D)),
                  pl.BlockSpec((8, D), lambda i: (0, 0)),
                  pl.BlockSpec((N_LRU_BLOCKS, 128, 128), lambda i: (0, 0, 0)),
                  pl.BlockSpec((N_LRU_BLOCKS, 128, 128), lambda i: (0, 0, 0))],
        out_specs=[pl.BlockSpec((t, D), lambda i: (i, 0)), pl.BlockSpec((t, D), lambda i: (i, 0))],
        out_shape=[SDS((s, D), BF16), SDS((s, D), F32)],
        scratch_shapes=[pltpu.VMEM((t + 8, D), F32), pltpu.VMEM((t, D), F32), pltpu.VMEM((t, D), F32),
                        pltpu.VMEM((8, D), F32)],
        compiler_params=_cp(1))(proj, proj, lvec, wa, wx)


def t5_bucket_table():
    qi = np.arange(BLOCK)[:, None]
    ki = np.arange(2 * BLOCK)[None, :]
    rel = qi + BLOCK - ki
    relc = np.maximum(rel, 0)
    max_exact = N_BUCKETS // 2
    relf = np.maximum(relc, 1).astype(np.float32)
    large = max_exact + (np.log(relf / np.float32(max_exact)) / np.float32(math.log(MAX_DISTANCE / max_exact))
                         * np.float32(N_BUCKETS - max_exact)).astype(np.int32)
    large = np.minimum(large, N_BUCKETS - 1)
    bucket = np.where(relc < max_exact, relc, large)
    bucket = np.where((rel >= 0) & (rel < BLOCK), bucket, -1)
    return jnp.asarray(bucket.reshape(1, BLOCK * 2 * BLOCK), jnp.int32)


def bias_band(rel_bias_t, buckets):
    n = BLOCK * 2 * BLOCK
    tn = 4096

    def body(bk_ref, rb_ref, o_ref):
        row = lax.broadcasted_iota(jnp.int32, (N_BUCKETS, tn), 0)
        oh = jnp.where(row == bk_ref[...], 1.0, 0.0).astype(BF16)
        rb = rb_ref[...]
        p0 = rb.astype(BF16)
        r1 = rb - p0.astype(F32)
        p1 = r1.astype(BF16)
        p2 = (r1 - p1.astype(F32)).astype(BF16)
        o_ref[...] = _dot(p0, oh) + _dot(p1, oh) + _dot(p2, oh)

    return pl.pallas_call(
        body, name="bias_band", grid=(n // tn,),
        in_specs=[pl.BlockSpec((1, tn), lambda i: (0, i)), pl.BlockSpec((N_HEADS, N_BUCKETS), lambda i: (0, 0))],
        out_specs=pl.BlockSpec((N_HEADS, tn), lambda i: (0, i)),
        out_shape=SDS((N_HEADS, n), F32), compiler_params=_cp(1))(buckets, rel_bias_t)


def bias_band_bwd(dband, buckets):
    n = BLOCK * 2 * BLOCK
    tn = 4096

    def body(bk_ref, d_ref, o_ref):
        @pl.when(pl.program_id(0) == 0)
        def _():
            o_ref[...] = jnp.zeros_like(o_ref)
        row = lax.broadcasted_iota(jnp.int32, (N_BUCKETS, tn), 0)
        oh = jnp.where(row == bk_ref[...], 1.0, 0.0).astype(BF16)
        dv = d_ref[...]
        p0 = dv.astype(BF16)
        r1 = dv - p0.astype(F32)
        p1 = r1.astype(BF16)
        p2 = (r1 - p1.astype(F32)).astype(BF16)
        o_ref[...] += _dot_nt(oh, p0) + _dot_nt(oh, p1) + _dot_nt(oh, p2)

    return pl.pallas_call(
        body, name="bias_band_bwd", grid=(n // tn,),
        in_specs=[pl.BlockSpec((1, tn), lambda i: (0, i)), pl.BlockSpec((N_HEADS, tn), lambda i: (0, i))],
        out_specs=pl.BlockSpec((N_BUCKETS, N_HEADS), lambda i: (0, 0)),
        out_shape=SDS((N_BUCKETS, N_HEADS), F32), compiler_params=_cp(1))(buckets, dband)


def _dup_half(band, which):
    lane = lax.broadcasted_iota(jnp.int32, band.shape, 1)
    rolled = pltpu.roll(band, 64, 1)
    keep = (lane < 64) if which == 0 else (lane >= 64)
    return jnp.where(keep, band, rolled)


def _attn_probs(scores, bias, sink, valid):
    sc = jnp.where(valid, scores * SCALE + bias, NEG_INF)
    m = jnp.maximum(jnp.max(sc, axis=-1, keepdims=True), sink)
    e = jnp.exp(sc - m)
    es = jnp.exp(sink - m)
    inv = 1.0 / (jnp.sum(e, axis=-1, keepdims=True) + es)
    return e * inv, es * inv


def _stack_heads(src_ref, kv, dst):
    lane = lax.broadcasted_iota(jnp.int32, (BLOCK, 128), 1)
    for jj in range(4):
        slab = src_ref[:, (4 * kv + jj) * 128:(4 * kv + jj + 1) * 128]
        for hh in range(2):
            keep = (lane < 64) if hh == 0 else (lane >= 64)
            dst[pl.ds((2 * jj + hh) * BLOCK, BLOCK), :] = jnp.where(keep, slab, jnp.zeros_like(slab))


def _unstack_heads(stacked, dst_ref, kv):
    lane = lax.broadcasted_iota(jnp.int32, (BLOCK, 128), 1)
    for jj in range(4):
        lo = stacked[(2 * jj) * BLOCK:(2 * jj + 1) * BLOCK]
        hi = stacked[(2 * jj + 1) * BLOCK:(2 * jj + 2) * BLOCK]
        dst_ref[:, (4 * kv + jj) * 128:(4 * kv + jj + 1) * 128] = jnp.where(lane < 64, lo, hi).astype(dst_ref.dtype)


def _band_valid(n):
    qi = lax.broadcasted_iota(jnp.int32, (BLOCK, 2 * BLOCK), 0)
    ki = lax.broadcasted_iota(jnp.int32, (BLOCK, 2 * BLOCK), 1)
    rel = qi + BLOCK - ki
    return (rel >= 0) & (rel < BLOCK) & ((ki >= BLOCK) | (n > 0))


def _kv_bands(prev_ref, cur_ref):
    band = jnp.concatenate([prev_ref[...].astype(F32), cur_ref[...].astype(F32)], axis=0)
    return [_dup_half(band, 0).astype(BF16), _dup_half(band, 1).astype(BF16)]


def attn_fwd(proj, band, sinks):
    s = proj.shape[0]
    nb = s // BLOCK
    qw = 1024

    def body(sk_ref, q_ref, kp_ref, kc_ref, vp_ref, vc_ref, b_ref, o_ref, qs_buf, s_buf, p_buf):
        n = pl.program_id(0)
        gp = pl.program_id(1)
        valid = _band_valid(n)
        kks = _kv_bands(kp_ref, kc_ref)
        vvs = _kv_bands(vp_ref, vc_ref)
        for kv in range(2):
            _stack_heads(q_ref, kv, qs_buf)
            s_buf[...] = _dot_nt(qs_buf[...], kks[kv])
            for hq in range(8):
                hl = 8 * kv + hq
                rows = pl.ds(hq * BLOCK, BLOCK)
                p, _ = _attn_probs(s_buf[rows, :], b_ref[hl], sk_ref[gp * 16 + hl], valid)
                p_buf[rows, :] = p.astype(BF16)
            _unstack_heads(_dot(p_buf[...], vvs[kv]), o_ref, kv)

    kb, vb = OFF_K // 128, OFF_V // 128
    return pl.pallas_call(
        body, name="attn_fwd", grid=(nb, 2),
        in_specs=[pl.BlockSpec(memory_space=pltpu.SMEM),
                  pl.BlockSpec((BLOCK, qw), lambda n, g: (n, OFF_Q // qw + g)),
                  pl.BlockSpec((BLOCK, 128), lambda n, g: (jnp.maximum(n - 1, 0), kb + g)),
                  pl.BlockSpec((BLOCK, 128), lambda n, g: (n, kb + g)),
                  pl.BlockSpec((BLOCK, 128), lambda n, g: (jnp.maximum(n - 1, 0), vb + g)),
                  pl.BlockSpec((BLOCK, 128), lambda n, g: (n, vb + g)),
                  pl.BlockSpec((16, BLOCK, 2 * BLOCK), lambda n, g: (g, 0, 0))],
        out_specs=pl.BlockSpec((BLOCK, qw), lambda n, g: (n, g)),
        out_shape=SDS((s, D), BF16),
        scratch_shapes=[pltpu.VMEM((8 * BLOCK, 128), BF16), pltpu.VMEM((8 * BLOCK, 2 * BLOCK), F32),
                        pltpu.VMEM((8 * BLOCK, 2 * BLOCK), BF16)],
        compiler_params=_cp(2))(sinks, proj, proj, proj, proj, proj, band)


def merge_fwd(ya, att, w_lru_out, w_attn_out, proj):
    s = ya.shape[0]
    tm, tn = min(1024, s), 512

    def body(ya_ref, at_ref, wl_ref, wt_ref, ga_ref, gb_ref, yab_ref, mg_ref):
        y_a = _dot(ya_ref[...], wl_ref[...])
        y_b = _dot(at_ref[...], wt_ref[...])
        yab_ref[0] = y_a.astype(BF16)
        yab_ref[1] = y_b.astype(BF16)
        mg_ref[...] = (_sigmoid(ga_ref[...].astype(F32)) * y_a + _sigmoid(gb_ref[...].astype(F32)) * y_b).astype(BF16)

    return pl.pallas_call(
        body, name="merge_fwd", grid=(s // tm, D // tn),
        in_specs=[pl.BlockSpec((tm, D), lambda i, j: (i, 0)), pl.BlockSpec((tm, D), lambda i, j: (i, 0)),
                  pl.BlockSpec((D, tn), lambda i, j: (0, j)), pl.BlockSpec((D, tn), lambda i, j: (0, j)),
                  pl.BlockSpec((tm, tn), lambda i, j: (i, OFF_GA // tn + j)),
                  pl.BlockSpec((tm, tn), lambda i, j: (i, OFF_GB // tn + j))],
        out_specs=[pl.BlockSpec((2, tm, tn), lambda i, j: (0, i, j)), pl.BlockSpec((tm, tn), lambda i, j: (i, j))],
        out_shape=[SDS((2, s, D), BF16), SDS((s, D), BF16)],
        compiler_params=_cp(2))(ya, att, w_lru_out, w_attn_out, proj, proj)


def outproj_fwd(merged, w_out, x, vecs):
    s = x.shape[0]
    tm, tn = min(1024, s), 512

    def body(m_ref, w_ref, x_ref, v_ref, x1_ref, o1_ref):
        o1 = _dot(m_ref[...], w_ref[...])
        o1_ref[...] = o1.astype(BF16)
        x1_ref[...] = x_ref[...] + v_ref[V_GATE1:V_GATE1 + 1, :] * o1

    return pl.pallas_call(
        body, name="outproj_fwd", grid=(s // tm, D // tn),
        in_specs=[pl.BlockSpec((tm, D), lambda i, j: (i, 0)), pl.BlockSpec((D, tn), lambda i, j: (0, j)),
                  pl.BlockSpec((tm, tn), lambda i, j: (i, j)), pl.BlockSpec((16, tn), lambda i, j: (0, j))],
        out_specs=[pl.BlockSpec((tm, tn), lambda i, j: (i, j)), pl.BlockSpec((tm, tn), lambda i, j: (i, j))],
        out_shape=[SDS((s, D), F32), SDS((s, D), BF16)],
        compiler_params=_cp(2))(merged, w_out, x, vecs)


def ff1_fwd(x1, vecs, w_ff1):
    s = x1.shape[0]
    tm, tn = min(1024, s), 512
    per = D // tn

    def body(x_ref, v_ref, w_ref, f_ref, h_ref):
        @pl.when(pl.program_id(1) == 0)
        def _():
            _modulated_norm(x_ref, v_ref, V_G2, V_SCALE2, V_SHIFT2, h_ref, tm)
        f_ref[...] = _dot(h_ref[...], w_ref[...]).astype(BF16)

    return pl.pallas_call(
        body, name="ff1_fwd", grid=(s // tm, D_FF // tn),
        in_specs=[pl.BlockSpec((tm, D), lambda i, j: (i, 0)), pl.BlockSpec((16, D), lambda i, j: (0, 0)),
                  pl.BlockSpec((None, D, tn), lambda i, j: (j // per, 0, j % per))],
        out_specs=[pl.BlockSpec((tm, tn), lambda i, j: (i, j)), pl.BlockSpec((tm, D), lambda i, j: (i, 0))],
        out_shape=[SDS((s, D_FF), BF16), SDS((s, D), BF16)],
        compiler_params=_cp(2))(x1, vecs, w_ff1)


def ff2_loss(f, w_ff2, x1, tgt, vecs):
    s = x1.shape[0]
    tm, tk = min(512, s), 1024
    nk = D_FF // tk

    def body(f_ref, w_ref, x1_hbm, t_hbm, v_ref, dx2_ref, do2_ref, sums_ref, loss_ref, acc, x1_ref, t_ref, sems):
        i, k = pl.program_id(0), pl.program_id(1)
        fetches = _row_fetches((x1_hbm, t_hbm), (x1_ref, t_ref), sems, i, tm)

        @pl.when((i == 0) & (k == 0))
        def _():
            sums_ref[...] = jnp.zeros_like(sums_ref)
            loss_ref[...] = jnp.zeros_like(loss_ref)

        @pl.when(k == 0)
        def _():
            acc[...] = jnp.zeros_like(acc)
            for cp in fetches:
                cp.start()

        fv = jnp.maximum(f_ref[...].astype(F32), 0.0)
        acc[...] += _dot((fv * fv).astype(BF16), w_ref[...])

        @pl.when(k == nk - 1)
        def _():
            for cp in fetches:
                cp.wait()
            gate2 = v_ref[V_GATE2:V_GATE2 + 1, :]
            g3 = v_ref[V_G3:V_G3 + 1, :]

            def sub(rb, carry):
                rs = pl.ds(pl.multiple_of(rb * SUB, SUB), SUB)
                o2 = acc[rs, :]
                x2 = x1_ref[rs, :] + gate2 * o2
                r3, xh = _rms_parts(x2)
                e = xh * g3 - t_ref[rs, :]
                loss_ref[...] += (0.5 / D) * jnp.sum(e * e)
                dy = e * (1.0 / D)
                sums_ref[0:1, :] += jnp.sum(dy * xh, axis=0, keepdims=True)
                dxh = dy * g3
                dx2 = r3 * (dxh - xh * jnp.mean(dxh * xh, axis=-1, keepdims=True))
                sums_ref[1:2, :] += jnp.sum(dx2 * o2, axis=0, keepdims=True)
                dx2_ref[rs, :] = dx2
                do2_ref[rs, :] = (dx2 * gate2).astype(BF16)
                return carry

            lax.fori_loop(0, tm // SUB, sub, 0)

    return pl.pallas_call(
        body, name="ff2_loss", grid=(s // tm, nk),
        in_specs=[pl.BlockSpec((tm, tk), lambda i, k: (i, k)), pl.BlockSpec((tk, D), lambda i, k: (k, 0)),
                  pl.BlockSpec(memory_space=pl.ANY), pl.BlockSpec(memory_space=pl.ANY),
                  pl.BlockSpec((16, D), lambda i, k: (0, 0))],
        out_specs=[pl.BlockSpec((tm, D), lambda i, k: (i, 0)), pl.BlockSpec((tm, D), lambda i, k: (i, 0)),
                   pl.BlockSpec((8, D), lambda i, k: (0, 0)), pl.BlockSpec((8, 128), lambda i, k: (0, 0))],
        out_shape=[SDS((s, D), F32), SDS((s, D), BF16), SDS((8, D), F32), SDS((8, 128), F32)],
        scratch_shapes=[pltpu.VMEM((tm, D), F32), pltpu.VMEM((tm, D), F32), pltpu.VMEM((tm, D), F32),
                        pltpu.SemaphoreType.DMA((2,))],
        compiler_params=_cp(2))(f, w_ff2, x1, tgt, vecs)


def ff2_bwd(do2, w_ff2, f):
    s = do2.shape[0]
    tm, tn = min(1024, s), 512

    def body(d_ref, w_ref, f_ref, o_ref):
        dff = _dot_nt(d_ref[...], w_ref[...])
        o_ref[...] = (dff * (2.0 * jnp.maximum(f_ref[...].astype(F32), 0.0))).astype(BF16)

    return pl.pallas_call(
        body, name="ff2_bwd", grid=(s // tm, D_FF // tn),
        in_specs=[pl.BlockSpec((tm, D), lambda i, j: (i, 0)), pl.BlockSpec((tn, D), lambda i, j: (j, 0)),
                  pl.BlockSpec((tm, tn), lambda i, j: (i, j))],
        out_specs=pl.BlockSpec((tm, tn), lambda i, j: (i, j)),
        out_shape=SDS((s, D_FF), BF16), compiler_params=_cp(2))(do2, w_ff2, f)


def weight_grad(name, a, b, tn, out_shape, out_block, out_map, relu2=False, b_part=0):
    s, m = a.shape
    n = b.shape[1]
    tm = WG_TM
    chunk = min(1024, s)
    nch = s // chunk

    def body(a_hbm, b_ref, o_ref, a_buf, at_s, sem):
        i = pl.program_id(0)

        @pl.when(pl.program_id(1) == 0)
        def _():
            def fetch(ch):
                return pltpu.make_async_copy(a_hbm.at[pl.ds(ch * chunk, chunk), pl.ds(i * tm, tm)],
                                             a_buf.at[ch % 2], sem.at[ch % 2])
            fetch(0).start()
            for ch in range(nch):
                if ch + 1 < nch:
                    fetch(ch + 1).start()
                fetch(ch).wait()
                av = a_buf[ch % 2]
                if relu2:
                    fv = jnp.maximum(av.astype(F32), 0.0)
                    av = (fv * fv).astype(BF16)
                at_s[:, ch * chunk:(ch + 1) * chunk] = av.T

        o_ref[...] = _dot(at_s[...], b_ref[...]).astype(BF16)

    return pl.pallas_call(
        body, name=name, grid=(m // tm, n // tn),
        in_specs=[pl.BlockSpec(memory_space=pl.ANY), pl.BlockSpec((s, tn), lambda i, j: (b_part, j))],
        out_specs=pl.BlockSpec(out_block, lambda i, j: out_map(i, j)),
        out_shape=SDS(out_shape, BF16),
        scratch_shapes=[pltpu.VMEM((2, chunk, tm), BF16), pltpu.VMEM((tm, s), BF16), pltpu.SemaphoreType.DMA((2,))],
        compiler_params=_cp(2))(a, b)


def ff1_bwd(df, w_ff1, x1, dx2, o1, vecs):
    s = df.shape[0]
    tm, tk = min(512, s), 1024
    nk = D_FF // tk
    per = D // tk

    def body(d_ref, w_ref, x1_hbm, dx2_hbm, o1_hbm, v_ref, dx1_ref, do1_ref, sums_ref, acc, x1_ref, dx2_ref, o1_ref, sems):
        i, k = pl.program_id(0), pl.program_id(1)
        fetches = _row_fetches((x1_hbm, dx2_hbm, o1_hbm), (x1_ref, dx2_ref, o1_ref), sems, i, tm)

        @pl.when((i == 0) & (k == 0))
        def _():
            sums_ref[...] = jnp.zeros_like(sums_ref)

        @pl.when(k == 0)
        def _():
            acc[...] = jnp.zeros_like(acc)
            for cp in fetches:
                cp.start()

        acc[...] += _dot_nt(d_ref[...], w_ref[...])

        @pl.when(k == nk - 1)
        def _():
            for cp in fetches:
                cp.wait()
            g2 = v_ref[V_G2:V_G2 + 1, :]
            scale2 = v_ref[V_SCALE2:V_SCALE2 + 1, :]
            gate1 = v_ref[V_GATE1:V_GATE1 + 1, :]

            def sub(rb, carry):
                rs = pl.ds(pl.multiple_of(rb * SUB, SUB), SUB)
                dh = acc[rs, :]
                r2, xh = _rms_parts(x1_ref[rs, :])
                sums_ref[0:1, :] += jnp.sum(dh, axis=0, keepdims=True)
                sums_ref[1:2, :] += jnp.sum(dh * (xh * g2), axis=0, keepdims=True)
                dxn = dh * (1.0 + scale2)
                sums_ref[2:3, :] += jnp.sum(dxn * xh, axis=0, keepdims=True)
                dxh = dxn * g2
                dx1 = dx2_ref[rs, :] + r2 * (dxh - xh * jnp.mean(dxh * xh, axis=-1, keepdims=True))
                sums_ref[3:4, :] += jnp.sum(dx1 * o1_ref[rs, :].astype(F32), axis=0, keepdims=True)
                dx1_ref[rs, :] = dx1
                do1_ref[rs, :] = (dx1 * gate1).astype(BF16)
                return carry

            lax.fori_loop(0, tm // SUB, sub, 0)

    return pl.pallas_call(
        body, name="ff1_bwd", grid=(s // tm, nk),
        in_specs=[pl.BlockSpec((tm, tk), lambda i, k: (i, k)),
                  pl.BlockSpec((None, D, tk), lambda i, k: (k // per, 0, k % per)),
                  pl.BlockSpec(memory_space=pl.ANY), pl.BlockSpec(memory_space=pl.ANY),
                  pl.BlockSpec(memory_space=pl.ANY), pl.BlockSpec((16, D), lambda i, k: (0, 0))],
        out_specs=[pl.BlockSpec((tm, D), lambda i, k: (i, 0)), pl.BlockSpec((tm, D), lambda i, k: (i, 0)),
                   pl.BlockSpec((8, D), lambda i, k: (0, 0))],
        out_shape=[SDS((s, D), F32), SDS((s, D), BF16), SDS((8, D), F32)],
        scratch_shapes=[pltpu.VMEM((tm, D), F32), pltpu.VMEM((tm, D), F32), pltpu.VMEM((tm, D), F32),
                        pltpu.VMEM((tm, D), BF16), pltpu.SemaphoreType.DMA((3,))],
        compiler_params=_cp(2))(df, w_ff1, x1, dx2, o1, vecs)


def outproj_bwd(do1, w_out, yab, proj):
    s = do1.shape[0]
    tm, tn = min(1024, s), 512
    per = D // tn

    def body(d_ref, w_ref, y_ref, g_ref, dy_ref, dp_ref):
        dm = _dot_nt(d_ref[...], w_ref[...])
        sg = _sigmoid(g_ref[...].astype(F32))
        dy_ref[...] = (dm * sg).astype(BF16)
        dp_ref[...] = (dm * y_ref[...].astype(F32) * sg * (1.0 - sg)).astype(BF16)

    return pl.pallas_call(
        body, name="outproj_bwd", grid=(s // tm, 2 * per),
        in_specs=[pl.BlockSpec((tm, D), lambda i, j: (i, 0)), pl.BlockSpec((tn, D), lambda i, j: (j % per, 0)),
                  pl.BlockSpec((None, tm, tn), lambda i, j: (j // per, i, j % per)),
                  pl.BlockSpec((tm, tn), lambda i, j: (i, OFF_GA // tn + j))],
        out_specs=[pl.BlockSpec((None, tm, tn), lambda i, j: (j // per, i, j % per)),
                   pl.BlockSpec((tm, tn), lambda i, j: (i, OFF_GA // tn + j))],
        out_shape=[SDS((2, s, D), BF16), SDS((s, IN_W), BF16)],
        compiler_params=_cp(2))(do1, w_out, yab, proj)


def lruout_bwd(dyab, w_lru_out, rec, proj, dproj):
    s = rec.shape[0]
    tm, tn = min(1024, s), 512

    def body(d_ref, w_ref, r_ref, g_ref, dp_in, dr_ref, dp_ref):
        dya = _dot_nt(d_ref[...], w_ref[...])
        gate = g_ref[...].astype(F32)
        dr_ref[...] = dya * _gelu(gate)
        dp_ref[...] = (dya * r_ref[...] * _gelu_grad(gate)).astype(BF16)

    return pl.pallas_call(
        body, name="lruout_bwd", grid=(s // tm, D // tn),
        in_specs=[pl.BlockSpec((None, tm, D), lambda i, j: (0, i, 0)), pl.BlockSpec((tn, D), lambda i, j: (j, 0)),
                  pl.BlockSpec((tm, tn), lambda i, j: (i, j)),
                  pl.BlockSpec((tm, tn), lambda i, j: (i, OFF_GATE // tn + j)),
                  pl.BlockSpec(memory_space=pl.ANY)],
        out_specs=[pl.BlockSpec((tm, tn), lambda i, j: (i, j)),
                   pl.BlockSpec((tm, tn), lambda i, j: (i, OFF_GATE // tn + j))],
        out_shape=[SDS((s, D), F32), SDS((s, IN_W), BF16)],
        input_output_aliases={4: 1},
        compiler_params=_cp(2))(dyab, w_lru_out, rec, proj, dproj)


def attnout_bwd(dyab, w_attn_out):
    s = dyab.shape[1]
    tm, tn = min(1024, s), 512

    def body(d_ref, w_ref, o_ref):
        o_ref[...] = _dot_nt(d_ref[...], w_ref[...]).astype(BF16)

    return pl.pallas_call(
        body, name="attnout_bwd", grid=(s // tm, D // tn),
        in_specs=[pl.BlockSpec((None, tm, D), lambda i, j: (1, i, 0)), pl.BlockSpec((tn, D), lambda i, j: (j, 0))],
        out_specs=pl.BlockSpec((tm, tn), lambda i, j: (i, j)),
        out_shape=SDS((s, D), BF16), compiler_params=_cp(2))(dyab, w_attn_out)


def attn_bwd(proj, band, sinks, datt, dproj):
    s = proj.shape[0]
    nb = s // BLOCK
    qw = 1024

    def body(sk_ref, q_ref, kp_ref, kc_ref, vp_ref, vc_ref, b_ref, do_ref, dp_in,
             dq_ref, dkb_ref, dvb_ref, db_ref, ds_ref, qs_buf, dos_buf, s_buf, dp_buf, p_buf, dsc_buf):
        gp = pl.program_id(0)
        n = pl.program_id(1)

        @pl.when(n == 0)
        def _():
            db_ref[...] = jnp.zeros_like(db_ref)
            ds_ref[...] = jnp.zeros_like(ds_ref)

        valid = _band_valid(n)
        kks = _kv_bands(kp_ref, kc_ref)
        vvs = _kv_bands(vp_ref, vc_ref)
        lane_b = lax.broadcasted_iota(jnp.int32, (2 * BLOCK, 128), 1)
        dks, dvs = [], []
        for kv in range(2):
            _stack_heads(q_ref, kv, qs_buf)
            _stack_heads(do_ref, kv, dos_buf)
            s_buf[...] = _dot_nt(qs_buf[...], kks[kv])
            dp_buf[...] = _dot_nt(dos_buf[...], vvs[kv])
            for hq in range(8):
                hl = 8 * kv + hq
                rows = pl.ds(hq * BLOCK, BLOCK)
                p, ps = _attn_probs(s_buf[rows, :], b_ref[hl], sk_ref[gp * 16 + hl], valid)
                dp = dp_buf[rows, :]
                delta = jnp.sum(p * dp, axis=-1, keepdims=True)
                dsc = p * (dp - delta)
                db_ref[hl] += dsc
                ds_ref[hl:hl + 1, :] += jnp.zeros((1, 128), F32) - jnp.sum(ps * delta)
                p_buf[rows, :] = p.astype(BF16)
                dsc_buf[rows, :] = (dsc * SCALE).astype(BF16)
            _unstack_heads(_dot(dsc_buf[...], kks[kv]), dq_ref, kv)
            dk = _dot_tn(dsc_buf[...], qs_buf[...])
            dv = _dot_tn(p_buf[...], dos_buf[...])
            dks.append(dk + pltpu.roll(dk, 64, 1))
            dvs.append(dv + pltpu.roll(dv, 64, 1))
        dkb_ref[...] = jnp.where(lane_b < 64, dks[0], dks[1])
        dvb_ref[...] = jnp.where(lane_b < 64, dvs[0], dvs[1])

    kb, vb = OFF_K // 128, OFF_V // 128
    return pl.pallas_call(
        body, name="attn_bwd", grid=(2, nb),
        in_specs=[pl.BlockSpec(memory_space=pltpu.SMEM),
                  pl.BlockSpec((BLOCK, qw), lambda g, n: (n, OFF_Q // qw + g)),
                  pl.BlockSpec((BLOCK, 128), lambda g, n: (jnp.maximum(n - 1, 0), kb + g)),
                  pl.BlockSpec((BLOCK, 128), lambda g, n: (n, kb + g)),
                  pl.BlockSpec((BLOCK, 128), lambda g, n: (jnp.maximum(n - 1, 0), vb + g)),
                  pl.BlockSpec((BLOCK, 128), lambda g, n: (n, vb + g)),
                  pl.BlockSpec((16, BLOCK, 2 * BLOCK), lambda g, n: (g, 0, 0)),
                  pl.BlockSpec((BLOCK, qw), lambda g, n: (n, g)),
                  pl.BlockSpec(memory_space=pl.ANY)],
        out_specs=[pl.BlockSpec((BLOCK, qw), lambda g, n: (n, OFF_Q // qw + g)),
                   pl.BlockSpec((2 * BLOCK, 128), lambda g, n: (n, g)),
                   pl.BlockSpec((2 * BLOCK, 128), lambda g, n: (n, g)),
                   pl.BlockSpec((16, BLOCK, 2 * BLOCK), lambda g, n: (g, 0, 0)),
                   pl.BlockSpec((16, 128), lambda g, n: (g, 0))],
        out_shape=[SDS((s, IN_W), BF16), SDS((nb * 2 * BLOCK, 256), F32), SDS((nb * 2 * BLOCK, 256), F32),
                   SDS((N_HEADS, BLOCK, 2 * BLOCK), F32), SDS((N_HEADS, 128), F32)],
        input_output_aliases={8: 0},
        scratch_shapes=[pltpu.VMEM((8 * BLOCK, 128), BF16), pltpu.VMEM((8 * BLOCK, 128), BF16),
                        pltpu.VMEM((8 * BLOCK, 2 * BLOCK), F32), pltpu.VMEM((8 * BLOCK, 2 * BLOCK), F32),
                        pltpu.VMEM((8 * BLOCK, 2 * BLOCK), BF16), pltpu.VMEM((8 * BLOCK, 2 * BLOCK), BF16)],
        compiler_params=_cp(2))(sinks, proj, proj, proj, proj, proj, band, datt, dproj)


def dkv_combine(dkb, dvb, dproj):
    nb = dkb.shape[0] // (2 * BLOCK)
    s = nb * BLOCK
    dkb3 = dkb.reshape(nb, 2 * BLOCK, 256)
    dvb3 = dvb.reshape(nb, 2 * BLOCK, 256)

    def body(k1, k2, v1, v2, dp_in, o_ref):
        nxt = jnp.where(pl.program_id(0) < nb - 1, 1.0, 0.0)
        o_ref[:, 0:256] = (k1[...] + nxt * k2[...]).astype(BF16)
        o_ref[:, 256:512] = (v1[...] + nxt * v2[...]).astype(BF16)

    spec1 = pl.BlockSpec((None, BLOCK, 256), lambda m: (m, 1, 0))
    spec2 = pl.BlockSpec((None, BLOCK, 256), lambda m: (jnp.minimum(m + 1, nb - 1), 0, 0))
    return pl.pallas_call(
        body, name="dkv_combine", grid=(nb,),
        in_specs=[spec1, spec2, spec1, spec2, pl.BlockSpec(memory_space=pl.ANY)],
        out_specs=pl.BlockSpec((BLOCK, 512), lambda m: (m, OFF_K // 512)),
        out_shape=SDS((s, IN_W), BF16), input_output_aliases={4: 0},
        compiler_params=_cp(1))(dkb3, dkb3, dvb3, dvb3, dproj)


def lru_bwd(proj, rec, drec, lvec, wa, wx, dproj):
    s = proj.shape[0]
    t = min(256, s)
    nt = s // t

    def body(lx_ref, lxh_ref, rec_ref, rech_ref, dr_ref, lv_ref, wa_ref, wx_ref, dp_in,
             dlx_ref, sums_ref, dwa_ref, dwx_ref,
             xbuf, hbuf, dxbuf, a_s, dh_s, xc_s, r_s, ig_s, mu_s, gc):
        step_i = pl.program_id(0)
        ti = nt - 1 - step_i

        @pl.when(step_i == 0)
        def _():
            sums_ref[...] = jnp.zeros_like(sums_ref)
            dwa_ref[...] = jnp.zeros_like(dwa_ref)
            dwx_ref[...] = jnp.zeros_like(dwx_ref)
            dxbuf[pl.ds(t, 8), :] = jnp.zeros((8, D), F32)
            gc[...] = jnp.zeros((8, D), F32)

        live = jnp.where(ti > 0, 1.0, 0.0)
        xbuf[pl.ds(0, 8), :] = lxh_ref[...].astype(F32)[8:16] * live
        xbuf[pl.ds(8, t), :] = lx_ref[...].astype(F32)
        hbuf[pl.ds(0, 8), :] = rech_ref[...] * live
        hbuf[pl.ds(8, t), :] = rec_ref[...]
        first = (lax.broadcasted_iota(jnp.int32, (t, 128), 0) + ti * t) == 0
        for b in range(N_LRU_BLOCKS):
            cs = slice(b * 128, (b + 1) * 128)
            _, xc, _, r, ig, _, a, mult = _lru_block_fwd(xbuf, lv_ref, wa_ref, wx_ref, b, t, first)
            a_s[:, cs] = a
            xc_s[:, cs] = xc
            r_s[:, cs] = r
            ig_s[:, cs] = ig
            mu_s[:, cs] = mult

        def step(q, g):
            tt = t - 1 - q
            dh = dr_ref[pl.ds(tt, 1), :] + g
            dh_s[pl.ds(tt, 1), :] = dh
            return a_s[pl.ds(tt, 1), :] * dh

        gc[0:1, :] = lax.fori_loop(0, t, step, gc[0:1, :], unroll=8)
        for b in range(N_LRU_BLOCKS):
            cs = slice(b * 128, (b + 1) * 128)
            dh = dh_s[:, cs]
            a = a_s[:, cs]
            xc = xc_s[:, cs]
            r = r_s[:, cs]
            ig = ig_s[:, cs]
            mult = mu_s[:, cs]
            sp = _softplus(-lv_ref[L_LAM:L_LAM + 1, cs])
            lam = lv_ref[L_LAM:L_LAM + 1, cs]
            da = dh * hbuf[pl.ds(7, t), cs]
            dmult = jnp.where(first, 0.0, dh * ig * xc)
            dig = dh * mult * xc
            dxc = dh * mult * ig
            dlog_a = da * a - dmult * (a * a) / mult
            dr = dlog_a * ((-LRU_C) * sp)
            dsp = jnp.sum(dlog_a * ((-LRU_C) * r), axis=0, keepdims=True)
            dza = dr * r * (1.0 - r)
            dzx = dig * ig * (1.0 - ig)
            dzab = dza.astype(BF16)
            dzxb = dzx.astype(BF16)
            xcb = xc.astype(BF16)
            dwa_ref[b] += _dot_tn(xcb, dzab)
            dwx_ref[b] += _dot_tn(xcb, dzxb)
            dxc = dxc + _dot_nt(dzab, wa_ref[b]) + _dot_nt(dzxb, wx_ref[b])
            sums_ref[L_LAM:L_LAM + 1, cs] += dsp * (-jax.nn.sigmoid(-lam))
            sums_ref[L_BA:L_BA + 1, cs] += jnp.sum(dza, axis=0, keepdims=True)
            sums_ref[L_BX:L_BX + 1, cs] += jnp.sum(dzx, axis=0, keepdims=True)
            sums_ref[L_CB:L_CB + 1, cs] += jnp.sum(dxc, axis=0, keepdims=True)
            for kk in range(4):
                sums_ref[kk:kk + 1, cs] += jnp.sum(dxc * xbuf[pl.ds(5 + kk, t), cs], axis=0, keepdims=True)
            dxbuf[pl.ds(0, t), cs] = dxc
            dlx = (lv_ref[3:4, cs] * dxc + lv_ref[2:3, cs] * dxbuf[pl.ds(1, t), cs]
                   + lv_ref[1:2, cs] * dxbuf[pl.ds(2, t), cs] + lv_ref[0:1, cs] * dxbuf[pl.ds(3, t), cs])
            dlx_ref[:, cs] = dlx.astype(BF16)
        dxbuf[pl.ds(t, 8), :] = dxbuf[pl.ds(0, 8), :]

    rev = lambda i: nt - 1 - i
    return pl.pallas_call(
        body, name="lru_bwd", grid=(nt,),
        in_specs=[pl.BlockSpec((t, D), lambda i: (rev(i), 0)),
                  pl.BlockSpec((16, D), lambda i: (jnp.maximum(rev(i) * (t // 16) - 1, 0), 0)),
                  pl.BlockSpec((t, D), lambda i: (rev(i), 0)),
                  pl.BlockSpec((8, D), lambda i: (jnp.maximum(rev(i) * (t // 8) - 1, 0), 0)),
                  pl.BlockSpec((t, D), lambda i: (rev(i), 0)),
                  pl.BlockSpec((8, D), lambda i: (0, 0)),
                  pl.BlockSpec((N_LRU_BLOCKS, 128, 128), lambda i: (0, 0, 0)),
                  pl.BlockSpec((N_LRU_BLOCKS, 128, 128), lambda i: (0, 0, 0)),
                  pl.BlockSpec(memory_space=pl.ANY)],
        out_specs=[pl.BlockSpec((t, D), lambda i: (rev(i), 0)),
                   pl.BlockSpec((8, D), lambda i: (0, 0)),
                   pl.BlockSpec((N_LRU_BLOCKS, 128, 128), lambda i: (0, 0, 0)),
                   pl.BlockSpec((N_LRU_BLOCKS, 128, 128), lambda i: (0, 0, 0))],
        out_shape=[SDS((s, IN_W), BF16), SDS((8, D), F32), SDS((N_LRU_BLOCKS, 128, 128), F32),
                   SDS((N_LRU_BLOCKS, 128, 128), F32)],
        scratch_shapes=[pltpu.VMEM((t + 8, D), F32), pltpu.VMEM((t + 8, D), F32), pltpu.VMEM((t + 8, D), F32)]
        + [pltpu.VMEM((t, D), F32)] * 6 + [pltpu.VMEM((8, D), F32)],
        input_output_aliases={8: 0},
        compiler_params=_cp(1))(proj, proj, rec, rec, drec, lvec, wa, wx, dproj)


def inproj_bwd(dproj, w_in, x, dx1, vecs):
    s = x.shape[0]
    tm, tk = min(512, s), IN_TILE
    nk = IN_W // tk
    per = IN_SHARD // tk

    def body(d_ref, w_ref, x_hbm, dx1_hbm, v_ref, gx_ref, sums_ref, acc, x_ref, dx1_ref, sems):
        i, k = pl.program_id(0), pl.program_id(1)
        fetches = _row_fetches((x_hbm, dx1_hbm), (x_ref, dx1_ref), sems, i, tm)

        @pl.when((i == 0) & (k == 0))
        def _():
            sums_ref[...] = jnp.zeros_like(sums_ref)

        @pl.when(k == 0)
        def _():
            acc[...] = jnp.zeros_like(acc)
            for cp in fetches:
                cp.start()

        acc[...] += _dot_nt(d_ref[...], w_ref[...])

        @pl.when(k == nk - 1)
        def _():
            for cp in fetches:
                cp.wait()
            g1 = v_ref[V_G1:V_G1 + 1, :]
            scale1 = v_ref[V_SCALE1:V_SCALE1 + 1, :]

            def sub(rb, carry):
                rs = pl.ds(pl.multiple_of(rb * SUB, SUB), SUB)
                dh = acc[rs, :]
                r1, xh = _rms_parts(x_ref[rs, :])
                sums_ref[0:1, :] += jnp.sum(dh, axis=0, keepdims=True)
                sums_ref[1:2, :] += jnp.sum(dh * (xh * g1), axis=0, keepdims=True)
                dxn = dh * (1.0 + scale1)
                sums_ref[2:3, :] += jnp.sum(dxn * xh, axis=0, keepdims=True)
                dxh = dxn * g1
                gx_ref[rs, :] = dx1_ref[rs, :] + r1 * (dxh - xh * jnp.mean(dxh * xh, axis=-1, keepdims=True))
                return carry

            lax.fori_loop(0, tm // SUB, sub, 0)

    return pl.pallas_call(
        body, name="inproj_bwd", grid=(s // tm, nk),
        in_specs=[pl.BlockSpec((tm, tk), lambda i, k: (i, k)),
                  pl.BlockSpec((None, D, tk), lambda i, k: (k // per, 0, k % per)),
                  pl.BlockSpec(memory_space=pl.ANY), pl.BlockSpec(memory_space=pl.ANY),
                  pl.BlockSpec((16, D), lambda i, k: (0, 0))],
        out_specs=[pl.BlockSpec((tm, D), lambda i, k: (i, 0)), pl.BlockSpec((8, D), lambda i, k: (0, 0))],
        out_shape=[SDS((s, D), F32), SDS((8, D), F32)],
        scratch_shapes=[pltpu.VMEM((tm, D), F32), pltpu.VMEM((tm, D), F32), pltpu.VMEM((tm, D), F32),
                        pltpu.SemaphoreType.DMA((2,))],
        compiler_params=_cp(2))(dproj, w_in, x, dx1, vecs)


def mod_columns(c16, w_ada, b_cols):
    tn = 512

    def body(c_ref, w_ref, b_ref, o_ref):
        cv = c_ref[...]
        ca = (cv * jax.nn.sigmoid(cv)).astype(BF16)
        o_ref[...] = _dot(ca, w_ref[...].astype(BF16)) + b_ref[...]

    return pl.pallas_call(
        body, name="mod_columns", grid=(ADA_SHARD // tn,),
        in_specs=[pl.BlockSpec((16, D), lambda j: (0, 0)), pl.BlockSpec((D, tn), lambda j: (0, j)),
                  pl.BlockSpec((1, tn), lambda j: (0, j))],
        out_specs=pl.BlockSpec((16, tn), lambda j: (0, j)),
        out_shape=SDS((16, ADA_SHARD), F32), compiler_params=_cp(1))(c16, w_ada, b_cols)


def wada_update(c16, dmod16, w, m, v):
    tm, tn = 512, 512

    def body(c_ref, d_ref, w_ref, m_ref, v_ref, g_out, dl_out, m_out, v_out):
        cv = c_ref[...]
        ca = (cv * jax.nn.sigmoid(cv)).astype(BF16)
        g = _dot_tn(ca, d_ref[...].astype(BF16))
        dl, m2, v2 = _adamw_math(w_ref[...], g, m_ref[...], v_ref[...])
        g_out[...] = g
        dl_out[...] = dl
        m_out[...] = m2
        v_out[...] = v2

    tile = pl.BlockSpec((tm, tn), lambda i, j: (i, j))
    return pl.pallas_call(
        body, name="wada_update", grid=(D // tm, ADA_SHARD // tn),
        in_specs=[pl.BlockSpec((16, tm), lambda i, j: (0, i)), pl.BlockSpec((16, tn), lambda i, j: (0, j)),
                  tile, tile, tile],
        out_specs=[tile] * 4, out_shape=[SDS((D, ADA_SHARD), F32)] * 4,
        compiler_params=_cp(2))(c16, dmod16, w, m, v)


def adamw_big(name, w, mine, theirs, m, v, c_idx):
    r, c = w.shape
    tr = 128
    per = (r // 2) // tr

    def body(c_ref, w_ref, a_ref, b_ref, m_ref, v_ref, g_out, dl_out, m_out, v_out):
        own = (pl.program_id(0) // per) == c_ref[0]
        g = jnp.where(own, a_ref[...], b_ref[...])
        dl, m2, v2 = _adamw_math(w_ref[...], g, m_ref[...], v_ref[...])
        g_out[...] = g
        dl_out[...] = dl
        m_out[...] = m2
        v_out[...] = v2

    tile = pl.BlockSpec((tr, c), lambda i, cr: (i, 0))
    half = pl.BlockSpec((tr, c), lambda i, cr: (i % per, 0))
    gs = pltpu.PrefetchScalarGridSpec(num_scalar_prefetch=1, grid=(r // tr,),
                                      in_specs=[tile, half, half, tile, tile], out_specs=[tile] * 4)
    return pl.pallas_call(body, name=name, grid_spec=gs, out_shape=[SDS((r, c), F32)] * 4,
                          compiler_params=_cp(1))(c_idx, w, mine, theirs, m, v)


def cast_into_slot(name, w, k_idx):
    r, c = w.shape
    tr = 256

    def body(k_ref, w_ref, o_ref):
        o_ref[...] = w_ref[...].astype(BF16)

    gs = pltpu.PrefetchScalarGridSpec(
        num_scalar_prefetch=1, grid=(r // tr,),
        in_specs=[pl.BlockSpec((tr, c), lambda i, kr: (i, 0))],
        out_specs=pl.BlockSpec((None, tr, c), lambda i, kr: (kr[0], i, 0)))
    return pl.pallas_call(body, name=name, grid_spec=gs, out_shape=SDS((N_CHIPS, r, c), BF16),
                          compiler_params=_cp(1))(k_idx, w)


def adamw_small(ws, gs, ms, vs):
    n = len(ws)

    def body(*refs):
        for i in range(n):
            dl, m2, v2 = _adamw_math(refs[i][...], refs[n + i][...], refs[2 * n + i][...], refs[3 * n + i][...])
            refs[4 * n + i][...] = dl
            refs[5 * n + i][...] = m2
            refs[6 * n + i][...] = v2

    vm = pl.BlockSpec(memory_space=pltpu.VMEM)
    shapes = [SDS(w.shape, F32) for w in ws]
    outs = pl.pallas_call(
        body, name="adamw_small", in_specs=[vm] * (4 * n), out_specs=[vm] * (3 * n), out_shape=shapes * 3,
        compiler_params=pltpu.CompilerParams(vmem_limit_bytes=VMEM_LIMIT))(*ws, *gs, *ms, *vs)
    return outs[:n], outs[n:2 * n], outs[2 * n:]


def sum_devices(name, gathered):
    rows = gathered.shape[1]
    tr = min(rows, 128)

    def body(x_ref, o_ref):
        acc = x_ref[0].astype(F32)
        for d in range(1, N_DEV):
            acc = acc + x_ref[d].astype(F32)
        o_ref[...] = acc

    return pl.pallas_call(
        body, name=name, grid=(rows // tr,),
        in_specs=[pl.BlockSpec((N_DEV, tr, D), lambda i: (0, i, 0))],
        out_specs=pl.BlockSpec((tr, D), lambda i: (i, 0)),
        out_shape=SDS((rows, D), F32), compiler_params=_cp(1))(gathered)


def _mesh_pos():
    return lax.axis_index("x"), lax.axis_index("y"), lax.axis_index("c")


def _other_chips(x, y):
    return [(1 - x, y), (x, 1 - y), (1 - x, 1 - y)]


def all_gather_small(name, block):
    m_per, n = block.shape

    def body(x_ref, out_ref, send_sems, recv_sems, local_sem):
        x, y, c = _mesh_pos()
        me, sibling = (x, y, c), (x, y, 1 - c)
        chips = _other_chips(x, y)

        def rows(px, py, pc):
            return out_ref.at[pl.ds((4 * px + 2 * py + pc) * m_per, m_per), :]

        def copy(k, blk, to, src=None):
            return pltpu.make_async_remote_copy(
                src_ref=rows(*blk) if src is None else src, dst_ref=rows(*blk),
                send_sem=send_sems.at[k], recv_sem=recv_sems.at[k], device_id=to, device_id_type=MESH)

        mine = pltpu.make_async_copy(x_ref, rows(*me), local_sem)
        mine.start()
        first = [copy(0, me, sibling, src=x_ref)]
        first += [copy(1 + j, me, (*chip, c), src=x_ref) for j, chip in enumerate(chips)]
        for cp in first:
            cp.start()
        passed = [copy(4 + j, (*chip, c), sibling) for j, chip in enumerate(chips)]
        for j, chip in enumerate(chips):
            copy(1 + j, (*chip, c), me).wait_recv()
            passed[j].start()
        copy(0, sibling, me).wait_recv()
        for j, chip in enumerate(chips):
            copy(4 + j, (*chip, 1 - c), me).wait_recv()
        for cp in first + passed:
            cp.wait_send()
        mine.wait()

    vm = pl.BlockSpec(memory_space=pltpu.VMEM)
    return pl.pallas_call(
        body, name=name, out_shape=SDS((N_DEV * m_per, n), block.dtype), in_specs=[vm], out_specs=vm,
        scratch_shapes=[pltpu.SemaphoreType.DMA((7,)), pltpu.SemaphoreType.DMA((7,)), pltpu.SemaphoreType.DMA],
        compiler_params=pltpu.CompilerParams(vmem_limit_bytes=VMEM_LIMIT))(block)


def sibling_sum(name, grad, other, c_idx):
    _, r, cc = grad.shape
    h = r // 2
    tr = min(256, h)
    g4 = grad.reshape(N_CHIPS, 2, h, cc)

    def body(c_ref, a_ref, b_ref, o_ref):
        o_ref[...] = (a_ref[...].astype(F32) + b_ref[...].astype(F32)).astype(BF16)

    gs = pltpu.PrefetchScalarGridSpec(
        num_scalar_prefetch=1, grid=(N_CHIPS, h // tr),
        in_specs=[pl.BlockSpec((None, None, tr, cc), lambda s, i, cr: (s, cr[0], i, 0)),
                  pl.BlockSpec((None, tr, cc), lambda s, i, cr: (s, i, 0))],
        out_specs=pl.BlockSpec((None, tr, cc), lambda s, i, cr: (s, i, 0)))
    return pl.pallas_call(body, name=name, grid_spec=gs, out_shape=SDS((N_CHIPS, h, cc), BF16),
                          compiler_params=_cp(2))(c_idx, g4, other)


HBM_SPEC = pl.BlockSpec(memory_space=pltpu.HBM)
SEM_SPEC = pl.BlockSpec(memory_space=pltpu.SEMAPHORE)


def _side_effecting():
    return pltpu.CompilerParams(has_side_effects=pltpu.SideEffectType.DATAFLOW_SIDE_EFFECTING)


def _in_hbm(a):
    return pltpu.with_memory_space_constraint(a, pltpu.HBM)


ALL_CHIPS = (0, 1, 2)


def gather_start(name, bufs, after, rel=ALL_CHIPS, carry=None):
    n = len(bufs)
    nr = len(rel)
    halves = [w.shape[1] // 2 for w in bufs]
    extra = [] if carry is None else [carry]

    def body(*refs):
        ins = refs[:n]
        send_sems, recv_sems, token = refs[n + 1 + len(extra)], refs[n + 2 + len(extra)], refs[-1]
        x, y, c = _mesh_pos()
        k = 2 * x + y
        for i in range(n):
            reg = ins[i].at[k, pl.ds(c * halves[i], halves[i]), :]
            for q, j in enumerate(rel):
                chip = _other_chips(x, y)[j]
                pltpu.make_async_remote_copy(src_ref=reg, dst_ref=reg, send_sem=send_sems.at[nr * i + q],
                                             recv_sem=recv_sems.at[nr * i + q], device_id=(*chip, c),
                                             device_id_type=MESH).start()
        token[...] = jnp.zeros_like(token)

    outs = pl.pallas_call(
        body, name=name,
        out_shape=(pltpu.SemaphoreType.DMA((nr * n,)), pltpu.SemaphoreType.DMA((nr * n,)),
                   *[pltpu.HBM(w.shape, w.dtype) for w in list(bufs) + extra], SDS((8, 128), F32)),
        in_specs=[HBM_SPEC] * n + [pl.BlockSpec(memory_space=pl.ANY)] + [HBM_SPEC] * len(extra),
        out_specs=(SEM_SPEC, SEM_SPEC, *[HBM_SPEC] * (n + len(extra)), pl.BlockSpec(memory_space=pltpu.VMEM)),
        input_output_aliases={**{i: 2 + i for i in range(n)}, **({n + 1: 2 + n} if extra else {})},
        compiler_params=_side_effecting())(*[_in_hbm(w) for w in bufs], after, *[_in_hbm(w) for w in extra])
    return (outs[0], outs[1], list(outs[2:2 + n]), outs[-1]) + ((outs[2 + n],) if extra else ())


def gather_wait(name, send_sems, recv_sems, bufs, after, rel=ALL_CHIPS):
    n = len(bufs)
    nr = len(rel)
    halves = [w.shape[1] // 2 for w in bufs]

    def body(*refs):
        ins = refs[:n]
        send_sems, recv_sems = refs[n], refs[n + 1]
        x, y, c = _mesh_pos()
        k = 2 * x + y
        for i in range(n):
            for q, j in enumerate(rel):
                chip = _other_chips(x, y)[j]
                kj = 2 * chip[0] + chip[1]
                cp = pltpu.make_async_remote_copy(
                    src_ref=ins[i].at[k, pl.ds(c * halves[i], halves[i]), :],
                    dst_ref=ins[i].at[kj, pl.ds(c * halves[i], halves[i]), :],
                    send_sem=send_sems.at[nr * i + q], recv_sem=recv_sems.at[nr * i + q], device_id=(*chip, c),
                    device_id_type=MESH)
                cp.wait_send()
                cp.wait_recv()

    return pl.pallas_call(
        body, name=name, out_shape=[pltpu.HBM(w.shape, w.dtype) for w in bufs],
        in_specs=[HBM_SPEC] * n + [SEM_SPEC, SEM_SPEC, pl.BlockSpec(memory_space=pl.ANY)],
        out_specs=[HBM_SPEC] * n, input_output_aliases={i: i for i in range(n)},
        compiler_params=_side_effecting())(*bufs, send_sems, recv_sems, after)


def gather_forward(name, bufs, rel=ALL_CHIPS):
    n = len(bufs)
    halves = [w.shape[1] // 2 for w in bufs]

    def body(*refs):
        outs = refs[n:2 * n]
        send_sems, recv_sems = refs[2 * n:]
        x, y, c = _mesh_pos()
        chips = _other_chips(x, y)

        def copy(i, j, half, to):
            kj = 2 * chips[j][0] + chips[j][1]
            reg = outs[i].at[kj, pl.ds(half * halves[i], halves[i]), :]
            return pltpu.make_async_remote_copy(src_ref=reg, dst_ref=reg, send_sem=send_sems.at[i, j],
                                                recv_sem=recv_sems.at[i, j], device_id=to, device_id_type=MESH)

        cps = [copy(i, j, c, (x, y, 1 - c)) for i in range(n) for j in rel]
        for cp in cps:
            cp.start()
        for i in range(n):
            for j in rel:
                copy(i, j, 1 - c, (x, y, c)).wait_recv()
        for cp in cps:
            cp.wait_send()

    hbm = pl.BlockSpec(memory_space=pl.ANY)
    return pl.pallas_call(
        body, name=name, in_specs=[hbm] * n, out_specs=[hbm] * n,
        out_shape=[SDS(w.shape, w.dtype) for w in bufs], input_output_aliases={i: i for i in range(n)},
        scratch_shapes=[pltpu.SemaphoreType.DMA((n, 3)), pltpu.SemaphoreType.DMA((n, 3))])(*bufs)


def _exchange_plan(kind, srcs, zones):
    x, y, c = _mesh_pos()
    plan = []
    for src, zone in zip(srcs, zones):
        if kind == "chips":
            for j, chip in enumerate(_other_chips(x, y)):
                plan.append((src.at[2 * chip[0] + chip[1]], zone.at[j], (*chip, c)))
        elif kind == "sibling":
            h = zone.shape[1]
            plan.append((src.at[:, pl.ds((1 - c) * h, h), :], zone, (x, y, 1 - c)))
        else:
            peers = [(x, y, 1 - c)] + [(*chip, cc) for chip in _other_chips(x, y) for cc in (c, 1 - c)]
            plan += [(src, zone.at[4 * x + 2 * y + c], peer) for peer in peers]
    return plan


_COPIES_PER_ARRAY = {"chips": 3, "sibling": 1, "all": N_DEV - 1}


def _landing_zones(kind, srcs):
    if kind == "chips":
        return [lax.empty((3,) + t.shape[1:], t.dtype) for t in srcs]
    if kind == "sibling":
        return [lax.empty((t.shape[0], t.shape[1] // 2, t.shape[2]), t.dtype) for t in srcs]
    return [lax.empty((N_DEV,) + t.shape, t.dtype) for t in srcs]


def exchange_start(name, kind, srcs, after):
    n = len(srcs)
    lands = _landing_zones(kind, srcs)
    n_copies = n * _COPIES_PER_ARRAY[kind]

    def body(*refs):
        send_sems, recv_sems, token = refs[2 * n + 1], refs[2 * n + 2], refs[-1]
        for q, (src, dst, dev) in enumerate(_exchange_plan(kind, refs[:n], refs[n:2 * n])):
            pltpu.make_async_remote_copy(src_ref=src, dst_ref=dst, send_sem=send_sems.at[q], recv_sem=recv_sems.at[q],
                                         device_id=dev, device_id_type=MESH).start()
        token[...] = jnp.zeros_like(token)

    outs = pl.pallas_call(
        body, name=name,
        out_shape=(pltpu.SemaphoreType.DMA((n_copies,)), pltpu.SemaphoreType.DMA((n_copies,)),
                   *[pltpu.HBM(t.shape, t.dtype) for t in srcs], *[pltpu.HBM(t.shape, t.dtype) for t in lands],
                   SDS((8, 128), F32)),
        in_specs=[HBM_SPEC] * (2 * n) + [pl.BlockSpec(memory_space=pl.ANY)],
        out_specs=(SEM_SPEC, SEM_SPEC, *[HBM_SPEC] * (2 * n), pl.BlockSpec(memory_space=pltpu.VMEM)),
        input_output_aliases={i: 2 + i for i in range(2 * n)},
        compiler_params=_side_effecting())(*[_in_hbm(t) for t in srcs], *[_in_hbm(t) for t in lands], after)
    return outs[0], outs[1], list(outs[2:2 + n]), list(outs[2 + n:2 + 2 * n]), outs[-1]


def exchange_wait(name, kind, send_sems, recv_sems, srcs, lands, after):
    n = len(srcs)

    def body(*refs):
        send_sems, recv_sems = refs[2 * n], refs[2 * n + 1]
        for q, (src, dst, dev) in enumerate(_exchange_plan(kind, refs[:n], refs[n:2 * n])):
            cp = pltpu.make_async_remote_copy(src_ref=src, dst_ref=dst, send_sem=send_sems.at[q],
                                              recv_sem=recv_sems.at[q], device_id=dev, device_id_type=MESH)
            cp.wait_send()
            cp.wait_recv()

    outs = pl.pallas_call(
        body, name=name, out_shape=[pltpu.HBM(t.shape, t.dtype) for t in srcs + lands],
        in_specs=[HBM_SPEC] * (2 * n) + [SEM_SPEC, SEM_SPEC, pl.BlockSpec(memory_space=pl.ANY)],
        out_specs=[HBM_SPEC] * (2 * n), input_output_aliases={i: i for i in range(2 * n)},
        compiler_params=_side_effecting())(*srcs, *lands, send_sems, recv_sems, after)
    return list(outs[:n]), list(outs[n:])


def chip_sum(name, sums, parts, k_idx):
    _, h, cc = parts.shape
    tr = min(256, h)

    def body(k_ref, own_ref, p_ref, o_ref):
        acc = own_ref[...].astype(F32)
        for s in range(3):
            acc = acc + p_ref[s].astype(F32)
        o_ref[...] = acc

    gs = pltpu.PrefetchScalarGridSpec(
        num_scalar_prefetch=1, grid=(h // tr,),
        in_specs=[pl.BlockSpec((None, tr, cc), lambda i, kr: (kr[0], i, 0)),
                  pl.BlockSpec((3, tr, cc), lambda i, kr: (0, i, 0))],
        out_specs=pl.BlockSpec((tr, cc), lambda i, kr: (i, 0)))
    return pl.pallas_call(body, name=name, grid_spec=gs, out_shape=SDS((h, cc), F32),
                          compiler_params=_cp(1))(k_idx, sums, parts)


def halves_exchange(name, halves):
    n = len(halves)

    def body(*refs):
        ins, outs = refs[:n], refs[n:2 * n]
        send_sems, recv_sems = refs[2 * n:]
        x, y, c = _mesh_pos()
        cps = []
        for i in range(n):
            cp = pltpu.make_async_remote_copy(
                src_ref=ins[i], dst_ref=outs[i], send_sem=send_sems.at[i], recv_sem=recv_sems.at[i],
                device_id=(x, y, 1 - c), device_id_type=MESH)
            cp.start()
            cps.append(cp)
        for cp in cps:
            cp.wait_recv()
        for cp in cps:
            cp.wait_send()

    hbm = pl.BlockSpec(memory_space=pl.ANY)
    return pl.pallas_call(
        body, name=name, in_specs=[hbm] * n, out_specs=[hbm] * n,
        out_shape=[SDS(t.shape, F32) for t in halves],
        scratch_shapes=[pltpu.SemaphoreType.DMA((n,)), pltpu.SemaphoreType.DMA((n,))])(*halves)


def local_step(x, tgt, vecs, lvec, wa, wx, sinks, rel_bias, proj, h, w_in, rest_weights, hook):
    buckets = t5_bucket_table()
    band = bias_band(rel_bias.T, buckets).reshape(N_HEADS, BLOCK, 2 * BLOCK)

    ya, rec = lru_fwd(proj, lvec, wa, wx)
    att = attn_fwd(proj, band, sinks)
    w_lru_out, w_attn_out, w_out, w_ff1, w_ff2 = rest_weights(att[:8, :128] + ya[:8, :128])
    w_lru_out2, w_attn_out2, w_out2 = w_lru_out.reshape(D, D), w_attn_out.reshape(D, D), w_out.reshape(D, D)
    w_ff2_2 = w_ff2.reshape(D_FF, D)
    yab, merged = merge_fwd(ya, att, w_lru_out2, w_attn_out2, proj)
    x1, o1 = outproj_fwd(merged, w_out2, x, vecs)
    f, h2 = ff1_fwd(x1, vecs, w_ff1)
    dx2, do2, sums_f, loss = ff2_loss(f, w_ff2_2, x1, tgt, vecs)

    df = ff2_bwd(do2, w_ff2_2, f)
    g_ff2 = weight_grad("dw_ff2", f, do2, 512, (D_FF, D), (WG_TM, 512), lambda i, j: (i, j), relu2=True)
    dx1, do1, sums_2 = ff1_bwd(df, w_ff1, x1, dx2, o1, vecs)
    g_ff1 = weight_grad("dw_ff1", h2, df, 512, (N_CHIPS, D, D), (None, WG_TM, 512), lambda i, j: (j // 4, i, j % 4))
    dyab, dproj = outproj_bwd(do1, w_out2, yab, proj)
    g_out = weight_grad("dw_out", merged, do1, 512, (D, D), (WG_TM, 512), lambda i, j: (i, j))
    drec, dproj = lruout_bwd(dyab, w_lru_out2, rec, proj, dproj)
    dyab2 = dyab.reshape(2 * x.shape[0], D)
    g_lru_out = weight_grad("dw_lru_out", ya, dyab2, 512, (D, D), (WG_TM, 512), lambda i, j: (i, j))
    datt = attnout_bwd(dyab, w_attn_out2)
    g_attn_out = weight_grad("dw_attn_out", att, dyab2, 512, (D, D), (WG_TM, 512), lambda i, j: (i, j), b_part=1)
    zero = hook("grads_a", [g_lru_out.reshape(N_CHIPS, D // 4, D), g_attn_out.reshape(N_CHIPS, D // 4, D),
                            g_out.reshape(N_CHIPS, D // 4, D), g_ff1, g_ff2.reshape(N_CHIPS, D_FF // 4, D)])
    dproj, dkb, dvb, dband, dsink = attn_bwd(proj, band, sinks + zero, datt, dproj)
    zero = hook("after_attn_bwd", dkb)
    dproj = dkv_combine(dkb, dvb, dproj)
    dproj, sums_l, d_wa, d_wx = lru_bwd(proj, rec, drec, lvec + zero, wa, wx, dproj)
    hook("lru_grads", (d_wa, d_wx))
    per = IN_SHARD // IN_TILE
    g_in = weight_grad("dw_in", h, dproj, IN_TILE, (N_CHIPS, D, IN_SHARD), (None, WG_TM, IN_TILE),
                       lambda i, j: (j // per, i, j % per))
    zero = hook("grads_b", [g_in])
    grad_x, sums_1 = inproj_bwd(dproj, w_in, x, dx1, vecs + zero)
    d_rel_bias = bias_band_bwd(dband.reshape(N_HEADS, BLOCK * 2 * BLOCK), buckets)

    small = dict(sums_f=sums_f, sums_2=sums_2, sums_1=sums_1, sums_l=sums_l, d_wa=d_wa, d_wx=d_wx,
                 d_sinks=dsink[:, 0], d_rel_bias=d_rel_bias)
    return loss, grad_x, small


def _pad_rows(a, rows):
    return jnp.concatenate([a, jnp.zeros((rows - a.shape[0], a.shape[1]), a.dtype)], axis=0)


def kernel(x, c, w_ada, b_ada, norm1_g, w_in, conv_w, conv_b, lru_wa, lru_ba, lru_wx, lru_bx, lru_lambda, w_lru_out, w_attn_out, attn_sinks, rel_bias, w_out, norm2_g, w_ff1, w_ff2, final_g, loss_target, m_w_ada, m_b_ada, m_norm1_g, m_w_in, m_conv_w, m_conv_b, m_lru_wa, m_lru_ba, m_lru_wx, m_lru_bx, m_lru_lambda, m_w_lru_out, m_w_attn_out, m_attn_sinks, m_rel_bias, m_w_out, m_norm2_g, m_w_ff1, m_w_ff2, m_final_g, v_w_ada, v_b_ada, v_norm1_g, v_w_in, v_conv_w, v_conv_b, v_lru_wa, v_lru_ba, v_lru_wx, v_lru_bx, v_lru_lambda, v_w_lru_out, v_w_attn_out, v_attn_sinks, v_rel_bias, v_w_out, v_norm2_g, v_w_ff1, v_w_ff2, v_final_g):
    xi, yi, ci = _mesh_pos()
    chip = 2 * xi + yi
    dev = 2 * chip + ci
    z8 = jnp.zeros((8, D), F32)

    conv_rows = jnp.concatenate([conv_w[0], jnp.zeros((4, D - D // 4), F32)], axis=1)
    pack0 = jnp.concatenate([c, conv_rows, jnp.zeros((3, D), F32)], axis=0)
    g0 = all_gather_small("gather_cond", pack0).reshape(N_DEV, 8, D)
    c_all = g0[:, 0, :]
    conv_full = jnp.concatenate([g0[2 * k, 1:5, :D // 4] for k in range(N_CHIPS)], axis=1)
    c16 = jnp.concatenate([c_all, z8], axis=0)
    b_cols = lax.dynamic_slice_in_dim(b_ada, chip * ADA_SHARD, ADA_SHARD, axis=1)
    mod_c = mod_columns(c16, w_ada[0], b_cols)
    g1 = all_gather_small("gather_mod", mod_c).reshape(N_DEV, 16, ADA_SHARD)
    mod = jnp.concatenate([lax.dynamic_index_in_dim(g1[2 * k], dev, axis=0, keepdims=False) for k in range(N_CHIPS)])
    shift1, scale1, gate1, shift2, scale2, gate2 = [mod[i * D:(i + 1) * D] for i in range(6)]
    vecs = jnp.stack([norm1_g[0], scale1, shift1, gate1, norm2_g[0], scale2, shift2, gate2, final_g]
                     + [jnp.zeros((D,), F32)] * 7)
    lvec = jnp.concatenate([conv_full, conv_b, lru_ba, lru_bx, lru_lambda], axis=0)

    shards = [w_in[0], w_lru_out[0], w_attn_out[0], w_out[0], w_ff1[0], w_ff2[0]]
    names = ["w_in", "w_lru_out", "w_attn_out", "w_out", "w_ff1", "w_ff2"]
    k_idx = jnp.reshape(chip, (1,)).astype(jnp.int32)
    c_idx = jnp.reshape(ci, (1,)).astype(jnp.int32)
    near, far = (0, 1), (2,)
    shard_of = lambda flip: jnp.reshape(chip ^ flip, (1,)).astype(jnp.int32)
    x2d = x[0]
    n_send, n_recv, w_in_buf, _ = gather_start(
        "gather_start_in_near", [cast_into_slot("cast_w_in", shards[0], k_idx)], vecs, near)
    proj, h = inproj_fwd("inproj_fwd_own", x2d, None, vecs, w_in_buf[0], None, k_idx)
    slots = [cast_into_slot("cast_" + nm, w, k_idx) for nm, w in zip(names[1:], shards[1:])]
    w_in_buf = gather_forward("gather_forward_in_near", gather_wait(
        "gather_wait_in_near", n_send, n_recv, w_in_buf, proj[:8, :128] + slots[-1][0, :8, :128], near), near)
    f_send, f_recv, w_in_buf, _ = gather_start("gather_start_in_far", w_in_buf, proj[:8, :128], far)
    g_send, g_recv, in_flight, _, carried = gather_start(
        "gather_start_rest", slots, proj[:8, :128], carry=w_in_buf[0])
    w_in_buf = [carried]
    proj = inproj_fwd("inproj_fwd_x", x2d, h, vecs, w_in_buf[0], proj, shard_of(2))
    proj = inproj_fwd("inproj_fwd_y", x2d, h, vecs, w_in_buf[0], proj, shard_of(1))
    w_in_buf = gather_forward("gather_forward_in_far", gather_wait(
        "gather_wait_in_far", f_send, f_recv, w_in_buf, proj[:8, :128], far), far)
    proj = inproj_fwd("inproj_fwd_d", x2d, h, vecs, w_in_buf[0], proj, shard_of(3))
    w_in_full = w_in_buf[0]
    pending = {}

    def rest_weights(after):
        return gather_forward("gather_forward_rest", gather_wait("gather_wait_rest", g_send, g_recv, in_flight, after))

    def reduce_hook(event, payload):
        if event == "grads_a":
            pending["sib_a"] = exchange_start("sibling_start_a", "sibling", payload, payload[0])
            return pending["sib_a"][-1][0, 0]
        if event == "lru_grads":
            pack_w = jnp.concatenate([payload[0].reshape(128, D), payload[1].reshape(128, D)], axis=0).astype(BF16)
            pending["lru_w"] = exchange_start("lru_w_grads_start", "all", [pack_w], pack_w)
            return pending["lru_w"][-1][0, 0]
        if event == "grads_b":
            pending["sib_b"] = exchange_start("sibling_start_b", "sibling", payload, pending["lru_w"][-1])
            return pending["sib_b"][-1][0, 0]
        return chips_start("a", names[1:], payload)

    def chips_start(tag, nms, after):
        send_sems, recv_sems, grads, lands, _ = pending["sib_" + tag]
        grads, lands = exchange_wait("sibling_wait_" + tag, "sibling", send_sems, recv_sems, grads, lands, after)
        sums = [sibling_sum("sibling_sum_" + nm, g, o, c_idx) for nm, g, o in zip(nms, grads, lands)]
        pending[tag] = exchange_start("exchange_start_" + tag, "chips", sums, sums[0])
        return pending[tag][-1][0, 0]

    loss_t, grad_x, small = local_step(
        x2d, loss_target[0], vecs, lvec, lru_wa[0].astype(BF16), lru_wx[0].astype(BF16),
        attn_sinks[0], rel_bias, proj, h, w_in_full, rest_weights, reduce_hook)
    chips_start("b", names[:1], grad_x)

    big_m = dict(zip(names, [m_w_in, m_w_lru_out, m_w_attn_out, m_w_out, m_w_ff1, m_w_ff2]))
    big_v = dict(zip(names, [v_w_in, v_w_lru_out, v_w_attn_out, v_w_out, v_w_ff1, v_w_ff2]))
    local_w = dict(zip(names, shards))
    g_big, d_big, nm_big, nv_big = {}, {}, {}, {}

    def links_done(tag, after):
        send_sems, recv_sems, sums, lands, _ = pending[tag]
        return exchange_wait("exchange_wait_" + tag, "chips", send_sems, recv_sems, sums, lands, after)

    def finish_reduce(nms, sums, lands):
        mine = [chip_sum("chip_sum_" + nm, t, p, k_idx) for nm, t, p in zip(nms, sums, lands)]
        theirs = halves_exchange("halves_exchange_" + nms[0], mine)
        for nm, a, b in zip(nms, mine, theirs):
            g2, dl, m2, v2 = adamw_big("adamw_" + nm, local_w[nm], a, b, big_m[nm][0], big_v[nm][0], c_idx)
            g_big[nm], d_big[nm], nm_big[nm], nv_big[nm] = g2[None], dl[None], m2[None], v2[None]
        return lax.optimization_barrier(tuple(nv_big[nm] for nm in nms))[0]

    done_a = finish_reduce(names[1:], *links_done("a", pending["b"][-1]))
    sums_b, lands_b = links_done("b", done_a)
    w_send, w_recv, w_src, w_lands, _ = pending["lru_w"]
    w_src, w_lands = exchange_wait("lru_w_grads_wait", "all", w_send, w_recv, w_src, w_lands, lands_b[0])

    sums_f, sums_2, sums_1, sums_l = small["sums_f"], small["sums_2"], small["sums_1"], small["sums_l"]
    vec_rows = jnp.stack([sums_1[2], sums_2[2], sums_f[0], sums_l[L_CB], sums_l[L_BA], sums_l[L_BX],
                          sums_l[L_LAM], jnp.zeros((D,), F32)])
    mod_rows = jnp.stack([sums_1[0], sums_1[1], sums_2[3], sums_2[0], sums_2[1], sums_f[1],
                          jnp.zeros((D,), F32), jnp.zeros((D,), F32)])
    att_rows = jnp.concatenate([
        jnp.concatenate([small["d_sinks"], jnp.zeros((D - N_HEADS,), F32)])[None],
        jnp.concatenate([small["d_rel_bias"].reshape(-1), jnp.zeros((D - N_BUCKETS * N_HEADS,), F32)])[None],
        jnp.zeros((6, D), F32)], axis=0)
    pack = jnp.concatenate([vec_rows, _pad_rows(sums_l[0:4], 8), mod_rows, att_rows], axis=0)
    pack, lru_w_all = lax.optimization_barrier((pack, w_lands[0]))
    gathered = all_gather_small("gather_small_grads", pack).reshape(N_DEV, P_WA, D)
    total = sum_devices("sum_small_grads", gathered)
    total_w = sum_devices("sum_lru_w_grads", lax.dynamic_update_slice(lru_w_all, w_src[0][None], (dev, 0, 0)))
    dmod_all = gathered[:, P_MOD:P_MOD + 6, :].reshape(N_DEV, 6 * D)
    dmod16 = jnp.concatenate([lax.dynamic_slice_in_dim(dmod_all, chip * ADA_SHARD, ADA_SHARD, axis=1),
                              jnp.zeros((8, ADA_SHARD), F32)], axis=0)
    g_w_ada, d_w_ada, nm_w_ada, nv_w_ada = wada_update(c16, dmod16, w_ada[0], m_w_ada[0], v_w_ada[0])
    finish_reduce(names[:1], sums_b, lands_b)
    loss = lax.psum(lax.optimization_barrier((loss_t, total))[0][0, 0], ("x", "y", "c"))

    conv_g = lax.dynamic_slice_in_dim(total[P_CONVW:P_CONVW + 4], chip * (D // 4), D // 4, axis=1)
    sm_names = ["b_ada", "norm1_g", "conv_w", "conv_b", "lru_wa", "lru_ba", "lru_wx", "lru_bx", "lru_lambda",
                "attn_sinks", "rel_bias", "norm2_g", "final_g"]
    sm_w = [b_ada.reshape(6, D), norm1_g, conv_w[0], conv_b, lru_wa.reshape(128, D), lru_ba, lru_wx.reshape(128, D),
            lru_bx, lru_lambda, attn_sinks, rel_bias, norm2_g, final_g[None]]
    sm_m = [m_b_ada.reshape(6, D), m_norm1_g, m_conv_w[0], m_conv_b, m_lru_wa.reshape(128, D), m_lru_ba,
            m_lru_wx.reshape(128, D), m_lru_bx, m_lru_lambda, m_attn_sinks, m_rel_bias, m_norm2_g, m_final_g[None]]
    sm_v = [v_b_ada.reshape(6, D), v_norm1_g, v_conv_w[0], v_conv_b, v_lru_wa.reshape(128, D), v_lru_ba,
            v_lru_wx.reshape(128, D), v_lru_bx, v_lru_lambda, v_attn_sinks, v_rel_bias, v_norm2_g, v_final_g[None]]
    sm_g = [total[P_MOD:P_MOD + 6], total[0:1], conv_g, total[3:4], total_w[0:128], total[4:5],
            total_w[128:256], total[5:6], total[6:7], total[P_ATT:P_ATT + 1, :N_HEADS],
            total[P_ATT + 1, :N_BUCKETS * N_HEADS].reshape(N_BUCKETS, N_HEADS), total[1:2], total[2:3]]
    sm_d, sm_nm, sm_nv = adamw_small(sm_w, sm_g, sm_m, sm_v)
    shapes = dict(b_ada=b_ada.shape, norm1_g=norm1_g.shape, conv_w=conv_w.shape, conv_b=conv_b.shape,
                  lru_wa=lru_wa.shape, lru_ba=lru_ba.shape, lru_wx=lru_wx.shape, lru_bx=lru_bx.shape,
                  lru_lambda=lru_lambda.shape, attn_sinks=attn_sinks.shape, rel_bias=rel_bias.shape,
                  norm2_g=norm2_g.shape, final_g=final_g.shape)
    grads = dict(w_ada=g_w_ada[None], **g_big)
    deltas = dict(w_ada=d_w_ada[None], **d_big)
    new_m = dict(w_ada=nm_w_ada[None], **nm_big)
    new_v = dict(w_ada=nv_w_ada[None], **nv_big)
    for i, nm in enumerate(sm_names):
        grads[nm] = sm_g[i].reshape(shapes[nm])
        deltas[nm] = sm_d[i].reshape(shapes[nm])
        new_m[nm] = sm_nm[i].reshape(shapes[nm])
        new_v[nm] = sm_nv[i].reshape(shapes[nm])
    order = ["w_ada", "b_ada", "norm1_g", "w_in", "conv_w", "conv_b", "lru_wa", "lru_ba", "lru_wx", "lru_bx",
             "lru_lambda", "w_lru_out", "w_attn_out", "attn_sinks", "rel_bias", "w_out", "norm2_g", "w_ff1", "w_ff2",
             "final_g"]
    return (loss, grad_x[None], *[grads[n] for n in order], *[deltas[n] for n in order],
            *[new_m[n] for n in order], *[new_v[n] for n in order])
```

```python
import math

import numpy as np
import jax
import jax.numpy as jnp
from jax import lax
from jax.experimental import pallas as pl
from jax.experimental.pallas import tpu as pltpu

F32 = jnp.float32
BF16 = jnp.bfloat16
SDS = jax.ShapeDtypeStruct
MESH = pl.DeviceIdType.MESH

D = 2048
D_FF = 4 * D
N_HEADS = 32
HEAD_DIM = 64
BLOCK = 128
N_LRU_BLOCKS = 16
LRU_C = 8.0
EPS = 1e-6
NEG_INF = -1e30
N_BUCKETS = 32
MAX_DISTANCE = 128
IN_W = 10752
IN_SHARD = IN_W // 4
IN_TILE = 896
ADA_SHARD = 6 * D // 4
OFF_LRU, OFF_GATE, OFF_Q, OFF_K, OFF_V, OFF_GA, OFF_GB = 0, 2048, 4096, 6144, 6400, 6656, 8704
SCALE = HEAD_DIM ** -0.5
N_CHIPS = 4
N_DEV = 8

ADAM_LR, ADAM_B1, ADAM_B2, ADAM_EPS, ADAM_WD, ADAM_STEP = 0.001, 0.9, 0.999, 1e-08, 0.01, 10
ADAM_C1 = 1.0 - ADAM_B1 ** ADAM_STEP
ADAM_C2 = 1.0 - ADAM_B2 ** ADAM_STEP

VMEM_LIMIT = 52 * 2 ** 20
SUB = 128
WG_TM = 1024
V_G1, V_SCALE1, V_SHIFT1, V_GATE1, V_G2, V_SCALE2, V_SHIFT2, V_GATE2, V_G3 = range(9)
L_CW0, L_CB, L_BA, L_BX, L_LAM = 0, 4, 5, 6, 7
P_VEC, P_CONVW, P_MOD, P_ATT, P_WA = 0, 8, 16, 24, 32


def _cp(n_axes):
    return pltpu.CompilerParams(dimension_semantics=("arbitrary",) * n_axes, vmem_limit_bytes=VMEM_LIMIT)


def _dot(a, b):
    return jnp.dot(a, b, preferred_element_type=F32)


def _dot_nt(a, b):
    return lax.dot_general(a, b, (((1,), (1,)), ((), ())), preferred_element_type=F32)


def _dot_tn(a, b):
    return lax.dot_general(a, b, (((0,), (0,)), ((), ())), preferred_element_type=F32)


_G0 = math.sqrt(2.0 / math.pi)
_G1 = 0.044715


def _gelu(x):
    return 0.5 * x * (1.0 + jnp.tanh(_G0 * (x + _G1 * x * x * x)))


def _gelu_grad(x):
    x2 = x * x
    t = jnp.tanh(_G0 * (x + _G1 * x * x2))
    return 0.5 * (1.0 + t) + 0.5 * x * (1.0 - t * t) * _G0 * (1.0 + 3.0 * _G1 * x2)


def _sigmoid(x):
    return 0.5 * jnp.tanh(0.5 * x) + 0.5


def _one_minus_exp2(x):
    t = jnp.tanh(x)
    return (-2.0 * t) / (1.0 - t)


def _softplus(z):
    e = jnp.exp(-jnp.abs(z))
    u = 1.0 + e
    l1p = jnp.where(u == 1.0, e, jnp.log(u) * e / (u - 1.0))
    return jnp.maximum(z, 0.0) + l1p


def _adamw_math(w, g, m, v):
    m2 = ADAM_B1 * m + (1.0 - ADAM_B1) * g
    v2 = ADAM_B2 * v + (1.0 - ADAM_B2) * (g * g)
    m_hat = m2 / ADAM_C1
    v_hat = v2 / ADAM_C2
    delta = -ADAM_LR * (m_hat / (jnp.sqrt(v_hat) + ADAM_EPS) + ADAM_WD * w)
    return delta, m2, v2


def _rms_parts(xv):
    r = lax.rsqrt(jnp.mean(xv * xv, axis=-1, keepdims=True) + EPS)
    return r, xv * r


def _row_fetches(hbm_refs, bufs, sems, i, rows):
    return [pltpu.make_async_copy(h.at[pl.ds(i * rows, rows), :], b, sems.at[n])
            for n, (h, b) in enumerate(zip(hbm_refs, bufs))]


def _modulated_norm(x_ref, v_ref, row_g, row_scale, row_shift, h_ref, rows):
    g, scale, shift = v_ref[row_g:row_g + 1, :], v_ref[row_scale:row_scale + 1, :], v_ref[row_shift:row_shift + 1, :]

    def sub(rb, carry):
        rs = pl.ds(pl.multiple_of(rb * SUB, SUB), SUB)
        _, xh = _rms_parts(x_ref[rs, :])
        h_ref[rs, :] = ((xh * g) * (1.0 + scale) + shift).astype(BF16)
        return carry

    lax.fori_loop(0, rows // SUB, sub, 0)


def inproj_fwd(name, x, h, vecs, w_in, proj, shard):
    s = x.shape[0]
    tm = min(1024, s)
    per = IN_SHARD // IN_TILE
    first = h is None

    def body(*refs):
        if first:
            _, x_ref, v_ref, w_ref, proj_ref, h_ref = refs

            @pl.when(pl.program_id(1) == 0)
            def _():
                _modulated_norm(x_ref, v_ref, V_G1, V_SCALE1, V_SHIFT1, h_ref, tm)
        else:
            _, h_ref, w_ref, _, proj_ref = refs
        proj_ref[...] = _dot(h_ref[...], w_ref[...]).astype(BF16)

    rows = pl.BlockSpec((tm, D), lambda i, j, sr: (i, 0))
    w_spec = pl.BlockSpec((None, D, IN_TILE), lambda i, j, sr: (sr[0], 0, j))
    proj_spec = pl.BlockSpec((tm, IN_TILE), lambda i, j, sr: (i, sr[0] * per + j))
    if first:
        gs = pltpu.PrefetchScalarGridSpec(
            num_scalar_prefetch=1, grid=(s // tm, per),
            in_specs=[rows, pl.BlockSpec((16, D), lambda i, j, sr: (0, 0)), w_spec], out_specs=[proj_spec, rows])
        return pl.pallas_call(body, name=name, grid_spec=gs, out_shape=[SDS((s, IN_W), BF16), SDS((s, D), BF16)],
                              compiler_params=_cp(2))(shard, x, vecs, w_in)
    gs = pltpu.PrefetchScalarGridSpec(
        num_scalar_prefetch=1, grid=(s // tm, per),
        in_specs=[rows, w_spec, pl.BlockSpec(memory_space=pl.ANY)], out_specs=proj_spec)
    return pl.pallas_call(body, name=name, grid_spec=gs, out_shape=SDS((s, IN_W), BF16),
                          input_output_aliases={3: 0}, compiler_params=_cp(2))(shard, h, w_in, proj)


def _lru_block_fwd(xbuf, lv_ref, wa_ref, wx_ref, b, t, first):
    cs = slice(b * 128, (b + 1) * 128)
    x0 = xbuf[pl.ds(8, t), cs]
    x1 = xbuf[pl.ds(7, t), cs]
    x2 = xbuf[pl.ds(6, t), cs]
    x3 = xbuf[pl.ds(5, t), cs]
    xc = (lv_ref[L_CB:L_CB + 1, cs] + lv_ref[3:4, cs] * x0 + lv_ref[2:3, cs] * x1
          + lv_ref[1:2, cs] * x2 + lv_ref[0:1, cs] * x3)
    xcb = xc.astype(BF16)
    r = _sigmoid(_dot(xcb, wa_ref[b]) + lv_ref[L_BA:L_BA + 1, cs])
    ig = _sigmoid(_dot(xcb, wx_ref[b]) + lv_ref[L_BX:L_BX + 1, cs])
    sp = _softplus(-lv_ref[L_LAM:L_LAM + 1, cs])
    log_a = (-LRU_C) * r * sp
    a = jnp.exp(log_a)
    mult = jnp.where(first, 1.0, jnp.sqrt(_one_minus_exp2(log_a)))
    return (x0, x1, x2, x3), xc, xcb, r, ig, sp, a, mult


def lru_fwd(proj, lvec, wa, wx):
    s = proj.shape[0]
    t = min(256, s)

    def body(lx_ref, gate_ref, lv_ref, wa_ref, wx_ref, ya_ref, rec_ref, xbuf, a_s, u_s, hc):
        i = pl.program_id(0)

        @pl.when(i == 0)
        def _():
            xbuf[pl.ds(0, 8), :] = jnp.zeros((8, D), F32)
            hc[...] = jnp.zeros((8, D), F32)

        @pl.when(i > 0)
        def _():
            xbuf[pl.ds(0, 8), :] = xbuf[pl.ds(t, 8), :]

        xbuf[pl.ds(8, t), :] = lx_ref[...].astype(F32)
        first = (lax.broadcasted_iota(jnp.int32, (t, 128), 0) + i * t) == 0
        for b in range(N_LRU_BLOCKS):
            cs = slice(b * 128, (b + 1) * 128)
            _, xc, _, _, ig, _, a, mult = _lru_block_fwd(xbuf, lv_ref, wa_ref, wx_ref, b, t, first)
            a_s[:, cs] = a
            u_s[:, cs] = mult * (ig * xc)

        def step(tt, h):
            h = a_s[pl.ds(tt, 1), :] * h + u_s[pl.ds(tt, 1), :]
            rec_ref[pl.ds(tt, 1), :] = h
            return h

        hc[0:1, :] = lax.fori_loop(0, t, step, hc[0:1, :], unroll=8)
        for b in range(N_LRU_BLOCKS):
            cs = slice(b * 128, (b + 1) * 128)
            ya_ref[:, cs] = (rec_ref[:, cs] * _gelu(gate_ref[:, cs].astype(F32))).astype(BF16)

    return pl.pallas_call(
        body, name="lru_fwd", grid=(s // t,),
        in_specs=[pl.BlockSpec((t, D), lambda i: (i, OFF_LRU // D)),
                  pl.BlockSpec((t, D), lambda i: (i, OFF_GATE // D)),
                  pl.BlockSpec((8, D), lambda i: (0, 0)),
                  pl.BlockSpec((N_LRU_BLOCKS, 128, 128), lambda i: (0, 0, 0)),
                  pl.BlockSpec((N_LRU_BLOCKS, 128, 128), lambda i: (0, 0, 0))],
        out_specs=[pl.BlockSpec((t, D), lambda i: (i, 0)), pl.BlockSpec((t, D), lambda i: (i, 0))],
        out_shape=[SDS((s, D), BF16), SDS((s, D), F32)],
        scratch_shapes=[pltpu.VMEM((t + 8, D), F32), pltpu.VMEM((t, D), F32), pltpu.VMEM((t, D), F32),
                        pltpu.VMEM((8, D), F32)],
        compiler_params=_cp(1))(proj, proj, lvec, wa, wx)


def t5_bucket_table():
    qi = np.arange(BLOCK)[:, None]
    ki = np.arange(2 * BLOCK)[None, :]
    rel = qi + BLOCK - ki
    relc = np.maximum(rel, 0)
    max_exact = N_BUCKETS // 2
    relf = np.maximum(relc, 1).astype(np.float32)
    large = max_exact + (np.log(relf / np.float32(max_exact)) / np.float32(math.log(MAX_DISTANCE / max_exact))
                         * np.float32(N_BUCKETS - max_exact)).astype(np.int32)
    large = np.minimum(large, N_BUCKETS - 1)
    bucket = np.where(relc < max_exact, relc, large)
    bucket = np.where((rel >= 0) & (rel < BLOCK), bucket, -1)
    return jnp.asarray(bucket.reshape(1, BLOCK * 2 * BLOCK), jnp.int32)


def bias_band(rel_bias_t, buckets):
    n = BLOCK * 2 * BLOCK
    tn = 4096

    def body(bk_ref, rb_ref, o_ref):
        row = lax.broadcasted_iota(jnp.int32, (N_BUCKETS, tn), 0)
        oh = jnp.where(row == bk_ref[...], 1.0, 0.0).astype(BF16)
        rb = rb_ref[...]
        p0 = rb.astype(BF16)
        r1 = rb - p0.astype(F32)
        p1 = r1.astype(BF16)
        p2 = (r1 - p1.astype(F32)).astype(BF16)
        o_ref[...] = _dot(p0, oh) + _dot(p1, oh) + _dot(p2, oh)

    return pl.pallas_call(
        body, name="bias_band", grid=(n // tn,),
        in_specs=[pl.BlockSpec((1, tn), lambda i: (0, i)), pl.BlockSpec((N_HEADS, N_BUCKETS), lambda i: (0, 0))],
        out_specs=pl.BlockSpec((N_HEADS, tn), lambda i: (0, i)),
        out_shape=SDS((N_HEADS, n), F32), compiler_params=_cp(1))(buckets, rel_bias_t)


def bias_band_bwd(dband, buckets):
    n = BLOCK * 2 * BLOCK
    tn = 4096

    def body(bk_ref, d_ref, o_ref):
        @pl.when(pl.program_id(0) == 0)
        def _():
            o_ref[...] = jnp.zeros_like(o_ref)
        row = lax.broadcasted_iota(jnp.int32, (N_BUCKETS, tn), 0)
        oh = jnp.where(row == bk_ref[...], 1.0, 0.0).astype(BF16)
        dv = d_ref[...]
        p0 = dv.astype(BF16)
        r1 = dv - p0.astype(F32)
        p1 = r1.astype(BF16)
        p2 = (r1 - p1.astype(F32)).astype(BF16)
        o_ref[...] += _dot_nt(oh, p0) + _dot_nt(oh, p1) + _dot_nt(oh, p2)

    return pl.pallas_call(
        body, name="bias_band_bwd", grid=(n // tn,),
        in_specs=[pl.BlockSpec((1, tn), lambda i: (0, i)), pl.BlockSpec((N_HEADS, tn), lambda i: (0, i))],
        out_specs=pl.BlockSpec((N_BUCKETS, N_HEADS), lambda i: (0, 0)),
        out_shape=SDS((N_BUCKETS, N_HEADS), F32), compiler_params=_cp(1))(buckets, dband)


def _dup_half(band, which):
    lane = lax.broadcasted_iota(jnp.int32, band.shape, 1)
    rolled = pltpu.roll(band, 64, 1)
    keep = (lane < 64) if which == 0 else (lane >= 64)
    return jnp.where(keep, band, rolled)


def _attn_probs(scores, bias, sink, valid):
    sc = jnp.where(valid, scores * SCALE + bias, NEG_INF)
    m = jnp.maximum(jnp.max(sc, axis=-1, keepdims=True), sink)
    e = jnp.exp(sc - m)
    es = jnp.exp(sink - m)
    inv = 1.0 / (jnp.sum(e, axis=-1, keepdims=True) + es)
    return e * inv, es * inv


def _stack_heads(src_ref, kv, dst):
    lane = lax.broadcasted_iota(jnp.int32, (BLOCK, 128), 1)
    for jj in range(4):
        slab = src_ref[:, (4 * kv + jj) * 128:(4 * kv + jj + 1) * 128]
        for hh in range(2):
            keep = (lane < 64) if hh == 0 else (lane >= 64)
            dst[pl.ds((2 * jj + hh) * BLOCK, BLOCK), :] = jnp.where(keep, slab, jnp.zeros_like(slab))


def _unstack_heads(stacked, dst_ref, kv):
    lane = lax.broadcasted_iota(jnp.int32, (BLOCK, 128), 1)
    for jj in range(4):
        lo = stacked[(2 * jj) * BLOCK:(2 * jj + 1) * BLOCK]
        hi = stacked[(2 * jj + 1) * BLOCK:(2 * jj + 2) * BLOCK]
        dst_ref[:, (4 * kv + jj) * 128:(4 * kv + jj + 1) * 128] = jnp.where(lane < 64, lo, hi).astype(dst_ref.dtype)


def _band_valid(n):
    qi = lax.broadcasted_iota(jnp.int32, (BLOCK, 2 * BLOCK), 0)
    ki = lax.broadcasted_iota(jnp.int32, (BLOCK, 2 * BLOCK), 1)
    rel = qi + BLOCK - ki
    return (rel >= 0) & (rel < BLOCK) & ((ki >= BLOCK) | (n > 0))


def _kv_bands(prev_ref, cur_ref):
    band = jnp.concatenate([prev_ref[...].astype(F32), cur_ref[...].astype(F32)], axis=0)
    return [_dup_half(band, 0).astype(BF16), _dup_half(band, 1).astype(BF16)]


def attn_fwd(proj, band, sinks):
    s = proj.shape[0]
    nb = s // BLOCK
    qw = 1024

    def body(sk_ref, q_ref, kp_ref, kc_ref, vp_ref, vc_ref, b_ref, o_ref, qs_buf, s_buf, p_buf):
        n = pl.program_id(0)
        gp = pl.program_id(1)
        valid = _band_valid(n)
        kks = _kv_bands(kp_ref, kc_ref)
        vvs = _kv_bands(vp_ref, vc_ref)
        for kv in range(2):
            _stack_heads(q_ref, kv, qs_buf)
            s_buf[...] = _dot_nt(qs_buf[...], kks[kv])
            for hq in range(8):
                hl = 8 * kv + hq
                rows = pl.ds(hq * BLOCK, BLOCK)
                p, _ = _attn_probs(s_buf[rows, :], b_ref[hl], sk_ref[gp * 16 + hl], valid)
                p_buf[rows, :] = p.astype(BF16)
            _unstack_heads(_dot(p_buf[...], vvs[kv]), o_ref, kv)

    kb, vb = OFF_K // 128, OFF_V // 128
    return pl.pallas_call(
        body, name="attn_fwd", grid=(nb, 2),
        in_specs=[pl.BlockSpec(memory_space=pltpu.SMEM),
                  pl.BlockSpec((BLOCK, qw), lambda n, g: (n, OFF_Q // qw + g)),
                  pl.BlockSpec((BLOCK, 128), lambda n, g: (jnp.maximum(n - 1, 0), kb + g)),
                  pl.BlockSpec((BLOCK, 128), lambda n, g: (n, kb + g)),
                  pl.BlockSpec((BLOCK, 128), lambda n, g: (jnp.maximum(n - 1, 0), vb + g)),
                  pl.BlockSpec((BLOCK, 128), lambda n, g: (n, vb + g)),
                  pl.BlockSpec((16, BLOCK, 2 * BLOCK), lambda n, g: (g, 0, 0))],
        out_specs=pl.BlockSpec((BLOCK, qw), lambda n, g: (n, g)),
        out_shape=SDS((s, D), BF16),
        scratch_shapes=[pltpu.VMEM((8 * BLOCK, 128), BF16), pltpu.VMEM((8 * BLOCK, 2 * BLOCK), F32),
                        pltpu.VMEM((8 * BLOCK, 2 * BLOCK), BF16)],
        compiler_params=_cp(2))(sinks, proj, proj, proj, proj, proj, band)


def merge_fwd(ya, att, w_lru_out, w_attn_out, proj):
    s = ya.shape[0]
    tm, tn = min(1024, s), 512

    def body(ya_ref, at_ref, wl_ref, wt_ref, ga_ref, gb_ref, yab_ref, mg_ref):
        y_a = _dot(ya_ref[...], wl_ref[...])
        y_b = _dot(at_ref[...], wt_ref[...])
        yab_ref[0] = y_a.astype(BF16)
        yab_ref[1] = y_b.astype(BF16)
        mg_ref[...] = (_sigmoid(ga_ref[...].astype(F32)) * y_a + _sigmoid(gb_ref[...].astype(F32)) * y_b).astype(BF16)

    return pl.pallas_call(
        body, name="merge_fwd", grid=(s // tm, D // tn),
        in_specs=[pl.BlockSpec((tm, D), lambda i, j: (i, 0)), pl.BlockSpec((tm, D), lambda i, j: (i, 0)),
                  pl.BlockSpec((D, tn), lambda i, j: (0, j)), pl.BlockSpec((D, tn), lambda i, j: (0, j)),
                  pl.BlockSpec((tm, tn), lambda i, j: (i, OFF_GA // tn + j)),
                  pl.BlockSpec((tm, tn), lambda i, j: (i, OFF_GB // tn + j))],
        out_specs=[pl.BlockSpec((2, tm, tn), lambda i, j: (0, i, j)), pl.BlockSpec((tm, tn), lambda i, j: (i, j))],
        out_shape=[SDS((2, s, D), BF16), SDS((s, D), BF16)],
        compiler_params=_cp(2))(ya, att, w_lru_out, w_attn_out, proj, proj)


def outproj_fwd(merged, w_out, x, vecs):
    s = x.shape[0]
    tm, tn = min(1024, s), 512

    def body(m_ref, w_ref, x_ref, v_ref, x1_ref, o1_ref):
        o1 = _dot(m_ref[...], w_ref[...])
        o1_ref[...] = o1.astype(BF16)
        x1_ref[...] = x_ref[...] + v_ref[V_GATE1:V_GATE1 + 1, :] * o1

    return pl.pallas_call(
        body, name="outproj_fwd", grid=(s // tm, D // tn),
        in_specs=[pl.BlockSpec((tm, D), lambda i, j: (i, 0)), pl.BlockSpec((D, tn), lambda i, j: (0, j)),
                  pl.BlockSpec((tm, tn), lambda i, j: (i, j)), pl.BlockSpec((16, tn), lambda i, j: (0, j))],
        out_specs=[pl.BlockSpec((tm, tn), lambda i, j: (i, j)), pl.BlockSpec((tm, tn), lambda i, j: (i, j))],
        out_shape=[SDS((s, D), F32), SDS((s, D), BF16)],
        compiler_params=_cp(2))(merged, w_out, x, vecs)


def ff1_fwd(x1, vecs, w_ff1):
    s = x1.shape[0]
    tm, tn = min(1024, s), 512
    per = D // tn

    def body(x_ref, v_ref, w_ref, f_ref, h_ref):
        @pl.when(pl.program_id(1) == 0)
        def _():
            _modulated_norm(x_ref, v_ref, V_G2, V_SCALE2, V_SHIFT2, h_ref, tm)
        f_ref[...] = _dot(h_ref[...], w_ref[...]).astype(BF16)

    return pl.pallas_call(
        body, name="ff1_fwd", grid=(s // tm, D_FF // tn),
        in_specs=[pl.BlockSpec((tm, D), lambda i, j: (i, 0)), pl.BlockSpec((16, D), lambda i, j: (0, 0)),
                  pl.BlockSpec((None, D, tn), lambda i, j: (j // per, 0, j % per))],
        out_specs=[pl.BlockSpec((tm, tn), lambda i, j: (i, j)), pl.BlockSpec((tm, D), lambda i, j: (i, 0))],
        out_shape=[SDS((s, D_FF), BF16), SDS((s, D), BF16)],
        compiler_params=_cp(2))(x1, vecs, w_ff1)


def ff2_loss(f, w_ff2, x1, tgt, vecs):
    s = x1.shape[0]
    tm, tk = min(512, s), 1024
    nk = D_FF // tk

    def body(f_ref, w_ref, x1_hbm, t_hbm, v_ref, dx2_ref, do2_ref, sums_ref, loss_ref, acc, x1_ref, t_ref, sems):
        i, k = pl.program_id(0), pl.program_id(1)
        fetches = _row_fetches((x1_hbm, t_hbm), (x1_ref, t_ref), sems, i, tm)

        @pl.when((i == 0) & (k == 0))
        def _():
            sums_ref[...] = jnp.zeros_like(sums_ref)
            loss_ref[...] = jnp.zeros_like(loss_ref)

        @pl.when(k == 0)
        def _():
            acc[...] = jnp.zeros_like(acc)
            for cp in fetches:
                cp.start()

        fv = jnp.maximum(f_ref[...].astype(F32), 0.0)
        acc[...] += _dot((fv * fv).astype(BF16), w_ref[...])

        @pl.when(k == nk - 1)
        def _():
            for cp in fetches:
                cp.wait()
            gate2 = v_ref[V_GATE2:V_GATE2 + 1, :]
            g3 = v_ref[V_G3:V_G3 + 1, :]

            def sub(rb, carry):
                rs = pl.ds(pl.multiple_of(rb * SUB, SUB), SUB)
                o2 = acc[rs, :]
                x2 = x1_ref[rs, :] + gate2 * o2
                r3, xh = _rms_parts(x2)
                e = xh * g3 - t_ref[rs, :]
                loss_ref[...] += (0.5 / D) * jnp.sum(e * e)
                dy = e * (1.0 / D)
                sums_ref[0:1, :] += jnp.sum(dy * xh, axis=0, keepdims=True)
                dxh = dy * g3
                dx2 = r3 * (dxh - xh * jnp.mean(dxh * xh, axis=-1, keepdims=True))
                sums_ref[1:2, :] += jnp.sum(dx2 * o2, axis=0, keepdims=True)
                dx2_ref[rs, :] = dx2
                do2_ref[rs, :] = (dx2 * gate2).astype(BF16)
                return carry

            lax.fori_loop(0, tm // SUB, sub, 0)

    return pl.pallas_call(
        body, name="ff2_loss", grid=(s // tm, nk),
        in_specs=[pl.BlockSpec((tm, tk), lambda i, k: (i, k)), pl.BlockSpec((tk, D), lambda i, k: (k, 0)),
                  pl.BlockSpec(memory_space=pl.ANY), pl.BlockSpec(memory_space=pl.ANY),
                  pl.BlockSpec((16, D), lambda i, k: (0, 0))],
        out_specs=[pl.BlockSpec((tm, D), lambda i, k: (i, 0)), pl.BlockSpec((tm, D), lambda i, k: (i, 0)),
                   pl.BlockSpec((8, D), lambda i, k: (0, 0)), pl.BlockSpec((8, 128), lambda i, k: (0, 0))],
        out_shape=[SDS((s, D), F32), SDS((s, D), BF16), SDS((8, D), F32), SDS((8, 128), F32)],
        scratch_shapes=[pltpu.VMEM((tm, D), F32), pltpu.VMEM((tm, D), F32), pltpu.VMEM((tm, D), F32),
                        pltpu.SemaphoreType.DMA((2,))],
        compiler_params=_cp(2))(f, w_ff2, x1, tgt, vecs)


def ff2_bwd(do2, w_ff2, f):
    s = do2.shape[0]
    tm, tn = min(1024, s), 512

    def body(d_ref, w_ref, f_ref, o_ref):
        dff = _dot_nt(d_ref[...], w_ref[...])
        o_ref[...] = (dff * (2.0 * jnp.maximum(f_ref[...].astype(F32), 0.0))).astype(BF16)

    return pl.pallas_call(
        body, name="ff2_bwd", grid=(s // tm, D_FF // tn),
        in_specs=[pl.BlockSpec((tm, D), lambda i, j: (i, 0)), pl.BlockSpec((tn, D), lambda i, j: (j, 0)),
                  pl.BlockSpec((tm, tn), lambda i, j: (i, j))],
        out_specs=pl.BlockSpec((tm, tn), lambda i, j: (i, j)),
        out_shape=SDS((s, D_FF), BF16), compiler_params=_cp(2))(do2, w_ff2, f)


def weight_grad(name, a, b, tn, out_shape, out_block, out_map, relu2=False, b_part=0):
    s, m = a.shape
    n = b.shape[1]
    tm = WG_TM
    chunk = min(1024, s)
    nch = s // chunk

    def body(a_hbm, b_ref, o_ref, a_buf, at_s, sem):
        i = pl.program_id(0)

        @pl.when(pl.program_id(1) == 0)
        def _():
            def fetch(ch):
                return pltpu.make_async_copy(a_hbm.at[pl.ds(ch * chunk, chunk), pl.ds(i * tm, tm)],
                                             a_buf.at[ch % 2], sem.at[ch % 2])
            fetch(0).start()
            for ch in range(nch):
                if ch + 1 < nch:
                    fetch(ch + 1).start()
                fetch(ch).wait()
                av = a_buf[ch % 2]
                if relu2:
                    fv = jnp.maximum(av.astype(F32), 0.0)
                    av = (fv * fv).astype(BF16)
                at_s[:, ch * chunk:(ch + 1) * chunk] = av.T

        o_ref[...] = _dot(at_s[...], b_ref[...]).astype(BF16)

    return pl.pallas_call(
        body, name=name, grid=(m // tm, n // tn),
        in_specs=[pl.BlockSpec(memory_space=pl.ANY), pl.BlockSpec((s, tn), lambda i, j: (b_part, j))],
        out_specs=pl.BlockSpec(out_block, lambda i, j: out_map(i, j)),
        out_shape=SDS(out_shape, BF16),
        scratch_shapes=[pltpu.VMEM((2, chunk, tm), BF16), pltpu.VMEM((tm, s), BF16), pltpu.SemaphoreType.DMA((2,))],
        compiler_params=_cp(2))(a, b)


def ff1_bwd(df, w_ff1, x1, dx2, o1, vecs):
    s = df.shape[0]
    tm, tk = min(512, s), 1024
    nk = D_FF // tk
    per = D // tk

    def body(d_ref, w_ref, x1_hbm, dx2_hbm, o1_hbm, v_ref, dx1_ref, do1_ref, sums_ref, acc, x1_ref, dx2_ref, o1_ref, sems):
        i, k = pl.program_id(0), pl.program_id(1)
        fetches = _row_fetches((x1_hbm, dx2_hbm, o1_hbm), (x1_ref, dx2_ref, o1_ref), sems, i, tm)

        @pl.when((i == 0) & (k == 0))
        def _():
            sums_ref[...] = jnp.zeros_like(sums_ref)

        @pl.when(k == 0)
        def _():
            acc[...] = jnp.zeros_like(acc)
            for cp in fetches:
                cp.start()

        acc[...] += _dot_nt(d_ref[...], w_ref[...])

        @pl.when(k == nk - 1)
        def _():
            for cp in fetches:
                cp.wait()
            g2 = v_ref[V_G2:V_G2 + 1, :]
            scale2 = v_ref[V_SCALE2:V_SCALE2 + 1, :]
            gate1 = v_ref[V_GATE1:V_GATE1 + 1, :]

            def sub(rb, carry):
                rs = pl.ds(pl.multiple_of(rb * SUB, SUB), SUB)
                dh = acc[rs, :]
                r2, xh = _rms_parts(x1_ref[rs, :])
                sums_ref[0:1, :] += jnp.sum(dh, axis=0, keepdims=True)
                sums_ref[1:2, :] += jnp.sum(dh * (xh * g2), axis=0, keepdims=True)
                dxn = dh * (1.0 + scale2)
                sums_ref[2:3, :] += jnp.sum(dxn * xh, axis=0, keepdims=True)
                dxh = dxn * g2
                dx1 = dx2_ref[rs, :] + r2 * (dxh - xh * jnp.mean(dxh * xh, axis=-1, keepdims=True))
                sums_ref[3:4, :] += jnp.sum(dx1 * o1_ref[rs, :].astype(F32), axis=0, keepdims=True)
                dx1_ref[rs, :] = dx1
                do1_ref[rs, :] = (dx1 * gate1).astype(BF16)
                return carry

            lax.fori_loop(0, tm // SUB, sub, 0)

    return pl.pallas_call(
        body, name="ff1_bwd", grid=(s // tm, nk),
        in_specs=[pl.BlockSpec((tm, tk), lambda i, k: (i, k)),
                  pl.BlockSpec((None, D, tk), lambda i, k: (k // per, 0, k % per)),
                  pl.BlockSpec(memory_space=pl.ANY), pl.BlockSpec(memory_space=pl.ANY),
                  pl.BlockSpec(memory_space=pl.ANY), pl.BlockSpec((16, D), lambda i, k: (0, 0))],
        out_specs=[pl.BlockSpec((tm, D), lambda i, k: (i, 0)), pl.BlockSpec((tm, D), lambda i, k: (i, 0)),
                   pl.BlockSpec((8, D), lambda i, k: (0, 0))],
        out_shape=[SDS((s, D), F32), SDS((s, D), BF16), SDS((8, D), F32)],
        scratch_shapes=[pltpu.VMEM((tm, D), F32), pltpu.VMEM((tm, D), F32), pltpu.VMEM((tm, D), F32),
                        pltpu.VMEM((tm, D), BF16), pltpu.SemaphoreType.DMA((3,))],
        compiler_params=_cp(2))(df, w_ff1, x1, dx2, o1, vecs)


def outproj_bwd(do1, w_out, yab, proj):
    s = do1.shape[0]
    tm, tn = min(1024, s), 512
    per = D // tn

    def body(d_ref, w_ref, y_ref, g_ref, dy_ref, dp_ref):
        dm = _dot_nt(d_ref[...], w_ref[...])
        sg = _sigmoid(g_ref[...].astype(F32))
        dy_ref[...] = (dm * sg).astype(BF16)
        dp_ref[...] = (dm * y_ref[...].astype(F32) * sg * (1.0 - sg)).astype(BF16)

    return pl.pallas_call(
        body, name="outproj_bwd", grid=(s // tm, 2 * per),
        in_specs=[pl.BlockSpec((tm, D), lambda i, j: (i, 0)), pl.BlockSpec((tn, D), lambda i, j: (j % per, 0)),
                  pl.BlockSpec((None, tm, tn), lambda i, j: (j // per, i, j % per)),
                  pl.BlockSpec((tm, tn), lambda i, j: (i, OFF_GA // tn + j))],
        out_specs=[pl.BlockSpec((None, tm, tn), lambda i, j: (j // per, i, j % per)),
                   pl.BlockSpec((tm, tn), lambda i, j: (i, OFF_GA // tn + j))],
        out_shape=[SDS((2, s, D), BF16), SDS((s, IN_W), BF16)],
        compiler_params=_cp(2))(do1, w_out, yab, proj)


def lruout_bwd(dyab, w_lru_out, rec, proj, dproj):
    s = rec.shape[0]
    tm, tn = min(1024, s), 512

    def body(d_ref, w_ref, r_ref, g_ref, dp_in, dr_ref, dp_ref):
        dya = _dot_nt(d_ref[...], w_ref[...])
        gate = g_ref[...].astype(F32)
        dr_ref[...] = dya * _gelu(gate)
        dp_ref[...] = (dya * r_ref[...] * _gelu_grad(gate)).astype(BF16)

    return pl.pallas_call(
        body, name="lruout_bwd", grid=(s // tm, D // tn),
        in_specs=[pl.BlockSpec((None, tm, D), lambda i, j: (0, i, 0)), pl.BlockSpec((tn, D), lambda i, j: (j, 0)),
                  pl.BlockSpec((tm, tn), lambda i, j: (i, j)),
                  pl.BlockSpec((tm, tn), lambda i, j: (i, OFF_GATE // tn + j)),
                  pl.BlockSpec(memory_space=pl.ANY)],
        out_specs=[pl.BlockSpec((tm, tn), lambda i, j: (i, j)),
                   pl.BlockSpec((tm, tn), lambda i, j: (i, OFF_GATE // tn + j))],
        out_shape=[SDS((s, D), F32), SDS((s, IN_W), BF16)],
        input_output_aliases={4: 1},
        compiler_params=_cp(2))(dyab, w_lru_out, rec, proj, dproj)


def attnout_bwd(dyab, w_attn_out):
    s = dyab.shape[1]
    tm, tn = min(1024, s), 512

    def body(d_ref, w_ref, o_ref):
        o_ref[...] = _dot_nt(d_ref[...], w_ref[...]).astype(BF16)

    return pl.pallas_call(
        body, name="attnout_bwd", grid=(s // tm, D // tn),
        in_specs=[pl.BlockSpec((None, tm, D), lambda i, j: (1, i, 0)), pl.BlockSpec((tn, D), lambda i, j: (j, 0))],
        out_specs=pl.BlockSpec((tm, tn), lambda i, j: (i, j)),
        out_shape=SDS((s, D), BF16), compiler_params=_cp(2))(dyab, w_attn_out)


def attn_bwd(proj, band, sinks, datt, dproj):
    s = proj.shape[0]
    nb = s // BLOCK
    qw = 1024

    def body(sk_ref, q_ref, kp_ref, kc_ref, vp_ref, vc_ref, b_ref, do_ref, dp_in,
             dq_ref, dkb_ref, dvb_ref, db_ref, ds_ref, qs_buf, dos_buf, s_buf, dp_buf, p_buf, dsc_buf):
        gp = pl.program_id(0)
        n = pl.program_id(1)

        @pl.when(n == 0)
        def _():
            db_ref[...] = jnp.zeros_like(db_ref)
            ds_ref[...] = jnp.zeros_like(ds_ref)

        valid = _band_valid(n)
        kks = _kv_bands(kp_ref, kc_ref)
        vvs = _kv_bands(vp_ref, vc_ref)
        lane_b = lax.broadcasted_iota(jnp.int32, (2 * BLOCK, 128), 1)
        dks, dvs = [], []
        for kv in range(2):
            _stack_heads(q_ref, kv, qs_buf)
            _stack_heads(do_ref, kv, dos_buf)
            s_buf[...] = _dot_nt(qs_buf[...], kks[kv])
            dp_buf[...] = _dot_nt(dos_buf[...], vvs[kv])
            for hq in range(8):
                hl = 8 * kv + hq
                rows = pl.ds(hq * BLOCK, BLOCK)
                p, ps = _attn_probs(s_buf[rows, :], b_ref[hl], sk_ref[gp * 16 + hl], valid)
                dp = dp_buf[rows, :]
                delta = jnp.sum(p * dp, axis=-1, keepdims=True)
                dsc = p * (dp - delta)
                db_ref[hl] += dsc
                ds_ref[hl:hl + 1, :] += jnp.zeros((1, 128), F32) - jnp.sum(ps * delta)
                p_buf[rows, :] = p.astype(BF16)
                dsc_buf[rows, :] = (dsc * SCALE).astype(BF16)
            _unstack_heads(_dot(dsc_buf[...], kks[kv]), dq_ref, kv)
            dk = _dot_tn(dsc_buf[...], qs_buf[...])
            dv = _dot_tn(p_buf[...], dos_buf[...])
            dks.append(dk + pltpu.roll(dk, 64, 1))
            dvs.append(dv + pltpu.roll(dv, 64, 1))
        dkb_ref[...] = jnp.where(lane_b < 64, dks[0], dks[1])
        dvb_ref[...] = jnp.where(lane_b < 64, dvs[0], dvs[1])

    kb, vb = OFF_K // 128, OFF_V // 128
    return pl.pallas_call(
        body, name="attn_bwd", grid=(2, nb),
        in_specs=[pl.BlockSpec(memory_space=pltpu.SMEM),
                  pl.BlockSpec((BLOCK, qw), lambda g, n: (n, OFF_Q // qw + g)),
                  pl.BlockSpec((BLOCK, 128), lambda g, n: (jnp.maximum(n - 1, 0), kb + g)),
                  pl.BlockSpec((BLOCK, 128), lambda g, n: (n, kb + g)),
                  pl.BlockSpec((BLOCK, 128), lambda g, n: (jnp.maximum(n - 1, 0), vb + g)),
                  pl.BlockSpec((BLOCK, 128), lambda g, n: (n, vb + g)),
                  pl.BlockSpec((16, BLOCK, 2 * BLOCK), lambda g, n: (g, 0, 0)),
                  pl.BlockSpec((BLOCK, qw), lambda g, n: (n, g)),
                  pl.BlockSpec(memory_space=pl.ANY)],
        out_specs=[pl.BlockSpec((BLOCK, qw), lambda g, n: (n, OFF_Q // qw + g)),
                   pl.BlockSpec((2 * BLOCK, 128), lambda g, n: (n, g)),
                   pl.BlockSpec((2 * BLOCK, 128), lambda g, n: (n, g)),
                   pl.BlockSpec((16, BLOCK, 2 * BLOCK), lambda g, n: (g, 0, 0)),
                   pl.BlockSpec((16, 128), lambda g, n: (g, 0))],
        out_shape=[SDS((s, IN_W), BF16), SDS((nb * 2 * BLOCK, 256), F32), SDS((nb * 2 * BLOCK, 256), F32),
                   SDS((N_HEADS, BLOCK, 2 * BLOCK), F32), SDS((N_HEADS, 128), F32)],
        input_output_aliases={8: 0},
        scratch_shapes=[pltpu.VMEM((8 * BLOCK, 128), BF16), pltpu.VMEM((8 * BLOCK, 128), BF16),
                        pltpu.VMEM((8 * BLOCK, 2 * BLOCK), F32), pltpu.VMEM((8 * BLOCK, 2 * BLOCK), F32),
                        pltpu.VMEM((8 * BLOCK, 2 * BLOCK), BF16), pltpu.VMEM((8 * BLOCK, 2 * BLOCK), BF16)],
        compiler_params=_cp(2))(sinks, proj, proj, proj, proj, proj, band, datt, dproj)


def dkv_combine(dkb, dvb, dproj):
    nb = dkb.shape[0] // (2 * BLOCK)
    s = nb * BLOCK
    dkb3 = dkb.reshape(nb, 2 * BLOCK, 256)
    dvb3 = dvb.reshape(nb, 2 * BLOCK, 256)

    def body(k1, k2, v1, v2, dp_in, o_ref):
        nxt = jnp.where(pl.program_id(0) < nb - 1, 1.0, 0.0)
        o_ref[:, 0:256] = (k1[...] + nxt * k2[...]).astype(BF16)
        o_ref[:, 256:512] = (v1[...] + nxt * v2[...]).astype(BF16)

    spec1 = pl.BlockSpec((None, BLOCK, 256), lambda m: (m, 1, 0))
    spec2 = pl.BlockSpec((None, BLOCK, 256), lambda m: (jnp.minimum(m + 1, nb - 1), 0, 0))
    return pl.pallas_call(
        body, name="dkv_combine", grid=(nb,),
        in_specs=[spec1, spec2, spec1, spec2, pl.BlockSpec(memory_space=pl.ANY)],
        out_specs=pl.BlockSpec((BLOCK, 512), lambda m: (m, OFF_K // 512)),
        out_shape=SDS((s, IN_W), BF16), input_output_aliases={4: 0},
        compiler_params=_cp(1))(dkb3, dkb3, dvb3, dvb3, dproj)


def lru_bwd(proj, rec, drec, lvec, wa, wx, dproj):
    s = proj.shape[0]
    t = min(256, s)
    nt = s // t

    def body(lx_ref, lxh_ref, rec_ref, rech_ref, dr_ref, lv_ref, wa_ref, wx_ref, dp_in,
             dlx_ref, sums_ref, dwa_ref, dwx_ref,
             xbuf, hbuf, dxbuf, a_s, dh_s, xc_s, r_s, ig_s, mu_s, gc):
        step_i = pl.program_id(0)
        ti = nt - 1 - step_i

        @pl.when(step_i == 0)
        def _():
            sums_ref[...] = jnp.zeros_like(sums_ref)
            dwa_ref[...] = jnp.zeros_like(dwa_ref)
            dwx_ref[...] = jnp.zeros_like(dwx_ref)
            dxbuf[pl.ds(t, 8), :] = jnp.zeros((8, D), F32)
            gc[...] = jnp.zeros((8, D), F32)

        live = jnp.where(ti > 0, 1.0, 0.0)
        xbuf[pl.ds(0, 8), :] = lxh_ref[...].astype(F32)[8:16] * live
        xbuf[pl.ds(8, t), :] = lx_ref[...].astype(F32)
        hbuf[pl.ds(0, 8), :] = rech_ref[...] * live
        hbuf[pl.ds(8, t), :] = rec_ref[...]
        first = (lax.broadcasted_iota(jnp.int32, (t, 128), 0) + ti * t) == 0
        for b in range(N_LRU_BLOCKS):
            cs = slice(b * 128, (b + 1) * 128)
            _, xc, _, r, ig, _, a, mult = _lru_block_fwd(xbuf, lv_ref, wa_ref, wx_ref, b, t, first)
            a_s[:, cs] = a
            xc_s[:, cs] = xc
            r_s[:, cs] = r
            ig_s[:, cs] = ig
            mu_s[:, cs] = mult

        def step(q, g):
            tt = t - 1 - q
            dh = dr_ref[pl.ds(tt, 1), :] + g
            dh_s[pl.ds(tt, 1), :] = dh
            return a_s[pl.ds(tt, 1), :] * dh

        gc[0:1, :] = lax.fori_loop(0, t, step, gc[0:1, :], unroll=8)
        for b in range(N_LRU_BLOCKS):
            cs = slice(b * 128, (b + 1) * 128)
            dh = dh_s[:, cs]
            a = a_s[:, cs]
            xc = xc_s[:, cs]
            r = r_s[:, cs]
            ig = ig_s[:, cs]
            mult = mu_s[:, cs]
            sp = _softplus(-lv_ref[L_LAM:L_LAM + 1, cs])
            lam = lv_ref[L_LAM:L_LAM + 1, cs]
            da = dh * hbuf[pl.ds(7, t), cs]
            dmult = jnp.where(first, 0.0, dh * ig * xc)
            dig = dh * mult * xc
            dxc = dh * mult * ig
            dlog_a = da * a - dmult * (a * a) / mult
            dr = dlog_a * ((-LRU_C) * sp)
            dsp = jnp.sum(dlog_a * ((-LRU_C) * r), axis=0, keepdims=True)
            dza = dr * r * (1.0 - r)
            dzx = dig * ig * (1.0 - ig)
            dzab = dza.astype(BF16)
            dzxb = dzx.astype(BF16)
            xcb = xc.astype(BF16)
            dwa_ref[b] += _dot_tn(xcb, dzab)
            dwx_ref[b] += _dot_tn(xcb, dzxb)
            dxc = dxc + _dot_nt(dzab, wa_ref[b]) + _dot_nt(dzxb, wx_ref[b])
            sums_ref[L_LAM:L_LAM + 1, cs] += dsp * (-jax.nn.sigmoid(-lam))
            sums_ref[L_BA:L_BA + 1, cs] += jnp.sum(dza, axis=0, keepdims=True)
            sums_ref[L_BX:L_BX + 1, cs] += jnp.sum(dzx, axis=0, keepdims=True)
            sums_ref[L_CB:L_CB + 1, cs] += jnp.sum(dxc, axis=0, keepdims=True)
            for kk in range(4):
                sums_ref[kk:kk + 1, cs] += jnp.sum(dxc * xbuf[pl.ds(5 + kk, t), cs], axis=0, keepdims=True)
            dxbuf[pl.ds(0, t), cs] = dxc
            dlx = (lv_ref[3:4, cs] * dxc + lv_ref[2:3, cs] * dxbuf[pl.ds(1, t), cs]
                   + lv_ref[1:2, cs] * dxbuf[pl.ds(2, t), cs] + lv_ref[0:1, cs] * dxbuf[pl.ds(3, t), cs])
            dlx_ref[:, cs] = dlx.astype(BF16)
        dxbuf[pl.ds(t, 8), :] = dxbuf[pl.ds(0, 8), :]

    rev = lambda i: nt - 1 - i
    return pl.pallas_call(
        body, name="lru_bwd", grid=(nt,),
        in_specs=[pl.BlockSpec((t, D), lambda i: (rev(i), 0)),
                  pl.BlockSpec((16, D), lambda i: (jnp.maximum(rev(i) * (t // 16) - 1, 0), 0)),
                  pl.BlockSpec((t, D), lambda i: (rev(i), 0)),
                  pl.BlockSpec((8, D), lambda i: (jnp.maximum(rev(i) * (t // 8) - 1, 0), 0)),
                  pl.BlockSpec((t, D), lambda i: (rev(i), 0)),
                  pl.BlockSpec((8, D), lambda i: (0, 0)),
                  pl.BlockSpec((N_LRU_BLOCKS, 128, 128), lambda i: (0, 0, 0)),
                  pl.BlockSpec((N_LRU_BLOCKS, 128, 128), lambda i: (0, 0, 0)),
                  pl.BlockSpec(memory_space=pl.ANY)],
        out_specs=[pl.BlockSpec((t, D), lambda i: (rev(i), 0)),
                   pl.BlockSpec((8, D), lambda i: (0, 0)),
                   pl.BlockSpec((N_LRU_BLOCKS, 128, 128), lambda i: (0, 0, 0)),
                   pl.BlockSpec((N_LRU_BLOCKS, 128, 128), lambda i: (0, 0, 0))],
        out_shape=[SDS((s, IN_W), BF16), SDS((8, D), F32), SDS((N_LRU_BLOCKS, 128, 128), F32),
                   SDS((N_LRU_BLOCKS, 128, 128), F32)],
        scratch_shapes=[pltpu.VMEM((t + 8, D), F32), pltpu.VMEM((t + 8, D), F32), pltpu.VMEM((t + 8, D), F32)]
        + [pltpu.VMEM((t, D), F32)] * 6 + [pltpu.VMEM((8, D), F32)],
        input_output_aliases={8: 0},
        compiler_params=_cp(1))(proj, proj, rec, rec, drec, lvec, wa, wx, dproj)


def inproj_bwd(dproj, w_in, x, dx1, vecs):
    s = x.shape[0]
    tm, tk = min(512, s), IN_TILE
    nk = IN_W // tk
    per = IN_SHARD // tk

    def body(d_ref, w_ref, x_hbm, dx1_hbm, v_ref, gx_ref, sums_ref, acc, x_ref, dx1_ref, sems):
        i, k = pl.program_id(0), pl.program_id(1)
        fetches = _row_fetches((x_hbm, dx1_hbm), (x_ref, dx1_ref), sems, i, tm)

        @pl.when((i == 0) & (k == 0))
        def _():
            sums_ref[...] = jnp.zeros_like(sums_ref)

        @pl.when(k == 0)
        def _():
            acc[...] = jnp.zeros_like(acc)
            for cp in fetches:
                cp.start()

        acc[...] += _dot_nt(d_ref[...], w_ref[...])

        @pl.when(k == nk - 1)
        def _():
            for cp in fetches:
                cp.wait()
            g1 = v_ref[V_G1:V_G1 + 1, :]
            scale1 = v_ref[V_SCALE1:V_SCALE1 + 1, :]

            def sub(rb, carry):
                rs = pl.ds(pl.multiple_of(rb * SUB, SUB), SUB)
                dh = acc[rs, :]
                r1, xh = _rms_parts(x_ref[rs, :])
                sums_ref[0:1, :] += jnp.sum(dh, axis=0, keepdims=True)
                sums_ref[1:2, :] += jnp.sum(dh * (xh * g1), axis=0, keepdims=True)
                dxn = dh * (1.0 + scale1)
                sums_ref[2:3, :] += jnp.sum(dxn * xh, axis=0, keepdims=True)
                dxh = dxn * g1
                gx_ref[rs, :] = dx1_ref[rs, :] + r1 * (dxh - xh * jnp.mean(dxh * xh, axis=-1, keepdims=True))
                return carry

            lax.fori_loop(0, tm // SUB, sub, 0)

    return pl.pallas_call(
        body, name="inproj_bwd", grid=(s // tm, nk),
        in_specs=[pl.BlockSpec((tm, tk), lambda i, k: (i, k)),
                  pl.BlockSpec((None, D, tk), lambda i, k: (k // per, 0, k % per)),
                  pl.BlockSpec(memory_space=pl.ANY), pl.BlockSpec(memory_space=pl.ANY),
                  pl.BlockSpec((16, D), lambda i, k: (0, 0))],
        out_specs=[pl.BlockSpec((tm, D), lambda i, k: (i, 0)), pl.BlockSpec((8, D), lambda i, k: (0, 0))],
        out_shape=[SDS((s, D), F32), SDS((8, D), F32)],
        scratch_shapes=[pltpu.VMEM((tm, D), F32), pltpu.VMEM((tm, D), F32), pltpu.VMEM((tm, D), F32),
                        pltpu.SemaphoreType.DMA((2,))],
        compiler_params=_cp(2))(dproj, w_in, x, dx1, vecs)


def mod_columns(c16, w_ada, b_cols):
    tn = 512

    def body(c_ref, w_ref, b_ref, o_ref):
        cv = c_ref[...]
        ca = (cv * jax.nn.sigmoid(cv)).astype(BF16)
        o_ref[...] = _dot(ca, w_ref[...].astype(BF16)) + b_ref[...]

    return pl.pallas_call(
        body, name="mod_columns", grid=(ADA_SHARD // tn,),
        in_specs=[pl.BlockSpec((16, D), lambda j: (0, 0)), pl.BlockSpec((D, tn), lambda j: (0, j)),
                  pl.BlockSpec((1, tn), lambda j: (0, j))],
        out_specs=pl.BlockSpec((16, tn), lambda j: (0, j)),
        out_shape=SDS((16, ADA_SHARD), F32), compiler_params=_cp(1))(c16, w_ada, b_cols)


def wada_update(c16, dmod16, w, m, v):
    tm, tn = 512, 512

    def body(c_ref, d_ref, w_ref, m_ref, v_ref, g_out, dl_out, m_out, v_out):
        cv = c_ref[...]
        ca = (cv * jax.nn.sigmoid(cv)).astype(BF16)
        g = _dot_tn(ca, d_ref[...].astype(BF16))
        dl, m2, v2 = _adamw_math(w_ref[...], g, m_ref[...], v_ref[...])
        g_out[...] = g
        dl_out[...] = dl
        m_out[...] = m2
        v_out[...] = v2

    tile = pl.BlockSpec((tm, tn), lambda i, j: (i, j))
    return pl.pallas_call(
        body, name="wada_update", grid=(D // tm, ADA_SHARD // tn),
        in_specs=[pl.BlockSpec((16, tm), lambda i, j: (0, i)), pl.BlockSpec((16, tn), lambda i, j: (0, j)),
                  tile, tile, tile],
        out_specs=[tile] * 4, out_shape=[SDS((D, ADA_SHARD), F32)] * 4,
        compiler_params=_cp(2))(c16, dmod16, w, m, v)


def adamw_big(name, w, mine, theirs, m, v, c_idx):
    r, c = w.shape
    tr = 128
    per = (r // 2) // tr

    def body(c_ref, w_ref, a_ref, b_ref, m_ref, v_ref, g_out, dl_out, m_out, v_out):
        own = (pl.program_id(0) // per) == c_ref[0]
        g = jnp.where(own, a_ref[...], b_ref[...])
        dl, m2, v2 = _adamw_math(w_ref[...], g, m_ref[...], v_ref[...])
        g_out[...] = g
        dl_out[...] = dl
        m_out[...] = m2
        v_out[...] = v2

    tile = pl.BlockSpec((tr, c), lambda i, cr: (i, 0))
    half = pl.BlockSpec((tr, c), lambda i, cr: (i % per, 0))
    gs = pltpu.PrefetchScalarGridSpec(num_scalar_prefetch=1, grid=(r // tr,),
                                      in_specs=[tile, half, half, tile, tile], out_specs=[tile] * 4)
    return pl.pallas_call(body, name=name, grid_spec=gs, out_shape=[SDS((r, c), F32)] * 4,
                          compiler_params=_cp(1))(c_idx, w, mine, theirs, m, v)


def cast_into_slot(name, w, k_idx):
    r, c = w.shape
    tr = 256

    def body(k_ref, w_ref, o_ref):
        o_ref[...] = w_ref[...].astype(BF16)

    gs = pltpu.PrefetchScalarGridSpec(
        num_scalar_prefetch=1, grid=(r // tr,),
        in_specs=[pl.BlockSpec((tr, c), lambda i, kr: (i, 0))],
        out_specs=pl.BlockSpec((None, tr, c), lambda i, kr: (kr[0], i, 0)))
    return pl.pallas_call(body, name=name, grid_spec=gs, out_shape=SDS((N_CHIPS, r, c), BF16),
                          compiler_params=_cp(1))(k_idx, w)


def adamw_small(ws, gs, ms, vs):
    n = len(ws)

    def body(*refs):
        for i in range(n):
            dl, m2, v2 = _adamw_math(refs[i][...], refs[n + i][...], refs[2 * n + i][...], refs[3 * n + i][...])
            refs[4 * n + i][...] = dl
            refs[5 * n + i][...] = m2
            refs[6 * n + i][...] = v2

    vm = pl.BlockSpec(memory_space=pltpu.VMEM)
    shapes = [SDS(w.shape, F32) for w in ws]
    outs = pl.pallas_call(
        body, name="adamw_small", in_specs=[vm] * (4 * n), out_specs=[vm] * (3 * n), out_shape=shapes * 3,
        compiler_params=pltpu.CompilerParams(vmem_limit_bytes=VMEM_LIMIT))(*ws, *gs, *ms, *vs)
    return outs[:n], outs[n:2 * n], outs[2 * n:]


def sum_devices(name, gathered):
    rows = gathered.shape[1]
    tr = min(rows, 128)

    def body(x_ref, o_ref):
        acc = x_ref[0].astype(F32)
        for d in range(1, N_DEV):
            acc = acc + x_ref[d].astype(F32)
        o_ref[...] = acc

    return pl.pallas_call(
        body, name=name, grid=(rows // tr,),
        in_specs=[pl.BlockSpec((N_DEV, tr, D), lambda i: (0, i, 0))],
        out_specs=pl.BlockSpec((tr, D), lambda i: (i, 0)),
        out_shape=SDS((rows, D), F32), compiler_params=_cp(1))(gathered)


def _mesh_pos():
    return lax.axis_index("x"), lax.axis_index("y"), lax.axis_index("c")


def _other_chips(x, y):
    return [(1 - x, y), (x, 1 - y), (1 - x, 1 - y)]


def all_gather_small(name, block):
    m_per, n = block.shape

    def body(x_ref, out_ref, send_sems, recv_sems, local_sem):
        x, y, c = _mesh_pos()
        me, sibling = (x, y, c), (x, y, 1 - c)
        chips = _other_chips(x, y)

        def rows(px, py, pc):
            return out_ref.at[pl.ds((4 * px + 2 * py + pc) * m_per, m_per), :]

        def copy(k, blk, to, src=None):
            return pltpu.make_async_remote_copy(
                src_ref=rows(*blk) if src is None else src, dst_ref=rows(*blk),
                send_sem=send_sems.at[k], recv_sem=recv_sems.at[k], device_id=to, device_id_type=MESH)

        mine = pltpu.make_async_copy(x_ref, rows(*me), local_sem)
        mine.start()
        first = [copy(0, me, sibling, src=x_ref)]
        first += [copy(1 + j, me, (*chip, c), src=x_ref) for j, chip in enumerate(chips)]
        for cp in first:
            cp.start()
        passed = [copy(4 + j, (*chip, c), sibling) for j, chip in enumerate(chips)]
        for j, chip in enumerate(chips):
            copy(1 + j, (*chip, c), me).wait_recv()
            passed[j].start()
        copy(0, sibling, me).wait_recv()
        for j, chip in enumerate(chips):
            copy(4 + j, (*chip, 1 - c), me).wait_recv()
        for cp in first + passed:
            cp.wait_send()
        mine.wait()

    vm = pl.BlockSpec(memory_space=pltpu.VMEM)
    return pl.pallas_call(
        body, name=name, out_shape=SDS((N_DEV * m_per, n), block.dtype), in_specs=[vm], out_specs=vm,
        scratch_shapes=[pltpu.SemaphoreType.DMA((7,)), pltpu.SemaphoreType.DMA((7,)), pltpu.SemaphoreType.DMA],
        compiler_params=pltpu.CompilerParams(vmem_limit_bytes=VMEM_LIMIT))(block)


def sibling_sum(name, grad, other, c_idx):
    _, r, cc = grad.shape
    h = r // 2
    tr = min(256, h)
    g4 = grad.reshape(N_CHIPS, 2, h, cc)

    def body(c_ref, a_ref, b_ref, o_ref):
        o_ref[...] = (a_ref[...].astype(F32) + b_ref[...].astype(F32)).astype(BF16)

    gs = pltpu.PrefetchScalarGridSpec(
        num_scalar_prefetch=1, grid=(N_CHIPS, h // tr),
        in_specs=[pl.BlockSpec((None, None, tr, cc), lambda s, i, cr: (s, cr[0], i, 0)),
                  pl.BlockSpec((None, tr, cc), lambda s, i, cr: (s, i, 0))],
        out_specs=pl.BlockSpec((None, tr, cc), lambda s, i, cr: (s, i, 0)))
    return pl.pallas_call(body, name=name, grid_spec=gs, out_shape=SDS((N_CHIPS, h, cc), BF16),
                          compiler_params=_cp(2))(c_idx, g4, other)


HBM_SPEC = pl.BlockSpec(memory_space=pltpu.HBM)
SEM_SPEC = pl.BlockSpec(memory_space=pltpu.SEMAPHORE)


def _side_effecting():
    return pltpu.CompilerParams(has_side_effects=pltpu.SideEffectType.DATAFLOW_SIDE_EFFECTING)


def _in_hbm(a):
    return pltpu.with_memory_space_constraint(a, pltpu.HBM)


ALL_CHIPS = (0, 1, 2)


def gather_start(name, bufs, after, rel=ALL_CHIPS, carry=None):
    n = len(bufs)
    nr = len(rel)
    halves = [w.shape[1] // 2 for w in bufs]
    extra = [] if carry is None else [carry]

    def body(*refs):
        ins = refs[:n]
        send_sems, recv_sems, token = refs[n + 1 + len(extra)], refs[n + 2 + len(extra)], refs[-1]
        x, y, c = _mesh_pos()
        k = 2 * x + y
        for i in range(n):
            reg = ins[i].at[k, pl.ds(c * halves[i], halves[i]), :]
            for q, j in enumerate(rel):
                chip = _other_chips(x, y)[j]
                pltpu.make_async_remote_copy(src_ref=reg, dst_ref=reg, send_sem=send_sems.at[nr * i + q],
                                             recv_sem=recv_sems.at[nr * i + q], device_id=(*chip, c),
                                             device_id_type=MESH).start()
        token[...] = jnp.zeros_like(token)

    outs = pl.pallas_call(
        body, name=name,
        out_shape=(pltpu.SemaphoreType.DMA((nr * n,)), pltpu.SemaphoreType.DMA((nr * n,)),
                   *[pltpu.HBM(w.shape, w.dtype) for w in list(bufs) + extra], SDS((8, 128), F32)),
        in_specs=[HBM_SPEC] * n + [pl.BlockSpec(memory_space=pl.ANY)] + [HBM_SPEC] * len(extra),
        out_specs=(SEM_SPEC, SEM_SPEC, *[HBM_SPEC] * (n + len(extra)), pl.BlockSpec(memory_space=pltpu.VMEM)),
        input_output_aliases={**{i: 2 + i for i in range(n)}, **({n + 1: 2 + n} if extra else {})},
        compiler_params=_side_effecting())(*[_in_hbm(w) for w in bufs], after, *[_in_hbm(w) for w in extra])
    return (outs[0], outs[1], list(outs[2:2 + n]), outs[-1]) + ((outs[2 + n],) if extra else ())


def gather_wait(name, send_sems, recv_sems, bufs, after, rel=ALL_CHIPS):
    n = len(bufs)
    nr = len(rel)
    halves = [w.shape[1] // 2 for w in bufs]

    def body(*refs):
        ins = refs[:n]
        send_sems, recv_sems = refs[n], refs[n + 1]
        x, y, c = _mesh_pos()
        k = 2 * x + y
        for i in range(n):
            for q, j in enumerate(rel):
                chip = _other_chips(x, y)[j]
                kj = 2 * chip[0] + chip[1]
                cp = pltpu.make_async_remote_copy(
                    src_ref=ins[i].at[k, pl.ds(c * halves[i], halves[i]), :],
                    dst_ref=ins[i].at[kj, pl.ds(c * halves[i], halves[i]), :],
                    send_sem=send_sems.at[nr * i + q], recv_sem=recv_sems.at[nr * i + q], device_id=(*chip, c),
                    device_id_type=MESH)
                cp.wait_send()
                cp.wait_recv()

    return pl.pallas_call(
        body, name=name, out_shape=[pltpu.HBM(w.shape, w.dtype) for w in bufs],
        in_specs=[HBM_SPEC] * n + [SEM_SPEC, SEM_SPEC, pl.BlockSpec(memory_space=pl.ANY)],
        out_specs=[HBM_SPEC] * n, input_output_aliases={i: i for i in range(n)},
        compiler_params=_side_effecting())(*bufs, send_sems, recv_sems, after)


def gather_forward(name, bufs, rel=ALL_CHIPS):
    n = len(bufs)
    halves = [w.shape[1] // 2 for w in bufs]

    def body(*refs):
        outs = refs[n:2 * n]
        send_sems, recv_sems = refs[2 * n:]
        x, y, c = _mesh_pos()
        chips = _other_chips(x, y)

        def copy(i, j, half, to):
            kj = 2 * chips[j][0] + chips[j][1]
            reg = outs[i].at[kj, pl.ds(half * halves[i], halves[i]), :]
            return pltpu.make_async_remote_copy(src_ref=reg, dst_ref=reg, send_sem=send_sems.at[i, j],
                                                recv_sem=recv_sems.at[i, j], device_id=to, device_id_type=MESH)

        cps = [copy(i, j, c, (x, y, 1 - c)) for i in range(n) for j in rel]
        for cp in cps:
            cp.start()
        for i in range(n):
            for j in rel:
                copy(i, j, 1 - c, (x, y, c)).wait_recv()
        for cp in cps:
            cp.wait_send()

    hbm = pl.BlockSpec(memory_space=pl.ANY)
    return pl.pallas_call(
        body, name=name, in_specs=[hbm] * n, out_specs=[hbm] * n,
        out_shape=[SDS(w.shape, w.dtype) for w in bufs], input_output_aliases={i: i for i in range(n)},
        scratch_shapes=[pltpu.SemaphoreType.DMA((n, 3)), pltpu.SemaphoreType.DMA((n, 3))])(*bufs)


def _exchange_plan(kind, srcs, zones):
    x, y, c = _mesh_pos()
    plan = []
    for src, zone in zip(srcs, zones):
        if kind == "chips":
            for j, chip in enumerate(_other_chips(x, y)):
                plan.append((src.at[2 * chip[0] + chip[1]], zone.at[j], (*chip, c)))
        elif kind == "sibling":
            h = zone.shape[1]
            plan.append((src.at[:, pl.ds((1 - c) * h, h), :], zone, (x, y, 1 - c)))
        else:
            peers = [(x, y, 1 - c)] + [(*chip, cc) for chip in _other_chips(x, y) for cc in (c, 1 - c)]
            plan += [(src, zone.at[4 * x + 2 * y + c], peer) for peer in peers]
    return plan


_COPIES_PER_ARRAY = {"chips": 3, "sibling": 1, "all": N_DEV - 1}


def _landing_zones(kind, srcs):
    if kind == "chips":
        return [lax.empty((3,) + t.shape[1:], t.dtype) for t in srcs]
    if kind == "sibling":
        return [lax.empty((t.shape[0], t.shape[1] // 2, t.shape[2]), t.dtype) for t in srcs]
    return [jnp.broadcast_to(t, (N_DEV,) + t.shape) for t in srcs]


def exchange_start(name, kind, srcs, after):
    n = len(srcs)
    lands = _landing_zones(kind, srcs)
    n_copies = n * _COPIES_PER_ARRAY[kind]

    def body(*refs):
        send_sems, recv_sems, token = refs[2 * n + 1], refs[2 * n + 2], refs[-1]
        for q, (src, dst, dev) in enumerate(_exchange_plan(kind, refs[:n], refs[n:2 * n])):
            pltpu.make_async_remote_copy(src_ref=src, dst_ref=dst, send_sem=send_sems.at[q], recv_sem=recv_sems.at[q],
                                         device_id=dev, device_id_type=MESH).start()
        token[...] = jnp.zeros_like(token)

    outs = pl.pallas_call(
        body, name=name,
        out_shape=(pltpu.SemaphoreType.DMA((n_copies,)), pltpu.SemaphoreType.DMA((n_copies,)),
                   *[pltpu.HBM(t.shape, t.dtype) for t in srcs], *[pltpu.HBM(t.shape, t.dtype) for t in lands],
                   SDS((8, 128), F32)),
        in_specs=[HBM_SPEC] * (2 * n) + [pl.BlockSpec(memory_space=pl.ANY)],
        out_specs=(SEM_SPEC, SEM_SPEC, *[HBM_SPEC] * (2 * n), pl.BlockSpec(memory_space=pltpu.VMEM)),
        input_output_aliases={i: 2 + i for i in range(2 * n)},
        compiler_params=_side_effecting())(*[_in_hbm(t) for t in srcs], *[_in_hbm(t) for t in lands], after)
    return outs[0], outs[1], list(outs[2:2 + n]), list(outs[2 + n:2 + 2 * n]), outs[-1]


def exchange_wait(name, kind, send_sems, recv_sems, srcs, lands, after):
    n = len(srcs)

    def body(*refs):
        send_sems, recv_sems = refs[2 * n], refs[2 * n + 1]
        for q, (src, dst, dev) in enumerate(_exchange_plan(kind, refs[:n], refs[n:2 * n])):
            cp = pltpu.make_async_remote_copy(src_ref=src, dst_ref=dst, send_sem=send_sems.at[q],
                                              recv_sem=recv_sems.at[q], device_id=dev, device_id_type=MESH)
            cp.wait_send()
            cp.wait_recv()

    outs = pl.pallas_call(
        body, name=name, out_shape=[pltpu.HBM(t.shape, t.dtype) for t in srcs + lands],
        in_specs=[HBM_SPEC] * (2 * n) + [SEM_SPEC, SEM_SPEC, pl.BlockSpec(memory_space=pl.ANY)],
        out_specs=[HBM_SPEC] * (2 * n), input_output_aliases={i: i for i in range(2 * n)},
        compiler_params=_side_effecting())(*srcs, *lands, send_sems, recv_sems, after)
    return list(outs[:n]), list(outs[n:])


def chip_sum(name, sums, parts, k_idx):
    _, h, cc = parts.shape
    tr = min(256, h)

    def body(k_ref, own_ref, p_ref, o_ref):
        acc = own_ref[...].astype(F32)
        for s in range(3):
            acc = acc + p_ref[s].astype(F32)
        o_ref[...] = acc

    gs = pltpu.PrefetchScalarGridSpec(
        num_scalar_prefetch=1, grid=(h // tr,),
        in_specs=[pl.BlockSpec((None, tr, cc), lambda i, kr: (kr[0], i, 0)),
                  pl.BlockSpec((3, tr, cc), lambda i, kr: (0, i, 0))],
        out_specs=pl.BlockSpec((tr, cc), lambda i, kr: (i, 0)))
    return pl.pallas_call(body, name=name, grid_spec=gs, out_shape=SDS((h, cc), F32),
                          compiler_params=_cp(1))(k_idx, sums, parts)


def halves_exchange(name, halves):
    n = len(halves)

    def body(*refs):
        ins, outs = refs[:n], refs[n:2 * n]
        send_sems, recv_sems = refs[2 * n:]
        x, y, c = _mesh_pos()
        cps = []
        for i in range(n):
            cp = pltpu.make_async_remote_copy(
                src_ref=ins[i], dst_ref=outs[i], send_sem=send_sems.at[i], recv_sem=recv_sems.at[i],
                device_id=(x, y, 1 - c), device_id_type=MESH)
            cp.start()
            cps.append(cp)
        for cp in cps:
            cp.wait_recv()
        for cp in cps:
            cp.wait_send()

    hbm = pl.BlockSpec(memory_space=pl.ANY)
    return pl.pallas_call(
        body, name=name, in_specs=[hbm] * n, out_specs=[hbm] * n,
        out_shape=[SDS(t.shape, F32) for t in halves],
        scratch_shapes=[pltpu.SemaphoreType.DMA((n,)), pltpu.SemaphoreType.DMA((n,))])(*halves)


def local_step(x, tgt, vecs, lvec, wa, wx, sinks, rel_bias, proj, h, w_in, rest_weights, hook):
    buckets = t5_bucket_table()
    band = bias_band(rel_bias.T, buckets).reshape(N_HEADS, BLOCK, 2 * BLOCK)

    ya, rec = lru_fwd(proj, lvec, wa, wx)
    att = attn_fwd(proj, band, sinks)
    w_lru_out, w_attn_out, w_out = rest_weights("mix", att[:8, :128] + ya[:8, :128])
    w_lru_out2, w_attn_out2, w_out2 = w_lru_out.reshape(D, D), w_attn_out.reshape(D, D), w_out.reshape(D, D)
    yab, merged = merge_fwd(ya, att, w_lru_out2, w_attn_out2, proj)
    x1, o1 = outproj_fwd(merged, w_out2, x, vecs)
    w_ff1, w_ff2 = rest_weights("ff", o1[:8, :128])
    w_ff2_2 = w_ff2.reshape(D_FF, D)
    f, h2 = ff1_fwd(x1, vecs, w_ff1)
    dx2, do2, sums_f, loss = ff2_loss(f, w_ff2_2, x1, tgt, vecs)

    df = ff2_bwd(do2, w_ff2_2, f)
    g_ff2 = weight_grad("dw_ff2", f, do2, 512, (D_FF, D), (WG_TM, 512), lambda i, j: (i, j), relu2=True)
    dx1, do1, sums_2 = ff1_bwd(df, w_ff1, x1, dx2, o1, vecs)
    g_ff1 = weight_grad("dw_ff1", h2, df, 512, (N_CHIPS, D, D), (None, WG_TM, 512), lambda i, j: (j // 4, i, j % 4))
    dyab, dproj = outproj_bwd(do1, w_out2, yab, proj)
    g_out = weight_grad("dw_out", merged, do1, 512, (D, D), (WG_TM, 512), lambda i, j: (i, j))
    drec, dproj = lruout_bwd(dyab, w_lru_out2, rec, proj, dproj)
    dyab2 = dyab.reshape(2 * x.shape[0], D)
    g_lru_out = weight_grad("dw_lru_out", ya, dyab2, 512, (D, D), (WG_TM, 512), lambda i, j: (i, j))
    datt = attnout_bwd(dyab, w_attn_out2)
    g_attn_out = weight_grad("dw_attn_out", att, dyab2, 512, (D, D), (WG_TM, 512), lambda i, j: (i, j), b_part=1)
    zero = hook("grads_a", [g_lru_out.reshape(N_CHIPS, D // 4, D), g_attn_out.reshape(N_CHIPS, D // 4, D),
                            g_out.reshape(N_CHIPS, D // 4, D), g_ff1, g_ff2.reshape(N_CHIPS, D_FF // 4, D)])
    dproj, dkb, dvb, dband, dsink = attn_bwd(proj, band, sinks + zero, datt, dproj)
    zero = hook("after_attn_bwd", dkb)
    dproj = dkv_combine(dkb, dvb, dproj)
    dproj, sums_l, d_wa, d_wx = lru_bwd(proj, rec, drec, lvec + zero, wa, wx, dproj)
    hook("lru_grads", (d_wa, d_wx))
    per = IN_SHARD // IN_TILE
    g_in = weight_grad("dw_in", h, dproj, IN_TILE, (N_CHIPS, D, IN_SHARD), (None, WG_TM, IN_TILE),
                       lambda i, j: (j // per, i, j % per))
    zero = hook("grads_b", [g_in])
    grad_x, sums_1 = inproj_bwd(dproj, w_in, x, dx1, vecs + zero)
    d_rel_bias = bias_band_bwd(dband.reshape(N_HEADS, BLOCK * 2 * BLOCK), buckets)

    small = dict(sums_f=sums_f, sums_2=sums_2, sums_1=sums_1, sums_l=sums_l, d_wa=d_wa, d_wx=d_wx,
                 d_sinks=dsink[:, 0], d_rel_bias=d_rel_bias)
    return loss, grad_x, small


def _pad_rows(a, rows):
    return jnp.concatenate([a, jnp.zeros((rows - a.shape[0], a.shape[1]), a.dtype)], axis=0)


def kernel(x, c, w_ada, b_ada, norm1_g, w_in, conv_w, conv_b, lru_wa, lru_ba, lru_wx, lru_bx, lru_lambda, w_lru_out, w_attn_out, attn_sinks, rel_bias, w_out, norm2_g, w_ff1, w_ff2, final_g, loss_target, m_w_ada, m_b_ada, m_norm1_g, m_w_in, m_conv_w, m_conv_b, m_lru_wa, m_lru_ba, m_lru_wx, m_lru_bx, m_lru_lambda, m_w_lru_out, m_w_attn_out, m_attn_sinks, m_rel_bias, m_w_out, m_norm2_g, m_w_ff1, m_w_ff2, m_final_g, v_w_ada, v_b_ada, v_norm1_g, v_w_in, v_conv_w, v_conv_b, v_lru_wa, v_lru_ba, v_lru_wx, v_lru_bx, v_lru_lambda, v_w_lru_out, v_w_attn_out, v_attn_sinks, v_rel_bias, v_w_out, v_norm2_g, v_w_ff1, v_w_ff2, v_final_g):
    xi, yi, ci = _mesh_pos()
    chip = 2 * xi + yi
    dev = 2 * chip + ci
    z8 = jnp.zeros((8, D), F32)

    conv_rows = jnp.concatenate([conv_w[0], jnp.zeros((4, D - D // 4), F32)], axis=1)
    pack0 = jnp.concatenate([c, conv_rows, jnp.zeros((3, D), F32)], axis=0)
    g0 = all_gather_small("gather_cond", pack0).reshape(N_DEV, 8, D)
    c_all = g0[:, 0, :]
    conv_full = jnp.concatenate([g0[2 * k, 1:5, :D // 4] for k in range(N_CHIPS)], axis=1)
    c16 = jnp.concatenate([c_all, z8], axis=0)
    b_cols = lax.dynamic_slice_in_dim(b_ada, chip * ADA_SHARD, ADA_SHARD, axis=1)
    mod_c = mod_columns(c16, w_ada[0], b_cols)
    g1 = all_gather_small("gather_mod", mod_c).reshape(N_DEV, 16, ADA_SHARD)
    mod = jnp.concatenate([lax.dynamic_index_in_dim(g1[2 * k], dev, axis=0, keepdims=False) for k in range(N_CHIPS)])
    shift1, scale1, gate1, shift2, scale2, gate2 = [mod[i * D:(i + 1) * D] for i in range(6)]
    vecs = jnp.stack([norm1_g[0], scale1, shift1, gate1, norm2_g[0], scale2, shift2, gate2, final_g]
                     + [jnp.zeros((D,), F32)] * 7)
    lvec = jnp.concatenate([conv_full, conv_b, lru_ba, lru_bx, lru_lambda], axis=0)

    shards = [w_in[0], w_lru_out[0], w_attn_out[0], w_out[0], w_ff1[0], w_ff2[0]]
    names = ["w_in", "w_lru_out", "w_attn_out", "w_out", "w_ff1", "w_ff2"]
    k_idx = jnp.reshape(chip, (1,)).astype(jnp.int32)
    c_idx = jnp.reshape(ci, (1,)).astype(jnp.int32)
    near, far = (0, 1), (2,)
    shard_of = lambda flip: jnp.reshape(chip ^ flip, (1,)).astype(jnp.int32)
    x2d = x[0]
    n_send, n_recv, w_in_buf, _ = gather_start(
        "gather_start_in_near", [cast_into_slot("cast_w_in", shards[0], k_idx)], vecs, near)
    proj, h = inproj_fwd("inproj_fwd_own", x2d, None, vecs, w_in_buf[0], None, k_idx)
    slots = [cast_into_slot("cast_" + nm, w, k_idx) for nm, w in zip(names[1:], shards[1:])]
    w_in_buf = gather_forward("gather_forward_in_near", gather_wait(
        "gather_wait_in_near", n_send, n_recv, w_in_buf, proj[:8, :128] + slots[-1][0, :8, :128], near), near)
    f_send, f_recv, w_in_buf, _ = gather_start("gather_start_in_far", w_in_buf, proj[:8, :128], far)
    in_flight = {"mix": gather_start("gather_start_mix", slots[:3], proj[:8, :128])}
    in_flight["ff"] = gather_start("gather_start_ff", slots[3:], in_flight["mix"][3], carry=w_in_buf[0])
    w_in_buf = [in_flight["ff"][4]]
    proj = inproj_fwd("inproj_fwd_x", x2d, h, vecs, w_in_buf[0], proj, shard_of(2))
    proj = inproj_fwd("inproj_fwd_y", x2d, h, vecs, w_in_buf[0], proj, shard_of(1))
    w_in_buf = gather_forward("gather_forward_in_far", gather_wait(
        "gather_wait_in_far", f_send, f_recv, w_in_buf, proj[:8, :128], far), far)
    proj = inproj_fwd("inproj_fwd_d", x2d, h, vecs, w_in_buf[0], proj, shard_of(3))
    w_in_full = w_in_buf[0]
    pending = {}

    def rest_weights(group, after):
        send_sems, recv_sems, bufs = in_flight[group][:3]
        return gather_forward("gather_forward_" + group,
                              gather_wait("gather_wait_" + group, send_sems, recv_sems, bufs, after))

    def reduce_hook(event, payload):
        if event == "grads_a":
            pending["sib_a"] = exchange_start("sibling_start_a", "sibling", payload, payload[0])
            return pending["sib_a"][-1][0, 0]
        if event == "lru_grads":
            pack_w = jnp.concatenate([payload[0].reshape(128, D), payload[1].reshape(128, D)], axis=0).astype(BF16)
            pending["lru_w"] = exchange_start("lru_w_grads_start", "all", [pack_w], pack_w)
            return pending["lru_w"][-1][0, 0]
        if event == "grads_b":
            pending["sib_b"] = exchange_start("sibling_start_b", "sibling", payload, pending["lru_w"][-1])
            return pending["sib_b"][-1][0, 0]
        return chips_start("a", names[1:], payload)

    def chips_start(tag, nms, after):
        send_sems, recv_sems, grads, lands, _ = pending["sib_" + tag]
        grads, lands = exchange_wait("sibling_wait_" + tag, "sibling", send_sems, recv_sems, grads, lands, after)
        sums = [sibling_sum("sibling_sum_" + nm, g, o, c_idx) for nm, g, o in zip(nms, grads, lands)]
        pending[tag] = exchange_start("exchange_start_" + tag, "chips", sums, sums[0])
        return pending[tag][-1][0, 0]

    loss_t, grad_x, small = local_step(
        x2d, loss_target[0], vecs, lvec, lru_wa[0].astype(BF16), lru_wx[0].astype(BF16),
        attn_sinks[0], rel_bias, proj, h, w_in_full, rest_weights, reduce_hook)
    chips_start("b", names[:1], grad_x)

    big_m = dict(zip(names, [m_w_in, m_w_lru_out, m_w_attn_out, m_w_out, m_w_ff1, m_w_ff2]))
    big_v = dict(zip(names, [v_w_in, v_w_lru_out, v_w_attn_out, v_w_out, v_w_ff1, v_w_ff2]))
    local_w = dict(zip(names, shards))
    g_big, d_big, nm_big, nv_big = {}, {}, {}, {}

    def links_done(tag, after):
        send_sems, recv_sems, sums, lands, _ = pending[tag]
        return exchange_wait("exchange_wait_" + tag, "chips", send_sems, recv_sems, sums, lands, after)

    def finish_reduce(nms, sums, lands):
        mine = [chip_sum("chip_sum_" + nm, t, p, k_idx) for nm, t, p in zip(nms, sums, lands)]
        theirs = halves_exchange("halves_exchange_" + nms[0], mine)
        for nm, a, b in zip(nms, mine, theirs):
            g2, dl, m2, v2 = adamw_big("adamw_" + nm, local_w[nm], a, b, big_m[nm][0], big_v[nm][0], c_idx)
            g_big[nm], d_big[nm], nm_big[nm], nv_big[nm] = g2[None], dl[None], m2[None], v2[None]
        return lax.optimization_barrier(tuple(nv_big[nm] for nm in nms))[0]

    done_a = finish_reduce(names[1:], *links_done("a", pending["b"][-1]))
    sums_b, lands_b = links_done("b", done_a)
    w_send, w_recv, w_src, w_lands, _ = pending["lru_w"]
    w_src, w_lands = exchange_wait("lru_w_grads_wait", "all", w_send, w_recv, w_src, w_lands, lands_b[0])

    sums_f, sums_2, sums_1, sums_l = small["sums_f"], small["sums_2"], small["sums_1"], small["sums_l"]
    vec_rows = jnp.stack([sums_1[2], sums_2[2], sums_f[0], sums_l[L_CB], sums_l[L_BA], sums_l[L_BX],
                          sums_l[L_LAM], jnp.zeros((D,), F32)])
    mod_rows = jnp.stack([sums_1[0], sums_1[1], sums_2[3], sums_2[0], sums_2[1], sums_f[1],
                          jnp.zeros((D,), F32), jnp.zeros((D,), F32)])
    att_rows = jnp.concatenate([
        jnp.concatenate([small["d_sinks"], jnp.zeros((D - N_HEADS,), F32)])[None],
        jnp.concatenate([small["d_rel_bias"].reshape(-1), jnp.zeros((D - N_BUCKETS * N_HEADS,), F32)])[None],
        jnp.zeros((6, D), F32)], axis=0)
    pack = jnp.concatenate([vec_rows, _pad_rows(sums_l[0:4], 8), mod_rows, att_rows], axis=0)
    pack, lru_w_all = lax.optimization_barrier((pack, w_lands[0]))
    gathered = all_gather_small("gather_small_grads", pack).reshape(N_DEV, P_WA, D)
    total = sum_devices("sum_small_grads", gathered)
    total_w = sum_devices("sum_lru_w_grads", lru_w_all)
    dmod_all = gathered[:, P_MOD:P_MOD + 6, :].reshape(N_DEV, 6 * D)
    dmod16 = jnp.concatenate([lax.dynamic_slice_in_dim(dmod_all, chip * ADA_SHARD, ADA_SHARD, axis=1),
                              jnp.zeros((8, ADA_SHARD), F32)], axis=0)
    g_w_ada, d_w_ada, nm_w_ada, nv_w_ada = wada_update(c16, dmod16, w_ada[0], m_w_ada[0], v_w_ada[0])
    finish_reduce(names[:1], sums_b, lands_b)
    loss = lax.psum(lax.optimization_barrier((loss_t, total))[0][0, 0], ("x", "y", "c"))

    conv_g = lax.dynamic_slice_in_dim(total[P_CONVW:P_CONVW + 4], chip * (D // 4), D // 4, axis=1)
    sm_names = ["b_ada", "norm1_g", "conv_w", "conv_b", "lru_wa", "lru_ba", "lru_wx", "lru_bx", "lru_lambda",
                "attn_sinks", "rel_bias", "norm2_g", "final_g"]
    sm_w = [b_ada.reshape(6, D), norm1_g, conv_w[0], conv_b, lru_wa.reshape(128, D), lru_ba, lru_wx.reshape(128, D),
            lru_bx, lru_lambda, attn_sinks, rel_bias, norm2_g, final_g[None]]
    sm_m = [m_b_ada.reshape(6, D), m_norm1_g, m_conv_w[0], m_conv_b, m_lru_wa.reshape(128, D), m_lru_ba,
            m_lru_wx.reshape(128, D), m_lru_bx, m_lru_lambda, m_attn_sinks, m_rel_bias, m_norm2_g, m_final_g[None]]
    sm_v = [v_b_ada.reshape(6, D), v_norm1_g, v_conv_w[0], v_conv_b, v_lru_wa.reshape(128, D), v_lru_ba,
            v_lru_wx.reshape(128, D), v_lru_bx, v_lru_lambda, v_attn_sinks, v_rel_bias, v_norm2_g, v_final_g[None]]
    sm_g = [total[P_MOD:P_MOD + 6], total[0:1], conv_g, total[3:4], total_w[0:128], total[4:5],
            total_w[128:256], total[5:6], total[6:7], total[P_ATT:P_ATT + 1, :N_HEADS],
            total[P_ATT + 1, :N_BUCKETS * N_HEADS].reshape(N_BUCKETS, N_HEADS), total[1:2], total[2:3]]
    sm_d, sm_nm, sm_nv = adamw_small(sm_w, sm_g, sm_m, sm_v)
    shapes = dict(b_ada=b_ada.shape, norm1_g=norm1_g.shape, conv_w=conv_w.shape, conv_b=conv_b.shape,
                  lru_wa=lru_wa.shape, lru_ba=lru_ba.shape, lru_wx=lru_wx.shape, lru_bx=lru_bx.shape,
                  lru_lambda=lru_lambda.shape, attn_sinks=attn_sinks.shape, rel_bias=rel_bias.shape,
                  norm2_g=norm2_g.shape, final_g=final_g.shape)
    grads = dict(w_ada=g_w_ada[None], **g_big)
    deltas = dict(w_ada=d_w_ada[None], **d_big)
    new_m = dict(w_ada=nm_w_ada[None], **nm_big)
    new_v = dict(w_ada=nv_w_ada[None], **nv_big)
    for i, nm in enumerate(sm_names):
        grads[nm] = sm_g[i].reshape(shapes[nm])
        deltas[nm] = sm_d[i].reshape(shapes[nm])
        new_m[nm] = sm_nm[i].reshape(shapes[nm])
        new_v[nm] = sm_nv[i].reshape(shapes[nm])
    order = ["w_ada", "b_ada", "norm1_g", "w_in", "conv_w", "conv_b", "lru_wa", "lru_ba", "lru_wx", "lru_bx",
             "lru_lambda", "w_lru_out", "w_attn_out", "attn_sinks", "rel_bias", "w_out", "norm2_g", "w_ff1", "w_ff2",
             "final_g"]
    return (loss, grad_x[None], *[grads[n] for n in order], *[deltas[n] for n in order],
            *[new_m[n] for n in order], *[new_v[n] for n in order])
```

```python
import math

import numpy as np
import jax
import jax.numpy as jnp
from jax import lax
from jax.experimental import pallas as pl
from jax.experimental.pallas import tpu as pltpu

F32 = jnp.float32
BF16 = jnp.bfloat16
SDS = jax.ShapeDtypeStruct
MESH = pl.DeviceIdType.MESH

D = 2048
D_FF = 4 * D
N_HEADS = 32
HEAD_DIM = 64
BLOCK = 128
N_LRU_BLOCKS = 16
LRU_C = 8.0
EPS = 1e-6
NEG_INF = -1e30
N_BUCKETS = 32
MAX_DISTANCE = 128
IN_W = 10752
IN_SHARD = IN_W // 4
IN_TILE = 896
ADA_SHARD = 6 * D // 4
OFF_LRU, OFF_GATE, OFF_Q, OFF_K, OFF_V, OFF_GA, OFF_GB = 0, 2048, 4096, 6144, 6400, 6656, 8704
SCALE = HEAD_DIM ** -0.5
N_CHIPS = 4
N_DEV = 8

ADAM_LR, ADAM_B1, ADAM_B2, ADAM_EPS, ADAM_WD, ADAM_STEP = 0.001, 0.9, 0.999, 1e-08, 0.01, 10
ADAM_C1 = 1.0 - ADAM_B1 ** ADAM_STEP
ADAM_C2 = 1.0 - ADAM_B2 ** ADAM_STEP

VMEM_LIMIT = 52 * 2 ** 20
SUB = 128
WG_TM = 1024
V_G1, V_SCALE1, V_SHIFT1, V_GATE1, V_G2, V_SCALE2, V_SHIFT2, V_GATE2, V_G3 = range(9)
L_CW0, L_CB, L_BA, L_BX, L_LAM = 0, 4, 5, 6, 7
P_VEC, P_CONVW, P_MOD, P_ATT, P_WA = 0, 8, 16, 24, 32


def _cp(n_axes):
    return pltpu.CompilerParams(dimension_semantics=("arbitrary",) * n_axes, vmem_limit_bytes=VMEM_LIMIT)


def _dot(a, b):
    return jnp.dot(a, b, preferred_element_type=F32)


def _dot_nt(a, b):
    return lax.dot_general(a, b, (((1,), (1,)), ((), ())), preferred_element_type=F32)


def _dot_tn(a, b):
    return lax.dot_general(a, b, (((0,), (0,)), ((), ())), preferred_element_type=F32)


_G0 = math.sqrt(2.0 / math.pi)
_G1 = 0.044715


def _gelu(x):
    return 0.5 * x * (1.0 + jnp.tanh(_G0 * (x + _G1 * x * x * x)))


def _gelu_grad(x):
    x2 = x * x
    t = jnp.tanh(_G0 * (x + _G1 * x * x2))
    return 0.5 * (1.0 + t) + 0.5 * x * (1.0 - t * t) * _G0 * (1.0 + 3.0 * _G1 * x2)


def _sigmoid(x):
    return 0.5 * jnp.tanh(0.5 * x) + 0.5


def _one_minus_exp2(x):
    t = jnp.tanh(x)
    return (-2.0 * t) / (1.0 - t)


def _softplus(z):
    e = jnp.exp(-jnp.abs(z))
    u = 1.0 + e
    l1p = jnp.where(u == 1.0, e, jnp.log(u) * e / (u - 1.0))
    return jnp.maximum(z, 0.0) + l1p


def _adamw_math(w, g, m, v):
    m2 = ADAM_B1 * m + (1.0 - ADAM_B1) * g
    v2 = ADAM_B2 * v + (1.0 - ADAM_B2) * (g * g)
    m_hat = m2 / ADAM_C1
    v_hat = v2 / ADAM_C2
    delta = -ADAM_LR * (m_hat / (jnp.sqrt(v_hat) + ADAM_EPS) + ADAM_WD * w)
    return delta, m2, v2


def _rms_parts(xv):
    r = lax.rsqrt(jnp.mean(xv * xv, axis=-1, keepdims=True) + EPS)
    return r, xv * r


def _row_fetches(hbm_refs, bufs, sems, i, rows):
    return [pltpu.make_async_copy(h.at[pl.ds(i * rows, rows), :], b, sems.at[n])
            for n, (h, b) in enumerate(zip(hbm_refs, bufs))]


def _modulated_norm(x_ref, v_ref, row_g, row_scale, row_shift, h_ref, rows):
    g, scale, shift = v_ref[row_g:row_g + 1, :], v_ref[row_scale:row_scale + 1, :], v_ref[row_shift:row_shift + 1, :]

    def sub(rb, carry):
        rs = pl.ds(pl.multiple_of(rb * SUB, SUB), SUB)
        _, xh = _rms_parts(x_ref[rs, :])
        h_ref[rs, :] = ((xh * g) * (1.0 + scale) + shift).astype(BF16)
        return carry

    lax.fori_loop(0, rows // SUB, sub, 0)


def inproj_fwd(name, x, h, vecs, w_in, proj, shard):
    s = x.shape[0]
    tm = min(1024, s)
    per = IN_SHARD // IN_TILE
    first = h is None

    def body(*refs):
        if first:
            _, x_ref, v_ref, w_ref, proj_ref, h_ref = refs

            @pl.when(pl.program_id(1) == 0)
            def _():
                _modulated_norm(x_ref, v_ref, V_G1, V_SCALE1, V_SHIFT1, h_ref, tm)
        else:
            _, h_ref, w_ref, _, proj_ref = refs
        proj_ref[...] = _dot(h_ref[...], w_ref[...]).astype(BF16)

    rows = pl.BlockSpec((tm, D), lambda i, j, sr: (i, 0))
    w_spec = pl.BlockSpec((None, D, IN_TILE), lambda i, j, sr: (sr[0], 0, j))
    proj_spec = pl.BlockSpec((tm, IN_TILE), lambda i, j, sr: (i, sr[0] * per + j))
    if first:
        gs = pltpu.PrefetchScalarGridSpec(
            num_scalar_prefetch=1, grid=(s // tm, per),
            in_specs=[rows, pl.BlockSpec((16, D), lambda i, j, sr: (0, 0)), w_spec], out_specs=[proj_spec, rows])
        return pl.pallas_call(body, name=name, grid_spec=gs, out_shape=[SDS((s, IN_W), BF16), SDS((s, D), BF16)],
                              compiler_params=_cp(2))(shard, x, vecs, w_in)
    gs = pltpu.PrefetchScalarGridSpec(
        num_scalar_prefetch=1, grid=(s // tm, per),
        in_specs=[rows, w_spec, pl.BlockSpec(memory_space=pl.ANY)], out_specs=proj_spec)
    return pl.pallas_call(body, name=name, grid_spec=gs, out_shape=SDS((s, IN_W), BF16),
                          input_output_aliases={3: 0}, compiler_params=_cp(2))(shard, h, w_in, proj)


def _lru_block_fwd(xbuf, lv_ref, wa_ref, wx_ref, b, t, first):
    cs = slice(b * 128, (b + 1) * 128)
    x0 = xbuf[pl.ds(8, t), cs]
    x1 = xbuf[pl.ds(7, t), cs]
    x2 = xbuf[pl.ds(6, t), cs]
    x3 = xbuf[pl.ds(5, t), cs]
    xc = (lv_ref[L_CB:L_CB + 1, cs] + lv_ref[3:4, cs] * x0 + lv_ref[2:3, cs] * x1
          + lv_ref[1:2, cs] * x2 + lv_ref[0:1, cs] * x3)
    xcb = xc.astype(BF16)
    r = _sigmoid(_dot(xcb, wa_ref[b]) + lv_ref[L_BA:L_BA + 1, cs])
    ig = _sigmoid(_dot(xcb, wx_ref[b]) + lv_ref[L_BX:L_BX + 1, cs])
    sp = _softplus(-lv_ref[L_LAM:L_LAM + 1, cs])
    log_a = (-LRU_C) * r * sp
    a = jnp.exp(log_a)
    mult = jnp.where(first, 1.0, jnp.sqrt(_one_minus_exp2(log_a)))
    return (x0, x1, x2, x3), xc, xcb, r, ig, sp, a, mult


def lru_fwd(proj, lvec, wa, wx):
    s = proj.shape[0]
    t = min(256, s)

    def body(lx_ref, gate_ref, lv_ref, wa_ref, wx_ref, ya_ref, rec_ref, xbuf, a_s, u_s, hc):
        i = pl.program_id(0)

        @pl.when(i == 0)
        def _():
            xbuf[pl.ds(0, 8), :] = jnp.zeros((8, D), F32)
            hc[...] = jnp.zeros((8, D), F32)

        @pl.when(i > 0)
        def _():
            xbuf[pl.ds(0, 8), :] = xbuf[pl.ds(t, 8), :]

        xbuf[pl.ds(8, t), :] = lx_ref[...].astype(F32)
        first = (lax.broadcasted_iota(jnp.int32, (t, 128), 0) + i * t) == 0
        for b in range(N_LRU_BLOCKS):
            cs = slice(b * 128, (b + 1) * 128)
            _, xc, _, _, ig, _, a, mult = _lru_block_fwd(xbuf, lv_ref, wa_ref, wx_ref, b, t, first)
            a_s[:, cs] = a
            u_s[:, cs] = mult * (ig * xc)

        def step(tt, h):
            h = a_s[pl.ds(tt, 1), :] * h + u_s[pl.ds(tt, 1), :]
            rec_ref[pl.ds(tt, 1), :] = h
            return h

        hc[0:1, :] = lax.fori_loop(0, t, step, hc[0:1, :], unroll=8)
        for b in range(N_LRU_BLOCKS):
            cs = slice(b * 128, (b + 1) * 128)
            ya_ref[:, cs] = (rec_ref[:, cs] * _gelu(gate_ref[:, cs].astype(F32))).astype(BF16)

    return pl.pallas_call(
        body, name="lru_fwd", grid=(s // t,),
        in_specs=[pl.BlockSpec((t, D), lambda i: (i, OFF_LRU // D)),
                  pl.BlockSpec((t, D), lambda i: (i, OFF_GATE // D)),
                  pl.BlockSpec((8, D), lambda i: (0, 0)),
                  pl.BlockSpec((N_LRU_BLOCKS, 128, 128), lambda i: (0, 0, 0)),
                  pl.BlockSpec((N_LRU_BLOCKS, 128, 128), lambda i: (0, 0, 0))],
        out_specs=[pl.BlockSpec((t, D), lambda i: (i, 0)), pl.BlockSpec((t, D), lambda i: (i, 0))],
        out_shape=[SDS((s, D), BF16), SDS((s, D), F32)],
        scratch_shapes=[pltpu.VMEM((t + 8, D), F32), pltpu.VMEM((t, D), F32), pltpu.VMEM((t, D), F32),
                        pltpu.VMEM((8, D), F32)],
        compiler_params=_cp(1))(proj, proj, lvec, wa, wx)


def t5_bucket_table():
    qi = np.arange(BLOCK)[:, None]
    ki = np.arange(2 * BLOCK)[None, :]
    rel = qi + BLOCK - ki
    relc = np.maximum(rel, 0)
    max_exact = N_BUCKETS // 2
    relf = np.maximum(relc, 1).astype(np.float32)
    large = max_exact + (np.log(relf / np.float32(max_exact)) / np.float32(math.log(MAX_DISTANCE / max_exact))
                         * np.float32(N_BUCKETS - max_exact)).astype(np.int32)
    large = np.minimum(large, N_BUCKETS - 1)
    bucket = np.where(relc < max_exact, relc, large)
    bucket = np.where((rel >= 0) & (rel < BLOCK), bucket, -1)
    return jnp.asarray(bucket.reshape(1, BLOCK * 2 * BLOCK), jnp.int32)


def bias_band(rel_bias_t, buckets):
    n = BLOCK * 2 * BLOCK
    tn = 4096

    def body(bk_ref, rb_ref, o_ref):
        row = lax.broadcasted_iota(jnp.int32, (N_BUCKETS, tn), 0)
        oh = jnp.where(row == bk_ref[...], 1.0, 0.0).astype(BF16)
        rb = rb_ref[...]
        p0 = rb.astype(BF16)
        r1 = rb - p0.astype(F32)
        p1 = r1.astype(BF16)
        p2 = (r1 - p1.astype(F32)).astype(BF16)
        o_ref[...] = _dot(p0, oh) + _dot(p1, oh) + _dot(p2, oh)

    return pl.pallas_call(
        body, name="bias_band", grid=(n // tn,),
        in_specs=[pl.BlockSpec((1, tn), lambda i: (0, i)), pl.BlockSpec((N_HEADS, N_BUCKETS), lambda i: (0, 0))],
        out_specs=pl.BlockSpec((N_HEADS, tn), lambda i: (0, i)),
        out_shape=SDS((N_HEADS, n), F32), compiler_params=_cp(1))(buckets, rel_bias_t)


def bias_band_bwd(dband, buckets):
    n = BLOCK * 2 * BLOCK
    tn = 4096

    def body(bk_ref, d_ref, o_ref):
        @pl.when(pl.program_id(0) == 0)
        def _():
            o_ref[...] = jnp.zeros_like(o_ref)
        row = lax.broadcasted_iota(jnp.int32, (N_BUCKETS, tn), 0)
        oh = jnp.where(row == bk_ref[...], 1.0, 0.0).astype(BF16)
        dv = d_ref[...]
        p0 = dv.astype(BF16)
        r1 = dv - p0.astype(F32)
        p1 = r1.astype(BF16)
        p2 = (r1 - p1.astype(F32)).astype(BF16)
        o_ref[...] += _dot_nt(oh, p0) + _dot_nt(oh, p1) + _dot_nt(oh, p2)

    return pl.pallas_call(
        body, name="bias_band_bwd", grid=(n // tn,),
        in_specs=[pl.BlockSpec((1, tn), lambda i: (0, i)), pl.BlockSpec((N_HEADS, tn), lambda i: (0, i))],
        out_specs=pl.BlockSpec((N_BUCKETS, N_HEADS), lambda i: (0, 0)),
        out_shape=SDS((N_BUCKETS, N_HEADS), F32), compiler_params=_cp(1))(buckets, dband)


def _dup_half(band, which):
    lane = lax.broadcasted_iota(jnp.int32, band.shape, 1)
    rolled = pltpu.roll(band, 64, 1)
    keep = (lane < 64) if which == 0 else (lane >= 64)
    return jnp.where(keep, band, rolled)


def _attn_probs(scores, bias, sink, valid):
    sc = jnp.where(valid, scores * SCALE + bias, NEG_INF)
    m = jnp.maximum(jnp.max(sc, axis=-1, keepdims=True), sink)
    e = jnp.exp(sc - m)
    es = jnp.exp(sink - m)
    inv = 1.0 / (jnp.sum(e, axis=-1, keepdims=True) + es)
    return e * inv, es * inv


def _stack_heads(src_ref, kv, dst):
    lane = lax.broadcasted_iota(jnp.int32, (BLOCK, 128), 1)
    for jj in range(4):
        slab = src_ref[:, (4 * kv + jj) * 128:(4 * kv + jj + 1) * 128]
        for hh in range(2):
            keep = (lane < 64) if hh == 0 else (lane >= 64)
            dst[pl.ds((2 * jj + hh) * BLOCK, BLOCK), :] = jnp.where(keep, slab, jnp.zeros_like(slab))


def _unstack_heads(stacked, dst_ref, kv):
    lane = lax.broadcasted_iota(jnp.int32, (BLOCK, 128), 1)
    for jj in range(4):
        lo = stacked[(2 * jj) * BLOCK:(2 * jj + 1) * BLOCK]
        hi = stacked[(2 * jj + 1) * BLOCK:(2 * jj + 2) * BLOCK]
        dst_ref[:, (4 * kv + jj) * 128:(4 * kv + jj + 1) * 128] = jnp.where(lane < 64, lo, hi).astype(dst_ref.dtype)


def _band_valid(n):
    qi = lax.broadcasted_iota(jnp.int32, (BLOCK, 2 * BLOCK), 0)
    ki = lax.broadcasted_iota(jnp.int32, (BLOCK, 2 * BLOCK), 1)
    rel = qi + BLOCK - ki
    return (rel >= 0) & (rel < BLOCK) & ((ki >= BLOCK) | (n > 0))


def _kv_bands(prev_ref, cur_ref):
    band = jnp.concatenate([prev_ref[...].astype(F32), cur_ref[...].astype(F32)], axis=0)
    return [_dup_half(band, 0).astype(BF16), _dup_half(band, 1).astype(BF16)]


def attn_fwd(proj, band, sinks):
    s = proj.shape[0]
    nb = s // BLOCK
    qw = 1024

    def body(sk_ref, q_ref, kp_ref, kc_ref, vp_ref, vc_ref, b_ref, o_ref, qs_buf, s_buf, p_buf):
        n = pl.program_id(0)
        gp = pl.program_id(1)
        valid = _band_valid(n)
        kks = _kv_bands(kp_ref, kc_ref)
        vvs = _kv_bands(vp_ref, vc_ref)
        for kv in range(2):
            _stack_heads(q_ref, kv, qs_buf)
            s_buf[...] = _dot_nt(qs_buf[...], kks[kv])
            for hq in range(8):
                hl = 8 * kv + hq
                rows = pl.ds(hq * BLOCK, BLOCK)
                p, _ = _attn_probs(s_buf[rows, :], b_ref[hl], sk_ref[gp * 16 + hl], valid)
                p_buf[rows, :] = p.astype(BF16)
            _unstack_heads(_dot(p_buf[...], vvs[kv]), o_ref, kv)

    kb, vb = OFF_K // 128, OFF_V // 128
    return pl.pallas_call(
        body, name="attn_fwd", grid=(nb, 2),
        in_specs=[pl.BlockSpec(memory_space=pltpu.SMEM),
                  pl.BlockSpec((BLOCK, qw), lambda n, g: (n, OFF_Q // qw + g)),
                  pl.BlockSpec((BLOCK, 128), lambda n, g: (jnp.maximum(n - 1, 0), kb + g)),
                  pl.BlockSpec((BLOCK, 128), lambda n, g: (n, kb + g)),
                  pl.BlockSpec((BLOCK, 128), lambda n, g: (jnp.maximum(n - 1, 0), vb + g)),
                  pl.BlockSpec((BLOCK, 128), lambda n, g: (n, vb + g)),
                  pl.BlockSpec((16, BLOCK, 2 * BLOCK), lambda n, g: (g, 0, 0))],
        out_specs=pl.BlockSpec((BLOCK, qw), lambda n, g: (n, g)),
        out_shape=SDS((s, D), BF16),
        scratch_shapes=[pltpu.VMEM((8 * BLOCK, 128), BF16), pltpu.VMEM((8 * BLOCK, 2 * BLOCK), F32),
                        pltpu.VMEM((8 * BLOCK, 2 * BLOCK), BF16)],
        compiler_params=_cp(2))(sinks, proj, proj, proj, proj, proj, band)


def merge_fwd(ya, att, w_lru_out, w_attn_out, proj):
    s = ya.shape[0]
    tm, tn = min(1024, s), 512

    def body(ya_ref, at_ref, wl_ref, wt_ref, ga_ref, gb_ref, yab_ref, mg_ref):
        y_a = _dot(ya_ref[...], wl_ref[...])
        y_b = _dot(at_ref[...], wt_ref[...])
        yab_ref[0] = y_a.astype(BF16)
        yab_ref[1] = y_b.astype(BF16)
        mg_ref[...] = (_sigmoid(ga_ref[...].astype(F32)) * y_a + _sigmoid(gb_ref[...].astype(F32)) * y_b).astype(BF16)

    return pl.pallas_call(
        body, name="merge_fwd", grid=(s // tm, D // tn),
        in_specs=[pl.BlockSpec((tm, D), lambda i, j: (i, 0)), pl.BlockSpec((tm, D), lambda i, j: (i, 0)),
                  pl.BlockSpec((D, tn), lambda i, j: (0, j)), pl.BlockSpec((D, tn), lambda i, j: (0, j)),
                  pl.BlockSpec((tm, tn), lambda i, j: (i, OFF_GA // tn + j)),
                  pl.BlockSpec((tm, tn), lambda i, j: (i, OFF_GB // tn + j))],
        out_specs=[pl.BlockSpec((2, tm, tn), lambda i, j: (0, i, j)), pl.BlockSpec((tm, tn), lambda i, j: (i, j))],
        out_shape=[SDS((2, s, D), BF16), SDS((s, D), BF16)],
        compiler_params=_cp(2))(ya, att, w_lru_out, w_attn_out, proj, proj)


def outproj_fwd(merged, w_out, x, vecs):
    s = x.shape[0]
    tm, tn = min(1024, s), 512

    def body(m_ref, w_ref, x_ref, v_ref, x1_ref, o1_ref):
        o1 = _dot(m_ref[...], w_ref[...])
        o1_ref[...] = o1.astype(BF16)
        x1_ref[...] = x_ref[...] + v_ref[V_GATE1:V_GATE1 + 1, :] * o1

    return pl.pallas_call(
        body, name="outproj_fwd", grid=(s // tm, D // tn),
        in_specs=[pl.BlockSpec((tm, D), lambda i, j: (i, 0)), pl.BlockSpec((D, tn), lambda i, j: (0, j)),
                  pl.BlockSpec((tm, tn), lambda i, j: (i, j)), pl.BlockSpec((16, tn), lambda i, j: (0, j))],
        out_specs=[pl.BlockSpec((tm, tn), lambda i, j: (i, j)), pl.BlockSpec((tm, tn), lambda i, j: (i, j))],
        out_shape=[SDS((s, D), F32), SDS((s, D), BF16)],
        compiler_params=_cp(2))(merged, w_out, x, vecs)


def ff1_fwd(x1, vecs, w_ff1):
    s = x1.shape[0]
    tm, tn = min(1024, s), 512
    per = D // tn

    def body(x_ref, v_ref, w_ref, f_ref, h_ref, fft_ref):
        @pl.when(pl.program_id(1) == 0)
        def _():
            _modulated_norm(x_ref, v_ref, V_G2, V_SCALE2, V_SHIFT2, h_ref, tm)
        fv = _dot(h_ref[...], w_ref[...])
        f_ref[...] = fv.astype(BF16)
        fp = jnp.maximum(fv, 0.0)
        fft_ref[...] = (fp * fp).astype(BF16).T

    return pl.pallas_call(
        body, name="ff1_fwd", grid=(s // tm, D_FF // tn),
        in_specs=[pl.BlockSpec((tm, D), lambda i, j: (i, 0)), pl.BlockSpec((16, D), lambda i, j: (0, 0)),
                  pl.BlockSpec((None, D, tn), lambda i, j: (j // per, 0, j % per))],
        out_specs=[pl.BlockSpec((tm, tn), lambda i, j: (i, j)), pl.BlockSpec((tm, D), lambda i, j: (i, 0)),
                   pl.BlockSpec((tn, tm), lambda i, j: (j, i))],
        out_shape=[SDS((s, D_FF), BF16), SDS((s, D), BF16), SDS((D_FF, s), BF16)],
        compiler_params=_cp(2))(x1, vecs, w_ff1)


def matmul_bf16(name, a, b, tm, tn):
    m, k = a.shape
    n = b.shape[1]

    def body(a_ref, b_ref, o_ref):
        o_ref[...] = _dot(a_ref[...], b_ref[...]).astype(BF16)

    return pl.pallas_call(
        body, name=name, grid=(m // tm, n // tn),
        in_specs=[pl.BlockSpec((tm, k), lambda i, j: (i, 0)), pl.BlockSpec((k, tn), lambda i, j: (0, j))],
        out_specs=pl.BlockSpec((tm, tn), lambda i, j: (i, j)),
        out_shape=SDS((m, n), BF16), compiler_params=_cp(2))(a, b)


def ff2_loss(f, w_ff2, x1, tgt, vecs):
    s = x1.shape[0]
    tm, tk = min(512, s), 1024
    nk = D_FF // tk

    def body(f_ref, w_ref, x1_hbm, t_hbm, v_ref, dx2_ref, do2_ref, sums_ref, loss_ref, acc, x1_ref, t_ref, sems):
        i, k = pl.program_id(0), pl.program_id(1)
        fetches = _row_fetches((x1_hbm, t_hbm), (x1_ref, t_ref), sems, i, tm)

        @pl.when((i == 0) & (k == 0))
        def _():
            sums_ref[...] = jnp.zeros_like(sums_ref)
            loss_ref[...] = jnp.zeros_like(loss_ref)

        @pl.when(k == 0)
        def _():
            acc[...] = jnp.zeros_like(acc)
            for cp in fetches:
                cp.start()

        fv = jnp.maximum(f_ref[...].astype(F32), 0.0)
        acc[...] += _dot((fv * fv).astype(BF16), w_ref[...])

        @pl.when(k == nk - 1)
        def _():
            for cp in fetches:
                cp.wait()
            gate2 = v_ref[V_GATE2:V_GATE2 + 1, :]
            g3 = v_ref[V_G3:V_G3 + 1, :]

            def sub(rb, carry):
                rs = pl.ds(pl.multiple_of(rb * SUB, SUB), SUB)
                o2 = acc[rs, :]
                x2 = x1_ref[rs, :] + gate2 * o2
                r3, xh = _rms_parts(x2)
                e = xh * g3 - t_ref[rs, :]
                loss_ref[...] += (0.5 / D) * jnp.sum(e * e)
                dy = e * (1.0 / D)
                sums_ref[0:1, :] += jnp.sum(dy * xh, axis=0, keepdims=True)
                dxh = dy * g3
                dx2 = r3 * (dxh - xh * jnp.mean(dxh * xh, axis=-1, keepdims=True))
                sums_ref[1:2, :] += jnp.sum(dx2 * o2, axis=0, keepdims=True)
                dx2_ref[rs, :] = dx2
                do2_ref[rs, :] = (dx2 * gate2).astype(BF16)
                return carry

            lax.fori_loop(0, tm // SUB, sub, 0)

    return pl.pallas_call(
        body, name="ff2_loss", grid=(s // tm, nk),
        in_specs=[pl.BlockSpec((tm, tk), lambda i, k: (i, k)), pl.BlockSpec((tk, D), lambda i, k: (k, 0)),
                  pl.BlockSpec(memory_space=pl.ANY), pl.BlockSpec(memory_space=pl.ANY),
                  pl.BlockSpec((16, D), lambda i, k: (0, 0))],
        out_specs=[pl.BlockSpec((tm, D), lambda i, k: (i, 0)), pl.BlockSpec((tm, D), lambda i, k: (i, 0)),
                   pl.BlockSpec((8, D), lambda i, k: (0, 0)), pl.BlockSpec((8, 128), lambda i, k: (0, 0))],
        out_shape=[SDS((s, D), F32), SDS((s, D), BF16), SDS((8, D), F32), SDS((8, 128), F32)],
        scratch_shapes=[pltpu.VMEM((tm, D), F32), pltpu.VMEM((tm, D), F32), pltpu.VMEM((tm, D), F32),
                        pltpu.SemaphoreType.DMA((2,))],
        compiler_params=_cp(2))(f, w_ff2, x1, tgt, vecs)


def ff2_bwd(do2, w_ff2, f):
    s = do2.shape[0]
    tm, tn = min(1024, s), 512

    def body(d_ref, w_ref, f_ref, o_ref):
        dff = _dot_nt(d_ref[...], w_ref[...])
        o_ref[...] = (dff * (2.0 * jnp.maximum(f_ref[...].astype(F32), 0.0))).astype(BF16)

    return pl.pallas_call(
        body, name="ff2_bwd", grid=(s // tm, D_FF // tn),
        in_specs=[pl.BlockSpec((tm, D), lambda i, j: (i, 0)), pl.BlockSpec((tn, D), lambda i, j: (j, 0)),
                  pl.BlockSpec((tm, tn), lambda i, j: (i, j))],
        out_specs=pl.BlockSpec((tm, tn), lambda i, j: (i, j)),
        out_shape=SDS((s, D_FF), BF16), compiler_params=_cp(2))(do2, w_ff2, f)


def weight_grad(name, a, b, tn, out_shape, out_block, out_map, b_part=0):
    s, m = a.shape
    n = b.shape[1]
    tm = WG_TM
    chunk = min(1024, s)
    nch = s // chunk

    def body(a_hbm, b_ref, o_ref, a_buf, at_s, sem):
        i = pl.program_id(0)

        @pl.when(pl.program_id(1) == 0)
        def _():
            def fetch(ch):
                return pltpu.make_async_copy(a_hbm.at[pl.ds(ch * chunk, chunk), pl.ds(i * tm, tm)],
                                             a_buf.at[ch % 2], sem.at[ch % 2])
            fetch(0).start()
            for ch in range(nch):
                if ch + 1 < nch:
                    fetch(ch + 1).start()
                fetch(ch).wait()
                at_s[:, ch * chunk:(ch + 1) * chunk] = a_buf[ch % 2].T

        o_ref[...] = _dot(at_s[...], b_ref[...]).astype(BF16)

    return pl.pallas_call(
        body, name=name, grid=(m // tm, n // tn),
        in_specs=[pl.BlockSpec(memory_space=pl.ANY), pl.BlockSpec((s, tn), lambda i, j: (b_part, j))],
        out_specs=pl.BlockSpec(out_block, lambda i, j: out_map(i, j)),
        out_shape=SDS(out_shape, BF16),
        scratch_shapes=[pltpu.VMEM((2, chunk, tm), BF16), pltpu.VMEM((tm, s), BF16), pltpu.SemaphoreType.DMA((2,))],
        compiler_params=_cp(2))(a, b)


def ff1_bwd(df, w_ff1, x1, dx2, o1, vecs):
    s = df.shape[0]
    tm, tk = min(512, s), 1024
    nk = D_FF // tk
    per = D // tk

    def body(d_ref, w_ref, x1_hbm, dx2_hbm, o1_hbm, v_ref, dx1_ref, do1_ref, sums_ref, acc, x1_ref, dx2_ref, o1_ref, sems):
        i, k = pl.program_id(0), pl.program_id(1)
        fetches = _row_fetches((x1_hbm, dx2_hbm, o1_hbm), (x1_ref, dx2_ref, o1_ref), sems, i, tm)

        @pl.when((i == 0) & (k == 0))
        def _():
            sums_ref[...] = jnp.zeros_like(sums_ref)

        @pl.when(k == 0)
        def _():
            acc[...] = jnp.zeros_like(acc)
            for cp in fetches:
                cp.start()

        acc[...] += _dot_nt(d_ref[...], w_ref[...])

        @pl.when(k == nk - 1)
        def _():
            for cp in fetches:
                cp.wait()
            g2 = v_ref[V_G2:V_G2 + 1, :]
            scale2 = v_ref[V_SCALE2:V_SCALE2 + 1, :]
            gate1 = v_ref[V_GATE1:V_GATE1 + 1, :]

            def sub(rb, carry):
                rs = pl.ds(pl.multiple_of(rb * SUB, SUB), SUB)
                dh = acc[rs, :]
                r2, xh = _rms_parts(x1_ref[rs, :])
                sums_ref[0:1, :] += jnp.sum(dh, axis=0, keepdims=True)
                sums_ref[1:2, :] += jnp.sum(dh * (xh * g2), axis=0, keepdims=True)
                dxn = dh * (1.0 + scale2)
                sums_ref[2:3, :] += jnp.sum(dxn * xh, axis=0, keepdims=True)
                dxh = dxn * g2
                dx1 = dx2_ref[rs, :] + r2 * (dxh - xh * jnp.mean(dxh * xh, axis=-1, keepdims=True))
                sums_ref[3:4, :] += jnp.sum(dx1 * o1_ref[rs, :].astype(F32), axis=0, keepdims=True)
                dx1_ref[rs, :] = dx1
                do1_ref[rs, :] = (dx1 * gate1).astype(BF16)
                return carry

            lax.fori_loop(0, tm // SUB, sub, 0)

    return pl.pallas_call(
        body, name="ff1_bwd", grid=(s // tm, nk),
        in_specs=[pl.BlockSpec((tm, tk), lambda i, k: (i, k)),
                  pl.BlockSpec((None, D, tk), lambda i, k: (k // per, 0, k % per)),
                  pl.BlockSpec(memory_space=pl.ANY), pl.BlockSpec(memory_space=pl.ANY),
                  pl.BlockSpec(memory_space=pl.ANY), pl.BlockSpec((16, D), lambda i, k: (0, 0))],
        out_specs=[pl.BlockSpec((tm, D), lambda i, k: (i, 0)), pl.BlockSpec((tm, D), lambda i, k: (i, 0)),
                   pl.BlockSpec((8, D), lambda i, k: (0, 0))],
        out_shape=[SDS((s, D), F32), SDS((s, D), BF16), SDS((8, D), F32)],
        scratch_shapes=[pltpu.VMEM((tm, D), F32), pltpu.VMEM((tm, D), F32), pltpu.VMEM((tm, D), F32),
                        pltpu.VMEM((tm, D), BF16), pltpu.SemaphoreType.DMA((3,))],
        compiler_params=_cp(2))(df, w_ff1, x1, dx2, o1, vecs)


def outproj_bwd(do1, w_out, yab, proj):
    s = do1.shape[0]
    tm, tn = min(1024, s), 512
    per = D // tn

    def body(d_ref, w_ref, y_ref, g_ref, dy_ref, dp_ref):
        dm = _dot_nt(d_ref[...], w_ref[...])
        sg = _sigmoid(g_ref[...].astype(F32))
        dy_ref[...] = (dm * sg).astype(BF16)
        dp_ref[...] = (dm * y_ref[...].astype(F32) * sg * (1.0 - sg)).astype(BF16)

    return pl.pallas_call(
        body, name="outproj_bwd", grid=(s // tm, 2 * per),
        in_specs=[pl.BlockSpec((tm, D), lambda i, j: (i, 0)), pl.BlockSpec((tn, D), lambda i, j: (j % per, 0)),
                  pl.BlockSpec((None, tm, tn), lambda i, j: (j // per, i, j % per)),
                  pl.BlockSpec((tm, tn), lambda i, j: (i, OFF_GA // tn + j))],
        out_specs=[pl.BlockSpec((None, tm, tn), lambda i, j: (j // per, i, j % per)),
                   pl.BlockSpec((tm, tn), lambda i, j: (i, OFF_GA // tn + j))],
        out_shape=[SDS((2, s, D), BF16), SDS((s, IN_W), BF16)],
        compiler_params=_cp(2))(do1, w_out, yab, proj)


def lruout_bwd(dyab, w_lru_out, rec, proj, dproj):
    s = rec.shape[0]
    tm, tn = min(1024, s), 512

    def body(d_ref, w_ref, r_ref, g_ref, dp_in, dr_ref, dp_ref):
        dya = _dot_nt(d_ref[...], w_ref[...])
        gate = g_ref[...].astype(F32)
        dr_ref[...] = dya * _gelu(gate)
        dp_ref[...] = (dya * r_ref[...] * _gelu_grad(gate)).astype(BF16)

    return pl.pallas_call(
        body, name="lruout_bwd", grid=(s // tm, D // tn),
        in_specs=[pl.BlockSpec((None, tm, D), lambda i, j: (0, i, 0)), pl.BlockSpec((tn, D), lambda i, j: (j, 0)),
                  pl.BlockSpec((tm, tn), lambda i, j: (i, j)),
                  pl.BlockSpec((tm, tn), lambda i, j: (i, OFF_GATE // tn + j)),
                  pl.BlockSpec(memory_space=pl.ANY)],
        out_specs=[pl.BlockSpec((tm, tn), lambda i, j: (i, j)),
                   pl.BlockSpec((tm, tn), lambda i, j: (i, OFF_GATE // tn + j))],
        out_shape=[SDS((s, D), F32), SDS((s, IN_W), BF16)],
        input_output_aliases={4: 1},
        compiler_params=_cp(2))(dyab, w_lru_out, rec, proj, dproj)


def attnout_bwd(dyab, w_attn_out):
    s = dyab.shape[1]
    tm, tn = min(1024, s), 512

    def body(d_ref, w_ref, o_ref):
        o_ref[...] = _dot_nt(d_ref[...], w_ref[...]).astype(BF16)

    return pl.pallas_call(
        body, name="attnout_bwd", grid=(s // tm, D // tn),
        in_specs=[pl.BlockSpec((None, tm, D), lambda i, j: (1, i, 0)), pl.BlockSpec((tn, D), lambda i, j: (j, 0))],
        out_specs=pl.BlockSpec((tm, tn), lambda i, j: (i, j)),
        out_shape=SDS((s, D), BF16), compiler_params=_cp(2))(dyab, w_attn_out)


def attn_bwd(proj, band, sinks, datt, dproj):
    s = proj.shape[0]
    nb = s // BLOCK
    qw = 1024

    def body(sk_ref, q_ref, kp_ref, kc_ref, vp_ref, vc_ref, b_ref, do_ref, dp_in,
             dq_ref, dkb_ref, dvb_ref, db_ref, ds_ref, qs_buf, dos_buf, s_buf, dp_buf, p_buf, dsc_buf):
        gp = pl.program_id(0)
        n = pl.program_id(1)

        @pl.when(n == 0)
        def _():
            db_ref[...] = jnp.zeros_like(db_ref)
            ds_ref[...] = jnp.zeros_like(ds_ref)

        valid = _band_valid(n)
        kks = _kv_bands(kp_ref, kc_ref)
        vvs = _kv_bands(vp_ref, vc_ref)
        lane_b = lax.broadcasted_iota(jnp.int32, (2 * BLOCK, 128), 1)
        dks, dvs = [], []
        for kv in range(2):
            _stack_heads(q_ref, kv, qs_buf)
            _stack_heads(do_ref, kv, dos_buf)
            s_buf[...] = _dot_nt(qs_buf[...], kks[kv])
            dp_buf[...] = _dot_nt(dos_buf[...], vvs[kv])
            for hq in range(8):
                hl = 8 * kv + hq
                rows = pl.ds(hq * BLOCK, BLOCK)
                p, ps = _attn_probs(s_buf[rows, :], b_ref[hl], sk_ref[gp * 16 + hl], valid)
                dp = dp_buf[rows, :]
                delta = jnp.sum(p * dp, axis=-1, keepdims=True)
                dsc = p * (dp - delta)
                db_ref[hl] += dsc
                ds_ref[hl:hl + 1, :] += jnp.zeros((1, 128), F32) - jnp.sum(ps * delta)
                p_buf[rows, :] = p.astype(BF16)
                dsc_buf[rows, :] = (dsc * SCALE).astype(BF16)
            _unstack_heads(_dot(dsc_buf[...], kks[kv]), dq_ref, kv)
            dk = _dot_tn(dsc_buf[...], qs_buf[...])
            dv = _dot_tn(p_buf[...], dos_buf[...])
            dks.append(dk + pltpu.roll(dk, 64, 1))
            dvs.append(dv + pltpu.roll(dv, 64, 1))
        dkb_ref[...] = jnp.where(lane_b < 64, dks[0], dks[1])
        dvb_ref[...] = jnp.where(lane_b < 64, dvs[0], dvs[1])

    kb, vb = OFF_K // 128, OFF_V // 128
    return pl.pallas_call(
        body, name="attn_bwd", grid=(2, nb),
        in_specs=[pl.BlockSpec(memory_space=pltpu.SMEM),
                  pl.BlockSpec((BLOCK, qw), lambda g, n: (n, OFF_Q // qw + g)),
                  pl.BlockSpec((BLOCK, 128), lambda g, n: (jnp.maximum(n - 1, 0), kb + g)),
                  pl.BlockSpec((BLOCK, 128), lambda g, n: (n, kb + g)),
                  pl.BlockSpec((BLOCK, 128), lambda g, n: (jnp.maximum(n - 1, 0), vb + g)),
                  pl.BlockSpec((BLOCK, 128), lambda g, n: (n, vb + g)),
                  pl.BlockSpec((16, BLOCK, 2 * BLOCK), lambda g, n: (g, 0, 0)),
                  pl.BlockSpec((BLOCK, qw), lambda g, n: (n, g)),
                  pl.BlockSpec(memory_space=pl.ANY)],
        out_specs=[pl.BlockSpec((BLOCK, qw), lambda g, n: (n, OFF_Q // qw + g)),
                   pl.BlockSpec((2 * BLOCK, 128), lambda g, n: (n, g)),
                   pl.BlockSpec((2 * BLOCK, 128), lambda g, n: (n, g)),
                   pl.BlockSpec((16, BLOCK, 2 * BLOCK), lambda g, n: (g, 0, 0)),
                   pl.BlockSpec((16, 128), lambda g, n: (g, 0))],
        out_shape=[SDS((s, IN_W), BF16), SDS((nb * 2 * BLOCK, 256), F32), SDS((nb * 2 * BLOCK, 256), F32),
                   SDS((N_HEADS, BLOCK, 2 * BLOCK), F32), SDS((N_HEADS, 128), F32)],
        input_output_aliases={8: 0},
        scratch_shapes=[pltpu.VMEM((8 * BLOCK, 128), BF16), pltpu.VMEM((8 * BLOCK, 128), BF16),
                        pltpu.VMEM((8 * BLOCK, 2 * BLOCK), F32), pltpu.VMEM((8 * BLOCK, 2 * BLOCK), F32),
                        pltpu.VMEM((8 * BLOCK, 2 * BLOCK), BF16), pltpu.VMEM((8 * BLOCK, 2 * BLOCK), BF16)],
        compiler_params=_cp(2))(sinks, proj, proj, proj, proj, proj, band, datt, dproj)


def dkv_combine(dkb, dvb, dproj):
    nb = dkb.shape[0] // (2 * BLOCK)
    s = nb * BLOCK
    dkb3 = dkb.reshape(nb, 2 * BLOCK, 256)
    dvb3 = dvb.reshape(nb, 2 * BLOCK, 256)

    def body(k1, k2, v1, v2, dp_in, o_ref):
        nxt = jnp.where(pl.program_id(0) < nb - 1, 1.0, 0.0)
        o_ref[:, 0:256] = (k1[...] + nxt * k2[...]).astype(BF16)
        o_ref[:, 256:512] = (v1[...] + nxt * v2[...]).astype(BF16)

    spec1 = pl.BlockSpec((None, BLOCK, 256), lambda m: (m, 1, 0))
    spec2 = pl.BlockSpec((None, BLOCK, 256), lambda m: (jnp.minimum(m + 1, nb - 1), 0, 0))
    return pl.pallas_call(
        body, name="dkv_combine", grid=(nb,),
        in_specs=[spec1, spec2, spec1, spec2, pl.BlockSpec(memory_space=pl.ANY)],
        out_specs=pl.BlockSpec((BLOCK, 512), lambda m: (m, OFF_K // 512)),
        out_shape=SDS((s, IN_W), BF16), input_output_aliases={4: 0},
        compiler_params=_cp(1))(dkb3, dkb3, dvb3, dvb3, dproj)


def lru_bwd(proj, rec, drec, lvec, wa, wx, dproj):
    s = proj.shape[0]
    t = min(256, s)
    nt = s // t

    def body(lx_ref, lxh_ref, rec_ref, rech_ref, dr_ref, lv_ref, wa_ref, wx_ref, dp_in,
             dlx_ref, sums_ref, dwa_ref, dwx_ref,
             xbuf, hbuf, dxbuf, a_s, dh_s, xc_s, r_s, ig_s, mu_s, gc):
        step_i = pl.program_id(0)
        ti = nt - 1 - step_i

        @pl.when(step_i == 0)
        def _():
            sums_ref[...] = jnp.zeros_like(sums_ref)
            dwa_ref[...] = jnp.zeros_like(dwa_ref)
            dwx_ref[...] = jnp.zeros_like(dwx_ref)
            dxbuf[pl.ds(t, 8), :] = jnp.zeros((8, D), F32)
            gc[...] = jnp.zeros((8, D), F32)

        live = jnp.where(ti > 0, 1.0, 0.0)
        xbuf[pl.ds(0, 8), :] = lxh_ref[...].astype(F32)[8:16] * live
        xbuf[pl.ds(8, t), :] = lx_ref[...].astype(F32)
        hbuf[pl.ds(0, 8), :] = rech_ref[...] * live
        hbuf[pl.ds(8, t), :] = rec_ref[...]
        first = (lax.broadcasted_iota(jnp.int32, (t, 128), 0) + ti * t) == 0
        for b in range(N_LRU_BLOCKS):
            cs = slice(b * 128, (b + 1) * 128)
            _, xc, _, r, ig, _, a, mult = _lru_block_fwd(xbuf, lv_ref, wa_ref, wx_ref, b, t, first)
            a_s[:, cs] = a
            xc_s[:, cs] = xc
            r_s[:, cs] = r
            ig_s[:, cs] = ig
            mu_s[:, cs] = mult

        def step(q, g):
            tt = t - 1 - q
            dh = dr_ref[pl.ds(tt, 1), :] + g
            dh_s[pl.ds(tt, 1), :] = dh
            return a_s[pl.ds(tt, 1), :] * dh

        gc[0:1, :] = lax.fori_loop(0, t, step, gc[0:1, :], unroll=8)
        for b in range(N_LRU_BLOCKS):
            cs = slice(b * 128, (b + 1) * 128)
            dh = dh_s[:, cs]
            a = a_s[:, cs]
            xc = xc_s[:, cs]
            r = r_s[:, cs]
            ig = ig_s[:, cs]
            mult = mu_s[:, cs]
            sp = _softplus(-lv_ref[L_LAM:L_LAM + 1, cs])
            lam = lv_ref[L_LAM:L_LAM + 1, cs]
            da = dh * hbuf[pl.ds(7, t), cs]
            dmult = jnp.where(first, 0.0, dh * ig * xc)
            dig = dh * mult * xc
            dxc = dh * mult * ig
            dlog_a = da * a - dmult * (a * a) / mult
            dr = dlog_a * ((-LRU_C) * sp)
            dsp = jnp.sum(dlog_a * ((-LRU_C) * r), axis=0, keepdims=True)
            dza = dr * r * (1.0 - r)
            dzx = dig * ig * (1.0 - ig)
            dzab = dza.astype(BF16)
            dzxb = dzx.astype(BF16)
            xcb = xc.astype(BF16)
            dwa_ref[b] += _dot_tn(xcb, dzab)
            dwx_ref[b] += _dot_tn(xcb, dzxb)
            dxc = dxc + _dot_nt(dzab, wa_ref[b]) + _dot_nt(dzxb, wx_ref[b])
            sums_ref[L_LAM:L_LAM + 1, cs] += dsp * (-jax.nn.sigmoid(-lam))
            sums_ref[L_BA:L_BA + 1, cs] += jnp.sum(dza, axis=0, keepdims=True)
            sums_ref[L_BX:L_BX + 1, cs] += jnp.sum(dzx, axis=0, keepdims=True)
            sums_ref[L_CB:L_CB + 1, cs] += jnp.sum(dxc, axis=0, keepdims=True)
            for kk in range(4):
                sums_ref[kk:kk + 1, cs] += jnp.sum(dxc * xbuf[pl.ds(5 + kk, t), cs], axis=0, keepdims=True)
            dxbuf[pl.ds(0, t), cs] = dxc
            dlx = (lv_ref[3:4, cs] * dxc + lv_ref[2:3, cs] * dxbuf[pl.ds(1, t), cs]
                   + lv_ref[1:2, cs] * dxbuf[pl.ds(2, t), cs] + lv_ref[0:1, cs] * dxbuf[pl.ds(3, t), cs])
            dlx_ref[:, cs] = dlx.astype(BF16)
        dxbuf[pl.ds(t, 8), :] = dxbuf[pl.ds(0, 8), :]

    rev = lambda i: nt - 1 - i
    return pl.pallas_call(
        body, name="lru_bwd", grid=(nt,),
        in_specs=[pl.BlockSpec((t, D), lambda i: (rev(i), 0)),
                  pl.BlockSpec((16, D), lambda i: (jnp.maximum(rev(i) * (t // 16) - 1, 0), 0)),
                  pl.BlockSpec((t, D), lambda i: (rev(i), 0)),
                  pl.BlockSpec((8, D), lambda i: (jnp.maximum(rev(i) * (t // 8) - 1, 0), 0)),
                  pl.BlockSpec((t, D), lambda i: (rev(i), 0)),
                  pl.BlockSpec((8, D), lambda i: (0, 0)),
                  pl.BlockSpec((N_LRU_BLOCKS, 128, 128), lambda i: (0, 0, 0)),
                  pl.BlockSpec((N_LRU_BLOCKS, 128, 128), lambda i: (0, 0, 0)),
                  pl.BlockSpec(memory_space=pl.ANY)],
        out_specs=[pl.BlockSpec((t, D), lambda i: (rev(i), 0)),
                   pl.BlockSpec((8, D), lambda i: (0, 0)),
                   pl.BlockSpec((N_LRU_BLOCKS, 128, 128), lambda i: (0, 0, 0)),
                   pl.BlockSpec((N_LRU_BLOCKS, 128, 128), lambda i: (0, 0, 0))],
        out_shape=[SDS((s, IN_W), BF16), SDS((8, D), F32), SDS((N_LRU_BLOCKS, 128, 128), F32),
                   SDS((N_LRU_BLOCKS, 128, 128), F32)],
        scratch_shapes=[pltpu.VMEM((t + 8, D), F32), pltpu.VMEM((t + 8, D), F32), pltpu.VMEM((t + 8, D), F32)]
        + [pltpu.VMEM((t, D), F32)] * 6 + [pltpu.VMEM((8, D), F32)],
        input_output_aliases={8: 0},
        compiler_params=_cp(1))(proj, proj, rec, rec, drec, lvec, wa, wx, dproj)


def inproj_bwd(dproj, w_in, x, dx1, vecs):
    s = x.shape[0]
    tm, tk = min(512, s), IN_TILE
    nk = IN_W // tk
    per = IN_SHARD // tk

    def body(d_ref, w_ref, x_hbm, dx1_hbm, v_ref, gx_ref, sums_ref, acc, x_ref, dx1_ref, sems):
        i, k = pl.program_id(0), pl.program_id(1)
        fetches = _row_fetches((x_hbm, dx1_hbm), (x_ref, dx1_ref), sems, i, tm)

        @pl.when((i == 0) & (k == 0))
        def _():
            sums_ref[...] = jnp.zeros_like(sums_ref)

        @pl.when(k == 0)
        def _():
            acc[...] = jnp.zeros_like(acc)
            for cp in fetches:
                cp.start()

        acc[...] += _dot_nt(d_ref[...], w_ref[...])

        @pl.when(k == nk - 1)
        def _():
            for cp in fetches:
                cp.wait()
            g1 = v_ref[V_G1:V_G1 + 1, :]
            scale1 = v_ref[V_SCALE1:V_SCALE1 + 1, :]

            def sub(rb, carry):
                rs = pl.ds(pl.multiple_of(rb * SUB, SUB), SUB)
                dh = acc[rs, :]
                r1, xh = _rms_parts(x_ref[rs, :])
                sums_ref[0:1, :] += jnp.sum(dh, axis=0, keepdims=True)
                sums_ref[1:2, :] += jnp.sum(dh * (xh * g1), axis=0, keepdims=True)
                dxn = dh * (1.0 + scale1)
                sums_ref[2:3, :] += jnp.sum(dxn * xh, axis=0, keepdims=True)
                dxh = dxn * g1
                gx_ref[rs, :] = dx1_ref[rs, :] + r1 * (dxh - xh * jnp.mean(dxh * xh, axis=-1, keepdims=True))
                return carry

            lax.fori_loop(0, tm // SUB, sub, 0)

    return pl.pallas_call(
        body, name="inproj_bwd", grid=(s // tm, nk),
        in_specs=[pl.BlockSpec((tm, tk), lambda i, k: (i, k)),
                  pl.BlockSpec((None, D, tk), lambda i, k: (k // per, 0, k % per)),
                  pl.BlockSpec(memory_space=pl.ANY), pl.BlockSpec(memory_space=pl.ANY),
                  pl.BlockSpec((16, D), lambda i, k: (0, 0))],
        out_specs=[pl.BlockSpec((tm, D), lambda i, k: (i, 0)), pl.BlockSpec((8, D), lambda i, k: (0, 0))],
        out_shape=[SDS((s, D), F32), SDS((8, D), F32)],
        scratch_shapes=[pltpu.VMEM((tm, D), F32), pltpu.VMEM((tm, D), F32), pltpu.VMEM((tm, D), F32),
                        pltpu.SemaphoreType.DMA((2,))],
        compiler_params=_cp(2))(dproj, w_in, x, dx1, vecs)


def mod_columns(c16, w_ada, b_cols):
    tn = 512

    def body(c_ref, w_ref, b_ref, o_ref):
        cv = c_ref[...]
        ca = (cv * jax.nn.sigmoid(cv)).astype(BF16)
        o_ref[...] = _dot(ca, w_ref[...].astype(BF16)) + b_ref[...]

    return pl.pallas_call(
        body, name="mod_columns", grid=(ADA_SHARD // tn,),
        in_specs=[pl.BlockSpec((16, D), lambda j: (0, 0)), pl.BlockSpec((D, tn), lambda j: (0, j)),
                  pl.BlockSpec((1, tn), lambda j: (0, j))],
        out_specs=pl.BlockSpec((16, tn), lambda j: (0, j)),
        out_shape=SDS((16, ADA_SHARD), F32), compiler_params=_cp(1))(c16, w_ada, b_cols)


def wada_update(c16, dmod16, w, m, v):
    tm, tn = 512, 512

    def body(c_ref, d_ref, w_ref, m_ref, v_ref, g_out, dl_out, m_out, v_out):
        cv = c_ref[...]
        ca = (cv * jax.nn.sigmoid(cv)).astype(BF16)
        g = _dot_tn(ca, d_ref[...].astype(BF16))
        dl, m2, v2 = _adamw_math(w_ref[...], g, m_ref[...], v_ref[...])
        g_out[...] = g
        dl_out[...] = dl
        m_out[...] = m2
        v_out[...] = v2

    tile = pl.BlockSpec((tm, tn), lambda i, j: (i, j))
    return pl.pallas_call(
        body, name="wada_update", grid=(D // tm, ADA_SHARD // tn),
        in_specs=[pl.BlockSpec((16, tm), lambda i, j: (0, i)), pl.BlockSpec((16, tn), lambda i, j: (0, j)),
                  tile, tile, tile],
        out_specs=[tile] * 4, out_shape=[SDS((D, ADA_SHARD), F32)] * 4,
        compiler_params=_cp(2))(c16, dmod16, w, m, v)


def adamw_big(name, w, mine, theirs, m, v, c_idx):
    r, c = w.shape
    tr = 128
    per = (r // 2) // tr

    def body(c_ref, w_ref, a_ref, b_ref, m_ref, v_ref, g_out, dl_out, m_out, v_out):
        own = (pl.program_id(0) // per) == c_ref[0]
        g = jnp.where(own, a_ref[...], b_ref[...])
        dl, m2, v2 = _adamw_math(w_ref[...], g, m_ref[...], v_ref[...])
        g_out[...] = g
        dl_out[...] = dl
        m_out[...] = m2
        v_out[...] = v2

    tile = pl.BlockSpec((tr, c), lambda i, cr: (i, 0))
    half = pl.BlockSpec((tr, c), lambda i, cr: (i % per, 0))
    gs = pltpu.PrefetchScalarGridSpec(num_scalar_prefetch=1, grid=(r // tr,),
                                      in_specs=[tile, half, half, tile, tile], out_specs=[tile] * 4)
    return pl.pallas_call(body, name=name, grid_spec=gs, out_shape=[SDS((r, c), F32)] * 4,
                          compiler_params=_cp(1))(c_idx, w, mine, theirs, m, v)


def cast_into_slot(name, w, k_idx):
    r, c = w.shape
    tr = 256

    def body(k_ref, w_ref, o_ref):
        o_ref[...] = w_ref[...].astype(BF16)

    gs = pltpu.PrefetchScalarGridSpec(
        num_scalar_prefetch=1, grid=(r // tr,),
        in_specs=[pl.BlockSpec((tr, c), lambda i, kr: (i, 0))],
        out_specs=pl.BlockSpec((None, tr, c), lambda i, kr: (kr[0], i, 0)))
    return pl.pallas_call(body, name=name, grid_spec=gs, out_shape=SDS((N_CHIPS, r, c), BF16),
                          compiler_params=_cp(1))(k_idx, w)


def adamw_small(ws, gs, ms, vs):
    n = len(ws)

    def body(*refs):
        for i in range(n):
            dl, m2, v2 = _adamw_math(refs[i][...], refs[n + i][...], refs[2 * n + i][...], refs[3 * n + i][...])
            refs[4 * n + i][...] = dl
            refs[5 * n + i][...] = m2
            refs[6 * n + i][...] = v2

    vm = pl.BlockSpec(memory_space=pltpu.VMEM)
    shapes = [SDS(w.shape, F32) for w in ws]
    outs = pl.pallas_call(
        body, name="adamw_small", in_specs=[vm] * (4 * n), out_specs=[vm] * (3 * n), out_shape=shapes * 3,
        compiler_params=pltpu.CompilerParams(vmem_limit_bytes=VMEM_LIMIT))(*ws, *gs, *ms, *vs)
    return outs[:n], outs[n:2 * n], outs[2 * n:]


def sum_devices(name, gathered):
    rows = gathered.shape[1]
    tr = min(rows, 128)

    def body(x_ref, o_ref):
        acc = x_ref[0].astype(F32)
        for d in range(1, N_DEV):
            acc = acc + x_ref[d].astype(F32)
        o_ref[...] = acc

    return pl.pallas_call(
        body, name=name, grid=(rows // tr,),
        in_specs=[pl.BlockSpec((N_DEV, tr, D), lambda i: (0, i, 0))],
        out_specs=pl.BlockSpec((tr, D), lambda i: (i, 0)),
        out_shape=SDS((rows, D), F32), compiler_params=_cp(1))(gathered)


def _mesh_pos():
    return lax.axis_index("x"), lax.axis_index("y"), lax.axis_index("c")


def _other_chips(x, y):
    return [(1 - x, y), (x, 1 - y), (1 - x, 1 - y)]


def all_gather_small(name, block):
    m_per, n = block.shape

    def body(x_ref, out_ref, send_sems, recv_sems, local_sem):
        x, y, c = _mesh_pos()
        me, sibling = (x, y, c), (x, y, 1 - c)
        chips = _other_chips(x, y)

        def rows(px, py, pc):
            return out_ref.at[pl.ds((4 * px + 2 * py + pc) * m_per, m_per), :]

        def copy(k, blk, to, src=None):
            return pltpu.make_async_remote_copy(
                src_ref=rows(*blk) if src is None else src, dst_ref=rows(*blk),
                send_sem=send_sems.at[k], recv_sem=recv_sems.at[k], device_id=to, device_id_type=MESH)

        mine = pltpu.make_async_copy(x_ref, rows(*me), local_sem)
        mine.start()
        first = [copy(0, me, sibling, src=x_ref)]
        first += [copy(1 + j, me, (*chip, c), src=x_ref) for j, chip in enumerate(chips)]
        for cp in first:
            cp.start()
        passed = [copy(4 + j, (*chip, c), sibling) for j, chip in enumerate(chips)]
        for j, chip in enumerate(chips):
            copy(1 + j, (*chip, c), me).wait_recv()
            passed[j].start()
        copy(0, sibling, me).wait_recv()
        for j, chip in enumerate(chips):
            copy(4 + j, (*chip, 1 - c), me).wait_recv()
        for cp in first + passed:
            cp.wait_send()
        mine.wait()

    vm = pl.BlockSpec(memory_space=pltpu.VMEM)
    return pl.pallas_call(
        body, name=name, out_shape=SDS((N_DEV * m_per, n), block.dtype), in_specs=[vm], out_specs=vm,
        scratch_shapes=[pltpu.SemaphoreType.DMA((7,)), pltpu.SemaphoreType.DMA((7,)), pltpu.SemaphoreType.DMA],
        compiler_params=pltpu.CompilerParams(vmem_limit_bytes=VMEM_LIMIT))(block)


def sibling_sum(name, grad, other, c_idx):
    _, r, cc = grad.shape
    h = r // 2
    tr = min(256, h)
    g4 = grad.reshape(N_CHIPS, 2, h, cc)

    def body(c_ref, a_ref, b_ref, o_ref):
        o_ref[...] = (a_ref[...].astype(F32) + b_ref[...].astype(F32)).astype(BF16)

    gs = pltpu.PrefetchScalarGridSpec(
        num_scalar_prefetch=1, grid=(N_CHIPS, h // tr),
        in_specs=[pl.BlockSpec((None, None, tr, cc), lambda s, i, cr: (s, cr[0], i, 0)),
                  pl.BlockSpec((None, tr, cc), lambda s, i, cr: (s, i, 0))],
        out_specs=pl.BlockSpec((None, tr, cc), lambda s, i, cr: (s, i, 0)))
    return pl.pallas_call(body, name=name, grid_spec=gs, out_shape=SDS((N_CHIPS, h, cc), BF16),
                          compiler_params=_cp(2))(c_idx, g4, other)


HBM_SPEC = pl.BlockSpec(memory_space=pltpu.HBM)
SEM_SPEC = pl.BlockSpec(memory_space=pltpu.SEMAPHORE)


def _side_effecting():
    return pltpu.CompilerParams(has_side_effects=pltpu.SideEffectType.DATAFLOW_SIDE_EFFECTING)


def _in_hbm(a):
    return pltpu.with_memory_space_constraint(a, pltpu.HBM)


ALL_CHIPS = (0, 1, 2)


def gather_start(name, bufs, after, rel=ALL_CHIPS, carry=None):
    n = len(bufs)
    nr = len(rel)
    halves = [w.shape[1] // 2 for w in bufs]
    extra = [] if carry is None else [carry]

    def body(*refs):
        ins = refs[:n]
        send_sems, recv_sems, token = refs[n + 1 + len(extra)], refs[n + 2 + len(extra)], refs[-1]
        x, y, c = _mesh_pos()
        k = 2 * x + y
        for i in range(n):
            reg = ins[i].at[k, pl.ds(c * halves[i], halves[i]), :]
            for q, j in enumerate(rel):
                chip = _other_chips(x, y)[j]
                pltpu.make_async_remote_copy(src_ref=reg, dst_ref=reg, send_sem=send_sems.at[nr * i + q],
                                             recv_sem=recv_sems.at[nr * i + q], device_id=(*chip, c),
                                             device_id_type=MESH).start()
        token[...] = jnp.zeros_like(token)

    outs = pl.pallas_call(
        body, name=name,
        out_shape=(pltpu.SemaphoreType.DMA((nr * n,)), pltpu.SemaphoreType.DMA((nr * n,)),
                   *[pltpu.HBM(w.shape, w.dtype) for w in list(bufs) + extra], SDS((8, 128), F32)),
        in_specs=[HBM_SPEC] * n + [pl.BlockSpec(memory_space=pl.ANY)] + [HBM_SPEC] * len(extra),
        out_specs=(SEM_SPEC, SEM_SPEC, *[HBM_SPEC] * (n + len(extra)), pl.BlockSpec(memory_space=pltpu.VMEM)),
        input_output_aliases={**{i: 2 + i for i in range(n)}, **({n + 1: 2 + n} if extra else {})},
        compiler_params=_side_effecting())(*[_in_hbm(w) for w in bufs], after, *[_in_hbm(w) for w in extra])
    return (outs[0], outs[1], list(outs[2:2 + n]), outs[-1]) + ((outs[2 + n],) if extra else ())


def gather_wait(name, send_sems, recv_sems, bufs, after, rel=ALL_CHIPS):
    n = len(bufs)
    nr = len(rel)
    halves = [w.shape[1] // 2 for w in bufs]

    def body(*refs):
        ins = refs[:n]
        send_sems, recv_sems = refs[n], refs[n + 1]
        x, y, c = _mesh_pos()
        k = 2 * x + y
        for i in range(n):
            for q, j in enumerate(rel):
                chip = _other_chips(x, y)[j]
                kj = 2 * chip[0] + chip[1]
                cp = pltpu.make_async_remote_copy(
                    src_ref=ins[i].at[k, pl.ds(c * halves[i], halves[i]), :],
                    dst_ref=ins[i].at[kj, pl.ds(c * halves[i], halves[i]), :],
                    send_sem=send_sems.at[nr * i + q], recv_sem=recv_sems.at[nr * i + q], device_id=(*chip, c),
                    device_id_type=MESH)
                cp.wait_send()
                cp.wait_recv()

    return pl.pallas_call(
        body, name=name, out_shape=[pltpu.HBM(w.shape, w.dtype) for w in bufs],
        in_specs=[HBM_SPEC] * n + [SEM_SPEC, SEM_SPEC, pl.BlockSpec(memory_space=pl.ANY)],
        out_specs=[HBM_SPEC] * n, input_output_aliases={i: i for i in range(n)},
        compiler_params=_side_effecting())(*bufs, send_sems, recv_sems, after)


def gather_forward(name, bufs, rel=ALL_CHIPS):
    n = len(bufs)
    halves = [w.shape[1] // 2 for w in bufs]

    def body(*refs):
        outs = refs[n:2 * n]
        send_sems, recv_sems = refs[2 * n:]
        x, y, c = _mesh_pos()
        chips = _other_chips(x, y)

        def copy(i, j, half, to):
            kj = 2 * chips[j][0] + chips[j][1]
            reg = outs[i].at[kj, pl.ds(half * halves[i], halves[i]), :]
            return pltpu.make_async_remote_copy(src_ref=reg, dst_ref=reg, send_sem=send_sems.at[i, j],
                                                recv_sem=recv_sems.at[i, j], device_id=to, device_id_type=MESH)

        cps = [copy(i, j, c, (x, y, 1 - c)) for i in range(n) for j in rel]
        for cp in cps:
            cp.start()
        for i in range(n):
            for j in rel:
                copy(i, j, 1 - c, (x, y, c)).wait_recv()
        for cp in cps:
            cp.wait_send()

    hbm = pl.BlockSpec(memory_space=pl.ANY)
    return pl.pallas_call(
        body, name=name, in_specs=[hbm] * n, out_specs=[hbm] * n,
        out_shape=[SDS(w.shape, w.dtype) for w in bufs], input_output_aliases={i: i for i in range(n)},
        scratch_shapes=[pltpu.SemaphoreType.DMA((n, 3)), pltpu.SemaphoreType.DMA((n, 3))])(*bufs)


def _exchange_plan(kind, srcs, zones):
    x, y, c = _mesh_pos()
    plan = []
    for src, zone in zip(srcs, zones):
        if kind == "chips":
            for j, chip in enumerate(_other_chips(x, y)):
                plan.append((src.at[2 * chip[0] + chip[1]], zone.at[j], (*chip, c)))
        elif kind == "sibling":
            h = zone.shape[1]
            plan.append((src.at[:, pl.ds((1 - c) * h, h), :], zone, (x, y, 1 - c)))
        else:
            peers = [(x, y, 1 - c)] + [(*chip, cc) for chip in _other_chips(x, y) for cc in (c, 1 - c)]
            plan += [(src, zone.at[4 * x + 2 * y + c], peer) for peer in peers]
    return plan


_COPIES_PER_ARRAY = {"chips": 3, "sibling": 1, "all": N_DEV - 1}


def _landing_zones(kind, srcs):
    if kind == "chips":
        return [lax.empty((3,) + t.shape[1:], t.dtype) for t in srcs]
    if kind == "sibling":
        return [lax.empty((t.shape[0], t.shape[1] // 2, t.shape[2]), t.dtype) for t in srcs]
    return [jnp.broadcast_to(t, (N_DEV,) + t.shape) for t in srcs]


def exchange_start(name, kind, srcs, after):
    n = len(srcs)
    lands = _landing_zones(kind, srcs)
    n_copies = n * _COPIES_PER_ARRAY[kind]

    def body(*refs):
        send_sems, recv_sems, token = refs[2 * n + 1], refs[2 * n + 2], refs[-1]
        for q, (src, dst, dev) in enumerate(_exchange_plan(kind, refs[:n], refs[n:2 * n])):
            pltpu.make_async_remote_copy(src_ref=src, dst_ref=dst, send_sem=send_sems.at[q], recv_sem=recv_sems.at[q],
                                         device_id=dev, device_id_type=MESH).start()
        token[...] = jnp.zeros_like(token)

    outs = pl.pallas_call(
        body, name=name,
        out_shape=(pltpu.SemaphoreType.DMA((n_copies,)), pltpu.SemaphoreType.DMA((n_copies,)),
                   *[pltpu.HBM(t.shape, t.dtype) for t in srcs], *[pltpu.HBM(t.shape, t.dtype) for t in lands],
                   SDS((8, 128), F32)),
        in_specs=[HBM_SPEC] * (2 * n) + [pl.BlockSpec(memory_space=pl.ANY)],
        out_specs=(SEM_SPEC, SEM_SPEC, *[HBM_SPEC] * (2 * n), pl.BlockSpec(memory_space=pltpu.VMEM)),
        input_output_aliases={i: 2 + i for i in range(2 * n)},
        compiler_params=_side_effecting())(*[_in_hbm(t) for t in srcs], *[_in_hbm(t) for t in lands], after)
    return outs[0], outs[1], list(outs[2:2 + n]), list(outs[2 + n:2 + 2 * n]), outs[-1]


def exchange_wait(name, kind, send_sems, recv_sems, srcs, lands, after):
    n = len(srcs)

    def body(*refs):
        send_sems, recv_sems = refs[2 * n], refs[2 * n + 1]
        for q, (src, dst, dev) in enumerate(_exchange_plan(kind, refs[:n], refs[n:2 * n])):
            cp = pltpu.make_async_remote_copy(src_ref=src, dst_ref=dst, send_sem=send_sems.at[q],
                                              recv_sem=recv_sems.at[q], device_id=dev, device_id_type=MESH)
            cp.wait_send()
            cp.wait_recv()

    outs = pl.pallas_call(
        body, name=name, out_shape=[pltpu.HBM(t.shape, t.dtype) for t in srcs + lands],
        in_specs=[HBM_SPEC] * (2 * n) + [SEM_SPEC, SEM_SPEC, pl.BlockSpec(memory_space=pl.ANY)],
        out_specs=[HBM_SPEC] * (2 * n), input_output_aliases={i: i for i in range(2 * n)},
        compiler_params=_side_effecting())(*srcs, *lands, send_sems, recv_sems, after)
    return list(outs[:n]), list(outs[n:])


def chip_sum(name, sums, parts, k_idx):
    _, h, cc = parts.shape
    tr = min(256, h)

    def body(k_ref, own_ref, p_ref, o_ref):
        acc = own_ref[...].astype(F32)
        for s in range(3):
            acc = acc + p_ref[s].astype(F32)
        o_ref[...] = acc

    gs = pltpu.PrefetchScalarGridSpec(
        num_scalar_prefetch=1, grid=(h // tr,),
        in_specs=[pl.BlockSpec((None, tr, cc), lambda i, kr: (kr[0], i, 0)),
                  pl.BlockSpec((3, tr, cc), lambda i, kr: (0, i, 0))],
        out_specs=pl.BlockSpec((tr, cc), lambda i, kr: (i, 0)))
    return pl.pallas_call(body, name=name, grid_spec=gs, out_shape=SDS((h, cc), F32),
                          compiler_params=_cp(1))(k_idx, sums, parts)


def halves_exchange(name, halves):
    n = len(halves)

    def body(*refs):
        ins, outs = refs[:n], refs[n:2 * n]
        send_sems, recv_sems = refs[2 * n:]
        x, y, c = _mesh_pos()
        cps = []
        for i in range(n):
            cp = pltpu.make_async_remote_copy(
                src_ref=ins[i], dst_ref=outs[i], send_sem=send_sems.at[i], recv_sem=recv_sems.at[i],
                device_id=(x, y, 1 - c), device_id_type=MESH)
            cp.start()
            cps.append(cp)
        for cp in cps:
            cp.wait_recv()
        for cp in cps:
            cp.wait_send()

    hbm = pl.BlockSpec(memory_space=pl.ANY)
    return pl.pallas_call(
        body, name=name, in_specs=[hbm] * n, out_specs=[hbm] * n,
        out_shape=[SDS(t.shape, F32) for t in halves],
        scratch_shapes=[pltpu.SemaphoreType.DMA((n,)), pltpu.SemaphoreType.DMA((n,))])(*halves)


def local_step(x, tgt, vecs, lvec, wa, wx, sinks, rel_bias, proj, h, w_in, rest_weights, hook):
    buckets = t5_bucket_table()
    band = bias_band(rel_bias.T, buckets).reshape(N_HEADS, BLOCK, 2 * BLOCK)

    ya, rec = lru_fwd(proj, lvec, wa, wx)
    att = attn_fwd(proj, band, sinks)
    w_lru_out, w_attn_out, w_out = rest_weights("mix", att[:8, :128] + ya[:8, :128])
    w_lru_out2, w_attn_out2, w_out2 = w_lru_out.reshape(D, D), w_attn_out.reshape(D, D), w_out.reshape(D, D)
    yab, merged = merge_fwd(ya, att, w_lru_out2, w_attn_out2, proj)
    x1, o1 = outproj_fwd(merged, w_out2, x, vecs)
    w_ff1, w_ff2 = rest_weights("ff", o1[:8, :128])
    w_ff2_2 = w_ff2.reshape(D_FF, D)
    f, h2, fft = ff1_fwd(x1, vecs, w_ff1)
    dx2, do2, sums_f, loss = ff2_loss(f, w_ff2_2, x1, tgt, vecs)

    df = ff2_bwd(do2, w_ff2_2, f)
    g_ff2 = matmul_bf16("dw_ff2", fft, do2, WG_TM, 512)
    dx1, do1, sums_2 = ff1_bwd(df, w_ff1, x1, dx2, o1, vecs)
    g_ff1 = weight_grad("dw_ff1", h2, df, 512, (N_CHIPS, D, D), (None, WG_TM, 512), lambda i, j: (j // 4, i, j % 4))
    dyab, dproj = outproj_bwd(do1, w_out2, yab, proj)
    g_out = weight_grad("dw_out", merged, do1, 512, (D, D), (WG_TM, 512), lambda i, j: (i, j))
    drec, dproj = lruout_bwd(dyab, w_lru_out2, rec, proj, dproj)
    dyab2 = dyab.reshape(2 * x.shape[0], D)
    g_lru_out = weight_grad("dw_lru_out", ya, dyab2, 512, (D, D), (WG_TM, 512), lambda i, j: (i, j))
    datt = attnout_bwd(dyab, w_attn_out2)
    g_attn_out = weight_grad("dw_attn_out", att, dyab2, 512, (D, D), (WG_TM, 512), lambda i, j: (i, j), b_part=1)
    zero = hook("grads_a", [g_lru_out.reshape(N_CHIPS, D // 4, D), g_attn_out.reshape(N_CHIPS, D // 4, D),
                            g_out.reshape(N_CHIPS, D // 4, D), g_ff1, g_ff2.reshape(N_CHIPS, D_FF // 4, D)])
    dproj, dkb, dvb, dband, dsink = attn_bwd(proj, band, sinks + zero, datt, dproj)
    zero = hook("after_attn_bwd", dkb)
    dproj = dkv_combine(dkb, dvb, dproj)
    dproj, sums_l, d_wa, d_wx = lru_bwd(proj, rec, drec, lvec + zero, wa, wx, dproj)
    hook("lru_grads", (d_wa, d_wx))
    per = IN_SHARD // IN_TILE
    g_in = weight_grad("dw_in", h, dproj, IN_TILE, (N_CHIPS, D, IN_SHARD), (None, WG_TM, IN_TILE),
                       lambda i, j: (j // per, i, j % per))
    zero = hook("grads_b", [g_in])
    grad_x, sums_1 = inproj_bwd(dproj, w_in, x, dx1, vecs + zero)
    d_rel_bias = bias_band_bwd(dband.reshape(N_HEADS, BLOCK * 2 * BLOCK), buckets)

    small = dict(sums_f=sums_f, sums_2=sums_2, sums_1=sums_1, sums_l=sums_l, d_wa=d_wa, d_wx=d_wx,
                 d_sinks=dsink[:, 0], d_rel_bias=d_rel_bias)
    return loss, grad_x, small


def _pad_rows(a, rows):
    return jnp.concatenate([a, jnp.zeros((rows - a.shape[0], a.shape[1]), a.dtype)], axis=0)


def kernel(x, c, w_ada, b_ada, norm1_g, w_in, conv_w, conv_b, lru_wa, lru_ba, lru_wx, lru_bx, lru_lambda, w_lru_out, w_attn_out, attn_sinks, rel_bias, w_out, norm2_g, w_ff1, w_ff2, final_g, loss_target, m_w_ada, m_b_ada, m_norm1_g, m_w_in, m_conv_w, m_conv_b, m_lru_wa, m_lru_ba, m_lru_wx, m_lru_bx, m_lru_lambda, m_w_lru_out, m_w_attn_out, m_attn_sinks, m_rel_bias, m_w_out, m_norm2_g, m_w_ff1, m_w_ff2, m_final_g, v_w_ada, v_b_ada, v_norm1_g, v_w_in, v_conv_w, v_conv_b, v_lru_wa, v_lru_ba, v_lru_wx, v_lru_bx, v_lru_lambda, v_w_lru_out, v_w_attn_out, v_attn_sinks, v_rel_bias, v_w_out, v_norm2_g, v_w_ff1, v_w_ff2, v_final_g):
    xi, yi, ci = _mesh_pos()
    chip = 2 * xi + yi
    dev = 2 * chip + ci
    z8 = jnp.zeros((8, D), F32)

    conv_rows = jnp.concatenate([conv_w[0], jnp.zeros((4, D - D // 4), F32)], axis=1)
    pack0 = jnp.concatenate([c, conv_rows, jnp.zeros((3, D), F32)], axis=0)
    g0 = all_gather_small("gather_cond", pack0).reshape(N_DEV, 8, D)
    c_all = g0[:, 0, :]
    conv_full = jnp.concatenate([g0[2 * k, 1:5, :D // 4] for k in range(N_CHIPS)], axis=1)
    c16 = jnp.concatenate([c_all, z8], axis=0)
    b_cols = lax.dynamic_slice_in_dim(b_ada, chip * ADA_SHARD, ADA_SHARD, axis=1)
    mod_c = mod_columns(c16, w_ada[0], b_cols)
    g1 = all_gather_small("gather_mod", mod_c).reshape(N_DEV, 16, ADA_SHARD)
    mod = jnp.concatenate([lax.dynamic_index_in_dim(g1[2 * k], dev, axis=0, keepdims=False) for k in range(N_CHIPS)])
    shift1, scale1, gate1, shift2, scale2, gate2 = [mod[i * D:(i + 1) * D] for i in range(6)]
    vecs = jnp.stack([norm1_g[0], scale1, shift1, gate1, norm2_g[0], scale2, shift2, gate2, final_g]
                     + [jnp.zeros((D,), F32)] * 7)
    lvec = jnp.concatenate([conv_full, conv_b, lru_ba, lru_bx, lru_lambda], axis=0)

    shards = [w_in[0], w_lru_out[0], w_attn_out[0], w_out[0], w_ff1[0], w_ff2[0]]
    names = ["w_in", "w_lru_out", "w_attn_out", "w_out", "w_ff1", "w_ff2"]
    k_idx = jnp.reshape(chip, (1,)).astype(jnp.int32)
    c_idx = jnp.reshape(ci, (1,)).astype(jnp.int32)
    near, far = (0, 1), (2,)
    shard_of = lambda flip: jnp.reshape(chip ^ flip, (1,)).astype(jnp.int32)
    x2d = x[0]
    n_send, n_recv, w_in_buf, _ = gather_start(
        "gather_start_in_near", [cast_into_slot("cast_w_in", shards[0], k_idx)], vecs, near)
    proj, h = inproj_fwd("inproj_fwd_own", x2d, None, vecs, w_in_buf[0], None, k_idx)
    slots = [cast_into_slot("cast_" + nm, w, k_idx) for nm, w in zip(names[1:], shards[1:])]
    w_in_buf = gather_forward("gather_forward_in_near", gather_wait(
        "gather_wait_in_near", n_send, n_recv, w_in_buf, proj[:8, :128] + slots[-1][0, :8, :128], near), near)
    f_send, f_recv, w_in_buf, _ = gather_start("gather_start_in_far", w_in_buf, proj[:8, :128], far)
    in_flight = {"mix": gather_start("gather_start_mix", slots[:3], proj[:8, :128])}
    in_flight["ff"] = gather_start("gather_start_ff", slots[3:], in_flight["mix"][3], carry=w_in_buf[0])
    w_in_buf = [in_flight["ff"][4]]
    proj = inproj_fwd("inproj_fwd_x", x2d, h, vecs, w_in_buf[0], proj, shard_of(2))
    proj = inproj_fwd("inproj_fwd_y", x2d, h, vecs, w_in_buf[0], proj, shard_of(1))
    w_in_buf = gather_forward("gather_forward_in_far", gather_wait(
        "gather_wait_in_far", f_send, f_recv, w_in_buf, proj[:8, :128], far), far)
    proj = inproj_fwd("inproj_fwd_d", x2d, h, vecs, w_in_buf[0], proj, shard_of(3))
    w_in_full = w_in_buf[0]
    pending = {}

    def rest_weights(group, after):
        send_sems, recv_sems, bufs = in_flight[group][:3]
        return gather_forward("gather_forward_" + group,
                              gather_wait("gather_wait_" + group, send_sems, recv_sems, bufs, after))

    def reduce_hook(event, payload):
        if event == "grads_a":
            pending["sib_a"] = exchange_start("sibling_start_a", "sibling", payload, payload[0])
            return pending["sib_a"][-1][0, 0]
        if event == "lru_grads":
            pack_w = jnp.concatenate([payload[0].reshape(128, D), payload[1].reshape(128, D)], axis=0).astype(BF16)
            pending["lru_w"] = exchange_start("lru_w_grads_start", "all", [pack_w], pack_w)
            return pending["lru_w"][-1][0, 0]
        if event == "grads_b":
            pending["sib_b"] = exchange_start("sibling_start_b", "sibling", payload, pending["lru_w"][-1])
            return pending["sib_b"][-1][0, 0]
        return chips_start("a", names[1:], payload)

    def chips_start(tag, nms, after):
        send_sems, recv_sems, grads, lands, _ = pending["sib_" + tag]
        grads, lands = exchange_wait("sibling_wait_" + tag, "sibling", send_sems, recv_sems, grads, lands, after)
        sums = [sibling_sum("sibling_sum_" + nm, g, o, c_idx) for nm, g, o in zip(nms, grads, lands)]
        pending[tag] = exchange_start("exchange_start_" + tag, "chips", sums, sums[0])
        return pending[tag][-1][0, 0]

    loss_t, grad_x, small = local_step(
        x2d, loss_target[0], vecs, lvec, lru_wa[0].astype(BF16), lru_wx[0].astype(BF16),
        attn_sinks[0], rel_bias, proj, h, w_in_full, rest_weights, reduce_hook)
    chips_start("b", names[:1], grad_x)

    big_m = dict(zip(names, [m_w_in, m_w_lru_out, m_w_attn_out, m_w_out, m_w_ff1, m_w_ff2]))
    big_v = dict(zip(names, [v_w_in, v_w_lru_out, v_w_attn_out, v_w_out, v_w_ff1, v_w_ff2]))
    local_w = dict(zip(names, shards))
    g_big, d_big, nm_big, nv_big = {}, {}, {}, {}

    def links_done(tag, after):
        send_sems, recv_sems, sums, lands, _ = pending[tag]
        return exchange_wait("exchange_wait_" + tag, "chips", send_sems, recv_sems, sums, lands, after)

    def finish_reduce(nms, sums, lands):
        mine = [chip_sum("chip_sum_" + nm, t, p, k_idx) for nm, t, p in zip(nms, sums, lands)]
        theirs = halves_exchange("halves_exchange_" + nms[0], mine)
        for nm, a, b in zip(nms, mine, theirs):
            g2, dl, m2, v2 = adamw_big("adamw_" + nm, local_w[nm], a, b, big_m[nm][0], big_v[nm][0], c_idx)
            g_big[nm], d_big[nm], nm_big[nm], nv_big[nm] = g2[None], dl[None], m2[None], v2[None]
        return lax.optimization_barrier(tuple(nv_big[nm] for nm in nms))[0]

    done_a = finish_reduce(names[1:], *links_done("a", pending["b"][-1]))
    sums_b, lands_b = links_done("b", done_a)
    w_send, w_recv, w_src, w_lands, _ = pending["lru_w"]
    w_src, w_lands = exchange_wait("lru_w_grads_wait", "all", w_send, w_recv, w_src, w_lands, lands_b[0])

    sums_f, sums_2, sums_1, sums_l = small["sums_f"], small["sums_2"], small["sums_1"], small["sums_l"]
    vec_rows = jnp.stack([sums_1[2], sums_2[2], sums_f[0], sums_l[L_CB], sums_l[L_BA], sums_l[L_BX],
                          sums_l[L_LAM], jnp.zeros((D,), F32)])
    mod_rows = jnp.stack([sums_1[0], sums_1[1], sums_2[3], sums_2[0], sums_2[1], sums_f[1],
                          jnp.zeros((D,), F32), jnp.zeros((D,), F32)])
    att_rows = jnp.concatenate([
        jnp.concatenate([small["d_sinks"], jnp.zeros((D - N_HEADS,), F32)])[None],
        jnp.concatenate([small["d_rel_bias"].reshape(-1), jnp.zeros((D - N_BUCKETS * N_HEADS,), F32)])[None],
        jnp.zeros((6, D), F32)], axis=0)
    pack = jnp.concatenate([vec_rows, _pad_rows(sums_l[0:4], 8), mod_rows, att_rows], axis=0)
    pack, lru_w_all = lax.optimization_barrier((pack, w_lands[0]))
    gathered = all_gather_small("gather_small_grads", pack).reshape(N_DEV, P_WA, D)
    total = sum_devices("sum_small_grads", gathered)
    total_w = sum_devices("sum_lru_w_grads", lru_w_all)
    dmod_all = gathered[:, P_MOD:P_MOD + 6, :].reshape(N_DEV, 6 * D)
    dmod16 = jnp.concatenate([lax.dynamic_slice_in_dim(dmod_all, chip * ADA_SHARD, ADA_SHARD, axis=1),
                              jnp.zeros((8, ADA_SHARD), F32)], axis=0)
    g_w_ada, d_w_ada, nm_w_ada, nv_w_ada = wada_update(c16, dmod16, w_ada[0], m_w_ada[0], v_w_ada[0])
    finish_reduce(names[:1], sums_b, lands_b)
    loss = lax.psum(lax.optimization_barrier((loss_t, total))[0][0, 0], ("x", "y", "c"))

    conv_g = lax.dynamic_slice_in_dim(total[P_CONVW:P_CONVW + 4], chip * (D // 4), D // 4, axis=1)
    sm_names = ["b_ada", "norm1_g", "conv_w", "conv_b", "lru_wa", "lru_ba", "lru_wx", "lru_bx", "lru_lambda",
                "attn_sinks", "rel_bias", "norm2_g", "final_g"]
    sm_w = [b_ada.reshape(6, D), norm1_g, conv_w[0], conv_b, lru_wa.reshape(128, D), lru_ba, lru_wx.reshape(128, D),
            lru_bx, lru_lambda, attn_sinks, rel_bias, norm2_g, final_g[None]]
    sm_m = [m_b_ada.reshape(6, D), m_norm1_g, m_conv_w[0], m_conv_b, m_lru_wa.reshape(128, D), m_lru_ba,
            m_lru_wx.reshape(128, D), m_lru_bx, m_lru_lambda, m_attn_sinks, m_rel_bias, m_norm2_g, m_final_g[None]]
    sm_v = [v_b_ada.reshape(6, D), v_norm1_g, v_conv_w[0], v_conv_b, v_lru_wa.reshape(128, D), v_lru_ba,
            v_lru_wx.reshape(128, D), v_lru_bx, v_lru_lambda, v_attn_sinks, v_rel_bias, v_norm2_g, v_final_g[None]]
    sm_g = [total[P_MOD:P_MOD + 6], total[0:1], conv_g, total[3:4], total_w[0:128], total[4:5],
            total_w[128:256], total[5:6], total[6:7], total[P_ATT:P_ATT + 1, :N_HEADS],
            total[P_ATT + 1, :N_BUCKETS * N_HEADS].reshape(N_BUCKETS, N_HEADS), total[1:2], total[2:3]]
    sm_d, sm_nm, sm_nv = adamw_small(sm_w, sm_g, sm_m, sm_v)
    shapes = dict(b_ada=b_ada.shape, norm1_g=norm1_g.shape, conv_w=conv_w.shape, conv_b=conv_b.shape,
                  lru_wa=lru_wa.shape, lru_ba=lru_ba.shape, lru_wx=lru_wx.shape, lru_bx=lru_bx.shape,
                  lru_lambda=lru_lambda.shape, attn_sinks=attn_sinks.shape, rel_bias=rel_bias.shape,
                  norm2_g=norm2_g.shape, final_g=final_g.shape)
    grads = dict(w_ada=g_w_ada[None], **g_big)
    deltas = dict(w_ada=d_w_ada[None], **d_big)
    new_m = dict(w_ada=nm_w_ada[None], **nm_big)
    new_v = dict(w_ada=nv_w_ada[None], **nv_big)
    for i, nm in enumerate(sm_names):
        grads[nm] = sm_g[i].reshape(shapes[nm])
        deltas[nm] = sm_d[i].reshape(shapes[nm])
        new_m[nm] = sm_nm[i].reshape(shapes[nm])
        new_v[nm] = sm_nv[i].reshape(shapes[nm])
    order = ["w_ada", "b_ada", "norm1_g", "w_in", "conv_w", "conv_b", "lru_wa", "lru_ba", "lru_wx", "lru_bx",
             "lru_lambda", "w_lru_out", "w_attn_out", "attn_sinks", "rel_bias", "w_out", "norm2_g", "w_ff1", "w_ff2",
             "final_g"]
    return (loss, grad_x[None], *[grads[n] for n in order], *[deltas[n] for n in order],
            *[new_m[n] for n in order], *[new_v[n] for n in order])
```

```python
import math

import numpy as np
import jax
import jax.numpy as jnp
from jax import lax
from jax.experimental import pallas as pl
from jax.experimental.pallas import tpu as pltpu

F32 = jnp.float32
BF16 = jnp.bfloat16
SDS = jax.ShapeDtypeStruct
MESH = pl.DeviceIdType.MESH

D = 2048
D_FF = 4 * D
N_HEADS = 32
HEAD_DIM = 64
BLOCK = 128
N_LRU_BLOCKS = 16
LRU_C = 8.0
EPS = 1e-6
NEG_INF = -1e30
N_BUCKETS = 32
MAX_DISTANCE = 128
IN_W = 10752
IN_SHARD = IN_W // 4
IN_TILE = 896
ADA_SHARD = 6 * D // 4
OFF_LRU, OFF_GATE, OFF_Q, OFF_K, OFF_V, OFF_GA, OFF_GB = 0, 2048, 4096, 6144, 6400, 6656, 8704
SCALE = HEAD_DIM ** -0.5
N_CHIPS = 4
N_DEV = 8

ADAM_LR, ADAM_B1, ADAM_B2, ADAM_EPS, ADAM_WD, ADAM_STEP = 0.001, 0.9, 0.999, 1e-08, 0.01, 10
ADAM_C1 = 1.0 - ADAM_B1 ** ADAM_STEP
ADAM_C2 = 1.0 - ADAM_B2 ** ADAM_STEP

VMEM_LIMIT = 52 * 2 ** 20
SUB = 128
WG_TM = 1024
V_G1, V_SCALE1, V_SHIFT1, V_GATE1, V_G2, V_SCALE2, V_SHIFT2, V_GATE2, V_G3 = range(9)
L_CW0, L_CB, L_BA, L_BX, L_LAM = 0, 4, 5, 6, 7
P_VEC, P_CONVW, P_MOD, P_ATT, P_WA = 0, 8, 16, 24, 32


def _cp(n_axes):
    return pltpu.CompilerParams(dimension_semantics=("arbitrary",) * n_axes, vmem_limit_bytes=VMEM_LIMIT)


def _dot(a, b):
    return jnp.dot(a, b, preferred_element_type=F32)


def _dot_nt(a, b):
    return lax.dot_general(a, b, (((1,), (1,)), ((), ())), preferred_element_type=F32)


def _dot_tn(a, b):
    return lax.dot_general(a, b, (((0,), (0,)), ((), ())), preferred_element_type=F32)


_G0 = math.sqrt(2.0 / math.pi)
_G1 = 0.044715


def _gelu(x):
    return 0.5 * x * (1.0 + jnp.tanh(_G0 * (x + _G1 * x * x * x)))


def _gelu_grad(x):
    x2 = x * x
    t = jnp.tanh(_G0 * (x + _G1 * x * x2))
    return 0.5 * (1.0 + t) + 0.5 * x * (1.0 - t * t) * _G0 * (1.0 + 3.0 * _G1 * x2)


def _sigmoid(x):
    return 0.5 * jnp.tanh(0.5 * x) + 0.5


def _one_minus_exp2(x):
    t = jnp.tanh(x)
    return (-2.0 * t) / (1.0 - t)


def _softplus(z):
    e = jnp.exp(-jnp.abs(z))
    u = 1.0 + e
    l1p = jnp.where(u == 1.0, e, jnp.log(u) * e / (u - 1.0))
    return jnp.maximum(z, 0.0) + l1p


def _adamw_math(w, g, m, v):
    m2 = ADAM_B1 * m + (1.0 - ADAM_B1) * g
    v2 = ADAM_B2 * v + (1.0 - ADAM_B2) * (g * g)
    m_hat = m2 / ADAM_C1
    v_hat = v2 / ADAM_C2
    delta = -ADAM_LR * (m_hat / (jnp.sqrt(v_hat) + ADAM_EPS) + ADAM_WD * w)
    return delta, m2, v2


def _rms_parts(xv):
    r = lax.rsqrt(jnp.mean(xv * xv, axis=-1, keepdims=True) + EPS)
    return r, xv * r


def _row_fetches(hbm_refs, bufs, sems, i, rows):
    return [pltpu.make_async_copy(h.at[pl.ds(i * rows, rows), :], b, sems.at[n])
            for n, (h, b) in enumerate(zip(hbm_refs, bufs))]


def _modulated_norm(x_ref, v_ref, row_g, row_scale, row_shift, h_ref, rows):
    g, scale, shift = v_ref[row_g:row_g + 1, :], v_ref[row_scale:row_scale + 1, :], v_ref[row_shift:row_shift + 1, :]

    def sub(rb, carry):
        rs = pl.ds(pl.multiple_of(rb * SUB, SUB), SUB)
        _, xh = _rms_parts(x_ref[rs, :])
        h_ref[rs, :] = ((xh * g) * (1.0 + scale) + shift).astype(BF16)
        return carry

    lax.fori_loop(0, rows // SUB, sub, 0)


def inproj_fwd(name, x, h, vecs, w_in, proj, shard):
    s = x.shape[0]
    tm = min(1024, s)
    per = IN_SHARD // IN_TILE
    first = h is None

    def body(*refs):
        if first:
            _, x_ref, v_ref, w_ref, proj_ref, h_ref = refs

            @pl.when(pl.program_id(1) == 0)
            def _():
                _modulated_norm(x_ref, v_ref, V_G1, V_SCALE1, V_SHIFT1, h_ref, tm)
        else:
            _, h_ref, w_ref, _, proj_ref = refs
        proj_ref[...] = _dot(h_ref[...], w_ref[...]).astype(BF16)

    rows = pl.BlockSpec((tm, D), lambda i, j, sr: (i, 0))
    w_spec = pl.BlockSpec((None, D, IN_TILE), lambda i, j, sr: (sr[0], 0, j))
    proj_spec = pl.BlockSpec((tm, IN_TILE), lambda i, j, sr: (i, sr[0] * per + j))
    if first:
        gs = pltpu.PrefetchScalarGridSpec(
            num_scalar_prefetch=1, grid=(s // tm, per),
            in_specs=[rows, pl.BlockSpec((16, D), lambda i, j, sr: (0, 0)), w_spec], out_specs=[proj_spec, rows])
        return pl.pallas_call(body, name=name, grid_spec=gs, out_shape=[SDS((s, IN_W), BF16), SDS((s, D), BF16)],
                              compiler_params=_cp(2))(shard, x, vecs, w_in)
    gs = pltpu.PrefetchScalarGridSpec(
        num_scalar_prefetch=1, grid=(s // tm, per),
        in_specs=[rows, w_spec, pl.BlockSpec(memory_space=pl.ANY)], out_specs=proj_spec)
    return pl.pallas_call(body, name=name, grid_spec=gs, out_shape=SDS((s, IN_W), BF16),
                          input_output_aliases={3: 0}, compiler_params=_cp(2))(shard, h, w_in, proj)


def _lru_block_fwd(xbuf, lv_ref, wa_ref, wx_ref, b, t, first):
    cs = slice(b * 128, (b + 1) * 128)
    x0 = xbuf[pl.ds(8, t), cs]
    x1 = xbuf[pl.ds(7, t), cs]
    x2 = xbuf[pl.ds(6, t), cs]
    x3 = xbuf[pl.ds(5, t), cs]
    xc = (lv_ref[L_CB:L_CB + 1, cs] + lv_ref[3:4, cs] * x0 + lv_ref[2:3, cs] * x1
          + lv_ref[1:2, cs] * x2 + lv_ref[0:1, cs] * x3)
    xcb = xc.astype(BF16)
    r = _sigmoid(_dot(xcb, wa_ref[b]) + lv_ref[L_BA:L_BA + 1, cs])
    ig = _sigmoid(_dot(xcb, wx_ref[b]) + lv_ref[L_BX:L_BX + 1, cs])
    sp = _softplus(-lv_ref[L_LAM:L_LAM + 1, cs])
    log_a = (-LRU_C) * r * sp
    a = jnp.exp(log_a)
    mult = jnp.where(first, 1.0, jnp.sqrt(_one_minus_exp2(log_a)))
    return (x0, x1, x2, x3), xc, xcb, r, ig, sp, a, mult


def lru_fwd(proj, lvec, wa, wx):
    s = proj.shape[0]
    t = min(256, s)

    def body(lx_ref, gate_ref, lv_ref, wa_ref, wx_ref, ya_ref, rec_ref, yat_ref, xbuf, a_s, u_s, hc):
        i = pl.program_id(0)

        @pl.when(i == 0)
        def _():
            xbuf[pl.ds(0, 8), :] = jnp.zeros((8, D), F32)
            hc[...] = jnp.zeros((8, D), F32)

        @pl.when(i > 0)
        def _():
            xbuf[pl.ds(0, 8), :] = xbuf[pl.ds(t, 8), :]

        xbuf[pl.ds(8, t), :] = lx_ref[...].astype(F32)
        first = (lax.broadcasted_iota(jnp.int32, (t, 128), 0) + i * t) == 0
        for b in range(N_LRU_BLOCKS):
            cs = slice(b * 128, (b + 1) * 128)
            _, xc, _, _, ig, _, a, mult = _lru_block_fwd(xbuf, lv_ref, wa_ref, wx_ref, b, t, first)
            a_s[:, cs] = a
            u_s[:, cs] = mult * (ig * xc)

        def step(tt, h):
            h = a_s[pl.ds(tt, 1), :] * h + u_s[pl.ds(tt, 1), :]
            rec_ref[pl.ds(tt, 1), :] = h
            return h

        hc[0:1, :] = lax.fori_loop(0, t, step, hc[0:1, :], unroll=8)
        for b in range(N_LRU_BLOCKS):
            cs = slice(b * 128, (b + 1) * 128)
            yb = (rec_ref[:, cs] * _gelu(gate_ref[:, cs].astype(F32))).astype(BF16)
            ya_ref[:, cs] = yb
            yat_ref[cs, :] = yb.T

    return pl.pallas_call(
        body, name="lru_fwd", grid=(s // t,),
        in_specs=[pl.BlockSpec((t, D), lambda i: (i, OFF_LRU // D)),
                  pl.BlockSpec((t, D), lambda i: (i, OFF_GATE // D)),
                  pl.BlockSpec((8, D), lambda i: (0, 0)),
                  pl.BlockSpec((N_LRU_BLOCKS, 128, 128), lambda i: (0, 0, 0)),
                  pl.BlockSpec((N_LRU_BLOCKS, 128, 128), lambda i: (0, 0, 0))],
        out_specs=[pl.BlockSpec((t, D), lambda i: (i, 0)), pl.BlockSpec((t, D), lambda i: (i, 0)),
                   pl.BlockSpec((D, t), lambda i: (0, i))],
        out_shape=[SDS((s, D), BF16), SDS((s, D), F32), SDS((D, s), BF16)],
        scratch_shapes=[pltpu.VMEM((t + 8, D), F32), pltpu.VMEM((t, D), F32), pltpu.VMEM((t, D), F32),
                        pltpu.VMEM((8, D), F32)],
        compiler_params=_cp(1))(proj, proj, lvec, wa, wx)


def t5_bucket_table():
    qi = np.arange(BLOCK)[:, None]
    ki = np.arange(2 * BLOCK)[None, :]
    rel = qi + BLOCK - ki
    relc = np.maximum(rel, 0)
    max_exact = N_BUCKETS // 2
    relf = np.maximum(relc, 1).astype(np.float32)
    large = max_exact + (np.log(relf / np.float32(max_exact)) / np.float32(math.log(MAX_DISTANCE / max_exact))
                         * np.float32(N_BUCKETS - max_exact)).astype(np.int32)
    large = np.minimum(large, N_BUCKETS - 1)
    bucket = np.where(relc < max_exact, relc, large)
    bucket = np.where((rel >= 0) & (rel < BLOCK), bucket, -1)
    return jnp.asarray(bucket.reshape(1, BLOCK * 2 * BLOCK), jnp.int32)


def bias_band(rel_bias_t, buckets):
    n = BLOCK * 2 * BLOCK
    tn = 4096

    def body(bk_ref, rb_ref, o_ref):
        row = lax.broadcasted_iota(jnp.int32, (N_BUCKETS, tn), 0)
        oh = jnp.where(row == bk_ref[...], 1.0, 0.0).astype(BF16)
        rb = rb_ref[...]
        p0 = rb.astype(BF16)
        r1 = rb - p0.astype(F32)
        p1 = r1.astype(BF16)
        p2 = (r1 - p1.astype(F32)).astype(BF16)
        o_ref[...] = _dot(p0, oh) + _dot(p1, oh) + _dot(p2, oh)

    return pl.pallas_call(
        body, name="bias_band", grid=(n // tn,),
        in_specs=[pl.BlockSpec((1, tn), lambda i: (0, i)), pl.BlockSpec((N_HEADS, N_BUCKETS), lambda i: (0, 0))],
        out_specs=pl.BlockSpec((N_HEADS, tn), lambda i: (0, i)),
        out_shape=SDS((N_HEADS, n), F32), compiler_params=_cp(1))(buckets, rel_bias_t)


def bias_band_bwd(dband, buckets):
    n = BLOCK * 2 * BLOCK
    tn = 4096

    def body(bk_ref, d_ref, o_ref):
        @pl.when(pl.program_id(0) == 0)
        def _():
            o_ref[...] = jnp.zeros_like(o_ref)
        row = lax.broadcasted_iota(jnp.int32, (N_BUCKETS, tn), 0)
        oh = jnp.where(row == bk_ref[...], 1.0, 0.0).astype(BF16)
        dv = d_ref[...]
        p0 = dv.astype(BF16)
        r1 = dv - p0.astype(F32)
        p1 = r1.astype(BF16)
        p2 = (r1 - p1.astype(F32)).astype(BF16)
        o_ref[...] += _dot_nt(oh, p0) + _dot_nt(oh, p1) + _dot_nt(oh, p2)

    return pl.pallas_call(
        body, name="bias_band_bwd", grid=(n // tn,),
        in_specs=[pl.BlockSpec((1, tn), lambda i: (0, i)), pl.BlockSpec((N_HEADS, tn), lambda i: (0, i))],
        out_specs=pl.BlockSpec((N_BUCKETS, N_HEADS), lambda i: (0, 0)),
        out_shape=SDS((N_BUCKETS, N_HEADS), F32), compiler_params=_cp(1))(buckets, dband)


def _dup_half(band, which):
    lane = lax.broadcasted_iota(jnp.int32, band.shape, 1)
    rolled = pltpu.roll(band, 64, 1)
    keep = (lane < 64) if which == 0 else (lane >= 64)
    return jnp.where(keep, band, rolled)


def _attn_probs(scores, bias, sink, valid):
    sc = jnp.where(valid, scores * SCALE + bias, NEG_INF)
    m = jnp.maximum(jnp.max(sc, axis=-1, keepdims=True), sink)
    e = jnp.exp(sc - m)
    es = jnp.exp(sink - m)
    inv = 1.0 / (jnp.sum(e, axis=-1, keepdims=True) + es)
    return e * inv, es * inv


def _stack_heads(src_ref, kv, dst):
    lane = lax.broadcasted_iota(jnp.int32, (BLOCK, 128), 1)
    for jj in range(4):
        slab = src_ref[:, (4 * kv + jj) * 128:(4 * kv + jj + 1) * 128]
        for hh in range(2):
            keep = (lane < 64) if hh == 0 else (lane >= 64)
            dst[pl.ds((2 * jj + hh) * BLOCK, BLOCK), :] = jnp.where(keep, slab, jnp.zeros_like(slab))


def _unstack_heads(stacked, dst_ref, kv, dst_t_ref=None):
    lane = lax.broadcasted_iota(jnp.int32, (BLOCK, 128), 1)
    for jj in range(4):
        lo = stacked[(2 * jj) * BLOCK:(2 * jj + 1) * BLOCK]
        hi = stacked[(2 * jj + 1) * BLOCK:(2 * jj + 2) * BLOCK]
        slab = jnp.where(lane < 64, lo, hi).astype(dst_ref.dtype)
        dst_ref[:, (4 * kv + jj) * 128:(4 * kv + jj + 1) * 128] = slab
        if dst_t_ref is not None:
            dst_t_ref[(4 * kv + jj) * 128:(4 * kv + jj + 1) * 128, :] = slab.T


def _band_valid(n):
    qi = lax.broadcasted_iota(jnp.int32, (BLOCK, 2 * BLOCK), 0)
    ki = lax.broadcasted_iota(jnp.int32, (BLOCK, 2 * BLOCK), 1)
    rel = qi + BLOCK - ki
    return (rel >= 0) & (rel < BLOCK) & ((ki >= BLOCK) | (n > 0))


def _kv_bands(prev_ref, cur_ref):
    band = jnp.concatenate([prev_ref[...].astype(F32), cur_ref[...].astype(F32)], axis=0)
    return [_dup_half(band, 0).astype(BF16), _dup_half(band, 1).astype(BF16)]


def attn_fwd(proj, band, sinks):
    s = proj.shape[0]
    nb = s // BLOCK
    qw = 1024

    def body(sk_ref, q_ref, kp_ref, kc_ref, vp_ref, vc_ref, b_ref, o_ref, ot_ref, qs_buf, s_buf, p_buf):
        n = pl.program_id(0)
        gp = pl.program_id(1)
        valid = _band_valid(n)
        kks = _kv_bands(kp_ref, kc_ref)
        vvs = _kv_bands(vp_ref, vc_ref)
        for kv in range(2):
            _stack_heads(q_ref, kv, qs_buf)
            s_buf[...] = _dot_nt(qs_buf[...], kks[kv])
            for hq in range(8):
                hl = 8 * kv + hq
                rows = pl.ds(hq * BLOCK, BLOCK)
                p, _ = _attn_probs(s_buf[rows, :], b_ref[hl], sk_ref[gp * 16 + hl], valid)
                p_buf[rows, :] = p.astype(BF16)
            _unstack_heads(_dot(p_buf[...], vvs[kv]), o_ref, kv, ot_ref)

    kb, vb = OFF_K // 128, OFF_V // 128
    return pl.pallas_call(
        body, name="attn_fwd", grid=(nb, 2),
        in_specs=[pl.BlockSpec(memory_space=pltpu.SMEM),
                  pl.BlockSpec((BLOCK, qw), lambda n, g: (n, OFF_Q // qw + g)),
                  pl.BlockSpec((BLOCK, 128), lambda n, g: (jnp.maximum(n - 1, 0), kb + g)),
                  pl.BlockSpec((BLOCK, 128), lambda n, g: (n, kb + g)),
                  pl.BlockSpec((BLOCK, 128), lambda n, g: (jnp.maximum(n - 1, 0), vb + g)),
                  pl.BlockSpec((BLOCK, 128), lambda n, g: (n, vb + g)),
                  pl.BlockSpec((16, BLOCK, 2 * BLOCK), lambda n, g: (g, 0, 0))],
        out_specs=[pl.BlockSpec((BLOCK, qw), lambda n, g: (n, g)), pl.BlockSpec((qw, BLOCK), lambda n, g: (g, n))],
        out_shape=[SDS((s, D), BF16), SDS((D, s), BF16)],
        scratch_shapes=[pltpu.VMEM((8 * BLOCK, 128), BF16), pltpu.VMEM((8 * BLOCK, 2 * BLOCK), F32),
                        pltpu.VMEM((8 * BLOCK, 2 * BLOCK), BF16)],
        compiler_params=_cp(2))(sinks, proj, proj, proj, proj, proj, band)


def merge_fwd(ya, att, w_lru_out, w_attn_out, proj):
    s = ya.shape[0]
    tm, tn = min(1024, s), 512

    def body(ya_ref, at_ref, wl_ref, wt_ref, ga_ref, gb_ref, yab_ref, mg_ref, mgt_ref):
        y_a = _dot(ya_ref[...], wl_ref[...])
        y_b = _dot(at_ref[...], wt_ref[...])
        yab_ref[0] = y_a.astype(BF16)
        yab_ref[1] = y_b.astype(BF16)
        mg = (_sigmoid(ga_ref[...].astype(F32)) * y_a + _sigmoid(gb_ref[...].astype(F32)) * y_b).astype(BF16)
        mg_ref[...] = mg
        mgt_ref[...] = mg.T

    return pl.pallas_call(
        body, name="merge_fwd", grid=(s // tm, D // tn),
        in_specs=[pl.BlockSpec((tm, D), lambda i, j: (i, 0)), pl.BlockSpec((tm, D), lambda i, j: (i, 0)),
                  pl.BlockSpec((D, tn), lambda i, j: (0, j)), pl.BlockSpec((D, tn), lambda i, j: (0, j)),
                  pl.BlockSpec((tm, tn), lambda i, j: (i, OFF_GA // tn + j)),
                  pl.BlockSpec((tm, tn), lambda i, j: (i, OFF_GB // tn + j))],
        out_specs=[pl.BlockSpec((2, tm, tn), lambda i, j: (0, i, j)), pl.BlockSpec((tm, tn), lambda i, j: (i, j)),
                   pl.BlockSpec((tn, tm), lambda i, j: (j, i))],
        out_shape=[SDS((2, s, D), BF16), SDS((s, D), BF16), SDS((D, s), BF16)],
        compiler_params=_cp(2))(ya, att, w_lru_out, w_attn_out, proj, proj)


def outproj_fwd(merged, w_out, x, vecs):
    s = x.shape[0]
    tm, tn = min(1024, s), 512

    def body(m_ref, w_ref, x_ref, v_ref, x1_ref, o1_ref):
        o1 = _dot(m_ref[...], w_ref[...])
        o1_ref[...] = o1.astype(BF16)
        x1_ref[...] = x_ref[...] + v_ref[V_GATE1:V_GATE1 + 1, :] * o1

    return pl.pallas_call(
        body, name="outproj_fwd", grid=(s // tm, D // tn),
        in_specs=[pl.BlockSpec((tm, D), lambda i, j: (i, 0)), pl.BlockSpec((D, tn), lambda i, j: (0, j)),
                  pl.BlockSpec((tm, tn), lambda i, j: (i, j)), pl.BlockSpec((16, tn), lambda i, j: (0, j))],
        out_specs=[pl.BlockSpec((tm, tn), lambda i, j: (i, j)), pl.BlockSpec((tm, tn), lambda i, j: (i, j))],
        out_shape=[SDS((s, D), F32), SDS((s, D), BF16)],
        compiler_params=_cp(2))(merged, w_out, x, vecs)


def ff1_fwd(x1, vecs, w_ff1):
    s = x1.shape[0]
    tm, tn = min(1024, s), 512
    per = D // tn

    def body(x_ref, v_ref, w_ref, f_ref, h_ref, fft_ref):
        @pl.when(pl.program_id(1) == 0)
        def _():
            _modulated_norm(x_ref, v_ref, V_G2, V_SCALE2, V_SHIFT2, h_ref, tm)
        fv = _dot(h_ref[...], w_ref[...])
        f_ref[...] = fv.astype(BF16)
        fp = jnp.maximum(fv, 0.0)
        fft_ref[...] = (fp * fp).astype(BF16).T

    return pl.pallas_call(
        body, name="ff1_fwd", grid=(s // tm, D_FF // tn),
        in_specs=[pl.BlockSpec((tm, D), lambda i, j: (i, 0)), pl.BlockSpec((16, D), lambda i, j: (0, 0)),
                  pl.BlockSpec((None, D, tn), lambda i, j: (j // per, 0, j % per))],
        out_specs=[pl.BlockSpec((tm, tn), lambda i, j: (i, j)), pl.BlockSpec((tm, D), lambda i, j: (i, 0)),
                   pl.BlockSpec((tn, tm), lambda i, j: (j, i))],
        out_shape=[SDS((s, D_FF), BF16), SDS((s, D), BF16), SDS((D_FF, s), BF16)],
        compiler_params=_cp(2))(x1, vecs, w_ff1)


def matmul_bf16(name, a, b, tm, tn, b_part=0):
    m, k = a.shape
    n = b.shape[1]

    def body(a_ref, b_ref, o_ref):
        o_ref[...] = _dot(a_ref[...], b_ref[...]).astype(BF16)

    return pl.pallas_call(
        body, name=name, grid=(m // tm, n // tn),
        in_specs=[pl.BlockSpec((tm, k), lambda i, j: (i, 0)), pl.BlockSpec((k, tn), lambda i, j: (b_part, j))],
        out_specs=pl.BlockSpec((tm, tn), lambda i, j: (i, j)),
        out_shape=SDS((m, n), BF16), compiler_params=_cp(2))(a, b)


def ff2_loss(f, w_ff2, x1, tgt, vecs):
    s = x1.shape[0]
    tm, tk = min(512, s), 1024
    nk = D_FF // tk

    def body(f_ref, w_ref, x1_hbm, t_hbm, v_ref, dx2_ref, do2_ref, sums_ref, loss_ref, acc, x1_ref, t_ref, sems):
        i, k = pl.program_id(0), pl.program_id(1)
        fetches = _row_fetches((x1_hbm, t_hbm), (x1_ref, t_ref), sems, i, tm)

        @pl.when((i == 0) & (k == 0))
        def _():
            sums_ref[...] = jnp.zeros_like(sums_ref)
            loss_ref[...] = jnp.zeros_like(loss_ref)

        @pl.when(k == 0)
        def _():
            acc[...] = jnp.zeros_like(acc)
            for cp in fetches:
                cp.start()

        fv = jnp.maximum(f_ref[...].astype(F32), 0.0)
        acc[...] += _dot((fv * fv).astype(BF16), w_ref[...])

        @pl.when(k == nk - 1)
        def _():
            for cp in fetches:
                cp.wait()
            gate2 = v_ref[V_GATE2:V_GATE2 + 1, :]
            g3 = v_ref[V_G3:V_G3 + 1, :]

            def sub(rb, carry):
                rs = pl.ds(pl.multiple_of(rb * SUB, SUB), SUB)
                o2 = acc[rs, :]
                x2 = x1_ref[rs, :] + gate2 * o2
                r3, xh = _rms_parts(x2)
                e = xh * g3 - t_ref[rs, :]
                loss_ref[...] += (0.5 / D) * jnp.sum(e * e)
                dy = e * (1.0 / D)
                sums_ref[0:1, :] += jnp.sum(dy * xh, axis=0, keepdims=True)
                dxh = dy * g3
                dx2 = r3 * (dxh - xh * jnp.mean(dxh * xh, axis=-1, keepdims=True))
                sums_ref[1:2, :] += jnp.sum(dx2 * o2, axis=0, keepdims=True)
                dx2_ref[rs, :] = dx2
                do2_ref[rs, :] = (dx2 * gate2).astype(BF16)
                return carry

            lax.fori_loop(0, tm // SUB, sub, 0)

    return pl.pallas_call(
        body, name="ff2_loss", grid=(s // tm, nk),
        in_specs=[pl.BlockSpec((tm, tk), lambda i, k: (i, k)), pl.BlockSpec((tk, D), lambda i, k: (k, 0)),
                  pl.BlockSpec(memory_space=pl.ANY), pl.BlockSpec(memory_space=pl.ANY),
                  pl.BlockSpec((16, D), lambda i, k: (0, 0))],
        out_specs=[pl.BlockSpec((tm, D), lambda i, k: (i, 0)), pl.BlockSpec((tm, D), lambda i, k: (i, 0)),
                   pl.BlockSpec((8, D), lambda i, k: (0, 0)), pl.BlockSpec((8, 128), lambda i, k: (0, 0))],
        out_shape=[SDS((s, D), F32), SDS((s, D), BF16), SDS((8, D), F32), SDS((8, 128), F32)],
        scratch_shapes=[pltpu.VMEM((tm, D), F32), pltpu.VMEM((tm, D), F32), pltpu.VMEM((tm, D), F32),
                        pltpu.SemaphoreType.DMA((2,))],
        compiler_params=_cp(2))(f, w_ff2, x1, tgt, vecs)


def ff2_bwd(do2, w_ff2, f):
    s = do2.shape[0]
    tm, tn = min(1024, s), 512

    def body(d_ref, w_ref, f_ref, o_ref):
        dff = _dot_nt(d_ref[...], w_ref[...])
        o_ref[...] = (dff * (2.0 * jnp.maximum(f_ref[...].astype(F32), 0.0))).astype(BF16)

    return pl.pallas_call(
        body, name="ff2_bwd", grid=(s // tm, D_FF // tn),
        in_specs=[pl.BlockSpec((tm, D), lambda i, j: (i, 0)), pl.BlockSpec((tn, D), lambda i, j: (j, 0)),
                  pl.BlockSpec((tm, tn), lambda i, j: (i, j))],
        out_specs=pl.BlockSpec((tm, tn), lambda i, j: (i, j)),
        out_shape=SDS((s, D_FF), BF16), compiler_params=_cp(2))(do2, w_ff2, f)


def weight_grad(name, a, b, tn, out_shape, out_block, out_map):
    s, m = a.shape
    n = b.shape[1]
    tm = WG_TM
    chunk = min(1024, s)
    nch = s // chunk

    def body(a_hbm, b_ref, o_ref, a_buf, at_s, sem):
        i = pl.program_id(0)

        @pl.when(pl.program_id(1) == 0)
        def _():
            def fetch(ch):
                return pltpu.make_async_copy(a_hbm.at[pl.ds(ch * chunk, chunk), pl.ds(i * tm, tm)],
                                             a_buf.at[ch % 2], sem.at[ch % 2])
            fetch(0).start()
            for ch in range(nch):
                if ch + 1 < nch:
                    fetch(ch + 1).start()
                fetch(ch).wait()
                at_s[:, ch * chunk:(ch + 1) * chunk] = a_buf[ch % 2].T

        o_ref[...] = _dot(at_s[...], b_ref[...]).astype(BF16)

    return pl.pallas_call(
        body, name=name, grid=(m // tm, n // tn),
        in_specs=[pl.BlockSpec(memory_space=pl.ANY), pl.BlockSpec((s, tn), lambda i, j: (0, j))],
        out_specs=pl.BlockSpec(out_block, lambda i, j: out_map(i, j)),
        out_shape=SDS(out_shape, BF16),
        scratch_shapes=[pltpu.VMEM((2, chunk, tm), BF16), pltpu.VMEM((tm, s), BF16), pltpu.SemaphoreType.DMA((2,))],
        compiler_params=_cp(2))(a, b)


def ff1_bwd(df, w_ff1, x1, dx2, o1, vecs):
    s = df.shape[0]
    tm, tk = min(512, s), 1024
    nk = D_FF // tk
    per = D // tk

    def body(d_ref, w_ref, x1_hbm, dx2_hbm, o1_hbm, v_ref, dx1_ref, do1_ref, sums_ref, acc, x1_ref, dx2_ref, o1_ref, sems):
        i, k = pl.program_id(0), pl.program_id(1)
        fetches = _row_fetches((x1_hbm, dx2_hbm, o1_hbm), (x1_ref, dx2_ref, o1_ref), sems, i, tm)

        @pl.when((i == 0) & (k == 0))
        def _():
            sums_ref[...] = jnp.zeros_like(sums_ref)

        @pl.when(k == 0)
        def _():
            acc[...] = jnp.zeros_like(acc)
            for cp in fetches:
                cp.start()

        acc[...] += _dot_nt(d_ref[...], w_ref[...])

        @pl.when(k == nk - 1)
        def _():
            for cp in fetches:
                cp.wait()
            g2 = v_ref[V_G2:V_G2 + 1, :]
            scale2 = v_ref[V_SCALE2:V_SCALE2 + 1, :]
            gate1 = v_ref[V_GATE1:V_GATE1 + 1, :]

            def sub(rb, carry):
                rs = pl.ds(pl.multiple_of(rb * SUB, SUB), SUB)
                dh = acc[rs, :]
                r2, xh = _rms_parts(x1_ref[rs, :])
                sums_ref[0:1, :] += jnp.sum(dh, axis=0, keepdims=True)
                sums_ref[1:2, :] += jnp.sum(dh * (xh * g2), axis=0, keepdims=True)
                dxn = dh * (1.0 + scale2)
                sums_ref[2:3, :] += jnp.sum(dxn * xh, axis=0, keepdims=True)
                dxh = dxn * g2
                dx1 = dx2_ref[rs, :] + r2 * (dxh - xh * jnp.mean(dxh * xh, axis=-1, keepdims=True))
                sums_ref[3:4, :] += jnp.sum(dx1 * o1_ref[rs, :].astype(F32), axis=0, keepdims=True)
                dx1_ref[rs, :] = dx1
                do1_ref[rs, :] = (dx1 * gate1).astype(BF16)
                return carry

            lax.fori_loop(0, tm // SUB, sub, 0)

    return pl.pallas_call(
        body, name="ff1_bwd", grid=(s // tm, nk),
        in_specs=[pl.BlockSpec((tm, tk), lambda i, k: (i, k)),
                  pl.BlockSpec((None, D, tk), lambda i, k: (k // per, 0, k % per)),
                  pl.BlockSpec(memory_space=pl.ANY), pl.BlockSpec(memory_space=pl.ANY),
                  pl.BlockSpec(memory_space=pl.ANY), pl.BlockSpec((16, D), lambda i, k: (0, 0))],
        out_specs=[pl.BlockSpec((tm, D), lambda i, k: (i, 0)), pl.BlockSpec((tm, D), lambda i, k: (i, 0)),
                   pl.BlockSpec((8, D), lambda i, k: (0, 0))],
        out_shape=[SDS((s, D), F32), SDS((s, D), BF16), SDS((8, D), F32)],
        scratch_shapes=[pltpu.VMEM((tm, D), F32), pltpu.VMEM((tm, D), F32), pltpu.VMEM((tm, D), F32),
                        pltpu.VMEM((tm, D), BF16), pltpu.SemaphoreType.DMA((3,))],
        compiler_params=_cp(2))(df, w_ff1, x1, dx2, o1, vecs)


def outproj_bwd(do1, w_out, yab, proj):
    s = do1.shape[0]
    tm, tn = min(1024, s), 512
    per = D // tn

    def body(d_ref, w_ref, y_ref, g_ref, dy_ref, dp_ref):
        dm = _dot_nt(d_ref[...], w_ref[...])
        sg = _sigmoid(g_ref[...].astype(F32))
        dy_ref[...] = (dm * sg).astype(BF16)
        dp_ref[...] = (dm * y_ref[...].astype(F32) * sg * (1.0 - sg)).astype(BF16)

    return pl.pallas_call(
        body, name="outproj_bwd", grid=(s // tm, 2 * per),
        in_specs=[pl.BlockSpec((tm, D), lambda i, j: (i, 0)), pl.BlockSpec((tn, D), lambda i, j: (j % per, 0)),
                  pl.BlockSpec((None, tm, tn), lambda i, j: (j // per, i, j % per)),
                  pl.BlockSpec((tm, tn), lambda i, j: (i, OFF_GA // tn + j))],
        out_specs=[pl.BlockSpec((None, tm, tn), lambda i, j: (j // per, i, j % per)),
                   pl.BlockSpec((tm, tn), lambda i, j: (i, OFF_GA // tn + j))],
        out_shape=[SDS((2, s, D), BF16), SDS((s, IN_W), BF16)],
        compiler_params=_cp(2))(do1, w_out, yab, proj)


def lruout_bwd(dyab, w_lru_out, rec, proj, dproj):
    s = rec.shape[0]
    tm, tn = min(1024, s), 512

    def body(d_ref, w_ref, r_ref, g_ref, dp_in, dr_ref, dp_ref):
        dya = _dot_nt(d_ref[...], w_ref[...])
        gate = g_ref[...].astype(F32)
        dr_ref[...] = dya * _gelu(gate)
        dp_ref[...] = (dya * r_ref[...] * _gelu_grad(gate)).astype(BF16)

    return pl.pallas_call(
        body, name="lruout_bwd", grid=(s // tm, D // tn),
        in_specs=[pl.BlockSpec((None, tm, D), lambda i, j: (0, i, 0)), pl.BlockSpec((tn, D), lambda i, j: (j, 0)),
                  pl.BlockSpec((tm, tn), lambda i, j: (i, j)),
                  pl.BlockSpec((tm, tn), lambda i, j: (i, OFF_GATE // tn + j)),
                  pl.BlockSpec(memory_space=pl.ANY)],
        out_specs=[pl.BlockSpec((tm, tn), lambda i, j: (i, j)),
                   pl.BlockSpec((tm, tn), lambda i, j: (i, OFF_GATE // tn + j))],
        out_shape=[SDS((s, D), F32), SDS((s, IN_W), BF16)],
        input_output_aliases={4: 1},
        compiler_params=_cp(2))(dyab, w_lru_out, rec, proj, dproj)


def attnout_bwd(dyab, w_attn_out):
    s = dyab.shape[1]
    tm, tn = min(1024, s), 512

    def body(d_ref, w_ref, o_ref):
        o_ref[...] = _dot_nt(d_ref[...], w_ref[...]).astype(BF16)

    return pl.pallas_call(
        body, name="attnout_bwd", grid=(s // tm, D // tn),
        in_specs=[pl.BlockSpec((None, tm, D), lambda i, j: (1, i, 0)), pl.BlockSpec((tn, D), lambda i, j: (j, 0))],
        out_specs=pl.BlockSpec((tm, tn), lambda i, j: (i, j)),
        out_shape=SDS((s, D), BF16), compiler_params=_cp(2))(dyab, w_attn_out)


def attn_bwd(proj, band, sinks, datt, dproj):
    s = proj.shape[0]
    nb = s // BLOCK
    qw = 1024

    def body(sk_ref, q_ref, kp_ref, kc_ref, vp_ref, vc_ref, b_ref, do_ref, dp_in,
             dq_ref, dkb_ref, dvb_ref, db_ref, ds_ref, qs_buf, dos_buf, s_buf, dp_buf, p_buf, dsc_buf):
        gp = pl.program_id(0)
        n = pl.program_id(1)

        @pl.when(n == 0)
        def _():
            db_ref[...] = jnp.zeros_like(db_ref)
            ds_ref[...] = jnp.zeros_like(ds_ref)

        valid = _band_valid(n)
        kks = _kv_bands(kp_ref, kc_ref)
        vvs = _kv_bands(vp_ref, vc_ref)
        lane_b = lax.broadcasted_iota(jnp.int32, (2 * BLOCK, 128), 1)
        dks, dvs = [], []
        for kv in range(2):
            _stack_heads(q_ref, kv, qs_buf)
            _stack_heads(do_ref, kv, dos_buf)
            s_buf[...] = _dot_nt(qs_buf[...], kks[kv])
            dp_buf[...] = _dot_nt(dos_buf[...], vvs[kv])
            for hq in range(8):
                hl = 8 * kv + hq
                rows = pl.ds(hq * BLOCK, BLOCK)
                p, ps = _attn_probs(s_buf[rows, :], b_ref[hl], sk_ref[gp * 16 + hl], valid)
                dp = dp_buf[rows, :]
                delta = jnp.sum(p * dp, axis=-1, keepdims=True)
                dsc = p * (dp - delta)
                db_ref[hl] += dsc
                ds_ref[hl:hl + 1, :] += jnp.zeros((1, 128), F32) - jnp.sum(ps * delta)
                p_buf[rows, :] = p.astype(BF16)
                dsc_buf[rows, :] = (dsc * SCALE).astype(BF16)
            _unstack_heads(_dot(dsc_buf[...], kks[kv]), dq_ref, kv)
            dk = _dot_tn(dsc_buf[...], qs_buf[...])
            dv = _dot_tn(p_buf[...], dos_buf[...])
            dks.append(dk + pltpu.roll(dk, 64, 1))
            dvs.append(dv + pltpu.roll(dv, 64, 1))
        dkb_ref[...] = jnp.where(lane_b < 64, dks[0], dks[1])
        dvb_ref[...] = jnp.where(lane_b < 64, dvs[0], dvs[1])

    kb, vb = OFF_K // 128, OFF_V // 128
    return pl.pallas_call(
        body, name="attn_bwd", grid=(2, nb),
        in_specs=[pl.BlockSpec(memory_space=pltpu.SMEM),
                  pl.BlockSpec((BLOCK, qw), lambda g, n: (n, OFF_Q // qw + g)),
                  pl.BlockSpec((BLOCK, 128), lambda g, n: (jnp.maximum(n - 1, 0), kb + g)),
                  pl.BlockSpec((BLOCK, 128), lambda g, n: (n, kb + g)),
                  pl.BlockSpec((BLOCK, 128), lambda g, n: (jnp.maximum(n - 1, 0), vb + g)),
                  pl.BlockSpec((BLOCK, 128), lambda g, n: (n, vb + g)),
                  pl.BlockSpec((16, BLOCK, 2 * BLOCK), lambda g, n: (g, 0, 0)),
                  pl.BlockSpec((BLOCK, qw), lambda g, n: (n, g)),
                  pl.BlockSpec(memory_space=pl.ANY)],
        out_specs=[pl.BlockSpec((BLOCK, qw), lambda g, n: (n, OFF_Q // qw + g)),
                   pl.BlockSpec((2 * BLOCK, 128), lambda g, n: (n, g)),
                   pl.BlockSpec((2 * BLOCK, 128), lambda g, n: (n, g)),
                   pl.BlockSpec((16, BLOCK, 2 * BLOCK), lambda g, n: (g, 0, 0)),
                   pl.BlockSpec((16, 128), lambda g, n: (g, 0))],
        out_shape=[SDS((s, IN_W), BF16), SDS((nb * 2 * BLOCK, 256), F32), SDS((nb * 2 * BLOCK, 256), F32),
                   SDS((N_HEADS, BLOCK, 2 * BLOCK), F32), SDS((N_HEADS, 128), F32)],
        input_output_aliases={8: 0},
        scratch_shapes=[pltpu.VMEM((8 * BLOCK, 128), BF16), pltpu.VMEM((8 * BLOCK, 128), BF16),
                        pltpu.VMEM((8 * BLOCK, 2 * BLOCK), F32), pltpu.VMEM((8 * BLOCK, 2 * BLOCK), F32),
                        pltpu.VMEM((8 * BLOCK, 2 * BLOCK), BF16), pltpu.VMEM((8 * BLOCK, 2 * BLOCK), BF16)],
        compiler_params=_cp(2))(sinks, proj, proj, proj, proj, proj, band, datt, dproj)


def dkv_combine(dkb, dvb, dproj):
    nb = dkb.shape[0] // (2 * BLOCK)
    s = nb * BLOCK
    dkb3 = dkb.reshape(nb, 2 * BLOCK, 256)
    dvb3 = dvb.reshape(nb, 2 * BLOCK, 256)

    def body(k1, k2, v1, v2, dp_in, o_ref):
        nxt = jnp.where(pl.program_id(0) < nb - 1, 1.0, 0.0)
        o_ref[:, 0:256] = (k1[...] + nxt * k2[...]).astype(BF16)
        o_ref[:, 256:512] = (v1[...] + nxt * v2[...]).astype(BF16)

    spec1 = pl.BlockSpec((None, BLOCK, 256), lambda m: (m, 1, 0))
    spec2 = pl.BlockSpec((None, BLOCK, 256), lambda m: (jnp.minimum(m + 1, nb - 1), 0, 0))
    return pl.pallas_call(
        body, name="dkv_combine", grid=(nb,),
        in_specs=[spec1, spec2, spec1, spec2, pl.BlockSpec(memory_space=pl.ANY)],
        out_specs=pl.BlockSpec((BLOCK, 512), lambda m: (m, OFF_K // 512)),
        out_shape=SDS((s, IN_W), BF16), input_output_aliases={4: 0},
        compiler_params=_cp(1))(dkb3, dkb3, dvb3, dvb3, dproj)


def lru_bwd(proj, rec, drec, lvec, wa, wx, dproj):
    s = proj.shape[0]
    t = min(256, s)
    nt = s // t

    def body(lx_ref, lxh_ref, rec_ref, rech_ref, dr_ref, lv_ref, wa_ref, wx_ref, dp_in,
             dlx_ref, sums_ref, dwa_ref, dwx_ref,
             xbuf, hbuf, dxbuf, a_s, dh_s, xc_s, r_s, ig_s, mu_s, gc):
        step_i = pl.program_id(0)
        ti = nt - 1 - step_i

        @pl.when(step_i == 0)
        def _():
            sums_ref[...] = jnp.zeros_like(sums_ref)
            dwa_ref[...] = jnp.zeros_like(dwa_ref)
            dwx_ref[...] = jnp.zeros_like(dwx_ref)
            dxbuf[pl.ds(t, 8), :] = jnp.zeros((8, D), F32)
            gc[...] = jnp.zeros((8, D), F32)

        live = jnp.where(ti > 0, 1.0, 0.0)
        xbuf[pl.ds(0, 8), :] = lxh_ref[...].astype(F32)[8:16] * live
        xbuf[pl.ds(8, t), :] = lx_ref[...].astype(F32)
        hbuf[pl.ds(0, 8), :] = rech_ref[...] * live
        hbuf[pl.ds(8, t), :] = rec_ref[...]
        first = (lax.broadcasted_iota(jnp.int32, (t, 128), 0) + ti * t) == 0
        for b in range(N_LRU_BLOCKS):
            cs = slice(b * 128, (b + 1) * 128)
            _, xc, _, r, ig, _, a, mult = _lru_block_fwd(xbuf, lv_ref, wa_ref, wx_ref, b, t, first)
            a_s[:, cs] = a
            xc_s[:, cs] = xc
            r_s[:, cs] = r
            ig_s[:, cs] = ig
            mu_s[:, cs] = mult

        def step(q, g):
            tt = t - 1 - q
            dh = dr_ref[pl.ds(tt, 1), :] + g
            dh_s[pl.ds(tt, 1), :] = dh
            return a_s[pl.ds(tt, 1), :] * dh

        gc[0:1, :] = lax.fori_loop(0, t, step, gc[0:1, :], unroll=8)
        for b in range(N_LRU_BLOCKS):
            cs = slice(b * 128, (b + 1) * 128)
            dh = dh_s[:, cs]
            a = a_s[:, cs]
            xc = xc_s[:, cs]
            r = r_s[:, cs]
            ig = ig_s[:, cs]
            mult = mu_s[:, cs]
            sp = _softplus(-lv_ref[L_LAM:L_LAM + 1, cs])
            lam = lv_ref[L_LAM:L_LAM + 1, cs]
            da = dh * hbuf[pl.ds(7, t), cs]
            dmult = jnp.where(first, 0.0, dh * ig * xc)
            dig = dh * mult * xc
            dxc = dh * mult * ig
            dlog_a = da * a - dmult * (a * a) / mult
            dr = dlog_a * ((-LRU_C) * sp)
            dsp = jnp.sum(dlog_a * ((-LRU_C) * r), axis=0, keepdims=True)
            dza = dr * r * (1.0 - r)
            dzx = dig * ig * (1.0 - ig)
            dzab = dza.astype(BF16)
            dzxb = dzx.astype(BF16)
            xcb = xc.astype(BF16)
            dwa_ref[b] += _dot_tn(xcb, dzab)
            dwx_ref[b] += _dot_tn(xcb, dzxb)
            dxc = dxc + _dot_nt(dzab, wa_ref[b]) + _dot_nt(dzxb, wx_ref[b])
            sums_ref[L_LAM:L_LAM + 1, cs] += dsp * (-jax.nn.sigmoid(-lam))
            sums_ref[L_BA:L_BA + 1, cs] += jnp.sum(dza, axis=0, keepdims=True)
            sums_ref[L_BX:L_BX + 1, cs] += jnp.sum(dzx, axis=0, keepdims=True)
            sums_ref[L_CB:L_CB + 1, cs] += jnp.sum(dxc, axis=0, keepdims=True)
            for kk in range(4):
                sums_ref[kk:kk + 1, cs] += jnp.sum(dxc * xbuf[pl.ds(5 + kk, t), cs], axis=0, keepdims=True)
            dxbuf[pl.ds(0, t), cs] = dxc
            dlx = (lv_ref[3:4, cs] * dxc + lv_ref[2:3, cs] * dxbuf[pl.ds(1, t), cs]
                   + lv_ref[1:2, cs] * dxbuf[pl.ds(2, t), cs] + lv_ref[0:1, cs] * dxbuf[pl.ds(3, t), cs])
            dlx_ref[:, cs] = dlx.astype(BF16)
        dxbuf[pl.ds(t, 8), :] = dxbuf[pl.ds(0, 8), :]

    rev = lambda i: nt - 1 - i
    return pl.pallas_call(
        body, name="lru_bwd", grid=(nt,),
        in_specs=[pl.BlockSpec((t, D), lambda i: (rev(i), 0)),
                  pl.BlockSpec((16, D), lambda i: (jnp.maximum(rev(i) * (t // 16) - 1, 0), 0)),
                  pl.BlockSpec((t, D), lambda i: (rev(i), 0)),
                  pl.BlockSpec((8, D), lambda i: (jnp.maximum(rev(i) * (t // 8) - 1, 0), 0)),
                  pl.BlockSpec((t, D), lambda i: (rev(i), 0)),
                  pl.BlockSpec((8, D), lambda i: (0, 0)),
                  pl.BlockSpec((N_LRU_BLOCKS, 128, 128), lambda i: (0, 0, 0)),
                  pl.BlockSpec((N_LRU_BLOCKS, 128, 128), lambda i: (0, 0, 0)),
                  pl.BlockSpec(memory_space=pl.ANY)],
        out_specs=[pl.BlockSpec((t, D), lambda i: (rev(i), 0)),
                   pl.BlockSpec((8, D), lambda i: (0, 0)),
                   pl.BlockSpec((N_LRU_BLOCKS, 128, 128), lambda i: (0, 0, 0)),
                   pl.BlockSpec((N_LRU_BLOCKS, 128, 128), lambda i: (0, 0, 0))],
        out_shape=[SDS((s, IN_W), BF16), SDS((8, D), F32), SDS((N_LRU_BLOCKS, 128, 128), F32),
                   SDS((N_LRU_BLOCKS, 128, 128), F32)],
        scratch_shapes=[pltpu.VMEM((t + 8, D), F32), pltpu.VMEM((t + 8, D), F32), pltpu.VMEM((t + 8, D), F32)]
        + [pltpu.VMEM((t, D), F32)] * 6 + [pltpu.VMEM((8, D), F32)],
        input_output_aliases={8: 0},
        compiler_params=_cp(1))(proj, proj, rec, rec, drec, lvec, wa, wx, dproj)


def inproj_bwd(dproj, w_in, x, dx1, vecs):
    s = x.shape[0]
    tm, tk = min(512, s), IN_TILE
    nk = IN_W // tk
    per = IN_SHARD // tk

    def body(d_ref, w_ref, x_hbm, dx1_hbm, v_ref, gx_ref, sums_ref, acc, x_ref, dx1_ref, sems):
        i, k = pl.program_id(0), pl.program_id(1)
        fetches = _row_fetches((x_hbm, dx1_hbm), (x_ref, dx1_ref), sems, i, tm)

        @pl.when((i == 0) & (k == 0))
        def _():
            sums_ref[...] = jnp.zeros_like(sums_ref)

        @pl.when(k == 0)
        def _():
            acc[...] = jnp.zeros_like(acc)
            for cp in fetches:
                cp.start()

        acc[...] += _dot_nt(d_ref[...], w_ref[...])

        @pl.when(k == nk - 1)
        def _():
            for cp in fetches:
                cp.wait()
            g1 = v_ref[V_G1:V_G1 + 1, :]
            scale1 = v_ref[V_SCALE1:V_SCALE1 + 1, :]

            def sub(rb, carry):
                rs = pl.ds(pl.multiple_of(rb * SUB, SUB), SUB)
                dh = acc[rs, :]
                r1, xh = _rms_parts(x_ref[rs, :])
                sums_ref[0:1, :] += jnp.sum(dh, axis=0, keepdims=True)
                sums_ref[1:2, :] += jnp.sum(dh * (xh * g1), axis=0, keepdims=True)
                dxn = dh * (1.0 + scale1)
                sums_ref[2:3, :] += jnp.sum(dxn * xh, axis=0, keepdims=True)
                dxh = dxn * g1
                gx_ref[rs, :] = dx1_ref[rs, :] + r1 * (dxh - xh * jnp.mean(dxh * xh, axis=-1, keepdims=True))
                return carry

            lax.fori_loop(0, tm // SUB, sub, 0)

    return pl.pallas_call(
        body, name="inproj_bwd", grid=(s // tm, nk),
        in_specs=[pl.BlockSpec((tm, tk), lambda i, k: (i, k)),
                  pl.BlockSpec((None, D, tk), lambda i, k: (k // per, 0, k % per)),
                  pl.BlockSpec(memory_space=pl.ANY), pl.BlockSpec(memory_space=pl.ANY),
                  pl.BlockSpec((16, D), lambda i, k: (0, 0))],
        out_specs=[pl.BlockSpec((tm, D), lambda i, k: (i, 0)), pl.BlockSpec((8, D), lambda i, k: (0, 0))],
        out_shape=[SDS((s, D), F32), SDS((8, D), F32)],
        scratch_shapes=[pltpu.VMEM((tm, D), F32), pltpu.VMEM((tm, D), F32), pltpu.VMEM((tm, D), F32),
                        pltpu.SemaphoreType.DMA((2,))],
        compiler_params=_cp(2))(dproj, w_in, x, dx1, vecs)


def mod_columns(c16, w_ada, b_cols):
    tn = 512

    def body(c_ref, w_ref, b_ref, o_ref):
        cv = c_ref[...]
        ca = (cv * jax.nn.sigmoid(cv)).astype(BF16)
        o_ref[...] = _dot(ca, w_ref[...].astype(BF16)) + b_ref[...]

    return pl.pallas_call(
        body, name="mod_columns", grid=(ADA_SHARD // tn,),
        in_specs=[pl.BlockSpec((16, D), lambda j: (0, 0)), pl.BlockSpec((D, tn), lambda j: (0, j)),
                  pl.BlockSpec((1, tn), lambda j: (0, j))],
        out_specs=pl.BlockSpec((16, tn), lambda j: (0, j)),
        out_shape=SDS((16, ADA_SHARD), F32), compiler_params=_cp(1))(c16, w_ada, b_cols)


def wada_update(c16, dmod16, w, m, v):
    tm, tn = 512, 512

    def body(c_ref, d_ref, w_ref, m_ref, v_ref, g_out, dl_out, m_out, v_out):
        cv = c_ref[...]
        ca = (cv * jax.nn.sigmoid(cv)).astype(BF16)
        g = _dot_tn(ca, d_ref[...].astype(BF16))
        dl, m2, v2 = _adamw_math(w_ref[...], g, m_ref[...], v_ref[...])
        g_out[...] = g
        dl_out[...] = dl
        m_out[...] = m2
        v_out[...] = v2

    tile = pl.BlockSpec((tm, tn), lambda i, j: (i, j))
    return pl.pallas_call(
        body, name="wada_update", grid=(D // tm, ADA_SHARD // tn),
        in_specs=[pl.BlockSpec((16, tm), lambda i, j: (0, i)), pl.BlockSpec((16, tn), lambda i, j: (0, j)),
                  tile, tile, tile],
        out_specs=[tile] * 4, out_shape=[SDS((D, ADA_SHARD), F32)] * 4,
        compiler_params=_cp(2))(c16, dmod16, w, m, v)


def adamw_big(name, w, mine, theirs, m, v, c_idx):
    r, c = w.shape
    tr = 128
    per = (r // 2) // tr

    def body(c_ref, w_ref, a_ref, b_ref, m_ref, v_ref, g_out, dl_out, m_out, v_out):
        own = (pl.program_id(0) // per) == c_ref[0]
        g = jnp.where(own, a_ref[...], b_ref[...])
        dl, m2, v2 = _adamw_math(w_ref[...], g, m_ref[...], v_ref[...])
        g_out[...] = g
        dl_out[...] = dl
        m_out[...] = m2
        v_out[...] = v2

    tile = pl.BlockSpec((tr, c), lambda i, cr: (i, 0))
    half = pl.BlockSpec((tr, c), lambda i, cr: (i % per, 0))
    gs = pltpu.PrefetchScalarGridSpec(num_scalar_prefetch=1, grid=(r // tr,),
                                      in_specs=[tile, half, half, tile, tile], out_specs=[tile] * 4)
    return pl.pallas_call(body, name=name, grid_spec=gs, out_shape=[SDS((r, c), F32)] * 4,
                          compiler_params=_cp(1))(c_idx, w, mine, theirs, m, v)


def cast_into_slot(name, w, k_idx):
    r, c = w.shape
    tr = 256

    def body(k_ref, w_ref, o_ref):
        o_ref[...] = w_ref[...].astype(BF16)

    gs = pltpu.PrefetchScalarGridSpec(
        num_scalar_prefetch=1, grid=(r // tr,),
        in_specs=[pl.BlockSpec((tr, c), lambda i, kr: (i, 0))],
        out_specs=pl.BlockSpec((None, tr, c), lambda i, kr: (kr[0], i, 0)))
    return pl.pallas_call(body, name=name, grid_spec=gs, out_shape=SDS((N_CHIPS, r, c), BF16),
                          compiler_params=_cp(1))(k_idx, w)


def adamw_small(ws, gs, ms, vs):
    n = len(ws)

    def body(*refs):
        for i in range(n):
            dl, m2, v2 = _adamw_math(refs[i][...], refs[n + i][...], refs[2 * n + i][...], refs[3 * n + i][...])
            refs[4 * n + i][...] = dl
            refs[5 * n + i][...] = m2
            refs[6 * n + i][...] = v2

    vm = pl.BlockSpec(memory_space=pltpu.VMEM)
    shapes = [SDS(w.shape, F32) for w in ws]
    outs = pl.pallas_call(
        body, name="adamw_small", in_specs=[vm] * (4 * n), out_specs=[vm] * (3 * n), out_shape=shapes * 3,
        compiler_params=pltpu.CompilerParams(vmem_limit_bytes=VMEM_LIMIT))(*ws, *gs, *ms, *vs)
    return outs[:n], outs[n:2 * n], outs[2 * n:]


def sum_devices(name, gathered):
    rows, cols = gathered.shape[1:]
    tr = min(rows, 128 * D // cols)

    def body(x_ref, o_ref):
        acc = x_ref[0].astype(F32)
        for d in range(1, N_DEV):
            acc = acc + x_ref[d].astype(F32)
        o_ref[...] = acc

    return pl.pallas_call(
        body, name=name, grid=(rows // tr,),
        in_specs=[pl.BlockSpec((N_DEV, tr, cols), lambda i: (0, i, 0))],
        out_specs=pl.BlockSpec((tr, cols), lambda i: (i, 0)),
        out_shape=SDS((rows, cols), F32), compiler_params=_cp(1))(gathered)


def _mesh_pos():
    return lax.axis_index("x"), lax.axis_index("y"), lax.axis_index("c")


def _other_chips(x, y):
    return [(1 - x, y), (x, 1 - y), (1 - x, 1 - y)]


def all_gather_small(name, block):
    m_per, n = block.shape

    def body(x_ref, out_ref, send_sems, recv_sems, local_sem):
        x, y, c = _mesh_pos()
        me, sibling = (x, y, c), (x, y, 1 - c)
        chips = _other_chips(x, y)

        def rows(px, py, pc):
            return out_ref.at[pl.ds((4 * px + 2 * py + pc) * m_per, m_per), :]

        def copy(k, blk, to, src=None):
            return pltpu.make_async_remote_copy(
                src_ref=rows(*blk) if src is None else src, dst_ref=rows(*blk),
                send_sem=send_sems.at[k], recv_sem=recv_sems.at[k], device_id=to, device_id_type=MESH)

        mine = pltpu.make_async_copy(x_ref, rows(*me), local_sem)
        mine.start()
        first = [copy(0, me, sibling, src=x_ref)]
        first += [copy(1 + j, me, (*chip, c), src=x_ref) for j, chip in enumerate(chips)]
        for cp in first:
            cp.start()
        passed = [copy(4 + j, (*chip, c), sibling) for j, chip in enumerate(chips)]
        for j, chip in enumerate(chips):
            copy(1 + j, (*chip, c), me).wait_recv()
            passed[j].start()
        copy(0, sibling, me).wait_recv()
        for j, chip in enumerate(chips):
            copy(4 + j, (*chip, 1 - c), me).wait_recv()
        for cp in first + passed:
            cp.wait_send()
        mine.wait()

    vm = pl.BlockSpec(memory_space=pltpu.VMEM)
    return pl.pallas_call(
        body, name=name, out_shape=SDS((N_DEV * m_per, n), block.dtype), in_specs=[vm], out_specs=vm,
        scratch_shapes=[pltpu.SemaphoreType.DMA((7,)), pltpu.SemaphoreType.DMA((7,)), pltpu.SemaphoreType.DMA],
        compiler_params=pltpu.CompilerParams(vmem_limit_bytes=VMEM_LIMIT))(block)


def sibling_sum(name, grad, other, c_idx):
    _, r, cc = grad.shape
    h = r // 2
    tr = min(256, h)
    g4 = grad.reshape(N_CHIPS, 2, h, cc)

    def body(c_ref, a_ref, b_ref, o_ref):
        o_ref[...] = (a_ref[...].astype(F32) + b_ref[...].astype(F32)).astype(BF16)

    gs = pltpu.PrefetchScalarGridSpec(
        num_scalar_prefetch=1, grid=(N_CHIPS, h // tr),
        in_specs=[pl.BlockSpec((None, None, tr, cc), lambda s, i, cr: (s, cr[0], i, 0)),
                  pl.BlockSpec((None, tr, cc), lambda s, i, cr: (s, i, 0))],
        out_specs=pl.BlockSpec((None, tr, cc), lambda s, i, cr: (s, i, 0)))
    return pl.pallas_call(body, name=name, grid_spec=gs, out_shape=SDS((N_CHIPS, h, cc), BF16),
                          compiler_params=_cp(2))(c_idx, g4, other)


HBM_SPEC = pl.BlockSpec(memory_space=pltpu.HBM)
SEM_SPEC = pl.BlockSpec(memory_space=pltpu.SEMAPHORE)


def _side_effecting():
    return pltpu.CompilerParams(has_side_effects=pltpu.SideEffectType.DATAFLOW_SIDE_EFFECTING)


def _in_hbm(a):
    return pltpu.with_memory_space_constraint(a, pltpu.HBM)


ALL_CHIPS = (0, 1, 2)


def gather_start(name, bufs, after, rel=ALL_CHIPS, carry=None):
    n = len(bufs)
    nr = len(rel)
    halves = [w.shape[1] // 2 for w in bufs]
    extra = [] if carry is None else [carry]

    def body(*refs):
        ins = refs[:n]
        send_sems, recv_sems, token = refs[n + 1 + len(extra)], refs[n + 2 + len(extra)], refs[-1]
        x, y, c = _mesh_pos()
        k = 2 * x + y
        for i in range(n):
            reg = ins[i].at[k, pl.ds(c * halves[i], halves[i]), :]
            for q, j in enumerate(rel):
                chip = _other_chips(x, y)[j]
                pltpu.make_async_remote_copy(src_ref=reg, dst_ref=reg, send_sem=send_sems.at[nr * i + q],
                                             recv_sem=recv_sems.at[nr * i + q], device_id=(*chip, c),
                                             device_id_type=MESH).start()
        token[...] = jnp.zeros_like(token)

    outs = pl.pallas_call(
        body, name=name,
        out_shape=(pltpu.SemaphoreType.DMA((nr * n,)), pltpu.SemaphoreType.DMA((nr * n,)),
                   *[pltpu.HBM(w.shape, w.dtype) for w in list(bufs) + extra], SDS((8, 128), F32)),
        in_specs=[HBM_SPEC] * n + [pl.BlockSpec(memory_space=pl.ANY)] + [HBM_SPEC] * len(extra),
        out_specs=(SEM_SPEC, SEM_SPEC, *[HBM_SPEC] * (n + len(extra)), pl.BlockSpec(memory_space=pltpu.VMEM)),
        input_output_aliases={**{i: 2 + i for i in range(n)}, **({n + 1: 2 + n} if extra else {})},
        compiler_params=_side_effecting())(*[_in_hbm(w) for w in bufs], after, *[_in_hbm(w) for w in extra])
    return (outs[0], outs[1], list(outs[2:2 + n]), outs[-1]) + ((outs[2 + n],) if extra else ())


def gather_wait(name, send_sems, recv_sems, bufs, after, rel=ALL_CHIPS):
    n = len(bufs)
    nr = len(rel)
    halves = [w.shape[1] // 2 for w in bufs]

    def body(*refs):
        ins = refs[:n]
        send_sems, recv_sems = refs[n], refs[n + 1]
        x, y, c = _mesh_pos()
        k = 2 * x + y
        for i in range(n):
            for q, j in enumerate(rel):
                chip = _other_chips(x, y)[j]
                kj = 2 * chip[0] + chip[1]
                cp = pltpu.make_async_remote_copy(
                    src_ref=ins[i].at[k, pl.ds(c * halves[i], halves[i]), :],
                    dst_ref=ins[i].at[kj, pl.ds(c * halves[i], halves[i]), :],
                    send_sem=send_sems.at[nr * i + q], recv_sem=recv_sems.at[nr * i + q], device_id=(*chip, c),
                    device_id_type=MESH)
                cp.wait_send()
                cp.wait_recv()

    return pl.pallas_call(
        body, name=name, out_shape=[pltpu.HBM(w.shape, w.dtype) for w in bufs],
        in_specs=[HBM_SPEC] * n + [SEM_SPEC, SEM_SPEC, pl.BlockSpec(memory_space=pl.ANY)],
        out_specs=[HBM_SPEC] * n, input_output_aliases={i: i for i in range(n)},
        compiler_params=_side_effecting())(*bufs, send_sems, recv_sems, after)


def gather_forward(name, bufs, rel=ALL_CHIPS):
    n = len(bufs)
    halves = [w.shape[1] // 2 for w in bufs]

    def body(*refs):
        outs = refs[n:2 * n]
        send_sems, recv_sems = refs[2 * n:]
        x, y, c = _mesh_pos()
        chips = _other_chips(x, y)

        def copy(i, j, half, to):
            kj = 2 * chips[j][0] + chips[j][1]
            reg = outs[i].at[kj, pl.ds(half * halves[i], halves[i]), :]
            return pltpu.make_async_remote_copy(src_ref=reg, dst_ref=reg, send_sem=send_sems.at[i, j],
                                                recv_sem=recv_sems.at[i, j], device_id=to, device_id_type=MESH)

        cps = [copy(i, j, c, (x, y, 1 - c)) for i in range(n) for j in rel]
        for cp in cps:
            cp.start()
        for i in range(n):
            for j in rel:
                copy(i, j, 1 - c, (x, y, c)).wait_recv()
        for cp in cps:
            cp.wait_send()

    hbm = pl.BlockSpec(memory_space=pl.ANY)
    return pl.pallas_call(
        body, name=name, in_specs=[hbm] * n, out_specs=[hbm] * n,
        out_shape=[SDS(w.shape, w.dtype) for w in bufs], input_output_aliases={i: i for i in range(n)},
        scratch_shapes=[pltpu.SemaphoreType.DMA((n, 3)), pltpu.SemaphoreType.DMA((n, 3))])(*bufs)


def _exchange_plan(kind, srcs, zones):
    x, y, c = _mesh_pos()
    plan = []
    for src, zone in zip(srcs, zones):
        if kind == "chips":
            for j, chip in enumerate(_other_chips(x, y)):
                plan.append((src.at[2 * chip[0] + chip[1]], zone.at[j], (*chip, c)))
        elif kind == "sibling":
            h = zone.shape[1]
            plan.append((src.at[:, pl.ds((1 - c) * h, h), :], zone, (x, y, 1 - c)))
        else:
            peers = [(x, y, 1 - c)] + [(*chip, cc) for chip in _other_chips(x, y) for cc in (c, 1 - c)]
            plan += [(src, zone.at[4 * x + 2 * y + c], peer) for peer in peers]
    return plan


_COPIES_PER_ARRAY = {"chips": 3, "sibling": 1, "all": N_DEV - 1}


def _landing_zones(kind, srcs):
    if kind == "chips":
        return [lax.empty((3,) + t.shape[1:], t.dtype) for t in srcs]
    if kind == "sibling":
        return [lax.empty((t.shape[0], t.shape[1] // 2, t.shape[2]), t.dtype) for t in srcs]
    return [jnp.broadcast_to(t, (N_DEV,) + t.shape) for t in srcs]


def exchange_start(name, kind, srcs, after):
    n = len(srcs)
    lands = _landing_zones(kind, srcs)
    n_copies = n * _COPIES_PER_ARRAY[kind]

    def body(*refs):
        send_sems, recv_sems, token = refs[2 * n + 1], refs[2 * n + 2], refs[-1]
        for q, (src, dst, dev) in enumerate(_exchange_plan(kind, refs[:n], refs[n:2 * n])):
            pltpu.make_async_remote_copy(src_ref=src, dst_ref=dst, send_sem=send_sems.at[q], recv_sem=recv_sems.at[q],
                                         device_id=dev, device_id_type=MESH).start()
        token[...] = jnp.zeros_like(token)

    outs = pl.pallas_call(
        body, name=name,
        out_shape=(pltpu.SemaphoreType.DMA((n_copies,)), pltpu.SemaphoreType.DMA((n_copies,)),
                   *[pltpu.HBM(t.shape, t.dtype) for t in srcs], *[pltpu.HBM(t.shape, t.dtype) for t in lands],
                   SDS((8, 128), F32)),
        in_specs=[HBM_SPEC] * (2 * n) + [pl.BlockSpec(memory_space=pl.ANY)],
        out_specs=(SEM_SPEC, SEM_SPEC, *[HBM_SPEC] * (2 * n), pl.BlockSpec(memory_space=pltpu.VMEM)),
        input_output_aliases={i: 2 + i for i in range(2 * n)},
        compiler_params=_side_effecting())(*[_in_hbm(t) for t in srcs], *[_in_hbm(t) for t in lands], after)
    return outs[0], outs[1], list(outs[2:2 + n]), list(outs[2 + n:2 + 2 * n]), outs[-1]


def exchange_wait(name, kind, send_sems, recv_sems, srcs, lands, after):
    n = len(srcs)

    def body(*refs):
        send_sems, recv_sems = refs[2 * n], refs[2 * n + 1]
        for q, (src, dst, dev) in enumerate(_exchange_plan(kind, refs[:n], refs[n:2 * n])):
            cp = pltpu.make_async_remote_copy(src_ref=src, dst_ref=dst, send_sem=send_sems.at[q],
                                              recv_sem=recv_sems.at[q], device_id=dev, device_id_type=MESH)
            cp.wait_send()
            cp.wait_recv()

    outs = pl.pallas_call(
        body, name=name, out_shape=[pltpu.HBM(t.shape, t.dtype) for t in srcs + lands],
        in_specs=[HBM_SPEC] * (2 * n) + [SEM_SPEC, SEM_SPEC, pl.BlockSpec(memory_space=pl.ANY)],
        out_specs=[HBM_SPEC] * (2 * n), input_output_aliases={i: i for i in range(2 * n)},
        compiler_params=_side_effecting())(*srcs, *lands, send_sems, recv_sems, after)
    return list(outs[:n]), list(outs[n:])


def chip_sum(name, sums, parts, k_idx):
    _, h, cc = parts.shape
    tr = min(256, h)

    def body(k_ref, own_ref, p_ref, o_ref):
        acc = own_ref[...].astype(F32)
        for s in range(3):
            acc = acc + p_ref[s].astype(F32)
        o_ref[...] = acc

    gs = pltpu.PrefetchScalarGridSpec(
        num_scalar_prefetch=1, grid=(h // tr,),
        in_specs=[pl.BlockSpec((None, tr, cc), lambda i, kr: (kr[0], i, 0)),
                  pl.BlockSpec((3, tr, cc), lambda i, kr: (0, i, 0))],
        out_specs=pl.BlockSpec((tr, cc), lambda i, kr: (i, 0)))
    return pl.pallas_call(body, name=name, grid_spec=gs, out_shape=SDS((h, cc), F32),
                          compiler_params=_cp(1))(k_idx, sums, parts)


def halves_exchange(name, halves):
    n = len(halves)

    def body(*refs):
        ins, outs = refs[:n], refs[n:2 * n]
        send_sems, recv_sems = refs[2 * n:]
        x, y, c = _mesh_pos()
        cps = []
        for i in range(n):
            cp = pltpu.make_async_remote_copy(
                src_ref=ins[i], dst_ref=outs[i], send_sem=send_sems.at[i], recv_sem=recv_sems.at[i],
                device_id=(x, y, 1 - c), device_id_type=MESH)
            cp.start()
            cps.append(cp)
        for cp in cps:
            cp.wait_recv()
        for cp in cps:
            cp.wait_send()

    hbm = pl.BlockSpec(memory_space=pl.ANY)
    return pl.pallas_call(
        body, name=name, in_specs=[hbm] * n, out_specs=[hbm] * n,
        out_shape=[SDS(t.shape, F32) for t in halves],
        scratch_shapes=[pltpu.SemaphoreType.DMA((n,)), pltpu.SemaphoreType.DMA((n,))])(*halves)


def local_step(x, tgt, vecs, lvec, wa, wx, sinks, rel_bias, proj, h, w_in, rest_weights, hook):
    buckets = t5_bucket_table()
    band = bias_band(rel_bias.T, buckets).reshape(N_HEADS, BLOCK, 2 * BLOCK)

    ya, rec, ya_t = lru_fwd(proj, lvec, wa, wx)
    att, att_t = attn_fwd(proj, band, sinks)
    w_lru_out, w_attn_out, w_out = rest_weights("mix", att[:8, :128] + ya[:8, :128])
    w_lru_out2, w_attn_out2, w_out2 = w_lru_out.reshape(D, D), w_attn_out.reshape(D, D), w_out.reshape(D, D)
    yab, merged, merged_t = merge_fwd(ya, att, w_lru_out2, w_attn_out2, proj)
    x1, o1 = outproj_fwd(merged, w_out2, x, vecs)
    w_ff1, w_ff2 = rest_weights("ff", o1[:8, :128])
    w_ff2_2 = w_ff2.reshape(D_FF, D)
    f, h2, fft = ff1_fwd(x1, vecs, w_ff1)
    dx2, do2, sums_f, loss = ff2_loss(f, w_ff2_2, x1, tgt, vecs)

    df = ff2_bwd(do2, w_ff2_2, f)
    g_ff2 = matmul_bf16("dw_ff2", fft, do2, WG_TM, 512)
    dx1, do1, sums_2 = ff1_bwd(df, w_ff1, x1, dx2, o1, vecs)
    g_ff1 = weight_grad("dw_ff1", h2, df, 512, (N_CHIPS, D, D), (None, WG_TM, 512), lambda i, j: (j // 4, i, j % 4))
    dyab, dproj = outproj_bwd(do1, w_out2, yab, proj)
    g_out = matmul_bf16("dw_out", merged_t, do1, WG_TM, 512)
    drec, dproj = lruout_bwd(dyab, w_lru_out2, rec, proj, dproj)
    dyab2 = dyab.reshape(2 * x.shape[0], D)
    g_lru_out = matmul_bf16("dw_lru_out", ya_t, dyab2, WG_TM, 512)
    datt = attnout_bwd(dyab, w_attn_out2)
    g_attn_out = matmul_bf16("dw_attn_out", att_t, dyab2, WG_TM, 512, b_part=1)
    zero = hook("grads_a", [g_lru_out.reshape(N_CHIPS, D // 4, D), g_attn_out.reshape(N_CHIPS, D // 4, D),
                            g_out.reshape(N_CHIPS, D // 4, D), g_ff1, g_ff2.reshape(N_CHIPS, D_FF // 4, D)])
    dproj, dkb, dvb, dband, dsink = attn_bwd(proj, band, sinks + zero, datt, dproj)
    zero = hook("after_attn_bwd", dkb)
    dproj = dkv_combine(dkb, dvb, dproj)
    dproj, sums_l, d_wa, d_wx = lru_bwd(proj, rec, drec, lvec + zero, wa, wx, dproj)
    hook("lru_grads", (d_wa, d_wx))
    per = IN_SHARD // IN_TILE
    g_in = weight_grad("dw_in", h, dproj, IN_TILE, (N_CHIPS, D, IN_SHARD), (None, WG_TM, IN_TILE),
                       lambda i, j: (j // per, i, j % per))
    zero = hook("grads_b", [g_in])
    grad_x, sums_1 = inproj_bwd(dproj, w_in, x, dx1, vecs + zero)
    d_rel_bias = bias_band_bwd(dband.reshape(N_HEADS, BLOCK * 2 * BLOCK), buckets)

    small = dict(sums_f=sums_f, sums_2=sums_2, sums_1=sums_1, sums_l=sums_l, d_wa=d_wa, d_wx=d_wx,
                 d_sinks=dsink[:, 0], d_rel_bias=d_rel_bias)
    return loss, grad_x, small


def _pad_rows(a, rows):
    return jnp.concatenate([a, jnp.zeros((rows - a.shape[0], a.shape[1]), a.dtype)], axis=0)


def kernel(x, c, w_ada, b_ada, norm1_g, w_in, conv_w, conv_b, lru_wa, lru_ba, lru_wx, lru_bx, lru_lambda, w_lru_out, w_attn_out, attn_sinks, rel_bias, w_out, norm2_g, w_ff1, w_ff2, final_g, loss_target, m_w_ada, m_b_ada, m_norm1_g, m_w_in, m_conv_w, m_conv_b, m_lru_wa, m_lru_ba, m_lru_wx, m_lru_bx, m_lru_lambda, m_w_lru_out, m_w_attn_out, m_attn_sinks, m_rel_bias, m_w_out, m_norm2_g, m_w_ff1, m_w_ff2, m_final_g, v_w_ada, v_b_ada, v_norm1_g, v_w_in, v_conv_w, v_conv_b, v_lru_wa, v_lru_ba, v_lru_wx, v_lru_bx, v_lru_lambda, v_w_lru_out, v_w_attn_out, v_attn_sinks, v_rel_bias, v_w_out, v_norm2_g, v_w_ff1, v_w_ff2, v_final_g):
    xi, yi, ci = _mesh_pos()
    chip = 2 * xi + yi
    dev = 2 * chip + ci
    z8 = jnp.zeros((8, D), F32)

    conv_rows = jnp.concatenate([conv_w[0], jnp.zeros((4, D - D // 4), F32)], axis=1)
    pack0 = jnp.concatenate([c, conv_rows, jnp.zeros((3, D), F32)], axis=0)
    g0 = all_gather_small("gather_cond", pack0).reshape(N_DEV, 8, D)
    c_all = g0[:, 0, :]
    conv_full = jnp.concatenate([g0[2 * k, 1:5, :D // 4] for k in range(N_CHIPS)], axis=1)
    c16 = jnp.concatenate([c_all, z8], axis=0)
    b_cols = lax.dynamic_slice_in_dim(b_ada, chip * ADA_SHARD, ADA_SHARD, axis=1)
    mod_c = mod_columns(c16, w_ada[0], b_cols)
    g1 = all_gather_small("gather_mod", mod_c).reshape(N_DEV, 16, ADA_SHARD)
    mod = jnp.concatenate([lax.dynamic_index_in_dim(g1[2 * k], dev, axis=0, keepdims=False) for k in range(N_CHIPS)])
    shift1, scale1, gate1, shift2, scale2, gate2 = [mod[i * D:(i + 1) * D] for i in range(6)]
    vecs = jnp.stack([norm1_g[0], scale1, shift1, gate1, norm2_g[0], scale2, shift2, gate2, final_g]
                     + [jnp.zeros((D,), F32)] * 7)
    lvec = jnp.concatenate([conv_full, conv_b, lru_ba, lru_bx, lru_lambda], axis=0)

    shards = [w_in[0], w_lru_out[0], w_attn_out[0], w_out[0], w_ff1[0], w_ff2[0]]
    names = ["w_in", "w_lru_out", "w_attn_out", "w_out", "w_ff1", "w_ff2"]
    k_idx = jnp.reshape(chip, (1,)).astype(jnp.int32)
    c_idx = jnp.reshape(ci, (1,)).astype(jnp.int32)
    near, far = (0, 1), (2,)
    shard_of = lambda flip: jnp.reshape(chip ^ flip, (1,)).astype(jnp.int32)
    x2d = x[0]
    n_send, n_recv, w_in_buf, _ = gather_start(
        "gather_start_in_near", [cast_into_slot("cast_w_in", shards[0], k_idx)], vecs, near)
    proj, h = inproj_fwd("inproj_fwd_own", x2d, None, vecs, w_in_buf[0], None, k_idx)
    slots = [cast_into_slot("cast_" + nm, w, k_idx) for nm, w in zip(names[1:], shards[1:])]
    w_in_buf = gather_forward("gather_forward_in_near", gather_wait(
        "gather_wait_in_near", n_send, n_recv, w_in_buf, proj[:8, :128] + slots[-1][0, :8, :128], near), near)
    f_send, f_recv, w_in_buf, _ = gather_start("gather_start_in_far", w_in_buf, proj[:8, :128], far)
    in_flight = {"mix": gather_start("gather_start_mix", slots[:3], proj[:8, :128])}
    in_flight["ff"] = gather_start("gather_start_ff", slots[3:], in_flight["mix"][3], carry=w_in_buf[0])
    w_in_buf = [in_flight["ff"][4]]
    proj = inproj_fwd("inproj_fwd_x", x2d, h, vecs, w_in_buf[0], proj, shard_of(2))
    proj = inproj_fwd("inproj_fwd_y", x2d, h, vecs, w_in_buf[0], proj, shard_of(1))
    w_in_buf = gather_forward("gather_forward_in_far", gather_wait(
        "gather_wait_in_far", f_send, f_recv, w_in_buf, proj[:8, :128], far), far)
    proj = inproj_fwd("inproj_fwd_d", x2d, h, vecs, w_in_buf[0], proj, shard_of(3))
    w_in_full = w_in_buf[0]
    pending = {}

    def rest_weights(group, after):
        send_sems, recv_sems, bufs = in_flight[group][:3]
        return gather_forward("gather_forward_" + group,
                              gather_wait("gather_wait_" + group, send_sems, recv_sems, bufs, after))

    def reduce_hook(event, payload):
        if event == "grads_a":
            pending["sib_a"] = exchange_start("sibling_start_a", "sibling", payload, payload[0])
            return pending["sib_a"][-1][0, 0]
        if event == "lru_grads":
            pack_w = jnp.concatenate([payload[0].reshape(D, 128), payload[1].reshape(D, 128)], axis=0).astype(BF16)
            pending["lru_w"] = exchange_start("lru_w_grads_start", "all", [pack_w], pack_w)
            return pending["lru_w"][-1][0, 0]
        if event == "grads_b":
            pending["sib_b"] = exchange_start("sibling_start_b", "sibling", payload, pending["lru_w"][-1])
            return pending["sib_b"][-1][0, 0]
        return chips_start("a", names[1:], payload)

    def chips_start(tag, nms, after):
        send_sems, recv_sems, grads, lands, _ = pending["sib_" + tag]
        grads, lands = exchange_wait("sibling_wait_" + tag, "sibling", send_sems, recv_sems, grads, lands, after)
        sums = [sibling_sum("sibling_sum_" + nm, g, o, c_idx) for nm, g, o in zip(nms, grads, lands)]
        pending[tag] = exchange_start("exchange_start_" + tag, "chips", sums, sums[0])
        return pending[tag][-1][0, 0]

    loss_t, grad_x, small = local_step(
        x2d, loss_target[0], vecs, lvec, lru_wa[0].astype(BF16), lru_wx[0].astype(BF16),
        attn_sinks[0], rel_bias, proj, h, w_in_full, rest_weights, reduce_hook)
    chips_start("b", names[:1], grad_x)

    big_m = dict(zip(names, [m_w_in, m_w_lru_out, m_w_attn_out, m_w_out, m_w_ff1, m_w_ff2]))
    big_v = dict(zip(names, [v_w_in, v_w_lru_out, v_w_attn_out, v_w_out, v_w_ff1, v_w_ff2]))
    local_w = dict(zip(names, shards))
    g_big, d_big, nm_big, nv_big = {}, {}, {}, {}

    def links_done(tag, after):
        send_sems, recv_sems, sums, lands, _ = pending[tag]
        return exchange_wait("exchange_wait_" + tag, "chips", send_sems, recv_sems, sums, lands, after)

    def finish_reduce(nms, sums, lands):
        mine = [chip_sum("chip_sum_" + nm, t, p, k_idx) for nm, t, p in zip(nms, sums, lands)]
        theirs = halves_exchange("halves_exchange_" + nms[0], mine)
        for nm, a, b in zip(nms, mine, theirs):
            g2, dl, m2, v2 = adamw_big("adamw_" + nm, local_w[nm], a, b, big_m[nm][0], big_v[nm][0], c_idx)
            g_big[nm], d_big[nm], nm_big[nm], nv_big[nm] = g2[None], dl[None], m2[None], v2[None]
        return lax.optimization_barrier(tuple(nv_big[nm] for nm in nms))[0]

    done_a = finish_reduce(names[1:], *links_done("a", pending["b"][-1]))
    sums_b, lands_b = links_done("b", done_a)
    w_send, w_recv, w_src, w_lands, _ = pending["lru_w"]
    w_src, w_lands = exchange_wait("lru_w_grads_wait", "all", w_send, w_recv, w_src, w_lands, lands_b[0])

    sums_f, sums_2, sums_1, sums_l = small["sums_f"], small["sums_2"], small["sums_1"], small["sums_l"]
    vec_rows = jnp.stack([sums_1[2], sums_2[2], sums_f[0], sums_l[L_CB], sums_l[L_BA], sums_l[L_BX],
                          sums_l[L_LAM], jnp.zeros((D,), F32)])
    mod_rows = jnp.stack([sums_1[0], sums_1[1], sums_2[3], sums_2[0], sums_2[1], sums_f[1],
                          jnp.zeros((D,), F32), jnp.zeros((D,), F32)])
    att_rows = jnp.concatenate([
        jnp.concatenate([small["d_sinks"], jnp.zeros((D - N_HEADS,), F32)])[None],
        jnp.concatenate([small["d_rel_bias"].reshape(-1), jnp.zeros((D - N_BUCKETS * N_HEADS,), F32)])[None],
        jnp.zeros((6, D), F32)], axis=0)
    pack = jnp.concatenate([vec_rows, _pad_rows(sums_l[0:4], 8), mod_rows, att_rows], axis=0)
    pack, lru_w_all = lax.optimization_barrier((pack, w_lands[0]))
    gathered = all_gather_small("gather_small_grads", pack).reshape(N_DEV, P_WA, D)
    total = sum_devices("sum_small_grads", gathered)
    total_w = sum_devices("sum_lru_w_grads", lru_w_all)
    dmod_all = gathered[:, P_MOD:P_MOD + 6, :].reshape(N_DEV, 6 * D)
    dmod16 = jnp.concatenate([lax.dynamic_slice_in_dim(dmod_all, chip * ADA_SHARD, ADA_SHARD, axis=1),
                              jnp.zeros((8, ADA_SHARD), F32)], axis=0)
    g_w_ada, d_w_ada, nm_w_ada, nv_w_ada = wada_update(c16, dmod16, w_ada[0], m_w_ada[0], v_w_ada[0])
    finish_reduce(names[:1], sums_b, lands_b)
    loss = lax.psum(lax.optimization_barrier((loss_t, total))[0][0, 0], ("x", "y", "c"))

    conv_g = lax.dynamic_slice_in_dim(total[P_CONVW:P_CONVW + 4], chip * (D // 4), D // 4, axis=1)
    sm_names = ["b_ada", "norm1_g", "conv_w", "conv_b", "lru_wa", "lru_ba", "lru_wx", "lru_bx", "lru_lambda",
                "attn_sinks", "rel_bias", "norm2_g", "final_g"]
    sm_w = [b_ada.reshape(6, D), norm1_g, conv_w[0], conv_b, lru_wa.reshape(D, 128), lru_ba, lru_wx.reshape(D, 128),
            lru_bx, lru_lambda, attn_sinks, rel_bias, norm2_g, final_g[None]]
    sm_m = [m_b_ada.reshape(6, D), m_norm1_g, m_conv_w[0], m_conv_b, m_lru_wa.reshape(D, 128), m_lru_ba,
            m_lru_wx.reshape(D, 128), m_lru_bx, m_lru_lambda, m_attn_sinks, m_rel_bias, m_norm2_g, m_final_g[None]]
    sm_v = [v_b_ada.reshape(6, D), v_norm1_g, v_conv_w[0], v_conv_b, v_lru_wa.reshape(D, 128), v_lru_ba,
            v_lru_wx.reshape(D, 128), v_lru_bx, v_lru_lambda, v_attn_sinks, v_rel_bias, v_norm2_g, v_final_g[None]]
    sm_g = [total[P_MOD:P_MOD + 6], total[0:1], conv_g, total[3:4], total_w[0:D], total[4:5],
            total_w[D:2 * D], total[5:6], total[6:7], total[P_ATT:P_ATT + 1, :N_HEADS],
            total[P_ATT + 1, :N_BUCKETS * N_HEADS].reshape(N_BUCKETS, N_HEADS), total[1:2], total[2:3]]
    sm_d, sm_nm, sm_nv = adamw_small(sm_w, sm_g, sm_m, sm_v)
    shapes = dict(b_ada=b_ada.shape, norm1_g=norm1_g.shape, conv_w=conv_w.shape, conv_b=conv_b.shape,
                  lru_wa=lru_wa.shape, lru_ba=lru_ba.shape, lru_wx=lru_wx.shape, lru_bx=lru_bx.shape,
                  lru_lambda=lru_lambda.shape, attn_sinks=attn_sinks.shape, rel_bias=rel_bias.shape,
                  norm2_g=norm2_g.shape, final_g=final_g.shape)
    grads = dict(w_ada=g_w_ada[None], **g_big)
    deltas = dict(w_ada=d_w_ada[None], **d_big)
    new_m = dict(w_ada=nm_w_ada[None], **nm_big)
    new_v = dict(w_ada=nv_w_ada[None], **nv_big)
    for i, nm in enumerate(sm_names):
        grads[nm] = sm_g[i].reshape(shapes[nm])
        deltas[nm] = sm_d[i].reshape(shapes[nm])
        new_m[nm] = sm_nm[i].reshape(shapes[nm])
        new_v[nm] = sm_nv[i].reshape(shapes[nm])
    order = ["w_ada", "b_ada", "norm1_g", "w_in", "conv_w", "conv_b", "lru_wa", "lru_ba", "lru_wx", "lru_bx",
             "lru_lambda", "w_lru_out", "w_attn_out", "attn_sinks", "rel_bias", "w_out", "norm2_g", "w_ff1", "w_ff2",
             "final_g"]
    return (loss, grad_x[None], *[grads[n] for n in order], *[deltas[n] for n in order],
            *[new_m[n] for n in order], *[new_v[n] for n in order])
```

```python
import math

import numpy as np
import jax
import jax.numpy as jnp
from jax import lax
from jax.experimental import pallas as pl
from jax.experimental.pallas import tpu as pltpu

F32 = jnp.float32
BF16 = jnp.bfloat16
SDS = jax.ShapeDtypeStruct
MESH = pl.DeviceIdType.MESH

D = 2048
D_FF = 4 * D
N_HEADS = 32
HEAD_DIM = 64
BLOCK = 128
N_LRU_BLOCKS = 16
LRU_C = 8.0
EPS = 1e-6
NEG_INF = -1e30
N_BUCKETS = 32
MAX_DISTANCE = 128
IN_W = 10752
IN_SHARD = IN_W // 4
IN_TILE = 896
ADA_SHARD = 6 * D // 4
OFF_LRU, OFF_GATE, OFF_Q, OFF_K, OFF_V, OFF_GA, OFF_GB = 0, 2048, 4096, 6144, 6400, 6656, 8704
SCALE = HEAD_DIM ** -0.5
N_CHIPS = 4
N_DEV = 8

ADAM_LR, ADAM_B1, ADAM_B2, ADAM_EPS, ADAM_WD, ADAM_STEP = 0.001, 0.9, 0.999, 1e-08, 0.01, 10
ADAM_C1 = 1.0 - ADAM_B1 ** ADAM_STEP
ADAM_C2 = 1.0 - ADAM_B2 ** ADAM_STEP

VMEM_LIMIT = 52 * 2 ** 20
SUB = 128
WG_TM = 1024
V_G1, V_SCALE1, V_SHIFT1, V_GATE1, V_G2, V_SCALE2, V_SHIFT2, V_GATE2, V_G3 = range(9)
L_CW0, L_CB, L_BA, L_BX, L_LAM = 0, 4, 5, 6, 7
P_VEC, P_CONVW, P_MOD, P_ATT, P_WA = 0, 8, 16, 24, 32


def _cp(n_axes):
    return pltpu.CompilerParams(dimension_semantics=("arbitrary",) * n_axes, vmem_limit_bytes=VMEM_LIMIT)


def _dot(a, b):
    return jnp.dot(a, b, preferred_element_type=F32)


def _dot_nt(a, b):
    return lax.dot_general(a, b, (((1,), (1,)), ((), ())), preferred_element_type=F32)


def _dot_tn(a, b):
    return lax.dot_general(a, b, (((0,), (0,)), ((), ())), preferred_element_type=F32)


_G0 = math.sqrt(2.0 / math.pi)
_G1 = 0.044715


def _gelu(x):
    return 0.5 * x * (1.0 + jnp.tanh(_G0 * (x + _G1 * x * x * x)))


def _gelu_grad(x):
    x2 = x * x
    t = jnp.tanh(_G0 * (x + _G1 * x * x2))
    return 0.5 * (1.0 + t) + 0.5 * x * (1.0 - t * t) * _G0 * (1.0 + 3.0 * _G1 * x2)


def _sigmoid(x):
    return 0.5 * jnp.tanh(0.5 * x) + 0.5


def _one_minus_exp2(x):
    t = jnp.tanh(x)
    return (-2.0 * t) / (1.0 - t)


def _softplus(z):
    e = jnp.exp(-jnp.abs(z))
    u = 1.0 + e
    l1p = jnp.where(u == 1.0, e, jnp.log(u) * e / (u - 1.0))
    return jnp.maximum(z, 0.0) + l1p


def _adamw_math(w, g, m, v):
    m2 = ADAM_B1 * m + (1.0 - ADAM_B1) * g
    v2 = ADAM_B2 * v + (1.0 - ADAM_B2) * (g * g)
    m_hat = m2 / ADAM_C1
    v_hat = v2 / ADAM_C2
    delta = -ADAM_LR * (m_hat / (jnp.sqrt(v_hat) + ADAM_EPS) + ADAM_WD * w)
    return delta, m2, v2


def _rms_parts(xv):
    r = lax.rsqrt(jnp.mean(xv * xv, axis=-1, keepdims=True) + EPS)
    return r, xv * r


def _row_fetches(hbm_refs, bufs, sems, i, rows):
    return [pltpu.make_async_copy(h.at[pl.ds(i * rows, rows), :], b, sems.at[n])
            for n, (h, b) in enumerate(zip(hbm_refs, bufs))]


def _modulated_norm(x_ref, v_ref, row_g, row_scale, row_shift, h_ref, rows):
    g, scale, shift = v_ref[row_g:row_g + 1, :], v_ref[row_scale:row_scale + 1, :], v_ref[row_shift:row_shift + 1, :]

    def sub(rb, carry):
        rs = pl.ds(pl.multiple_of(rb * SUB, SUB), SUB)
        _, xh = _rms_parts(x_ref[rs, :])
        h_ref[rs, :] = ((xh * g) * (1.0 + scale) + shift).astype(BF16)
        return carry

    lax.fori_loop(0, rows // SUB, sub, 0)


def inproj_fwd(name, x, h, vecs, w_in, proj, shard):
    s = x.shape[0]
    tm = min(1024, s)
    per = IN_SHARD // IN_TILE
    first = h is None

    def body(*refs):
        if first:
            _, x_ref, v_ref, w_ref, proj_ref, h_ref = refs

            @pl.when(pl.program_id(1) == 0)
            def _():
                _modulated_norm(x_ref, v_ref, V_G1, V_SCALE1, V_SHIFT1, h_ref, tm)
        else:
            _, h_ref, w_ref, _, proj_ref = refs
        proj_ref[...] = _dot(h_ref[...], w_ref[...]).astype(BF16)

    rows = pl.BlockSpec((tm, D), lambda i, j, sr: (i, 0))
    w_spec = pl.BlockSpec((None, D, IN_TILE), lambda i, j, sr: (sr[0], 0, j))
    proj_spec = pl.BlockSpec((tm, IN_TILE), lambda i, j, sr: (i, sr[0] * per + j))
    if first:
        gs = pltpu.PrefetchScalarGridSpec(
            num_scalar_prefetch=1, grid=(s // tm, per),
            in_specs=[rows, pl.BlockSpec((16, D), lambda i, j, sr: (0, 0)), w_spec], out_specs=[proj_spec, rows])
        return pl.pallas_call(body, name=name, grid_spec=gs, out_shape=[SDS((s, IN_W), BF16), SDS((s, D), BF16)],
                              compiler_params=_cp(2))(shard, x, vecs, w_in)
    gs = pltpu.PrefetchScalarGridSpec(
        num_scalar_prefetch=1, grid=(s // tm, per),
        in_specs=[rows, w_spec, pl.BlockSpec(memory_space=pl.ANY)], out_specs=proj_spec)
    return pl.pallas_call(body, name=name, grid_spec=gs, out_shape=SDS((s, IN_W), BF16),
                          input_output_aliases={3: 0}, compiler_params=_cp(2))(shard, h, w_in, proj)


def _lru_block_fwd(xbuf, lv_ref, wa_ref, wx_ref, b, t, first):
    cs = slice(b * 128, (b + 1) * 128)
    x0 = xbuf[pl.ds(8, t), cs]
    x1 = xbuf[pl.ds(7, t), cs]
    x2 = xbuf[pl.ds(6, t), cs]
    x3 = xbuf[pl.ds(5, t), cs]
    xc = (lv_ref[L_CB:L_CB + 1, cs] + lv_ref[3:4, cs] * x0 + lv_ref[2:3, cs] * x1
          + lv_ref[1:2, cs] * x2 + lv_ref[0:1, cs] * x3)
    xcb = xc.astype(BF16)
    r = _sigmoid(_dot(xcb, wa_ref[b]) + lv_ref[L_BA:L_BA + 1, cs])
    ig = _sigmoid(_dot(xcb, wx_ref[b]) + lv_ref[L_BX:L_BX + 1, cs])
    sp = _softplus(-lv_ref[L_LAM:L_LAM + 1, cs])
    log_a = (-LRU_C) * r * sp
    a = jnp.exp(log_a)
    mult = jnp.where(first, 1.0, jnp.sqrt(_one_minus_exp2(log_a)))
    return (x0, x1, x2, x3), xc, xcb, r, ig, sp, a, mult


def lru_fwd(proj, lvec, wa, wx):
    s = proj.shape[0]
    t = min(256, s)

    def body(lx_ref, gate_ref, lv_ref, wa_ref, wx_ref, ya_ref, rec_ref, yat_ref, xbuf, a_s, u_s, hc):
        i = pl.program_id(0)

        @pl.when(i == 0)
        def _():
            xbuf[pl.ds(0, 8), :] = jnp.zeros((8, D), F32)
            hc[...] = jnp.zeros((8, D), F32)

        @pl.when(i > 0)
        def _():
            xbuf[pl.ds(0, 8), :] = xbuf[pl.ds(t, 8), :]

        xbuf[pl.ds(8, t), :] = lx_ref[...].astype(F32)
        first = (lax.broadcasted_iota(jnp.int32, (t, 128), 0) + i * t) == 0
        for b in range(N_LRU_BLOCKS):
            cs = slice(b * 128, (b + 1) * 128)
            _, xc, _, _, ig, _, a, mult = _lru_block_fwd(xbuf, lv_ref, wa_ref, wx_ref, b, t, first)
            a_s[:, cs] = a
            u_s[:, cs] = mult * (ig * xc)

        def step(tt, h):
            h = a_s[pl.ds(tt, 1), :] * h + u_s[pl.ds(tt, 1), :]
            rec_ref[pl.ds(tt, 1), :] = h
            return h

        hc[0:1, :] = lax.fori_loop(0, t, step, hc[0:1, :], unroll=8)
        for b in range(N_LRU_BLOCKS):
            cs = slice(b * 128, (b + 1) * 128)
            yb = (rec_ref[:, cs] * _gelu(gate_ref[:, cs].astype(F32))).astype(BF16)
            ya_ref[:, cs] = yb
            yat_ref[cs, :] = yb.T

    return pl.pallas_call(
        body, name="lru_fwd", grid=(s // t,),
        in_specs=[pl.BlockSpec((t, D), lambda i: (i, OFF_LRU // D)),
                  pl.BlockSpec((t, D), lambda i: (i, OFF_GATE // D)),
                  pl.BlockSpec((8, D), lambda i: (0, 0)),
                  pl.BlockSpec((N_LRU_BLOCKS, 128, 128), lambda i: (0, 0, 0)),
                  pl.BlockSpec((N_LRU_BLOCKS, 128, 128), lambda i: (0, 0, 0))],
        out_specs=[pl.BlockSpec((t, D), lambda i: (i, 0)), pl.BlockSpec((t, D), lambda i: (i, 0)),
                   pl.BlockSpec((D, t), lambda i: (0, i))],
        out_shape=[SDS((s, D), BF16), SDS((s, D), F32), SDS((D, s), BF16)],
        scratch_shapes=[pltpu.VMEM((t + 8, D), F32), pltpu.VMEM((t, D), F32), pltpu.VMEM((t, D), F32),
                        pltpu.VMEM((8, D), F32)],
        compiler_params=_cp(1))(proj, proj, lvec, wa, wx)


def t5_bucket_table():
    qi = np.arange(BLOCK)[:, None]
    ki = np.arange(2 * BLOCK)[None, :]
    rel = qi + BLOCK - ki
    relc = np.maximum(rel, 0)
    max_exact = N_BUCKETS // 2
    relf = np.maximum(relc, 1).astype(np.float32)
    large = max_exact + (np.log(relf / np.float32(max_exact)) / np.float32(math.log(MAX_DISTANCE / max_exact))
                         * np.float32(N_BUCKETS - max_exact)).astype(np.int32)
    large = np.minimum(large, N_BUCKETS - 1)
    bucket = np.where(relc < max_exact, relc, large)
    bucket = np.where((rel >= 0) & (rel < BLOCK), bucket, -1)
    return jnp.asarray(bucket.reshape(1, BLOCK * 2 * BLOCK), jnp.int32)


def bias_band(rel_bias_t, buckets):
    n = BLOCK * 2 * BLOCK
    tn = 4096

    def body(bk_ref, rb_ref, o_ref):
        row = lax.broadcasted_iota(jnp.int32, (N_BUCKETS, tn), 0)
        oh = jnp.where(row == bk_ref[...], 1.0, 0.0).astype(BF16)
        rb = rb_ref[...]
        p0 = rb.astype(BF16)
        r1 = rb - p0.astype(F32)
        p1 = r1.astype(BF16)
        p2 = (r1 - p1.astype(F32)).astype(BF16)
        o_ref[...] = _dot(p0, oh) + _dot(p1, oh) + _dot(p2, oh)

    return pl.pallas_call(
        body, name="bias_band", grid=(n // tn,),
        in_specs=[pl.BlockSpec((1, tn), lambda i: (0, i)), pl.BlockSpec((N_HEADS, N_BUCKETS), lambda i: (0, 0))],
        out_specs=pl.BlockSpec((N_HEADS, tn), lambda i: (0, i)),
        out_shape=SDS((N_HEADS, n), F32), compiler_params=_cp(1))(buckets, rel_bias_t)


def bias_band_bwd(dband, buckets):
    n = BLOCK * 2 * BLOCK
    tn = 4096

    def body(bk_ref, d_ref, o_ref):
        @pl.when(pl.program_id(0) == 0)
        def _():
            o_ref[...] = jnp.zeros_like(o_ref)
        row = lax.broadcasted_iota(jnp.int32, (N_BUCKETS, tn), 0)
        oh = jnp.where(row == bk_ref[...], 1.0, 0.0).astype(BF16)
        dv = d_ref[...]
        p0 = dv.astype(BF16)
        r1 = dv - p0.astype(F32)
        p1 = r1.astype(BF16)
        p2 = (r1 - p1.astype(F32)).astype(BF16)
        o_ref[...] += _dot_nt(oh, p0) + _dot_nt(oh, p1) + _dot_nt(oh, p2)

    return pl.pallas_call(
        body, name="bias_band_bwd", grid=(n // tn,),
        in_specs=[pl.BlockSpec((1, tn), lambda i: (0, i)), pl.BlockSpec((N_HEADS, tn), lambda i: (0, i))],
        out_specs=pl.BlockSpec((N_BUCKETS, N_HEADS), lambda i: (0, 0)),
        out_shape=SDS((N_BUCKETS, N_HEADS), F32), compiler_params=_cp(1))(buckets, dband)


def _dup_half(band, which):
    lane = lax.broadcasted_iota(jnp.int32, band.shape, 1)
    rolled = pltpu.roll(band, 64, 1)
    keep = (lane < 64) if which == 0 else (lane >= 64)
    return jnp.where(keep, band, rolled)


def _attn_probs(scores, bias, sink, valid):
    sc = jnp.where(valid, scores * SCALE + bias, NEG_INF)
    m = jnp.maximum(jnp.max(sc, axis=-1, keepdims=True), sink)
    e = jnp.exp(sc - m)
    es = jnp.exp(sink - m)
    inv = 1.0 / (jnp.sum(e, axis=-1, keepdims=True) + es)
    return e * inv, es * inv


def _stack_heads(src_ref, kv, dst):
    lane = lax.broadcasted_iota(jnp.int32, (BLOCK, 128), 1)
    for jj in range(4):
        slab = src_ref[:, (4 * kv + jj) * 128:(4 * kv + jj + 1) * 128]
        for hh in range(2):
            keep = (lane < 64) if hh == 0 else (lane >= 64)
            dst[pl.ds((2 * jj + hh) * BLOCK, BLOCK), :] = jnp.where(keep, slab, jnp.zeros_like(slab))


def _unstack_heads(stacked, dst_ref, kv, dst_t_ref=None):
    lane = lax.broadcasted_iota(jnp.int32, (BLOCK, 128), 1)
    for jj in range(4):
        lo = stacked[(2 * jj) * BLOCK:(2 * jj + 1) * BLOCK]
        hi = stacked[(2 * jj + 1) * BLOCK:(2 * jj + 2) * BLOCK]
        slab = jnp.where(lane < 64, lo, hi).astype(dst_ref.dtype)
        dst_ref[:, (4 * kv + jj) * 128:(4 * kv + jj + 1) * 128] = slab
        if dst_t_ref is not None:
            dst_t_ref[(4 * kv + jj) * 128:(4 * kv + jj + 1) * 128, :] = slab.T


def _band_valid(n):
    qi = lax.broadcasted_iota(jnp.int32, (BLOCK, 2 * BLOCK), 0)
    ki = lax.broadcasted_iota(jnp.int32, (BLOCK, 2 * BLOCK), 1)
    rel = qi + BLOCK - ki
    return (rel >= 0) & (rel < BLOCK) & ((ki >= BLOCK) | (n > 0))


def _kv_bands(prev_ref, cur_ref):
    band = jnp.concatenate([prev_ref[...].astype(F32), cur_ref[...].astype(F32)], axis=0)
    return [_dup_half(band, 0).astype(BF16), _dup_half(band, 1).astype(BF16)]


def attn_fwd(proj, band, sinks):
    s = proj.shape[0]
    nb = s // BLOCK
    qw = 1024

    def body(sk_ref, q_ref, kp_ref, kc_ref, vp_ref, vc_ref, b_ref, o_ref, ot_ref, qs_buf, s_buf, p_buf):
        n = pl.program_id(0)
        gp = pl.program_id(1)
        valid = _band_valid(n)
        kks = _kv_bands(kp_ref, kc_ref)
        vvs = _kv_bands(vp_ref, vc_ref)
        for kv in range(2):
            _stack_heads(q_ref, kv, qs_buf)
            s_buf[...] = _dot_nt(qs_buf[...], kks[kv])
            for hq in range(8):
                hl = 8 * kv + hq
                rows = pl.ds(hq * BLOCK, BLOCK)
                p, _ = _attn_probs(s_buf[rows, :], b_ref[hl], sk_ref[gp * 16 + hl], valid)
                p_buf[rows, :] = p.astype(BF16)
            _unstack_heads(_dot(p_buf[...], vvs[kv]), o_ref, kv, ot_ref)

    kb, vb = OFF_K // 128, OFF_V // 128
    return pl.pallas_call(
        body, name="attn_fwd", grid=(nb, 2),
        in_specs=[pl.BlockSpec(memory_space=pltpu.SMEM),
                  pl.BlockSpec((BLOCK, qw), lambda n, g: (n, OFF_Q // qw + g)),
                  pl.BlockSpec((BLOCK, 128), lambda n, g: (jnp.maximum(n - 1, 0), kb + g)),
                  pl.BlockSpec((BLOCK, 128), lambda n, g: (n, kb + g)),
                  pl.BlockSpec((BLOCK, 128), lambda n, g: (jnp.maximum(n - 1, 0), vb + g)),
                  pl.BlockSpec((BLOCK, 128), lambda n, g: (n, vb + g)),
                  pl.BlockSpec((16, BLOCK, 2 * BLOCK), lambda n, g: (g, 0, 0))],
        out_specs=[pl.BlockSpec((BLOCK, qw), lambda n, g: (n, g)), pl.BlockSpec((qw, BLOCK), lambda n, g: (g, n))],
        out_shape=[SDS((s, D), BF16), SDS((D, s), BF16)],
        scratch_shapes=[pltpu.VMEM((8 * BLOCK, 128), BF16), pltpu.VMEM((8 * BLOCK, 2 * BLOCK), F32),
                        pltpu.VMEM((8 * BLOCK, 2 * BLOCK), BF16)],
        compiler_params=_cp(2))(sinks, proj, proj, proj, proj, proj, band)


def merge_fwd(ya, att, w_lru_out, w_attn_out, proj):
    s = ya.shape[0]
    tm, tn = min(1024, s), 512

    def body(ya_ref, at_ref, wl_ref, wt_ref, ga_ref, gb_ref, yab_ref, mg_ref, mgt_ref):
        y_a = _dot(ya_ref[...], wl_ref[...])
        y_b = _dot(at_ref[...], wt_ref[...])
        yab_ref[0] = y_a.astype(BF16)
        yab_ref[1] = y_b.astype(BF16)
        mg = (_sigmoid(ga_ref[...].astype(F32)) * y_a + _sigmoid(gb_ref[...].astype(F32)) * y_b).astype(BF16)
        mg_ref[...] = mg
        mgt_ref[...] = mg.T

    return pl.pallas_call(
        body, name="merge_fwd", grid=(s // tm, D // tn),
        in_specs=[pl.BlockSpec((tm, D), lambda i, j: (i, 0)), pl.BlockSpec((tm, D), lambda i, j: (i, 0)),
                  pl.BlockSpec((D, tn), lambda i, j: (0, j)), pl.BlockSpec((D, tn), lambda i, j: (0, j)),
                  pl.BlockSpec((tm, tn), lambda i, j: (i, OFF_GA // tn + j)),
                  pl.BlockSpec((tm, tn), lambda i, j: (i, OFF_GB // tn + j))],
        out_specs=[pl.BlockSpec((2, tm, tn), lambda i, j: (0, i, j)), pl.BlockSpec((tm, tn), lambda i, j: (i, j)),
                   pl.BlockSpec((tn, tm), lambda i, j: (j, i))],
        out_shape=[SDS((2, s, D), BF16), SDS((s, D), BF16), SDS((D, s), BF16)],
        compiler_params=_cp(2))(ya, att, w_lru_out, w_attn_out, proj, proj)


def outproj_fwd(merged, w_out, x, vecs):
    s = x.shape[0]
    tm, tn = min(1024, s), 512

    def body(m_ref, w_ref, x_ref, v_ref, x1_ref, o1_ref):
        o1 = _dot(m_ref[...], w_ref[...])
        o1_ref[...] = o1.astype(BF16)
        x1_ref[...] = x_ref[...] + v_ref[V_GATE1:V_GATE1 + 1, :] * o1

    return pl.pallas_call(
        body, name="outproj_fwd", grid=(s // tm, D // tn),
        in_specs=[pl.BlockSpec((tm, D), lambda i, j: (i, 0)), pl.BlockSpec((D, tn), lambda i, j: (0, j)),
                  pl.BlockSpec((tm, tn), lambda i, j: (i, j)), pl.BlockSpec((16, tn), lambda i, j: (0, j))],
        out_specs=[pl.BlockSpec((tm, tn), lambda i, j: (i, j)), pl.BlockSpec((tm, tn), lambda i, j: (i, j))],
        out_shape=[SDS((s, D), F32), SDS((s, D), BF16)],
        compiler_params=_cp(2))(merged, w_out, x, vecs)


def ff1_fwd(x1, vecs, w_ff1):
    s = x1.shape[0]
    tm, tn = min(1024, s), 1024
    per = D // tn

    def body(x_ref, v_ref, w_ref, f_ref, h_ref, fft_ref):
        @pl.when(pl.program_id(1) == 0)
        def _():
            _modulated_norm(x_ref, v_ref, V_G2, V_SCALE2, V_SHIFT2, h_ref, tm)
        fv = _dot(h_ref[...], w_ref[...])
        f_ref[...] = fv.astype(BF16)
        fp = jnp.maximum(fv, 0.0)
        fft_ref[...] = (fp * fp).astype(BF16).T

    return pl.pallas_call(
        body, name="ff1_fwd", grid=(s // tm, D_FF // tn),
        in_specs=[pl.BlockSpec((tm, D), lambda i, j: (i, 0)), pl.BlockSpec((16, D), lambda i, j: (0, 0)),
                  pl.BlockSpec((None, D, tn), lambda i, j: (j // per, 0, j % per))],
        out_specs=[pl.BlockSpec((tm, tn), lambda i, j: (i, j)), pl.BlockSpec((tm, D), lambda i, j: (i, 0)),
                   pl.BlockSpec((tn, tm), lambda i, j: (j, i))],
        out_shape=[SDS((s, D_FF), BF16), SDS((s, D), BF16), SDS((D_FF, s), BF16)],
        compiler_params=_cp(2))(x1, vecs, w_ff1)


def matmul_bf16(name, a, b, tm, tn, b_part=0):
    m, k = a.shape
    n = b.shape[1]

    def body(a_ref, b_ref, o_ref):
        o_ref[...] = _dot(a_ref[...], b_ref[...]).astype(BF16)

    return pl.pallas_call(
        body, name=name, grid=(m // tm, n // tn),
        in_specs=[pl.BlockSpec((tm, k), lambda i, j: (i, 0)), pl.BlockSpec((k, tn), lambda i, j: (b_part, j))],
        out_specs=pl.BlockSpec((tm, tn), lambda i, j: (i, j)),
        out_shape=SDS((m, n), BF16), compiler_params=_cp(2))(a, b)


def ff2_loss(f, w_ff2, x1, tgt, vecs):
    s = x1.shape[0]
    tm, tk = min(512, s), 1024
    nk = D_FF // tk

    def body(f_ref, w_ref, x1_hbm, t_hbm, v_ref, dx2_ref, do2_ref, sums_ref, loss_ref, acc, x1_ref, t_ref, sems):
        i, k = pl.program_id(0), pl.program_id(1)
        fetches = _row_fetches((x1_hbm, t_hbm), (x1_ref, t_ref), sems, i, tm)

        @pl.when((i == 0) & (k == 0))
        def _():
            sums_ref[...] = jnp.zeros_like(sums_ref)
            loss_ref[...] = jnp.zeros_like(loss_ref)

        @pl.when(k == 0)
        def _():
            acc[...] = jnp.zeros_like(acc)
            for cp in fetches:
                cp.start()

        fv = jnp.maximum(f_ref[...].astype(F32), 0.0)
        acc[...] += _dot((fv * fv).astype(BF16), w_ref[...])

        @pl.when(k == nk - 1)
        def _():
            for cp in fetches:
                cp.wait()
            gate2 = v_ref[V_GATE2:V_GATE2 + 1, :]
            g3 = v_ref[V_G3:V_G3 + 1, :]

            def sub(rb, carry):
                rs = pl.ds(pl.multiple_of(rb * SUB, SUB), SUB)
                o2 = acc[rs, :]
                x2 = x1_ref[rs, :] + gate2 * o2
                r3, xh = _rms_parts(x2)
                e = xh * g3 - t_ref[rs, :]
                loss_ref[...] += (0.5 / D) * jnp.sum(e * e)
                dy = e * (1.0 / D)
                sums_ref[0:1, :] += jnp.sum(dy * xh, axis=0, keepdims=True)
                dxh = dy * g3
                dx2 = r3 * (dxh - xh * jnp.mean(dxh * xh, axis=-1, keepdims=True))
                sums_ref[1:2, :] += jnp.sum(dx2 * o2, axis=0, keepdims=True)
                dx2_ref[rs, :] = dx2
                do2_ref[rs, :] = (dx2 * gate2).astype(BF16)
                return carry

            lax.fori_loop(0, tm // SUB, sub, 0)

    return pl.pallas_call(
        body, name="ff2_loss", grid=(s // tm, nk),
        in_specs=[pl.BlockSpec((tm, tk), lambda i, k: (i, k)), pl.BlockSpec((tk, D), lambda i, k: (k, 0)),
                  pl.BlockSpec(memory_space=pl.ANY), pl.BlockSpec(memory_space=pl.ANY),
                  pl.BlockSpec((16, D), lambda i, k: (0, 0))],
        out_specs=[pl.BlockSpec((tm, D), lambda i, k: (i, 0)), pl.BlockSpec((tm, D), lambda i, k: (i, 0)),
                   pl.BlockSpec((8, D), lambda i, k: (0, 0)), pl.BlockSpec((8, 128), lambda i, k: (0, 0))],
        out_shape=[SDS((s, D), F32), SDS((s, D), BF16), SDS((8, D), F32), SDS((8, 128), F32)],
        scratch_shapes=[pltpu.VMEM((tm, D), F32), pltpu.VMEM((tm, D), F32), pltpu.VMEM((tm, D), F32),
                        pltpu.SemaphoreType.DMA((2,))],
        compiler_params=_cp(2))(f, w_ff2, x1, tgt, vecs)


def ff2_bwd(do2, w_ff2, f):
    s = do2.shape[0]
    tm, tn = min(1024, s), 1024

    def body(d_ref, w_ref, f_ref, o_ref):
        dff = _dot_nt(d_ref[...], w_ref[...])
        o_ref[...] = (dff * (2.0 * jnp.maximum(f_ref[...].astype(F32), 0.0))).astype(BF16)

    return pl.pallas_call(
        body, name="ff2_bwd", grid=(s // tm, D_FF // tn),
        in_specs=[pl.BlockSpec((tm, D), lambda i, j: (i, 0)), pl.BlockSpec((tn, D), lambda i, j: (j, 0)),
                  pl.BlockSpec((tm, tn), lambda i, j: (i, j))],
        out_specs=pl.BlockSpec((tm, tn), lambda i, j: (i, j)),
        out_shape=SDS((s, D_FF), BF16), compiler_params=_cp(2))(do2, w_ff2, f)


def weight_grad(name, a, b, tn, out_shape, out_block, out_map):
    s, m = a.shape
    n = b.shape[1]
    tm = WG_TM
    chunk = min(1024, s)
    nch = s // chunk

    def body(a_hbm, b_ref, o_ref, a_buf, at_s, sem):
        i = pl.program_id(0)

        @pl.when(pl.program_id(1) == 0)
        def _():
            def fetch(ch):
                return pltpu.make_async_copy(a_hbm.at[pl.ds(ch * chunk, chunk), pl.ds(i * tm, tm)],
                                             a_buf.at[ch % 2], sem.at[ch % 2])
            fetch(0).start()
            for ch in range(nch):
                if ch + 1 < nch:
                    fetch(ch + 1).start()
                fetch(ch).wait()
                at_s[:, ch * chunk:(ch + 1) * chunk] = a_buf[ch % 2].T

        o_ref[...] = _dot(at_s[...], b_ref[...]).astype(BF16)

    return pl.pallas_call(
        body, name=name, grid=(m // tm, n // tn),
        in_specs=[pl.BlockSpec(memory_space=pl.ANY), pl.BlockSpec((s, tn), lambda i, j: (0, j))],
        out_specs=pl.BlockSpec(out_block, lambda i, j: out_map(i, j)),
        out_shape=SDS(out_shape, BF16),
        scratch_shapes=[pltpu.VMEM((2, chunk, tm), BF16), pltpu.VMEM((tm, s), BF16), pltpu.SemaphoreType.DMA((2,))],
        compiler_params=_cp(2))(a, b)


def ff1_bwd(df, w_ff1, x1, dx2, o1, vecs):
    s = df.shape[0]
    tm, tk = min(512, s), 1024
    nk = D_FF // tk
    per = D // tk

    def body(d_ref, w_ref, x1_hbm, dx2_hbm, o1_hbm, v_ref, dx1_ref, do1_ref, sums_ref, acc, x1_ref, dx2_ref, o1_ref, sems):
        i, k = pl.program_id(0), pl.program_id(1)
        fetches = _row_fetches((x1_hbm, dx2_hbm, o1_hbm), (x1_ref, dx2_ref, o1_ref), sems, i, tm)

        @pl.when((i == 0) & (k == 0))
        def _():
            sums_ref[...] = jnp.zeros_like(sums_ref)

        @pl.when(k == 0)
        def _():
            acc[...] = jnp.zeros_like(acc)
            for cp in fetches:
                cp.start()

        acc[...] += _dot_nt(d_ref[...], w_ref[...])

        @pl.when(k == nk - 1)
        def _():
            for cp in fetches:
                cp.wait()
            g2 = v_ref[V_G2:V_G2 + 1, :]
            scale2 = v_ref[V_SCALE2:V_SCALE2 + 1, :]
            gate1 = v_ref[V_GATE1:V_GATE1 + 1, :]

            def sub(rb, carry):
                rs = pl.ds(pl.multiple_of(rb * SUB, SUB), SUB)
                dh = acc[rs, :]
                r2, xh = _rms_parts(x1_ref[rs, :])
                sums_ref[0:1, :] += jnp.sum(dh, axis=0, keepdims=True)
                sums_ref[1:2, :] += jnp.sum(dh * (xh * g2), axis=0, keepdims=True)
                dxn = dh * (1.0 + scale2)
                sums_ref[2:3, :] += jnp.sum(dxn * xh, axis=0, keepdims=True)
                dxh = dxn * g2
                dx1 = dx2_ref[rs, :] + r2 * (dxh - xh * jnp.mean(dxh * xh, axis=-1, keepdims=True))
                sums_ref[3:4, :] += jnp.sum(dx1 * o1_ref[rs, :].astype(F32), axis=0, keepdims=True)
                dx1_ref[rs, :] = dx1
                do1_ref[rs, :] = (dx1 * gate1).astype(BF16)
                return carry

            lax.fori_loop(0, tm // SUB, sub, 0)

    return pl.pallas_call(
        body, name="ff1_bwd", grid=(s // tm, nk),
        in_specs=[pl.BlockSpec((tm, tk), lambda i, k: (i, k)),
                  pl.BlockSpec((None, D, tk), lambda i, k: (k // per, 0, k % per)),
                  pl.BlockSpec(memory_space=pl.ANY), pl.BlockSpec(memory_space=pl.ANY),
                  pl.BlockSpec(memory_space=pl.ANY), pl.BlockSpec((16, D), lambda i, k: (0, 0))],
        out_specs=[pl.BlockSpec((tm, D), lambda i, k: (i, 0)), pl.BlockSpec((tm, D), lambda i, k: (i, 0)),
                   pl.BlockSpec((8, D), lambda i, k: (0, 0))],
        out_shape=[SDS((s, D), F32), SDS((s, D), BF16), SDS((8, D), F32)],
        scratch_shapes=[pltpu.VMEM((tm, D), F32), pltpu.VMEM((tm, D), F32), pltpu.VMEM((tm, D), F32),
                        pltpu.VMEM((tm, D), BF16), pltpu.SemaphoreType.DMA((3,))],
        compiler_params=_cp(2))(df, w_ff1, x1, dx2, o1, vecs)


def outproj_bwd(do1, w_out, yab, proj):
    s = do1.shape[0]
    tm, tn = min(1024, s), 512
    per = D // tn

    def body(d_ref, w_ref, y_ref, g_ref, dy_ref, dp_ref):
        dm = _dot_nt(d_ref[...], w_ref[...])
        sg = _sigmoid(g_ref[...].astype(F32))
        dy_ref[...] = (dm * sg).astype(BF16)
        dp_ref[...] = (dm * y_ref[...].astype(F32) * sg * (1.0 - sg)).astype(BF16)

    return pl.pallas_call(
        body, name="outproj_bwd", grid=(s // tm, 2 * per),
        in_specs=[pl.BlockSpec((tm, D), lambda i, j: (i, 0)), pl.BlockSpec((tn, D), lambda i, j: (j % per, 0)),
                  pl.BlockSpec((None, tm, tn), lambda i, j: (j // per, i, j % per)),
                  pl.BlockSpec((tm, tn), lambda i, j: (i, OFF_GA // tn + j))],
        out_specs=[pl.BlockSpec((None, tm, tn), lambda i, j: (j // per, i, j % per)),
                   pl.BlockSpec((tm, tn), lambda i, j: (i, OFF_GA // tn + j))],
        out_shape=[SDS((2, s, D), BF16), SDS((s, IN_W), BF16)],
        compiler_params=_cp(2))(do1, w_out, yab, proj)


def lruout_bwd(dyab, w_lru_out, rec, proj, dproj):
    s = rec.shape[0]
    tm, tn = min(1024, s), 512

    def body(d_ref, w_ref, r_ref, g_ref, dp_in, dr_ref, dp_ref):
        dya = _dot_nt(d_ref[...], w_ref[...])
        gate = g_ref[...].astype(F32)
        dr_ref[...] = dya * _gelu(gate)
        dp_ref[...] = (dya * r_ref[...] * _gelu_grad(gate)).astype(BF16)

    return pl.pallas_call(
        body, name="lruout_bwd", grid=(s // tm, D // tn),
        in_specs=[pl.BlockSpec((None, tm, D), lambda i, j: (0, i, 0)), pl.BlockSpec((tn, D), lambda i, j: (j, 0)),
                  pl.BlockSpec((tm, tn), lambda i, j: (i, j)),
                  pl.BlockSpec((tm, tn), lambda i, j: (i, OFF_GATE // tn + j)),
                  pl.BlockSpec(memory_space=pl.ANY)],
        out_specs=[pl.BlockSpec((tm, tn), lambda i, j: (i, j)),
                   pl.BlockSpec((tm, tn), lambda i, j: (i, OFF_GATE // tn + j))],
        out_shape=[SDS((s, D), F32), SDS((s, IN_W), BF16)],
        input_output_aliases={4: 1},
        compiler_params=_cp(2))(dyab, w_lru_out, rec, proj, dproj)


def attnout_bwd(dyab, w_attn_out):
    s = dyab.shape[1]
    tm, tn = min(1024, s), 512

    def body(d_ref, w_ref, o_ref):
        o_ref[...] = _dot_nt(d_ref[...], w_ref[...]).astype(BF16)

    return pl.pallas_call(
        body, name="attnout_bwd", grid=(s // tm, D // tn),
        in_specs=[pl.BlockSpec((None, tm, D), lambda i, j: (1, i, 0)), pl.BlockSpec((tn, D), lambda i, j: (j, 0))],
        out_specs=pl.BlockSpec((tm, tn), lambda i, j: (i, j)),
        out_shape=SDS((s, D), BF16), compiler_params=_cp(2))(dyab, w_attn_out)


def attn_bwd(proj, band, sinks, datt, dproj):
    s = proj.shape[0]
    nb = s // BLOCK
    qw = 1024

    def body(sk_ref, q_ref, kp_ref, kc_ref, vp_ref, vc_ref, b_ref, do_ref, dp_in,
             dq_ref, dkb_ref, dvb_ref, db_ref, ds_ref, qs_buf, dos_buf, s_buf, dp_buf, p_buf, dsc_buf):
        gp = pl.program_id(0)
        n = pl.program_id(1)

        @pl.when(n == 0)
        def _():
            db_ref[...] = jnp.zeros_like(db_ref)
            ds_ref[...] = jnp.zeros_like(ds_ref)

        valid = _band_valid(n)
        kks = _kv_bands(kp_ref, kc_ref)
        vvs = _kv_bands(vp_ref, vc_ref)
        lane_b = lax.broadcasted_iota(jnp.int32, (2 * BLOCK, 128), 1)
        dks, dvs = [], []
        for kv in range(2):
            _stack_heads(q_ref, kv, qs_buf)
            _stack_heads(do_ref, kv, dos_buf)
            s_buf[...] = _dot_nt(qs_buf[...], kks[kv])
            dp_buf[...] = _dot_nt(dos_buf[...], vvs[kv])
            for hq in range(8):
                hl = 8 * kv + hq
                rows = pl.ds(hq * BLOCK, BLOCK)
                p, ps = _attn_probs(s_buf[rows, :], b_ref[hl], sk_ref[gp * 16 + hl], valid)
                dp = dp_buf[rows, :]
                delta = jnp.sum(p * dp, axis=-1, keepdims=True)
                dsc = p * (dp - delta)
                db_ref[hl] += dsc
                ds_ref[hl:hl + 1, :] += jnp.zeros((1, 128), F32) - jnp.sum(ps * delta)
                p_buf[rows, :] = p.astype(BF16)
                dsc_buf[rows, :] = (dsc * SCALE).astype(BF16)
            _unstack_heads(_dot(dsc_buf[...], kks[kv]), dq_ref, kv)
            dk = _dot_tn(dsc_buf[...], qs_buf[...])
            dv = _dot_tn(p_buf[...], dos_buf[...])
            dks.append(dk + pltpu.roll(dk, 64, 1))
            dvs.append(dv + pltpu.roll(dv, 64, 1))
        dkb_ref[...] = jnp.where(lane_b < 64, dks[0], dks[1])
        dvb_ref[...] = jnp.where(lane_b < 64, dvs[0], dvs[1])

    kb, vb = OFF_K // 128, OFF_V // 128
    return pl.pallas_call(
        body, name="attn_bwd", grid=(2, nb),
        in_specs=[pl.BlockSpec(memory_space=pltpu.SMEM),
                  pl.BlockSpec((BLOCK, qw), lambda g, n: (n, OFF_Q // qw + g)),
                  pl.BlockSpec((BLOCK, 128), lambda g, n: (jnp.maximum(n - 1, 0), kb + g)),
                  pl.BlockSpec((BLOCK, 128), lambda g, n: (n, kb + g)),
                  pl.BlockSpec((BLOCK, 128), lambda g, n: (jnp.maximum(n - 1, 0), vb + g)),
                  pl.BlockSpec((BLOCK, 128), lambda g, n: (n, vb + g)),
                  pl.BlockSpec((16, BLOCK, 2 * BLOCK), lambda g, n: (g, 0, 0)),
                  pl.BlockSpec((BLOCK, qw), lambda g, n: (n, g)),
                  pl.BlockSpec(memory_space=pl.ANY)],
        out_specs=[pl.BlockSpec((BLOCK, qw), lambda g, n: (n, OFF_Q // qw + g)),
                   pl.BlockSpec((2 * BLOCK, 128), lambda g, n: (n, g)),
                   pl.BlockSpec((2 * BLOCK, 128), lambda g, n: (n, g)),
                   pl.BlockSpec((16, BLOCK, 2 * BLOCK), lambda g, n: (g, 0, 0)),
                   pl.BlockSpec((16, 128), lambda g, n: (g, 0))],
        out_shape=[SDS((s, IN_W), BF16), SDS((nb * 2 * BLOCK, 256), F32), SDS((nb * 2 * BLOCK, 256), F32),
                   SDS((N_HEADS, BLOCK, 2 * BLOCK), F32), SDS((N_HEADS, 128), F32)],
        input_output_aliases={8: 0},
        scratch_shapes=[pltpu.VMEM((8 * BLOCK, 128), BF16), pltpu.VMEM((8 * BLOCK, 128), BF16),
                        pltpu.VMEM((8 * BLOCK, 2 * BLOCK), F32), pltpu.VMEM((8 * BLOCK, 2 * BLOCK), F32),
                        pltpu.VMEM((8 * BLOCK, 2 * BLOCK), BF16), pltpu.VMEM((8 * BLOCK, 2 * BLOCK), BF16)],
        compiler_params=_cp(2))(sinks, proj, proj, proj, proj, proj, band, datt, dproj)


def dkv_combine(dkb, dvb, dproj):
    nb = dkb.shape[0] // (2 * BLOCK)
    s = nb * BLOCK
    dkb3 = dkb.reshape(nb, 2 * BLOCK, 256)
    dvb3 = dvb.reshape(nb, 2 * BLOCK, 256)

    def body(k1, k2, v1, v2, dp_in, o_ref):
        nxt = jnp.where(pl.program_id(0) < nb - 1, 1.0, 0.0)
        o_ref[:, 0:256] = (k1[...] + nxt * k2[...]).astype(BF16)
        o_ref[:, 256:512] = (v1[...] + nxt * v2[...]).astype(BF16)

    spec1 = pl.BlockSpec((None, BLOCK, 256), lambda m: (m, 1, 0))
    spec2 = pl.BlockSpec((None, BLOCK, 256), lambda m: (jnp.minimum(m + 1, nb - 1), 0, 0))
    return pl.pallas_call(
        body, name="dkv_combine", grid=(nb,),
        in_specs=[spec1, spec2, spec1, spec2, pl.BlockSpec(memory_space=pl.ANY)],
        out_specs=pl.BlockSpec((BLOCK, 512), lambda m: (m, OFF_K // 512)),
        out_shape=SDS((s, IN_W), BF16), input_output_aliases={4: 0},
        compiler_params=_cp(1))(dkb3, dkb3, dvb3, dvb3, dproj)


def lru_bwd(proj, rec, drec, lvec, wa, wx, dproj):
    s = proj.shape[0]
    t = min(256, s)
    nt = s // t

    def body(lx_ref, lxh_ref, rec_ref, rech_ref, dr_ref, lv_ref, wa_ref, wx_ref, dp_in,
             dlx_ref, sums_ref, dwa_ref, dwx_ref,
             xbuf, hbuf, dxbuf, a_s, dh_s, xc_s, r_s, ig_s, mu_s, gc):
        step_i = pl.program_id(0)
        ti = nt - 1 - step_i

        @pl.when(step_i == 0)
        def _():
            sums_ref[...] = jnp.zeros_like(sums_ref)
            dwa_ref[...] = jnp.zeros_like(dwa_ref)
            dwx_ref[...] = jnp.zeros_like(dwx_ref)
            dxbuf[pl.ds(t, 8), :] = jnp.zeros((8, D), F32)
            gc[...] = jnp.zeros((8, D), F32)

        live = jnp.where(ti > 0, 1.0, 0.0)
        xbuf[pl.ds(0, 8), :] = lxh_ref[...].astype(F32)[8:16] * live
        xbuf[pl.ds(8, t), :] = lx_ref[...].astype(F32)
        hbuf[pl.ds(0, 8), :] = rech_ref[...] * live
        hbuf[pl.ds(8, t), :] = rec_ref[...]
        first = (lax.broadcasted_iota(jnp.int32, (t, 128), 0) + ti * t) == 0
        for b in range(N_LRU_BLOCKS):
            cs = slice(b * 128, (b + 1) * 128)
            _, xc, _, r, ig, _, a, mult = _lru_block_fwd(xbuf, lv_ref, wa_ref, wx_ref, b, t, first)
            a_s[:, cs] = a
            xc_s[:, cs] = xc
            r_s[:, cs] = r
            ig_s[:, cs] = ig
            mu_s[:, cs] = mult

        def step(q, g):
            tt = t - 1 - q
            dh = dr_ref[pl.ds(tt, 1), :] + g
            dh_s[pl.ds(tt, 1), :] = dh
            return a_s[pl.ds(tt, 1), :] * dh

        gc[0:1, :] = lax.fori_loop(0, t, step, gc[0:1, :], unroll=8)
        for b in range(N_LRU_BLOCKS):
            cs = slice(b * 128, (b + 1) * 128)
            dh = dh_s[:, cs]
            a = a_s[:, cs]
            xc = xc_s[:, cs]
            r = r_s[:, cs]
            ig = ig_s[:, cs]
            mult = mu_s[:, cs]
            sp = _softplus(-lv_ref[L_LAM:L_LAM + 1, cs])
            lam = lv_ref[L_LAM:L_LAM + 1, cs]
            da = dh * hbuf[pl.ds(7, t), cs]
            dmult = jnp.where(first, 0.0, dh * ig * xc)
            dig = dh * mult * xc
            dxc = dh * mult * ig
            dlog_a = da * a - dmult * (a * a) / mult
            dr = dlog_a * ((-LRU_C) * sp)
            dsp = jnp.sum(dlog_a * ((-LRU_C) * r), axis=0, keepdims=True)
            dza = dr * r * (1.0 - r)
            dzx = dig * ig * (1.0 - ig)
            dzab = dza.astype(BF16)
            dzxb = dzx.astype(BF16)
            xcb = xc.astype(BF16)
            dwa_ref[b] += _dot_tn(xcb, dzab)
            dwx_ref[b] += _dot_tn(xcb, dzxb)
            dxc = dxc + _dot_nt(dzab, wa_ref[b]) + _dot_nt(dzxb, wx_ref[b])
            sums_ref[L_LAM:L_LAM + 1, cs] += dsp * (-jax.nn.sigmoid(-lam))
            sums_ref[L_BA:L_BA + 1, cs] += jnp.sum(dza, axis=0, keepdims=True)
            sums_ref[L_BX:L_BX + 1, cs] += jnp.sum(dzx, axis=0, keepdims=True)
            sums_ref[L_CB:L_CB + 1, cs] += jnp.sum(dxc, axis=0, keepdims=True)
            for kk in range(4):
                sums_ref[kk:kk + 1, cs] += jnp.sum(dxc * xbuf[pl.ds(5 + kk, t), cs], axis=0, keepdims=True)
            dxbuf[pl.ds(0, t), cs] = dxc
            dlx = (lv_ref[3:4, cs] * dxc + lv_ref[2:3, cs] * dxbuf[pl.ds(1, t), cs]
                   + lv_ref[1:2, cs] * dxbuf[pl.ds(2, t), cs] + lv_ref[0:1, cs] * dxbuf[pl.ds(3, t), cs])
            dlx_ref[:, cs] = dlx.astype(BF16)
        dxbuf[pl.ds(t, 8), :] = dxbuf[pl.ds(0, 8), :]

    rev = lambda i: nt - 1 - i
    return pl.pallas_call(
        body, name="lru_bwd", grid=(nt,),
        in_specs=[pl.BlockSpec((t, D), lambda i: (rev(i), 0)),
                  pl.BlockSpec((16, D), lambda i: (jnp.maximum(rev(i) * (t // 16) - 1, 0), 0)),
                  pl.BlockSpec((t, D), lambda i: (rev(i), 0)),
                  pl.BlockSpec((8, D), lambda i: (jnp.maximum(rev(i) * (t // 8) - 1, 0), 0)),
                  pl.BlockSpec((t, D), lambda i: (rev(i), 0)),
                  pl.BlockSpec((8, D), lambda i: (0, 0)),
                  pl.BlockSpec((N_LRU_BLOCKS, 128, 128), lambda i: (0, 0, 0)),
                  pl.BlockSpec((N_LRU_BLOCKS, 128, 128), lambda i: (0, 0, 0)),
                  pl.BlockSpec(memory_space=pl.ANY)],
        out_specs=[pl.BlockSpec((t, D), lambda i: (rev(i), 0)),
                   pl.BlockSpec((8, D), lambda i: (0, 0)),
                   pl.BlockSpec((N_LRU_BLOCKS, 128, 128), lambda i: (0, 0, 0)),
                   pl.BlockSpec((N_LRU_BLOCKS, 128, 128), lambda i: (0, 0, 0))],
        out_shape=[SDS((s, IN_W), BF16), SDS((8, D), F32), SDS((N_LRU_BLOCKS, 128, 128), F32),
                   SDS((N_LRU_BLOCKS, 128, 128), F32)],
        scratch_shapes=[pltpu.VMEM((t + 8, D), F32), pltpu.VMEM((t + 8, D), F32), pltpu.VMEM((t + 8, D), F32)]
        + [pltpu.VMEM((t, D), F32)] * 6 + [pltpu.VMEM((8, D), F32)],
        input_output_aliases={8: 0},
        compiler_params=_cp(1))(proj, proj, rec, rec, drec, lvec, wa, wx, dproj)


def inproj_bwd(dproj, w_in, x, dx1, vecs):
    s = x.shape[0]
    tm, tk = min(512, s), IN_TILE
    nk = IN_W // tk
    per = IN_SHARD // tk

    def body(d_ref, w_ref, x_hbm, dx1_hbm, v_ref, gx_ref, sums_ref, acc, x_ref, dx1_ref, sems):
        i, k = pl.program_id(0), pl.program_id(1)
        fetches = _row_fetches((x_hbm, dx1_hbm), (x_ref, dx1_ref), sems, i, tm)

        @pl.when((i == 0) & (k == 0))
        def _():
            sums_ref[...] = jnp.zeros_like(sums_ref)

        @pl.when(k == 0)
        def _():
            acc[...] = jnp.zeros_like(acc)
            for cp in fetches:
                cp.start()

        acc[...] += _dot_nt(d_ref[...], w_ref[...])

        @pl.when(k == nk - 1)
        def _():
            for cp in fetches:
                cp.wait()
            g1 = v_ref[V_G1:V_G1 + 1, :]
            scale1 = v_ref[V_SCALE1:V_SCALE1 + 1, :]

            def sub(rb, carry):
                rs = pl.ds(pl.multiple_of(rb * SUB, SUB), SUB)
                dh = acc[rs, :]
                r1, xh = _rms_parts(x_ref[rs, :])
                sums_ref[0:1, :] += jnp.sum(dh, axis=0, keepdims=True)
                sums_ref[1:2, :] += jnp.sum(dh * (xh * g1), axis=0, keepdims=True)
                dxn = dh * (1.0 + scale1)
                sums_ref[2:3, :] += jnp.sum(dxn * xh, axis=0, keepdims=True)
                dxh = dxn * g1
                gx_ref[rs, :] = dx1_ref[rs, :] + r1 * (dxh - xh * jnp.mean(dxh * xh, axis=-1, keepdims=True))
                return carry

            lax.fori_loop(0, tm // SUB, sub, 0)

    return pl.pallas_call(
        body, name="inproj_bwd", grid=(s // tm, nk),
        in_specs=[pl.BlockSpec((tm, tk), lambda i, k: (i, k)),
                  pl.BlockSpec((None, D, tk), lambda i, k: (k // per, 0, k % per)),
                  pl.BlockSpec(memory_space=pl.ANY), pl.BlockSpec(memory_space=pl.ANY),
                  pl.BlockSpec((16, D), lambda i, k: (0, 0))],
        out_specs=[pl.BlockSpec((tm, D), lambda i, k: (i, 0)), pl.BlockSpec((8, D), lambda i, k: (0, 0))],
        out_shape=[SDS((s, D), F32), SDS((8, D), F32)],
        scratch_shapes=[pltpu.VMEM((tm, D), F32), pltpu.VMEM((tm, D), F32), pltpu.VMEM((tm, D), F32),
                        pltpu.SemaphoreType.DMA((2,))],
        compiler_params=_cp(2))(dproj, w_in, x, dx1, vecs)


def mod_columns(c16, w_ada, b_cols):
    tn = 512

    def body(c_ref, w_ref, b_ref, o_ref):
        cv = c_ref[...]
        ca = (cv * jax.nn.sigmoid(cv)).astype(BF16)
        o_ref[...] = _dot(ca, w_ref[...].astype(BF16)) + b_ref[...]

    return pl.pallas_call(
        body, name="mod_columns", grid=(ADA_SHARD // tn,),
        in_specs=[pl.BlockSpec((16, D), lambda j: (0, 0)), pl.BlockSpec((D, tn), lambda j: (0, j)),
                  pl.BlockSpec((1, tn), lambda j: (0, j))],
        out_specs=pl.BlockSpec((16, tn), lambda j: (0, j)),
        out_shape=SDS((16, ADA_SHARD), F32), compiler_params=_cp(1))(c16, w_ada, b_cols)


def wada_update(c16, dmod16, w, m, v):
    tm, tn = 512, 512

    def body(c_ref, d_ref, w_ref, m_ref, v_ref, g_out, dl_out, m_out, v_out):
        cv = c_ref[...]
        ca = (cv * jax.nn.sigmoid(cv)).astype(BF16)
        g = _dot_tn(ca, d_ref[...].astype(BF16))
        dl, m2, v2 = _adamw_math(w_ref[...], g, m_ref[...], v_ref[...])
        g_out[...] = g
        dl_out[...] = dl
        m_out[...] = m2
        v_out[...] = v2

    tile = pl.BlockSpec((tm, tn), lambda i, j: (i, j))
    return pl.pallas_call(
        body, name="wada_update", grid=(D // tm, ADA_SHARD // tn),
        in_specs=[pl.BlockSpec((16, tm), lambda i, j: (0, i)), pl.BlockSpec((16, tn), lambda i, j: (0, j)),
                  tile, tile, tile],
        out_specs=[tile] * 4, out_shape=[SDS((D, ADA_SHARD), F32)] * 4,
        compiler_params=_cp(2))(c16, dmod16, w, m, v)


def adamw_big(name, w, mine, theirs, m, v, c_idx):
    r, c = w.shape
    tr = 128
    per = (r // 2) // tr

    def body(c_ref, w_ref, a_ref, b_ref, m_ref, v_ref, g_out, dl_out, m_out, v_out):
        own = (pl.program_id(0) // per) == c_ref[0]
        g = jnp.where(own, a_ref[...], b_ref[...])
        dl, m2, v2 = _adamw_math(w_ref[...], g, m_ref[...], v_ref[...])
        g_out[...] = g
        dl_out[...] = dl
        m_out[...] = m2
        v_out[...] = v2

    tile = pl.BlockSpec((tr, c), lambda i, cr: (i, 0))
    half = pl.BlockSpec((tr, c), lambda i, cr: (i % per, 0))
    gs = pltpu.PrefetchScalarGridSpec(num_scalar_prefetch=1, grid=(r // tr,),
                                      in_specs=[tile, half, half, tile, tile], out_specs=[tile] * 4)
    return pl.pallas_call(body, name=name, grid_spec=gs, out_shape=[SDS((r, c), F32)] * 4,
                          compiler_params=_cp(1))(c_idx, w, mine, theirs, m, v)


def cast_into_slot(name, w, k_idx):
    r, c = w.shape
    tr = 256

    def body(k_ref, w_ref, o_ref):
        o_ref[...] = w_ref[...].astype(BF16)

    gs = pltpu.PrefetchScalarGridSpec(
        num_scalar_prefetch=1, grid=(r // tr,),
        in_specs=[pl.BlockSpec((tr, c), lambda i, kr: (i, 0))],
        out_specs=pl.BlockSpec((None, tr, c), lambda i, kr: (kr[0], i, 0)))
    return pl.pallas_call(body, name=name, grid_spec=gs, out_shape=SDS((N_CHIPS, r, c), BF16),
                          compiler_params=_cp(1))(k_idx, w)


def adamw_small(ws, gs, ms, vs):
    n = len(ws)

    def body(*refs):
        for i in range(n):
            dl, m2, v2 = _adamw_math(refs[i][...], refs[n + i][...], refs[2 * n + i][...], refs[3 * n + i][...])
            refs[4 * n + i][...] = dl
            refs[5 * n + i][...] = m2
            refs[6 * n + i][...] = v2

    vm = pl.BlockSpec(memory_space=pltpu.VMEM)
    shapes = [SDS(w.shape, F32) for w in ws]
    outs = pl.pallas_call(
        body, name="adamw_small", in_specs=[vm] * (4 * n), out_specs=[vm] * (3 * n), out_shape=shapes * 3,
        compiler_params=pltpu.CompilerParams(vmem_limit_bytes=VMEM_LIMIT))(*ws, *gs, *ms, *vs)
    return outs[:n], outs[n:2 * n], outs[2 * n:]


def sum_devices(name, gathered):
    rows, cols = gathered.shape[1:]
    tr = min(rows, 128 * D // cols)

    def body(x_ref, o_ref):
        acc = x_ref[0].astype(F32)
        for d in range(1, N_DEV):
            acc = acc + x_ref[d].astype(F32)
        o_ref[...] = acc

    return pl.pallas_call(
        body, name=name, grid=(rows // tr,),
        in_specs=[pl.BlockSpec((N_DEV, tr, cols), lambda i: (0, i, 0))],
        out_specs=pl.BlockSpec((tr, cols), lambda i: (i, 0)),
        out_shape=SDS((rows, cols), F32), compiler_params=_cp(1))(gathered)


def _mesh_pos():
    return lax.axis_index("x"), lax.axis_index("y"), lax.axis_index("c")


def _other_chips(x, y):
    return [(1 - x, y), (x, 1 - y), (1 - x, 1 - y)]


def all_gather_small(name, block):
    m_per, n = block.shape

    def body(x_ref, out_ref, send_sems, recv_sems, local_sem):
        x, y, c = _mesh_pos()
        me, sibling = (x, y, c), (x, y, 1 - c)
        chips = _other_chips(x, y)

        def rows(px, py, pc):
            return out_ref.at[pl.ds((4 * px + 2 * py + pc) * m_per, m_per), :]

        def copy(k, blk, to, src=None):
            return pltpu.make_async_remote_copy(
                src_ref=rows(*blk) if src is None else src, dst_ref=rows(*blk),
                send_sem=send_sems.at[k], recv_sem=recv_sems.at[k], device_id=to, device_id_type=MESH)

        mine = pltpu.make_async_copy(x_ref, rows(*me), local_sem)
        mine.start()
        first = [copy(0, me, sibling, src=x_ref)]
        first += [copy(1 + j, me, (*chip, c), src=x_ref) for j, chip in enumerate(chips)]
        for cp in first:
            cp.start()
        passed = [copy(4 + j, (*chip, c), sibling) for j, chip in enumerate(chips)]
        for j, chip in enumerate(chips):
            copy(1 + j, (*chip, c), me).wait_recv()
            passed[j].start()
        copy(0, sibling, me).wait_recv()
        for j, chip in enumerate(chips):
            copy(4 + j, (*chip, 1 - c), me).wait_recv()
        for cp in first + passed:
            cp.wait_send()
        mine.wait()

    vm = pl.BlockSpec(memory_space=pltpu.VMEM)
    return pl.pallas_call(
        body, name=name, out_shape=SDS((N_DEV * m_per, n), block.dtype), in_specs=[vm], out_specs=vm,
        scratch_shapes=[pltpu.SemaphoreType.DMA((7,)), pltpu.SemaphoreType.DMA((7,)), pltpu.SemaphoreType.DMA],
        compiler_params=pltpu.CompilerParams(vmem_limit_bytes=VMEM_LIMIT))(block)


def sibling_sum(name, grad, other, c_idx):
    _, r, cc = grad.shape
    h = r // 2
    tr = min(256, h)
    g4 = grad.reshape(N_CHIPS, 2, h, cc)

    def body(c_ref, a_ref, b_ref, o_ref):
        o_ref[...] = (a_ref[...].astype(F32) + b_ref[...].astype(F32)).astype(BF16)

    gs = pltpu.PrefetchScalarGridSpec(
        num_scalar_prefetch=1, grid=(N_CHIPS, h // tr),
        in_specs=[pl.BlockSpec((None, None, tr, cc), lambda s, i, cr: (s, cr[0], i, 0)),
                  pl.BlockSpec((None, tr, cc), lambda s, i, cr: (s, i, 0))],
        out_specs=pl.BlockSpec((None, tr, cc), lambda s, i, cr: (s, i, 0)))
    return pl.pallas_call(body, name=name, grid_spec=gs, out_shape=SDS((N_CHIPS, h, cc), BF16),
                          compiler_params=_cp(2))(c_idx, g4, other)


HBM_SPEC = pl.BlockSpec(memory_space=pltpu.HBM)
SEM_SPEC = pl.BlockSpec(memory_space=pltpu.SEMAPHORE)


def _side_effecting():
    return pltpu.CompilerParams(has_side_effects=pltpu.SideEffectType.DATAFLOW_SIDE_EFFECTING)


def _in_hbm(a):
    return pltpu.with_memory_space_constraint(a, pltpu.HBM)


ALL_CHIPS = (0, 1, 2)


def gather_start(name, bufs, after, rel=ALL_CHIPS, carry=None):
    n = len(bufs)
    nr = len(rel)
    halves = [w.shape[1] // 2 for w in bufs]
    extra = [] if carry is None else [carry]

    def body(*refs):
        ins = refs[:n]
        send_sems, recv_sems, token = refs[n + 1 + len(extra)], refs[n + 2 + len(extra)], refs[-1]
        x, y, c = _mesh_pos()
        k = 2 * x + y
        for i in range(n):
            reg = ins[i].at[k, pl.ds(c * halves[i], halves[i]), :]
            for q, j in enumerate(rel):
                chip = _other_chips(x, y)[j]
                pltpu.make_async_remote_copy(src_ref=reg, dst_ref=reg, send_sem=send_sems.at[nr * i + q],
                                             recv_sem=recv_sems.at[nr * i + q], device_id=(*chip, c),
                                             device_id_type=MESH).start()
        token[...] = jnp.zeros_like(token)

    outs = pl.pallas_call(
        body, name=name,
        out_shape=(pltpu.SemaphoreType.DMA((nr * n,)), pltpu.SemaphoreType.DMA((nr * n,)),
                   *[pltpu.HBM(w.shape, w.dtype) for w in list(bufs) + extra], SDS((8, 128), F32)),
        in_specs=[HBM_SPEC] * n + [pl.BlockSpec(memory_space=pl.ANY)] + [HBM_SPEC] * len(extra),
        out_specs=(SEM_SPEC, SEM_SPEC, *[HBM_SPEC] * (n + len(extra)), pl.BlockSpec(memory_space=pltpu.VMEM)),
        input_output_aliases={**{i: 2 + i for i in range(n)}, **({n + 1: 2 + n} if extra else {})},
        compiler_params=_side_effecting())(*[_in_hbm(w) for w in bufs], after, *[_in_hbm(w) for w in extra])
    return (outs[0], outs[1], list(outs[2:2 + n]), outs[-1]) + ((outs[2 + n],) if extra else ())


def gather_wait(name, send_sems, recv_sems, bufs, after, rel=ALL_CHIPS):
    n = len(bufs)
    nr = len(rel)
    halves = [w.shape[1] // 2 for w in bufs]

    def body(*refs):
        ins = refs[:n]
        send_sems, recv_sems = refs[n], refs[n + 1]
        x, y, c = _mesh_pos()
        k = 2 * x + y
        for i in range(n):
            for q, j in enumerate(rel):
                chip = _other_chips(x, y)[j]
                kj = 2 * chip[0] + chip[1]
                cp = pltpu.make_async_remote_copy(
                    src_ref=ins[i].at[k, pl.ds(c * halves[i], halves[i]), :],
                    dst_ref=ins[i].at[kj, pl.ds(c * halves[i], halves[i]), :],
                    send_sem=send_sems.at[nr * i + q], recv_sem=recv_sems.at[nr * i + q], device_id=(*chip, c),
                    device_id_type=MESH)
                cp.wait_send()
                cp.wait_recv()

    return pl.pallas_call(
        body, name=name, out_shape=[pltpu.HBM(w.shape, w.dtype) for w in bufs],
        in_specs=[HBM_SPEC] * n + [SEM_SPEC, SEM_SPEC, pl.BlockSpec(memory_space=pl.ANY)],
        out_specs=[HBM_SPEC] * n, input_output_aliases={i: i for i in range(n)},
        compiler_params=_side_effecting())(*bufs, send_sems, recv_sems, after)


def gather_forward(name, bufs, rel=ALL_CHIPS):
    n = len(bufs)
    halves = [w.shape[1] // 2 for w in bufs]

    def body(*refs):
        outs = refs[n:2 * n]
        send_sems, recv_sems = refs[2 * n:]
        x, y, c = _mesh_pos()
        chips = _other_chips(x, y)

        def copy(i, j, half, to):
            kj = 2 * chips[j][0] + chips[j][1]
            reg = outs[i].at[kj, pl.ds(half * halves[i], halves[i]), :]
            return pltpu.make_async_remote_copy(src_ref=reg, dst_ref=reg, send_sem=send_sems.at[i, j],
                                                recv_sem=recv_sems.at[i, j], device_id=to, device_id_type=MESH)

        cps = [copy(i, j, c, (x, y, 1 - c)) for i in range(n) for j in rel]
        for cp in cps:
            cp.start()
        for i in range(n):
            for j in rel:
                copy(i, j, 1 - c, (x, y, c)).wait_recv()
        for cp in cps:
            cp.wait_send()

    hbm = pl.BlockSpec(memory_space=pl.ANY)
    return pl.pallas_call(
        body, name=name, in_specs=[hbm] * n, out_specs=[hbm] * n,
        out_shape=[SDS(w.shape, w.dtype) for w in bufs], input_output_aliases={i: i for i in range(n)},
        scratch_shapes=[pltpu.SemaphoreType.DMA((n, 3)), pltpu.SemaphoreType.DMA((n, 3))])(*bufs)


def _exchange_plan(kind, srcs, zones):
    x, y, c = _mesh_pos()
    plan = []
    for src, zone in zip(srcs, zones):
        if kind == "chips":
            for j, chip in enumerate(_other_chips(x, y)):
                plan.append((src.at[2 * chip[0] + chip[1]], zone.at[j], (*chip, c)))
        elif kind == "sibling":
            h = zone.shape[1]
            plan.append((src.at[:, pl.ds((1 - c) * h, h), :], zone, (x, y, 1 - c)))
        else:
            peers = [(x, y, 1 - c)] + [(*chip, cc) for chip in _other_chips(x, y) for cc in (c, 1 - c)]
            plan += [(src, zone.at[4 * x + 2 * y + c], peer) for peer in peers]
    return plan


_COPIES_PER_ARRAY = {"chips": 3, "sibling": 1, "all": N_DEV - 1}


def _landing_zones(kind, srcs):
    if kind == "chips":
        return [lax.empty((3,) + t.shape[1:], t.dtype) for t in srcs]
    if kind == "sibling":
        return [lax.empty((t.shape[0], t.shape[1] // 2, t.shape[2]), t.dtype) for t in srcs]
    return [jnp.broadcast_to(t, (N_DEV,) + t.shape) for t in srcs]


def exchange_start(name, kind, srcs, after):
    n = len(srcs)
    lands = _landing_zones(kind, srcs)
    n_copies = n * _COPIES_PER_ARRAY[kind]

    def body(*refs):
        send_sems, recv_sems, token = refs[2 * n + 1], refs[2 * n + 2], refs[-1]
        for q, (src, dst, dev) in enumerate(_exchange_plan(kind, refs[:n], refs[n:2 * n])):
            pltpu.make_async_remote_copy(src_ref=src, dst_ref=dst, send_sem=send_sems.at[q], recv_sem=recv_sems.at[q],
                                         device_id=dev, device_id_type=MESH).start()
        token[...] = jnp.zeros_like(token)

    outs = pl.pallas_call(
        body, name=name,
        out_shape=(pltpu.SemaphoreType.DMA((n_copies,)), pltpu.SemaphoreType.DMA((n_copies,)),
                   *[pltpu.HBM(t.shape, t.dtype) for t in srcs], *[pltpu.HBM(t.shape, t.dtype) for t in lands],
                   SDS((8, 128), F32)),
        in_specs=[HBM_SPEC] * (2 * n) + [pl.BlockSpec(memory_space=pl.ANY)],
        out_specs=(SEM_SPEC, SEM_SPEC, *[HBM_SPEC] * (2 * n), pl.BlockSpec(memory_space=pltpu.VMEM)),
        input_output_aliases={i: 2 + i for i in range(2 * n)},
        compiler_params=_side_effecting())(*[_in_hbm(t) for t in srcs], *[_in_hbm(t) for t in lands], after)
    return outs[0], outs[1], list(outs[2:2 + n]), list(outs[2 + n:2 + 2 * n]), outs[-1]


def exchange_wait(name, kind, send_sems, recv_sems, srcs, lands, after):
    n = len(srcs)

    def body(*refs):
        send_sems, recv_sems = refs[2 * n], refs[2 * n + 1]
        for q, (src, dst, dev) in enumerate(_exchange_plan(kind, refs[:n], refs[n:2 * n])):
            cp = pltpu.make_async_remote_copy(src_ref=src, dst_ref=dst, send_sem=send_sems.at[q],
                                              recv_sem=recv_sems.at[q], device_id=dev, device_id_type=MESH)
            cp.wait_send()
            cp.wait_recv()

    outs = pl.pallas_call(
        body, name=name, out_shape=[pltpu.HBM(t.shape, t.dtype) for t in srcs + lands],
        in_specs=[HBM_SPEC] * (2 * n) + [SEM_SPEC, SEM_SPEC, pl.BlockSpec(memory_space=pl.ANY)],
        out_specs=[HBM_SPEC] * (2 * n), input_output_aliases={i: i for i in range(2 * n)},
        compiler_params=_side_effecting())(*srcs, *lands, send_sems, recv_sems, after)
    return list(outs[:n]), list(outs[n:])


def chip_sum(name, sums, parts, k_idx):
    _, h, cc = parts.shape
    tr = min(256, h)

    def body(k_ref, own_ref, p_ref, o_ref):
        acc = own_ref[...].astype(F32)
        for s in range(3):
            acc = acc + p_ref[s].astype(F32)
        o_ref[...] = acc

    gs = pltpu.PrefetchScalarGridSpec(
        num_scalar_prefetch=1, grid=(h // tr,),
        in_specs=[pl.BlockSpec((None, tr, cc), lambda i, kr: (kr[0], i, 0)),
                  pl.BlockSpec((3, tr, cc), lambda i, kr: (0, i, 0))],
        out_specs=pl.BlockSpec((tr, cc), lambda i, kr: (i, 0)))
    return pl.pallas_call(body, name=name, grid_spec=gs, out_shape=SDS((h, cc), F32),
                          compiler_params=_cp(1))(k_idx, sums, parts)


def halves_exchange(name, halves):
    n = len(halves)

    def body(*refs):
        ins, outs = refs[:n], refs[n:2 * n]
        send_sems, recv_sems = refs[2 * n:]
        x, y, c = _mesh_pos()
        cps = []
        for i in range(n):
            cp = pltpu.make_async_remote_copy(
                src_ref=ins[i], dst_ref=outs[i], send_sem=send_sems.at[i], recv_sem=recv_sems.at[i],
                device_id=(x, y, 1 - c), device_id_type=MESH)
            cp.start()
            cps.append(cp)
        for cp in cps:
            cp.wait_recv()
        for cp in cps:
            cp.wait_send()

    hbm = pl.BlockSpec(memory_space=pl.ANY)
    return pl.pallas_call(
        body, name=name, in_specs=[hbm] * n, out_specs=[hbm] * n,
        out_shape=[SDS(t.shape, F32) for t in halves],
        scratch_shapes=[pltpu.SemaphoreType.DMA((n,)), pltpu.SemaphoreType.DMA((n,))])(*halves)


def local_step(x, tgt, vecs, lvec, wa, wx, sinks, rel_bias, proj, h, w_in, rest_weights, hook):
    buckets = t5_bucket_table()
    band = bias_band(rel_bias.T, buckets).reshape(N_HEADS, BLOCK, 2 * BLOCK)

    ya, rec, ya_t = lru_fwd(proj, lvec, wa, wx)
    att, att_t = attn_fwd(proj, band, sinks)
    w_lru_out, w_attn_out, w_out = rest_weights("mix", att[:8, :128] + ya[:8, :128])
    w_lru_out2, w_attn_out2, w_out2 = w_lru_out.reshape(D, D), w_attn_out.reshape(D, D), w_out.reshape(D, D)
    yab, merged, merged_t = merge_fwd(ya, att, w_lru_out2, w_attn_out2, proj)
    x1, o1 = outproj_fwd(merged, w_out2, x, vecs)
    w_ff1, w_ff2 = rest_weights("ff", o1[:8, :128])
    w_ff2_2 = w_ff2.reshape(D_FF, D)
    f, h2, fft = ff1_fwd(x1, vecs, w_ff1)
    dx2, do2, sums_f, loss = ff2_loss(f, w_ff2_2, x1, tgt, vecs)

    df = ff2_bwd(do2, w_ff2_2, f)
    g_ff2 = matmul_bf16("dw_ff2", fft, do2, WG_TM, 512)
    dx1, do1, sums_2 = ff1_bwd(df, w_ff1, x1, dx2, o1, vecs)
    g_ff1 = weight_grad("dw_ff1", h2, df, 512, (N_CHIPS, D, D), (None, WG_TM, 512), lambda i, j: (j // 4, i, j % 4))
    dyab, dproj = outproj_bwd(do1, w_out2, yab, proj)
    g_out = matmul_bf16("dw_out", merged_t, do1, WG_TM, 512)
    drec, dproj = lruout_bwd(dyab, w_lru_out2, rec, proj, dproj)
    dyab2 = dyab.reshape(2 * x.shape[0], D)
    g_lru_out = matmul_bf16("dw_lru_out", ya_t, dyab2, WG_TM, 512)
    datt = attnout_bwd(dyab, w_attn_out2)
    g_attn_out = matmul_bf16("dw_attn_out", att_t, dyab2, WG_TM, 512, b_part=1)
    zero = hook("grads_a", [g_lru_out.reshape(N_CHIPS, D // 4, D), g_attn_out.reshape(N_CHIPS, D // 4, D),
                            g_out.reshape(N_CHIPS, D // 4, D), g_ff1, g_ff2.reshape(N_CHIPS, D_FF // 4, D)])
    dproj, dkb, dvb, dband, dsink = attn_bwd(proj, band, sinks + zero, datt, dproj)
    zero = hook("after_attn_bwd", dkb)
    dproj = dkv_combine(dkb, dvb, dproj)
    dproj, sums_l, d_wa, d_wx = lru_bwd(proj, rec, drec, lvec + zero, wa, wx, dproj)
    hook("lru_grads", (d_wa, d_wx))
    per = IN_SHARD // IN_TILE
    g_in = weight_grad("dw_in", h, dproj, IN_TILE, (N_CHIPS, D, IN_SHARD), (None, WG_TM, IN_TILE),
                       lambda i, j: (j // per, i, j % per))
    zero = hook("grads_b", [g_in])
    grad_x, sums_1 = inproj_bwd(dproj, w_in, x, dx1, vecs + zero)
    d_rel_bias = bias_band_bwd(dband.reshape(N_HEADS, BLOCK * 2 * BLOCK), buckets)

    small = dict(sums_f=sums_f, sums_2=sums_2, sums_1=sums_1, sums_l=sums_l, d_wa=d_wa, d_wx=d_wx,
                 d_sinks=dsink[:, 0], d_rel_bias=d_rel_bias)
    return loss, grad_x, small


def _pad_rows(a, rows):
    return jnp.concatenate([a, jnp.zeros((rows - a.shape[0], a.shape[1]), a.dtype)], axis=0)


def kernel(x, c, w_ada, b_ada, norm1_g, w_in, conv_w, conv_b, lru_wa, lru_ba, lru_wx, lru_bx, lru_lambda, w_lru_out, w_attn_out, attn_sinks, rel_bias, w_out, norm2_g, w_ff1, w_ff2, final_g, loss_target, m_w_ada, m_b_ada, m_norm1_g, m_w_in, m_conv_w, m_conv_b, m_lru_wa, m_lru_ba, m_lru_wx, m_lru_bx, m_lru_lambda, m_w_lru_out, m_w_attn_out, m_attn_sinks, m_rel_bias, m_w_out, m_norm2_g, m_w_ff1, m_w_ff2, m_final_g, v_w_ada, v_b_ada, v_norm1_g, v_w_in, v_conv_w, v_conv_b, v_lru_wa, v_lru_ba, v_lru_wx, v_lru_bx, v_lru_lambda, v_w_lru_out, v_w_attn_out, v_attn_sinks, v_rel_bias, v_w_out, v_norm2_g, v_w_ff1, v_w_ff2, v_final_g):
    xi, yi, ci = _mesh_pos()
    chip = 2 * xi + yi
    dev = 2 * chip + ci
    z8 = jnp.zeros((8, D), F32)

    conv_rows = jnp.concatenate([conv_w[0], jnp.zeros((4, D - D // 4), F32)], axis=1)
    pack0 = jnp.concatenate([c, conv_rows, jnp.zeros((3, D), F32)], axis=0)
    g0 = all_gather_small("gather_cond", pack0).reshape(N_DEV, 8, D)
    c_all = g0[:, 0, :]
    conv_full = jnp.concatenate([g0[2 * k, 1:5, :D // 4] for k in range(N_CHIPS)], axis=1)
    c16 = jnp.concatenate([c_all, z8], axis=0)
    b_cols = lax.dynamic_slice_in_dim(b_ada, chip * ADA_SHARD, ADA_SHARD, axis=1)
    mod_c = mod_columns(c16, w_ada[0], b_cols)
    g1 = all_gather_small("gather_mod", mod_c).reshape(N_DEV, 16, ADA_SHARD)
    mod = jnp.concatenate([lax.dynamic_index_in_dim(g1[2 * k], dev, axis=0, keepdims=False) for k in range(N_CHIPS)])
    shift1, scale1, gate1, shift2, scale2, gate2 = [mod[i * D:(i + 1) * D] for i in range(6)]
    vecs = jnp.stack([norm1_g[0], scale1, shift1, gate1, norm2_g[0], scale2, shift2, gate2, final_g]
                     + [jnp.zeros((D,), F32)] * 7)
    lvec = jnp.concatenate([conv_full, conv_b, lru_ba, lru_bx, lru_lambda], axis=0)

    shards = [w_in[0], w_lru_out[0], w_attn_out[0], w_out[0], w_ff1[0], w_ff2[0]]
    names = ["w_in", "w_lru_out", "w_attn_out", "w_out", "w_ff1", "w_ff2"]
    k_idx = jnp.reshape(chip, (1,)).astype(jnp.int32)
    c_idx = jnp.reshape(ci, (1,)).astype(jnp.int32)
    near, far = (0, 1), (2,)
    shard_of = lambda flip: jnp.reshape(chip ^ flip, (1,)).astype(jnp.int32)
    x2d = x[0]
    n_send, n_recv, w_in_buf, _ = gather_start(
        "gather_start_in_near", [cast_into_slot("cast_w_in", shards[0], k_idx)], vecs, near)
    proj, h = inproj_fwd("inproj_fwd_own", x2d, None, vecs, w_in_buf[0], None, k_idx)
    slots = [cast_into_slot("cast_" + nm, w, k_idx) for nm, w in zip(names[1:], shards[1:])]
    w_in_buf = gather_forward("gather_forward_in_near", gather_wait(
        "gather_wait_in_near", n_send, n_recv, w_in_buf, proj[:8, :128] + slots[-1][0, :8, :128], near), near)
    f_send, f_recv, w_in_buf, _ = gather_start("gather_start_in_far", w_in_buf, proj[:8, :128], far)
    in_flight = {"mix": gather_start("gather_start_mix", slots[:3], proj[:8, :128])}
    in_flight["ff"] = gather_start("gather_start_ff", slots[3:], in_flight["mix"][3], carry=w_in_buf[0])
    w_in_buf = [in_flight["ff"][4]]
    proj = inproj_fwd("inproj_fwd_x", x2d, h, vecs, w_in_buf[0], proj, shard_of(2))
    proj = inproj_fwd("inproj_fwd_y", x2d, h, vecs, w_in_buf[0], proj, shard_of(1))
    w_in_buf = gather_forward("gather_forward_in_far", gather_wait(
        "gather_wait_in_far", f_send, f_recv, w_in_buf, proj[:8, :128], far), far)
    proj = inproj_fwd("inproj_fwd_d", x2d, h, vecs, w_in_buf[0], proj, shard_of(3))
    w_in_full = w_in_buf[0]
    pending = {}

    def rest_weights(group, after):
        send_sems, recv_sems, bufs = in_flight[group][:3]
        return gather_forward("gather_forward_" + group,
                              gather_wait("gather_wait_" + group, send_sems, recv_sems, bufs, after))

    def reduce_hook(event, payload):
        if event == "grads_a":
            pending["sib_a"] = exchange_start("sibling_start_a", "sibling", payload, payload[0])
            return pending["sib_a"][-1][0, 0]
        if event == "lru_grads":
            pack_w = jnp.concatenate([payload[0].reshape(D, 128), payload[1].reshape(D, 128)], axis=0).astype(BF16)
            pending["lru_w"] = exchange_start("lru_w_grads_start", "all", [pack_w], pack_w)
            return pending["lru_w"][-1][0, 0]
        if event == "grads_b":
            pending["sib_b"] = exchange_start("sibling_start_b", "sibling", payload, pending["lru_w"][-1])
            return pending["sib_b"][-1][0, 0]
        return chips_start("a", names[1:], payload)

    def chips_start(tag, nms, after):
        send_sems, recv_sems, grads, lands, _ = pending["sib_" + tag]
        grads, lands = exchange_wait("sibling_wait_" + tag, "sibling", send_sems, recv_sems, grads, lands, after)
        sums = [sibling_sum("sibling_sum_" + nm, g, o, c_idx) for nm, g, o in zip(nms, grads, lands)]
        pending[tag] = exchange_start("exchange_start_" + tag, "chips", sums, sums[0])
        return pending[tag][-1][0, 0]

    loss_t, grad_x, small = local_step(
        x2d, loss_target[0], vecs, lvec, lru_wa[0].astype(BF16), lru_wx[0].astype(BF16),
        attn_sinks[0], rel_bias, proj, h, w_in_full, rest_weights, reduce_hook)
    chips_start("b", names[:1], grad_x)

    big_m = dict(zip(names, [m_w_in, m_w_lru_out, m_w_attn_out, m_w_out, m_w_ff1, m_w_ff2]))
    big_v = dict(zip(names, [v_w_in, v_w_lru_out, v_w_attn_out, v_w_out, v_w_ff1, v_w_ff2]))
    local_w = dict(zip(names, shards))
    g_big, d_big, nm_big, nv_big = {}, {}, {}, {}

    def links_done(tag, after):
        send_sems, recv_sems, sums, lands, _ = pending[tag]
        return exchange_wait("exchange_wait_" + tag, "chips", send_sems, recv_sems, sums, lands, after)

    def finish_reduce(nms, sums, lands):
        mine = [chip_sum("chip_sum_" + nm, t, p, k_idx) for nm, t, p in zip(nms, sums, lands)]
        theirs = halves_exchange("halves_exchange_" + nms[0], mine)
        for nm, a, b in zip(nms, mine, theirs):
            g2, dl, m2, v2 = adamw_big("adamw_" + nm, local_w[nm], a, b, big_m[nm][0], big_v[nm][0], c_idx)
            g_big[nm], d_big[nm], nm_big[nm], nv_big[nm] = g2[None], dl[None], m2[None], v2[None]
        return lax.optimization_barrier(tuple(nv_big[nm] for nm in nms))[0]

    done_a = finish_reduce(names[1:], *links_done("a", pending["b"][-1]))
    sums_b, lands_b = links_done("b", done_a)
    w_send, w_recv, w_src, w_lands, _ = pending["lru_w"]
    w_src, w_lands = exchange_wait("lru_w_grads_wait", "all", w_send, w_recv, w_src, w_lands, lands_b[0])

    sums_f, sums_2, sums_1, sums_l = small["sums_f"], small["sums_2"], small["sums_1"], small["sums_l"]
    vec_rows = jnp.stack([sums_1[2], sums_2[2], sums_f[0], sums_l[L_CB], sums_l[L_BA], sums_l[L_BX],
                          sums_l[L_LAM], jnp.zeros((D,), F32)])
    mod_rows = jnp.stack([sums_1[0], sums_1[1], sums_2[3], sums_2[0], sums_2[1], sums_f[1],
                          jnp.zeros((D,), F32), jnp.zeros((D,), F32)])
    att_rows = jnp.concatenate([
        jnp.concatenate([small["d_sinks"], jnp.zeros((D - N_HEADS,), F32)])[None],
        jnp.concatenate([small["d_rel_bias"].reshape(-1), jnp.zeros((D - N_BUCKETS * N_HEADS,), F32)])[None],
        jnp.zeros((6, D), F32)], axis=0)
    pack = jnp.concatenate([vec_rows, _pad_rows(sums_l[0:4], 8), mod_rows, att_rows], axis=0)
    pack, lru_w_all = lax.optimization_barrier((pack, w_lands[0]))
    gathered = all_gather_small("gather_small_grads", pack).reshape(N_DEV, P_WA, D)
    total = sum_devices("sum_small_grads", gathered)
    total_w = sum_devices("sum_lru_w_grads", lru_w_all)
    dmod_all = gathered[:, P_MOD:P_MOD + 6, :].reshape(N_DEV, 6 * D)
    dmod16 = jnp.concatenate([lax.dynamic_slice_in_dim(dmod_all, chip * ADA_SHARD, ADA_SHARD, axis=1),
                              jnp.zeros((8, ADA_SHARD), F32)], axis=0)
    g_w_ada, d_w_ada, nm_w_ada, nv_w_ada = wada_update(c16, dmod16, w_ada[0], m_w_ada[0], v_w_ada[0])
    finish_reduce(names[:1], sums_b, lands_b)
    loss = lax.psum(lax.optimization_barrier((loss_t, total))[0][0, 0], ("x", "y", "c"))

    conv_g = lax.dynamic_slice_in_dim(total[P_CONVW:P_CONVW + 4], chip * (D // 4), D // 4, axis=1)
    sm_names = ["b_ada", "norm1_g", "conv_w", "conv_b", "lru_wa", "lru_ba", "lru_wx", "lru_bx", "lru_lambda",
                "attn_sinks", "rel_bias", "norm2_g", "final_g"]
    sm_w = [b_ada.reshape(6, D), norm1_g, conv_w[0], conv_b, lru_wa.reshape(D, 128), lru_ba, lru_wx.reshape(D, 128),
            lru_bx, lru_lambda, attn_sinks, rel_bias, norm2_g, final_g[None]]
    sm_m = [m_b_ada.reshape(6, D), m_norm1_g, m_conv_w[0], m_conv_b, m_lru_wa.reshape(D, 128), m_lru_ba,
            m_lru_wx.reshape(D, 128), m_lru_bx, m_lru_lambda, m_attn_sinks, m_rel_bias, m_norm2_g, m_final_g[None]]
    sm_v = [v_b_ada.reshape(6, D), v_norm1_g, v_conv_w[0], v_conv_b, v_lru_wa.reshape(D, 128), v_lru_ba,
            v_lru_wx.reshape(D, 128), v_lru_bx, v_lru_lambda, v_attn_sinks, v_rel_bias, v_norm2_g, v_final_g[None]]
    sm_g = [total[P_MOD:P_MOD + 6], total[0:1], conv_g, total[3:4], total_w[0:D], total[4:5],
            total_w[D:2 * D], total[5:6], total[6:7], total[P_ATT:P_ATT + 1, :N_HEADS],
            total[P_ATT + 1, :N_BUCKETS * N_HEADS].reshape(N_BUCKETS, N_HEADS), total[1:2], total[2:3]]
    sm_d, sm_nm, sm_nv = adamw_small(sm_w, sm_g, sm_m, sm_v)
    shapes = dict(b_ada=b_ada.shape, norm1_g=norm1_g.shape, conv_w=conv_w.shape, conv_b=conv_b.shape,
                  lru_wa=lru_wa.shape, lru_ba=lru_ba.shape, lru_wx=lru_wx.shape, lru_bx=lru_bx.shape,
                  lru_lambda=lru_lambda.shape, attn_sinks=attn_sinks.shape, rel_bias=rel_bias.shape,
                  norm2_g=norm2_g.shape, final_g=final_g.shape)
    grads = dict(w_ada=g_w_ada[None], **g_big)
    deltas = dict(w_ada=d_w_ada[None], **d_big)
    new_m = dict(w_ada=nm_w_ada[None], **nm_big)
    new_v = dict(w_ada=nv_w_ada[None], **nv_big)
    for i, nm in enumerate(sm_names):
        grads[nm] = sm_g[i].reshape(shapes[nm])
        deltas[nm] = sm_d[i].reshape(shapes[nm])
        new_m[nm] = sm_nm[i].reshape(shapes[nm])
        new_v[nm] = sm_nv[i].reshape(shapes[nm])
    order = ["w_ada", "b_ada", "norm1_g", "w_in", "conv_w", "conv_b", "lru_wa", "lru_ba", "lru_wx", "lru_bx",
             "lru_lambda", "w_lru_out", "w_attn_out", "attn_sinks", "rel_bias", "w_out", "norm2_g", "w_ff1", "w_ff2",
             "final_g"]
    return (loss, grad_x[None], *[grads[n] for n in order], *[deltas[n] for n in order],
            *[new_m[n] for n in order], *[new_v[n] for n in order])
```

```python
import math

import numpy as np
import jax
import jax.numpy as jnp
from jax import lax
from jax.experimental import pallas as pl
from jax.experimental.pallas import tpu as pltpu

F32 = jnp.float32
BF16 = jnp.bfloat16
SDS = jax.ShapeDtypeStruct
MESH = pl.DeviceIdType.MESH

D = 2048
D_FF = 4 * D
N_HEADS = 32
HEAD_DIM = 64
BLOCK = 128
N_LRU_BLOCKS = 16
LRU_C = 8.0
EPS = 1e-6
NEG_INF = -1e30
N_BUCKETS = 32
MAX_DISTANCE = 128
IN_W = 10752
IN_SHARD = IN_W // 4
IN_TILE = 896
ADA_SHARD = 6 * D // 4
OFF_LRU, OFF_GATE, OFF_Q, OFF_K, OFF_V, OFF_GA, OFF_GB = 0, 2048, 4096, 6144, 6400, 6656, 8704
SCALE = HEAD_DIM ** -0.5
N_CHIPS = 4
N_DEV = 8

ADAM_LR, ADAM_B1, ADAM_B2, ADAM_EPS, ADAM_WD, ADAM_STEP = 0.001, 0.9, 0.999, 1e-08, 0.01, 10
ADAM_C1 = 1.0 - ADAM_B1 ** ADAM_STEP
ADAM_C2 = 1.0 - ADAM_B2 ** ADAM_STEP

VMEM_LIMIT = 52 * 2 ** 20
SUB = 128
WG_TM = 1024
V_G1, V_SCALE1, V_SHIFT1, V_GATE1, V_G2, V_SCALE2, V_SHIFT2, V_GATE2, V_G3 = range(9)
L_CW0, L_CB, L_BA, L_BX, L_LAM = 0, 4, 5, 6, 7
P_VEC, P_CONVW, P_MOD, P_ATT, P_WA = 0, 8, 16, 24, 32


def _cp(n_axes):
    return pltpu.CompilerParams(dimension_semantics=("arbitrary",) * n_axes, vmem_limit_bytes=VMEM_LIMIT)


def _dot(a, b):
    return jnp.dot(a, b, preferred_element_type=F32)


def _dot_nt(a, b):
    return lax.dot_general(a, b, (((1,), (1,)), ((), ())), preferred_element_type=F32)


def _dot_tn(a, b):
    return lax.dot_general(a, b, (((0,), (0,)), ((), ())), preferred_element_type=F32)


_G0 = math.sqrt(2.0 / math.pi)
_G1 = 0.044715


def _gelu(x):
    return 0.5 * x * (1.0 + jnp.tanh(_G0 * (x + _G1 * x * x * x)))


def _gelu_grad(x):
    x2 = x * x
    t = jnp.tanh(_G0 * (x + _G1 * x * x2))
    return 0.5 * (1.0 + t) + 0.5 * x * (1.0 - t * t) * _G0 * (1.0 + 3.0 * _G1 * x2)


def _sigmoid(x):
    return 0.5 * jnp.tanh(0.5 * x) + 0.5


def _one_minus_exp2(x):
    t = jnp.tanh(x)
    return (-2.0 * t) / (1.0 - t)


def _softplus(z):
    e = jnp.exp(-jnp.abs(z))
    u = 1.0 + e
    l1p = jnp.where(u == 1.0, e, jnp.log(u) * e / (u - 1.0))
    return jnp.maximum(z, 0.0) + l1p


def _adamw_math(w, g, m, v):
    m2 = ADAM_B1 * m + (1.0 - ADAM_B1) * g
    v2 = ADAM_B2 * v + (1.0 - ADAM_B2) * (g * g)
    m_hat = m2 / ADAM_C1
    v_hat = v2 / ADAM_C2
    delta = -ADAM_LR * (m_hat / (jnp.sqrt(v_hat) + ADAM_EPS) + ADAM_WD * w)
    return delta, m2, v2


def _rms_parts(xv):
    r = lax.rsqrt(jnp.mean(xv * xv, axis=-1, keepdims=True) + EPS)
    return r, xv * r


def _row_fetches(hbm_refs, bufs, sems, i, rows):
    return [pltpu.make_async_copy(h.at[pl.ds(i * rows, rows), :], b, sems.at[n])
            for n, (h, b) in enumerate(zip(hbm_refs, bufs))]


def _modulated_norm(x_ref, v_ref, row_g, row_scale, row_shift, h_ref, rows):
    g, scale, shift = v_ref[row_g:row_g + 1, :], v_ref[row_scale:row_scale + 1, :], v_ref[row_shift:row_shift + 1, :]

    def sub(rb, carry):
        rs = pl.ds(pl.multiple_of(rb * SUB, SUB), SUB)
        _, xh = _rms_parts(x_ref[rs, :])
        h_ref[rs, :] = ((xh * g) * (1.0 + scale) + shift).astype(BF16)
        return carry

    lax.fori_loop(0, rows // SUB, sub, 0)


def inproj_fwd(name, x, h, vecs, w_in, proj, shard):
    s = x.shape[0]
    tm = min(1024, s)
    per = IN_SHARD // IN_TILE
    first = h is None

    def body(*refs):
        if first:
            _, x_ref, v_ref, w_ref, proj_ref, h_ref = refs

            @pl.when(pl.program_id(1) == 0)
            def _():
                _modulated_norm(x_ref, v_ref, V_G1, V_SCALE1, V_SHIFT1, h_ref, tm)
        else:
            _, h_ref, w_ref, _, proj_ref = refs
        proj_ref[...] = _dot(h_ref[...], w_ref[...]).astype(BF16)

    rows = pl.BlockSpec((tm, D), lambda i, j, sr: (i, 0))
    w_spec = pl.BlockSpec((None, D, IN_TILE), lambda i, j, sr: (sr[0], 0, j))
    proj_spec = pl.BlockSpec((tm, IN_TILE), lambda i, j, sr: (i, sr[0] * per + j))
    if first:
        gs = pltpu.PrefetchScalarGridSpec(
            num_scalar_prefetch=1, grid=(s // tm, per),
            in_specs=[rows, pl.BlockSpec((16, D), lambda i, j, sr: (0, 0)), w_spec], out_specs=[proj_spec, rows])
        return pl.pallas_call(body, name=name, grid_spec=gs, out_shape=[SDS((s, IN_W), BF16), SDS((s, D), BF16)],
                              compiler_params=_cp(2))(shard, x, vecs, w_in)
    gs = pltpu.PrefetchScalarGridSpec(
        num_scalar_prefetch=1, grid=(s // tm, per),
        in_specs=[rows, w_spec, pl.BlockSpec(memory_space=pl.ANY)], out_specs=proj_spec)
    return pl.pallas_call(body, name=name, grid_spec=gs, out_shape=SDS((s, IN_W), BF16),
                          input_output_aliases={3: 0}, compiler_params=_cp(2))(shard, h, w_in, proj)


def _lru_block_fwd(xbuf, lv_ref, wa_ref, wx_ref, b, t, first):
    cs = slice(b * 128, (b + 1) * 128)
    x0 = xbuf[pl.ds(8, t), cs]
    x1 = xbuf[pl.ds(7, t), cs]
    x2 = xbuf[pl.ds(6, t), cs]
    x3 = xbuf[pl.ds(5, t), cs]
    xc = (lv_ref[L_CB:L_CB + 1, cs] + lv_ref[3:4, cs] * x0 + lv_ref[2:3, cs] * x1
          + lv_ref[1:2, cs] * x2 + lv_ref[0:1, cs] * x3)
    xcb = xc.astype(BF16)
    r = _sigmoid(_dot(xcb, wa_ref[b]) + lv_ref[L_BA:L_BA + 1, cs])
    ig = _sigmoid(_dot(xcb, wx_ref[b]) + lv_ref[L_BX:L_BX + 1, cs])
    sp = _softplus(-lv_ref[L_LAM:L_LAM + 1, cs])
    log_a = (-LRU_C) * r * sp
    a = jnp.exp(log_a)
    mult = jnp.where(first, 1.0, jnp.sqrt(_one_minus_exp2(log_a)))
    return (x0, x1, x2, x3), xc, xcb, r, ig, sp, a, mult


def lru_fwd(proj, lvec, wa, wx):
    s = proj.shape[0]
    t = min(256, s)

    def body(lx_ref, gate_ref, lv_ref, wa_ref, wx_ref, ya_ref, rec_ref, yat_ref, xbuf, a_s, u_s, hc):
        i = pl.program_id(0)

        @pl.when(i == 0)
        def _():
            xbuf[pl.ds(0, 8), :] = jnp.zeros((8, D), F32)
            hc[...] = jnp.zeros((8, D), F32)

        @pl.when(i > 0)
        def _():
            xbuf[pl.ds(0, 8), :] = xbuf[pl.ds(t, 8), :]

        xbuf[pl.ds(8, t), :] = lx_ref[...].astype(F32)
        first = (lax.broadcasted_iota(jnp.int32, (t, 128), 0) + i * t) == 0
        for b in range(N_LRU_BLOCKS):
            cs = slice(b * 128, (b + 1) * 128)
            _, xc, _, _, ig, _, a, mult = _lru_block_fwd(xbuf, lv_ref, wa_ref, wx_ref, b, t, first)
            a_s[:, cs] = a
            u_s[:, cs] = mult * (ig * xc)

        def step(tt, h):
            h = a_s[pl.ds(tt, 1), :] * h + u_s[pl.ds(tt, 1), :]
            rec_ref[pl.ds(tt, 1), :] = h
            return h

        hc[0:1, :] = lax.fori_loop(0, t, step, hc[0:1, :], unroll=8)
        for b in range(N_LRU_BLOCKS):
            cs = slice(b * 128, (b + 1) * 128)
            yb = (rec_ref[:, cs] * _gelu(gate_ref[:, cs].astype(F32))).astype(BF16)
            ya_ref[:, cs] = yb
            yat_ref[cs, :] = yb.T

    return pl.pallas_call(
        body, name="lru_fwd", grid=(s // t,),
        in_specs=[pl.BlockSpec((t, D), lambda i: (i, OFF_LRU // D)),
                  pl.BlockSpec((t, D), lambda i: (i, OFF_GATE // D)),
                  pl.BlockSpec((8, D), lambda i: (0, 0)),
                  pl.BlockSpec((N_LRU_BLOCKS, 128, 128), lambda i: (0, 0, 0)),
                  pl.BlockSpec((N_LRU_BLOCKS, 128, 128), lambda i: (0, 0, 0))],
        out_specs=[pl.BlockSpec((t, D), lambda i: (i, 0)), pl.BlockSpec((t, D), lambda i: (i, 0)),
                   pl.BlockSpec((D, t), lambda i: (0, i))],
        out_shape=[SDS((s, D), BF16), SDS((s, D), F32), SDS((D, s), BF16)],
        scratch_shapes=[pltpu.VMEM((t + 8, D), F32), pltpu.VMEM((t, D), F32), pltpu.VMEM((t, D), F32),
                        pltpu.VMEM((8, D), F32)],
        compiler_params=_cp(1))(proj, proj, lvec, wa, wx)


def t5_bucket_table():
    qi = np.arange(BLOCK)[:, None]
    ki = np.arange(2 * BLOCK)[None, :]
    rel = qi + BLOCK - ki
    relc = np.maximum(rel, 0)
    max_exact = N_BUCKETS // 2
    relf = np.maximum(relc, 1).astype(np.float32)
    large = max_exact + (np.log(relf / np.float32(max_exact)) / np.float32(math.log(MAX_DISTANCE / max_exact))
                         * np.float32(N_BUCKETS - max_exact)).astype(np.int32)
    large = np.minimum(large, N_BUCKETS - 1)
    bucket = np.where(relc < max_exact, relc, large)
    bucket = np.where((rel >= 0) & (rel < BLOCK), bucket, -1)
    return jnp.asarray(bucket.reshape(1, BLOCK * 2 * BLOCK), jnp.int32)


def bias_band(rel_bias_t, buckets):
    n = BLOCK * 2 * BLOCK
    tn = 4096

    def body(bk_ref, rb_ref, o_ref):
        row = lax.broadcasted_iota(jnp.int32, (N_BUCKETS, tn), 0)
        oh = jnp.where(row == bk_ref[...], 1.0, 0.0).astype(BF16)
        rb = rb_ref[...]
        p0 = rb.astype(BF16)
        r1 = rb - p0.astype(F32)
        p1 = r1.astype(BF16)
        p2 = (r1 - p1.astype(F32)).astype(BF16)
        o_ref[...] = _dot(p0, oh) + _dot(p1, oh) + _dot(p2, oh)

    return pl.pallas_call(
        body, name="bias_band", grid=(n // tn,),
        in_specs=[pl.BlockSpec((1, tn), lambda i: (0, i)), pl.BlockSpec((N_HEADS, N_BUCKETS), lambda i: (0, 0))],
        out_specs=pl.BlockSpec((N_HEADS, tn), lambda i: (0, i)),
        out_shape=SDS((N_HEADS, n), F32), compiler_params=_cp(1))(buckets, rel_bias_t)


def bias_band_bwd(dband, buckets):
    n = BLOCK * 2 * BLOCK
    tn = 4096

    def body(bk_ref, d_ref, o_ref):
        @pl.when(pl.program_id(0) == 0)
        def _():
            o_ref[...] = jnp.zeros_like(o_ref)
        row = lax.broadcasted_iota(jnp.int32, (N_BUCKETS, tn), 0)
        oh = jnp.where(row == bk_ref[...], 1.0, 0.0).astype(BF16)
        dv = d_ref[...]
        p0 = dv.astype(BF16)
        r1 = dv - p0.astype(F32)
        p1 = r1.astype(BF16)
        p2 = (r1 - p1.astype(F32)).astype(BF16)
        o_ref[...] += _dot_nt(oh, p0) + _dot_nt(oh, p1) + _dot_nt(oh, p2)

    return pl.pallas_call(
        body, name="bias_band_bwd", grid=(n // tn,),
        in_specs=[pl.BlockSpec((1, tn), lambda i: (0, i)), pl.BlockSpec((N_HEADS, tn), lambda i: (0, i))],
        out_specs=pl.BlockSpec((N_BUCKETS, N_HEADS), lambda i: (0, 0)),
        out_shape=SDS((N_BUCKETS, N_HEADS), F32), compiler_params=_cp(1))(buckets, dband)


def _dup_half(band, which):
    lane = lax.broadcasted_iota(jnp.int32, band.shape, 1)
    rolled = pltpu.roll(band, 64, 1)
    keep = (lane < 64) if which == 0 else (lane >= 64)
    return jnp.where(keep, band, rolled)


def _attn_probs(scores, bias, sink, valid):
    sc = jnp.where(valid, scores * SCALE + bias, NEG_INF)
    m = jnp.maximum(jnp.max(sc, axis=-1, keepdims=True), sink)
    e = jnp.exp(sc - m)
    es = jnp.exp(sink - m)
    inv = 1.0 / (jnp.sum(e, axis=-1, keepdims=True) + es)
    return e * inv, es * inv


def _stack_heads(src_ref, kv, dst):
    lane = lax.broadcasted_iota(jnp.int32, (BLOCK, 128), 1)
    for jj in range(4):
        slab = src_ref[:, (4 * kv + jj) * 128:(4 * kv + jj + 1) * 128]
        for hh in range(2):
            keep = (lane < 64) if hh == 0 else (lane >= 64)
            dst[pl.ds((2 * jj + hh) * BLOCK, BLOCK), :] = jnp.where(keep, slab, jnp.zeros_like(slab))


def _unstack_heads(stacked, dst_ref, kv, dst_t_ref=None):
    lane = lax.broadcasted_iota(jnp.int32, (BLOCK, 128), 1)
    for jj in range(4):
        lo = stacked[(2 * jj) * BLOCK:(2 * jj + 1) * BLOCK]
        hi = stacked[(2 * jj + 1) * BLOCK:(2 * jj + 2) * BLOCK]
        slab = jnp.where(lane < 64, lo, hi).astype(dst_ref.dtype)
        dst_ref[:, (4 * kv + jj) * 128:(4 * kv + jj + 1) * 128] = slab
        if dst_t_ref is not None:
            dst_t_ref[(4 * kv + jj) * 128:(4 * kv + jj + 1) * 128, :] = slab.T


def _band_valid(n):
    qi = lax.broadcasted_iota(jnp.int32, (BLOCK, 2 * BLOCK), 0)
    ki = lax.broadcasted_iota(jnp.int32, (BLOCK, 2 * BLOCK), 1)
    rel = qi + BLOCK - ki
    return (rel >= 0) & (rel < BLOCK) & ((ki >= BLOCK) | (n > 0))


def _kv_bands(prev_ref, cur_ref):
    band = jnp.concatenate([prev_ref[...].astype(F32), cur_ref[...].astype(F32)], axis=0)
    return [_dup_half(band, 0).astype(BF16), _dup_half(band, 1).astype(BF16)]


def attn_fwd(proj, band, sinks):
    s = proj.shape[0]
    nb = s // BLOCK
    qw = 1024

    def body(sk_ref, q_ref, kp_ref, kc_ref, vp_ref, vc_ref, b_ref, o_ref, ot_ref, qs_buf, s_buf, p_buf):
        n = pl.program_id(0)
        gp = pl.program_id(1)
        valid = _band_valid(n)
        kks = _kv_bands(kp_ref, kc_ref)
        vvs = _kv_bands(vp_ref, vc_ref)
        for kv in range(2):
            _stack_heads(q_ref, kv, qs_buf)
            s_buf[...] = _dot_nt(qs_buf[...], kks[kv])
            for hq in range(8):
                hl = 8 * kv + hq
                rows = pl.ds(hq * BLOCK, BLOCK)
                p, _ = _attn_probs(s_buf[rows, :], b_ref[hl], sk_ref[gp * 16 + hl], valid)
                p_buf[rows, :] = p.astype(BF16)
            _unstack_heads(_dot(p_buf[...], vvs[kv]), o_ref, kv, ot_ref)

    kb, vb = OFF_K // 128, OFF_V // 128
    return pl.pallas_call(
        body, name="attn_fwd", grid=(nb, 2),
        in_specs=[pl.BlockSpec(memory_space=pltpu.SMEM),
                  pl.BlockSpec((BLOCK, qw), lambda n, g: (n, OFF_Q // qw + g)),
                  pl.BlockSpec((BLOCK, 128), lambda n, g: (jnp.maximum(n - 1, 0), kb + g)),
                  pl.BlockSpec((BLOCK, 128), lambda n, g: (n, kb + g)),
                  pl.BlockSpec((BLOCK, 128), lambda n, g: (jnp.maximum(n - 1, 0), vb + g)),
                  pl.BlockSpec((BLOCK, 128), lambda n, g: (n, vb + g)),
                  pl.BlockSpec((16, BLOCK, 2 * BLOCK), lambda n, g: (g, 0, 0))],
        out_specs=[pl.BlockSpec((BLOCK, qw), lambda n, g: (n, g)), pl.BlockSpec((qw, BLOCK), lambda n, g: (g, n))],
        out_shape=[SDS((s, D), BF16), SDS((D, s), BF16)],
        scratch_shapes=[pltpu.VMEM((8 * BLOCK, 128), BF16), pltpu.VMEM((8 * BLOCK, 2 * BLOCK), F32),
                        pltpu.VMEM((8 * BLOCK, 2 * BLOCK), BF16)],
        compiler_params=_cp(2))(sinks, proj, proj, proj, proj, proj, band)


def merge_fwd(ya, att, w_lru_out, w_attn_out, proj):
    s = ya.shape[0]
    tm, tn = min(1024, s), 512

    def body(ya_ref, at_ref, wl_ref, wt_ref, ga_ref, gb_ref, yab_ref, mg_ref, mgt_ref):
        y_a = _dot(ya_ref[...], wl_ref[...])
        y_b = _dot(at_ref[...], wt_ref[...])
        yab_ref[0] = y_a.astype(BF16)
        yab_ref[1] = y_b.astype(BF16)
        mg = (_sigmoid(ga_ref[...].astype(F32)) * y_a + _sigmoid(gb_ref[...].astype(F32)) * y_b).astype(BF16)
        mg_ref[...] = mg
        mgt_ref[...] = mg.T

    return pl.pallas_call(
        body, name="merge_fwd", grid=(s // tm, D // tn),
        in_specs=[pl.BlockSpec((tm, D), lambda i, j: (i, 0)), pl.BlockSpec((tm, D), lambda i, j: (i, 0)),
                  pl.BlockSpec((D, tn), lambda i, j: (0, j)), pl.BlockSpec((D, tn), lambda i, j: (0, j)),
                  pl.BlockSpec((tm, tn), lambda i, j: (i, OFF_GA // tn + j)),
                  pl.BlockSpec((tm, tn), lambda i, j: (i, OFF_GB // tn + j))],
        out_specs=[pl.BlockSpec((2, tm, tn), lambda i, j: (0, i, j)), pl.BlockSpec((tm, tn), lambda i, j: (i, j)),
                   pl.BlockSpec((tn, tm), lambda i, j: (j, i))],
        out_shape=[SDS((2, s, D), BF16), SDS((s, D), BF16), SDS((D, s), BF16)],
        compiler_params=_cp(2))(ya, att, w_lru_out, w_attn_out, proj, proj)


def outproj_fwd(merged, w_out, x, vecs):
    s = x.shape[0]
    tm, tn = min(1024, s), 512

    def body(m_ref, w_ref, x_ref, v_ref, x1_ref, o1_ref):
        o1 = _dot(m_ref[...], w_ref[...])
        o1_ref[...] = o1.astype(BF16)
        x1_ref[...] = x_ref[...] + v_ref[V_GATE1:V_GATE1 + 1, :] * o1

    return pl.pallas_call(
        body, name="outproj_fwd", grid=(s // tm, D // tn),
        in_specs=[pl.BlockSpec((tm, D), lambda i, j: (i, 0)), pl.BlockSpec((D, tn), lambda i, j: (0, j)),
                  pl.BlockSpec((tm, tn), lambda i, j: (i, j)), pl.BlockSpec((16, tn), lambda i, j: (0, j))],
        out_specs=[pl.BlockSpec((tm, tn), lambda i, j: (i, j)), pl.BlockSpec((tm, tn), lambda i, j: (i, j))],
        out_shape=[SDS((s, D), F32), SDS((s, D), BF16)],
        compiler_params=_cp(2))(merged, w_out, x, vecs)


def ff1_fwd(x1, vecs, w_ff1):
    s = x1.shape[0]
    tm, tn = min(1024, s), 1024
    per = D // tn

    def body(x_ref, v_ref, w_ref, f_ref, h_ref, fft_ref):
        @pl.when(pl.program_id(1) == 0)
        def _():
            _modulated_norm(x_ref, v_ref, V_G2, V_SCALE2, V_SHIFT2, h_ref, tm)
        fv = _dot(h_ref[...], w_ref[...])
        f_ref[...] = fv.astype(BF16)
        fp = jnp.maximum(fv, 0.0)
        fft_ref[...] = (fp * fp).astype(BF16).T

    return pl.pallas_call(
        body, name="ff1_fwd", grid=(s // tm, D_FF // tn),
        in_specs=[pl.BlockSpec((tm, D), lambda i, j: (i, 0)), pl.BlockSpec((16, D), lambda i, j: (0, 0)),
                  pl.BlockSpec((None, D, tn), lambda i, j: (j // per, 0, j % per))],
        out_specs=[pl.BlockSpec((tm, tn), lambda i, j: (i, j)), pl.BlockSpec((tm, D), lambda i, j: (i, 0)),
                   pl.BlockSpec((tn, tm), lambda i, j: (j, i))],
        out_shape=[SDS((s, D_FF), BF16), SDS((s, D), BF16), SDS((D_FF, s), BF16)],
        compiler_params=_cp(2))(x1, vecs, w_ff1)


def matmul_bf16(name, a, b, tm, tn, b_part=0):
    m, k = a.shape
    n = b.shape[1]

    def body(a_ref, b_ref, o_ref):
        o_ref[...] = _dot(a_ref[...], b_ref[...]).astype(BF16)

    return pl.pallas_call(
        body, name=name, grid=(m // tm, n // tn),
        in_specs=[pl.BlockSpec((tm, k), lambda i, j: (i, 0)), pl.BlockSpec((k, tn), lambda i, j: (b_part, j))],
        out_specs=pl.BlockSpec((tm, tn), lambda i, j: (i, j)),
        out_shape=SDS((m, n), BF16), compiler_params=_cp(2))(a, b)


def ff2_loss(f, w_ff2, x1, tgt, vecs):
    s = x1.shape[0]
    tm, tk = min(512, s), 2048
    nk = D_FF // tk

    def body(f_ref, w_ref, x1_hbm, t_hbm, v_ref, dx2_ref, do2_ref, sums_ref, loss_ref, acc, x1_ref, t_ref, sems):
        i, k = pl.program_id(0), pl.program_id(1)
        fetches = _row_fetches((x1_hbm, t_hbm), (x1_ref, t_ref), sems, i, tm)

        @pl.when((i == 0) & (k == 0))
        def _():
            sums_ref[...] = jnp.zeros_like(sums_ref)
            loss_ref[...] = jnp.zeros_like(loss_ref)

        @pl.when(k == 0)
        def _():
            acc[...] = jnp.zeros_like(acc)
            for cp in fetches:
                cp.start()

        fv = jnp.maximum(f_ref[...].astype(F32), 0.0)
        acc[...] += _dot((fv * fv).astype(BF16), w_ref[...])

        @pl.when(k == nk - 1)
        def _():
            for cp in fetches:
                cp.wait()
            gate2 = v_ref[V_GATE2:V_GATE2 + 1, :]
            g3 = v_ref[V_G3:V_G3 + 1, :]

            def sub(rb, carry):
                rs = pl.ds(pl.multiple_of(rb * SUB, SUB), SUB)
                o2 = acc[rs, :]
                x2 = x1_ref[rs, :] + gate2 * o2
                r3, xh = _rms_parts(x2)
                e = xh * g3 - t_ref[rs, :]
                loss_ref[...] += (0.5 / D) * jnp.sum(e * e)
                dy = e * (1.0 / D)
                sums_ref[0:1, :] += jnp.sum(dy * xh, axis=0, keepdims=True)
                dxh = dy * g3
                dx2 = r3 * (dxh - xh * jnp.mean(dxh * xh, axis=-1, keepdims=True))
                sums_ref[1:2, :] += jnp.sum(dx2 * o2, axis=0, keepdims=True)
                dx2_ref[rs, :] = dx2
                do2_ref[rs, :] = (dx2 * gate2).astype(BF16)
                return carry

            lax.fori_loop(0, tm // SUB, sub, 0)

    return pl.pallas_call(
        body, name="ff2_loss", grid=(s // tm, nk),
        in_specs=[pl.BlockSpec((tm, tk), lambda i, k: (i, k)), pl.BlockSpec((tk, D), lambda i, k: (k, 0)),
                  pl.BlockSpec(memory_space=pl.ANY), pl.BlockSpec(memory_space=pl.ANY),
                  pl.BlockSpec((16, D), lambda i, k: (0, 0))],
        out_specs=[pl.BlockSpec((tm, D), lambda i, k: (i, 0)), pl.BlockSpec((tm, D), lambda i, k: (i, 0)),
                   pl.BlockSpec((8, D), lambda i, k: (0, 0)), pl.BlockSpec((8, 128), lambda i, k: (0, 0))],
        out_shape=[SDS((s, D), F32), SDS((s, D), BF16), SDS((8, D), F32), SDS((8, 128), F32)],
        scratch_shapes=[pltpu.VMEM((tm, D), F32), pltpu.VMEM((tm, D), F32), pltpu.VMEM((tm, D), F32),
                        pltpu.SemaphoreType.DMA((2,))],
        compiler_params=_cp(2))(f, w_ff2, x1, tgt, vecs)


def ff2_bwd(do2, w_ff2, f):
    s = do2.shape[0]
    tm, tn = min(1024, s), 1024

    def body(d_ref, w_ref, f_ref, o_ref):
        dff = _dot_nt(d_ref[...], w_ref[...])
        o_ref[...] = (dff * (2.0 * jnp.maximum(f_ref[...].astype(F32), 0.0))).astype(BF16)

    return pl.pallas_call(
        body, name="ff2_bwd", grid=(s // tm, D_FF // tn),
        in_specs=[pl.BlockSpec((tm, D), lambda i, j: (i, 0)), pl.BlockSpec((tn, D), lambda i, j: (j, 0)),
                  pl.BlockSpec((tm, tn), lambda i, j: (i, j))],
        out_specs=pl.BlockSpec((tm, tn), lambda i, j: (i, j)),
        out_shape=SDS((s, D_FF), BF16), compiler_params=_cp(2))(do2, w_ff2, f)


def weight_grad(name, a, b, tn, out_shape, out_block, out_map):
    s, m = a.shape
    n = b.shape[1]
    tm = WG_TM
    chunk = min(1024, s)
    nch = s // chunk

    def body(a_hbm, b_ref, o_ref, a_buf, at_s, sem):
        i = pl.program_id(0)

        @pl.when(pl.program_id(1) == 0)
        def _():
            def fetch(ch):
                return pltpu.make_async_copy(a_hbm.at[pl.ds(ch * chunk, chunk), pl.ds(i * tm, tm)],
                                             a_buf.at[ch % 2], sem.at[ch % 2])
            fetch(0).start()
            for ch in range(nch):
                if ch + 1 < nch:
                    fetch(ch + 1).start()
                fetch(ch).wait()
                at_s[:, ch * chunk:(ch + 1) * chunk] = a_buf[ch % 2].T

        o_ref[...] = _dot(at_s[...], b_ref[...]).astype(BF16)

    return pl.pallas_call(
        body, name=name, grid=(m // tm, n // tn),
        in_specs=[pl.BlockSpec(memory_space=pl.ANY), pl.BlockSpec((s, tn), lambda i, j: (0, j))],
        out_specs=pl.BlockSpec(out_block, lambda i, j: out_map(i, j)),
        out_shape=SDS(out_shape, BF16),
        scratch_shapes=[pltpu.VMEM((2, chunk, tm), BF16), pltpu.VMEM((tm, s), BF16), pltpu.SemaphoreType.DMA((2,))],
        compiler_params=_cp(2))(a, b)


def ff1_bwd(df, w_ff1, x1, dx2, o1, vecs):
    s = df.shape[0]
    tm, tk = min(512, s), 1024
    nk = D_FF // tk
    per = D // tk

    def body(d_ref, w_ref, x1_hbm, dx2_hbm, o1_hbm, v_ref, dx1_ref, do1_ref, sums_ref, acc, x1_ref, dx2_ref, o1_ref, sems):
        i, k = pl.program_id(0), pl.program_id(1)
        fetches = _row_fetches((x1_hbm, dx2_hbm, o1_hbm), (x1_ref, dx2_ref, o1_ref), sems, i, tm)

        @pl.when((i == 0) & (k == 0))
        def _():
            sums_ref[...] = jnp.zeros_like(sums_ref)

        @pl.when(k == 0)
        def _():
            acc[...] = jnp.zeros_like(acc)
            for cp in fetches:
                cp.start()

        acc[...] += _dot_nt(d_ref[...], w_ref[...])

        @pl.when(k == nk - 1)
        def _():
            for cp in fetches:
                cp.wait()
            g2 = v_ref[V_G2:V_G2 + 1, :]
            scale2 = v_ref[V_SCALE2:V_SCALE2 + 1, :]
            gate1 = v_ref[V_GATE1:V_GATE1 + 1, :]

            def sub(rb, carry):
                rs = pl.ds(pl.multiple_of(rb * SUB, SUB), SUB)
                dh = acc[rs, :]
                r2, xh = _rms_parts(x1_ref[rs, :])
                sums_ref[0:1, :] += jnp.sum(dh, axis=0, keepdims=True)
                sums_ref[1:2, :] += jnp.sum(dh * (xh * g2), axis=0, keepdims=True)
                dxn = dh * (1.0 + scale2)
                sums_ref[2:3, :] += jnp.sum(dxn * xh, axis=0, keepdims=True)
                dxh = dxn * g2
                dx1 = dx2_ref[rs, :] + r2 * (dxh - xh * jnp.mean(dxh * xh, axis=-1, keepdims=True))
                sums_ref[3:4, :] += jnp.sum(dx1 * o1_ref[rs, :].astype(F32), axis=0, keepdims=True)
                dx1_ref[rs, :] = dx1
                do1_ref[rs, :] = (dx1 * gate1).astype(BF16)
                return carry

            lax.fori_loop(0, tm // SUB, sub, 0)

    return pl.pallas_call(
        body, name="ff1_bwd", grid=(s // tm, nk),
        in_specs=[pl.BlockSpec((tm, tk), lambda i, k: (i, k)),
                  pl.BlockSpec((None, D, tk), lambda i, k: (k // per, 0, k % per)),
                  pl.BlockSpec(memory_space=pl.ANY), pl.BlockSpec(memory_space=pl.ANY),
                  pl.BlockSpec(memory_space=pl.ANY), pl.BlockSpec((16, D), lambda i, k: (0, 0))],
        out_specs=[pl.BlockSpec((tm, D), lambda i, k: (i, 0)), pl.BlockSpec((tm, D), lambda i, k: (i, 0)),
                   pl.BlockSpec((8, D), lambda i, k: (0, 0))],
        out_shape=[SDS((s, D), F32), SDS((s, D), BF16), SDS((8, D), F32)],
        scratch_shapes=[pltpu.VMEM((tm, D), F32), pltpu.VMEM((tm, D), F32), pltpu.VMEM((tm, D), F32),
                        pltpu.VMEM((tm, D), BF16), pltpu.SemaphoreType.DMA((3,))],
        compiler_params=_cp(2))(df, w_ff1, x1, dx2, o1, vecs)


def outproj_bwd(do1, w_out, yab, proj):
    s = do1.shape[0]
    tm, tn = min(1024, s), 512
    per = D // tn

    def body(d_ref, w_ref, y_ref, g_ref, dy_ref, dp_ref):
        dm = _dot_nt(d_ref[...], w_ref[...])
        sg = _sigmoid(g_ref[...].astype(F32))
        dy_ref[...] = (dm * sg).astype(BF16)
        dp_ref[...] = (dm * y_ref[...].astype(F32) * sg * (1.0 - sg)).astype(BF16)

    return pl.pallas_call(
        body, name="outproj_bwd", grid=(s // tm, 2 * per),
        in_specs=[pl.BlockSpec((tm, D), lambda i, j: (i, 0)), pl.BlockSpec((tn, D), lambda i, j: (j % per, 0)),
                  pl.BlockSpec((None, tm, tn), lambda i, j: (j // per, i, j % per)),
                  pl.BlockSpec((tm, tn), lambda i, j: (i, OFF_GA // tn + j))],
        out_specs=[pl.BlockSpec((None, tm, tn), lambda i, j: (j // per, i, j % per)),
                   pl.BlockSpec((tm, tn), lambda i, j: (i, OFF_GA // tn + j))],
        out_shape=[SDS((2, s, D), BF16), SDS((s, IN_W), BF16)],
        compiler_params=_cp(2))(do1, w_out, yab, proj)


def lruout_bwd(dyab, w_lru_out, rec, proj, dproj):
    s = rec.shape[0]
    tm, tn = min(1024, s), 512

    def body(d_ref, w_ref, r_ref, g_ref, dp_in, dr_ref, dp_ref):
        dya = _dot_nt(d_ref[...], w_ref[...])
        gate = g_ref[...].astype(F32)
        dr_ref[...] = dya * _gelu(gate)
        dp_ref[...] = (dya * r_ref[...] * _gelu_grad(gate)).astype(BF16)

    return pl.pallas_call(
        body, name="lruout_bwd", grid=(s // tm, D // tn),
        in_specs=[pl.BlockSpec((None, tm, D), lambda i, j: (0, i, 0)), pl.BlockSpec((tn, D), lambda i, j: (j, 0)),
                  pl.BlockSpec((tm, tn), lambda i, j: (i, j)),
                  pl.BlockSpec((tm, tn), lambda i, j: (i, OFF_GATE // tn + j)),
                  pl.BlockSpec(memory_space=pl.ANY)],
        out_specs=[pl.BlockSpec((tm, tn), lambda i, j: (i, j)),
                   pl.BlockSpec((tm, tn), lambda i, j: (i, OFF_GATE // tn + j))],
        out_shape=[SDS((s, D), F32), SDS((s, IN_W), BF16)],
        input_output_aliases={4: 1},
        compiler_params=_cp(2))(dyab, w_lru_out, rec, proj, dproj)


def attnout_bwd(dyab, w_attn_out):
    s = dyab.shape[1]
    tm, tn = min(1024, s), 512

    def body(d_ref, w_ref, o_ref):
        o_ref[...] = _dot_nt(d_ref[...], w_ref[...]).astype(BF16)

    return pl.pallas_call(
        body, name="attnout_bwd", grid=(s // tm, D // tn),
        in_specs=[pl.BlockSpec((None, tm, D), lambda i, j: (1, i, 0)), pl.BlockSpec((tn, D), lambda i, j: (j, 0))],
        out_specs=pl.BlockSpec((tm, tn), lambda i, j: (i, j)),
        out_shape=SDS((s, D), BF16), compiler_params=_cp(2))(dyab, w_attn_out)


def attn_bwd(proj, band, sinks, datt, dproj):
    s = proj.shape[0]
    nb = s // BLOCK
    qw = 1024

    def body(sk_ref, q_ref, kp_ref, kc_ref, vp_ref, vc_ref, b_ref, do_ref, dp_in,
             dq_ref, dkb_ref, dvb_ref, db_ref, ds_ref, qs_buf, dos_buf, s_buf, dp_buf, p_buf, dsc_buf):
        gp = pl.program_id(0)
        n = pl.program_id(1)

        @pl.when(n == 0)
        def _():
            db_ref[...] = jnp.zeros_like(db_ref)
            ds_ref[...] = jnp.zeros_like(ds_ref)

        valid = _band_valid(n)
        kks = _kv_bands(kp_ref, kc_ref)
        vvs = _kv_bands(vp_ref, vc_ref)
        lane_b = lax.broadcasted_iota(jnp.int32, (2 * BLOCK, 128), 1)
        dks, dvs = [], []
        for kv in range(2):
            _stack_heads(q_ref, kv, qs_buf)
            _stack_heads(do_ref, kv, dos_buf)
            s_buf[...] = _dot_nt(qs_buf[...], kks[kv])
            dp_buf[...] = _dot_nt(dos_buf[...], vvs[kv])
            for hq in range(8):
                hl = 8 * kv + hq
                rows = pl.ds(hq * BLOCK, BLOCK)
                p, ps = _attn_probs(s_buf[rows, :], b_ref[hl], sk_ref[gp * 16 + hl], valid)
                dp = dp_buf[rows, :]
                delta = jnp.sum(p * dp, axis=-1, keepdims=True)
                dsc = p * (dp - delta)
                db_ref[hl] += dsc
                ds_ref[hl:hl + 1, :] += jnp.zeros((1, 128), F32) - jnp.sum(ps * delta)
                p_buf[rows, :] = p.astype(BF16)
                dsc_buf[rows, :] = (dsc * SCALE).astype(BF16)
            _unstack_heads(_dot(dsc_buf[...], kks[kv]), dq_ref, kv)
            dk = _dot_tn(dsc_buf[...], qs_buf[...])
            dv = _dot_tn(p_buf[...], dos_buf[...])
            dks.append(dk + pltpu.roll(dk, 64, 1))
            dvs.append(dv + pltpu.roll(dv, 64, 1))
        dkb_ref[...] = jnp.where(lane_b < 64, dks[0], dks[1])
        dvb_ref[...] = jnp.where(lane_b < 64, dvs[0], dvs[1])

    kb, vb = OFF_K // 128, OFF_V // 128
    return pl.pallas_call(
        body, name="attn_bwd", grid=(2, nb),
        in_specs=[pl.BlockSpec(memory_space=pltpu.SMEM),
                  pl.BlockSpec((BLOCK, qw), lambda g, n: (n, OFF_Q // qw + g)),
                  pl.BlockSpec((BLOCK, 128), lambda g, n: (jnp.maximum(n - 1, 0), kb + g)),
                  pl.BlockSpec((BLOCK, 128), lambda g, n: (n, kb + g)),
                  pl.BlockSpec((BLOCK, 128), lambda g, n: (jnp.maximum(n - 1, 0), vb + g)),
                  pl.BlockSpec((BLOCK, 128), lambda g, n: (n, vb + g)),
                  pl.BlockSpec((16, BLOCK, 2 * BLOCK), lambda g, n: (g, 0, 0)),
                  pl.BlockSpec((BLOCK, qw), lambda g, n: (n, g)),
                  pl.BlockSpec(memory_space=pl.ANY)],
        out_specs=[pl.BlockSpec((BLOCK, qw), lambda g, n: (n, OFF_Q // qw + g)),
                   pl.BlockSpec((2 * BLOCK, 128), lambda g, n: (n, g)),
                   pl.BlockSpec((2 * BLOCK, 128), lambda g, n: (n, g)),
                   pl.BlockSpec((16, BLOCK, 2 * BLOCK), lambda g, n: (g, 0, 0)),
                   pl.BlockSpec((16, 128), lambda g, n: (g, 0))],
        out_shape=[SDS((s, IN_W), BF16), SDS((nb * 2 * BLOCK, 256), F32), SDS((nb * 2 * BLOCK, 256), F32),
                   SDS((N_HEADS, BLOCK, 2 * BLOCK), F32), SDS((N_HEADS, 128), F32)],
        input_output_aliases={8: 0},
        scratch_shapes=[pltpu.VMEM((8 * BLOCK, 128), BF16), pltpu.VMEM((8 * BLOCK, 128), BF16),
                        pltpu.VMEM((8 * BLOCK, 2 * BLOCK), F32), pltpu.VMEM((8 * BLOCK, 2 * BLOCK), F32),
                        pltpu.VMEM((8 * BLOCK, 2 * BLOCK), BF16), pltpu.VMEM((8 * BLOCK, 2 * BLOCK), BF16)],
        compiler_params=_cp(2))(sinks, proj, proj, proj, proj, proj, band, datt, dproj)


def dkv_combine(dkb, dvb, dproj):
    nb = dkb.shape[0] // (2 * BLOCK)
    s = nb * BLOCK
    dkb3 = dkb.reshape(nb, 2 * BLOCK, 256)
    dvb3 = dvb.reshape(nb, 2 * BLOCK, 256)

    def body(k1, k2, v1, v2, dp_in, o_ref):
        nxt = jnp.where(pl.program_id(0) < nb - 1, 1.0, 0.0)
        o_ref[:, 0:256] = (k1[...] + nxt * k2[...]).astype(BF16)
        o_ref[:, 256:512] = (v1[...] + nxt * v2[...]).astype(BF16)

    spec1 = pl.BlockSpec((None, BLOCK, 256), lambda m: (m, 1, 0))
    spec2 = pl.BlockSpec((None, BLOCK, 256), lambda m: (jnp.minimum(m + 1, nb - 1), 0, 0))
    return pl.pallas_call(
        body, name="dkv_combine", grid=(nb,),
        in_specs=[spec1, spec2, spec1, spec2, pl.BlockSpec(memory_space=pl.ANY)],
        out_specs=pl.BlockSpec((BLOCK, 512), lambda m: (m, OFF_K // 512)),
        out_shape=SDS((s, IN_W), BF16), input_output_aliases={4: 0},
        compiler_params=_cp(1))(dkb3, dkb3, dvb3, dvb3, dproj)


def lru_bwd(proj, rec, drec, lvec, wa, wx, dproj):
    s = proj.shape[0]
    t = min(256, s)
    nt = s // t

    def body(lx_ref, lxh_ref, rec_ref, rech_ref, dr_ref, lv_ref, wa_ref, wx_ref, dp_in,
             dlx_ref, sums_ref, dwa_ref, dwx_ref,
             xbuf, hbuf, dxbuf, a_s, dh_s, xc_s, r_s, ig_s, mu_s, gc):
        step_i = pl.program_id(0)
        ti = nt - 1 - step_i

        @pl.when(step_i == 0)
        def _():
            sums_ref[...] = jnp.zeros_like(sums_ref)
            dwa_ref[...] = jnp.zeros_like(dwa_ref)
            dwx_ref[...] = jnp.zeros_like(dwx_ref)
            dxbuf[pl.ds(t, 8), :] = jnp.zeros((8, D), F32)
            gc[...] = jnp.zeros((8, D), F32)

        live = jnp.where(ti > 0, 1.0, 0.0)
        xbuf[pl.ds(0, 8), :] = lxh_ref[...].astype(F32)[8:16] * live
        xbuf[pl.ds(8, t), :] = lx_ref[...].astype(F32)
        hbuf[pl.ds(0, 8), :] = rech_ref[...] * live
        hbuf[pl.ds(8, t), :] = rec_ref[...]
        first = (lax.broadcasted_iota(jnp.int32, (t, 128), 0) + ti * t) == 0
        for b in range(N_LRU_BLOCKS):
            cs = slice(b * 128, (b + 1) * 128)
            _, xc, _, r, ig, _, a, mult = _lru_block_fwd(xbuf, lv_ref, wa_ref, wx_ref, b, t, first)
            a_s[:, cs] = a
            xc_s[:, cs] = xc
            r_s[:, cs] = r
            ig_s[:, cs] = ig
            mu_s[:, cs] = mult

        def step(q, g):
            tt = t - 1 - q
            dh = dr_ref[pl.ds(tt, 1), :] + g
            dh_s[pl.ds(tt, 1), :] = dh
            return a_s[pl.ds(tt, 1), :] * dh

        gc[0:1, :] = lax.fori_loop(0, t, step, gc[0:1, :], unroll=8)
        for b in range(N_LRU_BLOCKS):
            cs = slice(b * 128, (b + 1) * 128)
            dh = dh_s[:, cs]
            a = a_s[:, cs]
            xc = xc_s[:, cs]
            r = r_s[:, cs]
            ig = ig_s[:, cs]
            mult = mu_s[:, cs]
            sp = _softplus(-lv_ref[L_LAM:L_LAM + 1, cs])
            lam = lv_ref[L_LAM:L_LAM + 1, cs]
            da = dh * hbuf[pl.ds(7, t), cs]
            dmult = jnp.where(first, 0.0, dh * ig * xc)
            dig = dh * mult * xc
            dxc = dh * mult * ig
            dlog_a = da * a - dmult * (a * a) / mult
            dr = dlog_a * ((-LRU_C) * sp)
            dsp = jnp.sum(dlog_a * ((-LRU_C) * r), axis=0, keepdims=True)
            dza = dr * r * (1.0 - r)
            dzx = dig * ig * (1.0 - ig)
            dzab = dza.astype(BF16)
            dzxb = dzx.astype(BF16)
            xcb = xc.astype(BF16)
            dwa_ref[b] += _dot_tn(xcb, dzab)
            dwx_ref[b] += _dot_tn(xcb, dzxb)
            dxc = dxc + _dot_nt(dzab, wa_ref[b]) + _dot_nt(dzxb, wx_ref[b])
            sums_ref[L_LAM:L_LAM + 1, cs] += dsp * (-jax.nn.sigmoid(-lam))
            sums_ref[L_BA:L_BA + 1, cs] += jnp.sum(dza, axis=0, keepdims=True)
            sums_ref[L_BX:L_BX + 1, cs] += jnp.sum(dzx, axis=0, keepdims=True)
            sums_ref[L_CB:L_CB + 1, cs] += jnp.sum(dxc, axis=0, keepdims=True)
            for kk in range(4):
                sums_ref[kk:kk + 1, cs] += jnp.sum(dxc * xbuf[pl.ds(5 + kk, t), cs], axis=0, keepdims=True)
            dxbuf[pl.ds(0, t), cs] = dxc
            dlx = (lv_ref[3:4, cs] * dxc + lv_ref[2:3, cs] * dxbuf[pl.ds(1, t), cs]
                   + lv_ref[1:2, cs] * dxbuf[pl.ds(2, t), cs] + lv_ref[0:1, cs] * dxbuf[pl.ds(3, t), cs])
            dlx_ref[:, cs] = dlx.astype(BF16)
        dxbuf[pl.ds(t, 8), :] = dxbuf[pl.ds(0, 8), :]

    rev = lambda i: nt - 1 - i
    return pl.pallas_call(
        body, name="lru_bwd", grid=(nt,),
        in_specs=[pl.BlockSpec((t, D), lambda i: (rev(i), 0)),
                  pl.BlockSpec((16, D), lambda i: (jnp.maximum(rev(i) * (t // 16) - 1, 0), 0)),
                  pl.BlockSpec((t, D), lambda i: (rev(i), 0)),
                  pl.BlockSpec((8, D), lambda i: (jnp.maximum(rev(i) * (t // 8) - 1, 0), 0)),
                  pl.BlockSpec((t, D), lambda i: (rev(i), 0)),
                  pl.BlockSpec((8, D), lambda i: (0, 0)),
                  pl.BlockSpec((N_LRU_BLOCKS, 128, 128), lambda i: (0, 0, 0)),
                  pl.BlockSpec((N_LRU_BLOCKS, 128, 128), lambda i: (0, 0, 0)),
                  pl.BlockSpec(memory_space=pl.ANY)],
        out_specs=[pl.BlockSpec((t, D), lambda i: (rev(i), 0)),
                   pl.BlockSpec((8, D), lambda i: (0, 0)),
                   pl.BlockSpec((N_LRU_BLOCKS, 128, 128), lambda i: (0, 0, 0)),
                   pl.BlockSpec((N_LRU_BLOCKS, 128, 128), lambda i: (0, 0, 0))],
        out_shape=[SDS((s, IN_W), BF16), SDS((8, D), F32), SDS((N_LRU_BLOCKS, 128, 128), F32),
                   SDS((N_LRU_BLOCKS, 128, 128), F32)],
        scratch_shapes=[pltpu.VMEM((t + 8, D), F32), pltpu.VMEM((t + 8, D), F32), pltpu.VMEM((t + 8, D), F32)]
        + [pltpu.VMEM((t, D), F32)] * 6 + [pltpu.VMEM((8, D), F32)],
        input_output_aliases={8: 0},
        compiler_params=_cp(1))(proj, proj, rec, rec, drec, lvec, wa, wx, dproj)


def inproj_bwd(dproj, w_in, x, dx1, vecs):
    s = x.shape[0]
    tm, tk = min(512, s), IN_TILE
    nk = IN_W // tk
    per = IN_SHARD // tk

    def body(d_ref, w_ref, x_hbm, dx1_hbm, v_ref, gx_ref, sums_ref, acc, x_ref, dx1_ref, sems):
        i, k = pl.program_id(0), pl.program_id(1)
        fetches = _row_fetches((x_hbm, dx1_hbm), (x_ref, dx1_ref), sems, i, tm)

        @pl.when((i == 0) & (k == 0))
        def _():
            sums_ref[...] = jnp.zeros_like(sums_ref)

        @pl.when(k == 0)
        def _():
            acc[...] = jnp.zeros_like(acc)
            for cp in fetches:
                cp.start()

        acc[...] += _dot_nt(d_ref[...], w_ref[...])

        @pl.when(k == nk - 1)
        def _():
            for cp in fetches:
                cp.wait()
            g1 = v_ref[V_G1:V_G1 + 1, :]
            scale1 = v_ref[V_SCALE1:V_SCALE1 + 1, :]

            def sub(rb, carry):
                rs = pl.ds(pl.multiple_of(rb * SUB, SUB), SUB)
                dh = acc[rs, :]
                r1, xh = _rms_parts(x_ref[rs, :])
                sums_ref[0:1, :] += jnp.sum(dh, axis=0, keepdims=True)
                sums_ref[1:2, :] += jnp.sum(dh * (xh * g1), axis=0, keepdims=True)
                dxn = dh * (1.0 + scale1)
                sums_ref[2:3, :] += jnp.sum(dxn * xh, axis=0, keepdims=True)
                dxh = dxn * g1
                gx_ref[rs, :] = dx1_ref[rs, :] + r1 * (dxh - xh * jnp.mean(dxh * xh, axis=-1, keepdims=True))
                return carry

            lax.fori_loop(0, tm // SUB, sub, 0)

    return pl.pallas_call(
        body, name="inproj_bwd", grid=(s // tm, nk),
        in_specs=[pl.BlockSpec((tm, tk), lambda i, k: (i, k)),
                  pl.BlockSpec((None, D, tk), lambda i, k: (k // per, 0, k % per)),
                  pl.BlockSpec(memory_space=pl.ANY), pl.BlockSpec(memory_space=pl.ANY),
                  pl.BlockSpec((16, D), lambda i, k: (0, 0))],
        out_specs=[pl.BlockSpec((tm, D), lambda i, k: (i, 0)), pl.BlockSpec((8, D), lambda i, k: (0, 0))],
        out_shape=[SDS((s, D), F32), SDS((8, D), F32)],
        scratch_shapes=[pltpu.VMEM((tm, D), F32), pltpu.VMEM((tm, D), F32), pltpu.VMEM((tm, D), F32),
                        pltpu.SemaphoreType.DMA((2,))],
        compiler_params=_cp(2))(dproj, w_in, x, dx1, vecs)


def mod_columns(c16, w_ada, b_cols):
    tn = 512

    def body(c_ref, w_ref, b_ref, o_ref):
        cv = c_ref[...]
        ca = (cv * jax.nn.sigmoid(cv)).astype(BF16)
        o_ref[...] = _dot(ca, w_ref[...].astype(BF16)) + b_ref[...]

    return pl.pallas_call(
        body, name="mod_columns", grid=(ADA_SHARD // tn,),
        in_specs=[pl.BlockSpec((16, D), lambda j: (0, 0)), pl.BlockSpec((D, tn), lambda j: (0, j)),
                  pl.BlockSpec((1, tn), lambda j: (0, j))],
        out_specs=pl.BlockSpec((16, tn), lambda j: (0, j)),
        out_shape=SDS((16, ADA_SHARD), F32), compiler_params=_cp(1))(c16, w_ada, b_cols)


def wada_update(c16, dmod16, w, m, v):
    tm, tn = 512, 512

    def body(c_ref, d_ref, w_ref, m_ref, v_ref, g_out, dl_out, m_out, v_out):
        cv = c_ref[...]
        ca = (cv * jax.nn.sigmoid(cv)).astype(BF16)
        g = _dot_tn(ca, d_ref[...].astype(BF16))
        dl, m2, v2 = _adamw_math(w_ref[...], g, m_ref[...], v_ref[...])
        g_out[...] = g
        dl_out[...] = dl
        m_out[...] = m2
        v_out[...] = v2

    tile = pl.BlockSpec((tm, tn), lambda i, j: (i, j))
    return pl.pallas_call(
        body, name="wada_update", grid=(D // tm, ADA_SHARD // tn),
        in_specs=[pl.BlockSpec((16, tm), lambda i, j: (0, i)), pl.BlockSpec((16, tn), lambda i, j: (0, j)),
                  tile, tile, tile],
        out_specs=[tile] * 4, out_shape=[SDS((D, ADA_SHARD), F32)] * 4,
        compiler_params=_cp(2))(c16, dmod16, w, m, v)


def adamw_big(name, w, mine, theirs, m, v, c_idx):
    r, c = w.shape
    tr = 128
    per = (r // 2) // tr

    def body(c_ref, w_ref, a_ref, b_ref, m_ref, v_ref, g_out, dl_out, m_out, v_out):
        own = (pl.program_id(0) // per) == c_ref[0]
        g = jnp.where(own, a_ref[...], b_ref[...])
        dl, m2, v2 = _adamw_math(w_ref[...], g, m_ref[...], v_ref[...])
        g_out[...] = g
        dl_out[...] = dl
        m_out[...] = m2
        v_out[...] = v2

    tile = pl.BlockSpec((tr, c), lambda i, cr: (i, 0))
    half = pl.BlockSpec((tr, c), lambda i, cr: (i % per, 0))
    gs = pltpu.PrefetchScalarGridSpec(num_scalar_prefetch=1, grid=(r // tr,),
                                      in_specs=[tile, half, half, tile, tile], out_specs=[tile] * 4)
    return pl.pallas_call(body, name=name, grid_spec=gs, out_shape=[SDS((r, c), F32)] * 4,
                          compiler_params=_cp(1))(c_idx, w, mine, theirs, m, v)


def cast_into_slot(name, w, k_idx):
    r, c = w.shape
    tr = 256

    def body(k_ref, w_ref, o_ref):
        o_ref[...] = w_ref[...].astype(BF16)

    gs = pltpu.PrefetchScalarGridSpec(
        num_scalar_prefetch=1, grid=(r // tr,),
        in_specs=[pl.BlockSpec((tr, c), lambda i, kr: (i, 0))],
        out_specs=pl.BlockSpec((None, tr, c), lambda i, kr: (kr[0], i, 0)))
    return pl.pallas_call(body, name=name, grid_spec=gs, out_shape=SDS((N_CHIPS, r, c), BF16),
                          compiler_params=_cp(1))(k_idx, w)


def adamw_small(ws, gs, ms, vs):
    n = len(ws)

    def body(*refs):
        for i in range(n):
            dl, m2, v2 = _adamw_math(refs[i][...], refs[n + i][...], refs[2 * n + i][...], refs[3 * n + i][...])
            refs[4 * n + i][...] = dl
            refs[5 * n + i][...] = m2
            refs[6 * n + i][...] = v2

    vm = pl.BlockSpec(memory_space=pltpu.VMEM)
    shapes = [SDS(w.shape, F32) for w in ws]
    outs = pl.pallas_call(
        body, name="adamw_small", in_specs=[vm] * (4 * n), out_specs=[vm] * (3 * n), out_shape=shapes * 3,
        compiler_params=pltpu.CompilerParams(vmem_limit_bytes=VMEM_LIMIT))(*ws, *gs, *ms, *vs)
    return outs[:n], outs[n:2 * n], outs[2 * n:]


def sum_devices(name, gathered):
    rows, cols = gathered.shape[1:]
    tr = min(rows, 128 * D // cols)

    def body(x_ref, o_ref):
        acc = x_ref[0].astype(F32)
        for d in range(1, N_DEV):
            acc = acc + x_ref[d].astype(F32)
        o_ref[...] = acc

    return pl.pallas_call(
        body, name=name, grid=(rows // tr,),
        in_specs=[pl.BlockSpec((N_DEV, tr, cols), lambda i: (0, i, 0))],
        out_specs=pl.BlockSpec((tr, cols), lambda i: (i, 0)),
        out_shape=SDS((rows, cols), F32), compiler_params=_cp(1))(gathered)


def _mesh_pos():
    return lax.axis_index("x"), lax.axis_index("y"), lax.axis_index("c")


def _other_chips(x, y):
    return [(1 - x, y), (x, 1 - y), (1 - x, 1 - y)]


def all_gather_small(name, block):
    m_per, n = block.shape

    def body(x_ref, out_ref, send_sems, recv_sems, local_sem):
        x, y, c = _mesh_pos()
        me, sibling = (x, y, c), (x, y, 1 - c)
        chips = _other_chips(x, y)

        def rows(px, py, pc):
            return out_ref.at[pl.ds((4 * px + 2 * py + pc) * m_per, m_per), :]

        def copy(k, blk, to, src=None):
            return pltpu.make_async_remote_copy(
                src_ref=rows(*blk) if src is None else src, dst_ref=rows(*blk),
                send_sem=send_sems.at[k], recv_sem=recv_sems.at[k], device_id=to, device_id_type=MESH)

        mine = pltpu.make_async_copy(x_ref, rows(*me), local_sem)
        mine.start()
        first = [copy(0, me, sibling, src=x_ref)]
        first += [copy(1 + j, me, (*chip, c), src=x_ref) for j, chip in enumerate(chips)]
        for cp in first:
            cp.start()
        passed = [copy(4 + j, (*chip, c), sibling) for j, chip in enumerate(chips)]
        for j, chip in enumerate(chips):
            copy(1 + j, (*chip, c), me).wait_recv()
            passed[j].start()
        copy(0, sibling, me).wait_recv()
        for j, chip in enumerate(chips):
            copy(4 + j, (*chip, 1 - c), me).wait_recv()
        for cp in first + passed:
            cp.wait_send()
        mine.wait()

    vm = pl.BlockSpec(memory_space=pltpu.VMEM)
    return pl.pallas_call(
        body, name=name, out_shape=SDS((N_DEV * m_per, n), block.dtype), in_specs=[vm], out_specs=vm,
        scratch_shapes=[pltpu.SemaphoreType.DMA((7,)), pltpu.SemaphoreType.DMA((7,)), pltpu.SemaphoreType.DMA],
        compiler_params=pltpu.CompilerParams(vmem_limit_bytes=VMEM_LIMIT))(block)


def sibling_sum(name, grad, other, c_idx):
    _, r, cc = grad.shape
    h = r // 2
    tr = min(256, h)
    g4 = grad.reshape(N_CHIPS, 2, h, cc)

    def body(c_ref, a_ref, b_ref, o_ref):
        o_ref[...] = (a_ref[...].astype(F32) + b_ref[...].astype(F32)).astype(BF16)

    gs = pltpu.PrefetchScalarGridSpec(
        num_scalar_prefetch=1, grid=(N_CHIPS, h // tr),
        in_specs=[pl.BlockSpec((None, None, tr, cc), lambda s, i, cr: (s, cr[0], i, 0)),
                  pl.BlockSpec((None, tr, cc), lambda s, i, cr: (s, i, 0))],
        out_specs=pl.BlockSpec((None, tr, cc), lambda s, i, cr: (s, i, 0)))
    return pl.pallas_call(body, name=name, grid_spec=gs, out_shape=SDS((N_CHIPS, h, cc), BF16),
                          compiler_params=_cp(2))(c_idx, g4, other)


HBM_SPEC = pl.BlockSpec(memory_space=pltpu.HBM)
SEM_SPEC = pl.BlockSpec(memory_space=pltpu.SEMAPHORE)


def _side_effecting():
    return pltpu.CompilerParams(has_side_effects=pltpu.SideEffectType.DATAFLOW_SIDE_EFFECTING)


def _in_hbm(a):
    return pltpu.with_memory_space_constraint(a, pltpu.HBM)


ALL_CHIPS = (0, 1, 2)


def gather_start(name, bufs, after, rel=ALL_CHIPS, carry=None):
    n = len(bufs)
    nr = len(rel)
    halves = [w.shape[1] // 2 for w in bufs]
    extra = [] if carry is None else [carry]

    def body(*refs):
        ins = refs[:n]
        send_sems, recv_sems, token = refs[n + 1 + len(extra)], refs[n + 2 + len(extra)], refs[-1]
        x, y, c = _mesh_pos()
        k = 2 * x + y
        for i in range(n):
            reg = ins[i].at[k, pl.ds(c * halves[i], halves[i]), :]
            for q, j in enumerate(rel):
                chip = _other_chips(x, y)[j]
                pltpu.make_async_remote_copy(src_ref=reg, dst_ref=reg, send_sem=send_sems.at[nr * i + q],
                                             recv_sem=recv_sems.at[nr * i + q], device_id=(*chip, c),
                                             device_id_type=MESH).start()
        token[...] = jnp.zeros_like(token)

    outs = pl.pallas_call(
        body, name=name,
        out_shape=(pltpu.SemaphoreType.DMA((nr * n,)), pltpu.SemaphoreType.DMA((nr * n,)),
                   *[pltpu.HBM(w.shape, w.dtype) for w in list(bufs) + extra], SDS((8, 128), F32)),
        in_specs=[HBM_SPEC] * n + [pl.BlockSpec(memory_space=pl.ANY)] + [HBM_SPEC] * len(extra),
        out_specs=(SEM_SPEC, SEM_SPEC, *[HBM_SPEC] * (n + len(extra)), pl.BlockSpec(memory_space=pltpu.VMEM)),
        input_output_aliases={**{i: 2 + i for i in range(n)}, **({n + 1: 2 + n} if extra else {})},
        compiler_params=_side_effecting())(*[_in_hbm(w) for w in bufs], after, *[_in_hbm(w) for w in extra])
    return (outs[0], outs[1], list(outs[2:2 + n]), outs[-1]) + ((outs[2 + n],) if extra else ())


def gather_wait(name, send_sems, recv_sems, bufs, after, rel=ALL_CHIPS):
    n = len(bufs)
    nr = len(rel)
    halves = [w.shape[1] // 2 for w in bufs]

    def body(*refs):
        ins = refs[:n]
        send_sems, recv_sems = refs[n], refs[n + 1]
        x, y, c = _mesh_pos()
        k = 2 * x + y
        for i in range(n):
            for q, j in enumerate(rel):
                chip = _other_chips(x, y)[j]
                kj = 2 * chip[0] + chip[1]
                cp = pltpu.make_async_remote_copy(
                    src_ref=ins[i].at[k, pl.ds(c * halves[i], halves[i]), :],
                    dst_ref=ins[i].at[kj, pl.ds(c * halves[i], halves[i]), :],
                    send_sem=send_sems.at[nr * i + q], recv_sem=recv_sems.at[nr * i + q], device_id=(*chip, c),
                    device_id_type=MESH)
                cp.wait_send()
                cp.wait_recv()

    return pl.pallas_call(
        body, name=name, out_shape=[pltpu.HBM(w.shape, w.dtype) for w in bufs],
        in_specs=[HBM_SPEC] * n + [SEM_SPEC, SEM_SPEC, pl.BlockSpec(memory_space=pl.ANY)],
        out_specs=[HBM_SPEC] * n, input_output_aliases={i: i for i in range(n)},
        compiler_params=_side_effecting())(*bufs, send_sems, recv_sems, after)


def gather_forward(name, bufs, rel=ALL_CHIPS):
    n = len(bufs)
    halves = [w.shape[1] // 2 for w in bufs]

    def body(*refs):
        outs = refs[n:2 * n]
        send_sems, recv_sems = refs[2 * n:]
        x, y, c = _mesh_pos()
        chips = _other_chips(x, y)

        def copy(i, j, half, to):
            kj = 2 * chips[j][0] + chips[j][1]
            reg = outs[i].at[kj, pl.ds(half * halves[i], halves[i]), :]
            return pltpu.make_async_remote_copy(src_ref=reg, dst_ref=reg, send_sem=send_sems.at[i, j],
                                                recv_sem=recv_sems.at[i, j], device_id=to, device_id_type=MESH)

        cps = [copy(i, j, c, (x, y, 1 - c)) for i in range(n) for j in rel]
        for cp in cps:
            cp.start()
        for i in range(n):
            for j in rel:
                copy(i, j, 1 - c, (x, y, c)).wait_recv()
        for cp in cps:
            cp.wait_send()

    hbm = pl.BlockSpec(memory_space=pl.ANY)
    return pl.pallas_call(
        body, name=name, in_specs=[hbm] * n, out_specs=[hbm] * n,
        out_shape=[SDS(w.shape, w.dtype) for w in bufs], input_output_aliases={i: i for i in range(n)},
        scratch_shapes=[pltpu.SemaphoreType.DMA((n, 3)), pltpu.SemaphoreType.DMA((n, 3))])(*bufs)


def _exchange_plan(kind, srcs, zones):
    x, y, c = _mesh_pos()
    plan = []
    for src, zone in zip(srcs, zones):
        if kind == "chips":
            for j, chip in enumerate(_other_chips(x, y)):
                plan.append((src.at[2 * chip[0] + chip[1]], zone.at[j], (*chip, c)))
        elif kind == "sibling":
            h = zone.shape[1]
            plan.append((src.at[:, pl.ds((1 - c) * h, h), :], zone, (x, y, 1 - c)))
        else:
            peers = [(x, y, 1 - c)] + [(*chip, cc) for chip in _other_chips(x, y) for cc in (c, 1 - c)]
            plan += [(src, zone.at[4 * x + 2 * y + c], peer) for peer in peers]
    return plan


_COPIES_PER_ARRAY = {"chips": 3, "sibling": 1, "all": N_DEV - 1}


def _landing_zones(kind, srcs):
    if kind == "chips":
        return [lax.empty((3,) + t.shape[1:], t.dtype) for t in srcs]
    if kind == "sibling":
        return [lax.empty((t.shape[0], t.shape[1] // 2, t.shape[2]), t.dtype) for t in srcs]
    return [jnp.broadcast_to(t, (N_DEV,) + t.shape) for t in srcs]


def exchange_start(name, kind, srcs, after):
    n = len(srcs)
    lands = _landing_zones(kind, srcs)
    n_copies = n * _COPIES_PER_ARRAY[kind]

    def body(*refs):
        send_sems, recv_sems, token = refs[2 * n + 1], refs[2 * n + 2], refs[-1]
        for q, (src, dst, dev) in enumerate(_exchange_plan(kind, refs[:n], refs[n:2 * n])):
            pltpu.make_async_remote_copy(src_ref=src, dst_ref=dst, send_sem=send_sems.at[q], recv_sem=recv_sems.at[q],
                                         device_id=dev, device_id_type=MESH).start()
        token[...] = jnp.zeros_like(token)

    outs = pl.pallas_call(
        body, name=name,
        out_shape=(pltpu.SemaphoreType.DMA((n_copies,)), pltpu.SemaphoreType.DMA((n_copies,)),
                   *[pltpu.HBM(t.shape, t.dtype) for t in srcs], *[pltpu.HBM(t.shape, t.dtype) for t in lands],
                   SDS((8, 128), F32)),
        in_specs=[HBM_SPEC] * (2 * n) + [pl.BlockSpec(memory_space=pl.ANY)],
        out_specs=(SEM_SPEC, SEM_SPEC, *[HBM_SPEC] * (2 * n), pl.BlockSpec(memory_space=pltpu.VMEM)),
        input_output_aliases={i: 2 + i for i in range(2 * n)},
        compiler_params=_side_effecting())(*[_in_hbm(t) for t in srcs], *[_in_hbm(t) for t in lands], after)
    return outs[0], outs[1], list(outs[2:2 + n]), list(outs[2 + n:2 + 2 * n]), outs[-1]


def exchange_wait(name, kind, send_sems, recv_sems, srcs, lands, after):
    n = len(srcs)

    def body(*refs):
        send_sems, recv_sems = refs[2 * n], refs[2 * n + 1]
        for q, (src, dst, dev) in enumerate(_exchange_plan(kind, refs[:n], refs[n:2 * n])):
            cp = pltpu.make_async_remote_copy(src_ref=src, dst_ref=dst, send_sem=send_sems.at[q],
                                              recv_sem=recv_sems.at[q], device_id=dev, device_id_type=MESH)
            cp.wait_send()
            cp.wait_recv()

    outs = pl.pallas_call(
        body, name=name, out_shape=[pltpu.HBM(t.shape, t.dtype) for t in srcs + lands],
        in_specs=[HBM_SPEC] * (2 * n) + [SEM_SPEC, SEM_SPEC, pl.BlockSpec(memory_space=pl.ANY)],
        out_specs=[HBM_SPEC] * (2 * n), input_output_aliases={i: i for i in range(2 * n)},
        compiler_params=_side_effecting())(*srcs, *lands, send_sems, recv_sems, after)
    return list(outs[:n]), list(outs[n:])


def chip_sum(name, sums, parts, k_idx):
    _, h, cc = parts.shape
    tr = min(256, h)

    def body(k_ref, own_ref, p_ref, o_ref):
        acc = own_ref[...].astype(F32)
        for s in range(3):
            acc = acc + p_ref[s].astype(F32)
        o_ref[...] = acc

    gs = pltpu.PrefetchScalarGridSpec(
        num_scalar_prefetch=1, grid=(h // tr,),
        in_specs=[pl.BlockSpec((None, tr, cc), lambda i, kr: (kr[0], i, 0)),
                  pl.BlockSpec((3, tr, cc), lambda i, kr: (0, i, 0))],
        out_specs=pl.BlockSpec((tr, cc), lambda i, kr: (i, 0)))
    return pl.pallas_call(body, name=name, grid_spec=gs, out_shape=SDS((h, cc), F32),
                          compiler_params=_cp(1))(k_idx, sums, parts)


def halves_exchange(name, halves):
    n = len(halves)

    def body(*refs):
        ins, outs = refs[:n], refs[n:2 * n]
        send_sems, recv_sems = refs[2 * n:]
        x, y, c = _mesh_pos()
        cps = []
        for i in range(n):
            cp = pltpu.make_async_remote_copy(
                src_ref=ins[i], dst_ref=outs[i], send_sem=send_sems.at[i], recv_sem=recv_sems.at[i],
                device_id=(x, y, 1 - c), device_id_type=MESH)
            cp.start()
            cps.append(cp)
        for cp in cps:
            cp.wait_recv()
        for cp in cps:
            cp.wait_send()

    hbm = pl.BlockSpec(memory_space=pl.ANY)
    return pl.pallas_call(
        body, name=name, in_specs=[hbm] * n, out_specs=[hbm] * n,
        out_shape=[SDS(t.shape, F32) for t in halves],
        scratch_shapes=[pltpu.SemaphoreType.DMA((n,)), pltpu.SemaphoreType.DMA((n,))])(*halves)


def local_step(x, tgt, vecs, lvec, wa, wx, sinks, rel_bias, proj, h, w_in, rest_weights, hook):
    buckets = t5_bucket_table()
    band = bias_band(rel_bias.T, buckets).reshape(N_HEADS, BLOCK, 2 * BLOCK)

    ya, rec, ya_t = lru_fwd(proj, lvec, wa, wx)
    att, att_t = attn_fwd(proj, band, sinks)
    w_lru_out, w_attn_out, w_out = rest_weights("mix", att[:8, :128] + ya[:8, :128])
    w_lru_out2, w_attn_out2, w_out2 = w_lru_out.reshape(D, D), w_attn_out.reshape(D, D), w_out.reshape(D, D)
    yab, merged, merged_t = merge_fwd(ya, att, w_lru_out2, w_attn_out2, proj)
    x1, o1 = outproj_fwd(merged, w_out2, x, vecs)
    w_ff1, w_ff2 = rest_weights("ff", o1[:8, :128])
    w_ff2_2 = w_ff2.reshape(D_FF, D)
    f, h2, fft = ff1_fwd(x1, vecs, w_ff1)
    dx2, do2, sums_f, loss = ff2_loss(f, w_ff2_2, x1, tgt, vecs)

    df = ff2_bwd(do2, w_ff2_2, f)
    g_ff2 = matmul_bf16("dw_ff2", fft, do2, WG_TM, 512)
    dx1, do1, sums_2 = ff1_bwd(df, w_ff1, x1, dx2, o1, vecs)
    g_ff1 = weight_grad("dw_ff1", h2, df, 512, (N_CHIPS, D, D), (None, WG_TM, 512), lambda i, j: (j // 4, i, j % 4))
    dyab, dproj = outproj_bwd(do1, w_out2, yab, proj)
    g_out = matmul_bf16("dw_out", merged_t, do1, WG_TM, 512)
    drec, dproj = lruout_bwd(dyab, w_lru_out2, rec, proj, dproj)
    dyab2 = dyab.reshape(2 * x.shape[0], D)
    g_lru_out = matmul_bf16("dw_lru_out", ya_t, dyab2, WG_TM, 512)
    datt = attnout_bwd(dyab, w_attn_out2)
    g_attn_out = matmul_bf16("dw_attn_out", att_t, dyab2, WG_TM, 512, b_part=1)
    zero = hook("grads_a", [g_lru_out.reshape(N_CHIPS, D // 4, D), g_attn_out.reshape(N_CHIPS, D // 4, D),
                            g_out.reshape(N_CHIPS, D // 4, D), g_ff1, g_ff2.reshape(N_CHIPS, D_FF // 4, D)])
    dproj, dkb, dvb, dband, dsink = attn_bwd(proj, band, sinks + zero, datt, dproj)
    zero = hook("after_attn_bwd", dkb)
    dproj = dkv_combine(dkb, dvb, dproj)
    dproj, sums_l, d_wa, d_wx = lru_bwd(proj, rec, drec, lvec + zero, wa, wx, dproj)
    hook("lru_grads", (d_wa, d_wx))
    per = IN_SHARD // IN_TILE
    g_in = weight_grad("dw_in", h, dproj, IN_TILE, (N_CHIPS, D, IN_SHARD), (None, WG_TM, IN_TILE),
                       lambda i, j: (j // per, i, j % per))
    zero = hook("grads_b", [g_in])
    grad_x, sums_1 = inproj_bwd(dproj, w_in, x, dx1, vecs + zero)
    d_rel_bias = bias_band_bwd(dband.reshape(N_HEADS, BLOCK * 2 * BLOCK), buckets)

    small = dict(sums_f=sums_f, sums_2=sums_2, sums_1=sums_1, sums_l=sums_l, d_wa=d_wa, d_wx=d_wx,
                 d_sinks=dsink[:, 0], d_rel_bias=d_rel_bias)
    return loss, grad_x, small


def _pad_rows(a, rows):
    return jnp.concatenate([a, jnp.zeros((rows - a.shape[0], a.shape[1]), a.dtype)], axis=0)


def kernel(x, c, w_ada, b_ada, norm1_g, w_in, conv_w, conv_b, lru_wa, lru_ba, lru_wx, lru_bx, lru_lambda, w_lru_out, w_attn_out, attn_sinks, rel_bias, w_out, norm2_g, w_ff1, w_ff2, final_g, loss_target, m_w_ada, m_b_ada, m_norm1_g, m_w_in, m_conv_w, m_conv_b, m_lru_wa, m_lru_ba, m_lru_wx, m_lru_bx, m_lru_lambda, m_w_lru_out, m_w_attn_out, m_attn_sinks, m_rel_bias, m_w_out, m_norm2_g, m_w_ff1, m_w_ff2, m_final_g, v_w_ada, v_b_ada, v_norm1_g, v_w_in, v_conv_w, v_conv_b, v_lru_wa, v_lru_ba, v_lru_wx, v_lru_bx, v_lru_lambda, v_w_lru_out, v_w_attn_out, v_attn_sinks, v_rel_bias, v_w_out, v_norm2_g, v_w_ff1, v_w_ff2, v_final_g):
    xi, yi, ci = _mesh_pos()
    chip = 2 * xi + yi
    dev = 2 * chip + ci
    z8 = jnp.zeros((8, D), F32)

    conv_rows = jnp.concatenate([conv_w[0], jnp.zeros((4, D - D // 4), F32)], axis=1)
    pack0 = jnp.concatenate([c, conv_rows, jnp.zeros((3, D), F32)], axis=0)
    g0 = all_gather_small("gather_cond", pack0).reshape(N_DEV, 8, D)
    c_all = g0[:, 0, :]
    conv_full = jnp.concatenate([g0[2 * k, 1:5, :D // 4] for k in range(N_CHIPS)], axis=1)
    c16 = jnp.concatenate([c_all, z8], axis=0)
    b_cols = lax.dynamic_slice_in_dim(b_ada, chip * ADA_SHARD, ADA_SHARD, axis=1)
    mod_c = mod_columns(c16, w_ada[0], b_cols)
    g1 = all_gather_small("gather_mod", mod_c).reshape(N_DEV, 16, ADA_SHARD)
    mod = jnp.concatenate([lax.dynamic_index_in_dim(g1[2 * k], dev, axis=0, keepdims=False) for k in range(N_CHIPS)])
    shift1, scale1, gate1, shift2, scale2, gate2 = [mod[i * D:(i + 1) * D] for i in range(6)]
    vecs = jnp.stack([norm1_g[0], scale1, shift1, gate1, norm2_g[0], scale2, shift2, gate2, final_g]
                     + [jnp.zeros((D,), F32)] * 7)
    lvec = jnp.concatenate([conv_full, conv_b, lru_ba, lru_bx, lru_lambda], axis=0)

    shards = [w_in[0], w_lru_out[0], w_attn_out[0], w_out[0], w_ff1[0], w_ff2[0]]
    names = ["w_in", "w_lru_out", "w_attn_out", "w_out", "w_ff1", "w_ff2"]
    k_idx = jnp.reshape(chip, (1,)).astype(jnp.int32)
    c_idx = jnp.reshape(ci, (1,)).astype(jnp.int32)
    near, far = (0, 1), (2,)
    shard_of = lambda flip: jnp.reshape(chip ^ flip, (1,)).astype(jnp.int32)
    x2d = x[0]
    n_send, n_recv, w_in_buf, _ = gather_start(
        "gather_start_in_near", [cast_into_slot("cast_w_in", shards[0], k_idx)], vecs, near)
    proj, h = inproj_fwd("inproj_fwd_own", x2d, None, vecs, w_in_buf[0], None, k_idx)
    slots = [cast_into_slot("cast_" + nm, w, k_idx) for nm, w in zip(names[1:], shards[1:])]
    w_in_buf = gather_forward("gather_forward_in_near", gather_wait(
        "gather_wait_in_near", n_send, n_recv, w_in_buf, proj[:8, :128] + slots[-1][0, :8, :128], near), near)
    f_send, f_recv, w_in_buf, _ = gather_start("gather_start_in_far", w_in_buf, proj[:8, :128], far)
    in_flight = {"mix": gather_start("gather_start_mix", slots[:3], proj[:8, :128])}
    in_flight["ff"] = gather_start("gather_start_ff", slots[3:], in_flight["mix"][3], carry=w_in_buf[0])
    w_in_buf = [in_flight["ff"][4]]
    proj = inproj_fwd("inproj_fwd_x", x2d, h, vecs, w_in_buf[0], proj, shard_of(2))
    proj = inproj_fwd("inproj_fwd_y", x2d, h, vecs, w_in_buf[0], proj, shard_of(1))
    w_in_buf = gather_forward("gather_forward_in_far", gather_wait(
        "gather_wait_in_far", f_send, f_recv, w_in_buf, proj[:8, :128], far), far)
    proj = inproj_fwd("inproj_fwd_d", x2d, h, vecs, w_in_buf[0], proj, shard_of(3))
    w_in_full = w_in_buf[0]
    pending = {}

    def rest_weights(group, after):
        send_sems, recv_sems, bufs = in_flight[group][:3]
        return gather_forward("gather_forward_" + group,
                              gather_wait("gather_wait_" + group, send_sems, recv_sems, bufs, after))

    def reduce_hook(event, payload):
        if event == "grads_a":
            pending["sib_a"] = exchange_start("sibling_start_a", "sibling", payload, payload[0])
            return pending["sib_a"][-1][0, 0]
        if event == "lru_grads":
            pack_w = jnp.concatenate([payload[0].reshape(D, 128), payload[1].reshape(D, 128)], axis=0).astype(BF16)
            pending["lru_w"] = exchange_start("lru_w_grads_start", "all", [pack_w], pack_w)
            return pending["lru_w"][-1][0, 0]
        if event == "grads_b":
            pending["sib_b"] = exchange_start("sibling_start_b", "sibling", payload, pending["lru_w"][-1])
            return pending["sib_b"][-1][0, 0]
        return chips_start("a", names[1:], payload)

    def chips_start(tag, nms, after):
        send_sems, recv_sems, grads, lands, _ = pending["sib_" + tag]
        grads, lands = exchange_wait("sibling_wait_" + tag, "sibling", send_sems, recv_sems, grads, lands, after)
        sums = [sibling_sum("sibling_sum_" + nm, g, o, c_idx) for nm, g, o in zip(nms, grads, lands)]
        pending[tag] = exchange_start("exchange_start_" + tag, "chips", sums, sums[0])
        return pending[tag][-1][0, 0]

    loss_t, grad_x, small = local_step(
        x2d, loss_target[0], vecs, lvec, lru_wa[0].astype(BF16), lru_wx[0].astype(BF16),
        attn_sinks[0], rel_bias, proj, h, w_in_full, rest_weights, reduce_hook)
    chips_start("b", names[:1], grad_x)

    big_m = dict(zip(names, [m_w_in, m_w_lru_out, m_w_attn_out, m_w_out, m_w_ff1, m_w_ff2]))
    big_v = dict(zip(names, [v_w_in, v_w_lru_out, v_w_attn_out, v_w_out, v_w_ff1, v_w_ff2]))
    local_w = dict(zip(names, shards))
    g_big, d_big, nm_big, nv_big = {}, {}, {}, {}

    def links_done(tag, after):
        send_sems, recv_sems, sums, lands, _ = pending[tag]
        return exchange_wait("exchange_wait_" + tag, "chips", send_sems, recv_sems, sums, lands, after)

    def finish_reduce(nms, sums, lands):
        mine = [chip_sum("chip_sum_" + nm, t, p, k_idx) for nm, t, p in zip(nms, sums, lands)]
        theirs = halves_exchange("halves_exchange_" + nms[0], mine)
        for nm, a, b in zip(nms, mine, theirs):
            g2, dl, m2, v2 = adamw_big("adamw_" + nm, local_w[nm], a, b, big_m[nm][0], big_v[nm][0], c_idx)
            g_big[nm], d_big[nm], nm_big[nm], nv_big[nm] = g2[None], dl[None], m2[None], v2[None]
        return lax.optimization_barrier(tuple(nv_big[nm] for nm in nms))[0]

    done_a = finish_reduce(names[1:], *links_done("a", pending["b"][-1]))
    sums_b, lands_b = links_done("b", done_a)
    w_send, w_recv, w_src, w_lands, _ = pending["lru_w"]
    w_src, w_lands = exchange_wait("lru_w_grads_wait", "all", w_send, w_recv, w_src, w_lands, lands_b[0])

    sums_f, sums_2, sums_1, sums_l = small["sums_f"], small["sums_2"], small["sums_1"], small["sums_l"]
    vec_rows = jnp.stack([sums_1[2], sums_2[2], sums_f[0], sums_l[L_CB], sums_l[L_BA], sums_l[L_BX],
                          sums_l[L_LAM], jnp.zeros((D,), F32)])
    mod_rows = jnp.stack([sums_1[0], sums_1[1], sums_2[3], sums_2[0], sums_2[1], sums_f[1],
                          jnp.zeros((D,), F32), jnp.zeros((D,), F32)])
    att_rows = jnp.concatenate([
        jnp.concatenate([small["d_sinks"], jnp.zeros((D - N_HEADS,), F32)])[None],
        jnp.concatenate([small["d_rel_bias"].reshape(-1), jnp.zeros((D - N_BUCKETS * N_HEADS,), F32)])[None],
        jnp.zeros((6, D), F32)], axis=0)
    pack = jnp.concatenate([vec_rows, _pad_rows(sums_l[0:4], 8), mod_rows, att_rows], axis=0)
    pack, lru_w_all = lax.optimization_barrier((pack, w_lands[0]))
    gathered = all_gather_small("gather_small_grads", pack).reshape(N_DEV, P_WA, D)
    total = sum_devices("sum_small_grads", gathered)
    total_w = sum_devices("sum_lru_w_grads", lru_w_all)
    dmod_all = gathered[:, P_MOD:P_MOD + 6, :].reshape(N_DEV, 6 * D)
    dmod16 = jnp.concatenate([lax.dynamic_slice_in_dim(dmod_all, chip * ADA_SHARD, ADA_SHARD, axis=1),
                              jnp.zeros((8, ADA_SHARD), F32)], axis=0)
    g_w_ada, d_w_ada, nm_w_ada, nv_w_ada = wada_update(c16, dmod16, w_ada[0], m_w_ada[0], v_w_ada[0])
    finish_reduce(names[:1], sums_b, lands_b)
    loss = lax.psum(lax.optimization_barrier((loss_t, total))[0][0, 0], ("x", "y", "c"))

    conv_g = lax.dynamic_slice_in_dim(total[P_CONVW:P_CONVW + 4], chip * (D // 4), D // 4, axis=1)
    sm_names = ["b_ada", "norm1_g", "conv_w", "conv_b", "lru_wa", "lru_ba", "lru_wx", "lru_bx", "lru_lambda",
                "attn_sinks", "rel_bias", "norm2_g", "final_g"]
    sm_w = [b_ada.reshape(6, D), norm1_g, conv_w[0], conv_b, lru_wa.reshape(D, 128), lru_ba, lru_wx.reshape(D, 128),
            lru_bx, lru_lambda, attn_sinks, rel_bias, norm2_g, final_g[None]]
    sm_m = [m_b_ada.reshape(6, D), m_norm1_g, m_conv_w[0], m_conv_b, m_lru_wa.reshape(D, 128), m_lru_ba,
            m_lru_wx.reshape(D, 128), m_lru_bx, m_lru_lambda, m_attn_sinks, m_rel_bias, m_norm2_g, m_final_g[None]]
    sm_v = [v_b_ada.reshape(6, D), v_norm1_g, v_conv_w[0], v_conv_b, v_lru_wa.reshape(D, 128), v_lru_ba,
            v_lru_wx.reshape(D, 128), v_lru_bx, v_lru_lambda, v_attn_sinks, v_rel_bias, v_norm2_g, v_final_g[None]]
    sm_g = [total[P_MOD:P_MOD + 6], total[0:1], conv_g, total[3:4], total_w[0:D], total[4:5],
            total_w[D:2 * D], total[5:6], total[6:7], total[P_ATT:P_ATT + 1, :N_HEADS],
            total[P_ATT + 1, :N_BUCKETS * N_HEADS].reshape(N_BUCKETS, N_HEADS), total[1:2], total[2:3]]
    sm_d, sm_nm, sm_nv = adamw_small(sm_w, sm_g, sm_m, sm_v)
    shapes = dict(b_ada=b_ada.shape, norm1_g=norm1_g.shape, conv_w=conv_w.shape, conv_b=conv_b.shape,
                  lru_wa=lru_wa.shape, lru_ba=lru_ba.shape, lru_wx=lru_wx.shape, lru_bx=lru_bx.shape,
                  lru_lambda=lru_lambda.shape, attn_sinks=attn_sinks.shape, rel_bias=rel_bias.shape,
                  norm2_g=norm2_g.shape, final_g=final_g.shape)
    grads = dict(w_ada=g_w_ada[None], **g_big)
    deltas = dict(w_ada=d_w_ada[None], **d_big)
    new_m = dict(w_ada=nm_w_ada[None], **nm_big)
    new_v = dict(w_ada=nv_w_ada[None], **nv_big)
    for i, nm in enumerate(sm_names):
        grads[nm] = sm_g[i].reshape(shapes[nm])
        deltas[nm] = sm_d[i].reshape(shapes[nm])
        new_m[nm] = sm_nm[i].reshape(shapes[nm])
        new_v[nm] = sm_nv[i].reshape(shapes[nm])
    order = ["w_ada", "b_ada", "norm1_g", "w_in", "conv_w", "conv_b", "lru_wa", "lru_ba", "lru_wx", "lru_bx",
             "lru_lambda", "w_lru_out", "w_attn_out", "attn_sinks", "rel_bias", "w_out", "norm2_g", "w_ff1", "w_ff2",
             "final_g"]
    return (loss, grad_x[None], *[grads[n] for n in order], *[deltas[n] for n in order],
            *[new_m[n] for n in order], *[new_v[n] for n in order])
```

```python
import math

import numpy as np
import jax
import jax.numpy as jnp
from jax import lax
from jax.experimental import pallas as pl
from jax.experimental.pallas import tpu as pltpu

F32 = jnp.float32
BF16 = jnp.bfloat16
SDS = jax.ShapeDtypeStruct
MESH = pl.DeviceIdType.MESH

D = 2048
D_FF = 4 * D
N_HEADS = 32
HEAD_DIM = 64
BLOCK = 128
N_LRU_BLOCKS = 16
LRU_C = 8.0
EPS = 1e-6
NEG_INF = -1e30
N_BUCKETS = 32
MAX_DISTANCE = 128
IN_W = 10752
IN_SHARD = IN_W // 4
IN_TILE = 896
ADA_SHARD = 6 * D // 4
OFF_LRU, OFF_GATE, OFF_Q, OFF_K, OFF_V, OFF_GA, OFF_GB = 0, 2048, 4096, 6144, 6400, 6656, 8704
SCALE = HEAD_DIM ** -0.5
N_CHIPS = 4
N_DEV = 8

ADAM_LR, ADAM_B1, ADAM_B2, ADAM_EPS, ADAM_WD, ADAM_STEP = 0.001, 0.9, 0.999, 1e-08, 0.01, 10
ADAM_C1 = 1.0 - ADAM_B1 ** ADAM_STEP
ADAM_C2 = 1.0 - ADAM_B2 ** ADAM_STEP

VMEM_LIMIT = 58 * 2 ** 20
SUB = 128
WG_TM = 1024
V_G1, V_SCALE1, V_SHIFT1, V_GATE1, V_G2, V_SCALE2, V_SHIFT2, V_GATE2, V_G3 = range(9)
L_CW0, L_CB, L_BA, L_BX, L_LAM = 0, 4, 5, 6, 7
P_VEC, P_CONVW, P_MOD, P_ATT, P_WA = 0, 8, 16, 24, 32


def _cp(n_axes):
    return pltpu.CompilerParams(dimension_semantics=("arbitrary",) * n_axes, vmem_limit_bytes=VMEM_LIMIT)


def _dot(a, b):
    return jnp.dot(a, b, preferred_element_type=F32)


def _dot_nt(a, b):
    return lax.dot_general(a, b, (((1,), (1,)), ((), ())), preferred_element_type=F32)


def _dot_tn(a, b):
    return lax.dot_general(a, b, (((0,), (0,)), ((), ())), preferred_element_type=F32)


_G0 = math.sqrt(2.0 / math.pi)
_G1 = 0.044715


def _gelu(x):
    return 0.5 * x * (1.0 + jnp.tanh(_G0 * (x + _G1 * x * x * x)))


def _gelu_grad(x):
    x2 = x * x
    t = jnp.tanh(_G0 * (x + _G1 * x * x2))
    return 0.5 * (1.0 + t) + 0.5 * x * (1.0 - t * t) * _G0 * (1.0 + 3.0 * _G1 * x2)


def _sigmoid(x):
    return 0.5 * jnp.tanh(0.5 * x) + 0.5


def _one_minus_exp2(x):
    t = jnp.tanh(x)
    return (-2.0 * t) / (1.0 - t)


def _softplus(z):
    e = jnp.exp(-jnp.abs(z))
    u = 1.0 + e
    l1p = jnp.where(u == 1.0, e, jnp.log(u) * e / (u - 1.0))
    return jnp.maximum(z, 0.0) + l1p


def _adamw_math(w, g, m, v):
    m2 = ADAM_B1 * m + (1.0 - ADAM_B1) * g
    v2 = ADAM_B2 * v + (1.0 - ADAM_B2) * (g * g)
    m_hat = m2 / ADAM_C1
    v_hat = v2 / ADAM_C2
    delta = -ADAM_LR * (m_hat / (jnp.sqrt(v_hat) + ADAM_EPS) + ADAM_WD * w)
    return delta, m2, v2


def _rms_parts(xv):
    r = lax.rsqrt(jnp.mean(xv * xv, axis=-1, keepdims=True) + EPS)
    return r, xv * r


def _row_fetches(hbm_refs, bufs, sems, i, rows):
    return [pltpu.make_async_copy(h.at[pl.ds(i * rows, rows), :], b, sems.at[n])
            for n, (h, b) in enumerate(zip(hbm_refs, bufs))]


def _modulated_norm(x_ref, v_ref, row_g, row_scale, row_shift, h_ref, rows):
    g, scale, shift = v_ref[row_g:row_g + 1, :], v_ref[row_scale:row_scale + 1, :], v_ref[row_shift:row_shift + 1, :]

    def sub(rb, carry):
        rs = pl.ds(pl.multiple_of(rb * SUB, SUB), SUB)
        _, xh = _rms_parts(x_ref[rs, :])
        h_ref[rs, :] = ((xh * g) * (1.0 + scale) + shift).astype(BF16)
        return carry

    lax.fori_loop(0, rows // SUB, sub, 0)


def inproj_fwd(name, x, h, vecs, w_in, proj, shard):
    s = x.shape[0]
    tm = min(1024, s)
    per = IN_SHARD // IN_TILE
    first = h is None

    def body(*refs):
        if first:
            _, x_ref, v_ref, w_ref, proj_ref, h_ref = refs

            @pl.when(pl.program_id(1) == 0)
            def _():
                _modulated_norm(x_ref, v_ref, V_G1, V_SCALE1, V_SHIFT1, h_ref, tm)
        else:
            _, h_ref, w_ref, _, proj_ref = refs
        proj_ref[...] = _dot(h_ref[...], w_ref[...]).astype(BF16)

    rows = pl.BlockSpec((tm, D), lambda i, j, sr: (i, 0))
    w_spec = pl.BlockSpec((None, D, IN_TILE), lambda i, j, sr: (sr[0], 0, j))
    proj_spec = pl.BlockSpec((tm, IN_TILE), lambda i, j, sr: (i, sr[0] * per + j))
    if first:
        gs = pltpu.PrefetchScalarGridSpec(
            num_scalar_prefetch=1, grid=(s // tm, per),
            in_specs=[rows, pl.BlockSpec((16, D), lambda i, j, sr: (0, 0)), w_spec], out_specs=[proj_spec, rows])
        return pl.pallas_call(body, name=name, grid_spec=gs, out_shape=[SDS((s, IN_W), BF16), SDS((s, D), BF16)],
                              compiler_params=_cp(2))(shard, x, vecs, w_in)
    gs = pltpu.PrefetchScalarGridSpec(
        num_scalar_prefetch=1, grid=(s // tm, per),
        in_specs=[rows, w_spec, pl.BlockSpec(memory_space=pl.ANY)], out_specs=proj_spec)
    return pl.pallas_call(body, name=name, grid_spec=gs, out_shape=SDS((s, IN_W), BF16),
                          input_output_aliases={3: 0}, compiler_params=_cp(2))(shard, h, w_in, proj)


def _lru_block_fwd(xbuf, lv_ref, wa_ref, wx_ref, b, t, first):
    cs = slice(b * 128, (b + 1) * 128)
    x0 = xbuf[pl.ds(8, t), cs]
    x1 = xbuf[pl.ds(7, t), cs]
    x2 = xbuf[pl.ds(6, t), cs]
    x3 = xbuf[pl.ds(5, t), cs]
    xc = (lv_ref[L_CB:L_CB + 1, cs] + lv_ref[3:4, cs] * x0 + lv_ref[2:3, cs] * x1
          + lv_ref[1:2, cs] * x2 + lv_ref[0:1, cs] * x3)
    xcb = xc.astype(BF16)
    r = _sigmoid(_dot(xcb, wa_ref[b]) + lv_ref[L_BA:L_BA + 1, cs])
    ig = _sigmoid(_dot(xcb, wx_ref[b]) + lv_ref[L_BX:L_BX + 1, cs])
    sp = _softplus(-lv_ref[L_LAM:L_LAM + 1, cs])
    log_a = (-LRU_C) * r * sp
    a = jnp.exp(log_a)
    mult = jnp.where(first, 1.0, jnp.sqrt(_one_minus_exp2(log_a)))
    return (x0, x1, x2, x3), xc, xcb, r, ig, sp, a, mult


def lru_fwd(proj, lvec, wa, wx):
    s = proj.shape[0]
    t = min(256, s)

    def body(lx_ref, gate_ref, lv_ref, wa_ref, wx_ref, ya_ref, rec_ref, yat_ref, xbuf, a_s, u_s, hc):
        i = pl.program_id(0)

        @pl.when(i == 0)
        def _():
            xbuf[pl.ds(0, 8), :] = jnp.zeros((8, D), F32)
            hc[...] = jnp.zeros((8, D), F32)

        @pl.when(i > 0)
        def _():
            xbuf[pl.ds(0, 8), :] = xbuf[pl.ds(t, 8), :]

        xbuf[pl.ds(8, t), :] = lx_ref[...].astype(F32)
        first = (lax.broadcasted_iota(jnp.int32, (t, 128), 0) + i * t) == 0
        for b in range(N_LRU_BLOCKS):
            cs = slice(b * 128, (b + 1) * 128)
            _, xc, _, _, ig, _, a, mult = _lru_block_fwd(xbuf, lv_ref, wa_ref, wx_ref, b, t, first)
            a_s[:, cs] = a
            u_s[:, cs] = mult * (ig * xc)

        def step(tt, h):
            h = a_s[pl.ds(tt, 1), :] * h + u_s[pl.ds(tt, 1), :]
            rec_ref[pl.ds(tt, 1), :] = h
            return h

        hc[0:1, :] = lax.fori_loop(0, t, step, hc[0:1, :], unroll=8)
        for b in range(N_LRU_BLOCKS):
            cs = slice(b * 128, (b + 1) * 128)
            yb = (rec_ref[:, cs] * _gelu(gate_ref[:, cs].astype(F32))).astype(BF16)
            ya_ref[:, cs] = yb
            yat_ref[cs, :] = yb.T

    return pl.pallas_call(
        body, name="lru_fwd", grid=(s // t,),
        in_specs=[pl.BlockSpec((t, D), lambda i: (i, OFF_LRU // D)),
                  pl.BlockSpec((t, D), lambda i: (i, OFF_GATE // D)),
                  pl.BlockSpec((8, D), lambda i: (0, 0)),
                  pl.BlockSpec((N_LRU_BLOCKS, 128, 128), lambda i: (0, 0, 0)),
                  pl.BlockSpec((N_LRU_BLOCKS, 128, 128), lambda i: (0, 0, 0))],
        out_specs=[pl.BlockSpec((t, D), lambda i: (i, 0)), pl.BlockSpec((t, D), lambda i: (i, 0)),
                   pl.BlockSpec((D, t), lambda i: (0, i))],
        out_shape=[SDS((s, D), BF16), SDS((s, D), F32), SDS((D, s), BF16)],
        scratch_shapes=[pltpu.VMEM((t + 8, D), F32), pltpu.VMEM((t, D), F32), pltpu.VMEM((t, D), F32),
                        pltpu.VMEM((8, D), F32)],
        compiler_params=_cp(1))(proj, proj, lvec, wa, wx)


def t5_bucket_table():
    qi = np.arange(BLOCK)[:, None]
    ki = np.arange(2 * BLOCK)[None, :]
    rel = qi + BLOCK - ki
    relc = np.maximum(rel, 0)
    max_exact = N_BUCKETS // 2
    relf = np.maximum(relc, 1).astype(np.float32)
    large = max_exact + (np.log(relf / np.float32(max_exact)) / np.float32(math.log(MAX_DISTANCE / max_exact))
                         * np.float32(N_BUCKETS - max_exact)).astype(np.int32)
    large = np.minimum(large, N_BUCKETS - 1)
    bucket = np.where(relc < max_exact, relc, large)
    bucket = np.where((rel >= 0) & (rel < BLOCK), bucket, -1)
    return jnp.asarray(bucket.reshape(1, BLOCK * 2 * BLOCK), jnp.int32)


def bias_band(rel_bias_t, buckets):
    n = BLOCK * 2 * BLOCK
    tn = 4096

    def body(bk_ref, rb_ref, o_ref):
        row = lax.broadcasted_iota(jnp.int32, (N_BUCKETS, tn), 0)
        oh = jnp.where(row == bk_ref[...], 1.0, 0.0).astype(BF16)
        rb = rb_ref[...]
        p0 = rb.astype(BF16)
        r1 = rb - p0.astype(F32)
        p1 = r1.astype(BF16)
        p2 = (r1 - p1.astype(F32)).astype(BF16)
        o_ref[...] = _dot(p0, oh) + _dot(p1, oh) + _dot(p2, oh)

    return pl.pallas_call(
        body, name="bias_band", grid=(n // tn,),
        in_specs=[pl.BlockSpec((1, tn), lambda i: (0, i)), pl.BlockSpec((N_HEADS, N_BUCKETS), lambda i: (0, 0))],
        out_specs=pl.BlockSpec((N_HEADS, tn), lambda i: (0, i)),
        out_shape=SDS((N_HEADS, n), F32), compiler_params=_cp(1))(buckets, rel_bias_t)


def bias_band_bwd(dband, buckets):
    n = BLOCK * 2 * BLOCK
    tn = 4096

    def body(bk_ref, d_ref, o_ref):
        @pl.when(pl.program_id(0) == 0)
        def _():
            o_ref[...] = jnp.zeros_like(o_ref)
        row = lax.broadcasted_iota(jnp.int32, (N_BUCKETS, tn), 0)
        oh = jnp.where(row == bk_ref[...], 1.0, 0.0).astype(BF16)
        dv = d_ref[...]
        p0 = dv.astype(BF16)
        r1 = dv - p0.astype(F32)
        p1 = r1.astype(BF16)
        p2 = (r1 - p1.astype(F32)).astype(BF16)
        o_ref[...] += _dot_nt(oh, p0) + _dot_nt(oh, p1) + _dot_nt(oh, p2)

    return pl.pallas_call(
        body, name="bias_band_bwd", grid=(n // tn,),
        in_specs=[pl.BlockSpec((1, tn), lambda i: (0, i)), pl.BlockSpec((N_HEADS, tn), lambda i: (0, i))],
        out_specs=pl.BlockSpec((N_BUCKETS, N_HEADS), lambda i: (0, 0)),
        out_shape=SDS((N_BUCKETS, N_HEADS), F32), compiler_params=_cp(1))(buckets, dband)


def _dup_half(band, which):
    lane = lax.broadcasted_iota(jnp.int32, band.shape, 1)
    rolled = pltpu.roll(band, 64, 1)
    keep = (lane < 64) if which == 0 else (lane >= 64)
    return jnp.where(keep, band, rolled)


def _attn_probs(scores, bias, sink, valid):
    sc = jnp.where(valid, scores * SCALE + bias, NEG_INF)
    m = jnp.maximum(jnp.max(sc, axis=-1, keepdims=True), sink)
    e = jnp.exp(sc - m)
    es = jnp.exp(sink - m)
    inv = 1.0 / (jnp.sum(e, axis=-1, keepdims=True) + es)
    return e * inv, es * inv


def _stack_heads(src_ref, kv, dst):
    lane = lax.broadcasted_iota(jnp.int32, (BLOCK, 128), 1)
    for jj in range(4):
        slab = src_ref[:, (4 * kv + jj) * 128:(4 * kv + jj + 1) * 128]
        for hh in range(2):
            keep = (lane < 64) if hh == 0 else (lane >= 64)
            dst[pl.ds((2 * jj + hh) * BLOCK, BLOCK), :] = jnp.where(keep, slab, jnp.zeros_like(slab))


def _unstack_heads(stacked, dst_ref, kv, dst_t_ref=None):
    lane = lax.broadcasted_iota(jnp.int32, (BLOCK, 128), 1)
    for jj in range(4):
        lo = stacked[(2 * jj) * BLOCK:(2 * jj + 1) * BLOCK]
        hi = stacked[(2 * jj + 1) * BLOCK:(2 * jj + 2) * BLOCK]
        slab = jnp.where(lane < 64, lo, hi).astype(dst_ref.dtype)
        dst_ref[:, (4 * kv + jj) * 128:(4 * kv + jj + 1) * 128] = slab
        if dst_t_ref is not None:
            dst_t_ref[(4 * kv + jj) * 128:(4 * kv + jj + 1) * 128, :] = slab.T


def _band_valid(n):
    qi = lax.broadcasted_iota(jnp.int32, (BLOCK, 2 * BLOCK), 0)
    ki = lax.broadcasted_iota(jnp.int32, (BLOCK, 2 * BLOCK), 1)
    rel = qi + BLOCK - ki
    return (rel >= 0) & (rel < BLOCK) & ((ki >= BLOCK) | (n > 0))


def _kv_bands(prev_ref, cur_ref):
    band = jnp.concatenate([prev_ref[...].astype(F32), cur_ref[...].astype(F32)], axis=0)
    return [_dup_half(band, 0).astype(BF16), _dup_half(band, 1).astype(BF16)]


def attn_fwd(proj, band, sinks):
    s = proj.shape[0]
    nb = s // BLOCK
    qw = 1024

    def body(sk_ref, q_ref, kp_ref, kc_ref, vp_ref, vc_ref, b_ref, o_ref, ot_ref, qs_buf, s_buf, p_buf):
        n = pl.program_id(0)
        gp = pl.program_id(1)
        valid = _band_valid(n)
        kks = _kv_bands(kp_ref, kc_ref)
        vvs = _kv_bands(vp_ref, vc_ref)
        for kv in range(2):
            _stack_heads(q_ref, kv, qs_buf)
            s_buf[...] = _dot_nt(qs_buf[...], kks[kv])
            for hq in range(8):
                hl = 8 * kv + hq
                rows = pl.ds(hq * BLOCK, BLOCK)
                p, _ = _attn_probs(s_buf[rows, :], b_ref[hl], sk_ref[gp * 16 + hl], valid)
                p_buf[rows, :] = p.astype(BF16)
            _unstack_heads(_dot(p_buf[...], vvs[kv]), o_ref, kv, ot_ref)

    kb, vb = OFF_K // 128, OFF_V // 128
    return pl.pallas_call(
        body, name="attn_fwd", grid=(nb, 2),
        in_specs=[pl.BlockSpec(memory_space=pltpu.SMEM),
                  pl.BlockSpec((BLOCK, qw), lambda n, g: (n, OFF_Q // qw + g)),
                  pl.BlockSpec((BLOCK, 128), lambda n, g: (jnp.maximum(n - 1, 0), kb + g)),
                  pl.BlockSpec((BLOCK, 128), lambda n, g: (n, kb + g)),
                  pl.BlockSpec((BLOCK, 128), lambda n, g: (jnp.maximum(n - 1, 0), vb + g)),
                  pl.BlockSpec((BLOCK, 128), lambda n, g: (n, vb + g)),
                  pl.BlockSpec((16, BLOCK, 2 * BLOCK), lambda n, g: (g, 0, 0))],
        out_specs=[pl.BlockSpec((BLOCK, qw), lambda n, g: (n, g)), pl.BlockSpec((qw, BLOCK), lambda n, g: (g, n))],
        out_shape=[SDS((s, D), BF16), SDS((D, s), BF16)],
        scratch_shapes=[pltpu.VMEM((8 * BLOCK, 128), BF16), pltpu.VMEM((8 * BLOCK, 2 * BLOCK), F32),
                        pltpu.VMEM((8 * BLOCK, 2 * BLOCK), BF16)],
        compiler_params=_cp(2))(sinks, proj, proj, proj, proj, proj, band)


def merge_fwd(ya, att, w_lru_out, w_attn_out, proj):
    s = ya.shape[0]
    tm, tn = min(1024, s), 512

    def body(ya_ref, at_ref, wl_ref, wt_ref, ga_ref, gb_ref, yab_ref, mg_ref, mgt_ref):
        y_a = _dot(ya_ref[...], wl_ref[...])
        y_b = _dot(at_ref[...], wt_ref[...])
        yab_ref[0] = y_a.astype(BF16)
        yab_ref[1] = y_b.astype(BF16)
        mg = (_sigmoid(ga_ref[...].astype(F32)) * y_a + _sigmoid(gb_ref[...].astype(F32)) * y_b).astype(BF16)
        mg_ref[...] = mg
        mgt_ref[...] = mg.T

    return pl.pallas_call(
        body, name="merge_fwd", grid=(s // tm, D // tn),
        in_specs=[pl.BlockSpec((tm, D), lambda i, j: (i, 0)), pl.BlockSpec((tm, D), lambda i, j: (i, 0)),
                  pl.BlockSpec((D, tn), lambda i, j: (0, j)), pl.BlockSpec((D, tn), lambda i, j: (0, j)),
                  pl.BlockSpec((tm, tn), lambda i, j: (i, OFF_GA // tn + j)),
                  pl.BlockSpec((tm, tn), lambda i, j: (i, OFF_GB // tn + j))],
        out_specs=[pl.BlockSpec((2, tm, tn), lambda i, j: (0, i, j)), pl.BlockSpec((tm, tn), lambda i, j: (i, j)),
                   pl.BlockSpec((tn, tm), lambda i, j: (j, i))],
        out_shape=[SDS((2, s, D), BF16), SDS((s, D), BF16), SDS((D, s), BF16)],
        compiler_params=_cp(2))(ya, att, w_lru_out, w_attn_out, proj, proj)


def outproj_fwd(merged, w_out, x, vecs):
    s = x.shape[0]
    tm, tn = min(1024, s), 512

    def body(m_ref, w_ref, x_ref, v_ref, x1_ref, o1_ref):
        o1 = _dot(m_ref[...], w_ref[...])
        o1_ref[...] = o1.astype(BF16)
        x1_ref[...] = x_ref[...] + v_ref[V_GATE1:V_GATE1 + 1, :] * o1

    return pl.pallas_call(
        body, name="outproj_fwd", grid=(s // tm, D // tn),
        in_specs=[pl.BlockSpec((tm, D), lambda i, j: (i, 0)), pl.BlockSpec((D, tn), lambda i, j: (0, j)),
                  pl.BlockSpec((tm, tn), lambda i, j: (i, j)), pl.BlockSpec((16, tn), lambda i, j: (0, j))],
        out_specs=[pl.BlockSpec((tm, tn), lambda i, j: (i, j)), pl.BlockSpec((tm, tn), lambda i, j: (i, j))],
        out_shape=[SDS((s, D), F32), SDS((s, D), BF16)],
        compiler_params=_cp(2))(merged, w_out, x, vecs)


def ff1_fwd(x1, vecs, w_ff1):
    s = x1.shape[0]
    tm, tn = min(1024, s), 1024
    per = D // tn

    def body(x_ref, v_ref, w_ref, f_ref, h_ref, fft_ref):
        @pl.when(pl.program_id(1) == 0)
        def _():
            _modulated_norm(x_ref, v_ref, V_G2, V_SCALE2, V_SHIFT2, h_ref, tm)
        fv = _dot(h_ref[...], w_ref[...])
        f_ref[...] = fv.astype(BF16)
        fp = jnp.maximum(fv, 0.0)
        fft_ref[...] = (fp * fp).astype(BF16).T

    return pl.pallas_call(
        body, name="ff1_fwd", grid=(s // tm, D_FF // tn),
        in_specs=[pl.BlockSpec((tm, D), lambda i, j: (i, 0)), pl.BlockSpec((16, D), lambda i, j: (0, 0)),
                  pl.BlockSpec((None, D, tn), lambda i, j: (j // per, 0, j % per))],
        out_specs=[pl.BlockSpec((tm, tn), lambda i, j: (i, j)), pl.BlockSpec((tm, D), lambda i, j: (i, 0)),
                   pl.BlockSpec((tn, tm), lambda i, j: (j, i))],
        out_shape=[SDS((s, D_FF), BF16), SDS((s, D), BF16), SDS((D_FF, s), BF16)],
        compiler_params=_cp(2))(x1, vecs, w_ff1)


def matmul_bf16(name, a, b, tm, tn, b_part=0):
    m, k = a.shape
    n = b.shape[1]

    def body(a_ref, b_ref, o_ref):
        o_ref[...] = _dot(a_ref[...], b_ref[...]).astype(BF16)

    return pl.pallas_call(
        body, name=name, grid=(m // tm, n // tn),
        in_specs=[pl.BlockSpec((tm, k), lambda i, j: (i, 0)), pl.BlockSpec((k, tn), lambda i, j: (b_part, j))],
        out_specs=pl.BlockSpec((tm, tn), lambda i, j: (i, j)),
        out_shape=SDS((m, n), BF16), compiler_params=_cp(2))(a, b)


def ff2_loss(f, w_ff2, x1, tgt, vecs):
    s = x1.shape[0]
    tm, tk = min(512, s), 2048
    nk = D_FF // tk

    def body(f_ref, w_ref, x1_hbm, t_hbm, v_ref, dx2_ref, do2_ref, sums_ref, loss_ref, acc, x1_ref, t_ref, sems):
        i, k = pl.program_id(0), pl.program_id(1)
        fetches = _row_fetches((x1_hbm, t_hbm), (x1_ref, t_ref), sems, i, tm)

        @pl.when((i == 0) & (k == 0))
        def _():
            sums_ref[...] = jnp.zeros_like(sums_ref)
            loss_ref[...] = jnp.zeros_like(loss_ref)

        @pl.when(k == 0)
        def _():
            acc[...] = jnp.zeros_like(acc)
            for cp in fetches:
                cp.start()

        fv = jnp.maximum(f_ref[...].astype(F32), 0.0)
        acc[...] += _dot((fv * fv).astype(BF16), w_ref[...])

        @pl.when(k == nk - 1)
        def _():
            for cp in fetches:
                cp.wait()
            gate2 = v_ref[V_GATE2:V_GATE2 + 1, :]
            g3 = v_ref[V_G3:V_G3 + 1, :]

            def sub(rb, carry):
                rs = pl.ds(pl.multiple_of(rb * SUB, SUB), SUB)
                o2 = acc[rs, :]
                x2 = x1_ref[rs, :] + gate2 * o2
                r3, xh = _rms_parts(x2)
                e = xh * g3 - t_ref[rs, :]
                loss_ref[...] += (0.5 / D) * jnp.sum(e * e)
                dy = e * (1.0 / D)
                sums_ref[0:1, :] += jnp.sum(dy * xh, axis=0, keepdims=True)
                dxh = dy * g3
                dx2 = r3 * (dxh - xh * jnp.mean(dxh * xh, axis=-1, keepdims=True))
                sums_ref[1:2, :] += jnp.sum(dx2 * o2, axis=0, keepdims=True)
                dx2_ref[rs, :] = dx2
                do2_ref[rs, :] = (dx2 * gate2).astype(BF16)
                return carry

            lax.fori_loop(0, tm // SUB, sub, 0)

    return pl.pallas_call(
        body, name="ff2_loss", grid=(s // tm, nk),
        in_specs=[pl.BlockSpec((tm, tk), lambda i, k: (i, k)), pl.BlockSpec((tk, D), lambda i, k: (k, 0)),
                  pl.BlockSpec(memory_space=pl.ANY), pl.BlockSpec(memory_space=pl.ANY),
                  pl.BlockSpec((16, D), lambda i, k: (0, 0))],
        out_specs=[pl.BlockSpec((tm, D), lambda i, k: (i, 0)), pl.BlockSpec((tm, D), lambda i, k: (i, 0)),
                   pl.BlockSpec((8, D), lambda i, k: (0, 0)), pl.BlockSpec((8, 128), lambda i, k: (0, 0))],
        out_shape=[SDS((s, D), F32), SDS((s, D), BF16), SDS((8, D), F32), SDS((8, 128), F32)],
        scratch_shapes=[pltpu.VMEM((tm, D), F32), pltpu.VMEM((tm, D), F32), pltpu.VMEM((tm, D), F32),
                        pltpu.SemaphoreType.DMA((2,))],
        compiler_params=_cp(2))(f, w_ff2, x1, tgt, vecs)


def ff2_bwd(do2, w_ff2, f):
    s = do2.shape[0]
    tm, tn = min(1024, s), 1024

    def body(d_ref, w_ref, f_ref, o_ref):
        dff = _dot_nt(d_ref[...], w_ref[...])
        o_ref[...] = (dff * (2.0 * jnp.maximum(f_ref[...].astype(F32), 0.0))).astype(BF16)

    return pl.pallas_call(
        body, name="ff2_bwd", grid=(s // tm, D_FF // tn),
        in_specs=[pl.BlockSpec((tm, D), lambda i, j: (i, 0)), pl.BlockSpec((tn, D), lambda i, j: (j, 0)),
                  pl.BlockSpec((tm, tn), lambda i, j: (i, j))],
        out_specs=pl.BlockSpec((tm, tn), lambda i, j: (i, j)),
        out_shape=SDS((s, D_FF), BF16), compiler_params=_cp(2))(do2, w_ff2, f)


def weight_grad(name, a, b, tn, out_shape, out_block, out_map):
    s, m = a.shape
    n = b.shape[1]
    tm = WG_TM
    chunk = min(1024, s)
    nch = s // chunk

    def body(a_hbm, b_ref, o_ref, a_buf, at_s, sem):
        i = pl.program_id(0)

        @pl.when(pl.program_id(1) == 0)
        def _():
            def fetch(ch):
                return pltpu.make_async_copy(a_hbm.at[pl.ds(ch * chunk, chunk), pl.ds(i * tm, tm)],
                                             a_buf.at[ch % 2], sem.at[ch % 2])
            fetch(0).start()
            for ch in range(nch):
                if ch + 1 < nch:
                    fetch(ch + 1).start()
                fetch(ch).wait()
                at_s[:, ch * chunk:(ch + 1) * chunk] = a_buf[ch % 2].T

        o_ref[...] = _dot(at_s[...], b_ref[...]).astype(BF16)

    return pl.pallas_call(
        body, name=name, grid=(m // tm, n // tn),
        in_specs=[pl.BlockSpec(memory_space=pl.ANY), pl.BlockSpec((s, tn), lambda i, j: (0, j))],
        out_specs=pl.BlockSpec(out_block, lambda i, j: out_map(i, j)),
        out_shape=SDS(out_shape, BF16),
        scratch_shapes=[pltpu.VMEM((2, chunk, tm), BF16), pltpu.VMEM((tm, s), BF16), pltpu.SemaphoreType.DMA((2,))],
        compiler_params=_cp(2))(a, b)


def ff1_bwd(df, w_ff1, x1, dx2, o1, vecs):
    s = df.shape[0]
    tm, tk = min(512, s), 2048
    nk = D_FF // tk
    per = D // tk

    def body(d_ref, w_ref, x1_hbm, dx2_hbm, o1_hbm, v_ref, dx1_ref, do1_ref, sums_ref, acc, x1_ref, dx2_ref, o1_ref, sems):
        i, k = pl.program_id(0), pl.program_id(1)
        fetches = _row_fetches((x1_hbm, dx2_hbm, o1_hbm), (x1_ref, dx2_ref, o1_ref), sems, i, tm)

        @pl.when((i == 0) & (k == 0))
        def _():
            sums_ref[...] = jnp.zeros_like(sums_ref)

        @pl.when(k == 0)
        def _():
            acc[...] = jnp.zeros_like(acc)
            for cp in fetches:
                cp.start()

        acc[...] += _dot_nt(d_ref[...], w_ref[...])

        @pl.when(k == nk - 1)
        def _():
            for cp in fetches:
                cp.wait()
            g2 = v_ref[V_G2:V_G2 + 1, :]
            scale2 = v_ref[V_SCALE2:V_SCALE2 + 1, :]
            gate1 = v_ref[V_GATE1:V_GATE1 + 1, :]

            def sub(rb, carry):
                rs = pl.ds(pl.multiple_of(rb * SUB, SUB), SUB)
                dh = acc[rs, :]
                r2, xh = _rms_parts(x1_ref[rs, :])
                sums_ref[0:1, :] += jnp.sum(dh, axis=0, keepdims=True)
                sums_ref[1:2, :] += jnp.sum(dh * (xh * g2), axis=0, keepdims=True)
                dxn = dh * (1.0 + scale2)
                sums_ref[2:3, :] += jnp.sum(dxn * xh, axis=0, keepdims=True)
                dxh = dxn * g2
                dx1 = dx2_ref[rs, :] + r2 * (dxh - xh * jnp.mean(dxh * xh, axis=-1, keepdims=True))
                sums_ref[3:4, :] += jnp.sum(dx1 * o1_ref[rs, :].astype(F32), axis=0, keepdims=True)
                dx1_ref[rs, :] = dx1
                do1_ref[rs, :] = (dx1 * gate1).astype(BF16)
                return carry

            lax.fori_loop(0, tm // SUB, sub, 0)

    return pl.pallas_call(
        body, name="ff1_bwd", grid=(s // tm, nk),
        in_specs=[pl.BlockSpec((tm, tk), lambda i, k: (i, k)),
                  pl.BlockSpec((None, D, tk), lambda i, k: (k // per, 0, k % per)),
                  pl.BlockSpec(memory_space=pl.ANY), pl.BlockSpec(memory_space=pl.ANY),
                  pl.BlockSpec(memory_space=pl.ANY), pl.BlockSpec((16, D), lambda i, k: (0, 0))],
        out_specs=[pl.BlockSpec((tm, D), lambda i, k: (i, 0)), pl.BlockSpec((tm, D), lambda i, k: (i, 0)),
                   pl.BlockSpec((8, D), lambda i, k: (0, 0))],
        out_shape=[SDS((s, D), F32), SDS((s, D), BF16), SDS((8, D), F32)],
        scratch_shapes=[pltpu.VMEM((tm, D), F32), pltpu.VMEM((tm, D), F32), pltpu.VMEM((tm, D), F32),
                        pltpu.VMEM((tm, D), BF16), pltpu.SemaphoreType.DMA((3,))],
        compiler_params=_cp(2))(df, w_ff1, x1, dx2, o1, vecs)


def outproj_bwd(do1, w_out, yab, proj):
    s = do1.shape[0]
    tm, tn = min(1024, s), 512
    per = D // tn

    def body(d_ref, w_ref, y_ref, g_ref, dy_ref, dp_ref):
        dm = _dot_nt(d_ref[...], w_ref[...])
        sg = _sigmoid(g_ref[...].astype(F32))
        dy_ref[...] = (dm * sg).astype(BF16)
        dp_ref[...] = (dm * y_ref[...].astype(F32) * sg * (1.0 - sg)).astype(BF16)

    return pl.pallas_call(
        body, name="outproj_bwd", grid=(s // tm, 2 * per),
        in_specs=[pl.BlockSpec((tm, D), lambda i, j: (i, 0)), pl.BlockSpec((tn, D), lambda i, j: (j % per, 0)),
                  pl.BlockSpec((None, tm, tn), lambda i, j: (j // per, i, j % per)),
                  pl.BlockSpec((tm, tn), lambda i, j: (i, OFF_GA // tn + j))],
        out_specs=[pl.BlockSpec((None, tm, tn), lambda i, j: (j // per, i, j % per)),
                   pl.BlockSpec((tm, tn), lambda i, j: (i, OFF_GA // tn + j))],
        out_shape=[SDS((2, s, D), BF16), SDS((s, IN_W), BF16)],
        compiler_params=_cp(2))(do1, w_out, yab, proj)


def lruout_bwd(dyab, w_lru_out, rec, proj, dproj):
    s = rec.shape[0]
    tm, tn = min(1024, s), 512

    def body(d_ref, w_ref, r_ref, g_ref, dp_in, dr_ref, dp_ref):
        dya = _dot_nt(d_ref[...], w_ref[...])
        gate = g_ref[...].astype(F32)
        dr_ref[...] = dya * _gelu(gate)
        dp_ref[...] = (dya * r_ref[...] * _gelu_grad(gate)).astype(BF16)

    return pl.pallas_call(
        body, name="lruout_bwd", grid=(s // tm, D // tn),
        in_specs=[pl.BlockSpec((None, tm, D), lambda i, j: (0, i, 0)), pl.BlockSpec((tn, D), lambda i, j: (j, 0)),
                  pl.BlockSpec((tm, tn), lambda i, j: (i, j)),
                  pl.BlockSpec((tm, tn), lambda i, j: (i, OFF_GATE // tn + j)),
                  pl.BlockSpec(memory_space=pl.ANY)],
        out_specs=[pl.BlockSpec((tm, tn), lambda i, j: (i, j)),
                   pl.BlockSpec((tm, tn), lambda i, j: (i, OFF_GATE // tn + j))],
        out_shape=[SDS((s, D), F32), SDS((s, IN_W), BF16)],
        input_output_aliases={4: 1},
        compiler_params=_cp(2))(dyab, w_lru_out, rec, proj, dproj)


def attnout_bwd(dyab, w_attn_out):
    s = dyab.shape[1]
    tm, tn = min(1024, s), 512

    def body(d_ref, w_ref, o_ref):
        o_ref[...] = _dot_nt(d_ref[...], w_ref[...]).astype(BF16)

    return pl.pallas_call(
        body, name="attnout_bwd", grid=(s // tm, D // tn),
        in_specs=[pl.BlockSpec((None, tm, D), lambda i, j: (1, i, 0)), pl.BlockSpec((tn, D), lambda i, j: (j, 0))],
        out_specs=pl.BlockSpec((tm, tn), lambda i, j: (i, j)),
        out_shape=SDS((s, D), BF16), compiler_params=_cp(2))(dyab, w_attn_out)


def attn_bwd(proj, band, sinks, datt, dproj):
    s = proj.shape[0]
    nb = s // BLOCK
    qw = 1024

    def body(sk_ref, q_ref, kp_ref, kc_ref, vp_ref, vc_ref, b_ref, do_ref, dp_in,
             dq_ref, dkb_ref, dvb_ref, db_ref, ds_ref, qs_buf, dos_buf, s_buf, dp_buf, p_buf, dsc_buf):
        gp = pl.program_id(0)
        n = pl.program_id(1)

        @pl.when(n == 0)
        def _():
            db_ref[...] = jnp.zeros_like(db_ref)
            ds_ref[...] = jnp.zeros_like(ds_ref)

        valid = _band_valid(n)
        kks = _kv_bands(kp_ref, kc_ref)
        vvs = _kv_bands(vp_ref, vc_ref)
        lane_b = lax.broadcasted_iota(jnp.int32, (2 * BLOCK, 128), 1)
        dks, dvs = [], []
        for kv in range(2):
            _stack_heads(q_ref, kv, qs_buf)
            _stack_heads(do_ref, kv, dos_buf)
            s_buf[...] = _dot_nt(qs_buf[...], kks[kv])
            dp_buf[...] = _dot_nt(dos_buf[...], vvs[kv])
            for hq in range(8):
                hl = 8 * kv + hq
                rows = pl.ds(hq * BLOCK, BLOCK)
                p, ps = _attn_probs(s_buf[rows, :], b_ref[hl], sk_ref[gp * 16 + hl], valid)
                dp = dp_buf[rows, :]
                delta = jnp.sum(p * dp, axis=-1, keepdims=True)
                dsc = p * (dp - delta)
                db_ref[hl] += dsc
                ds_ref[hl:hl + 1, :] += jnp.zeros((1, 128), F32) - jnp.sum(ps * delta)
                p_buf[rows, :] = p.astype(BF16)
                dsc_buf[rows, :] = (dsc * SCALE).astype(BF16)
            _unstack_heads(_dot(dsc_buf[...], kks[kv]), dq_ref, kv)
            dk = _dot_tn(dsc_buf[...], qs_buf[...])
            dv = _dot_tn(p_buf[...], dos_buf[...])
            dks.append(dk + pltpu.roll(dk, 64, 1))
            dvs.append(dv + pltpu.roll(dv, 64, 1))
        dkb_ref[...] = jnp.where(lane_b < 64, dks[0], dks[1])
        dvb_ref[...] = jnp.where(lane_b < 64, dvs[0], dvs[1])

    kb, vb = OFF_K // 128, OFF_V // 128
    return pl.pallas_call(
        body, name="attn_bwd", grid=(2, nb),
        in_specs=[pl.BlockSpec(memory_space=pltpu.SMEM),
                  pl.BlockSpec((BLOCK, qw), lambda g, n: (n, OFF_Q // qw + g)),
                  pl.BlockSpec((BLOCK, 128), lambda g, n: (jnp.maximum(n - 1, 0), kb + g)),
                  pl.BlockSpec((BLOCK, 128), lambda g, n: (n, kb + g)),
                  pl.BlockSpec((BLOCK, 128), lambda g, n: (jnp.maximum(n - 1, 0), vb + g)),
                  pl.BlockSpec((BLOCK, 128), lambda g, n: (n, vb + g)),
                  pl.BlockSpec((16, BLOCK, 2 * BLOCK), lambda g, n: (g, 0, 0)),
                  pl.BlockSpec((BLOCK, qw), lambda g, n: (n, g)),
                  pl.BlockSpec(memory_space=pl.ANY)],
        out_specs=[pl.BlockSpec((BLOCK, qw), lambda g, n: (n, OFF_Q // qw + g)),
                   pl.BlockSpec((2 * BLOCK, 128), lambda g, n: (n, g)),
                   pl.BlockSpec((2 * BLOCK, 128), lambda g, n: (n, g)),
                   pl.BlockSpec((16, BLOCK, 2 * BLOCK), lambda g, n: (g, 0, 0)),
                   pl.BlockSpec((16, 128), lambda g, n: (g, 0))],
        out_shape=[SDS((s, IN_W), BF16), SDS((nb * 2 * BLOCK, 256), F32), SDS((nb * 2 * BLOCK, 256), F32),
                   SDS((N_HEADS, BLOCK, 2 * BLOCK), F32), SDS((N_HEADS, 128), F32)],
        input_output_aliases={8: 0},
        scratch_shapes=[pltpu.VMEM((8 * BLOCK, 128), BF16), pltpu.VMEM((8 * BLOCK, 128), BF16),
                        pltpu.VMEM((8 * BLOCK, 2 * BLOCK), F32), pltpu.VMEM((8 * BLOCK, 2 * BLOCK), F32),
                        pltpu.VMEM((8 * BLOCK, 2 * BLOCK), BF16), pltpu.VMEM((8 * BLOCK, 2 * BLOCK), BF16)],
        compiler_params=_cp(2))(sinks, proj, proj, proj, proj, proj, band, datt, dproj)


def dkv_combine(dkb, dvb, dproj):
    nb = dkb.shape[0] // (2 * BLOCK)
    s = nb * BLOCK
    dkb3 = dkb.reshape(nb, 2 * BLOCK, 256)
    dvb3 = dvb.reshape(nb, 2 * BLOCK, 256)

    def body(k1, k2, v1, v2, dp_in, o_ref):
        nxt = jnp.where(pl.program_id(0) < nb - 1, 1.0, 0.0)
        o_ref[:, 0:256] = (k1[...] + nxt * k2[...]).astype(BF16)
        o_ref[:, 256:512] = (v1[...] + nxt * v2[...]).astype(BF16)

    spec1 = pl.BlockSpec((None, BLOCK, 256), lambda m: (m, 1, 0))
    spec2 = pl.BlockSpec((None, BLOCK, 256), lambda m: (jnp.minimum(m + 1, nb - 1), 0, 0))
    return pl.pallas_call(
        body, name="dkv_combine", grid=(nb,),
        in_specs=[spec1, spec2, spec1, spec2, pl.BlockSpec(memory_space=pl.ANY)],
        out_specs=pl.BlockSpec((BLOCK, 512), lambda m: (m, OFF_K // 512)),
        out_shape=SDS((s, IN_W), BF16), input_output_aliases={4: 0},
        compiler_params=_cp(1))(dkb3, dkb3, dvb3, dvb3, dproj)


def lru_bwd(proj, rec, drec, lvec, wa, wx, dproj):
    s = proj.shape[0]
    t = min(256, s)
    nt = s // t

    def body(lx_ref, lxh_ref, rec_ref, rech_ref, dr_ref, lv_ref, wa_ref, wx_ref, dp_in,
             dlx_ref, sums_ref, dwa_ref, dwx_ref,
             xbuf, hbuf, dxbuf, a_s, dh_s, xc_s, r_s, ig_s, mu_s, gc):
        step_i = pl.program_id(0)
        ti = nt - 1 - step_i

        @pl.when(step_i == 0)
        def _():
            sums_ref[...] = jnp.zeros_like(sums_ref)
            dwa_ref[...] = jnp.zeros_like(dwa_ref)
            dwx_ref[...] = jnp.zeros_like(dwx_ref)
            dxbuf[pl.ds(t, 8), :] = jnp.zeros((8, D), F32)
            gc[...] = jnp.zeros((8, D), F32)

        live = jnp.where(ti > 0, 1.0, 0.0)
        xbuf[pl.ds(0, 8), :] = lxh_ref[...].astype(F32)[8:16] * live
        xbuf[pl.ds(8, t), :] = lx_ref[...].astype(F32)
        hbuf[pl.ds(0, 8), :] = rech_ref[...] * live
        hbuf[pl.ds(8, t), :] = rec_ref[...]
        first = (lax.broadcasted_iota(jnp.int32, (t, 128), 0) + ti * t) == 0
        for b in range(N_LRU_BLOCKS):
            cs = slice(b * 128, (b + 1) * 128)
            _, xc, _, r, ig, _, a, mult = _lru_block_fwd(xbuf, lv_ref, wa_ref, wx_ref, b, t, first)
            a_s[:, cs] = a
            xc_s[:, cs] = xc
            r_s[:, cs] = r
            ig_s[:, cs] = ig
            mu_s[:, cs] = mult

        def step(q, g):
            tt = t - 1 - q
            dh = dr_ref[pl.ds(tt, 1), :] + g
            dh_s[pl.ds(tt, 1), :] = dh
            return a_s[pl.ds(tt, 1), :] * dh

        gc[0:1, :] = lax.fori_loop(0, t, step, gc[0:1, :], unroll=8)
        for b in range(N_LRU_BLOCKS):
            cs = slice(b * 128, (b + 1) * 128)
            dh = dh_s[:, cs]
            a = a_s[:, cs]
            xc = xc_s[:, cs]
            r = r_s[:, cs]
            ig = ig_s[:, cs]
            mult = mu_s[:, cs]
            sp = _softplus(-lv_ref[L_LAM:L_LAM + 1, cs])
            lam = lv_ref[L_LAM:L_LAM + 1, cs]
            da = dh * hbuf[pl.ds(7, t), cs]
            dmult = jnp.where(first, 0.0, dh * ig * xc)
            dig = dh * mult * xc
            dxc = dh * mult * ig
            dlog_a = da * a - dmult * (a * a) / mult
            dr = dlog_a * ((-LRU_C) * sp)
            dsp = jnp.sum(dlog_a * ((-LRU_C) * r), axis=0, keepdims=True)
            dza = dr * r * (1.0 - r)
            dzx = dig * ig * (1.0 - ig)
            dzab = dza.astype(BF16)
            dzxb = dzx.astype(BF16)
            xcb = xc.astype(BF16)
            dwa_ref[b] += _dot_tn(xcb, dzab)
            dwx_ref[b] += _dot_tn(xcb, dzxb)
            dxc = dxc + _dot_nt(dzab, wa_ref[b]) + _dot_nt(dzxb, wx_ref[b])
            sums_ref[L_LAM:L_LAM + 1, cs] += dsp * (-jax.nn.sigmoid(-lam))
            sums_ref[L_BA:L_BA + 1, cs] += jnp.sum(dza, axis=0, keepdims=True)
            sums_ref[L_BX:L_BX + 1, cs] += jnp.sum(dzx, axis=0, keepdims=True)
            sums_ref[L_CB:L_CB + 1, cs] += jnp.sum(dxc, axis=0, keepdims=True)
            for kk in range(4):
                sums_ref[kk:kk + 1, cs] += jnp.sum(dxc * xbuf[pl.ds(5 + kk, t), cs], axis=0, keepdims=True)
            dxbuf[pl.ds(0, t), cs] = dxc
            dlx = (lv_ref[3:4, cs] * dxc + lv_ref[2:3, cs] * dxbuf[pl.ds(1, t), cs]
                   + lv_ref[1:2, cs] * dxbuf[pl.ds(2, t), cs] + lv_ref[0:1, cs] * dxbuf[pl.ds(3, t), cs])
            dlx_ref[:, cs] = dlx.astype(BF16)
        dxbuf[pl.ds(t, 8), :] = dxbuf[pl.ds(0, 8), :]

    rev = lambda i: nt - 1 - i
    return pl.pallas_call(
        body, name="lru_bwd", grid=(nt,),
        in_specs=[pl.BlockSpec((t, D), lambda i: (rev(i), 0)),
                  pl.BlockSpec((16, D), lambda i: (jnp.maximum(rev(i) * (t // 16) - 1, 0), 0)),
                  pl.BlockSpec((t, D), lambda i: (rev(i), 0)),
                  pl.BlockSpec((8, D), lambda i: (jnp.maximum(rev(i) * (t // 8) - 1, 0), 0)),
                  pl.BlockSpec((t, D), lambda i: (rev(i), 0)),
                  pl.BlockSpec((8, D), lambda i: (0, 0)),
                  pl.BlockSpec((N_LRU_BLOCKS, 128, 128), lambda i: (0, 0, 0)),
                  pl.BlockSpec((N_LRU_BLOCKS, 128, 128), lambda i: (0, 0, 0)),
                  pl.BlockSpec(memory_space=pl.ANY)],
        out_specs=[pl.BlockSpec((t, D), lambda i: (rev(i), 0)),
                   pl.BlockSpec((8, D), lambda i: (0, 0)),
                   pl.BlockSpec((N_LRU_BLOCKS, 128, 128), lambda i: (0, 0, 0)),
                   pl.BlockSpec((N_LRU_BLOCKS, 128, 128), lambda i: (0, 0, 0))],
        out_shape=[SDS((s, IN_W), BF16), SDS((8, D), F32), SDS((N_LRU_BLOCKS, 128, 128), F32),
                   SDS((N_LRU_BLOCKS, 128, 128), F32)],
        scratch_shapes=[pltpu.VMEM((t + 8, D), F32), pltpu.VMEM((t + 8, D), F32), pltpu.VMEM((t + 8, D), F32)]
        + [pltpu.VMEM((t, D), F32)] * 6 + [pltpu.VMEM((8, D), F32)],
        input_output_aliases={8: 0},
        compiler_params=_cp(1))(proj, proj, rec, rec, drec, lvec, wa, wx, dproj)


def inproj_bwd(dproj, w_in, x, dx1, vecs):
    s = x.shape[0]
    tm, tk = min(512, s), IN_TILE
    nk = IN_W // tk
    per = IN_SHARD // tk

    def body(d_ref, w_ref, x_hbm, dx1_hbm, v_ref, gx_ref, sums_ref, acc, x_ref, dx1_ref, sems):
        i, k = pl.program_id(0), pl.program_id(1)
        fetches = _row_fetches((x_hbm, dx1_hbm), (x_ref, dx1_ref), sems, i, tm)

        @pl.when((i == 0) & (k == 0))
        def _():
            sums_ref[...] = jnp.zeros_like(sums_ref)

        @pl.when(k == 0)
        def _():
            acc[...] = jnp.zeros_like(acc)
            for cp in fetches:
                cp.start()

        acc[...] += _dot_nt(d_ref[...], w_ref[...])

        @pl.when(k == nk - 1)
        def _():
            for cp in fetches:
                cp.wait()
            g1 = v_ref[V_G1:V_G1 + 1, :]
            scale1 = v_ref[V_SCALE1:V_SCALE1 + 1, :]

            def sub(rb, carry):
                rs = pl.ds(pl.multiple_of(rb * SUB, SUB), SUB)
                dh = acc[rs, :]
                r1, xh = _rms_parts(x_ref[rs, :])
                sums_ref[0:1, :] += jnp.sum(dh, axis=0, keepdims=True)
                sums_ref[1:2, :] += jnp.sum(dh * (xh * g1), axis=0, keepdims=True)
                dxn = dh * (1.0 + scale1)
                sums_ref[2:3, :] += jnp.sum(dxn * xh, axis=0, keepdims=True)
                dxh = dxn * g1
                gx_ref[rs, :] = dx1_ref[rs, :] + r1 * (dxh - xh * jnp.mean(dxh * xh, axis=-1, keepdims=True))
                return carry

            lax.fori_loop(0, tm // SUB, sub, 0)

    return pl.pallas_call(
        body, name="inproj_bwd", grid=(s // tm, nk),
        in_specs=[pl.BlockSpec((tm, tk), lambda i, k: (i, k)),
                  pl.BlockSpec((None, D, tk), lambda i, k: (k // per, 0, k % per)),
                  pl.BlockSpec(memory_space=pl.ANY), pl.BlockSpec(memory_space=pl.ANY),
                  pl.BlockSpec((16, D), lambda i, k: (0, 0))],
        out_specs=[pl.BlockSpec((tm, D), lambda i, k: (i, 0)), pl.BlockSpec((8, D), lambda i, k: (0, 0))],
        out_shape=[SDS((s, D), F32), SDS((8, D), F32)],
        scratch_shapes=[pltpu.VMEM((tm, D), F32), pltpu.VMEM((tm, D), F32), pltpu.VMEM((tm, D), F32),
                        pltpu.SemaphoreType.DMA((2,))],
        compiler_params=_cp(2))(dproj, w_in, x, dx1, vecs)


def mod_columns(c16, w_ada, b_cols):
    tn = 512

    def body(c_ref, w_ref, b_ref, o_ref):
        cv = c_ref[...]
        ca = (cv * jax.nn.sigmoid(cv)).astype(BF16)
        o_ref[...] = _dot(ca, w_ref[...].astype(BF16)) + b_ref[...]

    return pl.pallas_call(
        body, name="mod_columns", grid=(ADA_SHARD // tn,),
        in_specs=[pl.BlockSpec((16, D), lambda j: (0, 0)), pl.BlockSpec((D, tn), lambda j: (0, j)),
                  pl.BlockSpec((1, tn), lambda j: (0, j))],
        out_specs=pl.BlockSpec((16, tn), lambda j: (0, j)),
        out_shape=SDS((16, ADA_SHARD), F32), compiler_params=_cp(1))(c16, w_ada, b_cols)


def wada_update(c16, dmod16, w, m, v):
    tm, tn = 512, 512

    def body(c_ref, d_ref, w_ref, m_ref, v_ref, g_out, dl_out, m_out, v_out):
        cv = c_ref[...]
        ca = (cv * jax.nn.sigmoid(cv)).astype(BF16)
        g = _dot_tn(ca, d_ref[...].astype(BF16))
        dl, m2, v2 = _adamw_math(w_ref[...], g, m_ref[...], v_ref[...])
        g_out[...] = g
        dl_out[...] = dl
        m_out[...] = m2
        v_out[...] = v2

    tile = pl.BlockSpec((tm, tn), lambda i, j: (i, j))
    return pl.pallas_call(
        body, name="wada_update", grid=(D // tm, ADA_SHARD // tn),
        in_specs=[pl.BlockSpec((16, tm), lambda i, j: (0, i)), pl.BlockSpec((16, tn), lambda i, j: (0, j)),
                  tile, tile, tile],
        out_specs=[tile] * 4, out_shape=[SDS((D, ADA_SHARD), F32)] * 4,
        compiler_params=_cp(2))(c16, dmod16, w, m, v)


def adamw_big(name, w, mine, theirs, m, v, c_idx):
    r, c = w.shape
    tr = 128
    per = (r // 2) // tr

    def body(c_ref, w_ref, a_ref, b_ref, m_ref, v_ref, g_out, dl_out, m_out, v_out):
        own = (pl.program_id(0) // per) == c_ref[0]
        g = jnp.where(own, a_ref[...], b_ref[...])
        dl, m2, v2 = _adamw_math(w_ref[...], g, m_ref[...], v_ref[...])
        g_out[...] = g
        dl_out[...] = dl
        m_out[...] = m2
        v_out[...] = v2

    tile = pl.BlockSpec((tr, c), lambda i, cr: (i, 0))
    half = pl.BlockSpec((tr, c), lambda i, cr: (i % per, 0))
    gs = pltpu.PrefetchScalarGridSpec(num_scalar_prefetch=1, grid=(r // tr,),
                                      in_specs=[tile, half, half, tile, tile], out_specs=[tile] * 4)
    return pl.pallas_call(body, name=name, grid_spec=gs, out_shape=[SDS((r, c), F32)] * 4,
                          compiler_params=_cp(1))(c_idx, w, mine, theirs, m, v)


def cast_into_slot(name, w, k_idx):
    r, c = w.shape
    tr = 256

    def body(k_ref, w_ref, o_ref):
        o_ref[...] = w_ref[...].astype(BF16)

    gs = pltpu.PrefetchScalarGridSpec(
        num_scalar_prefetch=1, grid=(r // tr,),
        in_specs=[pl.BlockSpec((tr, c), lambda i, kr: (i, 0))],
        out_specs=pl.BlockSpec((None, tr, c), lambda i, kr: (kr[0], i, 0)))
    return pl.pallas_call(body, name=name, grid_spec=gs, out_shape=SDS((N_CHIPS, r, c), BF16),
                          compiler_params=_cp(1))(k_idx, w)


def adamw_small(ws, gs, ms, vs):
    n = len(ws)

    def body(*refs):
        for i in range(n):
            dl, m2, v2 = _adamw_math(refs[i][...], refs[n + i][...], refs[2 * n + i][...], refs[3 * n + i][...])
            refs[4 * n + i][...] = dl
            refs[5 * n + i][...] = m2
            refs[6 * n + i][...] = v2

    vm = pl.BlockSpec(memory_space=pltpu.VMEM)
    shapes = [SDS(w.shape, F32) for w in ws]
    outs = pl.pallas_call(
        body, name="adamw_small", in_specs=[vm] * (4 * n), out_specs=[vm] * (3 * n), out_shape=shapes * 3,
        compiler_params=pltpu.CompilerParams(vmem_limit_bytes=VMEM_LIMIT))(*ws, *gs, *ms, *vs)
    return outs[:n], outs[n:2 * n], outs[2 * n:]


def sum_devices(name, gathered):
    rows, cols = gathered.shape[1:]
    tr = min(rows, 128 * D // cols)

    def body(x_ref, o_ref):
        acc = x_ref[0].astype(F32)
        for d in range(1, N_DEV):
            acc = acc + x_ref[d].astype(F32)
        o_ref[...] = acc

    return pl.pallas_call(
        body, name=name, grid=(rows // tr,),
        in_specs=[pl.BlockSpec((N_DEV, tr, cols), lambda i: (0, i, 0))],
        out_specs=pl.BlockSpec((tr, cols), lambda i: (i, 0)),
        out_shape=SDS((rows, cols), F32), compiler_params=_cp(1))(gathered)


def _mesh_pos():
    return lax.axis_index("x"), lax.axis_index("y"), lax.axis_index("c")


def _other_chips(x, y):
    return [(1 - x, y), (x, 1 - y), (1 - x, 1 - y)]


def all_gather_small(name, block):
    m_per, n = block.shape

    def body(x_ref, out_ref, send_sems, recv_sems, local_sem):
        x, y, c = _mesh_pos()
        me, sibling = (x, y, c), (x, y, 1 - c)
        chips = _other_chips(x, y)

        def rows(px, py, pc):
            return out_ref.at[pl.ds((4 * px + 2 * py + pc) * m_per, m_per), :]

        def copy(k, blk, to, src=None):
            return pltpu.make_async_remote_copy(
                src_ref=rows(*blk) if src is None else src, dst_ref=rows(*blk),
                send_sem=send_sems.at[k], recv_sem=recv_sems.at[k], device_id=to, device_id_type=MESH)

        mine = pltpu.make_async_copy(x_ref, rows(*me), local_sem)
        mine.start()
        first = [copy(0, me, sibling, src=x_ref)]
        first += [copy(1 + j, me, (*chip, c), src=x_ref) for j, chip in enumerate(chips)]
        for cp in first:
            cp.start()
        passed = [copy(4 + j, (*chip, c), sibling) for j, chip in enumerate(chips)]
        for j, chip in enumerate(chips):
            copy(1 + j, (*chip, c), me).wait_recv()
            passed[j].start()
        copy(0, sibling, me).wait_recv()
        for j, chip in enumerate(chips):
            copy(4 + j, (*chip, 1 - c), me).wait_recv()
        for cp in first + passed:
            cp.wait_send()
        mine.wait()

    vm = pl.BlockSpec(memory_space=pltpu.VMEM)
    return pl.pallas_call(
        body, name=name, out_shape=SDS((N_DEV * m_per, n), block.dtype), in_specs=[vm], out_specs=vm,
        scratch_shapes=[pltpu.SemaphoreType.DMA((7,)), pltpu.SemaphoreType.DMA((7,)), pltpu.SemaphoreType.DMA],
        compiler_params=pltpu.CompilerParams(vmem_limit_bytes=VMEM_LIMIT))(block)


def sibling_sum(name, grad, other, c_idx):
    _, r, cc = grad.shape
    h = r // 2
    tr = min(256, h)
    g4 = grad.reshape(N_CHIPS, 2, h, cc)

    def body(c_ref, a_ref, b_ref, o_ref):
        o_ref[...] = (a_ref[...].astype(F32) + b_ref[...].astype(F32)).astype(BF16)

    gs = pltpu.PrefetchScalarGridSpec(
        num_scalar_prefetch=1, grid=(N_CHIPS, h // tr),
        in_specs=[pl.BlockSpec((None, None, tr, cc), lambda s, i, cr: (s, cr[0], i, 0)),
                  pl.BlockSpec((None, tr, cc), lambda s, i, cr: (s, i, 0))],
        out_specs=pl.BlockSpec((None, tr, cc), lambda s, i, cr: (s, i, 0)))
    return pl.pallas_call(body, name=name, grid_spec=gs, out_shape=SDS((N_CHIPS, h, cc), BF16),
                          compiler_params=_cp(2))(c_idx, g4, other)


HBM_SPEC = pl.BlockSpec(memory_space=pltpu.HBM)
SEM_SPEC = pl.BlockSpec(memory_space=pltpu.SEMAPHORE)


def _side_effecting():
    return pltpu.CompilerParams(has_side_effects=pltpu.SideEffectType.DATAFLOW_SIDE_EFFECTING)


def _in_hbm(a):
    return pltpu.with_memory_space_constraint(a, pltpu.HBM)


ALL_CHIPS = (0, 1, 2)


def gather_start(name, bufs, after, rel=ALL_CHIPS, carry=None):
    n = len(bufs)
    nr = len(rel)
    halves = [w.shape[1] // 2 for w in bufs]
    extra = [] if carry is None else [carry]

    def body(*refs):
        ins = refs[:n]
        send_sems, recv_sems, token = refs[n + 1 + len(extra)], refs[n + 2 + len(extra)], refs[-1]
        x, y, c = _mesh_pos()
        k = 2 * x + y
        for i in range(n):
            reg = ins[i].at[k, pl.ds(c * halves[i], halves[i]), :]
            for q, j in enumerate(rel):
                chip = _other_chips(x, y)[j]
                pltpu.make_async_remote_copy(src_ref=reg, dst_ref=reg, send_sem=send_sems.at[nr * i + q],
                                             recv_sem=recv_sems.at[nr * i + q], device_id=(*chip, c),
                                             device_id_type=MESH).start()
        token[...] = jnp.zeros_like(token)

    outs = pl.pallas_call(
        body, name=name,
        out_shape=(pltpu.SemaphoreType.DMA((nr * n,)), pltpu.SemaphoreType.DMA((nr * n,)),
                   *[pltpu.HBM(w.shape, w.dtype) for w in list(bufs) + extra], SDS((8, 128), F32)),
        in_specs=[HBM_SPEC] * n + [pl.BlockSpec(memory_space=pl.ANY)] + [HBM_SPEC] * len(extra),
        out_specs=(SEM_SPEC, SEM_SPEC, *[HBM_SPEC] * (n + len(extra)), pl.BlockSpec(memory_space=pltpu.VMEM)),
        input_output_aliases={**{i: 2 + i for i in range(n)}, **({n + 1: 2 + n} if extra else {})},
        compiler_params=_side_effecting())(*[_in_hbm(w) for w in bufs], after, *[_in_hbm(w) for w in extra])
    return (outs[0], outs[1], list(outs[2:2 + n]), outs[-1]) + ((outs[2 + n],) if extra else ())


def gather_wait(name, send_sems, recv_sems, bufs, after, rel=ALL_CHIPS):
    n = len(bufs)
    nr = len(rel)
    halves = [w.shape[1] // 2 for w in bufs]

    def body(*refs):
        ins = refs[:n]
        send_sems, recv_sems = refs[n], refs[n + 1]
        x, y, c = _mesh_pos()
        k = 2 * x + y
        for i in range(n):
            for q, j in enumerate(rel):
                chip = _other_chips(x, y)[j]
                kj = 2 * chip[0] + chip[1]
                cp = pltpu.make_async_remote_copy(
                    src_ref=ins[i].at[k, pl.ds(c * halves[i], halves[i]), :],
                    dst_ref=ins[i].at[kj, pl.ds(c * halves[i], halves[i]), :],
                    send_sem=send_sems.at[nr * i + q], recv_sem=recv_sems.at[nr * i + q], device_id=(*chip, c),
                    device_id_type=MESH)
                cp.wait_send()
                cp.wait_recv()

    return pl.pallas_call(
        body, name=name, out_shape=[pltpu.HBM(w.shape, w.dtype) for w in bufs],
        in_specs=[HBM_SPEC] * n + [SEM_SPEC, SEM_SPEC, pl.BlockSpec(memory_space=pl.ANY)],
        out_specs=[HBM_SPEC] * n, input_output_aliases={i: i for i in range(n)},
        compiler_params=_side_effecting())(*bufs, send_sems, recv_sems, after)


def gather_forward(name, bufs, rel=ALL_CHIPS):
    n = len(bufs)
    halves = [w.shape[1] // 2 for w in bufs]

    def body(*refs):
        outs = refs[n:2 * n]
        send_sems, recv_sems = refs[2 * n:]
        x, y, c = _mesh_pos()
        chips = _other_chips(x, y)

        def copy(i, j, half, to):
            kj = 2 * chips[j][0] + chips[j][1]
            reg = outs[i].at[kj, pl.ds(half * halves[i], halves[i]), :]
            return pltpu.make_async_remote_copy(src_ref=reg, dst_ref=reg, send_sem=send_sems.at[i, j],
                                                recv_sem=recv_sems.at[i, j], device_id=to, device_id_type=MESH)

        cps = [copy(i, j, c, (x, y, 1 - c)) for i in range(n) for j in rel]
        for cp in cps:
            cp.start()
        for i in range(n):
            for j in rel:
                copy(i, j, 1 - c, (x, y, c)).wait_recv()
        for cp in cps:
            cp.wait_send()

    hbm = pl.BlockSpec(memory_space=pl.ANY)
    return pl.pallas_call(
        body, name=name, in_specs=[hbm] * n, out_specs=[hbm] * n,
        out_shape=[SDS(w.shape, w.dtype) for w in bufs], input_output_aliases={i: i for i in range(n)},
        scratch_shapes=[pltpu.SemaphoreType.DMA((n, 3)), pltpu.SemaphoreType.DMA((n, 3))])(*bufs)


def _exchange_plan(kind, srcs, zones):
    x, y, c = _mesh_pos()
    plan = []
    for src, zone in zip(srcs, zones):
        if kind == "chips":
            for j, chip in enumerate(_other_chips(x, y)):
                plan.append((src.at[2 * chip[0] + chip[1]], zone.at[j], (*chip, c)))
        elif kind == "sibling":
            h = zone.shape[1]
            plan.append((src.at[:, pl.ds((1 - c) * h, h), :], zone, (x, y, 1 - c)))
        else:
            peers = [(x, y, 1 - c)] + [(*chip, cc) for chip in _other_chips(x, y) for cc in (c, 1 - c)]
            plan += [(src, zone.at[4 * x + 2 * y + c], peer) for peer in peers]
    return plan


_COPIES_PER_ARRAY = {"chips": 3, "sibling": 1, "all": N_DEV - 1}


def _landing_zones(kind, srcs):
    if kind == "chips":
        return [lax.empty((3,) + t.shape[1:], t.dtype) for t in srcs]
    if kind == "sibling":
        return [lax.empty((t.shape[0], t.shape[1] // 2, t.shape[2]), t.dtype) for t in srcs]
    return [jnp.broadcast_to(t, (N_DEV,) + t.shape) for t in srcs]


def exchange_start(name, kind, srcs, after):
    n = len(srcs)
    lands = _landing_zones(kind, srcs)
    n_copies = n * _COPIES_PER_ARRAY[kind]

    def body(*refs):
        send_sems, recv_sems, token = refs[2 * n + 1], refs[2 * n + 2], refs[-1]
        for q, (src, dst, dev) in enumerate(_exchange_plan(kind, refs[:n], refs[n:2 * n])):
            pltpu.make_async_remote_copy(src_ref=src, dst_ref=dst, send_sem=send_sems.at[q], recv_sem=recv_sems.at[q],
                                         device_id=dev, device_id_type=MESH).start()
        token[...] = jnp.zeros_like(token)

    outs = pl.pallas_call(
        body, name=name,
        out_shape=(pltpu.SemaphoreType.DMA((n_copies,)), pltpu.SemaphoreType.DMA((n_copies,)),
                   *[pltpu.HBM(t.shape, t.dtype) for t in srcs], *[pltpu.HBM(t.shape, t.dtype) for t in lands],
                   SDS((8, 128), F32)),
        in_specs=[HBM_SPEC] * (2 * n) + [pl.BlockSpec(memory_space=pl.ANY)],
        out_specs=(SEM_SPEC, SEM_SPEC, *[HBM_SPEC] * (2 * n), pl.BlockSpec(memory_space=pltpu.VMEM)),
        input_output_aliases={i: 2 + i for i in range(2 * n)},
        compiler_params=_side_effecting())(*[_in_hbm(t) for t in srcs], *[_in_hbm(t) for t in lands], after)
    return outs[0], outs[1], list(outs[2:2 + n]), list(outs[2 + n:2 + 2 * n]), outs[-1]


def exchange_wait(name, kind, send_sems, recv_sems, srcs, lands, after):
    n = len(srcs)

    def body(*refs):
        send_sems, recv_sems = refs[2 * n], refs[2 * n + 1]
        for q, (src, dst, dev) in enumerate(_exchange_plan(kind, refs[:n], refs[n:2 * n])):
            cp = pltpu.make_async_remote_copy(src_ref=src, dst_ref=dst, send_sem=send_sems.at[q],
                                              recv_sem=recv_sems.at[q], device_id=dev, device_id_type=MESH)
            cp.wait_send()
            cp.wait_recv()

    outs = pl.pallas_call(
        body, name=name, out_shape=[pltpu.HBM(t.shape, t.dtype) for t in srcs + lands],
        in_specs=[HBM_SPEC] * (2 * n) + [SEM_SPEC, SEM_SPEC, pl.BlockSpec(memory_space=pl.ANY)],
        out_specs=[HBM_SPEC] * (2 * n), input_output_aliases={i: i for i in range(2 * n)},
        compiler_params=_side_effecting())(*srcs, *lands, send_sems, recv_sems, after)
    return list(outs[:n]), list(outs[n:])


def chip_sum(name, sums, parts, k_idx):
    _, h, cc = parts.shape
    tr = min(256, h)

    def body(k_ref, own_ref, p_ref, o_ref):
        acc = own_ref[...].astype(F32)
        for s in range(3):
            acc = acc + p_ref[s].astype(F32)
        o_ref[...] = acc

    gs = pltpu.PrefetchScalarGridSpec(
        num_scalar_prefetch=1, grid=(h // tr,),
        in_specs=[pl.BlockSpec((None, tr, cc), lambda i, kr: (kr[0], i, 0)),
                  pl.BlockSpec((3, tr, cc), lambda i, kr: (0, i, 0))],
        out_specs=pl.BlockSpec((tr, cc), lambda i, kr: (i, 0)))
    return pl.pallas_call(body, name=name, grid_spec=gs, out_shape=SDS((h, cc), F32),
                          compiler_params=_cp(1))(k_idx, sums, parts)


def halves_exchange(name, halves):
    n = len(halves)

    def body(*refs):
        ins, outs = refs[:n], refs[n:2 * n]
        send_sems, recv_sems = refs[2 * n:]
        x, y, c = _mesh_pos()
        cps = []
        for i in range(n):
            cp = pltpu.make_async_remote_copy(
                src_ref=ins[i], dst_ref=outs[i], send_sem=send_sems.at[i], recv_sem=recv_sems.at[i],
                device_id=(x, y, 1 - c), device_id_type=MESH)
            cp.start()
            cps.append(cp)
        for cp in cps:
            cp.wait_recv()
        for cp in cps:
            cp.wait_send()

    hbm = pl.BlockSpec(memory_space=pl.ANY)
    return pl.pallas_call(
        body, name=name, in_specs=[hbm] * n, out_specs=[hbm] * n,
        out_shape=[SDS(t.shape, F32) for t in halves],
        scratch_shapes=[pltpu.SemaphoreType.DMA((n,)), pltpu.SemaphoreType.DMA((n,))])(*halves)


def local_step(x, tgt, vecs, lvec, wa, wx, sinks, rel_bias, proj, h, w_in, rest_weights, hook):
    buckets = t5_bucket_table()
    band = bias_band(rel_bias.T, buckets).reshape(N_HEADS, BLOCK, 2 * BLOCK)

    ya, rec, ya_t = lru_fwd(proj, lvec, wa, wx)
    att, att_t = attn_fwd(proj, band, sinks)
    w_lru_out, w_attn_out, w_out = rest_weights("mix", att[:8, :128] + ya[:8, :128])
    w_lru_out2, w_attn_out2, w_out2 = w_lru_out.reshape(D, D), w_attn_out.reshape(D, D), w_out.reshape(D, D)
    yab, merged, merged_t = merge_fwd(ya, att, w_lru_out2, w_attn_out2, proj)
    x1, o1 = outproj_fwd(merged, w_out2, x, vecs)
    w_ff1, w_ff2 = rest_weights("ff", o1[:8, :128])
    w_ff2_2 = w_ff2.reshape(D_FF, D)
    f, h2, fft = ff1_fwd(x1, vecs, w_ff1)
    dx2, do2, sums_f, loss = ff2_loss(f, w_ff2_2, x1, tgt, vecs)

    df = ff2_bwd(do2, w_ff2_2, f)
    g_ff2 = matmul_bf16("dw_ff2", fft, do2, WG_TM, 512)
    dx1, do1, sums_2 = ff1_bwd(df, w_ff1, x1, dx2, o1, vecs)
    g_ff1 = weight_grad("dw_ff1", h2, df, 1024, (N_CHIPS, D, D), (None, WG_TM, 1024), lambda i, j: (j // 2, i, j % 2))
    dyab, dproj = outproj_bwd(do1, w_out2, yab, proj)
    g_out = matmul_bf16("dw_out", merged_t, do1, WG_TM, 512)
    drec, dproj = lruout_bwd(dyab, w_lru_out2, rec, proj, dproj)
    dyab2 = dyab.reshape(2 * x.shape[0], D)
    g_lru_out = matmul_bf16("dw_lru_out", ya_t, dyab2, WG_TM, 512)
    datt = attnout_bwd(dyab, w_attn_out2)
    g_attn_out = matmul_bf16("dw_attn_out", att_t, dyab2, WG_TM, 512, b_part=1)
    zero = hook("grads_a", [g_lru_out.reshape(N_CHIPS, D // 4, D), g_attn_out.reshape(N_CHIPS, D // 4, D),
                            g_out.reshape(N_CHIPS, D // 4, D), g_ff1, g_ff2.reshape(N_CHIPS, D_FF // 4, D)])
    dproj, dkb, dvb, dband, dsink = attn_bwd(proj, band, sinks + zero, datt, dproj)
    zero = hook("after_attn_bwd", dkb)
    dproj = dkv_combine(dkb, dvb, dproj)
    dproj, sums_l, d_wa, d_wx = lru_bwd(proj, rec, drec, lvec + zero, wa, wx, dproj)
    hook("lru_grads", (d_wa, d_wx))
    per = IN_SHARD // IN_TILE
    g_in = weight_grad("dw_in", h, dproj, IN_TILE, (N_CHIPS, D, IN_SHARD), (None, WG_TM, IN_TILE),
                       lambda i, j: (j // per, i, j % per))
    zero = hook("grads_b", [g_in])
    grad_x, sums_1 = inproj_bwd(dproj, w_in, x, dx1, vecs + zero)
    d_rel_bias = bias_band_bwd(dband.reshape(N_HEADS, BLOCK * 2 * BLOCK), buckets)

    small = dict(sums_f=sums_f, sums_2=sums_2, sums_1=sums_1, sums_l=sums_l, d_wa=d_wa, d_wx=d_wx,
                 d_sinks=dsink[:, 0], d_rel_bias=d_rel_bias)
    return loss, grad_x, small


def _pad_rows(a, rows):
    return jnp.concatenate([a, jnp.zeros((rows - a.shape[0], a.shape[1]), a.dtype)], axis=0)


def kernel(x, c, w_ada, b_ada, norm1_g, w_in, conv_w, conv_b, lru_wa, lru_ba, lru_wx, lru_bx, lru_lambda, w_lru_out, w_attn_out, attn_sinks, rel_bias, w_out, norm2_g, w_ff1, w_ff2, final_g, loss_target, m_w_ada, m_b_ada, m_norm1_g, m_w_in, m_conv_w, m_conv_b, m_lru_wa, m_lru_ba, m_lru_wx, m_lru_bx, m_lru_lambda, m_w_lru_out, m_w_attn_out, m_attn_sinks, m_rel_bias, m_w_out, m_norm2_g, m_w_ff1, m_w_ff2, m_final_g, v_w_ada, v_b_ada, v_norm1_g, v_w_in, v_conv_w, v_conv_b, v_lru_wa, v_lru_ba, v_lru_wx, v_lru_bx, v_lru_lambda, v_w_lru_out, v_w_attn_out, v_attn_sinks, v_rel_bias, v_w_out, v_norm2_g, v_w_ff1, v_w_ff2, v_final_g):
    xi, yi, ci = _mesh_pos()
    chip = 2 * xi + yi
    dev = 2 * chip + ci
    z8 = jnp.zeros((8, D), F32)

    conv_rows = jnp.concatenate([conv_w[0], jnp.zeros((4, D - D // 4), F32)], axis=1)
    pack0 = jnp.concatenate([c, conv_rows, jnp.zeros((3, D), F32)], axis=0)
    g0 = all_gather_small("gather_cond", pack0).reshape(N_DEV, 8, D)
    c_all = g0[:, 0, :]
    conv_full = jnp.concatenate([g0[2 * k, 1:5, :D // 4] for k in range(N_CHIPS)], axis=1)
    c16 = jnp.concatenate([c_all, z8], axis=0)
    b_cols = lax.dynamic_slice_in_dim(b_ada, chip * ADA_SHARD, ADA_SHARD, axis=1)
    mod_c = mod_columns(c16, w_ada[0], b_cols)
    g1 = all_gather_small("gather_mod", mod_c).reshape(N_DEV, 16, ADA_SHARD)
    mod = jnp.concatenate([lax.dynamic_index_in_dim(g1[2 * k], dev, axis=0, keepdims=False) for k in range(N_CHIPS)])
    shift1, scale1, gate1, shift2, scale2, gate2 = [mod[i * D:(i + 1) * D] for i in range(6)]
    vecs = jnp.stack([norm1_g[0], scale1, shift1, gate1, norm2_g[0], scale2, shift2, gate2, final_g]
                     + [jnp.zeros((D,), F32)] * 7)
    lvec = jnp.concatenate([conv_full, conv_b, lru_ba, lru_bx, lru_lambda], axis=0)

    shards = [w_in[0], w_lru_out[0], w_attn_out[0], w_out[0], w_ff1[0], w_ff2[0]]
    names = ["w_in", "w_lru_out", "w_attn_out", "w_out", "w_ff1", "w_ff2"]
    k_idx = jnp.reshape(chip, (1,)).astype(jnp.int32)
    c_idx = jnp.reshape(ci, (1,)).astype(jnp.int32)
    near, far = (0, 1), (2,)
    shard_of = lambda flip: jnp.reshape(chip ^ flip, (1,)).astype(jnp.int32)
    x2d = x[0]
    n_send, n_recv, w_in_buf, _ = gather_start(
        "gather_start_in_near", [cast_into_slot("cast_w_in", shards[0], k_idx)], vecs, near)
    proj, h = inproj_fwd("inproj_fwd_own", x2d, None, vecs, w_in_buf[0], None, k_idx)
    slots = [cast_into_slot("cast_" + nm, w, k_idx) for nm, w in zip(names[1:], shards[1:])]
    w_in_buf = gather_forward("gather_forward_in_near", gather_wait(
        "gather_wait_in_near", n_send, n_recv, w_in_buf, proj[:8, :128] + slots[-1][0, :8, :128], near), near)
    f_send, f_recv, w_in_buf, _ = gather_start("gather_start_in_far", w_in_buf, proj[:8, :128], far)
    in_flight = {"mix": gather_start("gather_start_mix", slots[:3], proj[:8, :128])}
    in_flight["ff"] = gather_start("gather_start_ff", slots[3:], in_flight["mix"][3], carry=w_in_buf[0])
    w_in_buf = [in_flight["ff"][4]]
    proj = inproj_fwd("inproj_fwd_x", x2d, h, vecs, w_in_buf[0], proj, shard_of(2))
    proj = inproj_fwd("inproj_fwd_y", x2d, h, vecs, w_in_buf[0], proj, shard_of(1))
    w_in_buf = gather_forward("gather_forward_in_far", gather_wait(
        "gather_wait_in_far", f_send, f_recv, w_in_buf, proj[:8, :128], far), far)
    proj = inproj_fwd("inproj_fwd_d", x2d, h, vecs, w_in_buf[0], proj, shard_of(3))
    w_in_full = w_in_buf[0]
    pending = {}

    def rest_weights(group, after):
        send_sems, recv_sems, bufs = in_flight[group][:3]
        return gather_forward("gather_forward_" + group,
                              gather_wait("gather_wait_" + group, send_sems, recv_sems, bufs, after))

    def reduce_hook(event, payload):
        if event == "grads_a":
            pending["sib_a"] = exchange_start("sibling_start_a", "sibling", payload, payload[0])
            return pending["sib_a"][-1][0, 0]
        if event == "lru_grads":
            pack_w = jnp.concatenate([payload[0].reshape(D, 128), payload[1].reshape(D, 128)], axis=0).astype(BF16)
            pending["lru_w"] = exchange_start("lru_w_grads_start", "all", [pack_w], pack_w)
            return pending["lru_w"][-1][0, 0]
        if event == "grads_b":
            pending["sib_b"] = exchange_start("sibling_start_b", "sibling", payload, pending["lru_w"][-1])
            return pending["sib_b"][-1][0, 0]
        return chips_start("a", names[1:], payload)

    def chips_start(tag, nms, after):
        send_sems, recv_sems, grads, lands, _ = pending["sib_" + tag]
        grads, lands = exchange_wait("sibling_wait_" + tag, "sibling", send_sems, recv_sems, grads, lands, after)
        sums = [sibling_sum("sibling_sum_" + nm, g, o, c_idx) for nm, g, o in zip(nms, grads, lands)]
        pending[tag] = exchange_start("exchange_start_" + tag, "chips", sums, sums[0])
        return pending[tag][-1][0, 0]

    loss_t, grad_x, small = local_step(
        x2d, loss_target[0], vecs, lvec, lru_wa[0].astype(BF16), lru_wx[0].astype(BF16),
        attn_sinks[0], rel_bias, proj, h, w_in_full, rest_weights, reduce_hook)
    chips_start("b", names[:1], grad_x)

    big_m = dict(zip(names, [m_w_in, m_w_lru_out, m_w_attn_out, m_w_out, m_w_ff1, m_w_ff2]))
    big_v = dict(zip(names, [v_w_in, v_w_lru_out, v_w_attn_out, v_w_out, v_w_ff1, v_w_ff2]))
    local_w = dict(zip(names, shards))
    g_big, d_big, nm_big, nv_big = {}, {}, {}, {}

    def links_done(tag, after):
        send_sems, recv_sems, sums, lands, _ = pending[tag]
        return exchange_wait("exchange_wait_" + tag, "chips", send_sems, recv_sems, sums, lands, after)

    def finish_reduce(nms, sums, lands):
        mine = [chip_sum("chip_sum_" + nm, t, p, k_idx) for nm, t, p in zip(nms, sums, lands)]
        theirs = halves_exchange("halves_exchange_" + nms[0], mine)
        for nm, a, b in zip(nms, mine, theirs):
            g2, dl, m2, v2 = adamw_big("adamw_" + nm, local_w[nm], a, b, big_m[nm][0], big_v[nm][0], c_idx)
            g_big[nm], d_big[nm], nm_big[nm], nv_big[nm] = g2[None], dl[None], m2[None], v2[None]
        return lax.optimization_barrier(tuple(nv_big[nm] for nm in nms))[0]

    done_a = finish_reduce(names[1:], *links_done("a", pending["b"][-1]))
    sums_b, lands_b = links_done("b", done_a)
    w_send, w_recv, w_src, w_lands, _ = pending["lru_w"]
    w_src, w_lands = exchange_wait("lru_w_grads_wait", "all", w_send, w_recv, w_src, w_lands, lands_b[0])

    sums_f, sums_2, sums_1, sums_l = small["sums_f"], small["sums_2"], small["sums_1"], small["sums_l"]
    vec_rows = jnp.stack([sums_1[2], sums_2[2], sums_f[0], sums_l[L_CB], sums_l[L_BA], sums_l[L_BX],
                          sums_l[L_LAM], jnp.zeros((D,), F32)])
    mod_rows = jnp.stack([sums_1[0], sums_1[1], sums_2[3], sums_2[0], sums_2[1], sums_f[1],
                          jnp.zeros((D,), F32), jnp.zeros((D,), F32)])
    att_rows = jnp.concatenate([
        jnp.concatenate([small["d_sinks"], jnp.zeros((D - N_HEADS,), F32)])[None],
        jnp.concatenate([small["d_rel_bias"].reshape(-1), jnp.zeros((D - N_BUCKETS * N_HEADS,), F32)])[None],
        jnp.zeros((6, D), F32)], axis=0)
    pack = jnp.concatenate([vec_rows, _pad_rows(sums_l[0:4], 8), mod_rows, att_rows], axis=0)
    pack, lru_w_all = lax.optimization_barrier((pack, w_lands[0]))
    gathered = all_gather_small("gather_small_grads", pack).reshape(N_DEV, P_WA, D)
    total = sum_devices("sum_small_grads", gathered)
    total_w = sum_devices("sum_lru_w_grads", lru_w_all)
    dmod_all = gathered[:, P_MOD:P_MOD + 6, :].reshape(N_DEV, 6 * D)
    dmod16 = jnp.concatenate([lax.dynamic_slice_in_dim(dmod_all, chip * ADA_SHARD, ADA_SHARD, axis=1),
                              jnp.zeros((8, ADA_SHARD), F32)], axis=0)
    g_w_ada, d_w_ada, nm_w_ada, nv_w_ada = wada_update(c16, dmod16, w_ada[0], m_w_ada[0], v_w_ada[0])
    finish_reduce(names[:1], sums_b, lands_b)
    loss = lax.psum(lax.optimization_barrier((loss_t, total))[0][0, 0], ("x", "y", "c"))

    conv_g = lax.dynamic_slice_in_dim(total[P_CONVW:P_CONVW + 4], chip * (D // 4), D // 4, axis=1)
    sm_names = ["b_ada", "norm1_g", "conv_w", "conv_b", "lru_wa", "lru_ba", "lru_wx", "lru_bx", "lru_lambda",
                "attn_sinks", "rel_bias", "norm2_g", "final_g"]
    sm_w = [b_ada.reshape(6, D), norm1_g, conv_w[0], conv_b, lru_wa.reshape(D, 128), lru_ba, lru_wx.reshape(D, 128),
            lru_bx, lru_lambda, attn_sinks, rel_bias, norm2_g, final_g[None]]
    sm_m = [m_b_ada.reshape(6, D), m_norm1_g, m_conv_w[0], m_conv_b, m_lru_wa.reshape(D, 128), m_lru_ba,
            m_lru_wx.reshape(D, 128), m_lru_bx, m_lru_lambda, m_attn_sinks, m_rel_bias, m_norm2_g, m_final_g[None]]
    sm_v = [v_b_ada.reshape(6, D), v_norm1_g, v_conv_w[0], v_conv_b, v_lru_wa.reshape(D, 128), v_lru_ba,
            v_lru_wx.reshape(D, 128), v_lru_bx, v_lru_lambda, v_attn_sinks, v_rel_bias, v_norm2_g, v_final_g[None]]
    sm_g = [total[P_MOD:P_MOD + 6], total[0:1], conv_g, total[3:4], total_w[0:D], total[4:5],
            total_w[D:2 * D], total[5:6], total[6:7], total[P_ATT:P_ATT + 1, :N_HEADS],
            total[P_ATT + 1, :N_BUCKETS * N_HEADS].reshape(N_BUCKETS, N_HEADS), total[1:2], total[2:3]]
    sm_d, sm_nm, sm_nv = adamw_small(sm_w, sm_g, sm_m, sm_v)
    shapes = dict(b_ada=b_ada.shape, norm1_g=norm1_g.shape, conv_w=conv_w.shape, conv_b=conv_b.shape,
                  lru_wa=lru_wa.shape, lru_ba=lru_ba.shape, lru_wx=lru_wx.shape, lru_bx=lru_bx.shape,
                  lru_lambda=lru_lambda.shape, attn_sinks=attn_sinks.shape, rel_bias=rel_bias.shape,
                  norm2_g=norm2_g.shape, final_g=final_g.shape)
    grads = dict(w_ada=g_w_ada[None], **g_big)
    deltas = dict(w_ada=d_w_ada[None], **d_big)
    new_m = dict(w_ada=nm_w_ada[None], **nm_big)
    new_v = dict(w_ada=nv_w_ada[None], **nv_big)
    for i, nm in enumerate(sm_names):
        grads[nm] = sm_g[i].reshape(shapes[nm])
        deltas[nm] = sm_d[i].reshape(shapes[nm])
        new_m[nm] = sm_nm[i].reshape(shapes[nm])
        new_v[nm] = sm_nv[i].reshape(shapes[nm])
    order = ["w_ada", "b_ada", "norm1_g", "w_in", "conv_w", "conv_b", "lru_wa", "lru_ba", "lru_wx", "lru_bx",
             "lru_lambda", "w_lru_out", "w_attn_out", "attn_sinks", "rel_bias", "w_out", "norm2_g", "w_ff1", "w_ff2",
             "final_g"]
    return (loss, grad_x[None], *[grads[n] for n in order], *[deltas[n] for n in order],
            *[new_m[n] for n in order], *[new_v[n] for n in order])
```

```python
import math

import numpy as np
import jax
import jax.numpy as jnp
from jax import lax
from jax.experimental import pallas as pl
from jax.experimental.pallas import tpu as pltpu

F32 = jnp.float32
BF16 = jnp.bfloat16
SDS = jax.ShapeDtypeStruct
MESH = pl.DeviceIdType.MESH

D = 2048
D_FF = 4 * D
N_HEADS = 32
HEAD_DIM = 64
BLOCK = 128
N_LRU_BLOCKS = 16
LRU_C = 8.0
EPS = 1e-6
NEG_INF = -1e30
N_BUCKETS = 32
MAX_DISTANCE = 128
IN_W = 10752
IN_SHARD = IN_W // 4
IN_TILE = 896
ADA_SHARD = 6 * D // 4
OFF_LRU, OFF_GATE, OFF_Q, OFF_K, OFF_V, OFF_GA, OFF_GB = 0, 2048, 4096, 6144, 6400, 6656, 8704
SCALE = HEAD_DIM ** -0.5
N_CHIPS = 4
N_DEV = 8

ADAM_LR, ADAM_B1, ADAM_B2, ADAM_EPS, ADAM_WD, ADAM_STEP = 0.001, 0.9, 0.999, 1e-08, 0.01, 10
ADAM_C1 = 1.0 - ADAM_B1 ** ADAM_STEP
ADAM_C2 = 1.0 - ADAM_B2 ** ADAM_STEP

VMEM_LIMIT = 58 * 2 ** 20
SUB = 128
WG_TM = 1024
V_G1, V_SCALE1, V_SHIFT1, V_GATE1, V_G2, V_SCALE2, V_SHIFT2, V_GATE2, V_G3 = range(9)
L_CW0, L_CB, L_BA, L_BX, L_LAM = 0, 4, 5, 6, 7
P_VEC, P_CONVW, P_MOD, P_ATT, P_WA = 0, 8, 16, 24, 32


def _cp(n_axes):
    return pltpu.CompilerParams(dimension_semantics=("arbitrary",) * n_axes, vmem_limit_bytes=VMEM_LIMIT)


def _dot(a, b):
    return jnp.dot(a, b, preferred_element_type=F32)


def _dot_nt(a, b):
    return lax.dot_general(a, b, (((1,), (1,)), ((), ())), preferred_element_type=F32)


def _dot_tn(a, b):
    return lax.dot_general(a, b, (((0,), (0,)), ((), ())), preferred_element_type=F32)


_G0 = math.sqrt(2.0 / math.pi)
_G1 = 0.044715


def _gelu(x):
    return 0.5 * x * (1.0 + jnp.tanh(_G0 * (x + _G1 * x * x * x)))


def _gelu_grad(x):
    x2 = x * x
    t = jnp.tanh(_G0 * (x + _G1 * x * x2))
    return 0.5 * (1.0 + t) + 0.5 * x * (1.0 - t * t) * _G0 * (1.0 + 3.0 * _G1 * x2)


def _sigmoid(x):
    return 0.5 * jnp.tanh(0.5 * x) + 0.5


def _one_minus_exp2(x):
    t = jnp.tanh(x)
    return (-2.0 * t) / (1.0 - t)


def _softplus(z):
    e = jnp.exp(-jnp.abs(z))
    u = 1.0 + e
    l1p = jnp.where(u == 1.0, e, jnp.log(u) * e / (u - 1.0))
    return jnp.maximum(z, 0.0) + l1p


def _adamw_math(w, g, m, v):
    m2 = ADAM_B1 * m + (1.0 - ADAM_B1) * g
    v2 = ADAM_B2 * v + (1.0 - ADAM_B2) * (g * g)
    m_hat = m2 / ADAM_C1
    v_hat = v2 / ADAM_C2
    delta = -ADAM_LR * (m_hat / (jnp.sqrt(v_hat) + ADAM_EPS) + ADAM_WD * w)
    return delta, m2, v2


def _rms_parts(xv):
    r = lax.rsqrt(jnp.mean(xv * xv, axis=-1, keepdims=True) + EPS)
    return r, xv * r


def _row_fetches(hbm_refs, bufs, sems, i, rows):
    return [pltpu.make_async_copy(h.at[pl.ds(i * rows, rows), :], b, sems.at[n])
            for n, (h, b) in enumerate(zip(hbm_refs, bufs))]


def _modulated_norm(x_ref, v_ref, row_g, row_scale, row_shift, h_ref, rows):
    g, scale, shift = v_ref[row_g:row_g + 1, :], v_ref[row_scale:row_scale + 1, :], v_ref[row_shift:row_shift + 1, :]

    def sub(rb, carry):
        rs = pl.ds(pl.multiple_of(rb * SUB, SUB), SUB)
        _, xh = _rms_parts(x_ref[rs, :])
        h_ref[rs, :] = ((xh * g) * (1.0 + scale) + shift).astype(BF16)
        return carry

    lax.fori_loop(0, rows // SUB, sub, 0)


def inproj_fwd(name, x, h, vecs, w_in, proj, shard):
    s = x.shape[0]
    tm = min(1024, s)
    per = IN_SHARD // IN_TILE
    first = h is None

    def body(*refs):
        if first:
            _, x_ref, v_ref, w_ref, proj_ref, h_ref = refs

            @pl.when(pl.program_id(1) == 0)
            def _():
                _modulated_norm(x_ref, v_ref, V_G1, V_SCALE1, V_SHIFT1, h_ref, tm)
        else:
            _, h_ref, w_ref, _, proj_ref = refs
        proj_ref[...] = _dot(h_ref[...], w_ref[...]).astype(BF16)

    rows = pl.BlockSpec((tm, D), lambda i, j, sr: (i, 0))
    w_spec = pl.BlockSpec((None, D, IN_TILE), lambda i, j, sr: (sr[0], 0, j))
    proj_spec = pl.BlockSpec((tm, IN_TILE), lambda i, j, sr: (i, sr[0] * per + j))
    if first:
        gs = pltpu.PrefetchScalarGridSpec(
            num_scalar_prefetch=1, grid=(s // tm, per),
            in_specs=[rows, pl.BlockSpec((16, D), lambda i, j, sr: (0, 0)), w_spec], out_specs=[proj_spec, rows])
        return pl.pallas_call(body, name=name, grid_spec=gs, out_shape=[SDS((s, IN_W), BF16), SDS((s, D), BF16)],
                              compiler_params=_cp(2))(shard, x, vecs, w_in)
    gs = pltpu.PrefetchScalarGridSpec(
        num_scalar_prefetch=1, grid=(s // tm, per),
        in_specs=[rows, w_spec, pl.BlockSpec(memory_space=pl.ANY)], out_specs=proj_spec)
    return pl.pallas_call(body, name=name, grid_spec=gs, out_shape=SDS((s, IN_W), BF16),
                          input_output_aliases={3: 0}, compiler_params=_cp(2))(shard, h, w_in, proj)


def _lru_block_fwd(xbuf, lv_ref, wa_ref, wx_ref, b, t, first):
    cs = slice(b * 128, (b + 1) * 128)
    x0 = xbuf[pl.ds(8, t), cs]
    x1 = xbuf[pl.ds(7, t), cs]
    x2 = xbuf[pl.ds(6, t), cs]
    x3 = xbuf[pl.ds(5, t), cs]
    xc = (lv_ref[L_CB:L_CB + 1, cs] + lv_ref[3:4, cs] * x0 + lv_ref[2:3, cs] * x1
          + lv_ref[1:2, cs] * x2 + lv_ref[0:1, cs] * x3)
    xcb = xc.astype(BF16)
    r = _sigmoid(_dot(xcb, wa_ref[b]) + lv_ref[L_BA:L_BA + 1, cs])
    ig = _sigmoid(_dot(xcb, wx_ref[b]) + lv_ref[L_BX:L_BX + 1, cs])
    sp = _softplus(-lv_ref[L_LAM:L_LAM + 1, cs])
    log_a = (-LRU_C) * r * sp
    a = jnp.exp(log_a)
    mult = jnp.where(first, 1.0, jnp.sqrt(_one_minus_exp2(log_a)))
    return (x0, x1, x2, x3), xc, xcb, r, ig, sp, a, mult


def lru_fwd(proj, lvec, wa, wx):
    s = proj.shape[0]
    t = min(256, s)

    def body(lx_ref, gate_ref, lv_ref, wa_ref, wx_ref, ya_ref, rec_ref, yat_ref, xbuf, a_s, u_s, hc):
        i = pl.program_id(0)

        @pl.when(i == 0)
        def _():
            xbuf[pl.ds(0, 8), :] = jnp.zeros((8, D), F32)
            hc[...] = jnp.zeros((8, D), F32)

        @pl.when(i > 0)
        def _():
            xbuf[pl.ds(0, 8), :] = xbuf[pl.ds(t, 8), :]

        xbuf[pl.ds(8, t), :] = lx_ref[...].astype(F32)
        first = (lax.broadcasted_iota(jnp.int32, (t, 128), 0) + i * t) == 0
        for b in range(N_LRU_BLOCKS):
            cs = slice(b * 128, (b + 1) * 128)
            _, xc, _, _, ig, _, a, mult = _lru_block_fwd(xbuf, lv_ref, wa_ref, wx_ref, b, t, first)
            a_s[:, cs] = a
            u_s[:, cs] = mult * (ig * xc)

        def step(tt, h):
            h = a_s[pl.ds(tt, 1), :] * h + u_s[pl.ds(tt, 1), :]
            rec_ref[pl.ds(tt, 1), :] = h
            return h

        hc[0:1, :] = lax.fori_loop(0, t, step, hc[0:1, :], unroll=8)
        for b in range(N_LRU_BLOCKS):
            cs = slice(b * 128, (b + 1) * 128)
            yb = (rec_ref[:, cs] * _gelu(gate_ref[:, cs].astype(F32))).astype(BF16)
            ya_ref[:, cs] = yb
            yat_ref[cs, :] = yb.T

    return pl.pallas_call(
        body, name="lru_fwd", grid=(s // t,),
        in_specs=[pl.BlockSpec((t, D), lambda i: (i, OFF_LRU // D)),
                  pl.BlockSpec((t, D), lambda i: (i, OFF_GATE // D)),
                  pl.BlockSpec((8, D), lambda i: (0, 0)),
                  pl.BlockSpec((N_LRU_BLOCKS, 128, 128), lambda i: (0, 0, 0)),
                  pl.BlockSpec((N_LRU_BLOCKS, 128, 128), lambda i: (0, 0, 0))],
        out_specs=[pl.BlockSpec((t, D), lambda i: (i, 0)), pl.BlockSpec((t, D), lambda i: (i, 0)),
                   pl.BlockSpec((D, t), lambda i: (0, i))],
        out_shape=[SDS((s, D), BF16), SDS((s, D), F32), SDS((D, s), BF16)],
        scratch_shapes=[pltpu.VMEM((t + 8, D), F32), pltpu.VMEM((t, D), F32), pltpu.VMEM((t, D), F32),
                        pltpu.VMEM((8, D), F32)],
        compiler_params=_cp(1))(proj, proj, lvec, wa, wx)


def t5_bucket_table():
    qi = np.arange(BLOCK)[:, None]
    ki = np.arange(2 * BLOCK)[None, :]
    rel = qi + BLOCK - ki
    relc = np.maximum(rel, 0)
    max_exact = N_BUCKETS // 2
    relf = np.maximum(relc, 1).astype(np.float32)
    large = max_exact + (np.log(relf / np.float32(max_exact)) / np.float32(math.log(MAX_DISTANCE / max_exact))
                         * np.float32(N_BUCKETS - max_exact)).astype(np.int32)
    large = np.minimum(large, N_BUCKETS - 1)
    bucket = np.where(relc < max_exact, relc, large)
    bucket = np.where((rel >= 0) & (rel < BLOCK), bucket, -1)
    return jnp.asarray(bucket.reshape(1, BLOCK * 2 * BLOCK), jnp.int32)


def bias_band(rel_bias_t, buckets):
    n = BLOCK * 2 * BLOCK
    tn = 4096

    def body(bk_ref, rb_ref, o_ref):
        row = lax.broadcasted_iota(jnp.int32, (N_BUCKETS, tn), 0)
        oh = jnp.where(row == bk_ref[...], 1.0, 0.0).astype(BF16)
        rb = rb_ref[...]
        p0 = rb.astype(BF16)
        r1 = rb - p0.astype(F32)
        p1 = r1.astype(BF16)
        p2 = (r1 - p1.astype(F32)).astype(BF16)
        o_ref[...] = _dot(p0, oh) + _dot(p1, oh) + _dot(p2, oh)

    return pl.pallas_call(
        body, name="bias_band", grid=(n // tn,),
        in_specs=[pl.BlockSpec((1, tn), lambda i: (0, i)), pl.BlockSpec((N_HEADS, N_BUCKETS), lambda i: (0, 0))],
        out_specs=pl.BlockSpec((N_HEADS, tn), lambda i: (0, i)),
        out_shape=SDS((N_HEADS, n), F32), compiler_params=_cp(1))(buckets, rel_bias_t)


def bias_band_bwd(dband, buckets):
    n = BLOCK * 2 * BLOCK
    tn = 4096

    def body(bk_ref, d_ref, o_ref):
        @pl.when(pl.program_id(0) == 0)
        def _():
            o_ref[...] = jnp.zeros_like(o_ref)
        row = lax.broadcasted_iota(jnp.int32, (N_BUCKETS, tn), 0)
        oh = jnp.where(row == bk_ref[...], 1.0, 0.0).astype(BF16)
        dv = d_ref[...]
        p0 = dv.astype(BF16)
        r1 = dv - p0.astype(F32)
        p1 = r1.astype(BF16)
        p2 = (r1 - p1.astype(F32)).astype(BF16)
        o_ref[...] += _dot_nt(oh, p0) + _dot_nt(oh, p1) + _dot_nt(oh, p2)

    return pl.pallas_call(
        body, name="bias_band_bwd", grid=(n // tn,),
        in_specs=[pl.BlockSpec((1, tn), lambda i: (0, i)), pl.BlockSpec((N_HEADS, tn), lambda i: (0, i))],
        out_specs=pl.BlockSpec((N_BUCKETS, N_HEADS), lambda i: (0, 0)),
        out_shape=SDS((N_BUCKETS, N_HEADS), F32), compiler_params=_cp(1))(buckets, dband)


def _dup_half(band, which):
    lane = lax.broadcasted_iota(jnp.int32, band.shape, 1)
    rolled = pltpu.roll(band, 64, 1)
    keep = (lane < 64) if which == 0 else (lane >= 64)
    return jnp.where(keep, band, rolled)


def _attn_probs(scores, bias, sink, valid):
    sc = jnp.where(valid, scores * SCALE + bias, NEG_INF)
    m = jnp.maximum(jnp.max(sc, axis=-1, keepdims=True), sink)
    e = jnp.exp(sc - m)
    es = jnp.exp(sink - m)
    inv = 1.0 / (jnp.sum(e, axis=-1, keepdims=True) + es)
    return e * inv, es * inv


def _stack_heads(src_ref, kv, dst):
    lane = lax.broadcasted_iota(jnp.int32, (BLOCK, 128), 1)
    for jj in range(4):
        slab = src_ref[:, (4 * kv + jj) * 128:(4 * kv + jj + 1) * 128]
        for hh in range(2):
            keep = (lane < 64) if hh == 0 else (lane >= 64)
            dst[pl.ds((2 * jj + hh) * BLOCK, BLOCK), :] = jnp.where(keep, slab, jnp.zeros_like(slab))


def _unstack_heads(stacked, dst_ref, kv, dst_t_ref=None):
    lane = lax.broadcasted_iota(jnp.int32, (BLOCK, 128), 1)
    for jj in range(4):
        lo = stacked[(2 * jj) * BLOCK:(2 * jj + 1) * BLOCK]
        hi = stacked[(2 * jj + 1) * BLOCK:(2 * jj + 2) * BLOCK]
        slab = jnp.where(lane < 64, lo, hi).astype(dst_ref.dtype)
        dst_ref[:, (4 * kv + jj) * 128:(4 * kv + jj + 1) * 128] = slab
        if dst_t_ref is not None:
            dst_t_ref[(4 * kv + jj) * 128:(4 * kv + jj + 1) * 128, :] = slab.T


def _band_valid(n):
    qi = lax.broadcasted_iota(jnp.int32, (BLOCK, 2 * BLOCK), 0)
    ki = lax.broadcasted_iota(jnp.int32, (BLOCK, 2 * BLOCK), 1)
    rel = qi + BLOCK - ki
    return (rel >= 0) & (rel < BLOCK) & ((ki >= BLOCK) | (n > 0))


def _kv_bands(prev_ref, cur_ref):
    band = jnp.concatenate([prev_ref[...].astype(F32), cur_ref[...].astype(F32)], axis=0)
    return [_dup_half(band, 0).astype(BF16), _dup_half(band, 1).astype(BF16)]


def attn_fwd(proj, band, sinks):
    s = proj.shape[0]
    nb = s // BLOCK
    qw = 1024

    def body(sk_ref, q_ref, kp_ref, kc_ref, vp_ref, vc_ref, b_ref, o_ref, ot_ref, qs_buf, s_buf, p_buf):
        n = pl.program_id(0)
        gp = pl.program_id(1)
        valid = _band_valid(n)
        kks = _kv_bands(kp_ref, kc_ref)
        vvs = _kv_bands(vp_ref, vc_ref)
        for kv in range(2):
            _stack_heads(q_ref, kv, qs_buf)
            s_buf[...] = _dot_nt(qs_buf[...], kks[kv])
            for hq in range(8):
                hl = 8 * kv + hq
                rows = pl.ds(hq * BLOCK, BLOCK)
                p, _ = _attn_probs(s_buf[rows, :], b_ref[hl], sk_ref[gp * 16 + hl], valid)
                p_buf[rows, :] = p.astype(BF16)
            _unstack_heads(_dot(p_buf[...], vvs[kv]), o_ref, kv, ot_ref)

    kb, vb = OFF_K // 128, OFF_V // 128
    return pl.pallas_call(
        body, name="attn_fwd", grid=(nb, 2),
        in_specs=[pl.BlockSpec(memory_space=pltpu.SMEM),
                  pl.BlockSpec((BLOCK, qw), lambda n, g: (n, OFF_Q // qw + g)),
                  pl.BlockSpec((BLOCK, 128), lambda n, g: (jnp.maximum(n - 1, 0), kb + g)),
                  pl.BlockSpec((BLOCK, 128), lambda n, g: (n, kb + g)),
                  pl.BlockSpec((BLOCK, 128), lambda n, g: (jnp.maximum(n - 1, 0), vb + g)),
                  pl.BlockSpec((BLOCK, 128), lambda n, g: (n, vb + g)),
                  pl.BlockSpec((16, BLOCK, 2 * BLOCK), lambda n, g: (g, 0, 0))],
        out_specs=[pl.BlockSpec((BLOCK, qw), lambda n, g: (n, g)), pl.BlockSpec((qw, BLOCK), lambda n, g: (g, n))],
        out_shape=[SDS((s, D), BF16), SDS((D, s), BF16)],
        scratch_shapes=[pltpu.VMEM((8 * BLOCK, 128), BF16), pltpu.VMEM((8 * BLOCK, 2 * BLOCK), F32),
                        pltpu.VMEM((8 * BLOCK, 2 * BLOCK), BF16)],
        compiler_params=_cp(2))(sinks, proj, proj, proj, proj, proj, band)


def merge_fwd(ya, att, w_lru_out, w_attn_out, proj):
    s = ya.shape[0]
    tm, tn = min(1024, s), 512

    def body(ya_ref, at_ref, wl_ref, wt_ref, ga_ref, gb_ref, yab_ref, mg_ref, mgt_ref):
        y_a = _dot(ya_ref[...], wl_ref[...])
        y_b = _dot(at_ref[...], wt_ref[...])
        yab_ref[0] = y_a.astype(BF16)
        yab_ref[1] = y_b.astype(BF16)
        mg = (_sigmoid(ga_ref[...].astype(F32)) * y_a + _sigmoid(gb_ref[...].astype(F32)) * y_b).astype(BF16)
        mg_ref[...] = mg
        mgt_ref[...] = mg.T

    return pl.pallas_call(
        body, name="merge_fwd", grid=(s // tm, D // tn),
        in_specs=[pl.BlockSpec((tm, D), lambda i, j: (i, 0)), pl.BlockSpec((tm, D), lambda i, j: (i, 0)),
                  pl.BlockSpec((D, tn), lambda i, j: (0, j)), pl.BlockSpec((D, tn), lambda i, j: (0, j)),
                  pl.BlockSpec((tm, tn), lambda i, j: (i, OFF_GA // tn + j)),
                  pl.BlockSpec((tm, tn), lambda i, j: (i, OFF_GB // tn + j))],
        out_specs=[pl.BlockSpec((2, tm, tn), lambda i, j: (0, i, j)), pl.BlockSpec((tm, tn), lambda i, j: (i, j)),
                   pl.BlockSpec((tn, tm), lambda i, j: (j, i))],
        out_shape=[SDS((2, s, D), BF16), SDS((s, D), BF16), SDS((D, s), BF16)],
        compiler_params=_cp(2))(ya, att, w_lru_out, w_attn_out, proj, proj)


def outproj_fwd(merged, w_out, x, vecs):
    s = x.shape[0]
    tm, tn = min(1024, s), 1024

    def body(m_ref, w_ref, x_ref, v_ref, x1_ref, o1_ref):
        o1 = _dot(m_ref[...], w_ref[...])
        o1_ref[...] = o1.astype(BF16)
        x1_ref[...] = x_ref[...] + v_ref[V_GATE1:V_GATE1 + 1, :] * o1

    return pl.pallas_call(
        body, name="outproj_fwd", grid=(s // tm, D // tn),
        in_specs=[pl.BlockSpec((tm, D), lambda i, j: (i, 0)), pl.BlockSpec((D, tn), lambda i, j: (0, j)),
                  pl.BlockSpec((tm, tn), lambda i, j: (i, j)), pl.BlockSpec((16, tn), lambda i, j: (0, j))],
        out_specs=[pl.BlockSpec((tm, tn), lambda i, j: (i, j)), pl.BlockSpec((tm, tn), lambda i, j: (i, j))],
        out_shape=[SDS((s, D), F32), SDS((s, D), BF16)],
        compiler_params=_cp(2))(merged, w_out, x, vecs)


def ff1_fwd(x1, vecs, w_ff1):
    s = x1.shape[0]
    tm, tn = min(1024, s), 1024
    per = D // tn

    def body(x_ref, v_ref, w_ref, f_ref, h_ref, fft_ref):
        @pl.when(pl.program_id(1) == 0)
        def _():
            _modulated_norm(x_ref, v_ref, V_G2, V_SCALE2, V_SHIFT2, h_ref, tm)
        fv = _dot(h_ref[...], w_ref[...])
        f_ref[...] = fv.astype(BF16)
        fp = jnp.maximum(fv, 0.0)
        fft_ref[...] = (fp * fp).astype(BF16).T

    return pl.pallas_call(
        body, name="ff1_fwd", grid=(s // tm, D_FF // tn),
        in_specs=[pl.BlockSpec((tm, D), lambda i, j: (i, 0)), pl.BlockSpec((16, D), lambda i, j: (0, 0)),
                  pl.BlockSpec((None, D, tn), lambda i, j: (j // per, 0, j % per))],
        out_specs=[pl.BlockSpec((tm, tn), lambda i, j: (i, j)), pl.BlockSpec((tm, D), lambda i, j: (i, 0)),
                   pl.BlockSpec((tn, tm), lambda i, j: (j, i))],
        out_shape=[SDS((s, D_FF), BF16), SDS((s, D), BF16), SDS((D_FF, s), BF16)],
        compiler_params=_cp(2))(x1, vecs, w_ff1)


def matmul_bf16(name, a, b, tm, tn, b_part=0):
    m, k = a.shape
    n = b.shape[1]

    def body(a_ref, b_ref, o_ref):
        o_ref[...] = _dot(a_ref[...], b_ref[...]).astype(BF16)

    return pl.pallas_call(
        body, name=name, grid=(m // tm, n // tn),
        in_specs=[pl.BlockSpec((tm, k), lambda i, j: (i, 0)), pl.BlockSpec((k, tn), lambda i, j: (b_part, j))],
        out_specs=pl.BlockSpec((tm, tn), lambda i, j: (i, j)),
        out_shape=SDS((m, n), BF16), compiler_params=_cp(2))(a, b)


def ff2_loss(f, w_ff2, x1, tgt, vecs):
    s = x1.shape[0]
    tm, tk = min(512, s), 2048
    nk = D_FF // tk

    def body(f_ref, w_ref, x1_hbm, t_hbm, v_ref, dx2_ref, do2_ref, sums_ref, loss_ref, acc, x1_ref, t_ref, sems):
        i, k = pl.program_id(0), pl.program_id(1)
        fetches = _row_fetches((x1_hbm, t_hbm), (x1_ref, t_ref), sems, i, tm)

        @pl.when((i == 0) & (k == 0))
        def _():
            sums_ref[...] = jnp.zeros_like(sums_ref)
            loss_ref[...] = jnp.zeros_like(loss_ref)

        @pl.when(k == 0)
        def _():
            acc[...] = jnp.zeros_like(acc)
            for cp in fetches:
                cp.start()

        fv = jnp.maximum(f_ref[...].astype(F32), 0.0)
        acc[...] += _dot((fv * fv).astype(BF16), w_ref[...])

        @pl.when(k == nk - 1)
        def _():
            for cp in fetches:
                cp.wait()
            gate2 = v_ref[V_GATE2:V_GATE2 + 1, :]
            g3 = v_ref[V_G3:V_G3 + 1, :]

            def sub(rb, carry):
                rs = pl.ds(pl.multiple_of(rb * SUB, SUB), SUB)
                o2 = acc[rs, :]
                x2 = x1_ref[rs, :] + gate2 * o2
                r3, xh = _rms_parts(x2)
                e = xh * g3 - t_ref[rs, :]
                loss_ref[...] += (0.5 / D) * jnp.sum(e * e)
                dy = e * (1.0 / D)
                sums_ref[0:1, :] += jnp.sum(dy * xh, axis=0, keepdims=True)
                dxh = dy * g3
                dx2 = r3 * (dxh - xh * jnp.mean(dxh * xh, axis=-1, keepdims=True))
                sums_ref[1:2, :] += jnp.sum(dx2 * o2, axis=0, keepdims=True)
                dx2_ref[rs, :] = dx2
                do2_ref[rs, :] = (dx2 * gate2).astype(BF16)
                return carry

            lax.fori_loop(0, tm // SUB, sub, 0)

    return pl.pallas_call(
        body, name="ff2_loss", grid=(s // tm, nk),
        in_specs=[pl.BlockSpec((tm, tk), lambda i, k: (i, k)), pl.BlockSpec((tk, D), lambda i, k: (k, 0)),
                  pl.BlockSpec(memory_space=pl.ANY), pl.BlockSpec(memory_space=pl.ANY),
                  pl.BlockSpec((16, D), lambda i, k: (0, 0))],
        out_specs=[pl.BlockSpec((tm, D), lambda i, k: (i, 0)), pl.BlockSpec((tm, D), lambda i, k: (i, 0)),
                   pl.BlockSpec((8, D), lambda i, k: (0, 0)), pl.BlockSpec((8, 128), lambda i, k: (0, 0))],
        out_shape=[SDS((s, D), F32), SDS((s, D), BF16), SDS((8, D), F32), SDS((8, 128), F32)],
        scratch_shapes=[pltpu.VMEM((tm, D), F32), pltpu.VMEM((tm, D), F32), pltpu.VMEM((tm, D), F32),
                        pltpu.SemaphoreType.DMA((2,))],
        compiler_params=_cp(2))(f, w_ff2, x1, tgt, vecs)


def ff2_bwd(do2, w_ff2, f):
    s = do2.shape[0]
    tm, tn = min(1024, s), 1024

    def body(d_ref, w_ref, f_ref, o_ref):
        dff = _dot_nt(d_ref[...], w_ref[...])
        o_ref[...] = (dff * (2.0 * jnp.maximum(f_ref[...].astype(F32), 0.0))).astype(BF16)

    return pl.pallas_call(
        body, name="ff2_bwd", grid=(s // tm, D_FF // tn),
        in_specs=[pl.BlockSpec((tm, D), lambda i, j: (i, 0)), pl.BlockSpec((tn, D), lambda i, j: (j, 0)),
                  pl.BlockSpec((tm, tn), lambda i, j: (i, j))],
        out_specs=pl.BlockSpec((tm, tn), lambda i, j: (i, j)),
        out_shape=SDS((s, D_FF), BF16), compiler_params=_cp(2))(do2, w_ff2, f)


def weight_grad(name, a, b, tn, out_shape, out_block, out_map):
    s, m = a.shape
    n = b.shape[1]
    tm = WG_TM
    chunk = min(1024, s)
    nch = s // chunk

    def body(a_hbm, b_ref, o_ref, a_buf, at_s, sem):
        i = pl.program_id(0)

        @pl.when(pl.program_id(1) == 0)
        def _():
            def fetch(ch):
                return pltpu.make_async_copy(a_hbm.at[pl.ds(ch * chunk, chunk), pl.ds(i * tm, tm)],
                                             a_buf.at[ch % 2], sem.at[ch % 2])
            fetch(0).start()
            for ch in range(nch):
                if ch + 1 < nch:
                    fetch(ch + 1).start()
                fetch(ch).wait()
                at_s[:, ch * chunk:(ch + 1) * chunk] = a_buf[ch % 2].T

        o_ref[...] = _dot(at_s[...], b_ref[...]).astype(BF16)

    return pl.pallas_call(
        body, name=name, grid=(m // tm, n // tn),
        in_specs=[pl.BlockSpec(memory_space=pl.ANY), pl.BlockSpec((s, tn), lambda i, j: (0, j))],
        out_specs=pl.BlockSpec(out_block, lambda i, j: out_map(i, j)),
        out_shape=SDS(out_shape, BF16),
        scratch_shapes=[pltpu.VMEM((2, chunk, tm), BF16), pltpu.VMEM((tm, s), BF16), pltpu.SemaphoreType.DMA((2,))],
        compiler_params=_cp(2))(a, b)


def ff1_bwd(df, w_ff1, x1, dx2, o1, vecs):
    s = df.shape[0]
    tm, tk = min(512, s), 2048
    nk = D_FF // tk
    per = D // tk

    def body(d_ref, w_ref, x1_hbm, dx2_hbm, o1_hbm, v_ref, dx1_ref, do1_ref, sums_ref, acc, x1_ref, dx2_ref, o1_ref, sems):
        i, k = pl.program_id(0), pl.program_id(1)
        fetches = _row_fetches((x1_hbm, dx2_hbm, o1_hbm), (x1_ref, dx2_ref, o1_ref), sems, i, tm)

        @pl.when((i == 0) & (k == 0))
        def _():
            sums_ref[...] = jnp.zeros_like(sums_ref)

        @pl.when(k == 0)
        def _():
            acc[...] = jnp.zeros_like(acc)
            for cp in fetches:
                cp.start()

        acc[...] += _dot_nt(d_ref[...], w_ref[...])

        @pl.when(k == nk - 1)
        def _():
            for cp in fetches:
                cp.wait()
            g2 = v_ref[V_G2:V_G2 + 1, :]
            scale2 = v_ref[V_SCALE2:V_SCALE2 + 1, :]
            gate1 = v_ref[V_GATE1:V_GATE1 + 1, :]

            def sub(rb, carry):
                rs = pl.ds(pl.multiple_of(rb * SUB, SUB), SUB)
                dh = acc[rs, :]
                r2, xh = _rms_parts(x1_ref[rs, :])
                sums_ref[0:1, :] += jnp.sum(dh, axis=0, keepdims=True)
                sums_ref[1:2, :] += jnp.sum(dh * (xh * g2), axis=0, keepdims=True)
                dxn = dh * (1.0 + scale2)
                sums_ref[2:3, :] += jnp.sum(dxn * xh, axis=0, keepdims=True)
                dxh = dxn * g2
                dx1 = dx2_ref[rs, :] + r2 * (dxh - xh * jnp.mean(dxh * xh, axis=-1, keepdims=True))
                sums_ref[3:4, :] += jnp.sum(dx1 * o1_ref[rs, :].astype(F32), axis=0, keepdims=True)
                dx1_ref[rs, :] = dx1
                do1_ref[rs, :] = (dx1 * gate1).astype(BF16)
                return carry

            lax.fori_loop(0, tm // SUB, sub, 0)

    return pl.pallas_call(
        body, name="ff1_bwd", grid=(s // tm, nk),
        in_specs=[pl.BlockSpec((tm, tk), lambda i, k: (i, k)),
                  pl.BlockSpec((None, D, tk), lambda i, k: (k // per, 0, k % per)),
                  pl.BlockSpec(memory_space=pl.ANY), pl.BlockSpec(memory_space=pl.ANY),
                  pl.BlockSpec(memory_space=pl.ANY), pl.BlockSpec((16, D), lambda i, k: (0, 0))],
        out_specs=[pl.BlockSpec((tm, D), lambda i, k: (i, 0)), pl.BlockSpec((tm, D), lambda i, k: (i, 0)),
                   pl.BlockSpec((8, D), lambda i, k: (0, 0))],
        out_shape=[SDS((s, D), F32), SDS((s, D), BF16), SDS((8, D), F32)],
        scratch_shapes=[pltpu.VMEM((tm, D), F32), pltpu.VMEM((tm, D), F32), pltpu.VMEM((tm, D), F32),
                        pltpu.VMEM((tm, D), BF16), pltpu.SemaphoreType.DMA((3,))],
        compiler_params=_cp(2))(df, w_ff1, x1, dx2, o1, vecs)


def outproj_bwd(do1, w_out, yab, proj):
    s = do1.shape[0]
    tm, tn = min(1024, s), 512
    per = D // tn

    def body(d_ref, w_ref, y_ref, g_ref, dy_ref, dp_ref):
        dm = _dot_nt(d_ref[...], w_ref[...])
        sg = _sigmoid(g_ref[...].astype(F32))
        dy_ref[...] = (dm * sg).astype(BF16)
        dp_ref[...] = (dm * y_ref[...].astype(F32) * sg * (1.0 - sg)).astype(BF16)

    return pl.pallas_call(
        body, name="outproj_bwd", grid=(s // tm, 2 * per),
        in_specs=[pl.BlockSpec((tm, D), lambda i, j: (i, 0)), pl.BlockSpec((tn, D), lambda i, j: (j % per, 0)),
                  pl.BlockSpec((None, tm, tn), lambda i, j: (j // per, i, j % per)),
                  pl.BlockSpec((tm, tn), lambda i, j: (i, OFF_GA // tn + j))],
        out_specs=[pl.BlockSpec((None, tm, tn), lambda i, j: (j // per, i, j % per)),
                   pl.BlockSpec((tm, tn), lambda i, j: (i, OFF_GA // tn + j))],
        out_shape=[SDS((2, s, D), BF16), SDS((s, IN_W), BF16)],
        compiler_params=_cp(2))(do1, w_out, yab, proj)


def lruout_bwd(dyab, w_lru_out, rec, proj, dproj):
    s = rec.shape[0]
    tm, tn = min(1024, s), 1024

    def body(d_ref, w_ref, r_ref, g_ref, dp_in, dr_ref, dp_ref):
        dya = _dot_nt(d_ref[...], w_ref[...])
        gate = g_ref[...].astype(F32)
        dr_ref[...] = dya * _gelu(gate)
        dp_ref[...] = (dya * r_ref[...] * _gelu_grad(gate)).astype(BF16)

    return pl.pallas_call(
        body, name="lruout_bwd", grid=(s // tm, D // tn),
        in_specs=[pl.BlockSpec((None, tm, D), lambda i, j: (0, i, 0)), pl.BlockSpec((tn, D), lambda i, j: (j, 0)),
                  pl.BlockSpec((tm, tn), lambda i, j: (i, j)),
                  pl.BlockSpec((tm, tn), lambda i, j: (i, OFF_GATE // tn + j)),
                  pl.BlockSpec(memory_space=pl.ANY)],
        out_specs=[pl.BlockSpec((tm, tn), lambda i, j: (i, j)),
                   pl.BlockSpec((tm, tn), lambda i, j: (i, OFF_GATE // tn + j))],
        out_shape=[SDS((s, D), F32), SDS((s, IN_W), BF16)],
        input_output_aliases={4: 1},
        compiler_params=_cp(2))(dyab, w_lru_out, rec, proj, dproj)


def attnout_bwd(dyab, w_attn_out):
    s = dyab.shape[1]
    tm, tn = min(1024, s), 1024

    def body(d_ref, w_ref, o_ref):
        o_ref[...] = _dot_nt(d_ref[...], w_ref[...]).astype(BF16)

    return pl.pallas_call(
        body, name="attnout_bwd", grid=(s // tm, D // tn),
        in_specs=[pl.BlockSpec((None, tm, D), lambda i, j: (1, i, 0)), pl.BlockSpec((tn, D), lambda i, j: (j, 0))],
        out_specs=pl.BlockSpec((tm, tn), lambda i, j: (i, j)),
        out_shape=SDS((s, D), BF16), compiler_params=_cp(2))(dyab, w_attn_out)


def attn_bwd(proj, band, sinks, datt, dproj):
    s = proj.shape[0]
    nb = s // BLOCK
    qw = 1024

    def body(sk_ref, q_ref, kp_ref, kc_ref, vp_ref, vc_ref, b_ref, do_ref, dp_in,
             dq_ref, dkb_ref, dvb_ref, db_ref, ds_ref, qs_buf, dos_buf, s_buf, dp_buf, p_buf, dsc_buf):
        gp = pl.program_id(0)
        n = pl.program_id(1)

        @pl.when(n == 0)
        def _():
            db_ref[...] = jnp.zeros_like(db_ref)
            ds_ref[...] = jnp.zeros_like(ds_ref)

        valid = _band_valid(n)
        kks = _kv_bands(kp_ref, kc_ref)
        vvs = _kv_bands(vp_ref, vc_ref)
        lane_b = lax.broadcasted_iota(jnp.int32, (2 * BLOCK, 128), 1)
        dks, dvs = [], []
        for kv in range(2):
            _stack_heads(q_ref, kv, qs_buf)
            _stack_heads(do_ref, kv, dos_buf)
            s_buf[...] = _dot_nt(qs_buf[...], kks[kv])
            dp_buf[...] = _dot_nt(dos_buf[...], vvs[kv])
            for hq in range(8):
                hl = 8 * kv + hq
                rows = pl.ds(hq * BLOCK, BLOCK)
                p, ps = _attn_probs(s_buf[rows, :], b_ref[hl], sk_ref[gp * 16 + hl], valid)
                dp = dp_buf[rows, :]
                delta = jnp.sum(p * dp, axis=-1, keepdims=True)
                dsc = p * (dp - delta)
                db_ref[hl] += dsc
                ds_ref[hl:hl + 1, :] += jnp.zeros((1, 128), F32) - jnp.sum(ps * delta)
                p_buf[rows, :] = p.astype(BF16)
                dsc_buf[rows, :] = (dsc * SCALE).astype(BF16)
            _unstack_heads(_dot(dsc_buf[...], kks[kv]), dq_ref, kv)
            dk = _dot_tn(dsc_buf[...], qs_buf[...])
            dv = _dot_tn(p_buf[...], dos_buf[...])
            dks.append(dk + pltpu.roll(dk, 64, 1))
            dvs.append(dv + pltpu.roll(dv, 64, 1))
        dkb_ref[...] = jnp.where(lane_b < 64, dks[0], dks[1])
        dvb_ref[...] = jnp.where(lane_b < 64, dvs[0], dvs[1])

    kb, vb = OFF_K // 128, OFF_V // 128
    return pl.pallas_call(
        body, name="attn_bwd", grid=(2, nb),
        in_specs=[pl.BlockSpec(memory_space=pltpu.SMEM),
                  pl.BlockSpec((BLOCK, qw), lambda g, n: (n, OFF_Q // qw + g)),
                  pl.BlockSpec((BLOCK, 128), lambda g, n: (jnp.maximum(n - 1, 0), kb + g)),
                  pl.BlockSpec((BLOCK, 128), lambda g, n: (n, kb + g)),
                  pl.BlockSpec((BLOCK, 128), lambda g, n: (jnp.maximum(n - 1, 0), vb + g)),
                  pl.BlockSpec((BLOCK, 128), lambda g, n: (n, vb + g)),
                  pl.BlockSpec((16, BLOCK, 2 * BLOCK), lambda g, n: (g, 0, 0)),
                  pl.BlockSpec((BLOCK, qw), lambda g, n: (n, g)),
                  pl.BlockSpec(memory_space=pl.ANY)],
        out_specs=[pl.BlockSpec((BLOCK, qw), lambda g, n: (n, OFF_Q // qw + g)),
                   pl.BlockSpec((2 * BLOCK, 128), lambda g, n: (n, g)),
                   pl.BlockSpec((2 * BLOCK, 128), lambda g, n: (n, g)),
                   pl.BlockSpec((16, BLOCK, 2 * BLOCK), lambda g, n: (g, 0, 0)),
                   pl.BlockSpec((16, 128), lambda g, n: (g, 0))],
        out_shape=[SDS((s, IN_W), BF16), SDS((nb * 2 * BLOCK, 256), F32), SDS((nb * 2 * BLOCK, 256), F32),
                   SDS((N_HEADS, BLOCK, 2 * BLOCK), F32), SDS((N_HEADS, 128), F32)],
        input_output_aliases={8: 0},
        scratch_shapes=[pltpu.VMEM((8 * BLOCK, 128), BF16), pltpu.VMEM((8 * BLOCK, 128), BF16),
                        pltpu.VMEM((8 * BLOCK, 2 * BLOCK), F32), pltpu.VMEM((8 * BLOCK, 2 * BLOCK), F32),
                        pltpu.VMEM((8 * BLOCK, 2 * BLOCK), BF16), pltpu.VMEM((8 * BLOCK, 2 * BLOCK), BF16)],
        compiler_params=_cp(2))(sinks, proj, proj, proj, proj, proj, band, datt, dproj)


def dkv_combine(dkb, dvb, dproj):
    nb = dkb.shape[0] // (2 * BLOCK)
    s = nb * BLOCK
    dkb3 = dkb.reshape(nb, 2 * BLOCK, 256)
    dvb3 = dvb.reshape(nb, 2 * BLOCK, 256)

    def body(k1, k2, v1, v2, dp_in, o_ref):
        nxt = jnp.where(pl.program_id(0) < nb - 1, 1.0, 0.0)
        o_ref[:, 0:256] = (k1[...] + nxt * k2[...]).astype(BF16)
        o_ref[:, 256:512] = (v1[...] + nxt * v2[...]).astype(BF16)

    spec1 = pl.BlockSpec((None, BLOCK, 256), lambda m: (m, 1, 0))
    spec2 = pl.BlockSpec((None, BLOCK, 256), lambda m: (jnp.minimum(m + 1, nb - 1), 0, 0))
    return pl.pallas_call(
        body, name="dkv_combine", grid=(nb,),
        in_specs=[spec1, spec2, spec1, spec2, pl.BlockSpec(memory_space=pl.ANY)],
        out_specs=pl.BlockSpec((BLOCK, 512), lambda m: (m, OFF_K // 512)),
        out_shape=SDS((s, IN_W), BF16), input_output_aliases={4: 0},
        compiler_params=_cp(1))(dkb3, dkb3, dvb3, dvb3, dproj)


def lru_bwd(proj, rec, drec, lvec, wa, wx, dproj):
    s = proj.shape[0]
    t = min(256, s)
    nt = s // t

    def body(lx_ref, lxh_ref, rec_ref, rech_ref, dr_ref, lv_ref, wa_ref, wx_ref, dp_in,
             dlx_ref, sums_ref, dwa_ref, dwx_ref,
             xbuf, hbuf, dxbuf, a_s, dh_s, xc_s, r_s, ig_s, mu_s, gc):
        step_i = pl.program_id(0)
        ti = nt - 1 - step_i

        @pl.when(step_i == 0)
        def _():
            sums_ref[...] = jnp.zeros_like(sums_ref)
            dwa_ref[...] = jnp.zeros_like(dwa_ref)
            dwx_ref[...] = jnp.zeros_like(dwx_ref)
            dxbuf[pl.ds(t, 8), :] = jnp.zeros((8, D), F32)
            gc[...] = jnp.zeros((8, D), F32)

        live = jnp.where(ti > 0, 1.0, 0.0)
        xbuf[pl.ds(0, 8), :] = lxh_ref[...].astype(F32)[8:16] * live
        xbuf[pl.ds(8, t), :] = lx_ref[...].astype(F32)
        hbuf[pl.ds(0, 8), :] = rech_ref[...] * live
        hbuf[pl.ds(8, t), :] = rec_ref[...]
        first = (lax.broadcasted_iota(jnp.int32, (t, 128), 0) + ti * t) == 0
        for b in range(N_LRU_BLOCKS):
            cs = slice(b * 128, (b + 1) * 128)
            _, xc, _, r, ig, _, a, mult = _lru_block_fwd(xbuf, lv_ref, wa_ref, wx_ref, b, t, first)
            a_s[:, cs] = a
            xc_s[:, cs] = xc
            r_s[:, cs] = r
            ig_s[:, cs] = ig
            mu_s[:, cs] = mult

        def step(q, g):
            tt = t - 1 - q
            dh = dr_ref[pl.ds(tt, 1), :] + g
            dh_s[pl.ds(tt, 1), :] = dh
            return a_s[pl.ds(tt, 1), :] * dh

        gc[0:1, :] = lax.fori_loop(0, t, step, gc[0:1, :], unroll=8)
        for b in range(N_LRU_BLOCKS):
            cs = slice(b * 128, (b + 1) * 128)
            dh = dh_s[:, cs]
            a = a_s[:, cs]
            xc = xc_s[:, cs]
            r = r_s[:, cs]
            ig = ig_s[:, cs]
            mult = mu_s[:, cs]
            sp = _softplus(-lv_ref[L_LAM:L_LAM + 1, cs])
            lam = lv_ref[L_LAM:L_LAM + 1, cs]
            da = dh * hbuf[pl.ds(7, t), cs]
            dmult = jnp.where(first, 0.0, dh * ig * xc)
            dig = dh * mult * xc
            dxc = dh * mult * ig
            dlog_a = da * a - dmult * (a * a) / mult
            dr = dlog_a * ((-LRU_C) * sp)
            dsp = jnp.sum(dlog_a * ((-LRU_C) * r), axis=0, keepdims=True)
            dza = dr * r * (1.0 - r)
            dzx = dig * ig * (1.0 - ig)
            dzab = dza.astype(BF16)
            dzxb = dzx.astype(BF16)
            xcb = xc.astype(BF16)
            dwa_ref[b] += _dot_tn(xcb, dzab)
            dwx_ref[b] += _dot_tn(xcb, dzxb)
            dxc = dxc + _dot_nt(dzab, wa_ref[b]) + _dot_nt(dzxb, wx_ref[b])
            sums_ref[L_LAM:L_LAM + 1, cs] += dsp * (-jax.nn.sigmoid(-lam))
            sums_ref[L_BA:L_BA + 1, cs] += jnp.sum(dza, axis=0, keepdims=True)
            sums_ref[L_BX:L_BX + 1, cs] += jnp.sum(dzx, axis=0, keepdims=True)
            sums_ref[L_CB:L_CB + 1, cs] += jnp.sum(dxc, axis=0, keepdims=True)
            for kk in range(4):
                sums_ref[kk:kk + 1, cs] += jnp.sum(dxc * xbuf[pl.ds(5 + kk, t), cs], axis=0, keepdims=True)
            dxbuf[pl.ds(0, t), cs] = dxc
            dlx = (lv_ref[3:4, cs] * dxc + lv_ref[2:3, cs] * dxbuf[pl.ds(1, t), cs]
                   + lv_ref[1:2, cs] * dxbuf[pl.ds(2, t), cs] + lv_ref[0:1, cs] * dxbuf[pl.ds(3, t), cs])
            dlx_ref[:, cs] = dlx.astype(BF16)
        dxbuf[pl.ds(t, 8), :] = dxbuf[pl.ds(0, 8), :]

    rev = lambda i: nt - 1 - i
    return pl.pallas_call(
        body, name="lru_bwd", grid=(nt,),
        in_specs=[pl.BlockSpec((t, D), lambda i: (rev(i), 0)),
                  pl.BlockSpec((16, D), lambda i: (jnp.maximum(rev(i) * (t // 16) - 1, 0), 0)),
                  pl.BlockSpec((t, D), lambda i: (rev(i), 0)),
                  pl.BlockSpec((8, D), lambda i: (jnp.maximum(rev(i) * (t // 8) - 1, 0), 0)),
                  pl.BlockSpec((t, D), lambda i: (rev(i), 0)),
                  pl.BlockSpec((8, D), lambda i: (0, 0)),
                  pl.BlockSpec((N_LRU_BLOCKS, 128, 128), lambda i: (0, 0, 0)),
                  pl.BlockSpec((N_LRU_BLOCKS, 128, 128), lambda i: (0, 0, 0)),
                  pl.BlockSpec(memory_space=pl.ANY)],
        out_specs=[pl.BlockSpec((t, D), lambda i: (rev(i), 0)),
                   pl.BlockSpec((8, D), lambda i: (0, 0)),
                   pl.BlockSpec((N_LRU_BLOCKS, 128, 128), lambda i: (0, 0, 0)),
                   pl.BlockSpec((N_LRU_BLOCKS, 128, 128), lambda i: (0, 0, 0))],
        out_shape=[SDS((s, IN_W), BF16), SDS((8, D), F32), SDS((N_LRU_BLOCKS, 128, 128), F32),
                   SDS((N_LRU_BLOCKS, 128, 128), F32)],
        scratch_shapes=[pltpu.VMEM((t + 8, D), F32), pltpu.VMEM((t + 8, D), F32), pltpu.VMEM((t + 8, D), F32)]
        + [pltpu.VMEM((t, D), F32)] * 6 + [pltpu.VMEM((8, D), F32)],
        input_output_aliases={8: 0},
        compiler_params=_cp(1))(proj, proj, rec, rec, drec, lvec, wa, wx, dproj)


def inproj_bwd(dproj, w_in, x, dx1, vecs):
    s = x.shape[0]
    tm, tk = min(512, s), IN_TILE
    nk = IN_W // tk
    per = IN_SHARD // tk

    def body(d_ref, w_ref, x_hbm, dx1_hbm, v_ref, gx_ref, sums_ref, acc, x_ref, dx1_ref, sems):
        i, k = pl.program_id(0), pl.program_id(1)
        fetches = _row_fetches((x_hbm, dx1_hbm), (x_ref, dx1_ref), sems, i, tm)

        @pl.when((i == 0) & (k == 0))
        def _():
            sums_ref[...] = jnp.zeros_like(sums_ref)

        @pl.when(k == 0)
        def _():
            acc[...] = jnp.zeros_like(acc)
            for cp in fetches:
                cp.start()

        acc[...] += _dot_nt(d_ref[...], w_ref[...])

        @pl.when(k == nk - 1)
        def _():
            for cp in fetches:
                cp.wait()
            g1 = v_ref[V_G1:V_G1 + 1, :]
            scale1 = v_ref[V_SCALE1:V_SCALE1 + 1, :]

            def sub(rb, carry):
                rs = pl.ds(pl.multiple_of(rb * SUB, SUB), SUB)
                dh = acc[rs, :]
                r1, xh = _rms_parts(x_ref[rs, :])
                sums_ref[0:1, :] += jnp.sum(dh, axis=0, keepdims=True)
                sums_ref[1:2, :] += jnp.sum(dh * (xh * g1), axis=0, keepdims=True)
                dxn = dh * (1.0 + scale1)
                sums_ref[2:3, :] += jnp.sum(dxn * xh, axis=0, keepdims=True)
                dxh = dxn * g1
                gx_ref[rs, :] = dx1_ref[rs, :] + r1 * (dxh - xh * jnp.mean(dxh * xh, axis=-1, keepdims=True))
                return carry

            lax.fori_loop(0, tm // SUB, sub, 0)

    return pl.pallas_call(
        body, name="inproj_bwd", grid=(s // tm, nk),
        in_specs=[pl.BlockSpec((tm, tk), lambda i, k: (i, k)),
                  pl.BlockSpec((None, D, tk), lambda i, k: (k // per, 0, k % per)),
                  pl.BlockSpec(memory_space=pl.ANY), pl.BlockSpec(memory_space=pl.ANY),
                  pl.BlockSpec((16, D), lambda i, k: (0, 0))],
        out_specs=[pl.BlockSpec((tm, D), lambda i, k: (i, 0)), pl.BlockSpec((8, D), lambda i, k: (0, 0))],
        out_shape=[SDS((s, D), F32), SDS((8, D), F32)],
        scratch_shapes=[pltpu.VMEM((tm, D), F32), pltpu.VMEM((tm, D), F32), pltpu.VMEM((tm, D), F32),
                        pltpu.SemaphoreType.DMA((2,))],
        compiler_params=_cp(2))(dproj, w_in, x, dx1, vecs)


def mod_columns(c16, w_ada, b_cols):
    tn = 512

    def body(c_ref, w_ref, b_ref, o_ref):
        cv = c_ref[...]
        ca = (cv * jax.nn.sigmoid(cv)).astype(BF16)
        o_ref[...] = _dot(ca, w_ref[...].astype(BF16)) + b_ref[...]

    return pl.pallas_call(
        body, name="mod_columns", grid=(ADA_SHARD // tn,),
        in_specs=[pl.BlockSpec((16, D), lambda j: (0, 0)), pl.BlockSpec((D, tn), lambda j: (0, j)),
                  pl.BlockSpec((1, tn), lambda j: (0, j))],
        out_specs=pl.BlockSpec((16, tn), lambda j: (0, j)),
        out_shape=SDS((16, ADA_SHARD), F32), compiler_params=_cp(1))(c16, w_ada, b_cols)


def wada_update(c16, dmod16, w, m, v):
    tm, tn = 512, 512

    def body(c_ref, d_ref, w_ref, m_ref, v_ref, g_out, dl_out, m_out, v_out):
        cv = c_ref[...]
        ca = (cv * jax.nn.sigmoid(cv)).astype(BF16)
        g = _dot_tn(ca, d_ref[...].astype(BF16))
        dl, m2, v2 = _adamw_math(w_ref[...], g, m_ref[...], v_ref[...])
        g_out[...] = g
        dl_out[...] = dl
        m_out[...] = m2
        v_out[...] = v2

    tile = pl.BlockSpec((tm, tn), lambda i, j: (i, j))
    return pl.pallas_call(
        body, name="wada_update", grid=(D // tm, ADA_SHARD // tn),
        in_specs=[pl.BlockSpec((16, tm), lambda i, j: (0, i)), pl.BlockSpec((16, tn), lambda i, j: (0, j)),
                  tile, tile, tile],
        out_specs=[tile] * 4, out_shape=[SDS((D, ADA_SHARD), F32)] * 4,
        compiler_params=_cp(2))(c16, dmod16, w, m, v)


def adamw_big(name, w, mine, theirs, m, v, c_idx):
    r, c = w.shape
    tr = 128
    per = (r // 2) // tr

    def body(c_ref, w_ref, a_ref, b_ref, m_ref, v_ref, g_out, dl_out, m_out, v_out):
        own = (pl.program_id(0) // per) == c_ref[0]
        g = jnp.where(own, a_ref[...], b_ref[...])
        dl, m2, v2 = _adamw_math(w_ref[...], g, m_ref[...], v_ref[...])
        g_out[...] = g
        dl_out[...] = dl
        m_out[...] = m2
        v_out[...] = v2

    tile = pl.BlockSpec((tr, c), lambda i, cr: (i, 0))
    half = pl.BlockSpec((tr, c), lambda i, cr: (i % per, 0))
    gs = pltpu.PrefetchScalarGridSpec(num_scalar_prefetch=1, grid=(r // tr,),
                                      in_specs=[tile, half, half, tile, tile], out_specs=[tile] * 4)
    return pl.pallas_call(body, name=name, grid_spec=gs, out_shape=[SDS((r, c), F32)] * 4,
                          compiler_params=_cp(1))(c_idx, w, mine, theirs, m, v)


def cast_into_slot(name, w, k_idx):
    r, c = w.shape
    tr = 256

    def body(k_ref, w_ref, o_ref):
        o_ref[...] = w_ref[...].astype(BF16)

    gs = pltpu.PrefetchScalarGridSpec(
        num_scalar_prefetch=1, grid=(r // tr,),
        in_specs=[pl.BlockSpec((tr, c), lambda i, kr: (i, 0))],
        out_specs=pl.BlockSpec((None, tr, c), lambda i, kr: (kr[0], i, 0)))
    return pl.pallas_call(body, name=name, grid_spec=gs, out_shape=SDS((N_CHIPS, r, c), BF16),
                          compiler_params=_cp(1))(k_idx, w)


def adamw_small(ws, gs, ms, vs):
    n = len(ws)

    def body(*refs):
        for i in range(n):
            dl, m2, v2 = _adamw_math(refs[i][...], refs[n + i][...], refs[2 * n + i][...], refs[3 * n + i][...])
            refs[4 * n + i][...] = dl
            refs[5 * n + i][...] = m2
            refs[6 * n + i][...] = v2

    vm = pl.BlockSpec(memory_space=pltpu.VMEM)
    shapes = [SDS(w.shape, F32) for w in ws]
    outs = pl.pallas_call(
        body, name="adamw_small", in_specs=[vm] * (4 * n), out_specs=[vm] * (3 * n), out_shape=shapes * 3,
        compiler_params=pltpu.CompilerParams(vmem_limit_bytes=VMEM_LIMIT))(*ws, *gs, *ms, *vs)
    return outs[:n], outs[n:2 * n], outs[2 * n:]


def sum_devices(name, gathered):
    rows, cols = gathered.shape[1:]
    tr = min(rows, 128 * D // cols)

    def body(x_ref, o_ref):
        acc = x_ref[0].astype(F32)
        for d in range(1, N_DEV):
            acc = acc + x_ref[d].astype(F32)
        o_ref[...] = acc

    return pl.pallas_call(
        body, name=name, grid=(rows // tr,),
        in_specs=[pl.BlockSpec((N_DEV, tr, cols), lambda i: (0, i, 0))],
        out_specs=pl.BlockSpec((tr, cols), lambda i: (i, 0)),
        out_shape=SDS((rows, cols), F32), compiler_params=_cp(1))(gathered)


def _mesh_pos():
    return lax.axis_index("x"), lax.axis_index("y"), lax.axis_index("c")


def _other_chips(x, y):
    return [(1 - x, y), (x, 1 - y), (1 - x, 1 - y)]


def all_gather_small(name, block):
    m_per, n = block.shape

    def body(x_ref, out_ref, send_sems, recv_sems, local_sem):
        x, y, c = _mesh_pos()
        me, sibling = (x, y, c), (x, y, 1 - c)
        chips = _other_chips(x, y)

        def rows(px, py, pc):
            return out_ref.at[pl.ds((4 * px + 2 * py + pc) * m_per, m_per), :]

        def copy(k, blk, to, src=None):
            return pltpu.make_async_remote_copy(
                src_ref=rows(*blk) if src is None else src, dst_ref=rows(*blk),
                send_sem=send_sems.at[k], recv_sem=recv_sems.at[k], device_id=to, device_id_type=MESH)

        mine = pltpu.make_async_copy(x_ref, rows(*me), local_sem)
        mine.start()
        first = [copy(0, me, sibling, src=x_ref)]
        first += [copy(1 + j, me, (*chip, c), src=x_ref) for j, chip in enumerate(chips)]
        for cp in first:
            cp.start()
        passed = [copy(4 + j, (*chip, c), sibling) for j, chip in enumerate(chips)]
        for j, chip in enumerate(chips):
            copy(1 + j, (*chip, c), me).wait_recv()
            passed[j].start()
        copy(0, sibling, me).wait_recv()
        for j, chip in enumerate(chips):
            copy(4 + j, (*chip, 1 - c), me).wait_recv()
        for cp in first + passed:
            cp.wait_send()
        mine.wait()

    vm = pl.BlockSpec(memory_space=pltpu.VMEM)
    return pl.pallas_call(
        body, name=name, out_shape=SDS((N_DEV * m_per, n), block.dtype), in_specs=[vm], out_specs=vm,
        scratch_shapes=[pltpu.SemaphoreType.DMA((7,)), pltpu.SemaphoreType.DMA((7,)), pltpu.SemaphoreType.DMA],
        compiler_params=pltpu.CompilerParams(vmem_limit_bytes=VMEM_LIMIT))(block)


def sibling_sum(name, grad, other, c_idx):
    _, r, cc = grad.shape
    h = r // 2
    tr = min(256, h)
    g4 = grad.reshape(N_CHIPS, 2, h, cc)

    def body(c_ref, a_ref, b_ref, o_ref):
        o_ref[...] = (a_ref[...].astype(F32) + b_ref[...].astype(F32)).astype(BF16)

    gs = pltpu.PrefetchScalarGridSpec(
        num_scalar_prefetch=1, grid=(N_CHIPS, h // tr),
        in_specs=[pl.BlockSpec((None, None, tr, cc), lambda s, i, cr: (s, cr[0], i, 0)),
                  pl.BlockSpec((None, tr, cc), lambda s, i, cr: (s, i, 0))],
        out_specs=pl.BlockSpec((None, tr, cc), lambda s, i, cr: (s, i, 0)))
    return pl.pallas_call(body, name=name, grid_spec=gs, out_shape=SDS((N_CHIPS, h, cc), BF16),
                          compiler_params=_cp(2))(c_idx, g4, other)


HBM_SPEC = pl.BlockSpec(memory_space=pltpu.HBM)
SEM_SPEC = pl.BlockSpec(memory_space=pltpu.SEMAPHORE)


def _side_effecting():
    return pltpu.CompilerParams(has_side_effects=pltpu.SideEffectType.DATAFLOW_SIDE_EFFECTING)


def _in_hbm(a):
    return pltpu.with_memory_space_constraint(a, pltpu.HBM)


ALL_CHIPS = (0, 1, 2)


def gather_start(name, bufs, after, rel=ALL_CHIPS, carry=None):
    n = len(bufs)
    nr = len(rel)
    halves = [w.shape[1] // 2 for w in bufs]
    extra = [] if carry is None else [carry]

    def body(*refs):
        ins = refs[:n]
        send_sems, recv_sems, token = refs[n + 1 + len(extra)], refs[n + 2 + len(extra)], refs[-1]
        x, y, c = _mesh_pos()
        k = 2 * x + y
        for i in range(n):
            reg = ins[i].at[k, pl.ds(c * halves[i], halves[i]), :]
            for q, j in enumerate(rel):
                chip = _other_chips(x, y)[j]
                pltpu.make_async_remote_copy(src_ref=reg, dst_ref=reg, send_sem=send_sems.at[nr * i + q],
                                             recv_sem=recv_sems.at[nr * i + q], device_id=(*chip, c),
                                             device_id_type=MESH).start()
        token[...] = jnp.zeros_like(token)

    outs = pl.pallas_call(
        body, name=name,
        out_shape=(pltpu.SemaphoreType.DMA((nr * n,)), pltpu.SemaphoreType.DMA((nr * n,)),
                   *[pltpu.HBM(w.shape, w.dtype) for w in list(bufs) + extra], SDS((8, 128), F32)),
        in_specs=[HBM_SPEC] * n + [pl.BlockSpec(memory_space=pl.ANY)] + [HBM_SPEC] * len(extra),
        out_specs=(SEM_SPEC, SEM_SPEC, *[HBM_SPEC] * (n + len(extra)), pl.BlockSpec(memory_space=pltpu.VMEM)),
        input_output_aliases={**{i: 2 + i for i in range(n)}, **({n + 1: 2 + n} if extra else {})},
        compiler_params=_side_effecting())(*[_in_hbm(w) for w in bufs], after, *[_in_hbm(w) for w in extra])
    return (outs[0], outs[1], list(outs[2:2 + n]), outs[-1]) + ((outs[2 + n],) if extra else ())


def gather_wait(name, send_sems, recv_sems, bufs, after, rel=ALL_CHIPS):
    n = len(bufs)
    nr = len(rel)
    halves = [w.shape[1] // 2 for w in bufs]

    def body(*refs):
        ins = refs[:n]
        send_sems, recv_sems = refs[n], refs[n + 1]
        x, y, c = _mesh_pos()
        k = 2 * x + y
        for i in range(n):
            for q, j in enumerate(rel):
                chip = _other_chips(x, y)[j]
                kj = 2 * chip[0] + chip[1]
                cp = pltpu.make_async_remote_copy(
                    src_ref=ins[i].at[k, pl.ds(c * halves[i], halves[i]), :],
                    dst_ref=ins[i].at[kj, pl.ds(c * halves[i], halves[i]), :],
                    send_sem=send_sems.at[nr * i + q], recv_sem=recv_sems.at[nr * i + q], device_id=(*chip, c),
                    device_id_type=MESH)
                cp.wait_send()
                cp.wait_recv()

    return pl.pallas_call(
        body, name=name, out_shape=[pltpu.HBM(w.shape, w.dtype) for w in bufs],
        in_specs=[HBM_SPEC] * n + [SEM_SPEC, SEM_SPEC, pl.BlockSpec(memory_space=pl.ANY)],
        out_specs=[HBM_SPEC] * n, input_output_aliases={i: i for i in range(n)},
        compiler_params=_side_effecting())(*bufs, send_sems, recv_sems, after)


def gather_forward(name, bufs, rel=ALL_CHIPS):
    n = len(bufs)
    halves = [w.shape[1] // 2 for w in bufs]

    def body(*refs):
        outs = refs[n:2 * n]
        send_sems, recv_sems = refs[2 * n:]
        x, y, c = _mesh_pos()
        chips = _other_chips(x, y)

        def copy(i, j, half, to):
            kj = 2 * chips[j][0] + chips[j][1]
            reg = outs[i].at[kj, pl.ds(half * halves[i], halves[i]), :]
            return pltpu.make_async_remote_copy(src_ref=reg, dst_ref=reg, send_sem=send_sems.at[i, j],
                                                recv_sem=recv_sems.at[i, j], device_id=to, device_id_type=MESH)

        cps = [copy(i, j, c, (x, y, 1 - c)) for i in range(n) for j in rel]
        for cp in cps:
            cp.start()
        for i in range(n):
            for j in rel:
                copy(i, j, 1 - c, (x, y, c)).wait_recv()
        for cp in cps:
            cp.wait_send()

    hbm = pl.BlockSpec(memory_space=pl.ANY)
    return pl.pallas_call(
        body, name=name, in_specs=[hbm] * n, out_specs=[hbm] * n,
        out_shape=[SDS(w.shape, w.dtype) for w in bufs], input_output_aliases={i: i for i in range(n)},
        scratch_shapes=[pltpu.SemaphoreType.DMA((n, 3)), pltpu.SemaphoreType.DMA((n, 3))])(*bufs)


def _exchange_plan(kind, srcs, zones):
    x, y, c = _mesh_pos()
    plan = []
    for src, zone in zip(srcs, zones):
        if kind == "chips":
            for j, chip in enumerate(_other_chips(x, y)):
                plan.append((src.at[2 * chip[0] + chip[1]], zone.at[j], (*chip, c)))
        elif kind == "sibling":
            h = zone.shape[1]
            plan.append((src.at[:, pl.ds((1 - c) * h, h), :], zone, (x, y, 1 - c)))
        else:
            peers = [(x, y, 1 - c)] + [(*chip, cc) for chip in _other_chips(x, y) for cc in (c, 1 - c)]
            plan += [(src, zone.at[4 * x + 2 * y + c], peer) for peer in peers]
    return plan


_COPIES_PER_ARRAY = {"chips": 3, "sibling": 1, "all": N_DEV - 1}


def _landing_zones(kind, srcs):
    if kind == "chips":
        return [lax.empty((3,) + t.shape[1:], t.dtype) for t in srcs]
    if kind == "sibling":
        return [lax.empty((t.shape[0], t.shape[1] // 2, t.shape[2]), t.dtype) for t in srcs]
    return [jnp.broadcast_to(t, (N_DEV,) + t.shape) for t in srcs]


def exchange_start(name, kind, srcs, after):
    n = len(srcs)
    lands = _landing_zones(kind, srcs)
    n_copies = n * _COPIES_PER_ARRAY[kind]

    def body(*refs):
        send_sems, recv_sems, token = refs[2 * n + 1], refs[2 * n + 2], refs[-1]
        for q, (src, dst, dev) in enumerate(_exchange_plan(kind, refs[:n], refs[n:2 * n])):
            pltpu.make_async_remote_copy(src_ref=src, dst_ref=dst, send_sem=send_sems.at[q], recv_sem=recv_sems.at[q],
                                         device_id=dev, device_id_type=MESH).start()
        token[...] = jnp.zeros_like(token)

    outs = pl.pallas_call(
        body, name=name,
        out_shape=(pltpu.SemaphoreType.DMA((n_copies,)), pltpu.SemaphoreType.DMA((n_copies,)),
                   *[pltpu.HBM(t.shape, t.dtype) for t in srcs], *[pltpu.HBM(t.shape, t.dtype) for t in lands],
                   SDS((8, 128), F32)),
        in_specs=[HBM_SPEC] * (2 * n) + [pl.BlockSpec(memory_space=pl.ANY)],
        out_specs=(SEM_SPEC, SEM_SPEC, *[HBM_SPEC] * (2 * n), pl.BlockSpec(memory_space=pltpu.VMEM)),
        input_output_aliases={i: 2 + i for i in range(2 * n)},
        compiler_params=_side_effecting())(*[_in_hbm(t) for t in srcs], *[_in_hbm(t) for t in lands], after)
    return outs[0], outs[1], list(outs[2:2 + n]), list(outs[2 + n:2 + 2 * n]), outs[-1]


def exchange_wait(name, kind, send_sems, recv_sems, srcs, lands, after):
    n = len(srcs)

    def body(*refs):
        send_sems, recv_sems = refs[2 * n], refs[2 * n + 1]
        for q, (src, dst, dev) in enumerate(_exchange_plan(kind, refs[:n], refs[n:2 * n])):
            cp = pltpu.make_async_remote_copy(src_ref=src, dst_ref=dst, send_sem=send_sems.at[q],
                                              recv_sem=recv_sems.at[q], device_id=dev, device_id_type=MESH)
            cp.wait_send()
            cp.wait_recv()

    outs = pl.pallas_call(
        body, name=name, out_shape=[pltpu.HBM(t.shape, t.dtype) for t in srcs + lands],
        in_specs=[HBM_SPEC] * (2 * n) + [SEM_SPEC, SEM_SPEC, pl.BlockSpec(memory_space=pl.ANY)],
        out_specs=[HBM_SPEC] * (2 * n), input_output_aliases={i: i for i in range(2 * n)},
        compiler_params=_side_effecting())(*srcs, *lands, send_sems, recv_sems, after)
    return list(outs[:n]), list(outs[n:])


def chip_sum(name, sums, parts, k_idx):
    _, h, cc = parts.shape
    tr = min(256, h)

    def body(k_ref, own_ref, p_ref, o_ref):
        acc = own_ref[...].astype(F32)
        for s in range(3):
            acc = acc + p_ref[s].astype(F32)
        o_ref[...] = acc

    gs = pltpu.PrefetchScalarGridSpec(
        num_scalar_prefetch=1, grid=(h // tr,),
        in_specs=[pl.BlockSpec((None, tr, cc), lambda i, kr: (kr[0], i, 0)),
                  pl.BlockSpec((3, tr, cc), lambda i, kr: (0, i, 0))],
        out_specs=pl.BlockSpec((tr, cc), lambda i, kr: (i, 0)))
    return pl.pallas_call(body, name=name, grid_spec=gs, out_shape=SDS((h, cc), F32),
                          compiler_params=_cp(1))(k_idx, sums, parts)


def halves_exchange(name, halves):
    n = len(halves)

    def body(*refs):
        ins, outs = refs[:n], refs[n:2 * n]
        send_sems, recv_sems = refs[2 * n:]
        x, y, c = _mesh_pos()
        cps = []
        for i in range(n):
            cp = pltpu.make_async_remote_copy(
                src_ref=ins[i], dst_ref=outs[i], send_sem=send_sems.at[i], recv_sem=recv_sems.at[i],
                device_id=(x, y, 1 - c), device_id_type=MESH)
            cp.start()
            cps.append(cp)
        for cp in cps:
            cp.wait_recv()
        for cp in cps:
            cp.wait_send()

    hbm = pl.BlockSpec(memory_space=pl.ANY)
    return pl.pallas_call(
        body, name=name, in_specs=[hbm] * n, out_specs=[hbm] * n,
        out_shape=[SDS(t.shape, F32) for t in halves],
        scratch_shapes=[pltpu.SemaphoreType.DMA((n,)), pltpu.SemaphoreType.DMA((n,))])(*halves)


def local_step(x, tgt, vecs, lvec, wa, wx, sinks, rel_bias, proj, h, w_in, rest_weights, hook):
    buckets = t5_bucket_table()
    band = bias_band(rel_bias.T, buckets).reshape(N_HEADS, BLOCK, 2 * BLOCK)

    ya, rec, ya_t = lru_fwd(proj, lvec, wa, wx)
    att, att_t = attn_fwd(proj, band, sinks)
    w_lru_out, w_attn_out, w_out = rest_weights("mix", att[:8, :128] + ya[:8, :128])
    w_lru_out2, w_attn_out2, w_out2 = w_lru_out.reshape(D, D), w_attn_out.reshape(D, D), w_out.reshape(D, D)
    yab, merged, merged_t = merge_fwd(ya, att, w_lru_out2, w_attn_out2, proj)
    x1, o1 = outproj_fwd(merged, w_out2, x, vecs)
    w_ff1, w_ff2 = rest_weights("ff", o1[:8, :128])
    w_ff2_2 = w_ff2.reshape(D_FF, D)
    f, h2, fft = ff1_fwd(x1, vecs, w_ff1)
    dx2, do2, sums_f, loss = ff2_loss(f, w_ff2_2, x1, tgt, vecs)

    df = ff2_bwd(do2, w_ff2_2, f)
    g_ff2 = matmul_bf16("dw_ff2", fft, do2, WG_TM, 512)
    dx1, do1, sums_2 = ff1_bwd(df, w_ff1, x1, dx2, o1, vecs)
    g_ff1 = weight_grad("dw_ff1", h2, df, 512, (N_CHIPS, D, D), (None, WG_TM, 512), lambda i, j: (j // 4, i, j % 4))
    dyab, dproj = outproj_bwd(do1, w_out2, yab, proj)
    g_out = matmul_bf16("dw_out", merged_t, do1, WG_TM, 512)
    drec, dproj = lruout_bwd(dyab, w_lru_out2, rec, proj, dproj)
    dyab2 = dyab.reshape(2 * x.shape[0], D)
    g_lru_out = matmul_bf16("dw_lru_out", ya_t, dyab2, WG_TM, 512)
    datt = attnout_bwd(dyab, w_attn_out2)
    g_attn_out = matmul_bf16("dw_attn_out", att_t, dyab2, WG_TM, 512, b_part=1)
    zero = hook("grads_a", [g_lru_out.reshape(N_CHIPS, D // 4, D), g_attn_out.reshape(N_CHIPS, D // 4, D),
                            g_out.reshape(N_CHIPS, D // 4, D), g_ff1, g_ff2.reshape(N_CHIPS, D_FF // 4, D)])
    dproj, dkb, dvb, dband, dsink = attn_bwd(proj, band, sinks + zero, datt, dproj)
    zero = hook("after_attn_bwd", dkb)
    dproj = dkv_combine(dkb, dvb, dproj)
    dproj, sums_l, d_wa, d_wx = lru_bwd(proj, rec, drec, lvec + zero, wa, wx, dproj)
    hook("lru_grads", (d_wa, d_wx))
    per = IN_SHARD // IN_TILE
    g_in = weight_grad("dw_in", h, dproj, IN_TILE, (N_CHIPS, D, IN_SHARD), (None, WG_TM, IN_TILE),
                       lambda i, j: (j // per, i, j % per))
    zero = hook("grads_b", [g_in])
    grad_x, sums_1 = inproj_bwd(dproj, w_in, x, dx1, vecs + zero)
    d_rel_bias = bias_band_bwd(dband.reshape(N_HEADS, BLOCK * 2 * BLOCK), buckets)

    small = dict(sums_f=sums_f, sums_2=sums_2, sums_1=sums_1, sums_l=sums_l, d_wa=d_wa, d_wx=d_wx,
                 d_sinks=dsink[:, 0], d_rel_bias=d_rel_bias)
    return loss, grad_x, small


def _pad_rows(a, rows):
    return jnp.concatenate([a, jnp.zeros((rows - a.shape[0], a.shape[1]), a.dtype)], axis=0)


def kernel(x, c, w_ada, b_ada, norm1_g, w_in, conv_w, conv_b, lru_wa, lru_ba, lru_wx, lru_bx, lru_lambda, w_lru_out, w_attn_out, attn_sinks, rel_bias, w_out, norm2_g, w_ff1, w_ff2, final_g, loss_target, m_w_ada, m_b_ada, m_norm1_g, m_w_in, m_conv_w, m_conv_b, m_lru_wa, m_lru_ba, m_lru_wx, m_lru_bx, m_lru_lambda, m_w_lru_out, m_w_attn_out, m_attn_sinks, m_rel_bias, m_w_out, m_norm2_g, m_w_ff1, m_w_ff2, m_final_g, v_w_ada, v_b_ada, v_norm1_g, v_w_in, v_conv_w, v_conv_b, v_lru_wa, v_lru_ba, v_lru_wx, v_lru_bx, v_lru_lambda, v_w_lru_out, v_w_attn_out, v_attn_sinks, v_rel_bias, v_w_out, v_norm2_g, v_w_ff1, v_w_ff2, v_final_g):
    xi, yi, ci = _mesh_pos()
    chip = 2 * xi + yi
    dev = 2 * chip + ci
    z8 = jnp.zeros((8, D), F32)

    conv_rows = jnp.concatenate([conv_w[0], jnp.zeros((4, D - D // 4), F32)], axis=1)
    pack0 = jnp.concatenate([c, conv_rows, jnp.zeros((3, D), F32)], axis=0)
    g0 = all_gather_small("gather_cond", pack0).reshape(N_DEV, 8, D)
    c_all = g0[:, 0, :]
    conv_full = jnp.concatenate([g0[2 * k, 1:5, :D // 4] for k in range(N_CHIPS)], axis=1)
    c16 = jnp.concatenate([c_all, z8], axis=0)
    b_cols = lax.dynamic_slice_in_dim(b_ada, chip * ADA_SHARD, ADA_SHARD, axis=1)
    mod_c = mod_columns(c16, w_ada[0], b_cols)
    g1 = all_gather_small("gather_mod", mod_c).reshape(N_DEV, 16, ADA_SHARD)
    mod = jnp.concatenate([lax.dynamic_index_in_dim(g1[2 * k], dev, axis=0, keepdims=False) for k in range(N_CHIPS)])
    shift1, scale1, gate1, shift2, scale2, gate2 = [mod[i * D:(i + 1) * D] for i in range(6)]
    vecs = jnp.stack([norm1_g[0], scale1, shift1, gate1, norm2_g[0], scale2, shift2, gate2, final_g]
                     + [jnp.zeros((D,), F32)] * 7)
    lvec = jnp.concatenate([conv_full, conv_b, lru_ba, lru_bx, lru_lambda], axis=0)

    shards = [w_in[0], w_lru_out[0], w_attn_out[0], w_out[0], w_ff1[0], w_ff2[0]]
    names = ["w_in", "w_lru_out", "w_attn_out", "w_out", "w_ff1", "w_ff2"]
    k_idx = jnp.reshape(chip, (1,)).astype(jnp.int32)
    c_idx = jnp.reshape(ci, (1,)).astype(jnp.int32)
    near, far = (0, 1), (2,)
    shard_of = lambda flip: jnp.reshape(chip ^ flip, (1,)).astype(jnp.int32)
    x2d = x[0]
    n_send, n_recv, w_in_buf, _ = gather_start(
        "gather_start_in_near", [cast_into_slot("cast_w_in", shards[0], k_idx)], vecs, near)
    proj, h = inproj_fwd("inproj_fwd_own", x2d, None, vecs, w_in_buf[0], None, k_idx)
    slots = [cast_into_slot("cast_" + nm, w, k_idx) for nm, w in zip(names[1:], shards[1:])]
    w_in_buf = gather_forward("gather_forward_in_near", gather_wait(
        "gather_wait_in_near", n_send, n_recv, w_in_buf, proj[:8, :128] + slots[-1][0, :8, :128], near), near)
    f_send, f_recv, w_in_buf, _ = gather_start("gather_start_in_far", w_in_buf, proj[:8, :128], far)
    in_flight = {"mix": gather_start("gather_start_mix", slots[:3], proj[:8, :128])}
    in_flight["ff"] = gather_start("gather_start_ff", slots[3:], in_flight["mix"][3], carry=w_in_buf[0])
    w_in_buf = [in_flight["ff"][4]]
    proj = inproj_fwd("inproj_fwd_x", x2d, h, vecs, w_in_buf[0], proj, shard_of(2))
    proj = inproj_fwd("inproj_fwd_y", x2d, h, vecs, w_in_buf[0], proj, shard_of(1))
    w_in_buf = gather_forward("gather_forward_in_far", gather_wait(
        "gather_wait_in_far", f_send, f_recv, w_in_buf, proj[:8, :128], far), far)
    proj = inproj_fwd("inproj_fwd_d", x2d, h, vecs, w_in_buf[0], proj, shard_of(3))
    w_in_full = w_in_buf[0]
    pending = {}

    def rest_weights(group, after):
        send_sems, recv_sems, bufs = in_flight[group][:3]
        return gather_forward("gather_forward_" + group,
                              gather_wait("gather_wait_" + group, send_sems, recv_sems, bufs, after))

    def reduce_hook(event, payload):
        if event == "grads_a":
            pending["sib_a"] = exchange_start("sibling_start_a", "sibling", payload, payload[0])
            return pending["sib_a"][-1][0, 0]
        if event == "lru_grads":
            pack_w = jnp.concatenate([payload[0].reshape(D, 128), payload[1].reshape(D, 128)], axis=0).astype(BF16)
            pending["lru_w"] = exchange_start("lru_w_grads_start", "all", [pack_w], pack_w)
            return pending["lru_w"][-1][0, 0]
        if event == "grads_b":
            pending["sib_b"] = exchange_start("sibling_start_b", "sibling", payload, pending["lru_w"][-1])
            return pending["sib_b"][-1][0, 0]
        return chips_start("a", names[1:], payload)

    def chips_start(tag, nms, after):
        send_sems, recv_sems, grads, lands, _ = pending["sib_" + tag]
        grads, lands = exchange_wait("sibling_wait_" + tag, "sibling", send_sems, recv_sems, grads, lands, after)
        sums = [sibling_sum("sibling_sum_" + nm, g, o, c_idx) for nm, g, o in zip(nms, grads, lands)]
        pending[tag] = exchange_start("exchange_start_" + tag, "chips", sums, sums[0])
        return pending[tag][-1][0, 0]

    loss_t, grad_x, small = local_step(
        x2d, loss_target[0], vecs, lvec, lru_wa[0].astype(BF16), lru_wx[0].astype(BF16),
        attn_sinks[0], rel_bias, proj, h, w_in_full, rest_weights, reduce_hook)
    chips_start("b", names[:1], grad_x)

    big_m = dict(zip(names, [m_w_in, m_w_lru_out, m_w_attn_out, m_w_out, m_w_ff1, m_w_ff2]))
    big_v = dict(zip(names, [v_w_in, v_w_lru_out, v_w_attn_out, v_w_out, v_w_ff1, v_w_ff2]))
    local_w = dict(zip(names, shards))
    g_big, d_big, nm_big, nv_big = {}, {}, {}, {}

    def links_done(tag, after):
        send_sems, recv_sems, sums, lands, _ = pending[tag]
        return exchange_wait("exchange_wait_" + tag, "chips", send_sems, recv_sems, sums, lands, after)

    def finish_reduce(nms, sums, lands):
        mine = [chip_sum("chip_sum_" + nm, t, p, k_idx) for nm, t, p in zip(nms, sums, lands)]
        theirs = halves_exchange("halves_exchange_" + nms[0], mine)
        for nm, a, b in zip(nms, mine, theirs):
            g2, dl, m2, v2 = adamw_big("adamw_" + nm, local_w[nm], a, b, big_m[nm][0], big_v[nm][0], c_idx)
            g_big[nm], d_big[nm], nm_big[nm], nv_big[nm] = g2[None], dl[None], m2[None], v2[None]
        return lax.optimization_barrier(tuple(nv_big[nm] for nm in nms))[0]

    done_a = finish_reduce(names[1:], *links_done("a", pending["b"][-1]))
    sums_b, lands_b = links_done("b", done_a)
    w_send, w_recv, w_src, w_lands, _ = pending["lru_w"]
    w_src, w_lands = exchange_wait("lru_w_grads_wait", "all", w_send, w_recv, w_src, w_lands, lands_b[0])

    sums_f, sums_2, sums_1, sums_l = small["sums_f"], small["sums_2"], small["sums_1"], small["sums_l"]
    vec_rows = jnp.stack([sums_1[2], sums_2[2], sums_f[0], sums_l[L_CB], sums_l[L_BA], sums_l[L_BX],
                          sums_l[L_LAM], jnp.zeros((D,), F32)])
    mod_rows = jnp.stack([sums_1[0], sums_1[1], sums_2[3], sums_2[0], sums_2[1], sums_f[1],
                          jnp.zeros((D,), F32), jnp.zeros((D,), F32)])
    att_rows = jnp.concatenate([
        jnp.concatenate([small["d_sinks"], jnp.zeros((D - N_HEADS,), F32)])[None],
        jnp.concatenate([small["d_rel_bias"].reshape(-1), jnp.zeros((D - N_BUCKETS * N_HEADS,), F32)])[None],
        jnp.zeros((6, D), F32)], axis=0)
    pack = jnp.concatenate([vec_rows, _pad_rows(sums_l[0:4], 8), mod_rows, att_rows], axis=0)
    pack, lru_w_all = lax.optimization_barrier((pack, w_lands[0]))
    gathered = all_gather_small("gather_small_grads", pack).reshape(N_DEV, P_WA, D)
    total = sum_devices("sum_small_grads", gathered)
    total_w = sum_devices("sum_lru_w_grads", lru_w_all)
    dmod_all = gathered[:, P_MOD:P_MOD + 6, :].reshape(N_DEV, 6 * D)
    dmod16 = jnp.concatenate([lax.dynamic_slice_in_dim(dmod_all, chip * ADA_SHARD, ADA_SHARD, axis=1),
                              jnp.zeros((8, ADA_SHARD), F32)], axis=0)
    g_w_ada, d_w_ada, nm_w_ada, nv_w_ada = wada_update(c16, dmod16, w_ada[0], m_w_ada[0], v_w_ada[0])
    finish_reduce(names[:1], sums_b, lands_b)
    loss = lax.psum(lax.optimization_barrier((loss_t, total))[0][0, 0], ("x", "y", "c"))

    conv_g = lax.dynamic_slice_in_dim(total[P_CONVW:P_CONVW + 4], chip * (D // 4), D // 4, axis=1)
    sm_names = ["b_ada", "norm1_g", "conv_w", "conv_b", "lru_wa", "lru_ba", "lru_wx", "lru_bx", "lru_lambda",
                "attn_sinks", "rel_bias", "norm2_g", "final_g"]
    sm_w = [b_ada.reshape(6, D), norm1_g, conv_w[0], conv_b, lru_wa.reshape(D, 128), lru_ba, lru_wx.reshape(D, 128),
            lru_bx, lru_lambda, attn_sinks, rel_bias, norm2_g, final_g[None]]
    sm_m = [m_b_ada.reshape(6, D), m_norm1_g, m_conv_w[0], m_conv_b, m_lru_wa.reshape(D, 128), m_lru_ba,
            m_lru_wx.reshape(D, 128), m_lru_bx, m_lru_lambda, m_attn_sinks, m_rel_bias, m_norm2_g, m_final_g[None]]
    sm_v = [v_b_ada.reshape(6, D), v_norm1_g, v_conv_w[0], v_conv_b, v_lru_wa.reshape(D, 128), v_lru_ba,
            v_lru_wx.reshape(D, 128), v_lru_bx, v_lru_lambda, v_attn_sinks, v_rel_bias, v_norm2_g, v_final_g[None]]
    sm_g = [total[P_MOD:P_MOD + 6], total[0:1], conv_g, total[3:4], total_w[0:D], total[4:5],
            total_w[D:2 * D], total[5:6], total[6:7], total[P_ATT:P_ATT + 1, :N_HEADS],
            total[P_ATT + 1, :N_BUCKETS * N_HEADS].reshape(N_BUCKETS, N_HEADS), total[1:2], total[2:3]]
    sm_d, sm_nm, sm_nv = adamw_small(sm_w, sm_g, sm_m, sm_v)
    shapes = dict(b_ada=b_ada.shape, norm1_g=norm1_g.shape, conv_w=conv_w.shape, conv_b=conv_b.shape,
                  lru_wa=lru_wa.shape, lru_ba=lru_ba.shape, lru_wx=lru_wx.shape, lru_bx=lru_bx.shape,
                  lru_lambda=lru_lambda.shape, attn_sinks=attn_sinks.shape, rel_bias=rel_bias.shape,
                  norm2_g=norm2_g.shape, final_g=final_g.shape)
    grads = dict(w_ada=g_w_ada[None], **g_big)
    deltas = dict(w_ada=d_w_ada[None], **d_big)
    new_m = dict(w_ada=nm_w_ada[None], **nm_big)
    new_v = dict(w_ada=nv_w_ada[None], **nv_big)
    for i, nm in enumerate(sm_names):
        grads[nm] = sm_g[i].reshape(shapes[nm])
        deltas[nm] = sm_d[i].reshape(shapes[nm])
        new_m[nm] = sm_nm[i].reshape(shapes[nm])
        new_v[nm] = sm_nv[i].reshape(shapes[nm])
    order = ["w_ada", "b_ada", "norm1_g", "w_in", "conv_w", "conv_b", "lru_wa", "lru_ba", "lru_wx", "lru_bx",
             "lru_lambda", "w_lru_out", "w_attn_out", "attn_sinks", "rel_bias", "w_out", "norm2_g", "w_ff1", "w_ff2",
             "final_g"]
    return (loss, grad_x[None], *[grads[n] for n in order], *[deltas[n] for n in order],
            *[new_m[n] for n in order], *[new_v[n] for n in order])
```
